```python
import jax, jax.numpy as jnp
from jax import lax
import numpy as np

D_MODEL = 2048
BATCH = 8
SEQ = 8192
DEPTH = 1

CHUNK = 64
EPS = 1e-6

RET_HEADS = 8
RET_DK = 256
RET_DV = 256
RET_QK_WIDTH = RET_HEADS * RET_DK
RET_WIDTH = RET_HEADS * RET_DV
ROPE_THETA = 10000.0

SSD_EXPAND = 2
SSD_WIDTH = SSD_EXPAND * D_MODEL
SSD_HEADDIM = 64
SSD_HEADS = SSD_WIDTH // SSD_HEADDIM
SSD_GROUPS = 8
SSD_HPG = SSD_HEADS // SSD_GROUPS
SSD_STATE = 128
SSD_CONV = 4
SSD_CONV_DIM = SSD_WIDTH + 2 * SSD_GROUPS * SSD_STATE
DT_MIN = 0.001
DT_MAX = 0.1

SPLITS = (RET_QK_WIDTH, RET_QK_WIDTH, RET_WIDTH, RET_WIDTH,
          SSD_WIDTH, SSD_CONV_DIM, SSD_HEADS, D_MODEL, D_MODEL)
IN_PROJ_DIM = sum(SPLITS)

kernel_name = "retention_ssd_gated_hybrid"

f32 = jnp.float32


def rmsnorm(x, w):
    xf = x.astype(f32)
    y = xf * lax.rsqrt(jnp.mean(xf * xf, axis=-1, keepdims=True) + EPS)
    return (y * w.astype(f32)).astype(x.dtype)


def to_chunks(t):
    b, s = t.shape[0], t.shape[1]
    return jnp.moveaxis(t.reshape((b, s // CHUNK, CHUNK) + t.shape[2:]), 1, 0)


def from_chunks(t):
    t = jnp.moveaxis(t, 0, 1)
    return t.reshape((t.shape[0], t.shape[1] * t.shape[2]) + t.shape[3:])


def rope(t, positions):
    half = t.shape[-1] // 2
    inv_freq = ROPE_THETA ** (-jnp.arange(half, dtype=f32) / half)
    ang = positions.astype(f32)[..., None] * inv_freq
    cos, sin = jnp.cos(ang)[:, :, None, :], jnp.sin(ang)[:, :, None, :]
    t1, t2 = t[..., :half], t[..., half:]
    return jnp.concatenate([t1 * cos - t2 * sin, t2 * cos + t1 * sin], axis=-1)


def retention(q, k, v, positions):
    b, s, _ = q.shape
    q = rope(q.reshape(b, s, RET_HEADS, RET_DK).astype(f32), positions)
    k = rope(k.reshape(b, s, RET_HEADS, RET_DK).astype(f32), positions) * (RET_DK ** -0.5)
    v = v.reshape(b, s, RET_HEADS, RET_DV).astype(f32)

    log_gamma = jnp.log1p(-(2.0 ** (-5.0 - jnp.arange(RET_HEADS, dtype=f32))))
    idx = jnp.arange(CHUNK, dtype=f32)
    intra = jnp.exp(jnp.abs(idx[:, None] - idx[None, :]) * log_gamma[:, None, None])
    q_decay = jnp.exp((idx[:, None] + 1.0) * log_gamma[None, :])[None, :, :, None]
    k_decay = jnp.exp((CHUNK - 1.0 - idx[:, None]) * log_gamma[None, :])[None, :, :, None]
    chunk_decay = jnp.exp(CHUNK * log_gamma)[None, :, None, None]

    def step(state, inp):
        qc, kc, vc = inp
        scores = jnp.einsum('blhd,bshd->bhls', qc, kc) * intra
        y = jnp.einsum('bhls,bshv->blhv', scores, vc)
        y = y + jnp.einsum('blhd,bhdv->blhv', qc, state) * q_decay
        state = state * chunk_decay + jnp.einsum('bshd,bshv->bhdv', kc * k_decay, vc)
        return state, y

    state0 = jnp.zeros((b, RET_HEADS, RET_DK, RET_DV), f32)
    _, y = lax.scan(step, state0, (to_chunks(q), to_chunks(k), to_chunks(v)))
    y = from_chunks(y)
    mu = jnp.mean(y, axis=-1, keepdims=True)
    var = jnp.mean(jnp.square(y - mu), axis=-1, keepdims=True)
    y = (y - mu) * lax.rsqrt(var + EPS)
    return y.reshape(b, s, RET_WIDTH)


def ssd(xbc, dt_raw, conv_w, conv_b, dt_bias, a_log, d_skip):
    b, s, _ = xbc.shape
    xbc = lax.conv_general_dilated(
        xbc.astype(f32), conv_w.astype(f32)[:, None, :], window_strides=(1,),
        padding=[(SSD_CONV - 1, 0)], dimension_numbers=('NWC', 'WIO', 'NWC'),
        feature_group_count=SSD_CONV_DIM) + conv_b.astype(f32)
    xbc = jax.nn.silu(xbc)
    gn = SSD_GROUPS * SSD_STATE
    xs = xbc[..., :SSD_WIDTH].reshape(b, s, SSD_GROUPS, SSD_HPG, SSD_HEADDIM)
    bm = xbc[..., SSD_WIDTH:SSD_WIDTH + gn].reshape(b, s, SSD_GROUPS, SSD_STATE)
    cm = xbc[..., SSD_WIDTH + gn:].reshape(b, s, SSD_GROUPS, SSD_STATE)
    dt = jax.nn.softplus(dt_raw.astype(f32) + dt_bias.astype(f32)).reshape(b, s, SSD_GROUPS, SSD_HPG)
    a = dt * (-jnp.exp(a_log.astype(f32))).reshape(SSD_GROUPS, SSD_HPG)
    xdt = xs * dt[..., None]
    causal = jnp.tril(jnp.ones((CHUNK, CHUNK), dtype=bool))[None, :, :, None, None]

    def step(state, inp):
        xc, bc, cc, ac = inp
        acum = jnp.cumsum(ac, axis=1)
        seg = acum[:, :, None] - acum[:, None, :]
        decay = jnp.exp(jnp.where(causal, seg, -jnp.inf))
        cb = jnp.einsum('blgn,bsgn->blsg', cc, bc)
        y = jnp.einsum('blsg,blsgh,bsghp->blghp', cb, decay, xc)
        y = y + jnp.einsum('blgn,bghpn->blghp', cc, state) * jnp.exp(acum)[..., None]
        tail = jnp.exp(acum[:, -1:] - acum)
        state = (state * jnp.exp(acum[:, -1])[..., None, None]
                 + jnp.einsum('bsgn,bsgh,bsghp->bghpn', bc, tail, xc))
        return state, y

    state0 = jnp.zeros((b, SSD_GROUPS, SSD_HPG, SSD_HEADDIM, SSD_STATE), f32)
    _, y = lax.scan(step, state0, (to_chunks(xdt), to_chunks(bm), to_chunks(cm), to_chunks(a)))
    y = from_chunks(y) + d_skip.astype(f32).reshape(SSD_GROUPS, SSD_HPG)[..., None] * xs
    return y.reshape(b, s, SSD_WIDTH)


def _fwd_setup_inputs(seed: int = 0) -> dict:
    key = jax.random.key(seed)
    ks = jax.random.split(key, 16)
    x = jax.random.normal(ks[0], (BATCH, SEQ, D_MODEL), f32)
    offset = jax.random.randint(ks[1], (BATCH, 1), 0, 100000, dtype=jnp.int32)
    positions = offset + jnp.arange(SEQ, dtype=jnp.int32)[None, :]
    norm1_w = 1.0 + 0.02 * jax.random.normal(ks[2], (DEPTH, D_MODEL), f32)
    w_in = jax.random.normal(ks[3], (DEPTH, D_MODEL, IN_PROJ_DIM), f32) * D_MODEL ** -0.5
    conv_w = jax.random.normal(ks[4], (DEPTH, SSD_CONV, SSD_CONV_DIM), f32) * SSD_CONV ** -0.5
    conv_b = 0.01 * jax.random.normal(ks[5], (DEPTH, SSD_CONV_DIM), f32)
    u = jax.random.uniform(ks[6], (DEPTH, SSD_HEADS), f32)
    dt0 = jnp.exp(u * (np.log(DT_MAX) - np.log(DT_MIN)) + np.log(DT_MIN))
    dt_bias = dt0 + jnp.log(-jnp.expm1(-dt0))
    a_log = jnp.log(jax.random.uniform(ks[7], (DEPTH, SSD_HEADS), f32, minval=1.0, maxval=16.0))
    d_skip = 1.0 + 0.1 * jax.random.normal(ks[8], (DEPTH, SSD_HEADS), f32)
    ssd_norm_w = 1.0 + 0.02 * jax.random.normal(ks[9], (DEPTH, SSD_WIDTH), f32)
    w_br_ret = jax.random.normal(ks[10], (DEPTH, RET_WIDTH, D_MODEL), f32) * RET_WIDTH ** -0.5
    w_br_ssd = jax.random.normal(ks[11], (DEPTH, SSD_WIDTH, D_MODEL), f32) * SSD_WIDTH ** -0.5
    w_out = jax.random.normal(ks[12], (DEPTH, D_MODEL, D_MODEL), f32) * D_MODEL ** -0.5
    norm_f_w = 1.0 + 0.02 * jax.random.normal(ks[13], (D_MODEL,), f32)
    return {"x": x, "positions": positions, "norm1_w": norm1_w, "w_in": w_in,
            "conv_w": conv_w, "conv_b": conv_b, "dt_bias": dt_bias, "a_log": a_log,
            "d_skip": d_skip, "ssd_norm_w": ssd_norm_w, "w_br_ret": w_br_ret,
            "w_br_ssd": w_br_ssd, "w_out": w_out, "norm_f_w": norm_f_w}


def _fwd_reference(x, positions, norm1_w, w_in, conv_w, conv_b, dt_bias, a_log, d_skip,
              ssd_norm_w, w_br_ret, w_br_ssd, w_out, norm_f_w):
    offsets = [int(o) for o in np.cumsum(SPLITS)[:-1]]
    for l in range(DEPTH):
        h = rmsnorm(x, norm1_w[l])
        proj = jnp.einsum('bsd,de->bse', h, w_in[l])
        q, k, v, g_ret, z, xbc, dt_raw, gate_r, gate_s = jnp.split(proj, offsets, axis=-1)
        y_r = (retention(q, k, v, positions) * jax.nn.silu(g_ret.astype(f32))).astype(x.dtype)
        y_s = ssd(xbc, dt_raw, conv_w[l], conv_b[l], dt_bias[l], a_log[l], d_skip[l])
        y_s = rmsnorm((y_s * jax.nn.silu(z.astype(f32))).astype(x.dtype), ssd_norm_w[l])
        p_r = jnp.einsum('bse,ed->bsd', y_r, w_br_ret[l])
        p_s = jnp.einsum('bse,ed->bsd', y_s, w_br_ssd[l])
        merged = jax.nn.sigmoid(gate_r) * p_r + jax.nn.sigmoid(gate_s) * p_s
        x = x + jnp.einsum('bsd,de->bse', merged, w_out[l])
    return rmsnorm(x, norm_f_w)


import jax as _jax
import jax.numpy as _jnp

TWIN_FORMAT = 'train_step'
FWD_PARAMS = ['x', 'positions', 'norm1_w', 'w_in', 'conv_w', 'conv_b', 'dt_bias', 'a_log', 'd_skip', 'ssd_norm_w', 'w_br_ret', 'w_br_ssd', 'w_out', 'norm_f_w']
TWIN_WEIGHTS = ['norm1_w', 'w_in', 'conv_w', 'conv_b', 'dt_bias', 'a_log', 'd_skip', 'ssd_norm_w', 'w_br_ret', 'w_br_ssd', 'w_out', 'norm_f_w']
TWIN_DIFF_INPUT = 'x'
TWIN_INPUTS = ['x', 'positions', 'norm1_w', 'w_in', 'conv_w', 'conv_b', 'dt_bias', 'a_log', 'd_skip', 'ssd_norm_w', 'w_br_ret', 'w_br_ssd', 'w_out', 'norm_f_w', 'loss_target', 'm_norm1_w', 'm_w_in', 'm_conv_w', 'm_conv_b', 'm_dt_bias', 'm_a_log', 'm_d_skip', 'm_ssd_norm_w', 'm_w_br_ret', 'm_w_br_ssd', 'm_w_out', 'm_norm_f_w', 'v_norm1_w', 'v_w_in', 'v_conv_w', 'v_conv_b', 'v_dt_bias', 'v_a_log', 'v_d_skip', 'v_ssd_norm_w', 'v_w_br_ret', 'v_w_br_ssd', 'v_w_out', 'v_norm_f_w']
TWIN_OUTPUTS = ['loss', 'grad_x', 'grad_norm1_w', 'grad_w_in', 'grad_conv_w', 'grad_conv_b', 'grad_dt_bias', 'grad_a_log', 'grad_d_skip', 'grad_ssd_norm_w', 'grad_w_br_ret', 'grad_w_br_ssd', 'grad_w_out', 'grad_norm_f_w', 'delta_norm1_w', 'delta_w_in', 'delta_conv_w', 'delta_conv_b', 'delta_dt_bias', 'delta_a_log', 'delta_d_skip', 'delta_ssd_norm_w', 'delta_w_br_ret', 'delta_w_br_ssd', 'delta_w_out', 'delta_norm_f_w', 'new_m_norm1_w', 'new_m_w_in', 'new_m_conv_w', 'new_m_conv_b', 'new_m_dt_bias', 'new_m_a_log', 'new_m_d_skip', 'new_m_ssd_norm_w', 'new_m_w_br_ret', 'new_m_w_br_ssd', 'new_m_w_out', 'new_m_norm_f_w', 'new_v_norm1_w', 'new_v_w_in', 'new_v_conv_w', 'new_v_conv_b', 'new_v_dt_bias', 'new_v_a_log', 'new_v_d_skip', 'new_v_ssd_norm_w', 'new_v_w_br_ret', 'new_v_w_br_ssd', 'new_v_w_out', 'new_v_norm_f_w']
TWIN_LEAF_KINDS = {'loss': 'loss', 'grad_x': 'grad_x', 'grad_norm1_w': 'grad_w', 'grad_w_in': 'grad_w', 'grad_conv_w': 'grad_w', 'grad_conv_b': 'grad_w', 'grad_dt_bias': 'grad_w', 'grad_a_log': 'grad_w', 'grad_d_skip': 'grad_w', 'grad_ssd_norm_w': 'grad_w', 'grad_w_br_ret': 'grad_w', 'grad_w_br_ssd': 'grad_w', 'grad_w_out': 'grad_w', 'grad_norm_f_w': 'grad_w', 'delta_norm1_w': 'delta_w', 'delta_w_in': 'delta_w', 'delta_conv_w': 'delta_w', 'delta_conv_b': 'delta_w', 'delta_dt_bias': 'delta_w', 'delta_a_log': 'delta_w', 'delta_d_skip': 'delta_w', 'delta_ssd_norm_w': 'delta_w', 'delta_w_br_ret': 'delta_w', 'delta_w_br_ssd': 'delta_w', 'delta_w_out': 'delta_w', 'delta_norm_f_w': 'delta_w', 'new_m_norm1_w': 'new_m', 'new_m_w_in': 'new_m', 'new_m_conv_w': 'new_m', 'new_m_conv_b': 'new_m', 'new_m_dt_bias': 'new_m', 'new_m_a_log': 'new_m', 'new_m_d_skip': 'new_m', 'new_m_ssd_norm_w': 'new_m', 'new_m_w_br_ret': 'new_m', 'new_m_w_br_ssd': 'new_m', 'new_m_w_out': 'new_m', 'new_m_norm_f_w': 'new_m', 'new_v_norm1_w': 'new_v', 'new_v_w_in': 'new_v', 'new_v_conv_w': 'new_v', 'new_v_conv_b': 'new_v', 'new_v_dt_bias': 'new_v', 'new_v_a_log': 'new_v', 'new_v_d_skip': 'new_v', 'new_v_ssd_norm_w': 'new_v', 'new_v_w_br_ret': 'new_v', 'new_v_w_br_ssd': 'new_v', 'new_v_w_out': 'new_v', 'new_v_norm_f_w': 'new_v'}


def _forward(args):
    return _fwd_reference(*[args[k] for k in FWD_PARAMS])


def _output_shape():
    def fwd():
        inp = _fwd_setup_inputs(0)
        return _fwd_reference(*[inp[k] for k in FWD_PARAMS])
    out = _jax.eval_shape(fwd)
    return out.shape, out.dtype

N_MICROBATCH = 1
ADAM_LR = 0.001
ADAM_B1 = 0.9
ADAM_B2 = 0.999
ADAM_EPS = 1e-08
ADAM_WD = 0.01
ADAM_STEP = 10
PER_EXAMPLE_BATCH_AXIS = {'x': 0, 'positions': 0, 'loss_target': 0}
SHARED_INPUTS = []
_WEIGHT_DTYPES = {'norm1_w': _jnp.float32, 'w_in': _jnp.float32, 'conv_w': _jnp.float32, 'conv_b': _jnp.float32, 'dt_bias': _jnp.float32, 'a_log': _jnp.float32, 'd_skip': _jnp.float32, 'ssd_norm_w': _jnp.float32, 'w_br_ret': _jnp.float32, 'w_br_ssd': _jnp.float32, 'w_out': _jnp.float32, 'norm_f_w': _jnp.float32}
MOMENT_SCALE = {'norm1_w': 1.142111e-01, 'w_in': 3.467376e-02, 'conv_w': 3.568331e-02, 'conv_b': 4.889146e-02, 'dt_bias': 1.050616e-01, 'a_log': 1.206475e-01, 'd_skip': 2.196655e-01, 'ssd_norm_w': 4.026348e-02, 'w_br_ret': 3.426528e-02, 'w_br_ssd': 5.780387e-02, 'w_out': 6.721335e-02, 'norm_f_w': 3.198289e+01}


def _to_microbatches(a, axis):
    t = _jnp.moveaxis(a, axis, 0)
    t = t.reshape((N_MICROBATCH, t.shape[0] // N_MICROBATCH) + t.shape[1:])
    return _jnp.moveaxis(t, 1, axis + 1)


def setup_inputs(seed: int = 0) -> dict:
    inp = _fwd_setup_inputs(seed)
    key = _jax.random.fold_in(_jax.random.key(seed), 7919)
    shape, _ = _output_shape()
    out = dict(inp)
    out["loss_target"] = _jax.random.normal(_jax.random.fold_in(key, 0), shape, _jnp.float32)
    for i, name in enumerate(TWIN_WEIGHTS):
        w = inp[name].astype(_jnp.float32)
        if MOMENT_SCALE is None:
            s = _jnp.sqrt(_jnp.mean(_jnp.square(w)) + 1e-30)
        else:
            s = MOMENT_SCALE[name]
        km, kv = _jax.random.split(_jax.random.fold_in(key, i + 1))
        out[name] = w
        out["m_" + name] = s * _jax.random.normal(km, w.shape, _jnp.float32)
        out["v_" + name] = (s * s) * _jax.random.uniform(kv, w.shape, _jnp.float32, 0.5, 1.5)
    if N_MICROBATCH > 1:
        for name, axis in PER_EXAMPLE_BATCH_AXIS.items():
            out[name] = _to_microbatches(out[name], axis)
    return {'x': out['x'], 'positions': out['positions'], 'norm1_w': out['norm1_w'], 'w_in': out['w_in'], 'conv_w': out['conv_w'], 'conv_b': out['conv_b'], 'dt_bias': out['dt_bias'], 'a_log': out['a_log'], 'd_skip': out['d_skip'], 'ssd_norm_w': out['ssd_norm_w'], 'w_br_ret': out['w_br_ret'], 'w_br_ssd': out['w_br_ssd'], 'w_out': out['w_out'], 'norm_f_w': out['norm_f_w'], 'loss_target': out['loss_target'], 'm_norm1_w': out['m_norm1_w'], 'm_w_in': out['m_w_in'], 'm_conv_w': out['m_conv_w'], 'm_conv_b': out['m_conv_b'], 'm_dt_bias': out['m_dt_bias'], 'm_a_log': out['m_a_log'], 'm_d_skip': out['m_d_skip'], 'm_ssd_norm_w': out['m_ssd_norm_w'], 'm_w_br_ret': out['m_w_br_ret'], 'm_w_br_ssd': out['m_w_br_ssd'], 'm_w_out': out['m_w_out'], 'm_norm_f_w': out['m_norm_f_w'], 'v_norm1_w': out['v_norm1_w'], 'v_w_in': out['v_w_in'], 'v_conv_w': out['v_conv_w'], 'v_conv_b': out['v_conv_b'], 'v_dt_bias': out['v_dt_bias'], 'v_a_log': out['v_a_log'], 'v_d_skip': out['v_d_skip'], 'v_ssd_norm_w': out['v_ssd_norm_w'], 'v_w_br_ret': out['v_w_br_ret'], 'v_w_br_ssd': out['v_w_br_ssd'], 'v_w_out': out['v_w_out'], 'v_norm_f_w': out['v_norm_f_w']}


def _loss(weights, diff, rest, loss_target):
    with _jax.named_scope("forward"):
        args = {**rest, TWIN_DIFF_INPUT: diff, **{k: w.astype(_WEIGHT_DTYPES[k]) for k, w in weights.items()}}
        y = _forward(args)
    with _jax.named_scope("loss_head"):
        err = _jnp.square(y.astype(_jnp.float32) - loss_target)
        return 0.5 * _jnp.sum(_jnp.mean(err, axis=-1)) if err.ndim else 0.5 * err


def _adamw(w, g, m, v):
    m = ADAM_B1 * m + (1.0 - ADAM_B1) * g
    v = ADAM_B2 * v + (1.0 - ADAM_B2) * _jnp.square(g)
    m_hat = m / (1.0 - ADAM_B1 ** ADAM_STEP)
    v_hat = v / (1.0 - ADAM_B2 ** ADAM_STEP)
    delta = -ADAM_LR * (m_hat / (_jnp.sqrt(v_hat) + ADAM_EPS) + ADAM_WD * w)
    return delta, m, v


def reference(x, positions, norm1_w, w_in, conv_w, conv_b, dt_bias, a_log, d_skip, ssd_norm_w, w_br_ret, w_br_ssd, w_out, norm_f_w, loss_target, m_norm1_w, m_w_in, m_conv_w, m_conv_b, m_dt_bias, m_a_log, m_d_skip, m_ssd_norm_w, m_w_br_ret, m_w_br_ssd, m_w_out, m_norm_f_w, v_norm1_w, v_w_in, v_conv_w, v_conv_b, v_dt_bias, v_a_log, v_d_skip, v_ssd_norm_w, v_w_br_ret, v_w_br_ssd, v_w_out, v_norm_f_w):
    given = dict(x=x, positions=positions, norm1_w=norm1_w, w_in=w_in, conv_w=conv_w, conv_b=conv_b, dt_bias=dt_bias, a_log=a_log, d_skip=d_skip, ssd_norm_w=ssd_norm_w, w_br_ret=w_br_ret, w_br_ssd=w_br_ssd, w_out=w_out, norm_f_w=norm_f_w, loss_target=loss_target, m_norm1_w=m_norm1_w, m_w_in=m_w_in, m_conv_w=m_conv_w, m_conv_b=m_conv_b, m_dt_bias=m_dt_bias, m_a_log=m_a_log, m_d_skip=m_d_skip, m_ssd_norm_w=m_ssd_norm_w, m_w_br_ret=m_w_br_ret, m_w_br_ssd=m_w_br_ssd, m_w_out=m_w_out, m_norm_f_w=m_norm_f_w, v_norm1_w=v_norm1_w, v_w_in=v_w_in, v_conv_w=v_conv_w, v_conv_b=v_conv_b, v_dt_bias=v_dt_bias, v_a_log=v_a_log, v_d_skip=v_d_skip, v_ssd_norm_w=v_ssd_norm_w, v_w_br_ret=v_w_br_ret, v_w_br_ssd=v_w_br_ssd, v_w_out=v_w_out, v_norm_f_w=v_norm_f_w)
    weights = {n: given[n] for n in TWIN_WEIGHTS}
    shared = {n: given[n] for n in SHARED_INPUTS}
    per_example = {n: given[n] for n in ['x', 'positions']}
    grad_fn = _jax.value_and_grad(_loss, argnums=(0, 1))

    def one_microbatch(ex, loss_target):
        ex = dict(ex)
        diff = ex.pop(TWIN_DIFF_INPUT)
        return grad_fn(weights, diff, {**shared, **ex}, loss_target)

    if N_MICROBATCH == 1:
        loss, (grad_w, grad_x) = one_microbatch(per_example, given["loss_target"])
    else:
        def body(carry, xs):
            loss_sum, grad_sum = carry
            l_k, (gw_k, gx_k) = one_microbatch(xs[0], xs[1])
            with _jax.named_scope("update"):
                return (loss_sum + l_k, _jax.tree.map(_jnp.add, grad_sum, gw_k)), gx_k

        init = (_jnp.zeros((), _jnp.float32), _jax.tree.map(_jnp.zeros_like, weights))
        (loss, grad_w), grad_x = _jax.lax.scan(body, init, (per_example, given["loss_target"]))
    with _jax.named_scope("update"):
        delta_w, new_m, new_v = {}, {}, {}
        for n in TWIN_WEIGHTS:
            delta_w[n], new_m[n], new_v[n] = _adamw(weights[n], grad_w[n], given["m_" + n], given["v_" + n])
    return (loss, grad_x, *[grad_w[n] for n in TWIN_WEIGHTS], *[delta_w[n] for n in TWIN_WEIGHTS],
            *[new_m[n] for n in TWIN_WEIGHTS], *[new_v[n] for n in TWIN_WEIGHTS])
```

```python
import functools

import jax
import jax.numpy as jnp
import numpy as np
from jax import lax
from jax.experimental import pallas as pl
from jax.experimental.pallas import tpu as pltpu

f32 = jnp.float32
bf16 = jnp.bfloat16
HIGHEST = lax.Precision.HIGHEST
MESH = pl.DeviceIdType.MESH

D_MODEL = 2048
EPS = 1e-6
CHUNK = 64
RET_HEADS = 8
RET_DK = 256
ROPE_THETA = 10000.0
SSD_WIDTH = 4096
SSD_GROUPS = 8
SSD_STATE = 128
SSD_GW = 512
SSD_HPG = 8
SSD_CONV = 4
CONV_DIM = 6144
SSD_HEADS = 64
LS = 64

C_Q, C_K, C_V, C_G, C_Z, C_XBC, C_GR, C_GS = 0, 2048, 4096, 6144, 8192, 12288, 18432, 20480
N_MAIN = 22528
DT_OFF = 18432
IN_PROJ = 22592
N_SHARD = 4
W_IN_SHARD = IN_PROJ // N_SHARD

ADAM_LR, ADAM_B1, ADAM_B2, ADAM_EPS, ADAM_WD, ADAM_STEP = 0.001, 0.9, 0.999, 1e-08, 0.01, 10

VMEM_LIMIT = 56 * 1024 * 1024


def _params(dims):
    return pltpu.CompilerParams(dimension_semantics=dims, vmem_limit_bytes=VMEM_LIMIT)


def _silu(x):
    return x * jax.nn.sigmoid(x)


def _dsilu(x):
    s = jax.nn.sigmoid(x)
    return s * (1.0 + x * (1.0 - s))


def _nt(a, b):
    return lax.dot_general(a, b, (((1,), (1,)), ((), ())), preferred_element_type=f32)


def _tn(a, b):
    return lax.dot_general(a, b, (((0,), (0,)), ((), ())), preferred_element_type=f32)


def _nn(a, b):
    return jnp.dot(a, b, preferred_element_type=f32)


def _hi(a, b):
    return jnp.dot(a, b, precision=HIGHEST, preferred_element_type=f32)


def _mm(pairs, M, N, *, tm, tn, out_dtype, name, tb=False):
    P = len(pairs)
    nks = [K // tk for (_, _, _, _, _, K, tk) in pairs]
    starts = [int(s) for s in np.cumsum([0] + nks[:-1])]
    KT = int(sum(nks))
    in_specs, args = [], []
    for (a, a_cb, b, b_kb, b_nb, K, tk), s, nk in zip(pairs, starts, nks):
        def kk(k, s=s, nk=nk):
            return jnp.clip(k - s, 0, nk - 1)
        in_specs.append(pl.BlockSpec((tm, tk), lambda m, n, k, kk=kk, a_cb=a_cb: (m, a_cb + kk(k))))
        if tb:
            in_specs.append(pl.BlockSpec((tn, tk), lambda m, n, k, kk=kk, b_kb=b_kb, b_nb=b_nb: (b_nb + n, b_kb + kk(k))))
        else:
            in_specs.append(pl.BlockSpec((tk, tn), lambda m, n, k, kk=kk, b_kb=b_kb, b_nb=b_nb: (b_kb + kk(k), b_nb + n)))
        args += [a, b]

    def body(*refs):
        o_ref = refs[2 * P]
        k = pl.program_id(2)

        def prod(i):
            a = refs[2 * i][...].astype(bf16)
            b = refs[2 * i + 1][...].astype(bf16)
            return _nt(a, b) if tb else _nn(a, b)

        if KT == 1:
            o_ref[...] = prod(0).astype(out_dtype)
            return
        acc = refs[2 * P + 1]

        @pl.when(k == 0)
        def _():
            acc[...] = jnp.zeros_like(acc)

        for i in range(P):
            @pl.when((k >= starts[i]) & (k < starts[i] + nks[i]))
            def _(i=i):
                acc[...] += prod(i)

        @pl.when(k == KT - 1)
        def _():
            o_ref[...] = acc[...].astype(out_dtype)

    return pl.pallas_call(
        body, name=name, grid=(M // tm, N // tn, KT), in_specs=in_specs,
        out_specs=pl.BlockSpec((tm, tn), lambda m, n, k: (m, n)),
        out_shape=jax.ShapeDtypeStruct((M, N), out_dtype),
        scratch_shapes=[] if KT == 1 else [pltpu.VMEM((tm, tn), f32)],
        compiler_params=_params(("parallel", "parallel", "arbitrary")),
    )(*args)


def _mm1(a, b, *, tm, tn, tk, out_dtype, name, tb=False, a_cb=0, b_nb=0, b_kb=0, K=None, N=None):
    M = a.shape[0]
    if K is None:
        K = a.shape[1]
    if N is None:
        N = b.shape[0] if tb else b.shape[1]
    return _mm([(a, a_cb, b, b_kb, b_nb, K, tk)], M, N, tm=tm, tn=tn, out_dtype=out_dtype, name=name, tb=tb)


def _norm1_fwd(x, w, tr):
    S, D = x.shape

    def body(x_ref, w_ref, h_ref):
        xv = x_ref[...]
        r = lax.rsqrt(jnp.mean(xv * xv, axis=-1, keepdims=True) + EPS)
        h_ref[...] = (xv * r * w_ref[...]).astype(bf16)

    return pl.pallas_call(
        body, name="norm1_fwd", grid=(S // tr,),
        in_specs=[pl.BlockSpec((tr, D), lambda i: (i, 0)), pl.BlockSpec((1, D), lambda i: (0, 0))],
        out_specs=pl.BlockSpec((tr, D), lambda i: (i, 0)),
        out_shape=jax.ShapeDtypeStruct((S, D), bf16), compiler_params=_params(("parallel",)),
    )(x, w)


def _norm1_bwd(x, w, dh, dx2, tr):
    S, D = x.shape

    def body(x_ref, w_ref, dh_ref, dx2_ref, gx_ref, gw_ref):
        @pl.when(pl.program_id(0) == 0)
        def _():
            gw_ref[...] = jnp.zeros_like(gw_ref)

        xv = x_ref[...]
        r = lax.rsqrt(jnp.mean(xv * xv, axis=-1, keepdims=True) + EPS)
        xh = xv * r
        dhv = dh_ref[...]
        gw_ref[...] += jnp.sum(dhv * xh, axis=0, keepdims=True)
        dxh = dhv * w_ref[...]
        gx_ref[...] = dx2_ref[...] + r * (dxh - xh * jnp.mean(dxh * xh, axis=-1, keepdims=True))

    row = pl.BlockSpec((tr, D), lambda i: (i, 0))
    vec = pl.BlockSpec((1, D), lambda i: (0, 0))
    return pl.pallas_call(
        body, name="norm1_bwd", grid=(S // tr,), in_specs=[row, vec, row, row], out_specs=[row, vec],
        out_shape=[jax.ShapeDtypeStruct((S, D), f32), jax.ShapeDtypeStruct((1, D), f32)],
        compiler_params=_params(("arbitrary",)),
    )(x, w, dh, dx2)


def _final_fwd_bwd(x, mo, target, wf, tr):
    S, D = x.shape

    def body(x_ref, mo_ref, t_ref, w_ref, dx2_ref, loss_ref, gw_ref):
        @pl.when(pl.program_id(0) == 0)
        def _():
            gw_ref[...] = jnp.zeros_like(gw_ref)
            loss_ref[...] = jnp.zeros_like(loss_ref)

        x2 = x_ref[...] + mo_ref[...]
        r = lax.rsqrt(jnp.mean(x2 * x2, axis=-1, keepdims=True) + EPS)
        xh = x2 * r
        wv = w_ref[...]
        err = xh * wv - t_ref[...]
        loss_ref[...] += 0.5 * jnp.sum(jnp.mean(err * err, axis=-1, keepdims=True), axis=0, keepdims=True)
        dy = err * (1.0 / D)
        gw_ref[...] += jnp.sum(dy * xh, axis=0, keepdims=True)
        dxh = dy * wv
        dx2_ref[...] = r * (dxh - xh * jnp.mean(dxh * xh, axis=-1, keepdims=True))

    row = pl.BlockSpec((tr, D), lambda i: (i, 0))
    vec = pl.BlockSpec((1, D), lambda i: (0, 0))
    return pl.pallas_call(
        body, name="final_norm_loss", grid=(S // tr,), in_specs=[row, row, row, vec],
        out_specs=[row, pl.BlockSpec((1, 1), lambda i: (0, 0)), vec],
        out_shape=[jax.ShapeDtypeStruct((S, D), f32), jax.ShapeDtypeStruct((1, 1), f32), jax.ShapeDtypeStruct((1, D), f32)],
        compiler_params=_params(("arbitrary",)),
    )(x, mo, target, wf)


def _merge_fwd(p_r, p_s, proj, tr):
    S, D = p_r.shape

    def body(pr_ref, ps_ref, gr_ref, gs_ref, o_ref):
        o_ref[...] = (jax.nn.sigmoid(gr_ref[...]) * pr_ref[...] + jax.nn.sigmoid(gs_ref[...]) * ps_ref[...]).astype(bf16)

    row = pl.BlockSpec((tr, D), lambda i: (i, 0))
    return pl.pallas_call(
        body, name="merge_fwd", grid=(S // tr,),
        in_specs=[row, row, pl.BlockSpec((tr, D), lambda i: (i, C_GR // D)), pl.BlockSpec((tr, D), lambda i: (i, C_GS // D))],
        out_specs=row, out_shape=jax.ShapeDtypeStruct((S, D), bf16), compiler_params=_params(("parallel",)),
    )(p_r, p_s, proj, proj)


def _merge_bwd(dm, p_r, p_s, proj, tr):
    S, D = p_r.shape

    def body(dm_ref, pr_ref, ps_ref, gr_ref, gs_ref, dpr_ref, dps_ref, dgr_ref, dgs_ref):
        dmv = dm_ref[...]
        sr = jax.nn.sigmoid(gr_ref[...])
        ss = jax.nn.sigmoid(gs_ref[...])
        dpr_ref[...] = (dmv * sr).astype(bf16)
        dps_ref[...] = (dmv * ss).astype(bf16)
        dgr_ref[...] = (dmv * pr_ref[...] * sr * (1.0 - sr)).astype(bf16)
        dgs_ref[...] = (dmv * ps_ref[...] * ss * (1.0 - ss)).astype(bf16)

    row = pl.BlockSpec((tr, D), lambda i: (i, 0))
    o = jax.ShapeDtypeStruct((S, D), bf16)
    return pl.pallas_call(
        body, name="merge_bwd", grid=(S // tr,),
        in_specs=[row, row, row, pl.BlockSpec((tr, D), lambda i: (i, C_GR // D)), pl.BlockSpec((tr, D), lambda i: (i, C_GS // D))],
        out_specs=[row, row, row, row], out_shape=[o, o, o, o], compiler_params=_params(("parallel",)),
    )(dm, p_r, p_s, proj, proj)


def _ssd_norm_fwd(y, proj, w, tr):
    S, W = y.shape

    def body(y_ref, z_ref, w_ref, o_ref):
        u = y_ref[...] * _silu(z_ref[...])
        r = lax.rsqrt(jnp.mean(u * u, axis=-1, keepdims=True) + EPS)
        o_ref[...] = (u * r * w_ref[...]).astype(bf16)

    row = pl.BlockSpec((tr, W), lambda i: (i, 0))
    return pl.pallas_call(
        body, name="ssd_norm_fwd", grid=(S // tr,),
        in_specs=[row, pl.BlockSpec((tr, W), lambda i: (i, C_Z // W)), pl.BlockSpec((1, W), lambda i: (0, 0))],
        out_specs=row, out_shape=jax.ShapeDtypeStruct((S, W), bf16), compiler_params=_params(("parallel",)),
    )(y, proj, w)


def _ssd_norm_bwd(y, proj, w, dys, tr):
    S, W = y.shape

    def body(y_ref, z_ref, w_ref, d_ref, dy_ref, dz_ref, gw_ref):
        @pl.when(pl.program_id(0) == 0)
        def _():
            gw_ref[...] = jnp.zeros_like(gw_ref)

        yv, zv, dv = y_ref[...], z_ref[...], d_ref[...]
        sz = _silu(zv)
        u = yv * sz
        r = lax.rsqrt(jnp.mean(u * u, axis=-1, keepdims=True) + EPS)
        un = u * r
        gw_ref[...] += jnp.sum(dv * un, axis=0, keepdims=True)
        dun = dv * w_ref[...]
        du = r * (dun - un * jnp.mean(dun * un, axis=-1, keepdims=True))
        dy_ref[...] = du * sz
        dz_ref[...] = (du * yv * _dsilu(zv)).astype(bf16)

    row = pl.BlockSpec((tr, W), lambda i: (i, 0))
    vec = pl.BlockSpec((1, W), lambda i: (0, 0))
    return pl.pallas_call(
        body, name="ssd_norm_bwd", grid=(S // tr,),
        in_specs=[row, pl.BlockSpec((tr, W), lambda i: (i, C_Z // W)), vec, row], out_specs=[row, row, vec],
        out_shape=[jax.ShapeDtypeStruct((S, W), f32), jax.ShapeDtypeStruct((S, W), bf16), jax.ShapeDtypeStruct((1, W), f32)],
        compiler_params=_params(("arbitrary",)),
    )(y, proj, w, dys)


def _rope(t, cos, sin):
    t1, t2 = t[:, :128], t[:, 128:]
    return jnp.concatenate([t1 * cos - t2 * sin, t2 * cos + t1 * sin], axis=1)


def _rope_t(d, cos, sin):
    d1, d2 = d[:, :128], d[:, 128:]
    return jnp.concatenate([d1 * cos + d2 * sin, d2 * cos - d1 * sin], axis=1)


def _ret_specs(tb, rev_nb=None):
    def blk(i):
        return i if rev_nb is None else rev_nb - 1 - i
    col = lambda off: pl.BlockSpec((tb, 256), lambda h, i, off=off: (blk(i), off // 256 + h))
    tab = pl.BlockSpec((tb, 128), lambda h, i: (blk(i), 0))
    mat = pl.BlockSpec((1, CHUNK, CHUNK), lambda h, i: (h, 0, 0))
    vec = pl.BlockSpec((1, CHUNK, 1), lambda h, i: (h, 0, 0))
    one = pl.BlockSpec((1, 1, 1), lambda h, i: (h, 0, 0))
    own = pl.BlockSpec((tb, 256), lambda h, i: (blk(i), h))
    st = pl.BlockSpec((1, tb // CHUNK, 256, 256), lambda h, i: (h, blk(i), 0, 0))
    return col, tab, mat, vec, one, own, st


def _ret_fwd(proj, cos, sin, intra, qdec, kdec, cdec, tb):
    S = proj.shape[0]
    nc = S // CHUNK
    scale = RET_DK ** -0.5

    def body(q_ref, k_ref, v_ref, g_ref, cos_ref, sin_ref, m_ref, qd_ref, kd_ref, cd_ref, y_ref, yr_ref, st_ref, st):
        @pl.when(pl.program_id(1) == 0)
        def _():
            st[...] = jnp.zeros_like(st)

        mm, qd, kd, cd = m_ref[0], qd_ref[0], kd_ref[0], cd_ref[0]

        def chunk(c, carry):
            rows = pl.ds(pl.multiple_of(c * CHUNK, CHUNK), CHUNK)
            cs, sn = cos_ref[rows, :], sin_ref[rows, :]
            qr = _rope(q_ref[rows, :], cs, sn)
            kr = _rope(k_ref[rows, :], cs, sn) * scale
            qb, kb, vb = qr.astype(bf16), kr.astype(bf16), v_ref[rows, :].astype(bf16)
            stb = st[...].astype(bf16)
            st_ref[0, c] = stb
            sc = (_nt(qb, kb) * mm).astype(bf16)
            y = _nn(sc, vb) + _nn(qb, stb) * qd
            st[...] = st[...] * cd + _tn((kr * kd).astype(bf16), vb)
            y_ref[rows, :] = y
            mu = jnp.mean(y, axis=-1, keepdims=True)
            yc = y - mu
            var = jnp.mean(yc * yc, axis=-1, keepdims=True)
            yr_ref[rows, :] = (yc * lax.rsqrt(var + EPS) * _silu(g_ref[rows, :])).astype(bf16)
            return carry

        lax.fori_loop(0, tb // CHUNK, chunk, 0)

    col, tab, mat, vec, one, own, stspec = _ret_specs(tb)
    return pl.pallas_call(
        body, name="ret_fwd", grid=(RET_HEADS, S // tb),
        in_specs=[col(C_Q), col(C_K), col(C_V), col(C_G), tab, tab, mat, vec, vec, one],
        out_specs=[own, own, stspec],
        out_shape=[jax.ShapeDtypeStruct((S, 2048), f32), jax.ShapeDtypeStruct((S, 2048), bf16),
                   jax.ShapeDtypeStruct((RET_HEADS, nc, 256, 256), bf16)],
        scratch_shapes=[pltpu.VMEM((256, 256), f32)], compiler_params=_params(("parallel", "arbitrary")),
    )(proj, proj, proj, proj, cos, sin, intra, qdec, kdec, cdec)


def _ret_bwd(proj, cos, sin, intra, qdec, kdec, cdec, y, dyr, states, tb):
    S = proj.shape[0]
    nb = S // tb
    nck = tb // CHUNK
    scale = RET_DK ** -0.5

    def body(q_ref, k_ref, v_ref, g_ref, cos_ref, sin_ref, m_ref, qd_ref, kd_ref, cd_ref, y_ref, dyr_ref, st_ref,
             dq_ref, dk_ref, dv_ref, dg_ref, dst):
        @pl.when(pl.program_id(1) == 0)
        def _():
            dst[...] = jnp.zeros_like(dst)

        mm, qd, kd, cd = m_ref[0], qd_ref[0], kd_ref[0], cd_ref[0]

        def chunk(cc, carry):
            c = nck - 1 - cc
            rows = pl.ds(pl.multiple_of(c * CHUNK, CHUNK), CHUNK)
            cs, sn = cos_ref[rows, :], sin_ref[rows, :]
            qr = _rope(q_ref[rows, :], cs, sn)
            kr = _rope(k_ref[rows, :], cs, sn) * scale
            qb, kb, vb = qr.astype(bf16), kr.astype(bf16), v_ref[rows, :].astype(bf16)
            kdb = (kr * kd).astype(bf16)
            stb = st_ref[0, c]
            yv, gv, dyrv = y_ref[rows, :], g_ref[rows, :], dyr_ref[rows, :]
            mu = jnp.mean(yv, axis=-1, keepdims=True)
            yc = yv - mu
            rstd = lax.rsqrt(jnp.mean(yc * yc, axis=-1, keepdims=True) + EPS)
            yn = yc * rstd
            dg_ref[rows, :] = (dyrv * yn * _dsilu(gv)).astype(bf16)
            dyn = dyrv * _silu(gv)
            dy = rstd * (dyn - jnp.mean(dyn, axis=-1, keepdims=True) - yn * jnp.mean(dyn * yn, axis=-1, keepdims=True))
            dyb = dy.astype(bf16)
            dyqb = (dy * qd).astype(bf16)
            dstb = dst[...].astype(bf16)
            sct = (_nt(kb, qb) * mm).astype(bf16)
            ds = (_nt(dyb, vb) * mm).astype(bf16)
            dsT = (_nt(vb, dyb) * mm).astype(bf16)
            dv = _nn(sct, dyb) + _nn(kdb, dstb)
            dqr = _nn(ds, kb) + _nt(dyqb, stb)
            dkr = _nn(dsT, qb) + _nt(vb, dstb) * kd
            dst[...] = dst[...] * cd + _tn(qb, dyqb)
            dq_ref[rows, :] = _rope_t(dqr, cs, sn).astype(bf16)
            dk_ref[rows, :] = (_rope_t(dkr, cs, sn) * scale).astype(bf16)
            dv_ref[rows, :] = dv.astype(bf16)
            return carry

        lax.fori_loop(0, nck, chunk, 0)

    col, tab, mat, vec, one, own, stspec = _ret_specs(tb, rev_nb=nb)
    o = jax.ShapeDtypeStruct((S, 2048), bf16)
    return pl.pallas_call(
        body, name="ret_bwd", grid=(RET_HEADS, nb),
        in_specs=[col(C_Q), col(C_K), col(C_V), col(C_G), tab, tab, mat, vec, vec, one, own, own, stspec],
        out_specs=[own, own, own, own], out_shape=[o, o, o, o],
        scratch_shapes=[pltpu.VMEM((256, 256), f32)], compiler_params=_params(("parallel", "arbitrary")),
    )(proj, proj, proj, proj, cos, sin, intra, qdec, kdec, cdec, y, dyr, states)


def _conv_fwd(proj, conv_w, conv_b, tb, cw):
    S = proj.shape[0]
    off = C_XBC // cw

    def body(x_ref, halo_ref, w_ref, b_ref, o_ref, xe):
        xe[pl.ds(0, 8), :] = jnp.where(pl.program_id(1) == 0, 0.0, halo_ref[...])
        xe[pl.ds(8, tb), :] = x_ref[...]
        acc = b_ref[...] + w_ref[pl.ds(0, 1), :] * xe[pl.ds(5, tb), :]
        for j in range(1, SSD_CONV):
            acc = acc + w_ref[pl.ds(j, 1), :] * xe[pl.ds(5 + j, tb), :]
        o_ref[...] = acc

    return pl.pallas_call(
        body, name="conv_fwd", grid=(CONV_DIM // cw, S // tb),
        in_specs=[pl.BlockSpec((tb, cw), lambda j, i: (i, off + j)),
                  pl.BlockSpec((8, cw), lambda j, i: (jnp.maximum(i * (tb // 8) - 1, 0), off + j)),
                  pl.BlockSpec((SSD_CONV, cw), lambda j, i: (0, j)), pl.BlockSpec((1, cw), lambda j, i: (0, j))],
        out_specs=pl.BlockSpec((tb, cw), lambda j, i: (i, j)),
        out_shape=jax.ShapeDtypeStruct((S, CONV_DIM), f32),
        scratch_shapes=[pltpu.VMEM((tb + 8, cw), f32)], compiler_params=_params(("parallel", "arbitrary")),
    )(proj, proj, conv_w, conv_b)


def _conv_bwd(dpre, proj, conv_w, col0, tb, cw, name):
    S, n = dpre.shape
    nb = S // tb
    xoff = (C_XBC + col0) // cw
    woff = col0 // cw

    def body(d_ref, dh_ref, x_ref, xh_ref, w_ref, dx_ref, gw_ref, gb_ref, de, xe):
        i = pl.program_id(1)

        @pl.when(i == 0)
        def _():
            gw_ref[...] = jnp.zeros_like(gw_ref)
            gb_ref[...] = jnp.zeros_like(gb_ref)

        dv = d_ref[...]
        de[pl.ds(0, tb), :] = dv
        de[pl.ds(tb, 8), :] = jnp.where(i == nb - 1, 0.0, dh_ref[...])
        xe[pl.ds(0, 8), :] = jnp.where(i == 0, 0.0, xh_ref[...])
        xe[pl.ds(8, tb), :] = x_ref[...]
        acc = w_ref[pl.ds(SSD_CONV - 1, 1), :] * dv
        for j in range(SSD_CONV - 1):
            acc = acc + w_ref[pl.ds(j, 1), :] * de[pl.ds(3 - j, tb), :]
        dx_ref[...] = acc.astype(bf16)
        gb_ref[...] += jnp.sum(dv, axis=0, keepdims=True)
        for j in range(SSD_CONV):
            gw_ref[pl.ds(j, 1), :] += jnp.sum(dv * xe[pl.ds(5 + j, tb), :], axis=0, keepdims=True)

    return pl.pallas_call(
        body, name=name, grid=(n // cw, nb),
        in_specs=[pl.BlockSpec((tb, cw), lambda j, i: (i, j)),
                  pl.BlockSpec((8, cw), lambda j, i: (jnp.minimum((i + 1) * (tb // 8), S // 8 - 1), j)),
                  pl.BlockSpec((tb, cw), lambda j, i: (i, xoff + j)),
                  pl.BlockSpec((8, cw), lambda j, i: (jnp.maximum(i * (tb // 8) - 1, 0), xoff + j)),
                  pl.BlockSpec((SSD_CONV, cw), lambda j, i: (0, woff + j))],
        out_specs=[pl.BlockSpec((tb, cw), lambda j, i: (i, j)), pl.BlockSpec((SSD_CONV, cw), lambda j, i: (0, j)),
                   pl.BlockSpec((1, cw), lambda j, i: (0, j))],
        out_shape=[jax.ShapeDtypeStruct((S, n), bf16), jax.ShapeDtypeStruct((SSD_CONV, n), f32), jax.ShapeDtypeStruct((1, n), f32)],
        scratch_shapes=[pltpu.VMEM((tb + 8, cw), f32), pltpu.VMEM((tb + 8, cw), f32)],
        compiler_params=_params(("parallel", "arbitrary")),
    )(dpre, dpre, proj, proj, conv_w)


def _dt_prep(dt_raw, dt_bias, a_log, tb):
    S = dt_raw.shape[0]

    def body(r_ref, b_ref, al_ref, dt_ref, sg_ref, ac_ref):
        li = lax.broadcasted_iota(jnp.int32, (LS, LS), 0)
        si = lax.broadcasted_iota(jnp.int32, (LS, LS), 1)
        tri = (li >= si).astype(f32)
        neg_a = -jnp.exp(al_ref[...])
        for c in range(tb // LS):
            rows = pl.ds(c * LS, LS)
            xv = r_ref[rows, :] + b_ref[...]
            dtv = jax.nn.softplus(xv)
            dt_ref[rows, :] = dtv
            sg_ref[rows, :] = jax.nn.sigmoid(xv)
            ac_ref[rows, :] = _hi(tri, dtv * neg_a)

    row = pl.BlockSpec((tb, 128), lambda i: (i, 0))
    vec = pl.BlockSpec((1, 128), lambda i: (0, 0))
    o = jax.ShapeDtypeStruct((S, 128), f32)
    return pl.pallas_call(body, name="dt_prep", grid=(S // tb,), in_specs=[row, vec, vec], out_specs=[row, row, row],
                          out_shape=[o, o, o], compiler_params=_params(("parallel",)))(dt_raw, dt_bias, a_log)


def _group_major(t):
    S = t.shape[0]
    return jnp.transpose(t[:, :SSD_HEADS].reshape(S, SSD_GROUPS, SSD_HPG), (1, 0, 2))


def _group_major_t(t):
    S = t.shape[0]
    return jnp.transpose(t[:, :SSD_HEADS].reshape(S // LS, LS, SSD_GROUPS, SSD_HPG), (2, 0, 3, 1))


def _ssd_specs(tb, rev_nb=None):
    def blk(i):
        return i if rev_nb is None else rev_nb - 1 - i
    xs = pl.BlockSpec((tb, SSD_GW), lambda g, i: (blk(i), g))
    bm = pl.BlockSpec((tb, SSD_STATE), lambda g, i: (blk(i), SSD_WIDTH // SSD_STATE + g))
    cm = pl.BlockSpec((tb, SSD_STATE), lambda g, i: (blk(i), SSD_WIDTH // SSD_STATE + SSD_GROUPS + g))
    ph = pl.BlockSpec((1, tb, SSD_HPG), lambda g, i: (g, blk(i), 0))
    pht = pl.BlockSpec((1, tb // LS, SSD_HPG, LS), lambda g, i: (g, blk(i), 0, 0))
    gvec = pl.BlockSpec((1, 1, SSD_GW), lambda g, i: (g, 0, 0))
    ex = pl.BlockSpec((SSD_HPG, SSD_GW), lambda g, i: (0, 0))
    st = pl.BlockSpec((1, tb // LS, SSD_STATE, SSD_GW), lambda g, i: (g, blk(i), 0, 0))
    return xs, bm, cm, ph, pht, gvec, ex, st


def _expander():
    return jnp.repeat(jnp.eye(SSD_HPG, dtype=f32), SSD_GW // SSD_HPG, axis=1)


def _ssd_fwd(pre, dt_g, ac_g, act_g, dskx, tb):
    S = pre.shape[0]
    nc = S // LS
    hd = SSD_GW // SSD_HPG

    def body(xs_ref, bm_ref, cm_ref, dt_ref, ac_ref, act_ref, dsk_ref, ex_ref, y_ref, st_ref, st):
        @pl.when(pl.program_id(1) == 0)
        def _():
            st[...] = jnp.zeros_like(st)

        ex = ex_ref[...]
        li = lax.broadcasted_iota(jnp.int32, (LS, LS), 0)
        si = lax.broadcasted_iota(jnp.int32, (LS, LS), 1)
        causal = li >= si

        def chunk(c, carry):
            rows = pl.ds(pl.multiple_of(c * LS, LS), LS)
            xs = _silu(xs_ref[rows, :])
            bcb = _silu(bm_ref[rows, :]).astype(bf16)
            ccb = _silu(cm_ref[rows, :]).astype(bf16)
            dt8, ac8, act = dt_ref[0, rows, :], ac_ref[0, rows, :], act_ref[0, c]
            xdt = xs * _hi(dt8, ex)
            eax = _hi(jnp.exp(ac8), ex)
            tailx = _hi(jnp.exp(ac8[LS - 1:LS, :] - ac8), ex)
            cb = _nt(ccb, bcb)
            stb = st[...].astype(bf16)
            st_ref[0, c] = stb
            xdtb = xdt.astype(bf16)
            outs = []
            for h in range(SSD_HPG):
                dec = jnp.exp(jnp.where(causal, ac8[:, h:h + 1] - act[h:h + 1, :], -1e30))
                outs.append(_nn((cb * dec).astype(bf16), xdtb[:, hd * h:hd * (h + 1)]))
            y_ref[rows, :] = jnp.concatenate(outs, axis=1) + _nn(ccb, stb) * eax + dsk_ref[0] * xs
            st[...] = st[...] * eax[LS - 1:LS, :] + _tn(bcb, (xdt * tailx).astype(bf16))
            return carry

        lax.fori_loop(0, tb // LS, chunk, 0)

    xs, bm, cm, ph, pht, gvec, ex, stspec = _ssd_specs(tb)
    return pl.pallas_call(
        body, name="ssd_fwd", grid=(SSD_GROUPS, S // tb),
        in_specs=[xs, bm, cm, ph, ph, pht, gvec, ex], out_specs=[xs, stspec],
        out_shape=[jax.ShapeDtypeStruct((S, SSD_WIDTH), f32), jax.ShapeDtypeStruct((SSD_GROUPS, nc, SSD_STATE, SSD_GW), bf16)],
        scratch_shapes=[pltpu.VMEM((SSD_STATE, SSD_GW), f32)], compiler_params=_params(("parallel", "arbitrary")),
    )(pre, pre, pre, dt_g, ac_g, act_g, dskx, _expander())


def _ssd_bwd(pre, dt_g, ac_g, act_g, sg_g, dskx, nega_g, dy, states, tb):
    S = pre.shape[0]
    nb = S // tb
    nck = tb // LS
    hd = SSD_GW // SSD_HPG

    def body(xs_ref, bm_ref, cm_ref, dt_ref, ac_ref, act_ref, sg_ref, dsk_ref, na_ref, ex_ref, ext_ref, dy_ref, st_ref,
             dx_ref, db_ref, dc_ref, ddt_ref, gsk_ref, gal_ref, gdb_ref, dst, skacc):
        @pl.when(pl.program_id(1) == 0)
        def _():
            dst[...] = jnp.zeros_like(dst)
            skacc[...] = jnp.zeros_like(skacc)
            gal_ref[...] = jnp.zeros_like(gal_ref)
            gdb_ref[...] = jnp.zeros_like(gdb_ref)

        ex, ext = ex_ref[...], ext_ref[...]
        li = lax.broadcasted_iota(jnp.int32, (LS, LS), 0)
        si = lax.broadcasted_iota(jnp.int32, (LS, LS), 1)
        causal = li >= si
        anti = si >= li
        upper = anti.astype(f32)
        last_row = (lax.broadcasted_iota(jnp.int32, (LS, 1), 0) == LS - 1).astype(f32)
        head_id = lax.broadcasted_iota(jnp.int32, (1, SSD_HPG), 1)
        neg_a = na_ref[0]
        dskv = dsk_ref[0]

        def chunk(cc, carry):
            c = nck - 1 - cc
            rows = pl.ds(pl.multiple_of(c * LS, LS), LS)
            px, pb, pc = xs_ref[rows, :], bm_ref[rows, :], cm_ref[rows, :]
            xs = _silu(px)
            bcb = _silu(pb).astype(bf16)
            ccb = _silu(pc).astype(bf16)
            dt8, ac8, act = dt_ref[0, rows, :], ac_ref[0, rows, :], act_ref[0, c]
            dtx = _hi(dt8, ex)
            xdt = xs * dtx
            eax = _hi(jnp.exp(ac8), ex)
            tailx = _hi(jnp.exp(ac8[LS - 1:LS, :] - ac8), ex)
            ex_last = eax[LS - 1:LS, :]
            stb = st_ref[0, c]
            dyv = dy_ref[rows, :]
            dyb = dyv.astype(bf16)
            xdtb = xdt.astype(bf16)
            skacc[...] += jnp.sum(dyv * xs, axis=0, keepdims=True)
            yinter = _nn(ccb, stb) * eax
            dzb = (dyv * eax).astype(bf16)
            dcc = _nt(dzb, stb)
            dstv = dst[...]
            dstb = dstv.astype(bf16)
            xt = xdt * tailx
            dxt = _nn(bcb, dstb)
            dbc = _nt(xt.astype(bf16), dstb)
            dxdt = dxt * tailx
            lastrow = jnp.sum(dxt * xt, axis=0, keepdims=True) + jnp.sum(dstv * stb.astype(f32), axis=0, keepdims=True) * ex_last
            dst[...] = dstv * ex_last + _tn(ccb, dzb)
            cb = _nt(ccb, bcb)
            cbt = _nt(bcb, ccb)
            dcb = jnp.zeros((LS, LS), f32)
            dcbt = jnp.zeros((LS, LS), f32)
            dac8 = jnp.zeros((LS, SSD_HPG), f32)
            dxin = []
            for h in range(SSD_HPG):
                sl = slice(hd * h, hd * (h + 1))
                col, rowv = ac8[:, h:h + 1], act[h:h + 1, :]
                dec = jnp.exp(jnp.where(causal, col - rowv, -1e30))
                dect = jnp.exp(jnp.where(anti, rowv - col, -1e30))
                gm, gmt = cb * dec, cbt * dect
                dgm, dgmt = _nt(dyb[:, sl], xdtb[:, sl]), _nt(xdtb[:, sl], dyb[:, sl])
                dxin.append(_nn(gmt.astype(bf16), dyb[:, sl]))
                dcb = dcb + dgm * dec
                dcbt = dcbt + dgmt * dect
                dcol = jnp.sum(dgm * gm, axis=1, keepdims=True) - jnp.sum(dgmt * gmt, axis=1, keepdims=True)
                dac8 = dac8 + dcol * (head_id == h).astype(f32)
            dxintra = jnp.concatenate(dxin, axis=1)
            dcc = dcc + _nn(dcb.astype(bf16), bcb)
            dbc = dbc + _nn(dcbt.astype(bf16), ccb)
            dxdt = dxdt + dxintra
            dacx = dyv * yinter - dxt * xt + last_row * lastrow
            dac8 = dac8 + _hi(dacx, ext)
            da8 = _hi(upper, dac8)
            ddt8 = _hi(dxdt * xs, ext) + da8 * neg_a
            gal_ref[0] += jnp.sum(da8 * dt8 * neg_a, axis=0, keepdims=True)
            ddr = ddt8 * sg_ref[0, rows, :]
            ddt_ref[0, rows, :] = ddr
            gdb_ref[0] += jnp.sum(ddr, axis=0, keepdims=True)
            dx_ref[rows, :] = (dskv * dyv + dxdt * dtx) * _dsilu(px)
            db_ref[rows, :] = dbc * _dsilu(pb)
            dc_ref[rows, :] = dcc * _dsilu(pc)
            return carry

        lax.fori_loop(0, nck, chunk, 0)

        @pl.when(pl.program_id(1) == nb - 1)
        def _():
            gsk_ref[0] = skacc[...]

    xs, bm, cm, ph, pht, gvec, ex, stspec = _ssd_specs(tb, rev_nb=nb)
    small = pl.BlockSpec((1, 1, SSD_HPG), lambda g, i: (g, 0, 0))
    ext = pl.BlockSpec((SSD_GW, SSD_HPG), lambda g, i: (0, 0))
    own_b = pl.BlockSpec((tb, SSD_STATE), lambda g, i: (nb - 1 - i, g))
    sm = jax.ShapeDtypeStruct((SSD_GROUPS, 1, SSD_HPG), f32)
    expander = _expander()
    return pl.pallas_call(
        body, name="ssd_bwd", grid=(SSD_GROUPS, nb),
        in_specs=[xs, bm, cm, ph, ph, pht, ph, gvec, small, ex, ext, xs, stspec],
        out_specs=[xs, own_b, own_b, ph, gvec, small, small],
        out_shape=[jax.ShapeDtypeStruct((S, SSD_WIDTH), f32), jax.ShapeDtypeStruct((S, SSD_GROUPS * SSD_STATE), f32),
                   jax.ShapeDtypeStruct((S, SSD_GROUPS * SSD_STATE), f32), jax.ShapeDtypeStruct((SSD_GROUPS, S, SSD_HPG), f32),
                   jax.ShapeDtypeStruct((SSD_GROUPS, 1, SSD_GW), f32), sm, sm],
        scratch_shapes=[pltpu.VMEM((SSD_STATE, SSD_GW), f32), pltpu.VMEM((1, SSD_GW), f32)],
        compiler_params=_params(("parallel", "arbitrary")),
    )(pre, pre, pre, dt_g, ac_g, act_g, sg_g, dskx, nega_g, expander, expander.T, dy, states)


def _tiles(S):
    return dict(tb=min(512, S), tr=min(256, S), tm=min(1024, S))


def _local_step(x, positions, target, norm1_w, w_main, w_dt, conv_w, conv_b, dt_bias, a_log, d_skip, ssd_norm_w,
                w_br, w_bs, w_o, norm_f_w):
    S, D = x.shape
    t = _tiles(S)
    tb, tr, tm = t["tb"], t["tr"], t["tm"]

    half = RET_DK // 2
    inv_freq = ROPE_THETA ** (-jnp.arange(half, dtype=f32) / half)
    ang = positions.astype(f32)[:, None] * inv_freq
    cos, sin = jnp.cos(ang), jnp.sin(ang)
    log_gamma = jnp.log1p(-(2.0 ** (-5.0 - jnp.arange(RET_HEADS, dtype=f32))))
    idx = jnp.arange(CHUNK, dtype=f32)
    intra = jnp.exp(jnp.abs(idx[:, None] - idx[None, :]) * log_gamma[:, None, None])
    qdec = jnp.exp((idx + 1.0)[None, :] * log_gamma[:, None])[:, :, None]
    kdec = jnp.exp((CHUNK - 1.0 - idx)[None, :] * log_gamma[:, None])[:, :, None]
    cdec = jnp.exp(CHUNK * log_gamma)[:, None, None]

    h = _norm1_fwd(x, norm1_w, tr)
    proj = _mm1(h, w_main, tm=tm, tn=1024, tk=D, out_dtype=f32, name="proj_main")
    dt_raw = _mm1(h, w_dt, tm=tm, tn=128, tk=D, out_dtype=f32, name="proj_dt")
    y_ret, yr, ret_states = _ret_fwd(proj, cos, sin, intra, qdec, kdec, cdec, tb)
    pre = _conv_fwd(proj, conv_w, conv_b, tb, 512)
    pad64 = lambda v: jnp.pad(v, ((0, 0), (0, 128 - SSD_HEADS)))
    dt, sg, ac = _dt_prep(dt_raw, pad64(dt_bias), pad64(a_log), tb)
    dt_g, ac_g, sg_g, act_g = _group_major(dt), _group_major(ac), _group_major(sg), _group_major_t(ac)
    dskx = jnp.repeat(d_skip.reshape(SSD_GROUPS, 1, SSD_HPG), SSD_GW // SSD_HPG, axis=2)
    nega_g = (-jnp.exp(a_log)).reshape(SSD_GROUPS, 1, SSD_HPG)
    y_ssd, ssd_states = _ssd_fwd(pre, dt_g, ac_g, act_g, dskx, tb)
    ys = _ssd_norm_fwd(y_ssd, proj, ssd_norm_w, tr // 2)
    p_r = _mm1(yr, w_br, tm=tm, tn=1024, tk=2048, out_dtype=f32, name="branch_ret")
    p_s = _mm1(ys, w_bs, tm=tm, tn=1024, tk=2048, out_dtype=f32, name="branch_ssd")
    merged = _merge_fwd(p_r, p_s, proj, tr)
    mo = _mm1(merged, w_o, tm=tm, tn=1024, tk=2048, out_dtype=f32, name="out_proj")
    dx2, loss, g_norm_f = _final_fwd_bwd(x, mo, target, norm_f_w.reshape(1, D), tr)

    tkt = min(1024, S)
    wg = lambda at, b, name, tn=1024: _mm1(at, b, tm=min(1024, at.shape[0]), tn=tn, tk=tkt, out_dtype=f32, name=name)
    dm = _mm1(dx2, w_o, tm=tm, tn=1024, tk=1024, out_dtype=f32, name="d_merged", tb=True)
    g_w_o = wg(merged.T, dx2, "g_w_out")
    dp_r, dp_s, dg_r, dg_s = _merge_bwd(dm, p_r, p_s, proj, tr)
    dyr = _mm1(dp_r, w_br, tm=tm, tn=1024, tk=2048, out_dtype=f32, name="d_yr", tb=True)
    dys = _mm1(dp_s, w_bs, tm=tm, tn=1024, tk=2048, out_dtype=f32, name="d_ys", tb=True)
    g_w_br = wg(yr.T, dp_r, "g_w_br_ret")
    g_w_bs = wg(ys.T, dp_s, "g_w_br_ssd")
    dy_ssd, dz, g_ssd_norm = _ssd_norm_bwd(y_ssd, proj, ssd_norm_w, dys, tr // 2)
    dq, dk, dv, dg = _ret_bwd(proj, cos, sin, intra, qdec, kdec, cdec, y_ret, dyr, ret_states, tb)
    dpx, dpb, dpc, ddt_g, gsk, gal, gdb = _ssd_bwd(pre, dt_g, ac_g, act_g, sg_g, dskx, nega_g, dy_ssd, ssd_states, tb)
    dxx, gcw_x, gcb_x = _conv_bwd(dpx, proj, conv_w, 0, tb, 512, "conv_bwd_x")
    dxb, gcw_b, gcb_b = _conv_bwd(dpb, proj, conv_w, SSD_WIDTH, tb, 512, "conv_bwd_b")
    dxc, gcw_c, gcb_c = _conv_bwd(dpc, proj, conv_w, SSD_WIDTH + 1024, tb, 512, "conv_bwd_c")
    ddt = jnp.transpose(ddt_g, (1, 0, 2)).reshape(S, SSD_HEADS)
    ddt_p = jnp.pad(ddt, ((0, 0), (0, 128 - SSD_HEADS))).astype(bf16)

    pieces = [(dq, C_Q), (dk, C_K), (dv, C_V), (dg, C_G), (dz, C_Z), (dxx, C_XBC), (dxb, C_XBC + 4096), (dxc, C_XBC + 5120),
              (dg_r, C_GR), (dg_s, C_GS)]
    tk = 512
    pairs = [(p, 0, w_main, off // tk, 0, p.shape[1], tk) for p, off in pieces]
    pairs.append((ddt_p, 0, w_dt, 0, 0, 128, 128))
    dh = _mm(pairs, S, D, tm=tm, tn=1024, out_dtype=f32, name="d_h", tb=True)
    ht = h.T
    g_main = [wg(ht, p, f"g_w_in_{off}", tn=min(1024, p.shape[1])) for p, off in pieces]
    g_dt = wg(ht, ddt_p, "g_w_in_dt", tn=128)
    grad_x, g_norm1 = _norm1_bwd(x, norm1_w, dh, dx2, tr)

    seg = lambda v: jnp.sum(v.reshape(SSD_HEADS, SSD_GW // SSD_HPG), axis=1).reshape(1, SSD_HEADS)
    grads = dict(
        norm1_w=g_norm1, w_in_main=g_main, w_in_dt=g_dt,
        conv_w=jnp.concatenate([gcw_x, gcw_b, gcw_c], axis=1), conv_b=jnp.concatenate([gcb_x, gcb_b, gcb_c], axis=1),
        dt_bias=gdb.reshape(1, SSD_HEADS), a_log=gal.reshape(1, SSD_HEADS), d_skip=seg(gsk),
        ssd_norm_w=g_ssd_norm, w_br_ret=g_w_br, w_br_ssd=g_w_bs, w_out=g_w_o, norm_f_w=g_norm_f,
    )
    return loss, grad_x, grads


def _w_in_grad_full(g_main, g_dt):
    gm = jnp.concatenate(g_main, axis=1)
    return jnp.concatenate([gm[:, :DT_OFF], g_dt[:, :SSD_HEADS], gm[:, DT_OFF:]], axis=1)


def _split_w_in(w):
    w_main = jnp.concatenate([w[:, :DT_OFF], w[:, DT_OFF + SSD_HEADS:]], axis=1)
    w_dt = jnp.pad(w[:, DT_OFF:DT_OFF + SSD_HEADS], ((0, 0), (0, 128 - SSD_HEADS)))
    return w_main, w_dt


ANY = pl.BlockSpec(memory_space=pl.ANY)


def _me():
    return lax.axis_index("x"), lax.axis_index("y"), lax.axis_index("c")


def _other_chips(x, y):
    return [(1 - x, y), (x, 1 - y), (1 - x, 1 - y)]


def _gather_weights(a, b, cw):
    R = a.shape[0]
    hr = R // 2

    def body(a_ref, b_ref, cw_ref, ga_ref, gb_ref, gc_ref, send_sems, recv_sems, local_sems):
        x, y, c = _me()
        k = 2 * x + y
        sibling = (x, y, 1 - c)
        chips = _other_chips(x, y)
        mine = [pltpu.make_async_copy(a_ref, ga_ref.at[k], local_sems.at[0]),
                pltpu.make_async_copy(b_ref, gb_ref.at[k], local_sems.at[1]),
                pltpu.make_async_copy(cw_ref, gc_ref.at[k], local_sems.at[2])]
        for cp in mine:
            cp.start()

        def small(j, src_shard, to):
            return pltpu.make_async_remote_copy(
                src_ref=cw_ref, dst_ref=gc_ref.at[src_shard], send_sem=send_sems.at[12 + j], recv_sem=recv_sems.at[12 + j],
                device_id=to, device_id_type=MESH)

        def copies(j, src_shard, half, to, from_input):
            rows = pl.ds(half * hr, hr)
            out = []
            for t, (inp, g) in enumerate(((a_ref, ga_ref), (b_ref, gb_ref))):
                src = inp.at[rows, :] if from_input else g.at[src_shard, rows, :]
                out.append(pltpu.make_async_remote_copy(
                    src_ref=src, dst_ref=g.at[src_shard, rows, :], send_sem=send_sems.at[2 * j + t],
                    recv_sem=recv_sems.at[2 * j + t], device_id=to, device_id_type=MESH))
            return out

        first = []
        for j, chip in enumerate(chips):
            first += copies(j, k, c, (*chip, c), True)
            first.append(small(j, k, (*chip, c)))
        for cp in first:
            cp.start()
        passed = []
        for j, chip in enumerate(chips):
            kk = 2 * chip[0] + chip[1]
            for cp in copies(j, kk, c, (x, y, c), False):
                cp.wait_recv()
            fw = copies(3 + j, kk, c, sibling, False)
            for cp in fw:
                cp.start()
            passed += fw
        for j, chip in enumerate(chips):
            kk = 2 * chip[0] + chip[1]
            for cp in copies(3 + j, kk, 1 - c, (x, y, c), False):
                cp.wait_recv()
            small(j, kk, (x, y, c)).wait_recv()
        for cp in first + passed:
            cp.wait_send()
        for cp in mine:
            cp.wait()

    return pl.pallas_call(
        body, name="gather_weights", in_specs=[ANY, ANY, ANY], out_specs=[ANY, ANY, ANY],
        out_shape=[jax.ShapeDtypeStruct((N_SHARD,) + a.shape, a.dtype), jax.ShapeDtypeStruct((N_SHARD,) + b.shape, b.dtype),
                   jax.ShapeDtypeStruct((N_SHARD,) + cw.shape, cw.dtype)],
        scratch_shapes=[pltpu.SemaphoreType.DMA((15,)), pltpu.SemaphoreType.DMA((15,)), pltpu.SemaphoreType.DMA((3,))],
        compiler_params=pltpu.CompilerParams(has_side_effects=True),
    )(a, b, cw)


def _sibling_swap(arrs, name):
    n = len(arrs)

    def body(*refs):
        ins, outs = refs[:n], refs[n:2 * n]
        send_sems, recv_sems = refs[2 * n], refs[2 * n + 1]
        x, y, c = _me()
        cps = [pltpu.make_async_remote_copy(src_ref=ins[t].at[1 - c], dst_ref=outs[t], send_sem=send_sems.at[t],
                                            recv_sem=recv_sems.at[t], device_id=(x, y, 1 - c), device_id_type=MESH)
               for t in range(n)]
        for cp in cps:
            cp.start()
        for cp in cps:
            cp.wait()

    return pl.pallas_call(
        body, name=name, in_specs=[ANY] * n, out_specs=[ANY] * n,
        out_shape=[jax.ShapeDtypeStruct(a.shape[1:], a.dtype) for a in arrs],
        scratch_shapes=[pltpu.SemaphoreType.DMA((n,)), pltpu.SemaphoreType.DMA((n,))],
        compiler_params=pltpu.CompilerParams(has_side_effects=True),
    )(*arrs)


def _chip_exchange(arrs, name):
    n = len(arrs)

    def body(*refs):
        ins, outs = refs[:n], refs[n:2 * n]
        send_sems, recv_sems = refs[2 * n], refs[2 * n + 1]
        x, y, c = _me()
        cps = []
        for j, chip in enumerate(_other_chips(x, y)):
            kk = 2 * chip[0] + chip[1]
            for t in range(n):
                cps.append(pltpu.make_async_remote_copy(
                    src_ref=ins[t].at[kk], dst_ref=outs[t].at[j], send_sem=send_sems.at[n * j + t],
                    recv_sem=recv_sems.at[n * j + t], device_id=(*chip, c), device_id_type=MESH))
        for cp in cps:
            cp.start()
        for cp in cps:
            cp.wait()

    return pl.pallas_call(
        body, name=name, in_specs=[ANY] * n, out_specs=[ANY] * n,
        out_shape=[jax.ShapeDtypeStruct((3,) + a.shape[1:], a.dtype) for a in arrs],
        scratch_shapes=[pltpu.SemaphoreType.DMA((3 * n,)), pltpu.SemaphoreType.DMA((3 * n,))],
        compiler_params=pltpu.CompilerParams(has_side_effects=True),
    )(*arrs)


def _gather_vec(v):
    n = v.shape[1]

    def body(v_ref, o_ref, send_sems, recv_sems, local_sem):
        x, y, c = _me()
        me = 4 * x + 2 * y + c
        mine = pltpu.make_async_copy(v_ref, o_ref.at[pl.ds(me, 1), :], local_sem)
        mine.start()
        cps = []
        for j in range(1, 8):
            fx, fy, fc = (j >> 2) & 1, (j >> 1) & 1, j & 1
            peer = (x ^ fx, y ^ fy, c ^ fc)
            cps.append(pltpu.make_async_remote_copy(
                src_ref=v_ref, dst_ref=o_ref.at[pl.ds(me, 1), :], send_sem=send_sems.at[j - 1], recv_sem=recv_sems.at[j - 1],
                device_id=peer, device_id_type=MESH))
        for cp in cps:
            cp.start()
        for j in range(1, 8):
            fx, fy, fc = (j >> 2) & 1, (j >> 1) & 1, j & 1
            src = 4 * (x ^ fx) + 2 * (y ^ fy) + (c ^ fc)
            pltpu.make_async_remote_copy(
                src_ref=v_ref, dst_ref=o_ref.at[pl.ds(src, 1), :], send_sem=send_sems.at[j - 1], recv_sem=recv_sems.at[j - 1],
                device_id=(x, y, c), device_id_type=MESH).wait_recv()
        for cp in cps:
            cp.wait_send()
        mine.wait()

    return pl.pallas_call(
        body, name="gather_vec", in_specs=[ANY], out_specs=ANY, out_shape=jax.ShapeDtypeStruct((8, n), v.dtype),
        scratch_shapes=[pltpu.SemaphoreType.DMA((7,)), pltpu.SemaphoreType.DMA((7,)), pltpu.SemaphoreType.DMA],
        compiler_params=pltpu.CompilerParams(has_side_effects=True),
    )(v)


SUM_ROWS = 128


def _pair_sum(g, r, name):
    _, _, hr, C = g.shape
    tr = SUM_ROWS

    def body(c_ref, g_ref, r_ref, o_ref):
        o_ref[0] = (g_ref[0, 0] + r_ref[0]).astype(bf16)

    grid_spec = pltpu.PrefetchScalarGridSpec(
        num_scalar_prefetch=1, grid=(N_SHARD, hr // tr),
        in_specs=[pl.BlockSpec((1, 1, tr, C), lambda s, i, c_ref: (c_ref[0], s, i, 0)),
                  pl.BlockSpec((1, tr, C), lambda s, i, c_ref: (s, i, 0))],
        out_specs=pl.BlockSpec((1, tr, C), lambda s, i, c_ref: (s, i, 0)))
    c = lax.axis_index("c").reshape(1).astype(jnp.int32)
    return pl.pallas_call(body, name=name, grid_spec=grid_spec, out_shape=jax.ShapeDtypeStruct((N_SHARD, hr, C), bf16),
                          compiler_params=_params(("parallel", "parallel")))(c, g, r)


def _own_sum(g, r, got, name):
    _, _, hr, C = g.shape
    tr = SUM_ROWS

    def body(idx_ref, g_ref, r_ref, got_ref, o_ref):
        s = g_ref[0, 0] + r_ref[0]
        o_ref[...] = ((s + got_ref[0].astype(f32)) + got_ref[1].astype(f32)) + got_ref[2].astype(f32)

    grid_spec = pltpu.PrefetchScalarGridSpec(
        num_scalar_prefetch=1, grid=(hr // tr,),
        in_specs=[pl.BlockSpec((1, 1, tr, C), lambda i, idx: (idx[1], idx[0], i, 0)),
                  pl.BlockSpec((1, tr, C), lambda i, idx: (idx[0], i, 0)),
                  pl.BlockSpec((3, tr, C), lambda i, idx: (0, i, 0))],
        out_specs=pl.BlockSpec((tr, C), lambda i, idx: (i, 0)))
    x, y, c = _me()
    idx = jnp.stack([2 * x + y, c]).astype(jnp.int32)
    return pl.pallas_call(body, name=name, grid_spec=grid_spec, out_shape=jax.ShapeDtypeStruct((hr, C), f32),
                          compiler_params=_params(("parallel",)))(idx, g, r, got)


def _share_halves(arrs, name):
    n = len(arrs)

    def body(*refs):
        ins, outs = refs[:n], refs[n:2 * n]
        send_sems, recv_sems, local_sems = refs[2 * n], refs[2 * n + 1], refs[2 * n + 2]
        x, y, c = _me()
        mine = [pltpu.make_async_copy(ins[t], outs[t].at[c], local_sems.at[t]) for t in range(n)]
        sends = [pltpu.make_async_remote_copy(src_ref=ins[t], dst_ref=outs[t].at[c], send_sem=send_sems.at[t],
                                              recv_sem=recv_sems.at[t], device_id=(x, y, 1 - c), device_id_type=MESH)
                 for t in range(n)]
        for cp in mine + sends:
            cp.start()
        for t in range(n):
            pltpu.make_async_remote_copy(src_ref=ins[t], dst_ref=outs[t].at[1 - c], send_sem=send_sems.at[t],
                                         recv_sem=recv_sems.at[t], device_id=(x, y, c), device_id_type=MESH).wait_recv()
        for cp in sends:
            cp.wait_send()
        for cp in mine:
            cp.wait()

    return pl.pallas_call(
        body, name=name, in_specs=[ANY] * n, out_specs=[ANY] * n,
        out_shape=[jax.ShapeDtypeStruct((2,) + a.shape, a.dtype) for a in arrs],
        scratch_shapes=[pltpu.SemaphoreType.DMA((n,)), pltpu.SemaphoreType.DMA((n,)), pltpu.SemaphoreType.DMA((n,))],
        compiler_params=pltpu.CompilerParams(has_side_effects=True),
    )(*arrs)


def _adamw(w, g, m, v, name, tr):
    R, C = w.shape

    def body(w_ref, g_ref, m_ref, v_ref, d_ref, nm_ref, nv_ref):
        gv = g_ref[...]
        mn = ADAM_B1 * m_ref[...] + (1.0 - ADAM_B1) * gv
        vn = ADAM_B2 * v_ref[...] + (1.0 - ADAM_B2) * (gv * gv)
        m_hat = mn / (1.0 - ADAM_B1 ** ADAM_STEP)
        v_hat = vn / (1.0 - ADAM_B2 ** ADAM_STEP)
        d_ref[...] = -ADAM_LR * (m_hat / (jnp.sqrt(v_hat) + ADAM_EPS) + ADAM_WD * w_ref[...])
        nm_ref[...] = mn
        nv_ref[...] = vn

    blk = pl.BlockSpec((tr, C), lambda i: (i, 0))
    o = jax.ShapeDtypeStruct((R, C), f32)
    return pl.pallas_call(body, name=name, grid=(R // tr,), in_specs=[blk] * 4, out_specs=[blk] * 3, out_shape=[o, o, o],
                          compiler_params=_params(("parallel",)))(w, g, m, v)


def _sum8(t):
    n = t.shape[1]

    def body(t_ref, o_ref):
        acc = t_ref[pl.ds(0, 1), :]
        for r in range(1, 8):
            acc = acc + t_ref[pl.ds(r, 1), :]
        o_ref[...] = acc

    return pl.pallas_call(body, name="sum_devices", out_shape=jax.ShapeDtypeStruct((1, n), f32))(t)


SMALL = (("norm1_w", 2048), ("conv_b", 6144), ("dt_bias", 64), ("a_log", 64), ("d_skip", 64), ("ssd_norm_w", 4096), ("norm_f_w", 2048))


def kernel(x, positions, norm1_w, w_in, conv_w, conv_b, dt_bias, a_log, d_skip, ssd_norm_w, w_br_ret, w_br_ssd, w_out, norm_f_w, loss_target, m_norm1_w, m_w_in, m_conv_w, m_conv_b, m_dt_bias, m_a_log, m_d_skip, m_ssd_norm_w, m_w_br_ret, m_w_br_ssd, m_w_out, m_norm_f_w, v_norm1_w, v_w_in, v_conv_w, v_conv_b, v_dt_bias, v_a_log, v_d_skip, v_ssd_norm_w, v_w_br_ret, v_w_br_ssd, v_w_out, v_norm_f_w):
    D = D_MODEL
    xi, yi, ci = _me()
    k = 2 * xi + yi
    weights = dict(norm1_w=norm1_w, w_in=w_in, conv_w=conv_w, conv_b=conv_b, dt_bias=dt_bias, a_log=a_log, d_skip=d_skip,
                   ssd_norm_w=ssd_norm_w, w_br_ret=w_br_ret, w_br_ssd=w_br_ssd, w_out=w_out, norm_f_w=norm_f_w)
    mom1 = dict(norm1_w=m_norm1_w, w_in=m_w_in, conv_w=m_conv_w, conv_b=m_conv_b, dt_bias=m_dt_bias, a_log=m_a_log, d_skip=m_d_skip,
                ssd_norm_w=m_ssd_norm_w, w_br_ret=m_w_br_ret, w_br_ssd=m_w_br_ssd, w_out=m_w_out, norm_f_w=m_norm_f_w)
    mom2 = dict(norm1_w=v_norm1_w, w_in=v_w_in, conv_w=v_conv_w, conv_b=v_conv_b, dt_bias=v_dt_bias, a_log=v_a_log, d_skip=v_d_skip,
                ssd_norm_w=v_ssd_norm_w, w_br_ret=v_w_br_ret, w_br_ssd=v_w_br_ssd, w_out=v_w_out, norm_f_w=v_norm_f_w)

    a_sh = w_in[0].astype(bf16)
    b_sh = jnp.concatenate([w_br_ret[0], w_br_ssd[0], w_out[0]], axis=0).astype(bf16)
    ga, gb, gc = _gather_weights(a_sh, b_sh, conv_w[0])
    w_main, w_dt = _split_w_in(jnp.transpose(ga, (1, 0, 2)).reshape(D, IN_PROJ))
    w_br = gb[:, 0:512].reshape(2048, D)
    w_bs = gb[:, 512:1536].reshape(4096, D)
    w_o = gb[:, 1536:2048].reshape(2048, D)
    conv_full = jnp.transpose(gc, (1, 0, 2)).reshape(SSD_CONV, CONV_DIM)

    loss, grad_x, g = _local_step(x[0], positions[0], loss_target[0], norm1_w, w_main, w_dt, conv_full, conv_b, dt_bias, a_log,
                                  d_skip, ssd_norm_w, w_br, w_bs, w_o, norm_f_w)

    hr = D // 2
    g_in = _w_in_grad_full(g["w_in_main"], g["w_in_dt"])
    g_in = jnp.transpose(g_in.reshape(2, hr, N_SHARD, W_IN_SHARD), (0, 2, 1, 3))
    g_b = jnp.concatenate([g["w_br_ret"].reshape(N_SHARD, 512, D), g["w_br_ssd"].reshape(N_SHARD, 1024, D),
                           g["w_out"].reshape(N_SHARD, 512, D)], axis=1)
    g_b = jnp.transpose(g_b.reshape(N_SHARD, 2, hr, D), (1, 0, 2, 3))
    r_in, r_b = _sibling_swap([g_in, g_b], "sibling_swap")
    p_in, p_b = _pair_sum(g_in, r_in, "pair_sum_in"), _pair_sum(g_b, r_b, "pair_sum_b")
    got_in, got_b = _chip_exchange([p_in, p_b], "chip_exchange")
    mine_in, mine_b = _own_sum(g_in, r_in, got_in, "own_sum_in"), _own_sum(g_b, r_b, got_b, "own_sum_b")
    full_in, full_b = _share_halves([mine_in, mine_b], "share_halves")
    grad_w_in = full_in.reshape(D, W_IN_SHARD)
    full_b = full_b.reshape(D, D)
    grad_mats = dict(w_in=grad_w_in, w_br_ret=full_b[0:512], w_br_ssd=full_b[512:1536], w_out=full_b[1536:2048])

    small = [(n, weights[n].size) for n in ("norm1_w", "conv_b", "dt_bias", "a_log", "d_skip", "ssd_norm_w", "norm_f_w")]
    parts = [jnp.pad(loss.reshape(1, 1), ((0, 0), (0, 127)))] + [g[n].reshape(1, -1) for n, _ in small] + [g["conv_w"].reshape(1, -1)]
    vec = jnp.concatenate(parts, axis=1)
    nv = vec.shape[1]
    nvp = -(-nv // 128) * 128
    total = _sum8(_gather_vec(jnp.pad(vec, ((0, 0), (0, nvp - nv)))))
    loss_out = total[0, 0]
    off = 128
    grad_small = {}
    for n, sz in small:
        grad_small[n] = total[:, off:off + sz]
        off += sz
    g_conv = total[:, off:off + SSD_CONV * CONV_DIM].reshape(SSD_CONV, CONV_DIM)
    g_conv = lax.dynamic_slice_in_dim(g_conv, k * (CONV_DIM // N_SHARD), CONV_DIM // N_SHARD, axis=1)
    grad_small["conv_w"] = g_conv.reshape(1, -1)

    upd = {}
    for n, rows in (("w_in", D), ("w_br_ret", 512), ("w_br_ssd", 1024), ("w_out", 512)):
        upd[n] = _adamw(weights[n][0], grad_mats[n], mom1[n][0], mom2[n][0], "adamw_" + n, SUM_ROWS)
    names_small = [n for n, _ in small] + ["conv_w"]
    flat = lambda d: jnp.concatenate([d[n].reshape(1, -1) for n in names_small], axis=1)
    ns = sum(weights[n].size for n in names_small)
    nsp = -(-ns // 128) * 128
    padv = lambda t: jnp.pad(t, ((0, 0), (0, nsp - ns)))
    ds, ms, vs = _adamw(padv(flat(weights)), padv(flat(grad_small)), padv(flat(mom1)),
                        jnp.pad(flat(mom2), ((0, 0), (0, nsp - ns)), constant_values=1.0), "adamw_small", 1)
    off = 0
    for n in names_small:
        sz = weights[n].size
        upd[n] = tuple(t[:, off:off + sz] for t in (ds, ms, vs))
        off += sz

    order = ["norm1_w", "w_in", "conv_w", "conv_b", "dt_bias", "a_log", "d_skip", "ssd_norm_w", "w_br_ret", "w_br_ssd", "w_out", "norm_f_w"]
    grads_out = {**grad_mats, **grad_small}
    shp = lambda n, t: t.reshape(weights[n].shape)
    return (loss_out, grad_x[None], *[shp(n, grads_out[n]) for n in order], *[shp(n, upd[n][0]) for n in order],
            *[shp(n, upd[n][1]) for n in order], *[shp(n, upd[n][2]) for n in order])
```

```python
import jax
import jax.numpy as jnp
import numpy as np
from jax import lax
from jax.experimental import pallas as pl
from jax.experimental.pallas import tpu as pltpu

f32 = jnp.float32
bf16 = jnp.bfloat16
HIGHEST = lax.Precision.HIGHEST
MESH = pl.DeviceIdType.MESH

D_MODEL = 2048
EPS = 1e-6
CHUNK = 64
RET_HEADS = 8
RET_DK = 256
RET_HW = 4 * RET_DK
ROPE_THETA = 10000.0
SSD_WIDTH = 4096
SSD_GROUPS = 8
SSD_STATE = 128
SSD_GW = 512
SSD_GC = SSD_GW + 2 * SSD_STATE
SSD_HPG = 8
SSD_CONV = 4
CONV_DIM = 6144
SSD_HEADS = 64
LS = 128

C_RET, C_Z, C_GATES, C_XBC = 0, 8192, 12288, 16384
N_MAIN = 22528
DT_OFF = 18432
IN_PROJ = 22592
N_SHARD = 4
W_IN_SHARD = IN_PROJ // N_SHARD

ADAM_LR, ADAM_B1, ADAM_B2, ADAM_EPS, ADAM_WD, ADAM_STEP = 0.001, 0.9, 0.999, 1e-08, 0.01, 10

VMEM_LIMIT = 56 * 1024 * 1024
SUM_ROWS = 128
ANY = pl.BlockSpec(memory_space=pl.ANY)


def _params(dims):
    return pltpu.CompilerParams(dimension_semantics=dims, vmem_limit_bytes=VMEM_LIMIT)


def _silu(x):
    return x * jax.nn.sigmoid(x)


def _dsilu(x):
    s = jax.nn.sigmoid(x)
    return s * (1.0 + x * (1.0 - s))


def _nt(a, b):
    return lax.dot_general(a, b, (((1,), (1,)), ((), ())), preferred_element_type=f32)


def _tn(a, b):
    return lax.dot_general(a, b, (((0,), (0,)), ((), ())), preferred_element_type=f32)


def _nn(a, b):
    return jnp.dot(a, b, preferred_element_type=f32)


def _hi(a, b):
    return jnp.dot(a, b, precision=HIGHEST, preferred_element_type=f32)


def _split(a):
    hi = a.astype(bf16)
    return hi, (a - hi.astype(f32)).astype(bf16)


def _sel_r(a, sel):
    hi, lo = _split(a)
    return _nn(hi, sel) + _nn(lo, sel)


def _sel_l(sel, a):
    hi, lo = _split(a)
    return _nn(sel, hi) + _nn(sel, lo)


def _main_segments():
    segs = []
    for h in range(RET_HEADS):
        for base in (0, 2048, 4096, 6144):
            segs.append((base + RET_DK * h, RET_DK))
    segs.append((8192, SSD_WIDTH))
    segs += [(18496, D_MODEL), (20544, D_MODEL)]
    segs += [(12288 + s, n) for s, n in _xbc_segments()]
    return segs


def _xbc_segments():
    segs = []
    for g in range(SSD_GROUPS):
        segs += [(SSD_GW * g, SSD_GW), (SSD_WIDTH + SSD_STATE * g, SSD_STATE), (SSD_WIDTH + SSD_GROUPS * SSD_STATE + SSD_STATE * g, SSD_STATE)]
    return segs


def _take(t, segs):
    return jnp.concatenate([t[:, s:s + n] for s, n in segs], axis=1)


def _untake(t, segs):
    starts = np.cumsum([0] + [n for _, n in segs[:-1]])
    order = sorted(range(len(segs)), key=lambda i: segs[i][0])
    return jnp.concatenate([t[:, int(starts[i]):int(starts[i]) + segs[i][1]] for i in order], axis=1)


def _split_w_in(w):
    w_dt = jnp.pad(w[:, DT_OFF:DT_OFF + SSD_HEADS], ((0, 0), (0, 128 - SSD_HEADS)))
    return _take(w, _main_segments()), w_dt


def _w_in_cols(g_main, g_dt, lo, hi):
    pieces, ms = [], 0
    for s, n in _main_segments():
        pieces.append((s, n, g_main, ms))
        ms += n
    pieces.append((DT_OFF, SSD_HEADS, g_dt, 0))
    parts = []
    for s, n, arr, ss in sorted(pieces, key=lambda p: p[0]):
        a, b = max(s, lo), min(s + n, hi)
        if a < b:
            parts.append(arr[:, ss + a - s:ss + b - s])
    return jnp.concatenate(parts, axis=1)


def _w_in_grad_full(g_main, g_dt):
    return _w_in_cols(g_main, g_dt, 0, IN_PROJ)


def _mm(pairs, M, N, *, tm, tn, out_dtype, name, tb=False):
    P = len(pairs)
    nks = [K // tk for (_, _, _, _, _, K, tk) in pairs]
    starts = [int(s) for s in np.cumsum([0] + nks[:-1])]
    KT = int(sum(nks))
    in_specs, args = [], []
    for (a, a_cb, b, b_kb, b_nb, K, tk), s, nk in zip(pairs, starts, nks):
        def kk(k, s=s, nk=nk):
            return jnp.clip(k - s, 0, nk - 1)
        in_specs.append(pl.BlockSpec((tm, tk), lambda m, n, k, kk=kk, a_cb=a_cb: (m, a_cb + kk(k))))
        if tb:
            in_specs.append(pl.BlockSpec((tn, tk), lambda m, n, k, kk=kk, b_kb=b_kb, b_nb=b_nb: (b_nb + n, b_kb + kk(k))))
        else:
            in_specs.append(pl.BlockSpec((tk, tn), lambda m, n, k, kk=kk, b_kb=b_kb, b_nb=b_nb: (b_kb + kk(k), b_nb + n)))
        args += [a, b]

    def body(*refs):
        o_ref = refs[2 * P]
        k = pl.program_id(2)

        def prod(i):
            a = refs[2 * i][...].astype(bf16)
            b = refs[2 * i + 1][...].astype(bf16)
            return _nt(a, b) if tb else _nn(a, b)

        if KT == 1:
            o_ref[...] = prod(0).astype(out_dtype)
            return
        acc = refs[2 * P + 1]

        @pl.when(k == 0)
        def _():
            acc[...] = jnp.zeros_like(acc)

        for i in range(P):
            @pl.when((k >= starts[i]) & (k < starts[i] + nks[i]))
            def _(i=i):
                acc[...] += prod(i)

        @pl.when(k == KT - 1)
        def _():
            o_ref[...] = acc[...].astype(out_dtype)

    return pl.pallas_call(
        body, name=name, grid=(M // tm, N // tn, KT), in_specs=in_specs,
        out_specs=pl.BlockSpec((tm, tn), lambda m, n, k: (m, n)),
        out_shape=jax.ShapeDtypeStruct((M, N), out_dtype),
        scratch_shapes=[] if KT == 1 else [pltpu.VMEM((tm, tn), f32)],
        compiler_params=_params(("parallel", "parallel", "arbitrary")),
    )(*args)


def _mm1(a, b, *, tm, tn, tk, out_dtype, name, tb=False):
    M, K = a.shape
    N = b.shape[0] if tb else b.shape[1]
    return _mm([(a, 0, b, 0, 0, K, tk)], M, N, tm=tm, tn=tn, out_dtype=out_dtype, name=name, tb=tb)


def _norm1_fwd(x, w, tr):
    S, D = x.shape

    def body(x_ref, w_ref, h_ref):
        xv = x_ref[...]
        r = lax.rsqrt(jnp.mean(xv * xv, axis=-1, keepdims=True) + EPS)
        h_ref[...] = (xv * r * w_ref[...]).astype(bf16)

    return pl.pallas_call(
        body, name="norm1_fwd", grid=(S // tr,),
        in_specs=[pl.BlockSpec((tr, D), lambda i: (i, 0)), pl.BlockSpec((1, D), lambda i: (0, 0))],
        out_specs=pl.BlockSpec((tr, D), lambda i: (i, 0)),
        out_shape=jax.ShapeDtypeStruct((S, D), bf16), compiler_params=_params(("parallel",)),
    )(x, w)


def _norm1_bwd(x, w, dh, dx2, tr):
    S, D = x.shape

    def body(x_ref, w_ref, dh_ref, dx2_ref, gx_ref, gw_ref):
        @pl.when(pl.program_id(0) == 0)
        def _():
            gw_ref[...] = jnp.zeros_like(gw_ref)

        xv = x_ref[...]
        r = lax.rsqrt(jnp.mean(xv * xv, axis=-1, keepdims=True) + EPS)
        xh = xv * r
        dhv = dh_ref[...]
        gw_ref[...] += jnp.sum(dhv * xh, axis=0, keepdims=True)
        dxh = dhv * w_ref[...]
        gx_ref[...] = dx2_ref[...] + r * (dxh - xh * jnp.mean(dxh * xh, axis=-1, keepdims=True))

    row = pl.BlockSpec((tr, D), lambda i: (i, 0))
    vec = pl.BlockSpec((1, D), lambda i: (0, 0))
    return pl.pallas_call(
        body, name="norm1_bwd", grid=(S // tr,), in_specs=[row, vec, row, row], out_specs=[row, vec],
        out_shape=[jax.ShapeDtypeStruct((S, D), f32), jax.ShapeDtypeStruct((1, D), f32)],
        compiler_params=_params(("arbitrary",)),
    )(x, w, dh, dx2)


def _final_fwd_bwd(x, mo, target, wf, tr):
    S, D = x.shape

    def body(x_ref, mo_ref, t_ref, w_ref, dx2_ref, dx2b_ref, loss_ref, gw_ref):
        @pl.when(pl.program_id(0) == 0)
        def _():
            gw_ref[...] = jnp.zeros_like(gw_ref)
            loss_ref[...] = jnp.zeros_like(loss_ref)

        x2 = x_ref[...] + mo_ref[...]
        r = lax.rsqrt(jnp.mean(x2 * x2, axis=-1, keepdims=True) + EPS)
        xh = x2 * r
        wv = w_ref[...]
        err = xh * wv - t_ref[...]
        loss_ref[...] += 0.5 * jnp.sum(jnp.mean(err * err, axis=-1, keepdims=True), axis=0, keepdims=True)
        dy = err * (1.0 / D)
        gw_ref[...] += jnp.sum(dy * xh, axis=0, keepdims=True)
        dxh = dy * wv
        dx2 = r * (dxh - xh * jnp.mean(dxh * xh, axis=-1, keepdims=True))
        dx2_ref[...] = dx2
        dx2b_ref[...] = dx2.astype(bf16)

    row = pl.BlockSpec((tr, D), lambda i: (i, 0))
    vec = pl.BlockSpec((1, D), lambda i: (0, 0))
    return pl.pallas_call(
        body, name="final_norm_loss", grid=(S // tr,), in_specs=[row, row, row, vec],
        out_specs=[row, row, pl.BlockSpec((1, 1), lambda i: (0, 0)), vec],
        out_shape=[jax.ShapeDtypeStruct((S, D), f32), jax.ShapeDtypeStruct((S, D), bf16), jax.ShapeDtypeStruct((1, 1), f32),
                   jax.ShapeDtypeStruct((1, D), f32)],
        compiler_params=_params(("arbitrary",)),
    )(x, mo, target, wf)


def _merge_fwd(p_r, p_s, proj, tr):
    S, D = p_r.shape

    def body(pr_ref, ps_ref, g_ref, o_ref):
        gr, gs = g_ref[:, pl.ds(0, D)], g_ref[:, pl.ds(D, D)]
        o_ref[...] = (jax.nn.sigmoid(gr) * pr_ref[...] + jax.nn.sigmoid(gs) * ps_ref[...]).astype(bf16)

    row = pl.BlockSpec((tr, D), lambda i: (i, 0))
    return pl.pallas_call(
        body, name="merge_fwd", grid=(S // tr,),
        in_specs=[row, row, pl.BlockSpec((tr, 2 * D), lambda i: (i, C_GATES // (2 * D)))],
        out_specs=row, out_shape=jax.ShapeDtypeStruct((S, D), bf16), compiler_params=_params(("parallel",)),
    )(p_r, p_s, proj)


def _merge_bwd(dm, p_r, p_s, proj, tr):
    S, D = p_r.shape

    def body(dm_ref, pr_ref, ps_ref, g_ref, dpr_ref, dps_ref, dproj_ref):
        dmv = dm_ref[...]
        sr = jax.nn.sigmoid(g_ref[:, pl.ds(0, D)])
        ss = jax.nn.sigmoid(g_ref[:, pl.ds(D, D)])
        dpr_ref[...] = (dmv * sr).astype(bf16)
        dps_ref[...] = (dmv * ss).astype(bf16)
        dproj_ref[:, pl.ds(0, D)] = (dmv * pr_ref[...] * sr * (1.0 - sr)).astype(bf16)
        dproj_ref[:, pl.ds(D, D)] = (dmv * ps_ref[...] * ss * (1.0 - ss)).astype(bf16)

    row = pl.BlockSpec((tr, D), lambda i: (i, 0))
    gates = pl.BlockSpec((tr, 2 * D), lambda i: (i, C_GATES // (2 * D)))
    o = jax.ShapeDtypeStruct((S, D), bf16)
    return pl.pallas_call(
        body, name="merge_bwd", grid=(S // tr,), in_specs=[row, row, row, gates],
        out_specs=[row, row, gates], out_shape=[o, o, jax.ShapeDtypeStruct((S, N_MAIN), bf16)],
        compiler_params=_params(("parallel",)),
    )(dm, p_r, p_s, proj)


def _ssd_norm_fwd(y, proj, w, tr):
    S, W = y.shape

    def body(y_ref, z_ref, w_ref, o_ref):
        u = y_ref[...] * _silu(z_ref[...])
        r = lax.rsqrt(jnp.mean(u * u, axis=-1, keepdims=True) + EPS)
        o_ref[...] = (u * r * w_ref[...]).astype(bf16)

    row = pl.BlockSpec((tr, W), lambda i: (i, 0))
    return pl.pallas_call(
        body, name="ssd_norm_fwd", grid=(S // tr,),
        in_specs=[row, pl.BlockSpec((tr, W), lambda i: (i, C_Z // W)), pl.BlockSpec((1, W), lambda i: (0, 0))],
        out_specs=row, out_shape=jax.ShapeDtypeStruct((S, W), bf16), compiler_params=_params(("parallel",)),
    )(y, proj, w)


def _ssd_norm_bwd(y, proj, w, dys, dproj, tr):
    S, W = y.shape

    def body(y_ref, z_ref, w_ref, d_ref, _, dy_ref, dz_ref, gw_ref):
        @pl.when(pl.program_id(0) == 0)
        def _():
            gw_ref[...] = jnp.zeros_like(gw_ref)

        yv, zv, dv = y_ref[...], z_ref[...], d_ref[...]
        sz = _silu(zv)
        u = yv * sz
        r = lax.rsqrt(jnp.mean(u * u, axis=-1, keepdims=True) + EPS)
        un = u * r
        gw_ref[...] += jnp.sum(dv * un, axis=0, keepdims=True)
        dun = dv * w_ref[...]
        du = r * (dun - un * jnp.mean(dun * un, axis=-1, keepdims=True))
        dy_ref[...] = du * sz
        dz_ref[...] = (du * yv * _dsilu(zv)).astype(bf16)

    row = pl.BlockSpec((tr, W), lambda i: (i, 0))
    zcol = pl.BlockSpec((tr, W), lambda i: (i, C_Z // W))
    vec = pl.BlockSpec((1, W), lambda i: (0, 0))
    return pl.pallas_call(
        body, name="ssd_norm_bwd", grid=(S // tr,),
        in_specs=[row, zcol, vec, row, ANY], out_specs=[row, zcol, vec],
        out_shape=[jax.ShapeDtypeStruct((S, W), f32), jax.ShapeDtypeStruct(dproj.shape, bf16), jax.ShapeDtypeStruct((1, W), f32)],
        input_output_aliases={4: 1}, compiler_params=_params(("arbitrary",)),
    )(y, proj, w, dys, dproj)


def _rope(t, cos, sin):
    t1, t2 = t[:, :128], t[:, 128:]
    return jnp.concatenate([t1 * cos - t2 * sin, t2 * cos + t1 * sin], axis=1)


def _rope_t(d, cos, sin):
    d1, d2 = d[:, :128], d[:, 128:]
    return jnp.concatenate([d1 * cos + d2 * sin, d2 * cos - d1 * sin], axis=1)


def _ret_specs(tb, rev_nb=None):
    def blk(i):
        return i if rev_nb is None else rev_nb - 1 - i
    head = pl.BlockSpec((tb, RET_HW), lambda h, i: (blk(i), h))
    tab = pl.BlockSpec((tb, 128), lambda h, i: (blk(i), 0))
    mat = pl.BlockSpec((1, CHUNK, CHUNK), lambda h, i: (h, 0, 0))
    vec = pl.BlockSpec((1, CHUNK, 1), lambda h, i: (h, 0, 0))
    one = pl.BlockSpec((1, 1, 1), lambda h, i: (h, 0, 0))
    own = pl.BlockSpec((tb, RET_DK), lambda h, i: (blk(i), h))
    st = pl.BlockSpec((1, tb // CHUNK, RET_DK, RET_DK), lambda h, i: (h, blk(i), 0, 0))
    return head, tab, mat, vec, one, own, st


def _ret_fwd(proj, cos, sin, intra, qdec, kdec, cdec, tb):
    S = proj.shape[0]
    nc = S // CHUNK
    scale = RET_DK ** -0.5
    dk = RET_DK

    def body(p_ref, cos_ref, sin_ref, m_ref, qd_ref, kd_ref, cd_ref, y_ref, yr_ref, st_ref, st):
        @pl.when(pl.program_id(1) == 0)
        def _():
            st[...] = jnp.zeros_like(st)

        mm, qd, kd, cd = m_ref[0], qd_ref[0], kd_ref[0], cd_ref[0]

        def chunk(c, carry):
            rows = pl.ds(pl.multiple_of(c * CHUNK, CHUNK), CHUNK)
            cs, sn = cos_ref[rows, :], sin_ref[rows, :]
            qr = _rope(p_ref[rows, pl.ds(0, dk)], cs, sn)
            kr = _rope(p_ref[rows, pl.ds(dk, dk)], cs, sn) * scale
            qb, kb, vb = qr.astype(bf16), kr.astype(bf16), p_ref[rows, pl.ds(2 * dk, dk)].astype(bf16)
            stb = st[...].astype(bf16)
            st_ref[0, c] = stb
            sc = (_nt(qb, kb) * mm).astype(bf16)
            y = _nn(sc, vb) + _nn(qb, stb) * qd
            st[...] = st[...] * cd + _tn((kr * kd).astype(bf16), vb)
            y_ref[rows, :] = y
            mu = jnp.mean(y, axis=-1, keepdims=True)
            yc = y - mu
            var = jnp.mean(yc * yc, axis=-1, keepdims=True)
            yr_ref[rows, :] = (yc * lax.rsqrt(var + EPS) * _silu(p_ref[rows, pl.ds(3 * dk, dk)])).astype(bf16)
            return carry

        lax.fori_loop(0, tb // CHUNK, chunk, 0, unroll=2)

    head, tab, mat, vec, one, own, stspec = _ret_specs(tb)
    return pl.pallas_call(
        body, name="ret_fwd", grid=(RET_HEADS, S // tb),
        in_specs=[head, tab, tab, mat, vec, vec, one], out_specs=[own, own, stspec],
        out_shape=[jax.ShapeDtypeStruct((S, 2048), f32), jax.ShapeDtypeStruct((S, 2048), bf16),
                   jax.ShapeDtypeStruct((RET_HEADS, nc, dk, dk), bf16)],
        scratch_shapes=[pltpu.VMEM((dk, dk), f32)], compiler_params=_params(("parallel", "arbitrary")),
    )(proj, cos, sin, intra, qdec, kdec, cdec)


def _ret_bwd(proj, cos, sin, intra, qdec, kdec, cdec, y, dyr, states, dproj, tb):
    S = proj.shape[0]
    nb = S // tb
    nck = tb // CHUNK
    scale = RET_DK ** -0.5
    dk = RET_DK

    def body(p_ref, cos_ref, sin_ref, m_ref, qd_ref, kd_ref, cd_ref, y_ref, dyr_ref, st_ref, _, o_ref, dst):
        @pl.when(pl.program_id(1) == 0)
        def _():
            dst[...] = jnp.zeros_like(dst)

        mm, qd, kd, cd = m_ref[0], qd_ref[0], kd_ref[0], cd_ref[0]

        def chunk(cc, carry):
            c = nck - 1 - cc
            rows = pl.ds(pl.multiple_of(c * CHUNK, CHUNK), CHUNK)
            cs, sn = cos_ref[rows, :], sin_ref[rows, :]
            qr = _rope(p_ref[rows, pl.ds(0, dk)], cs, sn)
            kr = _rope(p_ref[rows, pl.ds(dk, dk)], cs, sn) * scale
            qb, kb, vb = qr.astype(bf16), kr.astype(bf16), p_ref[rows, pl.ds(2 * dk, dk)].astype(bf16)
            kdb = (kr * kd).astype(bf16)
            stb = st_ref[0, c]
            yv, gv, dyrv = y_ref[rows, :], p_ref[rows, pl.ds(3 * dk, dk)], dyr_ref[rows, :]
            mu = jnp.mean(yv, axis=-1, keepdims=True)
            yc = yv - mu
            rstd = lax.rsqrt(jnp.mean(yc * yc, axis=-1, keepdims=True) + EPS)
            yn = yc * rstd
            o_ref[rows, pl.ds(3 * dk, dk)] = (dyrv * yn * _dsilu(gv)).astype(bf16)
            dyn = dyrv * _silu(gv)
            dy = rstd * (dyn - jnp.mean(dyn, axis=-1, keepdims=True) - yn * jnp.mean(dyn * yn, axis=-1, keepdims=True))
            dyb = dy.astype(bf16)
            dyqb = (dy * qd).astype(bf16)
            dstb = dst[...].astype(bf16)
            sct = (_nt(kb, qb) * mm).astype(bf16)
            ds = (_nt(dyb, vb) * mm).astype(bf16)
            dsT = (_nt(vb, dyb) * mm).astype(bf16)
            dv = _nn(sct, dyb) + _nn(kdb, dstb)
            dqr = _nn(ds, kb) + _nt(dyqb, stb)
            dkr = _nn(dsT, qb) + _nt(vb, dstb) * kd
            dst[...] = dst[...] * cd + _tn(qb, dyqb)
            o_ref[rows, pl.ds(0, dk)] = _rope_t(dqr, cs, sn).astype(bf16)
            o_ref[rows, pl.ds(dk, dk)] = (_rope_t(dkr, cs, sn) * scale).astype(bf16)
            o_ref[rows, pl.ds(2 * dk, dk)] = dv.astype(bf16)
            return carry

        lax.fori_loop(0, nck, chunk, 0, unroll=2)

    head, tab, mat, vec, one, own, stspec = _ret_specs(tb, rev_nb=nb)
    return pl.pallas_call(
        body, name="ret_bwd", grid=(RET_HEADS, nb),
        in_specs=[head, tab, tab, mat, vec, vec, one, own, own, stspec, ANY],
        out_specs=head, out_shape=jax.ShapeDtypeStruct(dproj.shape, bf16), input_output_aliases={10: 0},
        scratch_shapes=[pltpu.VMEM((dk, dk), f32)], compiler_params=_params(("parallel", "arbitrary")),
    )(proj, cos, sin, intra, qdec, kdec, cdec, y, dyr, states, dproj)


def _conv_fwd(proj, conv_w, conv_b, tb, cw):
    S = proj.shape[0]
    off = C_XBC // cw

    def body(x_ref, halo_ref, w_ref, b_ref, o_ref, xe):
        xe[pl.ds(0, 8), :] = jnp.where(pl.program_id(1) == 0, 0.0, halo_ref[...])
        xe[pl.ds(8, tb), :] = x_ref[...]
        acc = b_ref[...] + w_ref[pl.ds(0, 1), :] * xe[pl.ds(5, tb), :]
        for j in range(1, SSD_CONV):
            acc = acc + w_ref[pl.ds(j, 1), :] * xe[pl.ds(5 + j, tb), :]
        o_ref[...] = acc

    return pl.pallas_call(
        body, name="conv_fwd", grid=(CONV_DIM // cw, S // tb),
        in_specs=[pl.BlockSpec((tb, cw), lambda j, i: (i, off + j)),
                  pl.BlockSpec((8, cw), lambda j, i: (jnp.maximum(i * (tb // 8) - 1, 0), off + j)),
                  pl.BlockSpec((SSD_CONV, cw), lambda j, i: (0, j)), pl.BlockSpec((1, cw), lambda j, i: (0, j))],
        out_specs=pl.BlockSpec((tb, cw), lambda j, i: (i, j)),
        out_shape=jax.ShapeDtypeStruct((S, CONV_DIM), f32),
        scratch_shapes=[pltpu.VMEM((tb + 8, cw), f32)], compiler_params=_params(("parallel", "arbitrary")),
    )(proj, proj, conv_w, conv_b)


def _conv_bwd(dpre, proj, conv_w, dproj, tb, cw):
    S, n = dpre.shape
    nb = S // tb
    xoff = C_XBC // cw

    def body(d_ref, dh_ref, x_ref, xh_ref, w_ref, _, dx_ref, gw_ref, gb_ref, de, xe):
        i = pl.program_id(1)

        @pl.when(i == 0)
        def _():
            gw_ref[...] = jnp.zeros_like(gw_ref)
            gb_ref[...] = jnp.zeros_like(gb_ref)

        dv = d_ref[...]
        de[pl.ds(0, tb), :] = dv
        de[pl.ds(tb, 8), :] = jnp.where(i == nb - 1, 0.0, dh_ref[...])
        xe[pl.ds(0, 8), :] = jnp.where(i == 0, 0.0, xh_ref[...])
        xe[pl.ds(8, tb), :] = x_ref[...]
        acc = w_ref[pl.ds(SSD_CONV - 1, 1), :] * dv
        for j in range(SSD_CONV - 1):
            acc = acc + w_ref[pl.ds(j, 1), :] * de[pl.ds(3 - j, tb), :]
        dx_ref[...] = acc.astype(bf16)
        gb_ref[...] += jnp.sum(dv, axis=0, keepdims=True)
        for j in range(SSD_CONV):
            gw_ref[pl.ds(j, 1), :] += jnp.sum(dv * xe[pl.ds(5 + j, tb), :], axis=0, keepdims=True)

    return pl.pallas_call(
        body, name="conv_bwd", grid=(n // cw, nb),
        in_specs=[pl.BlockSpec((tb, cw), lambda j, i: (i, j)),
                  pl.BlockSpec((8, cw), lambda j, i: (jnp.minimum((i + 1) * (tb // 8), S // 8 - 1), j)),
                  pl.BlockSpec((tb, cw), lambda j, i: (i, xoff + j)),
                  pl.BlockSpec((8, cw), lambda j, i: (jnp.maximum(i * (tb // 8) - 1, 0), xoff + j)),
                  pl.BlockSpec((SSD_CONV, cw), lambda j, i: (0, j)), ANY],
        out_specs=[pl.BlockSpec((tb, cw), lambda j, i: (i, xoff + j)), pl.BlockSpec((SSD_CONV, cw), lambda j, i: (0, j)),
                   pl.BlockSpec((1, cw), lambda j, i: (0, j))],
        out_shape=[jax.ShapeDtypeStruct(dproj.shape, bf16), jax.ShapeDtypeStruct((SSD_CONV, n), f32), jax.ShapeDtypeStruct((1, n), f32)],
        input_output_aliases={5: 0},
        scratch_shapes=[pltpu.VMEM((tb + 8, cw), f32), pltpu.VMEM((tb + 8, cw), f32)],
        compiler_params=_params(("parallel", "arbitrary")),
    )(dpre, dpre, proj, proj, conv_w, dproj)


def _dt_prep(dt_raw, dt_bias, a_log, tb):
    S = dt_raw.shape[0]

    def body(r_ref, b_ref, al_ref, dt_ref, sg_ref, ac_ref):
        li = lax.broadcasted_iota(jnp.int32, (LS, LS), 0)
        si = lax.broadcasted_iota(jnp.int32, (LS, LS), 1)
        tri = (li >= si).astype(f32)
        neg_a = -jnp.exp(al_ref[...])
        for c in range(tb // LS):
            rows = pl.ds(c * LS, LS)
            xv = r_ref[rows, :] + b_ref[...]
            dtv = jax.nn.softplus(xv)
            dt_ref[rows, :] = dtv
            sg_ref[rows, :] = jax.nn.sigmoid(xv)
            ac_ref[rows, :] = _hi(tri, dtv * neg_a)

    row = pl.BlockSpec((tb, 128), lambda i: (i, 0))
    vec = pl.BlockSpec((1, 128), lambda i: (0, 0))
    o = jax.ShapeDtypeStruct((S, 128), f32)
    return pl.pallas_call(body, name="dt_prep", grid=(S // tb,), in_specs=[row, vec, vec], out_specs=[row, row, row],
                          out_shape=[o, o, o], compiler_params=_params(("parallel",)))(dt_raw, dt_bias, a_log)


def _group_major(t):
    S = t.shape[0]
    return jnp.transpose(t[:, :SSD_HEADS].reshape(S, SSD_GROUPS, SSD_HPG), (1, 0, 2))


def _group_major_t(t):
    S = t.shape[0]
    return jnp.transpose(t[:, :SSD_HEADS].reshape(S // LS, LS, SSD_GROUPS, SSD_HPG), (2, 0, 3, 1))


def _ssd_specs(tb, rev_nb=None):
    def blk(i):
        return i if rev_nb is None else rev_nb - 1 - i
    grp = pl.BlockSpec((tb, SSD_GC), lambda g, i: (blk(i), g))
    xs = pl.BlockSpec((tb, SSD_GW), lambda g, i: (blk(i), g))
    ph = pl.BlockSpec((1, tb, SSD_HPG), lambda g, i: (g, blk(i), 0))
    pht = pl.BlockSpec((1, tb // LS, SSD_HPG, LS), lambda g, i: (g, blk(i), 0, 0))
    gvec = pl.BlockSpec((1, 1, SSD_GW), lambda g, i: (g, 0, 0))
    ex = pl.BlockSpec((SSD_HPG, SSD_GW), lambda g, i: (0, 0))
    st = pl.BlockSpec((1, tb // LS, SSD_STATE, SSD_GW), lambda g, i: (g, blk(i), 0, 0))
    return grp, xs, ph, pht, gvec, ex, st


def _expander():
    return jnp.repeat(jnp.eye(SSD_HPG, dtype=f32), SSD_GW // SSD_HPG, axis=1).astype(bf16)


def _expand3(dt8, ac8, ex):
    stack = jnp.concatenate([dt8, jnp.exp(ac8), jnp.exp(ac8[LS - 1:LS, :] - ac8)], axis=0)
    wide = _sel_r(stack, ex)
    return wide[0:LS], wide[LS:2 * LS], wide[2 * LS:3 * LS]


def _ssd_fwd(pre, dt_g, ac_g, act_g, dskx, tb):
    S = pre.shape[0]
    nc = S // LS
    hd = SSD_GW // SSD_HPG

    def body(p_ref, dt_ref, ac_ref, act_ref, dsk_ref, ex_ref, y_ref, st_ref, st):
        @pl.when(pl.program_id(1) == 0)
        def _():
            st[...] = jnp.zeros_like(st)

        ex = ex_ref[...]
        li = lax.broadcasted_iota(jnp.int32, (LS, LS), 0)
        si = lax.broadcasted_iota(jnp.int32, (LS, LS), 1)
        causal = li >= si

        def chunk(c, carry):
            rows = pl.ds(pl.multiple_of(c * LS, LS), LS)
            xs = _silu(p_ref[rows, pl.ds(0, SSD_GW)])
            bcb = _silu(p_ref[rows, pl.ds(SSD_GW, SSD_STATE)]).astype(bf16)
            ccb = _silu(p_ref[rows, pl.ds(SSD_GW + SSD_STATE, SSD_STATE)]).astype(bf16)
            dt8, ac8, act = dt_ref[0, rows, :], ac_ref[0, rows, :], act_ref[0, c]
            dtx, eax, tailx = _expand3(dt8, ac8, ex)
            xdt = xs * dtx
            cb = _nt(ccb, bcb)
            stb = st[...].astype(bf16)
            st_ref[0, c] = stb
            xdtb = xdt.astype(bf16)
            outs = []
            for h in range(SSD_HPG):
                dec = jnp.exp(jnp.where(causal, ac8[:, h:h + 1] - act[h:h + 1, :], -1e30))
                outs.append(_nn((cb * dec).astype(bf16), xdtb[:, hd * h:hd * (h + 1)]))
            y_ref[rows, :] = jnp.concatenate(outs, axis=1) + _nn(ccb, stb) * eax + dsk_ref[0] * xs
            st[...] = st[...] * eax[LS - 1:LS, :] + _tn(bcb, (xdt * tailx).astype(bf16))
            return carry

        lax.fori_loop(0, tb // LS, chunk, 0)

    grp, xs, ph, pht, gvec, ex, stspec = _ssd_specs(tb)
    return pl.pallas_call(
        body, name="ssd_fwd", grid=(SSD_GROUPS, S // tb),
        in_specs=[grp, ph, ph, pht, gvec, ex], out_specs=[xs, stspec],
        out_shape=[jax.ShapeDtypeStruct((S, SSD_WIDTH), f32), jax.ShapeDtypeStruct((SSD_GROUPS, nc, SSD_STATE, SSD_GW), bf16)],
        scratch_shapes=[pltpu.VMEM((SSD_STATE, SSD_GW), f32)], compiler_params=_params(("parallel", "arbitrary")),
    )(pre, dt_g, ac_g, act_g, dskx, _expander())


def _ssd_bwd(pre, dt_g, ac_g, act_g, sg_g, dskx, nega_g, dy, states, tb):
    S = pre.shape[0]
    nb = S // tb
    nck = tb // LS
    hd = SSD_GW // SSD_HPG

    def body(p_ref, dt_ref, ac_ref, act_ref, sg_ref, dsk_ref, na_ref, ex_ref, ext_ref, dy_ref, st_ref,
             dp_ref, ddt_ref, gsk_ref, gal_ref, gdb_ref, dst, skacc):
        @pl.when(pl.program_id(1) == 0)
        def _():
            dst[...] = jnp.zeros_like(dst)
            skacc[...] = jnp.zeros_like(skacc)
            gal_ref[...] = jnp.zeros_like(gal_ref)
            gdb_ref[...] = jnp.zeros_like(gdb_ref)

        ex, ext = ex_ref[...], ext_ref[...]
        li = lax.broadcasted_iota(jnp.int32, (LS, LS), 0)
        si = lax.broadcasted_iota(jnp.int32, (LS, LS), 1)
        causal = li >= si
        anti = si >= li
        upper = anti.astype(bf16)
        last_row = (lax.broadcasted_iota(jnp.int32, (LS, 1), 0) == LS - 1).astype(f32)
        head_id = lax.broadcasted_iota(jnp.int32, (1, SSD_HPG), 1)
        neg_a = na_ref[0]
        dskv = dsk_ref[0]

        def chunk(cc, carry):
            c = nck - 1 - cc
            rows = pl.ds(pl.multiple_of(c * LS, LS), LS)
            px = p_ref[rows, pl.ds(0, SSD_GW)]
            pb = p_ref[rows, pl.ds(SSD_GW, SSD_STATE)]
            pc = p_ref[rows, pl.ds(SSD_GW + SSD_STATE, SSD_STATE)]
            xs = _silu(px)
            bcb = _silu(pb).astype(bf16)
            ccb = _silu(pc).astype(bf16)
            dt8, ac8, act = dt_ref[0, rows, :], ac_ref[0, rows, :], act_ref[0, c]
            dtx, eax, tailx = _expand3(dt8, ac8, ex)
            xdt = xs * dtx
            ex_last = eax[LS - 1:LS, :]
            stb = st_ref[0, c]
            dyv = dy_ref[rows, :]
            dyb = dyv.astype(bf16)
            xdtb = xdt.astype(bf16)
            skacc[...] += jnp.sum(dyv * xs, axis=0, keepdims=True)
            yinter = _nn(ccb, stb) * eax
            dzb = (dyv * eax).astype(bf16)
            dcc = _nt(dzb, stb)
            dstv = dst[...]
            dstb = dstv.astype(bf16)
            xt = xdt * tailx
            dxt = _nn(bcb, dstb)
            dbc = _nt(xt.astype(bf16), dstb)
            dxdt = dxt * tailx
            lastrow = jnp.sum(dxt * xt, axis=0, keepdims=True) + jnp.sum(dstv * stb.astype(f32), axis=0, keepdims=True) * ex_last
            dst[...] = dstv * ex_last + _tn(ccb, dzb)
            cb = _nt(ccb, bcb)
            cbt = _nt(bcb, ccb)
            dcb = jnp.zeros((LS, LS), f32)
            dcbt = jnp.zeros((LS, LS), f32)
            dac8 = jnp.zeros((LS, SSD_HPG), f32)
            dxin = []
            for h in range(SSD_HPG):
                sl = slice(hd * h, hd * (h + 1))
                col, rowv = ac8[:, h:h + 1], act[h:h + 1, :]
                dec = jnp.exp(jnp.where(causal, col - rowv, -1e30))
                dect = jnp.exp(jnp.where(anti, rowv - col, -1e30))
                gm, gmt = cb * dec, cbt * dect
                dgm, dgmt = _nt(dyb[:, sl], xdtb[:, sl]), _nt(xdtb[:, sl], dyb[:, sl])
                dxin.append(_nn(gmt.astype(bf16), dyb[:, sl]))
                dcb = dcb + dgm * dec
                dcbt = dcbt + dgmt * dect
                dcol = jnp.sum(dgm * gm, axis=1, keepdims=True) - jnp.sum(dgmt * gmt, axis=1, keepdims=True)
                dac8 = dac8 + dcol * (head_id == h).astype(f32)
            dxintra = jnp.concatenate(dxin, axis=1)
            dcc = dcc + _nn(dcb.astype(bf16), bcb)
            dbc = dbc + _nn(dcbt.astype(bf16), ccb)
            dxdt = dxdt + dxintra
            dacx = dyv * yinter - dxt * xt + last_row * lastrow
            red = _sel_r(jnp.concatenate([dacx, dxdt * xs], axis=0), ext)
            dac8 = dac8 + red[0:LS]
            da8 = _sel_l(upper, dac8)
            ddt8 = red[LS:2 * LS] + da8 * neg_a
            gal_ref[0] += jnp.sum(da8 * dt8 * neg_a, axis=0, keepdims=True)
            ddr = ddt8 * sg_ref[0, rows, :]
            ddt_ref[0, rows, :] = ddr
            gdb_ref[0] += jnp.sum(ddr, axis=0, keepdims=True)
            dp_ref[rows, pl.ds(0, SSD_GW)] = (dskv * dyv + dxdt * dtx) * _dsilu(px)
            dp_ref[rows, pl.ds(SSD_GW, SSD_STATE)] = dbc * _dsilu(pb)
            dp_ref[rows, pl.ds(SSD_GW + SSD_STATE, SSD_STATE)] = dcc * _dsilu(pc)
            return carry

        lax.fori_loop(0, nck, chunk, 0)

        @pl.when(pl.program_id(1) == nb - 1)
        def _():
            gsk_ref[0] = skacc[...]

    grp, xs, ph, pht, gvec, ex, stspec = _ssd_specs(tb, rev_nb=nb)
    small = pl.BlockSpec((1, 1, SSD_HPG), lambda g, i: (g, 0, 0))
    ext = pl.BlockSpec((SSD_GW, SSD_HPG), lambda g, i: (0, 0))
    sm = jax.ShapeDtypeStruct((SSD_GROUPS, 1, SSD_HPG), f32)
    expander = _expander()
    return pl.pallas_call(
        body, name="ssd_bwd", grid=(SSD_GROUPS, nb),
        in_specs=[grp, ph, ph, pht, ph, gvec, small, ex, ext, xs, stspec],
        out_specs=[grp, ph, gvec, small, small],
        out_shape=[jax.ShapeDtypeStruct((S, CONV_DIM), f32), jax.ShapeDtypeStruct((SSD_GROUPS, S, SSD_HPG), f32),
                   jax.ShapeDtypeStruct((SSD_GROUPS, 1, SSD_GW), f32), sm, sm],
        scratch_shapes=[pltpu.VMEM((SSD_STATE, SSD_GW), f32), pltpu.VMEM((1, SSD_GW), f32)],
        compiler_params=_params(("parallel", "arbitrary")),
    )(pre, dt_g, ac_g, act_g, sg_g, dskx, nega_g, expander, expander.T, dy, states)


def _tiles(S):
    return dict(tb=min(512, S), tr=min(256, S), tm=min(1024, S))


def _local_step(x, positions, target, norm1_w, w_main, w_dt, conv_w, conv_b, dt_bias, a_log, d_skip, ssd_norm_w,
                w_br, w_bs, w_o, norm_f_w):
    S, D = x.shape
    t = _tiles(S)
    tb, tr, tm = t["tb"], t["tr"], t["tm"]

    half = RET_DK // 2
    inv_freq = ROPE_THETA ** (-jnp.arange(half, dtype=f32) / half)
    ang = positions.astype(f32)[:, None] * inv_freq
    cos, sin = jnp.cos(ang), jnp.sin(ang)
    log_gamma = jnp.log1p(-(2.0 ** (-5.0 - jnp.arange(RET_HEADS, dtype=f32))))
    idx = jnp.arange(CHUNK, dtype=f32)
    intra = jnp.exp(jnp.abs(idx[:, None] - idx[None, :]) * log_gamma[:, None, None])
    qdec = jnp.exp((idx + 1.0)[None, :] * log_gamma[:, None])[:, :, None]
    kdec = jnp.exp((CHUNK - 1.0 - idx)[None, :] * log_gamma[:, None])[:, :, None]
    cdec = jnp.exp(CHUNK * log_gamma)[:, None, None]
    conv_wm, conv_bm = _take(conv_w, _xbc_segments()), _take(conv_b, _xbc_segments())

    h = _norm1_fwd(x, norm1_w, tr)
    proj = _mm1(h, w_main, tm=tm, tn=1024, tk=D, out_dtype=f32, name="proj_main")
    dt_raw = _mm1(h, w_dt, tm=tm, tn=128, tk=D, out_dtype=f32, name="proj_dt")
    y_ret, yr, ret_states = _ret_fwd(proj, cos, sin, intra, qdec, kdec, cdec, tb)
    pre = _conv_fwd(proj, conv_wm, conv_bm, tb, 512)
    pad64 = lambda v: jnp.pad(v, ((0, 0), (0, 128 - SSD_HEADS)))
    dt, sg, ac = _dt_prep(dt_raw, pad64(dt_bias), pad64(a_log), tb)
    dt_g, ac_g, sg_g, act_g = _group_major(dt), _group_major(ac), _group_major(sg), _group_major_t(ac)
    dskx = jnp.repeat(d_skip.reshape(SSD_GROUPS, 1, SSD_HPG), SSD_GW // SSD_HPG, axis=2)
    nega_g = (-jnp.exp(a_log)).reshape(SSD_GROUPS, 1, SSD_HPG)
    y_ssd, ssd_states = _ssd_fwd(pre, dt_g, ac_g, act_g, dskx, tb)
    ys = _ssd_norm_fwd(y_ssd, proj, ssd_norm_w, tr // 2)
    p_r = _mm1(yr, w_br, tm=tm, tn=1024, tk=2048, out_dtype=f32, name="branch_ret")
    p_s = _mm1(ys, w_bs, tm=tm, tn=1024, tk=2048, out_dtype=f32, name="branch_ssd")
    merged = _merge_fwd(p_r, p_s, proj, tr)
    mo = _mm1(merged, w_o, tm=tm, tn=1024, tk=2048, out_dtype=f32, name="out_proj")
    dx2, dx2b, loss, g_norm_f = _final_fwd_bwd(x, mo, target, norm_f_w.reshape(1, D), tr)

    tkt = min(1024, S)
    wg = lambda at, b, name, tn=1024: _mm1(at, b, tm=min(1024, at.shape[0]), tn=tn, tk=tkt, out_dtype=f32, name=name)
    dm = _mm1(dx2b, w_o, tm=tm, tn=1024, tk=2048, out_dtype=f32, name="d_merged", tb=True)
    g_w_o = wg(merged.T, dx2b, "g_w_out")
    dp_r, dp_s, dproj = _merge_bwd(dm, p_r, p_s, proj, tr)
    dyr = _mm1(dp_r, w_br, tm=tm, tn=1024, tk=2048, out_dtype=f32, name="d_yr", tb=True)
    dys = _mm1(dp_s, w_bs, tm=tm, tn=1024, tk=2048, out_dtype=f32, name="d_ys", tb=True)
    g_w_br = wg(yr.T, dp_r, "g_w_br_ret")
    g_w_bs = wg(ys.T, dp_s, "g_w_br_ssd")
    dy_ssd, dproj, g_ssd_norm = _ssd_norm_bwd(y_ssd, proj, ssd_norm_w, dys, dproj, tr // 2)
    dproj = _ret_bwd(proj, cos, sin, intra, qdec, kdec, cdec, y_ret, dyr, ret_states, dproj, tb)
    dpre, ddt_g, gsk, gal, gdb = _ssd_bwd(pre, dt_g, ac_g, act_g, sg_g, dskx, nega_g, dy_ssd, ssd_states, tb)
    dproj, gcw, gcb = _conv_bwd(dpre, proj, conv_wm, dproj, tb, 512)
    ddt = jnp.transpose(ddt_g, (1, 0, 2)).reshape(S, SSD_HEADS)
    ddt_p = jnp.pad(ddt, ((0, 0), (0, 128 - SSD_HEADS))).astype(bf16)

    dh = _mm([(dproj, 0, w_main, 0, 0, N_MAIN, 2048), (ddt_p, 0, w_dt, 0, 0, 128, 128)], S, D, tm=tm, tn=1024,
             out_dtype=f32, name="d_h", tb=True)
    ht = h.T
    g_main = wg(ht, dproj, "g_w_in_main")
    g_dt = wg(ht, ddt_p, "g_w_in_dt", tn=128)
    grad_x, g_norm1 = _norm1_bwd(x, norm1_w, dh, dx2, tr)

    seg = lambda v: jnp.sum(v.reshape(SSD_HEADS, SSD_GW // SSD_HPG), axis=1).reshape(1, SSD_HEADS)
    grads = dict(
        norm1_w=g_norm1, w_in_main=g_main, w_in_dt=g_dt,
        conv_w=_untake(gcw, _xbc_segments()), conv_b=_untake(gcb, _xbc_segments()),
        dt_bias=gdb.reshape(1, SSD_HEADS), a_log=gal.reshape(1, SSD_HEADS), d_skip=seg(gsk),
        ssd_norm_w=g_ssd_norm, w_br_ret=g_w_br, w_br_ssd=g_w_bs, w_out=g_w_o, norm_f_w=g_norm_f,
    )
    return loss, grad_x, grads


def _me():
    return lax.axis_index("x"), lax.axis_index("y"), lax.axis_index("c")


def _other_chips(x, y):
    return [(1 - x, y), (x, 1 - y), (1 - x, 1 - y)]


def _gather_weights(a, b, cw):
    R = a.shape[0]
    hr = R // 2

    def body(a_ref, b_ref, cw_ref, ga_ref, gb_ref, gc_ref, send_sems, recv_sems):
        x, y, c = _me()
        k = 2 * x + y
        sibling = (x, y, 1 - c)
        chips = _other_chips(x, y)

        def small(j, src_shard, to):
            return pltpu.make_async_remote_copy(
                src_ref=cw_ref, dst_ref=gc_ref.at[src_shard], send_sem=send_sems.at[12 + j], recv_sem=recv_sems.at[12 + j],
                device_id=to, device_id_type=MESH)

        def copies(j, src_shard, half, to, from_input):
            rows = pl.ds(half * hr, hr)
            out = []
            for t, (inp, g) in enumerate(((a_ref, ga_ref), (b_ref, gb_ref))):
                src = inp.at[rows, :] if from_input else g.at[src_shard, rows, :]
                out.append(pltpu.make_async_remote_copy(
                    src_ref=src, dst_ref=g.at[src_shard, rows, :], send_sem=send_sems.at[2 * j + t],
                    recv_sem=recv_sems.at[2 * j + t], device_id=to, device_id_type=MESH))
            return out

        first = []
        for j, chip in enumerate(chips):
            first += copies(j, k, c, (*chip, c), True)
            first.append(small(j, k, (*chip, c)))
        for cp in first:
            cp.start()
        passed = []
        for j, chip in enumerate(chips):
            kk = 2 * chip[0] + chip[1]
            for cp in copies(j, kk, c, (x, y, c), False):
                cp.wait_recv()
            fw = copies(3 + j, kk, c, sibling, False)
            for cp in fw:
                cp.start()
            passed += fw
        for j, chip in enumerate(chips):
            kk = 2 * chip[0] + chip[1]
            for cp in copies(3 + j, kk, 1 - c, (x, y, c), False):
                cp.wait_recv()
            small(j, kk, (x, y, c)).wait_recv()
        for cp in first + passed:
            cp.wait_send()

    return pl.pallas_call(
        body, name="gather_weights", in_specs=[ANY, ANY, ANY], out_specs=[ANY, ANY, ANY],
        out_shape=[jax.ShapeDtypeStruct((N_SHARD,) + a.shape, a.dtype), jax.ShapeDtypeStruct((N_SHARD,) + b.shape, b.dtype),
                   jax.ShapeDtypeStruct((N_SHARD,) + cw.shape, cw.dtype)],
        scratch_shapes=[pltpu.SemaphoreType.DMA((15,)), pltpu.SemaphoreType.DMA((15,))],
        compiler_params=pltpu.CompilerParams(has_side_effects=True),
    )(a, b, cw)


def _sibling_swap(arrs, name):
    n = len(arrs)

    def body(*refs):
        ins, outs = refs[:n], refs[n:2 * n]
        send_sems, recv_sems = refs[2 * n], refs[2 * n + 1]
        x, y, c = _me()
        cps = [pltpu.make_async_remote_copy(src_ref=ins[t].at[s, 1 - c], dst_ref=outs[t].at[s], send_sem=send_sems.at[N_SHARD * t + s],
                                            recv_sem=recv_sems.at[N_SHARD * t + s], device_id=(x, y, 1 - c), device_id_type=MESH)
               for t in range(n) for s in range(N_SHARD)]
        for cp in cps:
            cp.start()
        for cp in cps:
            cp.wait()

    return pl.pallas_call(
        body, name=name, in_specs=[ANY] * n, out_specs=[ANY] * n,
        out_shape=[jax.ShapeDtypeStruct((N_SHARD,) + a.shape[2:], a.dtype) for a in arrs],
        scratch_shapes=[pltpu.SemaphoreType.DMA((N_SHARD * n,)), pltpu.SemaphoreType.DMA((N_SHARD * n,))],
        compiler_params=pltpu.CompilerParams(has_side_effects=True),
    )(*arrs)


def _chip_exchange(arrs, name):
    n = len(arrs)

    def body(*refs):
        ins, outs = refs[:n], refs[n:2 * n]
        send_sems, recv_sems = refs[2 * n], refs[2 * n + 1]
        x, y, c = _me()
        cps = []
        for j, chip in enumerate(_other_chips(x, y)):
            kk = 2 * chip[0] + chip[1]
            for t in range(n):
                cps.append(pltpu.make_async_remote_copy(
                    src_ref=ins[t].at[kk], dst_ref=outs[t].at[j], send_sem=send_sems.at[n * j + t],
                    recv_sem=recv_sems.at[n * j + t], device_id=(*chip, c), device_id_type=MESH))
        for cp in cps:
            cp.start()
        for cp in cps:
            cp.wait()

    return pl.pallas_call(
        body, name=name, in_specs=[ANY] * n, out_specs=[ANY] * n,
        out_shape=[jax.ShapeDtypeStruct((3,) + a.shape[1:], a.dtype) for a in arrs],
        scratch_shapes=[pltpu.SemaphoreType.DMA((3 * n,)), pltpu.SemaphoreType.DMA((3 * n,))],
        compiler_params=pltpu.CompilerParams(has_side_effects=True),
    )(*arrs)


def _send_to_sibling(arrs, name):
    n = len(arrs)

    def body(*refs):
        ins, outs = refs[:n], refs[n:2 * n]
        send_sems, recv_sems = refs[2 * n], refs[2 * n + 1]
        x, y, c = _me()
        cps = [pltpu.make_async_remote_copy(src_ref=ins[t], dst_ref=outs[t], send_sem=send_sems.at[t], recv_sem=recv_sems.at[t],
                                            device_id=(x, y, 1 - c), device_id_type=MESH) for t in range(n)]
        for cp in cps:
            cp.start()
        for cp in cps:
            cp.wait()

    return pl.pallas_call(
        body, name=name, in_specs=[ANY] * n, out_specs=[ANY] * n,
        out_shape=[jax.ShapeDtypeStruct(a.shape, a.dtype) for a in arrs],
        scratch_shapes=[pltpu.SemaphoreType.DMA((n,)), pltpu.SemaphoreType.DMA((n,))],
        compiler_params=pltpu.CompilerParams(has_side_effects=True),
    )(*arrs)


def _gather_vec(v):
    n = v.shape[1]

    def body(v_ref, o_ref, send_sems, recv_sems):
        x, y, c = _me()
        me = 4 * x + 2 * y + c
        cps = []
        for j in range(1, 8):
            fx, fy, fc = (j >> 2) & 1, (j >> 1) & 1, j & 1
            peer = (x ^ fx, y ^ fy, c ^ fc)
            cps.append(pltpu.make_async_remote_copy(
                src_ref=v_ref, dst_ref=o_ref.at[pl.ds(me, 1), :], send_sem=send_sems.at[j - 1], recv_sem=recv_sems.at[j - 1],
                device_id=peer, device_id_type=MESH))
        for cp in cps:
            cp.start()
        for j in range(1, 8):
            fx, fy, fc = (j >> 2) & 1, (j >> 1) & 1, j & 1
            src = 4 * (x ^ fx) + 2 * (y ^ fy) + (c ^ fc)
            pltpu.make_async_remote_copy(
                src_ref=v_ref, dst_ref=o_ref.at[pl.ds(src, 1), :], send_sem=send_sems.at[j - 1], recv_sem=recv_sems.at[j - 1],
                device_id=(x, y, c), device_id_type=MESH).wait_recv()
        for cp in cps:
            cp.wait_send()

    return pl.pallas_call(
        body, name="gather_vec", in_specs=[ANY], out_specs=ANY, out_shape=jax.ShapeDtypeStruct((8, n), v.dtype),
        scratch_shapes=[pltpu.SemaphoreType.DMA((7,)), pltpu.SemaphoreType.DMA((7,))],
        compiler_params=pltpu.CompilerParams(has_side_effects=True),
    )(v)


def _pair_sum(g, r, name):
    _, _, hr, C = g.shape
    tr = SUM_ROWS

    def body(c_ref, g_ref, r_ref, o_ref):
        o_ref[0] = (g_ref[0, 0] + r_ref[0]).astype(bf16)

    grid_spec = pltpu.PrefetchScalarGridSpec(
        num_scalar_prefetch=1, grid=(N_SHARD, hr // tr),
        in_specs=[pl.BlockSpec((1, 1, tr, C), lambda s, i, c_ref: (s, c_ref[0], i, 0)),
                  pl.BlockSpec((1, tr, C), lambda s, i, c_ref: (s, i, 0))],
        out_specs=pl.BlockSpec((1, tr, C), lambda s, i, c_ref: (s, i, 0)))
    c = lax.axis_index("c").reshape(1).astype(jnp.int32)
    return pl.pallas_call(body, name=name, grid_spec=grid_spec, out_shape=jax.ShapeDtypeStruct((N_SHARD, hr, C), bf16),
                          compiler_params=_params(("parallel", "parallel")))(c, g, r)


def _own_sum(g, r, got, name):
    _, _, hr, C = g.shape
    tr = SUM_ROWS

    def body(idx_ref, g_ref, r_ref, got_ref, o_ref):
        s = g_ref[0, 0] + r_ref[0]
        o_ref[...] = ((s + got_ref[0].astype(f32)) + got_ref[1].astype(f32)) + got_ref[2].astype(f32)

    grid_spec = pltpu.PrefetchScalarGridSpec(
        num_scalar_prefetch=1, grid=(hr // tr,),
        in_specs=[pl.BlockSpec((1, 1, tr, C), lambda i, idx: (idx[0], idx[1], i, 0)),
                  pl.BlockSpec((1, tr, C), lambda i, idx: (idx[0], i, 0)),
                  pl.BlockSpec((3, tr, C), lambda i, idx: (0, i, 0))],
        out_specs=pl.BlockSpec((tr, C), lambda i, idx: (i, 0)))
    x, y, c = _me()
    idx = jnp.stack([2 * x + y, c]).astype(jnp.int32)
    return pl.pallas_call(body, name=name, grid_spec=grid_spec, out_shape=jax.ShapeDtypeStruct((hr, C), f32),
                          compiler_params=_params(("parallel",)))(idx, g, r, got)


def _adamw(w, g, m, v, name, tr):
    R, C = w.shape

    def body(w_ref, g_ref, m_ref, v_ref, d_ref, nm_ref, nv_ref):
        gv = g_ref[...]
        mn = ADAM_B1 * m_ref[...] + (1.0 - ADAM_B1) * gv
        vn = ADAM_B2 * v_ref[...] + (1.0 - ADAM_B2) * (gv * gv)
        m_hat = mn / (1.0 - ADAM_B1 ** ADAM_STEP)
        v_hat = vn / (1.0 - ADAM_B2 ** ADAM_STEP)
        d_ref[...] = -ADAM_LR * (m_hat / (jnp.sqrt(v_hat) + ADAM_EPS) + ADAM_WD * w_ref[...])
        nm_ref[...] = mn
        nv_ref[...] = vn

    blk = pl.BlockSpec((tr, C), lambda i: (i, 0))
    o = jax.ShapeDtypeStruct((R, C), f32)
    return pl.pallas_call(body, name=name, grid=(R // tr,), in_specs=[blk] * 4, out_specs=[blk] * 3, out_shape=[o, o, o],
                          compiler_params=_params(("parallel",)))(w, g, m, v)


def _sum8(t):
    n = t.shape[1]

    def body(t_ref, o_ref):
        acc = t_ref[pl.ds(0, 1), :]
        for r in range(1, 8):
            acc = acc + t_ref[pl.ds(r, 1), :]
        o_ref[...] = acc

    return pl.pallas_call(body, name="sum_devices", out_shape=jax.ShapeDtypeStruct((1, n), f32))(t)


def _reduce_to_owner(g_in, g_b):
    xi, yi, ci = _me()
    hr = g_in.shape[1] // 2
    g_in = g_in.reshape(N_SHARD, 2, hr, g_in.shape[2])
    g_b = g_b.reshape(N_SHARD, 2, hr, g_b.shape[2])
    r_in, r_b = _sibling_swap([g_in, g_b], "sibling_swap")
    p_in, p_b = _pair_sum(g_in, r_in, "pair_sum_in"), _pair_sum(g_b, r_b, "pair_sum_b")
    got_in, got_b = _chip_exchange([p_in, p_b], "chip_exchange")
    mine_in, mine_b = _own_sum(g_in, r_in, got_in, "own_sum_in"), _own_sum(g_b, r_b, got_b, "own_sum_b")
    other_in, other_b = _send_to_sibling([mine_in, mine_b], "share_halves")
    both = lambda mine, other: jnp.where(ci == 0, jnp.concatenate([mine, other], axis=0), jnp.concatenate([other, mine], axis=0))
    return both(mine_in, other_in), both(mine_b, other_b)


def kernel(x, positions, norm1_w, w_in, conv_w, conv_b, dt_bias, a_log, d_skip, ssd_norm_w, w_br_ret, w_br_ssd, w_out, norm_f_w, loss_target, m_norm1_w, m_w_in, m_conv_w, m_conv_b, m_dt_bias, m_a_log, m_d_skip, m_ssd_norm_w, m_w_br_ret, m_w_br_ssd, m_w_out, m_norm_f_w, v_norm1_w, v_w_in, v_conv_w, v_conv_b, v_dt_bias, v_a_log, v_d_skip, v_ssd_norm_w, v_w_br_ret, v_w_br_ssd, v_w_out, v_norm_f_w):
    D = D_MODEL
    xi, yi, ci = _me()
    k = 2 * xi + yi
    me = 2 * k + ci
    weights = dict(norm1_w=norm1_w, w_in=w_in, conv_w=conv_w, conv_b=conv_b, dt_bias=dt_bias, a_log=a_log, d_skip=d_skip,
                   ssd_norm_w=ssd_norm_w, w_br_ret=w_br_ret, w_br_ssd=w_br_ssd, w_out=w_out, norm_f_w=norm_f_w)
    mom1 = dict(norm1_w=m_norm1_w, w_in=m_w_in, conv_w=m_conv_w, conv_b=m_conv_b, dt_bias=m_dt_bias, a_log=m_a_log, d_skip=m_d_skip,
                ssd_norm_w=m_ssd_norm_w, w_br_ret=m_w_br_ret, w_br_ssd=m_w_br_ssd, w_out=m_w_out, norm_f_w=m_norm_f_w)
    mom2 = dict(norm1_w=v_norm1_w, w_in=v_w_in, conv_w=v_conv_w, conv_b=v_conv_b, dt_bias=v_dt_bias, a_log=v_a_log, d_skip=v_d_skip,
                ssd_norm_w=v_ssd_norm_w, w_br_ret=v_w_br_ret, w_br_ssd=v_w_br_ssd, w_out=v_w_out, norm_f_w=v_norm_f_w)

    a_sh = w_in[0].astype(bf16)
    b_sh = jnp.concatenate([w_br_ret[0], w_br_ssd[0], w_out[0]], axis=0).astype(bf16)
    ga, gb, gc = _gather_weights(a_sh, b_sh, conv_w[0])
    own = lambda g, s: lax.dynamic_update_slice_in_dim(g, s[None], k, axis=0)
    ga, gb, gc = own(ga, a_sh), own(gb, b_sh), own(gc, conv_w[0])
    w_main, w_dt = _split_w_in(jnp.transpose(ga, (1, 0, 2)).reshape(D, IN_PROJ))
    w_br = gb[:, 0:512].reshape(2048, D)
    w_bs = gb[:, 512:1536].reshape(4096, D)
    w_o = gb[:, 1536:2048].reshape(2048, D)
    conv_full = jnp.transpose(gc, (1, 0, 2)).reshape(SSD_CONV, CONV_DIM)

    loss, grad_x, g = _local_step(x[0], positions[0], loss_target[0], norm1_w, w_main, w_dt, conv_full, conv_b, dt_bias, a_log,
                                  d_skip, ssd_norm_w, w_br, w_bs, w_o, norm_f_w)

    g_in = jnp.stack([_w_in_cols(g["w_in_main"], g["w_in_dt"], W_IN_SHARD * s, W_IN_SHARD * (s + 1)) for s in range(N_SHARD)])
    g_b = jnp.concatenate([g["w_br_ret"].reshape(N_SHARD, 512, D), g["w_br_ssd"].reshape(N_SHARD, 1024, D),
                           g["w_out"].reshape(N_SHARD, 512, D)], axis=1)
    grad_w_in, full_b = _reduce_to_owner(g_in, g_b)
    grad_mats = dict(w_in=grad_w_in, w_br_ret=full_b[0:512], w_br_ssd=full_b[512:1536], w_out=full_b[1536:2048])

    small = [(n, weights[n].size) for n in ("norm1_w", "conv_b", "dt_bias", "a_log", "d_skip", "ssd_norm_w", "norm_f_w")]
    parts = [jnp.pad(loss.reshape(1, 1), ((0, 0), (0, 127)))] + [g[n].reshape(1, -1) for n, _ in small] + [g["conv_w"].reshape(1, -1)]
    vec = jnp.concatenate(parts, axis=1)
    nv = vec.shape[1]
    nvp = -(-nv // 128) * 128
    vec = jnp.pad(vec, ((0, 0), (0, nvp - nv)))
    total = _sum8(lax.dynamic_update_slice_in_dim(_gather_vec(vec), vec, me, axis=0))
    loss_out = total[0, 0]
    off = 128
    grad_small = {}
    for n, sz in small:
        grad_small[n] = total[:, off:off + sz]
        off += sz
    g_conv = total[:, off:off + SSD_CONV * CONV_DIM].reshape(SSD_CONV, CONV_DIM)
    g_conv = lax.dynamic_slice_in_dim(g_conv, k * (CONV_DIM // N_SHARD), CONV_DIM // N_SHARD, axis=1)
    grad_small["conv_w"] = g_conv.reshape(1, -1)

    upd = {}
    for n in ("w_in", "w_br_ret", "w_br_ssd", "w_out"):
        upd[n] = _adamw(weights[n][0], grad_mats[n], mom1[n][0], mom2[n][0], "adamw_" + n, SUM_ROWS)
    names_small = [n for n, _ in small] + ["conv_w"]
    flat = lambda d: jnp.concatenate([d[n].reshape(1, -1) for n in names_small], axis=1)
    ns = sum(weights[n].size for n in names_small)
    nsp = -(-ns // 128) * 128
    padv = lambda t: jnp.pad(t, ((0, 0), (0, nsp - ns)))
    ds, ms, vs = _adamw(padv(flat(weights)), padv(flat(grad_small)), padv(flat(mom1)),
                        jnp.pad(flat(mom2), ((0, 0), (0, nsp - ns)), constant_values=1.0), "adamw_small", 1)
    off = 0
    for n in names_small:
        sz = weights[n].size
        upd[n] = tuple(t[:, off:off + sz] for t in (ds, ms, vs))
        off += sz

    order = ["norm1_w", "w_in", "conv_w", "conv_b", "dt_bias", "a_log", "d_skip", "ssd_norm_w", "w_br_ret", "w_br_ssd", "w_out", "norm_f_w"]
    grads_out = {**grad_mats, **grad_small}
    shp = lambda n, t: t.reshape(weights[n].shape)
    return (loss_out, grad_x[None], *[shp(n, grads_out[n]) for n in order], *[shp(n, upd[n][0]) for n in order],
            *[shp(n, upd[n][1]) for n in order], *[shp(n, upd[n][2]) for n in order])
```

```python
import jax
import jax.numpy as jnp
import numpy as np
from jax import lax
from jax.experimental import pallas as pl
from jax.experimental.pallas import tpu as pltpu

f32 = jnp.float32
bf16 = jnp.bfloat16
HIGHEST = lax.Precision.HIGHEST
MESH = pl.DeviceIdType.MESH

D_MODEL = 2048
EPS = 1e-6
CHUNK = 64
RET_HEADS = 8
RET_DK = 256
RET_HW = 4 * RET_DK
ROPE_THETA = 10000.0
SSD_WIDTH = 4096
SSD_GROUPS = 8
SSD_STATE = 128
SSD_GW = 512
SSD_GC = SSD_GW + 2 * SSD_STATE
SSD_HPG = 8
SSD_CONV = 4
CONV_DIM = 6144
SSD_HEADS = 64
LS = 128

C_RET, C_Z, C_GATES, C_XBC = 0, 8192, 12288, 16384
N_MAIN = 22528
DT_OFF = 18432
IN_PROJ = 22592
N_SHARD = 4
W_IN_SHARD = IN_PROJ // N_SHARD

ADAM_LR, ADAM_B1, ADAM_B2, ADAM_EPS, ADAM_WD, ADAM_STEP = 0.001, 0.9, 0.999, 1e-08, 0.01, 10

VMEM_LIMIT = 56 * 1024 * 1024
SUM_ROWS = 128
ANY = pl.BlockSpec(memory_space=pl.ANY)


def _params(dims):
    return pltpu.CompilerParams(dimension_semantics=dims, vmem_limit_bytes=VMEM_LIMIT)


def _silu(x):
    return x * jax.nn.sigmoid(x)


def _dsilu(x):
    s = jax.nn.sigmoid(x)
    return s * (1.0 + x * (1.0 - s))


def _nt(a, b):
    return lax.dot_general(a, b, (((1,), (1,)), ((), ())), preferred_element_type=f32)


def _tn(a, b):
    return lax.dot_general(a, b, (((0,), (0,)), ((), ())), preferred_element_type=f32)


def _nn(a, b):
    return jnp.dot(a, b, preferred_element_type=f32)


def _hi(a, b):
    return jnp.dot(a, b, precision=HIGHEST, preferred_element_type=f32)


def _split(a):
    hi = a.astype(bf16)
    return hi, (a - hi.astype(f32)).astype(bf16)


def _sel_r(a, sel):
    hi, lo = _split(a)
    return _nn(hi, sel) + _nn(lo, sel)


def _sel_l(sel, a):
    hi, lo = _split(a)
    return _nn(sel, hi) + _nn(sel, lo)


def _xbc_group_major(t):
    R = t.shape[0]
    nb = SSD_GROUPS * SSD_STATE
    parts = [t[:, :SSD_WIDTH].reshape(R, SSD_GROUPS, SSD_GW), t[:, SSD_WIDTH:SSD_WIDTH + nb].reshape(R, SSD_GROUPS, SSD_STATE),
             t[:, SSD_WIDTH + nb:].reshape(R, SSD_GROUPS, SSD_STATE)]
    return jnp.concatenate(parts, axis=2).reshape(R, CONV_DIM)


def _xbc_original(t):
    R = t.shape[0]
    g = t.reshape(R, SSD_GROUPS, SSD_GC)
    parts = [g[:, :, :SSD_GW].reshape(R, SSD_WIDTH), g[:, :, SSD_GW:SSD_GW + SSD_STATE].reshape(R, SSD_GROUPS * SSD_STATE),
             g[:, :, SSD_GW + SSD_STATE:].reshape(R, SSD_GROUPS * SSD_STATE)]
    return jnp.concatenate(parts, axis=1)


def _split_w_in(w):
    D = w.shape[0]
    ret = jnp.transpose(w[:, :4 * 2048].reshape(D, 4, RET_HEADS, RET_DK), (0, 2, 1, 3)).reshape(D, 4 * 2048)
    w_dt = jnp.pad(w[:, DT_OFF:DT_OFF + SSD_HEADS], ((0, 0), (0, 128 - SSD_HEADS)))
    main = jnp.concatenate([ret, w[:, 8192:12288], w[:, DT_OFF + SSD_HEADS:], _xbc_group_major(w[:, 12288:DT_OFF])], axis=1)
    return main, w_dt


def _w_in_grad_full(g_main, g_dt):
    D = g_main.shape[0]
    ret = jnp.transpose(g_main[:, :C_Z].reshape(D, RET_HEADS, 4, RET_DK), (0, 2, 1, 3)).reshape(D, C_Z)
    return jnp.concatenate([ret, g_main[:, C_Z:C_GATES], _xbc_original(g_main[:, C_XBC:]), g_dt[:, :SSD_HEADS],
                            g_main[:, C_GATES:C_XBC]], axis=1)


def _mm(pairs, M, N, *, tm, tn, out_dtype, name, tb=False):
    P = len(pairs)
    nks = [K // tk for (_, _, _, _, _, K, tk) in pairs]
    starts = [int(s) for s in np.cumsum([0] + nks[:-1])]
    KT = int(sum(nks))
    in_specs, args = [], []
    for (a, a_cb, b, b_kb, b_nb, K, tk), s, nk in zip(pairs, starts, nks):
        def kk(k, s=s, nk=nk):
            return jnp.clip(k - s, 0, nk - 1)
        in_specs.append(pl.BlockSpec((tm, tk), lambda m, n, k, kk=kk, a_cb=a_cb: (m, a_cb + kk(k))))
        if tb:
            in_specs.append(pl.BlockSpec((tn, tk), lambda m, n, k, kk=kk, b_kb=b_kb, b_nb=b_nb: (b_nb + n, b_kb + kk(k))))
        else:
            in_specs.append(pl.BlockSpec((tk, tn), lambda m, n, k, kk=kk, b_kb=b_kb, b_nb=b_nb: (b_kb + kk(k), b_nb + n)))
        args += [a, b]

    def body(*refs):
        o_ref = refs[2 * P]
        k = pl.program_id(2)

        def prod(i):
            a = refs[2 * i][...].astype(bf16)
            b = refs[2 * i + 1][...].astype(bf16)
            return _nt(a, b) if tb else _nn(a, b)

        if KT == 1:
            o_ref[...] = prod(0).astype(out_dtype)
            return
        acc = refs[2 * P + 1]

        @pl.when(k == 0)
        def _():
            acc[...] = jnp.zeros_like(acc)

        for i in range(P):
            @pl.when((k >= starts[i]) & (k < starts[i] + nks[i]))
            def _(i=i):
                acc[...] += prod(i)

        @pl.when(k == KT - 1)
        def _():
            o_ref[...] = acc[...].astype(out_dtype)

    return pl.pallas_call(
        body, name=name, grid=(M // tm, N // tn, KT), in_specs=in_specs,
        out_specs=pl.BlockSpec((tm, tn), lambda m, n, k: (m, n)),
        out_shape=jax.ShapeDtypeStruct((M, N), out_dtype),
        scratch_shapes=[] if KT == 1 else [pltpu.VMEM((tm, tn), f32)],
        compiler_params=_params(("parallel", "parallel", "arbitrary")),
    )(*args)


def _mm1(a, b, *, tm, tn, tk, out_dtype, name, tb=False):
    M, K = a.shape
    N = b.shape[0] if tb else b.shape[1]
    return _mm([(a, 0, b, 0, 0, K, tk)], M, N, tm=tm, tn=tn, out_dtype=out_dtype, name=name, tb=tb)


RS = 16
CS = 32


def _for_strips(n_rows, fn, rs=RS, unroll=4):
    def step(s, carry):
        fn(pl.ds(pl.multiple_of(s * rs, rs), rs))
        return carry
    n = n_rows // rs
    lax.fori_loop(0, n, step, 0, unroll=min(unroll, n))


def _norm1_fwd(x, w, tr):
    S, D = x.shape

    def body(x_ref, w_ref, h_ref):
        def strip(rows):
            xv = x_ref[rows, :]
            r = lax.rsqrt(jnp.mean(xv * xv, axis=-1, keepdims=True) + EPS)
            h_ref[rows, :] = (xv * r * w_ref[...]).astype(bf16)
        _for_strips(tr, strip)

    return pl.pallas_call(
        body, name="norm1_fwd", grid=(S // tr,),
        in_specs=[pl.BlockSpec((tr, D), lambda i: (i, 0)), pl.BlockSpec((1, D), lambda i: (0, 0))],
        out_specs=pl.BlockSpec((tr, D), lambda i: (i, 0)),
        out_shape=jax.ShapeDtypeStruct((S, D), bf16), compiler_params=_params(("parallel",)),
    )(x, w)


def _norm1_bwd(x, w, dh, dx2, tr):
    S, D = x.shape

    def body(x_ref, w_ref, dh_ref, dx2_ref, gx_ref, gw_ref, acc):
        @pl.when(pl.program_id(0) == 0)
        def _():
            acc[...] = jnp.zeros_like(acc)

        def strip(rows):
            xv = x_ref[rows, :]
            r = lax.rsqrt(jnp.mean(xv * xv, axis=-1, keepdims=True) + EPS)
            xh = xv * r
            dhv = dh_ref[rows, :]
            acc[...] += dhv * xh
            dxh = dhv * w_ref[...]
            gx_ref[rows, :] = dx2_ref[rows, :] + r * (dxh - xh * jnp.mean(dxh * xh, axis=-1, keepdims=True))
        _for_strips(tr, strip)

        @pl.when(pl.program_id(0) == S // tr - 1)
        def _():
            gw_ref[...] = jnp.sum(acc[...], axis=0, keepdims=True)

    row = pl.BlockSpec((tr, D), lambda i: (i, 0))
    vec = pl.BlockSpec((1, D), lambda i: (0, 0))
    return pl.pallas_call(
        body, name="norm1_bwd", grid=(S // tr,), in_specs=[row, vec, row, row], out_specs=[row, vec],
        out_shape=[jax.ShapeDtypeStruct((S, D), f32), jax.ShapeDtypeStruct((1, D), f32)],
        scratch_shapes=[pltpu.VMEM((RS, D), f32)], compiler_params=_params(("arbitrary",)),
    )(x, w, dh, dx2)


def _final_fwd_bwd(x, mo, target, wf, tr):
    S, D = x.shape

    def body(x_ref, mo_ref, t_ref, w_ref, dx2_ref, dx2b_ref, loss_ref, gw_ref, acc, lacc):
        @pl.when(pl.program_id(0) == 0)
        def _():
            acc[...] = jnp.zeros_like(acc)
            lacc[...] = jnp.zeros_like(lacc)

        def strip(rows):
            x2 = x_ref[rows, :] + mo_ref[rows, :]
            r = lax.rsqrt(jnp.mean(x2 * x2, axis=-1, keepdims=True) + EPS)
            xh = x2 * r
            wv = w_ref[...]
            err = xh * wv - t_ref[rows, :]
            lacc[...] += jnp.mean(err * err, axis=-1, keepdims=True)
            dy = err * (1.0 / D)
            acc[...] += dy * xh
            dxh = dy * wv
            dx2 = r * (dxh - xh * jnp.mean(dxh * xh, axis=-1, keepdims=True))
            dx2_ref[rows, :] = dx2
            dx2b_ref[rows, :] = dx2.astype(bf16)
        _for_strips(tr, strip)

        @pl.when(pl.program_id(0) == S // tr - 1)
        def _():
            gw_ref[...] = jnp.sum(acc[...], axis=0, keepdims=True)
            loss_ref[...] = 0.5 * jnp.sum(lacc[...], axis=0, keepdims=True)

    row = pl.BlockSpec((tr, D), lambda i: (i, 0))
    vec = pl.BlockSpec((1, D), lambda i: (0, 0))
    return pl.pallas_call(
        body, name="final_norm_loss", grid=(S // tr,), in_specs=[row, row, row, vec],
        out_specs=[row, row, pl.BlockSpec((1, 1), lambda i: (0, 0)), vec],
        out_shape=[jax.ShapeDtypeStruct((S, D), f32), jax.ShapeDtypeStruct((S, D), bf16), jax.ShapeDtypeStruct((1, 1), f32),
                   jax.ShapeDtypeStruct((1, D), f32)],
        scratch_shapes=[pltpu.VMEM((RS, D), f32), pltpu.VMEM((RS, 1), f32)], compiler_params=_params(("arbitrary",)),
    )(x, mo, target, wf)


def _merge_fwd(p_r, p_s, proj, tr):
    S, D = p_r.shape

    def body(pr_ref, ps_ref, g_ref, o_ref):
        def strip(rows):
            gr, gs = g_ref[rows, pl.ds(0, D)], g_ref[rows, pl.ds(D, D)]
            o_ref[rows, :] = (jax.nn.sigmoid(gr) * pr_ref[rows, :] + jax.nn.sigmoid(gs) * ps_ref[rows, :]).astype(bf16)
        _for_strips(tr, strip)

    row = pl.BlockSpec((tr, D), lambda i: (i, 0))
    return pl.pallas_call(
        body, name="merge_fwd", grid=(S // tr,),
        in_specs=[row, row, pl.BlockSpec((tr, 2 * D), lambda i: (i, C_GATES // (2 * D)))],
        out_specs=row, out_shape=jax.ShapeDtypeStruct((S, D), bf16), compiler_params=_params(("parallel",)),
    )(p_r, p_s, proj)


def _merge_bwd(dm, p_r, p_s, proj, tr):
    S, D = p_r.shape

    def body(dm_ref, pr_ref, ps_ref, g_ref, dpr_ref, dps_ref, dproj_ref):
        def strip(rows):
            dmv = dm_ref[rows, :]
            sr = jax.nn.sigmoid(g_ref[rows, pl.ds(0, D)])
            ss = jax.nn.sigmoid(g_ref[rows, pl.ds(D, D)])
            dpr_ref[rows, :] = (dmv * sr).astype(bf16)
            dps_ref[rows, :] = (dmv * ss).astype(bf16)
            dproj_ref[rows, pl.ds(0, D)] = (dmv * pr_ref[rows, :] * sr * (1.0 - sr)).astype(bf16)
            dproj_ref[rows, pl.ds(D, D)] = (dmv * ps_ref[rows, :] * ss * (1.0 - ss)).astype(bf16)
        _for_strips(tr, strip)

    row = pl.BlockSpec((tr, D), lambda i: (i, 0))
    gates = pl.BlockSpec((tr, 2 * D), lambda i: (i, C_GATES // (2 * D)))
    o = jax.ShapeDtypeStruct((S, D), bf16)
    return pl.pallas_call(
        body, name="merge_bwd", grid=(S // tr,), in_specs=[row, row, row, gates],
        out_specs=[row, row, gates], out_shape=[o, o, jax.ShapeDtypeStruct((S, N_MAIN), bf16)],
        compiler_params=_params(("parallel",)),
    )(dm, p_r, p_s, proj)


def _ssd_norm_fwd(y, proj, w, tr):
    S, W = y.shape

    def body(y_ref, z_ref, w_ref, o_ref):
        def strip(rows):
            u = y_ref[rows, :] * _silu(z_ref[rows, :])
            r = lax.rsqrt(jnp.mean(u * u, axis=-1, keepdims=True) + EPS)
            o_ref[rows, :] = (u * r * w_ref[...]).astype(bf16)
        _for_strips(tr, strip)

    row = pl.BlockSpec((tr, W), lambda i: (i, 0))
    return pl.pallas_call(
        body, name="ssd_norm_fwd", grid=(S // tr,),
        in_specs=[row, pl.BlockSpec((tr, W), lambda i: (i, C_Z // W)), pl.BlockSpec((1, W), lambda i: (0, 0))],
        out_specs=row, out_shape=jax.ShapeDtypeStruct((S, W), bf16), compiler_params=_params(("parallel",)),
    )(y, proj, w)


def _ssd_norm_bwd(y, proj, w, dys, dproj, tr):
    S, W = y.shape

    def body(y_ref, z_ref, w_ref, d_ref, _, dy_ref, dz_ref, gw_ref, acc):
        @pl.when(pl.program_id(0) == 0)
        def _():
            acc[...] = jnp.zeros_like(acc)

        def strip(rows):
            yv, zv, dv = y_ref[rows, :], z_ref[rows, :], d_ref[rows, :]
            sz = _silu(zv)
            u = yv * sz
            r = lax.rsqrt(jnp.mean(u * u, axis=-1, keepdims=True) + EPS)
            un = u * r
            acc[...] += dv * un
            dun = dv * w_ref[...]
            du = r * (dun - un * jnp.mean(dun * un, axis=-1, keepdims=True))
            dy_ref[rows, :] = du * sz
            dz_ref[rows, :] = (du * yv * _dsilu(zv)).astype(bf16)
        _for_strips(tr, strip)

        @pl.when(pl.program_id(0) == S // tr - 1)
        def _():
            gw_ref[...] = jnp.sum(acc[...], axis=0, keepdims=True)

    row = pl.BlockSpec((tr, W), lambda i: (i, 0))
    zcol = pl.BlockSpec((tr, W), lambda i: (i, C_Z // W))
    vec = pl.BlockSpec((1, W), lambda i: (0, 0))
    return pl.pallas_call(
        body, name="ssd_norm_bwd", grid=(S // tr,),
        in_specs=[row, zcol, vec, row, ANY], out_specs=[row, zcol, vec],
        out_shape=[jax.ShapeDtypeStruct((S, W), f32), jax.ShapeDtypeStruct(dproj.shape, bf16), jax.ShapeDtypeStruct((1, W), f32)],
        input_output_aliases={4: 1}, scratch_shapes=[pltpu.VMEM((RS, W), f32)], compiler_params=_params(("arbitrary",)),
    )(y, proj, w, dys, dproj)


def _rope(t, cos, sin):
    t1, t2 = t[:, :128], t[:, 128:]
    return jnp.concatenate([t1 * cos - t2 * sin, t2 * cos + t1 * sin], axis=1)


def _rope_t(d, cos, sin):
    d1, d2 = d[:, :128], d[:, 128:]
    return jnp.concatenate([d1 * cos + d2 * sin, d2 * cos - d1 * sin], axis=1)


def _ret_specs(tb, rev_nb=None):
    def blk(i):
        return i if rev_nb is None else rev_nb - 1 - i
    head = pl.BlockSpec((tb, RET_HW), lambda h, i: (blk(i), h))
    tab = pl.BlockSpec((tb, 128), lambda h, i: (blk(i), 0))
    mat = pl.BlockSpec((1, CHUNK, CHUNK), lambda h, i: (h, 0, 0))
    vec = pl.BlockSpec((1, CHUNK, 1), lambda h, i: (h, 0, 0))
    one = pl.BlockSpec((1, 1, 1), lambda h, i: (h, 0, 0))
    own = pl.BlockSpec((tb, RET_DK), lambda h, i: (blk(i), h))
    st = pl.BlockSpec((1, tb // CHUNK, RET_DK, RET_DK), lambda h, i: (h, blk(i), 0, 0))
    return head, tab, mat, vec, one, own, st


def _ret_fwd(proj, cos, sin, intra, qdec, kdec, cdec, tb):
    S = proj.shape[0]
    nc = S // CHUNK
    scale = RET_DK ** -0.5
    dk = RET_DK

    def body(p_ref, cos_ref, sin_ref, m_ref, qd_ref, kd_ref, cd_ref, y_ref, yr_ref, st_ref, st):
        @pl.when(pl.program_id(1) == 0)
        def _():
            st[...] = jnp.zeros_like(st)

        mm, qd, kd, cd = m_ref[0], qd_ref[0], kd_ref[0], cd_ref[0]

        def chunk(c, carry):
            rows = pl.ds(pl.multiple_of(c * CHUNK, CHUNK), CHUNK)
            cs, sn = cos_ref[rows, :], sin_ref[rows, :]
            qr = _rope(p_ref[rows, pl.ds(0, dk)], cs, sn)
            kr = _rope(p_ref[rows, pl.ds(dk, dk)], cs, sn) * scale
            qb, kb, vb = qr.astype(bf16), kr.astype(bf16), p_ref[rows, pl.ds(2 * dk, dk)].astype(bf16)
            stb = st[...].astype(bf16)
            st_ref[0, c] = stb
            sc = (_nt(qb, kb) * mm).astype(bf16)
            y = _nn(sc, vb) + _nn(qb, stb) * qd
            st[...] = st[...] * cd + _tn((kr * kd).astype(bf16), vb)
            y_ref[rows, :] = y
            mu = jnp.mean(y, axis=-1, keepdims=True)
            yc = y - mu
            var = jnp.mean(yc * yc, axis=-1, keepdims=True)
            yr_ref[rows, :] = (yc * lax.rsqrt(var + EPS) * _silu(p_ref[rows, pl.ds(3 * dk, dk)])).astype(bf16)
            return carry

        lax.fori_loop(0, tb // CHUNK, chunk, 0, unroll=2)

    head, tab, mat, vec, one, own, stspec = _ret_specs(tb)
    return pl.pallas_call(
        body, name="ret_fwd", grid=(RET_HEADS, S // tb),
        in_specs=[head, tab, tab, mat, vec, vec, one], out_specs=[own, own, stspec],
        out_shape=[jax.ShapeDtypeStruct((S, 2048), f32), jax.ShapeDtypeStruct((S, 2048), bf16),
                   jax.ShapeDtypeStruct((RET_HEADS, nc, dk, dk), bf16)],
        scratch_shapes=[pltpu.VMEM((dk, dk), f32)], compiler_params=_params(("parallel", "arbitrary")),
    )(proj, cos, sin, intra, qdec, kdec, cdec)


def _ret_bwd(proj, cos, sin, intra, qdec, kdec, cdec, y, dyr, states, dproj, tb):
    S = proj.shape[0]
    nb = S // tb
    nck = tb // CHUNK
    scale = RET_DK ** -0.5
    dk = RET_DK

    def body(p_ref, cos_ref, sin_ref, m_ref, qd_ref, kd_ref, cd_ref, y_ref, dyr_ref, st_ref, _, o_ref, dst):
        @pl.when(pl.program_id(1) == 0)
        def _():
            dst[...] = jnp.zeros_like(dst)

        mm, qd, kd, cd = m_ref[0], qd_ref[0], kd_ref[0], cd_ref[0]

        def chunk(cc, carry):
            c = nck - 1 - cc
            rows = pl.ds(pl.multiple_of(c * CHUNK, CHUNK), CHUNK)
            cs, sn = cos_ref[rows, :], sin_ref[rows, :]
            qr = _rope(p_ref[rows, pl.ds(0, dk)], cs, sn)
            kr = _rope(p_ref[rows, pl.ds(dk, dk)], cs, sn) * scale
            qb, kb, vb = qr.astype(bf16), kr.astype(bf16), p_ref[rows, pl.ds(2 * dk, dk)].astype(bf16)
            kdb = (kr * kd).astype(bf16)
            stb = st_ref[0, c]
            yv, gv, dyrv = y_ref[rows, :], p_ref[rows, pl.ds(3 * dk, dk)], dyr_ref[rows, :]
            mu = jnp.mean(yv, axis=-1, keepdims=True)
            yc = yv - mu
            rstd = lax.rsqrt(jnp.mean(yc * yc, axis=-1, keepdims=True) + EPS)
            yn = yc * rstd
            o_ref[rows, pl.ds(3 * dk, dk)] = (dyrv * yn * _dsilu(gv)).astype(bf16)
            dyn = dyrv * _silu(gv)
            dy = rstd * (dyn - jnp.mean(dyn, axis=-1, keepdims=True) - yn * jnp.mean(dyn * yn, axis=-1, keepdims=True))
            dyb = dy.astype(bf16)
            dyqb = (dy * qd).astype(bf16)
            dstb = dst[...].astype(bf16)
            sct = (_nt(kb, qb) * mm).astype(bf16)
            ds = (_nt(dyb, vb) * mm).astype(bf16)
            dsT = (_nt(vb, dyb) * mm).astype(bf16)
            dv = _nn(sct, dyb) + _nn(kdb, dstb)
            dqr = _nn(ds, kb) + _nt(dyqb, stb)
            dkr = _nn(dsT, qb) + _nt(vb, dstb) * kd
            dst[...] = dst[...] * cd + _tn(qb, dyqb)
            o_ref[rows, pl.ds(0, dk)] = _rope_t(dqr, cs, sn).astype(bf16)
            o_ref[rows, pl.ds(dk, dk)] = (_rope_t(dkr, cs, sn) * scale).astype(bf16)
            o_ref[rows, pl.ds(2 * dk, dk)] = dv.astype(bf16)
            return carry

        lax.fori_loop(0, nck, chunk, 0, unroll=2)

    head, tab, mat, vec, one, own, stspec = _ret_specs(tb, rev_nb=nb)
    return pl.pallas_call(
        body, name="ret_bwd", grid=(RET_HEADS, nb),
        in_specs=[head, tab, tab, mat, vec, vec, one, own, own, stspec, ANY],
        out_specs=head, out_shape=jax.ShapeDtypeStruct(dproj.shape, bf16), input_output_aliases={10: 0},
        scratch_shapes=[pltpu.VMEM((dk, dk), f32)], compiler_params=_params(("parallel", "arbitrary")),
    )(proj, cos, sin, intra, qdec, kdec, cdec, y, dyr, states, dproj)


def _conv_fwd(proj, conv_w, conv_b, tb, cw):
    S = proj.shape[0]
    off = C_XBC // cw

    def body(x_ref, halo_ref, w_ref, b_ref, o_ref, xe):
        xe[pl.ds(0, 8), :] = jnp.where(pl.program_id(1) == 0, 0.0, halo_ref[...])
        xe[pl.ds(8, CS), :] = x_ref[pl.ds(0, CS), :]
        ws = [w_ref[pl.ds(j, 1), :] for j in range(SSD_CONV)]
        for s in range(tb // CS):
            tap = (lambda j: xe[pl.ds(5 + j, CS), :]) if s == 0 else (lambda j, s=s: x_ref[pl.ds(s * CS - 3 + j, CS), :])
            acc = b_ref[...] + ws[0] * tap(0)
            for j in range(1, SSD_CONV):
                acc = acc + ws[j] * tap(j)
            o_ref[pl.ds(s * CS, CS), :] = acc

    return pl.pallas_call(
        body, name="conv_fwd", grid=(CONV_DIM // cw, S // tb),
        in_specs=[pl.BlockSpec((tb, cw), lambda j, i: (i, off + j)),
                  pl.BlockSpec((8, cw), lambda j, i: (jnp.maximum(i * (tb // 8) - 1, 0), off + j)),
                  pl.BlockSpec((SSD_CONV, cw), lambda j, i: (0, j)), pl.BlockSpec((1, cw), lambda j, i: (0, j))],
        out_specs=pl.BlockSpec((tb, cw), lambda j, i: (i, j)),
        out_shape=jax.ShapeDtypeStruct((S, CONV_DIM), f32),
        scratch_shapes=[pltpu.VMEM((CS + 8, cw), f32)], compiler_params=_params(("parallel", "arbitrary")),
    )(proj, proj, conv_w, conv_b)


def _conv_bwd(dpre, proj, conv_w, dproj, tb, cw):
    S, n = dpre.shape
    nb = S // tb
    xoff = C_XBC // cw

    def body(d_ref, dh_ref, x_ref, xh_ref, w_ref, _, dx_ref, gw_ref, gb_ref, de, xe, accw, accb):
        i = pl.program_id(1)

        @pl.when(i == 0)
        def _():
            accw[...] = jnp.zeros_like(accw)
            accb[...] = jnp.zeros_like(accb)

        ns = tb // CS
        de[pl.ds(0, CS), :] = d_ref[pl.ds(tb - CS, CS), :]
        de[pl.ds(CS, 8), :] = jnp.where(i == nb - 1, 0.0, dh_ref[...])
        xe[pl.ds(0, 8), :] = jnp.where(i == 0, 0.0, xh_ref[...])
        xe[pl.ds(8, CS), :] = x_ref[pl.ds(0, CS), :]
        ws = [w_ref[pl.ds(j, 1), :] for j in range(SSD_CONV)]
        fold = lambda p: sum(p[8 * q:8 * (q + 1)] for q in range(1, CS // 8)) + p[0:8]
        for s in range(ns):
            dv = d_ref[pl.ds(s * CS, CS), :]
            ahead = (lambda o: de[pl.ds(o, CS), :]) if s == ns - 1 else (lambda o, s=s: d_ref[pl.ds(s * CS + o, CS), :])
            xtap = (lambda j: xe[pl.ds(5 + j, CS), :]) if s == 0 else (lambda j, s=s: x_ref[pl.ds(s * CS - 3 + j, CS), :])
            acc = ws[SSD_CONV - 1] * dv
            for j in range(SSD_CONV - 1):
                acc = acc + ws[j] * ahead(3 - j)
            dx_ref[pl.ds(s * CS, CS), :] = acc.astype(bf16)
            accb[...] += fold(dv)
            for j in range(SSD_CONV):
                accw[j] += fold(dv * xtap(j))

        @pl.when(i == nb - 1)
        def _():
            gb_ref[...] = jnp.sum(accb[...], axis=0, keepdims=True)
            for j in range(SSD_CONV):
                gw_ref[pl.ds(j, 1), :] = jnp.sum(accw[j], axis=0, keepdims=True)

    return pl.pallas_call(
        body, name="conv_bwd", grid=(n // cw, nb),
        in_specs=[pl.BlockSpec((tb, cw), lambda j, i: (i, j)),
                  pl.BlockSpec((8, cw), lambda j, i: (jnp.minimum((i + 1) * (tb // 8), S // 8 - 1), j)),
                  pl.BlockSpec((tb, cw), lambda j, i: (i, xoff + j)),
                  pl.BlockSpec((8, cw), lambda j, i: (jnp.maximum(i * (tb // 8) - 1, 0), xoff + j)),
                  pl.BlockSpec((SSD_CONV, cw), lambda j, i: (0, j)), ANY],
        out_specs=[pl.BlockSpec((tb, cw), lambda j, i: (i, xoff + j)), pl.BlockSpec((SSD_CONV, cw), lambda j, i: (0, j)),
                   pl.BlockSpec((1, cw), lambda j, i: (0, j))],
        out_shape=[jax.ShapeDtypeStruct(dproj.shape, bf16), jax.ShapeDtypeStruct((SSD_CONV, n), f32), jax.ShapeDtypeStruct((1, n), f32)],
        input_output_aliases={5: 0},
        scratch_shapes=[pltpu.VMEM((CS + 8, cw), f32), pltpu.VMEM((CS + 8, cw), f32), pltpu.VMEM((SSD_CONV, 8, cw), f32),
                        pltpu.VMEM((8, cw), f32)],
        compiler_params=_params(("parallel", "arbitrary")),
    )(dpre, dpre, proj, proj, conv_w, dproj)


def _dt_prep(dt_raw, dt_bias, a_log, tb):
    S = dt_raw.shape[0]

    def body(r_ref, b_ref, al_ref, dt_ref, sg_ref, ac_ref):
        li = lax.broadcasted_iota(jnp.int32, (LS, LS), 0)
        si = lax.broadcasted_iota(jnp.int32, (LS, LS), 1)
        tri = (li >= si).astype(f32)
        neg_a = -jnp.exp(al_ref[...])
        for c in range(tb // LS):
            rows = pl.ds(c * LS, LS)
            xv = r_ref[rows, :] + b_ref[...]
            dtv = jax.nn.softplus(xv)
            dt_ref[rows, :] = dtv
            sg_ref[rows, :] = jax.nn.sigmoid(xv)
            ac_ref[rows, :] = _hi(tri, dtv * neg_a)

    row = pl.BlockSpec((tb, 128), lambda i: (i, 0))
    vec = pl.BlockSpec((1, 128), lambda i: (0, 0))
    o = jax.ShapeDtypeStruct((S, 128), f32)
    return pl.pallas_call(body, name="dt_prep", grid=(S // tb,), in_specs=[row, vec, vec], out_specs=[row, row, row],
                          out_shape=[o, o, o], compiler_params=_params(("parallel",)))(dt_raw, dt_bias, a_log)


def _group_major(t):
    S = t.shape[0]
    return jnp.transpose(t[:, :SSD_HEADS].reshape(S, SSD_GROUPS, SSD_HPG), (1, 0, 2))


def _group_major_t(t):
    S = t.shape[0]
    return jnp.transpose(t[:, :SSD_HEADS].reshape(S // LS, LS, SSD_GROUPS, SSD_HPG), (2, 0, 3, 1))


def _ssd_specs(tb, rev_nb=None):
    def blk(i):
        return i if rev_nb is None else rev_nb - 1 - i
    grp = pl.BlockSpec((tb, SSD_GC), lambda g, i: (blk(i), g))
    xs = pl.BlockSpec((tb, SSD_GW), lambda g, i: (blk(i), g))
    ph = pl.BlockSpec((1, tb, SSD_HPG), lambda g, i: (g, blk(i), 0))
    pht = pl.BlockSpec((1, tb // LS, SSD_HPG, LS), lambda g, i: (g, blk(i), 0, 0))
    gvec = pl.BlockSpec((1, 1, SSD_GW), lambda g, i: (g, 0, 0))
    ex = pl.BlockSpec((SSD_HPG, SSD_GW), lambda g, i: (0, 0))
    st = pl.BlockSpec((1, tb // LS, SSD_STATE, SSD_GW), lambda g, i: (g, blk(i), 0, 0))
    return grp, xs, ph, pht, gvec, ex, st


def _expander():
    return jnp.repeat(jnp.eye(SSD_HPG, dtype=f32), SSD_GW // SSD_HPG, axis=1).astype(bf16)


def _expand3(dt8, ac8, ex):
    stack = jnp.concatenate([dt8, jnp.exp(ac8), jnp.exp(ac8[LS - 1:LS, :] - ac8)], axis=0)
    wide = _sel_r(stack, ex)
    return wide[0:LS], wide[LS:2 * LS], wide[2 * LS:3 * LS]


def _ssd_fwd(pre, dt_g, ac_g, act_g, dskx, tb):
    S = pre.shape[0]
    nc = S // LS
    hd = SSD_GW // SSD_HPG

    def body(p_ref, dt_ref, ac_ref, act_ref, dsk_ref, ex_ref, y_ref, st_ref, st):
        @pl.when(pl.program_id(1) == 0)
        def _():
            st[...] = jnp.zeros_like(st)

        ex = ex_ref[...]
        li = lax.broadcasted_iota(jnp.int32, (LS, LS), 0)
        si = lax.broadcasted_iota(jnp.int32, (LS, LS), 1)
        causal = li >= si

        def chunk(c, carry):
            rows = pl.ds(pl.multiple_of(c * LS, LS), LS)
            xs = _silu(p_ref[rows, pl.ds(0, SSD_GW)])
            bcb = _silu(p_ref[rows, pl.ds(SSD_GW, SSD_STATE)]).astype(bf16)
            ccb = _silu(p_ref[rows, pl.ds(SSD_GW + SSD_STATE, SSD_STATE)]).astype(bf16)
            dt8, ac8, act = dt_ref[0, rows, :], ac_ref[0, rows, :], act_ref[0, c]
            dtx, eax, tailx = _expand3(dt8, ac8, ex)
            xdt = xs * dtx
            cb = _nt(ccb, bcb)
            stb = st[...].astype(bf16)
            st_ref[0, c] = stb
            xdtb = xdt.astype(bf16)
            outs = []
            for h in range(SSD_HPG):
                dec = jnp.exp(jnp.where(causal, ac8[:, h:h + 1] - act[h:h + 1, :], -1e30))
                outs.append(_nn((cb * dec).astype(bf16), xdtb[:, hd * h:hd * (h + 1)]))
            y_ref[rows, :] = jnp.concatenate(outs, axis=1) + _nn(ccb, stb) * eax + dsk_ref[0] * xs
            st[...] = st[...] * eax[LS - 1:LS, :] + _tn(bcb, (xdt * tailx).astype(bf16))
            return carry

        lax.fori_loop(0, tb // LS, chunk, 0)

    grp, xs, ph, pht, gvec, ex, stspec = _ssd_specs(tb)
    return pl.pallas_call(
        body, name="ssd_fwd", grid=(SSD_GROUPS, S // tb),
        in_specs=[grp, ph, ph, pht, gvec, ex], out_specs=[xs, stspec],
        out_shape=[jax.ShapeDtypeStruct((S, SSD_WIDTH), f32), jax.ShapeDtypeStruct((SSD_GROUPS, nc, SSD_STATE, SSD_GW), bf16)],
        scratch_shapes=[pltpu.VMEM((SSD_STATE, SSD_GW), f32)], compiler_params=_params(("parallel", "arbitrary")),
    )(pre, dt_g, ac_g, act_g, dskx, _expander())


def _ssd_bwd(pre, dt_g, ac_g, act_g, sg_g, dskx, nega_g, dy, states, tb):
    S = pre.shape[0]
    nb = S // tb
    nck = tb // LS
    hd = SSD_GW // SSD_HPG

    def body(p_ref, dt_ref, ac_ref, act_ref, sg_ref, dsk_ref, na_ref, ex_ref, ext_ref, dy_ref, st_ref,
             dp_ref, ddt_ref, gsk_ref, gal_ref, gdb_ref, dst, skacc):
        @pl.when(pl.program_id(1) == 0)
        def _():
            dst[...] = jnp.zeros_like(dst)
            skacc[...] = jnp.zeros_like(skacc)
            gal_ref[...] = jnp.zeros_like(gal_ref)
            gdb_ref[...] = jnp.zeros_like(gdb_ref)

        ex, ext = ex_ref[...], ext_ref[...]
        li = lax.broadcasted_iota(jnp.int32, (LS, LS), 0)
        si = lax.broadcasted_iota(jnp.int32, (LS, LS), 1)
        causal = li >= si
        anti = si >= li
        upper = anti.astype(bf16)
        last_row = (lax.broadcasted_iota(jnp.int32, (LS, 1), 0) == LS - 1).astype(f32)
        head_id = lax.broadcasted_iota(jnp.int32, (1, SSD_HPG), 1)
        neg_a = na_ref[0]
        dskv = dsk_ref[0]

        def chunk(cc, carry):
            c = nck - 1 - cc
            rows = pl.ds(pl.multiple_of(c * LS, LS), LS)
            px = p_ref[rows, pl.ds(0, SSD_GW)]
            pb = p_ref[rows, pl.ds(SSD_GW, SSD_STATE)]
            pc = p_ref[rows, pl.ds(SSD_GW + SSD_STATE, SSD_STATE)]
            xs = _silu(px)
            bcb = _silu(pb).astype(bf16)
            ccb = _silu(pc).astype(bf16)
            dt8, ac8, act = dt_ref[0, rows, :], ac_ref[0, rows, :], act_ref[0, c]
            dtx, eax, tailx = _expand3(dt8, ac8, ex)
            xdt = xs * dtx
            ex_last = eax[LS - 1:LS, :]
            stb = st_ref[0, c]
            dyv = dy_ref[rows, :]
            dyb = dyv.astype(bf16)
            xdtb = xdt.astype(bf16)
            skacc[...] += jnp.sum(dyv * xs, axis=0, keepdims=True)
            yinter = _nn(ccb, stb) * eax
            dzb = (dyv * eax).astype(bf16)
            dcc = _nt(dzb, stb)
            dstv = dst[...]
            dstb = dstv.astype(bf16)
            xt = xdt * tailx
            dxt = _nn(bcb, dstb)
            dbc = _nt(xt.astype(bf16), dstb)
            dxdt = dxt * tailx
            lastrow = jnp.sum(dxt * xt, axis=0, keepdims=True) + jnp.sum(dstv * stb.astype(f32), axis=0, keepdims=True) * ex_last
            dst[...] = dstv * ex_last + _tn(ccb, dzb)
            cb = _nt(ccb, bcb)
            cbt = _nt(bcb, ccb)
            dcb = jnp.zeros((LS, LS), f32)
            dcbt = jnp.zeros((LS, LS), f32)
            dac8 = jnp.zeros((LS, SSD_HPG), f32)
            dxin = []
            for h in range(SSD_HPG):
                sl = slice(hd * h, hd * (h + 1))
                col, rowv = ac8[:, h:h + 1], act[h:h + 1, :]
                dec = jnp.exp(jnp.where(causal, col - rowv, -1e30))
                dect = jnp.exp(jnp.where(anti, rowv - col, -1e30))
                gm, gmt = cb * dec, cbt * dect
                dgm, dgmt = _nt(dyb[:, sl], xdtb[:, sl]), _nt(xdtb[:, sl], dyb[:, sl])
                dxin.append(_nn(gmt.astype(bf16), dyb[:, sl]))
                dcb = dcb + dgm * dec
                dcbt = dcbt + dgmt * dect
                dcol = jnp.sum(dgm * gm, axis=1, keepdims=True) - jnp.sum(dgmt * gmt, axis=1, keepdims=True)
                dac8 = dac8 + dcol * (head_id == h).astype(f32)
            dxintra = jnp.concatenate(dxin, axis=1)
            dcc = dcc + _nn(dcb.astype(bf16), bcb)
            dbc = dbc + _nn(dcbt.astype(bf16), ccb)
            dxdt = dxdt + dxintra
            dacx = dyv * yinter - dxt * xt + last_row * lastrow
            red = _sel_r(jnp.concatenate([dacx, dxdt * xs], axis=0), ext)
            dac8 = dac8 + red[0:LS]
            da8 = _sel_l(upper, dac8)
            ddt8 = red[LS:2 * LS] + da8 * neg_a
            gal_ref[0] += jnp.sum(da8 * dt8 * neg_a, axis=0, keepdims=True)
            ddr = ddt8 * sg_ref[0, rows, :]
            ddt_ref[0, rows, :] = ddr
            gdb_ref[0] += jnp.sum(ddr, axis=0, keepdims=True)
            dp_ref[rows, pl.ds(0, SSD_GW)] = (dskv * dyv + dxdt * dtx) * _dsilu(px)
            dp_ref[rows, pl.ds(SSD_GW, SSD_STATE)] = dbc * _dsilu(pb)
            dp_ref[rows, pl.ds(SSD_GW + SSD_STATE, SSD_STATE)] = dcc * _dsilu(pc)
            return carry

        lax.fori_loop(0, nck, chunk, 0)

        @pl.when(pl.program_id(1) == nb - 1)
        def _():
            gsk_ref[0] = skacc[...]

    grp, xs, ph, pht, gvec, ex, stspec = _ssd_specs(tb, rev_nb=nb)
    small = pl.BlockSpec((1, 1, SSD_HPG), lambda g, i: (g, 0, 0))
    ext = pl.BlockSpec((SSD_GW, SSD_HPG), lambda g, i: (0, 0))
    sm = jax.ShapeDtypeStruct((SSD_GROUPS, 1, SSD_HPG), f32)
    expander = _expander()
    return pl.pallas_call(
        body, name="ssd_bwd", grid=(SSD_GROUPS, nb),
        in_specs=[grp, ph, ph, pht, ph, gvec, small, ex, ext, xs, stspec],
        out_specs=[grp, ph, gvec, small, small],
        out_shape=[jax.ShapeDtypeStruct((S, CONV_DIM), f32), jax.ShapeDtypeStruct((SSD_GROUPS, S, SSD_HPG), f32),
                   jax.ShapeDtypeStruct((SSD_GROUPS, 1, SSD_GW), f32), sm, sm],
        scratch_shapes=[pltpu.VMEM((SSD_STATE, SSD_GW), f32), pltpu.VMEM((1, SSD_GW), f32)],
        compiler_params=_params(("parallel", "arbitrary")),
    )(pre, dt_g, ac_g, act_g, sg_g, dskx, nega_g, expander, expander.T, dy, states)


def _tiles(S):
    return dict(tb=min(512, S), tr=min(256, S), tm=min(1024, S))


def _local_step(x, positions, target, norm1_w, w_main, w_dt, conv_w, conv_b, dt_bias, a_log, d_skip, ssd_norm_w,
                w_br, w_bs, w_o, norm_f_w):
    S, D = x.shape
    t = _tiles(S)
    tb, tr, tm = t["tb"], t["tr"], t["tm"]

    half = RET_DK // 2
    inv_freq = ROPE_THETA ** (-jnp.arange(half, dtype=f32) / half)
    ang = positions.astype(f32)[:, None] * inv_freq
    cos, sin = jnp.cos(ang), jnp.sin(ang)
    log_gamma = jnp.log1p(-(2.0 ** (-5.0 - jnp.arange(RET_HEADS, dtype=f32))))
    idx = jnp.arange(CHUNK, dtype=f32)
    intra = jnp.exp(jnp.abs(idx[:, None] - idx[None, :]) * log_gamma[:, None, None])
    qdec = jnp.exp((idx + 1.0)[None, :] * log_gamma[:, None])[:, :, None]
    kdec = jnp.exp((CHUNK - 1.0 - idx)[None, :] * log_gamma[:, None])[:, :, None]
    cdec = jnp.exp(CHUNK * log_gamma)[:, None, None]
    conv_wm, conv_bm = _xbc_group_major(conv_w), _xbc_group_major(conv_b)

    h = _norm1_fwd(x, norm1_w, tr)
    proj = _mm1(h, w_main, tm=tm, tn=1024, tk=D, out_dtype=f32, name="proj_main")
    dt_raw = _mm1(h, w_dt, tm=tm, tn=128, tk=D, out_dtype=f32, name="proj_dt")
    y_ret, yr, ret_states = _ret_fwd(proj, cos, sin, intra, qdec, kdec, cdec, tb)
    pre = _conv_fwd(proj, conv_wm, conv_bm, tb, 512)
    pad64 = lambda v: jnp.pad(v, ((0, 0), (0, 128 - SSD_HEADS)))
    dt, sg, ac = _dt_prep(dt_raw, pad64(dt_bias), pad64(a_log), tb)
    dt_g, ac_g, sg_g, act_g = _group_major(dt), _group_major(ac), _group_major(sg), _group_major_t(ac)
    dskx = jnp.repeat(d_skip.reshape(SSD_GROUPS, 1, SSD_HPG), SSD_GW // SSD_HPG, axis=2)
    nega_g = (-jnp.exp(a_log)).reshape(SSD_GROUPS, 1, SSD_HPG)
    y_ssd, ssd_states = _ssd_fwd(pre, dt_g, ac_g, act_g, dskx, tb)
    ys = _ssd_norm_fwd(y_ssd, proj, ssd_norm_w, tr // 2)
    p_r = _mm1(yr, w_br, tm=tm, tn=1024, tk=2048, out_dtype=f32, name="branch_ret")
    p_s = _mm1(ys, w_bs, tm=tm, tn=1024, tk=2048, out_dtype=f32, name="branch_ssd")
    merged = _merge_fwd(p_r, p_s, proj, tr)
    mo = _mm1(merged, w_o, tm=tm, tn=1024, tk=2048, out_dtype=f32, name="out_proj")
    dx2, dx2b, loss, g_norm_f = _final_fwd_bwd(x, mo, target, norm_f_w.reshape(1, D), tr)

    tkt = min(4096, S)
    wg = lambda at, b, name, tn=1024: _mm1(at, b, tm=min(1024, at.shape[0]), tn=tn, tk=tkt, out_dtype=f32, name=name)
    dm = _mm1(dx2b, w_o, tm=tm, tn=1024, tk=2048, out_dtype=f32, name="d_merged", tb=True)
    g_w_o = wg(merged.T, dx2b, "g_w_out")
    dp_r, dp_s, dproj = _merge_bwd(dm, p_r, p_s, proj, tr)
    dyr = _mm1(dp_r, w_br, tm=tm, tn=1024, tk=2048, out_dtype=f32, name="d_yr", tb=True)
    dys = _mm1(dp_s, w_bs, tm=tm, tn=1024, tk=2048, out_dtype=f32, name="d_ys", tb=True)
    g_w_br = wg(yr.T, dp_r, "g_w_br_ret")
    g_w_bs = wg(ys.T, dp_s, "g_w_br_ssd")
    dy_ssd, dproj, g_ssd_norm = _ssd_norm_bwd(y_ssd, proj, ssd_norm_w, dys, dproj, tr // 2)
    dproj = _ret_bwd(proj, cos, sin, intra, qdec, kdec, cdec, y_ret, dyr, ret_states, dproj, tb)
    dpre, ddt_g, gsk, gal, gdb = _ssd_bwd(pre, dt_g, ac_g, act_g, sg_g, dskx, nega_g, dy_ssd, ssd_states, tb)
    dproj, gcw, gcb = _conv_bwd(dpre, proj, conv_wm, dproj, tb, 512)
    ddt = jnp.transpose(ddt_g, (1, 0, 2)).reshape(S, SSD_HEADS)
    ddt_p = jnp.pad(ddt, ((0, 0), (0, 128 - SSD_HEADS))).astype(bf16)

    dh = _mm([(dproj, 0, w_main, 0, 0, N_MAIN, N_MAIN // 8), (ddt_p, 0, w_dt, 0, 0, 128, 128)], S, D, tm=tm, tn=1024,
             out_dtype=f32, name="d_h", tb=True)
    ht = h.T
    g_main = wg(ht, dproj, "g_w_in_main")
    g_dt = wg(ht, ddt_p, "g_w_in_dt", tn=128)
    grad_x, g_norm1 = _norm1_bwd(x, norm1_w, dh, dx2, tr)

    seg = lambda v: jnp.sum(v.reshape(SSD_HEADS, SSD_GW // SSD_HPG), axis=1).reshape(1, SSD_HEADS)
    grads = dict(
        norm1_w=g_norm1, w_in_main=g_main, w_in_dt=g_dt,
        conv_w=_xbc_original(gcw), conv_b=_xbc_original(gcb),
        dt_bias=gdb.reshape(1, SSD_HEADS), a_log=gal.reshape(1, SSD_HEADS), d_skip=seg(gsk),
        ssd_norm_w=g_ssd_norm, w_br_ret=g_w_br, w_br_ssd=g_w_bs, w_out=g_w_o, norm_f_w=g_norm_f,
    )
    return loss, grad_x, grads


def _me():
    return lax.axis_index("x"), lax.axis_index("y"), lax.axis_index("c")


def _other_chips(x, y):
    return [(1 - x, y), (x, 1 - y), (1 - x, 1 - y)]


def _gather_weights(a, b, cw):
    R = a.shape[0]
    hr = R // 2

    def body(a_ref, b_ref, cw_ref, ga_ref, gb_ref, gc_ref, send_sems, recv_sems):
        x, y, c = _me()
        k = 2 * x + y
        sibling = (x, y, 1 - c)
        chips = _other_chips(x, y)

        def small(j, src_shard, to):
            return pltpu.make_async_remote_copy(
                src_ref=cw_ref, dst_ref=gc_ref.at[src_shard], send_sem=send_sems.at[12 + j], recv_sem=recv_sems.at[12 + j],
                device_id=to, device_id_type=MESH)

        def copies(j, src_shard, half, to, from_input):
            rows = pl.ds(half * hr, hr)
            out = []
            for t, (inp, g) in enumerate(((a_ref, ga_ref), (b_ref, gb_ref))):
                src = inp.at[rows, :] if from_input else g.at[src_shard, rows, :]
                out.append(pltpu.make_async_remote_copy(
                    src_ref=src, dst_ref=g.at[src_shard, rows, :], send_sem=send_sems.at[2 * j + t],
                    recv_sem=recv_sems.at[2 * j + t], device_id=to, device_id_type=MESH))
            return out

        first = []
        for j, chip in enumerate(chips):
            first += copies(j, k, c, (*chip, c), True)
            first.append(small(j, k, (*chip, c)))
        for cp in first:
            cp.start()
        passed = []
        for j, chip in enumerate(chips):
            kk = 2 * chip[0] + chip[1]
            for cp in copies(j, kk, c, (x, y, c), False):
                cp.wait_recv()
            fw = copies(3 + j, kk, c, sibling, False)
            for cp in fw:
                cp.start()
            passed += fw
        for j, chip in enumerate(chips):
            kk = 2 * chip[0] + chip[1]
            for cp in copies(3 + j, kk, 1 - c, (x, y, c), False):
                cp.wait_recv()
            small(j, kk, (x, y, c)).wait_recv()
        for cp in first + passed:
            cp.wait_send()

    return pl.pallas_call(
        body, name="gather_weights", in_specs=[ANY, ANY, ANY], out_specs=[ANY, ANY, ANY],
        out_shape=[jax.ShapeDtypeStruct((N_SHARD,) + a.shape, a.dtype), jax.ShapeDtypeStruct((N_SHARD,) + b.shape, b.dtype),
                   jax.ShapeDtypeStruct((N_SHARD,) + cw.shape, cw.dtype)],
        scratch_shapes=[pltpu.SemaphoreType.DMA((15,)), pltpu.SemaphoreType.DMA((15,))],
        compiler_params=pltpu.CompilerParams(has_side_effects=True),
    )(a, b, cw)


def _sibling_swap(arrs, name):
    n = len(arrs)

    def body(*refs):
        ins, outs = refs[:n], refs[n:2 * n]
        send_sems, recv_sems = refs[2 * n], refs[2 * n + 1]
        x, y, c = _me()
        cps = [pltpu.make_async_remote_copy(src_ref=ins[t].at[s, 1 - c], dst_ref=outs[t].at[s], send_sem=send_sems.at[N_SHARD * t + s],
                                            recv_sem=recv_sems.at[N_SHARD * t + s], device_id=(x, y, 1 - c), device_id_type=MESH)
               for t in range(n) for s in range(N_SHARD)]
        for cp in cps:
            cp.start()
        for cp in cps:
            cp.wait()

    return pl.pallas_call(
        body, name=name, in_specs=[ANY] * n, out_specs=[ANY] * n,
        out_shape=[jax.ShapeDtypeStruct((N_SHARD,) + a.shape[2:], a.dtype) for a in arrs],
        scratch_shapes=[pltpu.SemaphoreType.DMA((N_SHARD * n,)), pltpu.SemaphoreType.DMA((N_SHARD * n,))],
        compiler_params=pltpu.CompilerParams(has_side_effects=True),
    )(*arrs)


def _chip_exchange(arrs, name):
    n = len(arrs)

    def body(*refs):
        ins, outs = refs[:n], refs[n:2 * n]
        send_sems, recv_sems = refs[2 * n], refs[2 * n + 1]
        x, y, c = _me()
        cps = []
        for j, chip in enumerate(_other_chips(x, y)):
            kk = 2 * chip[0] + chip[1]
            for t in range(n):
                cps.append(pltpu.make_async_remote_copy(
                    src_ref=ins[t].at[kk], dst_ref=outs[t].at[j], send_sem=send_sems.at[n * j + t],
                    recv_sem=recv_sems.at[n * j + t], device_id=(*chip, c), device_id_type=MESH))
        for cp in cps:
            cp.start()
        for cp in cps:
            cp.wait()

    return pl.pallas_call(
        body, name=name, in_specs=[ANY] * n, out_specs=[ANY] * n,
        out_shape=[jax.ShapeDtypeStruct((3,) + a.shape[1:], a.dtype) for a in arrs],
        scratch_shapes=[pltpu.SemaphoreType.DMA((3 * n,)), pltpu.SemaphoreType.DMA((3 * n,))],
        compiler_params=pltpu.CompilerParams(has_side_effects=True),
    )(*arrs)


def _send_to_sibling(arrs, name):
    n = len(arrs)

    def body(*refs):
        ins, outs = refs[:n], refs[n:2 * n]
        send_sems, recv_sems = refs[2 * n], refs[2 * n + 1]
        x, y, c = _me()
        sends = [pltpu.make_async_remote_copy(src_ref=ins[t], dst_ref=outs[t].at[c], send_sem=send_sems.at[t], recv_sem=recv_sems.at[t],
                                              device_id=(x, y, 1 - c), device_id_type=MESH) for t in range(n)]
        for cp in sends:
            cp.start()
        for t in range(n):
            pltpu.make_async_remote_copy(src_ref=ins[t], dst_ref=outs[t].at[1 - c], send_sem=send_sems.at[t], recv_sem=recv_sems.at[t],
                                         device_id=(x, y, c), device_id_type=MESH).wait_recv()
        for cp in sends:
            cp.wait_send()

    return pl.pallas_call(
        body, name=name, in_specs=[ANY] * n, out_specs=[ANY] * n,
        out_shape=[jax.ShapeDtypeStruct((2,) + a.shape, a.dtype) for a in arrs],
        scratch_shapes=[pltpu.SemaphoreType.DMA((n,)), pltpu.SemaphoreType.DMA((n,))],
        compiler_params=pltpu.CompilerParams(has_side_effects=True),
    )(*arrs)


def _gather_vec(v):
    n = v.shape[1]

    def body(v_ref, o_ref, send_sems, recv_sems):
        x, y, c = _me()
        me = 4 * x + 2 * y + c
        cps = []
        for j in range(1, 8):
            fx, fy, fc = (j >> 2) & 1, (j >> 1) & 1, j & 1
            peer = (x ^ fx, y ^ fy, c ^ fc)
            cps.append(pltpu.make_async_remote_copy(
                src_ref=v_ref, dst_ref=o_ref.at[pl.ds(me, 1), :], send_sem=send_sems.at[j - 1], recv_sem=recv_sems.at[j - 1],
                device_id=peer, device_id_type=MESH))
        for cp in cps:
            cp.start()
        for j in range(1, 8):
            fx, fy, fc = (j >> 2) & 1, (j >> 1) & 1, j & 1
            src = 4 * (x ^ fx) + 2 * (y ^ fy) + (c ^ fc)
            pltpu.make_async_remote_copy(
                src_ref=v_ref, dst_ref=o_ref.at[pl.ds(src, 1), :], send_sem=send_sems.at[j - 1], recv_sem=recv_sems.at[j - 1],
                device_id=(x, y, c), device_id_type=MESH).wait_recv()
        for cp in cps:
            cp.wait_send()

    return pl.pallas_call(
        body, name="gather_vec", in_specs=[ANY], out_specs=ANY, out_shape=jax.ShapeDtypeStruct((8, n), v.dtype),
        scratch_shapes=[pltpu.SemaphoreType.DMA((7,)), pltpu.SemaphoreType.DMA((7,))],
        compiler_params=pltpu.CompilerParams(has_side_effects=True),
    )(v)


def _pair_sum(g, r, name):
    _, _, hr, C = g.shape
    tr = SUM_ROWS

    def body(c_ref, g_ref, r_ref, o_ref):
        def strip(rows):
            o_ref[0, rows, :] = (g_ref[0, 0, rows, :] + r_ref[0, rows, :]).astype(bf16)
        _for_strips(tr, strip)

    grid_spec = pltpu.PrefetchScalarGridSpec(
        num_scalar_prefetch=1, grid=(N_SHARD, hr // tr),
        in_specs=[pl.BlockSpec((1, 1, tr, C), lambda s, i, c_ref: (s, c_ref[0], i, 0)),
                  pl.BlockSpec((1, tr, C), lambda s, i, c_ref: (s, i, 0))],
        out_specs=pl.BlockSpec((1, tr, C), lambda s, i, c_ref: (s, i, 0)))
    c = lax.axis_index("c").reshape(1).astype(jnp.int32)
    return pl.pallas_call(body, name=name, grid_spec=grid_spec, out_shape=jax.ShapeDtypeStruct((N_SHARD, hr, C), bf16),
                          compiler_params=_params(("parallel", "parallel")))(c, g, r)


def _own_sum(g, r, got, name):
    _, _, hr, C = g.shape
    tr = SUM_ROWS

    def body(idx_ref, g_ref, r_ref, got_ref, o_ref):
        def strip(rows):
            s = g_ref[0, 0, rows, :] + r_ref[0, rows, :]
            o_ref[rows, :] = ((s + got_ref[0, rows, :].astype(f32)) + got_ref[1, rows, :].astype(f32)) + got_ref[2, rows, :].astype(f32)
        _for_strips(tr, strip)

    grid_spec = pltpu.PrefetchScalarGridSpec(
        num_scalar_prefetch=1, grid=(hr // tr,),
        in_specs=[pl.BlockSpec((1, 1, tr, C), lambda i, idx: (idx[0], idx[1], i, 0)),
                  pl.BlockSpec((1, tr, C), lambda i, idx: (idx[0], i, 0)),
                  pl.BlockSpec((3, tr, C), lambda i, idx: (0, i, 0))],
        out_specs=pl.BlockSpec((tr, C), lambda i, idx: (i, 0)))
    x, y, c = _me()
    idx = jnp.stack([2 * x + y, c]).astype(jnp.int32)
    return pl.pallas_call(body, name=name, grid_spec=grid_spec, out_shape=jax.ShapeDtypeStruct((hr, C), f32),
                          compiler_params=_params(("parallel",)))(idx, g, r, got)


def _adamw(w, g, m, v, name, tr):
    R, C = w.shape

    def body(w_ref, g_ref, m_ref, v_ref, d_ref, nm_ref, nv_ref):
        def strip(rows):
            gv = g_ref[rows, :]
            mn = ADAM_B1 * m_ref[rows, :] + (1.0 - ADAM_B1) * gv
            vn = ADAM_B2 * v_ref[rows, :] + (1.0 - ADAM_B2) * (gv * gv)
            m_hat = mn / (1.0 - ADAM_B1 ** ADAM_STEP)
            v_hat = vn / (1.0 - ADAM_B2 ** ADAM_STEP)
            d_ref[rows, :] = -ADAM_LR * (m_hat / (jnp.sqrt(v_hat) + ADAM_EPS) + ADAM_WD * w_ref[rows, :])
            nm_ref[rows, :] = mn
            nv_ref[rows, :] = vn
        _for_strips(tr, strip, rs=min(8, tr))

    blk = pl.BlockSpec((tr, C), lambda i: (i, 0))
    o = jax.ShapeDtypeStruct((R, C), f32)
    return pl.pallas_call(body, name=name, grid=(R // tr,), in_specs=[blk] * 4, out_specs=[blk] * 3, out_shape=[o, o, o],
                          compiler_params=_params(("parallel",)))(w, g, m, v)


def _sum8(t):
    n = t.shape[1]

    def body(t_ref, o_ref):
        acc = t_ref[pl.ds(0, 1), :]
        for r in range(1, 8):
            acc = acc + t_ref[pl.ds(r, 1), :]
        o_ref[...] = acc

    return pl.pallas_call(body, name="sum_devices", out_shape=jax.ShapeDtypeStruct((1, n), f32))(t)


def _reduce_to_owner(g_in, g_b):
    xi, yi, ci = _me()
    hr = g_in.shape[1] // 2
    g_in = g_in.reshape(N_SHARD, 2, hr, g_in.shape[2])
    g_b = g_b.reshape(N_SHARD, 2, hr, g_b.shape[2])
    r_in, r_b = _sibling_swap([g_in, g_b], "sibling_swap")
    p_in, p_b = _pair_sum(g_in, r_in, "pair_sum_in"), _pair_sum(g_b, r_b, "pair_sum_b")
    got_in, got_b = _chip_exchange([p_in, p_b], "chip_exchange")
    mine_in, mine_b = _own_sum(g_in, r_in, got_in, "own_sum_in"), _own_sum(g_b, r_b, got_b, "own_sum_b")
    both_in, both_b = _send_to_sibling([mine_in, mine_b], "share_halves")
    full = lambda both, mine: lax.dynamic_update_slice_in_dim(both, mine[None], ci, axis=0).reshape(2 * hr, mine.shape[1])
    return full(both_in, mine_in), full(both_b, mine_b)


def kernel(x, positions, norm1_w, w_in, conv_w, conv_b, dt_bias, a_log, d_skip, ssd_norm_w, w_br_ret, w_br_ssd, w_out, norm_f_w, loss_target, m_norm1_w, m_w_in, m_conv_w, m_conv_b, m_dt_bias, m_a_log, m_d_skip, m_ssd_norm_w, m_w_br_ret, m_w_br_ssd, m_w_out, m_norm_f_w, v_norm1_w, v_w_in, v_conv_w, v_conv_b, v_dt_bias, v_a_log, v_d_skip, v_ssd_norm_w, v_w_br_ret, v_w_br_ssd, v_w_out, v_norm_f_w):
    D = D_MODEL
    xi, yi, ci = _me()
    k = 2 * xi + yi
    me = 2 * k + ci
    weights = dict(norm1_w=norm1_w, w_in=w_in, conv_w=conv_w, conv_b=conv_b, dt_bias=dt_bias, a_log=a_log, d_skip=d_skip,
                   ssd_norm_w=ssd_norm_w, w_br_ret=w_br_ret, w_br_ssd=w_br_ssd, w_out=w_out, norm_f_w=norm_f_w)
    mom1 = dict(norm1_w=m_norm1_w, w_in=m_w_in, conv_w=m_conv_w, conv_b=m_conv_b, dt_bias=m_dt_bias, a_log=m_a_log, d_skip=m_d_skip,
                ssd_norm_w=m_ssd_norm_w, w_br_ret=m_w_br_ret, w_br_ssd=m_w_br_ssd, w_out=m_w_out, norm_f_w=m_norm_f_w)
    mom2 = dict(norm1_w=v_norm1_w, w_in=v_w_in, conv_w=v_conv_w, conv_b=v_conv_b, dt_bias=v_dt_bias, a_log=v_a_log, d_skip=v_d_skip,
                ssd_norm_w=v_ssd_norm_w, w_br_ret=v_w_br_ret, w_br_ssd=v_w_br_ssd, w_out=v_w_out, norm_f_w=v_norm_f_w)

    a_sh = w_in[0].astype(bf16)
    b_sh = jnp.concatenate([w_br_ret[0], w_br_ssd[0], w_out[0]], axis=0).astype(bf16)
    ga, gb, gc = _gather_weights(a_sh, b_sh, conv_w[0])
    own = lambda g, s: lax.dynamic_update_slice_in_dim(g, s[None], k, axis=0)
    ga, gb, gc = own(ga, a_sh), own(gb, b_sh), own(gc, conv_w[0])
    w_main, w_dt = _split_w_in(jnp.transpose(ga, (1, 0, 2)).reshape(D, IN_PROJ))
    w_br = gb[:, 0:512].reshape(2048, D)
    w_bs = gb[:, 512:1536].reshape(4096, D)
    w_o = gb[:, 1536:2048].reshape(2048, D)
    conv_full = jnp.transpose(gc, (1, 0, 2)).reshape(SSD_CONV, CONV_DIM)

    loss, grad_x, g = _local_step(x[0], positions[0], loss_target[0], norm1_w, w_main, w_dt, conv_full, conv_b, dt_bias, a_log,
                                  d_skip, ssd_norm_w, w_br, w_bs, w_o, norm_f_w)

    g_in = jnp.transpose(_w_in_grad_full(g["w_in_main"], g["w_in_dt"]).reshape(D, N_SHARD, W_IN_SHARD), (1, 0, 2))
    g_b = jnp.concatenate([g["w_br_ret"].reshape(N_SHARD, 512, D), g["w_br_ssd"].reshape(N_SHARD, 1024, D),
                           g["w_out"].reshape(N_SHARD, 512, D)], axis=1)
    grad_w_in, full_b = _reduce_to_owner(g_in, g_b)
    grad_mats = dict(w_in=grad_w_in, w_br_ret=full_b[0:512], w_br_ssd=full_b[512:1536], w_out=full_b[1536:2048])

    small = [(n, weights[n].size) for n in ("norm1_w", "conv_b", "dt_bias", "a_log", "d_skip", "ssd_norm_w", "norm_f_w")]
    parts = [jnp.pad(loss.reshape(1, 1), ((0, 0), (0, 127)))] + [g[n].reshape(1, -1) for n, _ in small] + [g["conv_w"].reshape(1, -1)]
    vec = jnp.concatenate(parts, axis=1)
    nv = vec.shape[1]
    nvp = -(-nv // 128) * 128
    vec = jnp.pad(vec, ((0, 0), (0, nvp - nv)))
    total = _sum8(lax.dynamic_update_slice_in_dim(_gather_vec(vec), vec, me, axis=0))
    loss_out = total[0, 0]
    off = 128
    grad_small = {}
    for n, sz in small:
        grad_small[n] = total[:, off:off + sz]
        off += sz
    g_conv = total[:, off:off + SSD_CONV * CONV_DIM].reshape(SSD_CONV, CONV_DIM)
    g_conv = lax.dynamic_slice_in_dim(g_conv, k * (CONV_DIM // N_SHARD), CONV_DIM // N_SHARD, axis=1)
    grad_small["conv_w"] = g_conv.reshape(1, -1)

    upd = {}
    for n in ("w_in", "w_br_ret", "w_br_ssd", "w_out"):
        upd[n] = _adamw(weights[n][0], grad_mats[n], mom1[n][0], mom2[n][0], "adamw_" + n, SUM_ROWS)
    names_small = [n for n, _ in small] + ["conv_w"]
    flat = lambda d: jnp.concatenate([d[n].reshape(1, -1) for n in names_small], axis=1)
    ns = sum(weights[n].size for n in names_small)
    nsp = -(-ns // 128) * 128
    padv = lambda t: jnp.pad(t, ((0, 0), (0, nsp - ns)))
    ds, ms, vs = _adamw(padv(flat(weights)), padv(flat(grad_small)), padv(flat(mom1)),
                        jnp.pad(flat(mom2), ((0, 0), (0, nsp - ns)), constant_values=1.0), "adamw_small", 1)
    off = 0
    for n in names_small:
        sz = weights[n].size
        upd[n] = tuple(t[:, off:off + sz] for t in (ds, ms, vs))
        off += sz

    order = ["norm1_w", "w_in", "conv_w", "conv_b", "dt_bias", "a_log", "d_skip", "ssd_norm_w", "w_br_ret", "w_br_ssd", "w_out", "norm_f_w"]
    grads_out = {**grad_mats, **grad_small}
    shp = lambda n, t: t.reshape(weights[n].shape)
    return (loss_out, grad_x[None], *[shp(n, grads_out[n]) for n in order], *[shp(n, upd[n][0]) for n in order],
            *[shp(n, upd[n][1]) for n in order], *[shp(n, upd[n][2]) for n in order])
```

```python
import jax
import jax.numpy as jnp
import numpy as np
from jax import lax
from jax.experimental import pallas as pl
from jax.experimental.pallas import tpu as pltpu

f32 = jnp.float32
bf16 = jnp.bfloat16
HIGHEST = lax.Precision.HIGHEST
MESH = pl.DeviceIdType.MESH

D_MODEL = 2048
EPS = 1e-6
CHUNK = 64
RET_HEADS = 8
RET_DK = 256
RET_HW = 4 * RET_DK
ROPE_THETA = 10000.0
SSD_WIDTH = 4096
SSD_GROUPS = 8
SSD_STATE = 128
SSD_GW = 512
SSD_GC = SSD_GW + 2 * SSD_STATE
SSD_HPG = 8
SSD_CONV = 4
CONV_DIM = 6144
SSD_HEADS = 64
LS = 128

C_RET, C_Z, C_GATES, C_XBC = 0, 8192, 12288, 16384
N_MAIN = 22528
DT_OFF = 18432
IN_PROJ = 22592
N_SHARD = 4
W_IN_SHARD = IN_PROJ // N_SHARD

ADAM_LR, ADAM_B1, ADAM_B2, ADAM_EPS, ADAM_WD, ADAM_STEP = 0.001, 0.9, 0.999, 1e-08, 0.01, 10

VMEM_LIMIT = 56 * 1024 * 1024
SUM_ROWS = 128
ANY = pl.BlockSpec(memory_space=pl.ANY)


def _params(dims):
    return pltpu.CompilerParams(dimension_semantics=dims, vmem_limit_bytes=VMEM_LIMIT)


def _silu(x):
    return x * jax.nn.sigmoid(x)


def _dsilu(x):
    s = jax.nn.sigmoid(x)
    return s * (1.0 + x * (1.0 - s))


def _nt(a, b):
    return lax.dot_general(a, b, (((1,), (1,)), ((), ())), preferred_element_type=f32)


def _tn(a, b):
    return lax.dot_general(a, b, (((0,), (0,)), ((), ())), preferred_element_type=f32)


def _nn(a, b):
    return jnp.dot(a, b, preferred_element_type=f32)


def _hi(a, b):
    return jnp.dot(a, b, precision=HIGHEST, preferred_element_type=f32)


def _split(a):
    hi = a.astype(bf16)
    return hi, (a - hi.astype(f32)).astype(bf16)


def _sel_r(a, sel):
    hi, lo = _split(a)
    return _nn(hi, sel) + _nn(lo, sel)


def _sel_l(sel, a):
    hi, lo = _split(a)
    return _nn(sel, hi) + _nn(sel, lo)


def _xbc_group_major(t):
    R = t.shape[0]
    nb = SSD_GROUPS * SSD_STATE
    parts = [t[:, :SSD_WIDTH].reshape(R, SSD_GROUPS, SSD_GW), t[:, SSD_WIDTH:SSD_WIDTH + nb].reshape(R, SSD_GROUPS, SSD_STATE),
             t[:, SSD_WIDTH + nb:].reshape(R, SSD_GROUPS, SSD_STATE)]
    return jnp.concatenate(parts, axis=2).reshape(R, CONV_DIM)


def _xbc_original(t):
    R = t.shape[0]
    g = t.reshape(R, SSD_GROUPS, SSD_GC)
    parts = [g[:, :, :SSD_GW].reshape(R, SSD_WIDTH), g[:, :, SSD_GW:SSD_GW + SSD_STATE].reshape(R, SSD_GROUPS * SSD_STATE),
             g[:, :, SSD_GW + SSD_STATE:].reshape(R, SSD_GROUPS * SSD_STATE)]
    return jnp.concatenate(parts, axis=1)


def _split_w_in(w):
    D = w.shape[0]
    ret = jnp.transpose(w[:, :4 * 2048].reshape(D, 4, RET_HEADS, RET_DK), (0, 2, 1, 3)).reshape(D, 4 * 2048)
    w_dt = jnp.pad(w[:, DT_OFF:DT_OFF + SSD_HEADS], ((0, 0), (0, 128 - SSD_HEADS)))
    main = jnp.concatenate([ret, w[:, 8192:12288], w[:, DT_OFF + SSD_HEADS:], _xbc_group_major(w[:, 12288:DT_OFF])], axis=1)
    return main, w_dt


def _w_in_grad_full(g_main, g_dt):
    D = g_main.shape[0]
    ret = jnp.transpose(g_main[:, :C_Z].reshape(D, RET_HEADS, 4, RET_DK), (0, 2, 1, 3)).reshape(D, C_Z)
    return jnp.concatenate([ret, g_main[:, C_Z:C_GATES], _xbc_original(g_main[:, C_XBC:]), g_dt[:, :SSD_HEADS],
                            g_main[:, C_GATES:C_XBC]], axis=1)


def _mm(pairs, M, N, *, tm, tn, out_dtype, name, tb=False):
    P = len(pairs)
    nks = [K // tk for (_, _, _, _, _, K, tk) in pairs]
    starts = [int(s) for s in np.cumsum([0] + nks[:-1])]
    KT = int(sum(nks))
    in_specs, args = [], []
    for (a, a_cb, b, b_kb, b_nb, K, tk), s, nk in zip(pairs, starts, nks):
        def kk(k, s=s, nk=nk):
            return jnp.clip(k - s, 0, nk - 1)
        in_specs.append(pl.BlockSpec((tm, tk), lambda m, n, k, kk=kk, a_cb=a_cb: (m, a_cb + kk(k))))
        if tb:
            in_specs.append(pl.BlockSpec((tn, tk), lambda m, n, k, kk=kk, b_kb=b_kb, b_nb=b_nb: (b_nb + n, b_kb + kk(k))))
        else:
            in_specs.append(pl.BlockSpec((tk, tn), lambda m, n, k, kk=kk, b_kb=b_kb, b_nb=b_nb: (b_kb + kk(k), b_nb + n)))
        args += [a, b]

    def body(*refs):
        o_ref = refs[2 * P]
        k = pl.program_id(2)

        def prod(i):
            a = refs[2 * i][...].astype(bf16)
            b = refs[2 * i + 1][...].astype(bf16)
            return _nt(a, b) if tb else _nn(a, b)

        if KT == 1:
            o_ref[...] = prod(0).astype(out_dtype)
            return
        acc = refs[2 * P + 1]

        @pl.when(k == 0)
        def _():
            acc[...] = jnp.zeros_like(acc)

        for i in range(P):
            @pl.when((k >= starts[i]) & (k < starts[i] + nks[i]))
            def _(i=i):
                acc[...] += prod(i)

        @pl.when(k == KT - 1)
        def _():
            o_ref[...] = acc[...].astype(out_dtype)

    return pl.pallas_call(
        body, name=name, grid=(M // tm, N // tn, KT), in_specs=in_specs,
        out_specs=pl.BlockSpec((tm, tn), lambda m, n, k: (m, n)),
        out_shape=jax.ShapeDtypeStruct((M, N), out_dtype),
        scratch_shapes=[] if KT == 1 else [pltpu.VMEM((tm, tn), f32)],
        compiler_params=_params(("parallel", "parallel", "arbitrary")),
    )(*args)


def _mm1(a, b, *, tm, tn, tk, out_dtype, name, tb=False):
    M, K = a.shape
    N = b.shape[0] if tb else b.shape[1]
    return _mm([(a, 0, b, 0, 0, K, tk)], M, N, tm=tm, tn=tn, out_dtype=out_dtype, name=name, tb=tb)


RS = 16
CS = 32


def _for_strips(n_rows, fn, rs=RS, unroll=4):
    def step(s, carry):
        fn(pl.ds(pl.multiple_of(s * rs, rs), rs))
        return carry
    n = n_rows // rs
    lax.fori_loop(0, n, step, 0, unroll=min(unroll, n))


def _norm1_fwd(x, w, tr):
    S, D = x.shape

    def body(x_ref, w_ref, h_ref, ht_ref):
        def strip(rows):
            xv = x_ref[rows, :]
            r = lax.rsqrt(jnp.mean(xv * xv, axis=-1, keepdims=True) + EPS)
            h_ref[rows, :] = (xv * r * w_ref[...]).astype(bf16)
        _for_strips(tr, strip)
        ht_ref[...] = h_ref[...].T

    return pl.pallas_call(
        body, name="norm1_fwd", grid=(S // tr,),
        in_specs=[pl.BlockSpec((tr, D), lambda i: (i, 0)), pl.BlockSpec((1, D), lambda i: (0, 0))],
        out_specs=[pl.BlockSpec((tr, D), lambda i: (i, 0)), pl.BlockSpec((D, tr), lambda i: (0, i))],
        out_shape=[jax.ShapeDtypeStruct((S, D), bf16), jax.ShapeDtypeStruct((D, S), bf16)], compiler_params=_params(("parallel",)),
    )(x, w)


def _norm1_bwd(x, w, dh, dx2, tr):
    S, D = x.shape

    def body(x_ref, w_ref, dh_ref, dx2_ref, gx_ref, gw_ref, acc):
        @pl.when(pl.program_id(0) == 0)
        def _():
            acc[...] = jnp.zeros_like(acc)

        def strip(rows):
            xv = x_ref[rows, :]
            r = lax.rsqrt(jnp.mean(xv * xv, axis=-1, keepdims=True) + EPS)
            xh = xv * r
            dhv = dh_ref[rows, :]
            acc[...] += dhv * xh
            dxh = dhv * w_ref[...]
            gx_ref[rows, :] = dx2_ref[rows, :] + r * (dxh - xh * jnp.mean(dxh * xh, axis=-1, keepdims=True))
        _for_strips(tr, strip)

        @pl.when(pl.program_id(0) == S // tr - 1)
        def _():
            gw_ref[...] = jnp.sum(acc[...], axis=0, keepdims=True)

    row = pl.BlockSpec((tr, D), lambda i: (i, 0))
    vec = pl.BlockSpec((1, D), lambda i: (0, 0))
    return pl.pallas_call(
        body, name="norm1_bwd", grid=(S // tr,), in_specs=[row, vec, row, row], out_specs=[row, vec],
        out_shape=[jax.ShapeDtypeStruct((S, D), f32), jax.ShapeDtypeStruct((1, D), f32)],
        scratch_shapes=[pltpu.VMEM((RS, D), f32)], compiler_params=_params(("arbitrary",)),
    )(x, w, dh, dx2)


def _final_fwd_bwd(x, mo, target, wf, tr):
    S, D = x.shape

    def body(x_ref, mo_ref, t_ref, w_ref, dx2_ref, dx2b_ref, loss_ref, gw_ref, acc, lacc):
        @pl.when(pl.program_id(0) == 0)
        def _():
            acc[...] = jnp.zeros_like(acc)
            lacc[...] = jnp.zeros_like(lacc)

        def strip(rows):
            x2 = x_ref[rows, :] + mo_ref[rows, :]
            r = lax.rsqrt(jnp.mean(x2 * x2, axis=-1, keepdims=True) + EPS)
            xh = x2 * r
            wv = w_ref[...]
            err = xh * wv - t_ref[rows, :]
            lacc[...] += jnp.mean(err * err, axis=-1, keepdims=True)
            dy = err * (1.0 / D)
            acc[...] += dy * xh
            dxh = dy * wv
            dx2 = r * (dxh - xh * jnp.mean(dxh * xh, axis=-1, keepdims=True))
            dx2_ref[rows, :] = dx2
            dx2b_ref[rows, :] = dx2.astype(bf16)
        _for_strips(tr, strip)

        @pl.when(pl.program_id(0) == S // tr - 1)
        def _():
            gw_ref[...] = jnp.sum(acc[...], axis=0, keepdims=True)
            loss_ref[...] = 0.5 * jnp.sum(lacc[...], axis=0, keepdims=True)

    row = pl.BlockSpec((tr, D), lambda i: (i, 0))
    vec = pl.BlockSpec((1, D), lambda i: (0, 0))
    return pl.pallas_call(
        body, name="final_norm_loss", grid=(S // tr,), in_specs=[row, row, row, vec],
        out_specs=[row, row, pl.BlockSpec((1, 1), lambda i: (0, 0)), vec],
        out_shape=[jax.ShapeDtypeStruct((S, D), f32), jax.ShapeDtypeStruct((S, D), bf16), jax.ShapeDtypeStruct((1, 1), f32),
                   jax.ShapeDtypeStruct((1, D), f32)],
        scratch_shapes=[pltpu.VMEM((RS, D), f32), pltpu.VMEM((RS, 1), f32)], compiler_params=_params(("arbitrary",)),
    )(x, mo, target, wf)


def _merge_fwd(p_r, p_s, proj, tr):
    S, D = p_r.shape

    def body(pr_ref, ps_ref, g_ref, o_ref, ot_ref):
        def strip(rows):
            gr, gs = g_ref[rows, pl.ds(0, D)], g_ref[rows, pl.ds(D, D)]
            o_ref[rows, :] = (jax.nn.sigmoid(gr) * pr_ref[rows, :] + jax.nn.sigmoid(gs) * ps_ref[rows, :]).astype(bf16)
        _for_strips(tr, strip)
        ot_ref[...] = o_ref[...].T

    row = pl.BlockSpec((tr, D), lambda i: (i, 0))
    return pl.pallas_call(
        body, name="merge_fwd", grid=(S // tr,),
        in_specs=[row, row, pl.BlockSpec((tr, 2 * D), lambda i: (i, C_GATES // (2 * D)))],
        out_specs=[row, pl.BlockSpec((D, tr), lambda i: (0, i))],
        out_shape=[jax.ShapeDtypeStruct((S, D), bf16), jax.ShapeDtypeStruct((D, S), bf16)], compiler_params=_params(("parallel",)),
    )(p_r, p_s, proj)


def _merge_bwd(dm, p_r, p_s, proj, tr):
    S, D = p_r.shape

    def body(dm_ref, pr_ref, ps_ref, g_ref, dpr_ref, dps_ref, dproj_ref):
        def strip(rows):
            dmv = dm_ref[rows, :]
            sr = jax.nn.sigmoid(g_ref[rows, pl.ds(0, D)])
            ss = jax.nn.sigmoid(g_ref[rows, pl.ds(D, D)])
            dpr_ref[rows, :] = (dmv * sr).astype(bf16)
            dps_ref[rows, :] = (dmv * ss).astype(bf16)
            dproj_ref[rows, pl.ds(0, D)] = (dmv * pr_ref[rows, :] * sr * (1.0 - sr)).astype(bf16)
            dproj_ref[rows, pl.ds(D, D)] = (dmv * ps_ref[rows, :] * ss * (1.0 - ss)).astype(bf16)
        _for_strips(tr, strip)

    row = pl.BlockSpec((tr, D), lambda i: (i, 0))
    gates = pl.BlockSpec((tr, 2 * D), lambda i: (i, C_GATES // (2 * D)))
    o = jax.ShapeDtypeStruct((S, D), bf16)
    return pl.pallas_call(
        body, name="merge_bwd", grid=(S // tr,), in_specs=[row, row, row, gates],
        out_specs=[row, row, gates], out_shape=[o, o, jax.ShapeDtypeStruct((S, N_MAIN), bf16)],
        compiler_params=_params(("parallel",)),
    )(dm, p_r, p_s, proj)


def _ssd_norm_fwd(y, proj, w, tr):
    S, W = y.shape

    def body(y_ref, z_ref, w_ref, o_ref, ot_ref):
        def strip(rows):
            u = y_ref[rows, :] * _silu(z_ref[rows, :])
            r = lax.rsqrt(jnp.mean(u * u, axis=-1, keepdims=True) + EPS)
            o_ref[rows, :] = (u * r * w_ref[...]).astype(bf16)
        _for_strips(tr, strip)
        ot_ref[...] = o_ref[...].T

    row = pl.BlockSpec((tr, W), lambda i: (i, 0))
    return pl.pallas_call(
        body, name="ssd_norm_fwd", grid=(S // tr,),
        in_specs=[row, pl.BlockSpec((tr, W), lambda i: (i, C_Z // W)), pl.BlockSpec((1, W), lambda i: (0, 0))],
        out_specs=[row, pl.BlockSpec((W, tr), lambda i: (0, i))],
        out_shape=[jax.ShapeDtypeStruct((S, W), bf16), jax.ShapeDtypeStruct((W, S), bf16)], compiler_params=_params(("parallel",)),
    )(y, proj, w)


def _ssd_norm_bwd(y, proj, w, dys, dproj, tr):
    S, W = y.shape

    def body(y_ref, z_ref, w_ref, d_ref, _, dy_ref, dz_ref, gw_ref, acc):
        @pl.when(pl.program_id(0) == 0)
        def _():
            acc[...] = jnp.zeros_like(acc)

        def strip(rows):
            yv, zv, dv = y_ref[rows, :], z_ref[rows, :], d_ref[rows, :]
            sz = _silu(zv)
            u = yv * sz
            r = lax.rsqrt(jnp.mean(u * u, axis=-1, keepdims=True) + EPS)
            un = u * r
            acc[...] += dv * un
            dun = dv * w_ref[...]
            du = r * (dun - un * jnp.mean(dun * un, axis=-1, keepdims=True))
            dy_ref[rows, :] = du * sz
            dz_ref[rows, :] = (du * yv * _dsilu(zv)).astype(bf16)
        _for_strips(tr, strip)

        @pl.when(pl.program_id(0) == S // tr - 1)
        def _():
            gw_ref[...] = jnp.sum(acc[...], axis=0, keepdims=True)

    row = pl.BlockSpec((tr, W), lambda i: (i, 0))
    zcol = pl.BlockSpec((tr, W), lambda i: (i, C_Z // W))
    vec = pl.BlockSpec((1, W), lambda i: (0, 0))
    return pl.pallas_call(
        body, name="ssd_norm_bwd", grid=(S // tr,),
        in_specs=[row, zcol, vec, row, ANY], out_specs=[row, zcol, vec],
        out_shape=[jax.ShapeDtypeStruct((S, W), f32), jax.ShapeDtypeStruct(dproj.shape, bf16), jax.ShapeDtypeStruct((1, W), f32)],
        input_output_aliases={4: 1}, scratch_shapes=[pltpu.VMEM((RS, W), f32)], compiler_params=_params(("arbitrary",)),
    )(y, proj, w, dys, dproj)


def _rope(t, cos, sin):
    t1, t2 = t[:, :128], t[:, 128:]
    return jnp.concatenate([t1 * cos - t2 * sin, t2 * cos + t1 * sin], axis=1)


def _rope_t(d, cos, sin):
    d1, d2 = d[:, :128], d[:, 128:]
    return jnp.concatenate([d1 * cos + d2 * sin, d2 * cos - d1 * sin], axis=1)


def _ret_specs(tb, rev_nb=None):
    def blk(i):
        return i if rev_nb is None else rev_nb - 1 - i
    head = pl.BlockSpec((tb, RET_HW), lambda h, i: (blk(i), h))
    tab = pl.BlockSpec((tb, 128), lambda h, i: (blk(i), 0))
    mat = pl.BlockSpec((1, CHUNK, CHUNK), lambda h, i: (h, 0, 0))
    vec = pl.BlockSpec((1, CHUNK, 1), lambda h, i: (h, 0, 0))
    one = pl.BlockSpec((1, 1, 1), lambda h, i: (h, 0, 0))
    own = pl.BlockSpec((tb, RET_DK), lambda h, i: (blk(i), h))
    st = pl.BlockSpec((1, tb // CHUNK, RET_DK, RET_DK), lambda h, i: (h, blk(i), 0, 0))
    return head, tab, mat, vec, one, own, st


def _ret_fwd(proj, cos, sin, intra, qdec, kdec, cdec, tb):
    S = proj.shape[0]
    nc = S // CHUNK
    scale = RET_DK ** -0.5
    dk = RET_DK

    def body(p_ref, cos_ref, sin_ref, m_ref, qd_ref, kd_ref, cd_ref, y_ref, yr_ref, yrt_ref, st_ref, st):
        @pl.when(pl.program_id(1) == 0)
        def _():
            st[...] = jnp.zeros_like(st)

        mm, qd, kd, cd = m_ref[0], qd_ref[0], kd_ref[0], cd_ref[0]

        def chunk(c, carry):
            rows = pl.ds(pl.multiple_of(c * CHUNK, CHUNK), CHUNK)
            cs, sn = cos_ref[rows, :], sin_ref[rows, :]
            qr = _rope(p_ref[rows, pl.ds(0, dk)], cs, sn)
            kr = _rope(p_ref[rows, pl.ds(dk, dk)], cs, sn) * scale
            qb, kb, vb = qr.astype(bf16), kr.astype(bf16), p_ref[rows, pl.ds(2 * dk, dk)].astype(bf16)
            stb = st[...].astype(bf16)
            st_ref[0, c] = stb
            sc = (_nt(qb, kb) * mm).astype(bf16)
            y = _nn(sc, vb) + _nn(qb, stb) * qd
            st[...] = st[...] * cd + _tn((kr * kd).astype(bf16), vb)
            y_ref[rows, :] = y
            mu = jnp.mean(y, axis=-1, keepdims=True)
            yc = y - mu
            var = jnp.mean(yc * yc, axis=-1, keepdims=True)
            yr_ref[rows, :] = (yc * lax.rsqrt(var + EPS) * _silu(p_ref[rows, pl.ds(3 * dk, dk)])).astype(bf16)
            return carry

        lax.fori_loop(0, tb // CHUNK, chunk, 0, unroll=2)
        yrt_ref[...] = yr_ref[...].T

    head, tab, mat, vec, one, own, stspec = _ret_specs(tb)
    return pl.pallas_call(
        body, name="ret_fwd", grid=(RET_HEADS, S // tb),
        in_specs=[head, tab, tab, mat, vec, vec, one],
        out_specs=[own, own, pl.BlockSpec((RET_DK, tb), lambda h, i: (h, i)), stspec],
        out_shape=[jax.ShapeDtypeStruct((S, 2048), f32), jax.ShapeDtypeStruct((S, 2048), bf16), jax.ShapeDtypeStruct((2048, S), bf16),
                   jax.ShapeDtypeStruct((RET_HEADS, nc, dk, dk), bf16)],
        scratch_shapes=[pltpu.VMEM((dk, dk), f32)], compiler_params=_params(("parallel", "arbitrary")),
    )(proj, cos, sin, intra, qdec, kdec, cdec)


def _ret_bwd(proj, cos, sin, intra, qdec, kdec, cdec, y, dyr, states, dproj, tb):
    S = proj.shape[0]
    nb = S // tb
    nck = tb // CHUNK
    scale = RET_DK ** -0.5
    dk = RET_DK

    def body(p_ref, cos_ref, sin_ref, m_ref, qd_ref, kd_ref, cd_ref, y_ref, dyr_ref, st_ref, _, o_ref, dst):
        @pl.when(pl.program_id(1) == 0)
        def _():
            dst[...] = jnp.zeros_like(dst)

        mm, qd, kd, cd = m_ref[0], qd_ref[0], kd_ref[0], cd_ref[0]

        def chunk(cc, carry):
            c = nck - 1 - cc
            rows = pl.ds(pl.multiple_of(c * CHUNK, CHUNK), CHUNK)
            cs, sn = cos_ref[rows, :], sin_ref[rows, :]
            qr = _rope(p_ref[rows, pl.ds(0, dk)], cs, sn)
            kr = _rope(p_ref[rows, pl.ds(dk, dk)], cs, sn) * scale
            qb, kb, vb = qr.astype(bf16), kr.astype(bf16), p_ref[rows, pl.ds(2 * dk, dk)].astype(bf16)
            kdb = (kr * kd).astype(bf16)
            stb = st_ref[0, c]
            yv, gv, dyrv = y_ref[rows, :], p_ref[rows, pl.ds(3 * dk, dk)], dyr_ref[rows, :]
            mu = jnp.mean(yv, axis=-1, keepdims=True)
            yc = yv - mu
            rstd = lax.rsqrt(jnp.mean(yc * yc, axis=-1, keepdims=True) + EPS)
            yn = yc * rstd
            o_ref[rows, pl.ds(3 * dk, dk)] = (dyrv * yn * _dsilu(gv)).astype(bf16)
            dyn = dyrv * _silu(gv)
            dy = rstd * (dyn - jnp.mean(dyn, axis=-1, keepdims=True) - yn * jnp.mean(dyn * yn, axis=-1, keepdims=True))
            dyb = dy.astype(bf16)
            dyqb = (dy * qd).astype(bf16)
            dstb = dst[...].astype(bf16)
            sct = (_nt(kb, qb) * mm).astype(bf16)
            ds = (_nt(dyb, vb) * mm).astype(bf16)
            dsT = (_nt(vb, dyb) * mm).astype(bf16)
            dv = _nn(sct, dyb) + _nn(kdb, dstb)
            dqr = _nn(ds, kb) + _nt(dyqb, stb)
            dkr = _nn(dsT, qb) + _nt(vb, dstb) * kd
            dst[...] = dst[...] * cd + _tn(qb, dyqb)
            o_ref[rows, pl.ds(0, dk)] = _rope_t(dqr, cs, sn).astype(bf16)
            o_ref[rows, pl.ds(dk, dk)] = (_rope_t(dkr, cs, sn) * scale).astype(bf16)
            o_ref[rows, pl.ds(2 * dk, dk)] = dv.astype(bf16)
            return carry

        lax.fori_loop(0, nck, chunk, 0, unroll=2)

    head, tab, mat, vec, one, own, stspec = _ret_specs(tb, rev_nb=nb)
    return pl.pallas_call(
        body, name="ret_bwd", grid=(RET_HEADS, nb),
        in_specs=[head, tab, tab, mat, vec, vec, one, own, own, stspec, ANY],
        out_specs=head, out_shape=jax.ShapeDtypeStruct(dproj.shape, bf16), input_output_aliases={10: 0},
        scratch_shapes=[pltpu.VMEM((dk, dk), f32)], compiler_params=_params(("parallel", "arbitrary")),
    )(proj, cos, sin, intra, qdec, kdec, cdec, y, dyr, states, dproj)


def _conv_fwd(proj, conv_w, conv_b, tb, cw):
    S = proj.shape[0]
    off = C_XBC // cw

    def body(x_ref, halo_ref, w_ref, b_ref, o_ref, xe):
        xe[pl.ds(0, 8), :] = jnp.where(pl.program_id(1) == 0, 0.0, halo_ref[...])
        xe[pl.ds(8, CS), :] = x_ref[pl.ds(0, CS), :]
        ws = [w_ref[pl.ds(j, 1), :] for j in range(SSD_CONV)]
        for s in range(tb // CS):
            tap = (lambda j: xe[pl.ds(5 + j, CS), :]) if s == 0 else (lambda j, s=s: x_ref[pl.ds(s * CS - 3 + j, CS), :])
            acc = b_ref[...] + ws[0] * tap(0)
            for j in range(1, SSD_CONV):
                acc = acc + ws[j] * tap(j)
            o_ref[pl.ds(s * CS, CS), :] = acc

    return pl.pallas_call(
        body, name="conv_fwd", grid=(CONV_DIM // cw, S // tb),
        in_specs=[pl.BlockSpec((tb, cw), lambda j, i: (i, off + j)),
                  pl.BlockSpec((8, cw), lambda j, i: (jnp.maximum(i * (tb // 8) - 1, 0), off + j)),
                  pl.BlockSpec((SSD_CONV, cw), lambda j, i: (0, j)), pl.BlockSpec((1, cw), lambda j, i: (0, j))],
        out_specs=pl.BlockSpec((tb, cw), lambda j, i: (i, j)),
        out_shape=jax.ShapeDtypeStruct((S, CONV_DIM), f32),
        scratch_shapes=[pltpu.VMEM((CS + 8, cw), f32)], compiler_params=_params(("parallel", "arbitrary")),
    )(proj, proj, conv_w, conv_b)


def _conv_bwd(dpre, proj, conv_w, dproj, tb, cw):
    S, n = dpre.shape
    nb = S // tb
    xoff = C_XBC // cw

    def body(d_ref, dh_ref, x_ref, xh_ref, w_ref, _, dx_ref, gw_ref, gb_ref, de, xe, accw, accb):
        i = pl.program_id(1)

        @pl.when(i == 0)
        def _():
            accw[...] = jnp.zeros_like(accw)
            accb[...] = jnp.zeros_like(accb)

        ns = tb // CS
        de[pl.ds(0, CS), :] = d_ref[pl.ds(tb - CS, CS), :]
        de[pl.ds(CS, 8), :] = jnp.where(i == nb - 1, 0.0, dh_ref[...])
        xe[pl.ds(0, 8), :] = jnp.where(i == 0, 0.0, xh_ref[...])
        xe[pl.ds(8, CS), :] = x_ref[pl.ds(0, CS), :]
        ws = [w_ref[pl.ds(j, 1), :] for j in range(SSD_CONV)]
        fold = lambda p: sum(p[8 * q:8 * (q + 1)] for q in range(1, CS // 8)) + p[0:8]
        for s in range(ns):
            dv = d_ref[pl.ds(s * CS, CS), :]
            ahead = (lambda o: de[pl.ds(o, CS), :]) if s == ns - 1 else (lambda o, s=s: d_ref[pl.ds(s * CS + o, CS), :])
            xtap = (lambda j: xe[pl.ds(5 + j, CS), :]) if s == 0 else (lambda j, s=s: x_ref[pl.ds(s * CS - 3 + j, CS), :])
            acc = ws[SSD_CONV - 1] * dv
            for j in range(SSD_CONV - 1):
                acc = acc + ws[j] * ahead(3 - j)
            dx_ref[pl.ds(s * CS, CS), :] = acc.astype(bf16)
            accb[...] += fold(dv)
            for j in range(SSD_CONV):
                accw[j] += fold(dv * xtap(j))

        @pl.when(i == nb - 1)
        def _():
            gb_ref[...] = jnp.sum(accb[...], axis=0, keepdims=True)
            for j in range(SSD_CONV):
                gw_ref[pl.ds(j, 1), :] = jnp.sum(accw[j], axis=0, keepdims=True)

    return pl.pallas_call(
        body, name="conv_bwd", grid=(n // cw, nb),
        in_specs=[pl.BlockSpec((tb, cw), lambda j, i: (i, j)),
                  pl.BlockSpec((8, cw), lambda j, i: (jnp.minimum((i + 1) * (tb // 8), S // 8 - 1), j)),
                  pl.BlockSpec((tb, cw), lambda j, i: (i, xoff + j)),
                  pl.BlockSpec((8, cw), lambda j, i: (jnp.maximum(i * (tb // 8) - 1, 0), xoff + j)),
                  pl.BlockSpec((SSD_CONV, cw), lambda j, i: (0, j)), ANY],
        out_specs=[pl.BlockSpec((tb, cw), lambda j, i: (i, xoff + j)), pl.BlockSpec((SSD_CONV, cw), lambda j, i: (0, j)),
                   pl.BlockSpec((1, cw), lambda j, i: (0, j))],
        out_shape=[jax.ShapeDtypeStruct(dproj.shape, bf16), jax.ShapeDtypeStruct((SSD_CONV, n), f32), jax.ShapeDtypeStruct((1, n), f32)],
        input_output_aliases={5: 0},
        scratch_shapes=[pltpu.VMEM((CS + 8, cw), f32), pltpu.VMEM((CS + 8, cw), f32), pltpu.VMEM((SSD_CONV, 8, cw), f32),
                        pltpu.VMEM((8, cw), f32)],
        compiler_params=_params(("parallel", "arbitrary")),
    )(dpre, dpre, proj, proj, conv_w, dproj)


def _dt_prep(dt_raw, dt_bias, a_log, tb):
    S = dt_raw.shape[0]

    def body(r_ref, b_ref, al_ref, dt_ref, sg_ref, ac_ref):
        li = lax.broadcasted_iota(jnp.int32, (LS, LS), 0)
        si = lax.broadcasted_iota(jnp.int32, (LS, LS), 1)
        tri = (li >= si).astype(f32)
        neg_a = -jnp.exp(al_ref[...])
        for c in range(tb // LS):
            rows = pl.ds(c * LS, LS)
            xv = r_ref[rows, :] + b_ref[...]
            dtv = jax.nn.softplus(xv)
            dt_ref[rows, :] = dtv
            sg_ref[rows, :] = jax.nn.sigmoid(xv)
            ac_ref[rows, :] = _hi(tri, dtv * neg_a)

    row = pl.BlockSpec((tb, 128), lambda i: (i, 0))
    vec = pl.BlockSpec((1, 128), lambda i: (0, 0))
    o = jax.ShapeDtypeStruct((S, 128), f32)
    return pl.pallas_call(body, name="dt_prep", grid=(S // tb,), in_specs=[row, vec, vec], out_specs=[row, row, row],
                          out_shape=[o, o, o], compiler_params=_params(("parallel",)))(dt_raw, dt_bias, a_log)


def _group_major(t):
    S = t.shape[0]
    return jnp.transpose(t[:, :SSD_HEADS].reshape(S, SSD_GROUPS, SSD_HPG), (1, 0, 2))


def _group_major_t(t):
    S = t.shape[0]
    return jnp.transpose(t[:, :SSD_HEADS].reshape(S // LS, LS, SSD_GROUPS, SSD_HPG), (2, 0, 3, 1))


def _ssd_specs(tb, rev_nb=None):
    def blk(i):
        return i if rev_nb is None else rev_nb - 1 - i
    grp = pl.BlockSpec((tb, SSD_GC), lambda g, i: (blk(i), g))
    xs = pl.BlockSpec((tb, SSD_GW), lambda g, i: (blk(i), g))
    ph = pl.BlockSpec((1, tb, SSD_HPG), lambda g, i: (g, blk(i), 0))
    pht = pl.BlockSpec((1, tb // LS, SSD_HPG, LS), lambda g, i: (g, blk(i), 0, 0))
    gvec = pl.BlockSpec((1, 1, SSD_GW), lambda g, i: (g, 0, 0))
    ex = pl.BlockSpec((SSD_HPG, SSD_GW), lambda g, i: (0, 0))
    st = pl.BlockSpec((1, tb // LS, SSD_STATE, SSD_GW), lambda g, i: (g, blk(i), 0, 0))
    return grp, xs, ph, pht, gvec, ex, st


def _expander():
    return jnp.repeat(jnp.eye(SSD_HPG, dtype=f32), SSD_GW // SSD_HPG, axis=1).astype(bf16)


def _expand3(dt8, ac8, ex):
    stack = jnp.concatenate([dt8, jnp.exp(ac8), jnp.exp(ac8[LS - 1:LS, :] - ac8)], axis=0)
    wide = _sel_r(stack, ex)
    return wide[0:LS], wide[LS:2 * LS], wide[2 * LS:3 * LS]


def _ssd_fwd(pre, dt_g, ac_g, act_g, dskx, tb):
    S = pre.shape[0]
    nc = S // LS
    hd = SSD_GW // SSD_HPG

    def body(p_ref, dt_ref, ac_ref, act_ref, dsk_ref, ex_ref, y_ref, st_ref, st):
        @pl.when(pl.program_id(1) == 0)
        def _():
            st[...] = jnp.zeros_like(st)

        ex = ex_ref[...]
        li = lax.broadcasted_iota(jnp.int32, (LS, LS), 0)
        si = lax.broadcasted_iota(jnp.int32, (LS, LS), 1)
        causal = li >= si

        def chunk(c, carry):
            rows = pl.ds(pl.multiple_of(c * LS, LS), LS)
            xs = _silu(p_ref[rows, pl.ds(0, SSD_GW)])
            bcb = _silu(p_ref[rows, pl.ds(SSD_GW, SSD_STATE)]).astype(bf16)
            ccb = _silu(p_ref[rows, pl.ds(SSD_GW + SSD_STATE, SSD_STATE)]).astype(bf16)
            dt8, ac8, act = dt_ref[0, rows, :], ac_ref[0, rows, :], act_ref[0, c]
            dtx, eax, tailx = _expand3(dt8, ac8, ex)
            xdt = xs * dtx
            cb = _nt(ccb, bcb)
            stb = st[...].astype(bf16)
            st_ref[0, c] = stb
            xdtb = xdt.astype(bf16)
            outs = []
            for h in range(SSD_HPG):
                dec = jnp.exp(jnp.where(causal, ac8[:, h:h + 1] - act[h:h + 1, :], -1e30))
                outs.append(_nn((cb * dec).astype(bf16), xdtb[:, hd * h:hd * (h + 1)]))
            y_ref[rows, :] = jnp.concatenate(outs, axis=1) + _nn(ccb, stb) * eax + dsk_ref[0] * xs
            st[...] = st[...] * eax[LS - 1:LS, :] + _tn(bcb, (xdt * tailx).astype(bf16))
            return carry

        lax.fori_loop(0, tb // LS, chunk, 0)

    grp, xs, ph, pht, gvec, ex, stspec = _ssd_specs(tb)
    return pl.pallas_call(
        body, name="ssd_fwd", grid=(SSD_GROUPS, S // tb),
        in_specs=[grp, ph, ph, pht, gvec, ex], out_specs=[xs, stspec],
        out_shape=[jax.ShapeDtypeStruct((S, SSD_WIDTH), f32), jax.ShapeDtypeStruct((SSD_GROUPS, nc, SSD_STATE, SSD_GW), bf16)],
        scratch_shapes=[pltpu.VMEM((SSD_STATE, SSD_GW), f32)], compiler_params=_params(("parallel", "arbitrary")),
    )(pre, dt_g, ac_g, act_g, dskx, _expander())


def _ssd_bwd(pre, dt_g, ac_g, act_g, sg_g, dskx, nega_g, dy, states, tb):
    S = pre.shape[0]
    nb = S // tb
    nck = tb // LS
    hd = SSD_GW // SSD_HPG

    def body(p_ref, dt_ref, ac_ref, act_ref, sg_ref, dsk_ref, na_ref, ex_ref, ext_ref, dy_ref, st_ref,
             dp_ref, ddt_ref, gsk_ref, gal_ref, gdb_ref, dst, skacc):
        @pl.when(pl.program_id(1) == 0)
        def _():
            dst[...] = jnp.zeros_like(dst)
            skacc[...] = jnp.zeros_like(skacc)
            gal_ref[...] = jnp.zeros_like(gal_ref)
            gdb_ref[...] = jnp.zeros_like(gdb_ref)

        ex, ext = ex_ref[...], ext_ref[...]
        li = lax.broadcasted_iota(jnp.int32, (LS, LS), 0)
        si = lax.broadcasted_iota(jnp.int32, (LS, LS), 1)
        causal = li >= si
        anti = si >= li
        upper = anti.astype(bf16)
        last_row = (lax.broadcasted_iota(jnp.int32, (LS, 1), 0) == LS - 1).astype(f32)
        head_id = lax.broadcasted_iota(jnp.int32, (1, SSD_HPG), 1)
        neg_a = na_ref[0]
        dskv = dsk_ref[0]

        def chunk(cc, carry):
            c = nck - 1 - cc
            rows = pl.ds(pl.multiple_of(c * LS, LS), LS)
            px = p_ref[rows, pl.ds(0, SSD_GW)]
            pb = p_ref[rows, pl.ds(SSD_GW, SSD_STATE)]
            pc = p_ref[rows, pl.ds(SSD_GW + SSD_STATE, SSD_STATE)]
            xs = _silu(px)
            bcb = _silu(pb).astype(bf16)
            ccb = _silu(pc).astype(bf16)
            dt8, ac8, act = dt_ref[0, rows, :], ac_ref[0, rows, :], act_ref[0, c]
            dtx, eax, tailx = _expand3(dt8, ac8, ex)
            xdt = xs * dtx
            ex_last = eax[LS - 1:LS, :]
            stb = st_ref[0, c]
            dyv = dy_ref[rows, :]
            dyb = dyv.astype(bf16)
            xdtb = xdt.astype(bf16)
            skacc[...] += jnp.sum(dyv * xs, axis=0, keepdims=True)
            yinter = _nn(ccb, stb) * eax
            dzb = (dyv * eax).astype(bf16)
            dcc = _nt(dzb, stb)
            dstv = dst[...]
            dstb = dstv.astype(bf16)
            xt = xdt * tailx
            dxt = _nn(bcb, dstb)
            dbc = _nt(xt.astype(bf16), dstb)
            dxdt = dxt * tailx
            lastrow = jnp.sum(dxt * xt, axis=0, keepdims=True) + jnp.sum(dstv * stb.astype(f32), axis=0, keepdims=True) * ex_last
            dst[...] = dstv * ex_last + _tn(ccb, dzb)
            cb = _nt(ccb, bcb)
            cbt = _nt(bcb, ccb)
            dcb = jnp.zeros((LS, LS), f32)
            dcbt = jnp.zeros((LS, LS), f32)
            dac8 = jnp.zeros((LS, SSD_HPG), f32)
            dxin = []
            for h in range(SSD_HPG):
                sl = slice(hd * h, hd * (h + 1))
                col, rowv = ac8[:, h:h + 1], act[h:h + 1, :]
                dec = jnp.exp(jnp.where(causal, col - rowv, -1e30))
                dect = jnp.exp(jnp.where(anti, rowv - col, -1e30))
                gm, gmt = cb * dec, cbt * dect
                dgm, dgmt = _nt(dyb[:, sl], xdtb[:, sl]), _nt(xdtb[:, sl], dyb[:, sl])
                dxin.append(_nn(gmt.astype(bf16), dyb[:, sl]))
                dcb = dcb + dgm * dec
                dcbt = dcbt + dgmt * dect
                dcol = jnp.sum(dgm * gm, axis=1, keepdims=True) - jnp.sum(dgmt * gmt, axis=1, keepdims=True)
                dac8 = dac8 + dcol * (head_id == h).astype(f32)
            dxintra = jnp.concatenate(dxin, axis=1)
            dcc = dcc + _nn(dcb.astype(bf16), bcb)
            dbc = dbc + _nn(dcbt.astype(bf16), ccb)
            dxdt = dxdt + dxintra
            dacx = dyv * yinter - dxt * xt + last_row * lastrow
            red = _sel_r(jnp.concatenate([dacx, dxdt * xs], axis=0), ext)
            dac8 = dac8 + red[0:LS]
            da8 = _sel_l(upper, dac8)
            ddt8 = red[LS:2 * LS] + da8 * neg_a
            gal_ref[0] += jnp.sum(da8 * dt8 * neg_a, axis=0, keepdims=True)
            ddr = ddt8 * sg_ref[0, rows, :]
            ddt_ref[0, rows, :] = ddr
            gdb_ref[0] += jnp.sum(ddr, axis=0, keepdims=True)
            dp_ref[rows, pl.ds(0, SSD_GW)] = (dskv * dyv + dxdt * dtx) * _dsilu(px)
            dp_ref[rows, pl.ds(SSD_GW, SSD_STATE)] = dbc * _dsilu(pb)
            dp_ref[rows, pl.ds(SSD_GW + SSD_STATE, SSD_STATE)] = dcc * _dsilu(pc)
            return carry

        lax.fori_loop(0, nck, chunk, 0)

        @pl.when(pl.program_id(1) == nb - 1)
        def _():
            gsk_ref[0] = skacc[...]

    grp, xs, ph, pht, gvec, ex, stspec = _ssd_specs(tb, rev_nb=nb)
    small = pl.BlockSpec((1, 1, SSD_HPG), lambda g, i: (g, 0, 0))
    ext = pl.BlockSpec((SSD_GW, SSD_HPG), lambda g, i: (0, 0))
    sm = jax.ShapeDtypeStruct((SSD_GROUPS, 1, SSD_HPG), f32)
    expander = _expander()
    return pl.pallas_call(
        body, name="ssd_bwd", grid=(SSD_GROUPS, nb),
        in_specs=[grp, ph, ph, pht, ph, gvec, small, ex, ext, xs, stspec],
        out_specs=[grp, ph, gvec, small, small],
        out_shape=[jax.ShapeDtypeStruct((S, CONV_DIM), f32), jax.ShapeDtypeStruct((SSD_GROUPS, S, SSD_HPG), f32),
                   jax.ShapeDtypeStruct((SSD_GROUPS, 1, SSD_GW), f32), sm, sm],
        scratch_shapes=[pltpu.VMEM((SSD_STATE, SSD_GW), f32), pltpu.VMEM((1, SSD_GW), f32)],
        compiler_params=_params(("parallel", "arbitrary")),
    )(pre, dt_g, ac_g, act_g, sg_g, dskx, nega_g, expander, expander.T, dy, states)


def _tiles(S):
    return dict(tb=min(512, S), tr=min(256, S), tm=min(1024, S))


def _local_step(x, positions, target, norm1_w, w_main, w_dt, conv_w, conv_b, dt_bias, a_log, d_skip, ssd_norm_w,
                w_br, w_bs, w_o, norm_f_w):
    S, D = x.shape
    t = _tiles(S)
    tb, tr, tm = t["tb"], t["tr"], t["tm"]

    half = RET_DK // 2
    inv_freq = ROPE_THETA ** (-jnp.arange(half, dtype=f32) / half)
    ang = positions.astype(f32)[:, None] * inv_freq
    cos, sin = jnp.cos(ang), jnp.sin(ang)
    log_gamma = jnp.log1p(-(2.0 ** (-5.0 - jnp.arange(RET_HEADS, dtype=f32))))
    idx = jnp.arange(CHUNK, dtype=f32)
    intra = jnp.exp(jnp.abs(idx[:, None] - idx[None, :]) * log_gamma[:, None, None])
    qdec = jnp.exp((idx + 1.0)[None, :] * log_gamma[:, None])[:, :, None]
    kdec = jnp.exp((CHUNK - 1.0 - idx)[None, :] * log_gamma[:, None])[:, :, None]
    cdec = jnp.exp(CHUNK * log_gamma)[:, None, None]
    conv_wm, conv_bm = _xbc_group_major(conv_w), _xbc_group_major(conv_b)

    h, ht = _norm1_fwd(x, norm1_w, tr)
    proj = _mm1(h, w_main, tm=tm, tn=1024, tk=D, out_dtype=f32, name="proj_main")
    dt_raw = _mm1(h, w_dt, tm=tm, tn=128, tk=D, out_dtype=f32, name="proj_dt")
    y_ret, yr, yrt, ret_states = _ret_fwd(proj, cos, sin, intra, qdec, kdec, cdec, tb)
    pre = _conv_fwd(proj, conv_wm, conv_bm, tb, 512)
    pad64 = lambda v: jnp.pad(v, ((0, 0), (0, 128 - SSD_HEADS)))
    dt, sg, ac = _dt_prep(dt_raw, pad64(dt_bias), pad64(a_log), tb)
    dt_g, ac_g, sg_g, act_g = _group_major(dt), _group_major(ac), _group_major(sg), _group_major_t(ac)
    dskx = jnp.repeat(d_skip.reshape(SSD_GROUPS, 1, SSD_HPG), SSD_GW // SSD_HPG, axis=2)
    nega_g = (-jnp.exp(a_log)).reshape(SSD_GROUPS, 1, SSD_HPG)
    y_ssd, ssd_states = _ssd_fwd(pre, dt_g, ac_g, act_g, dskx, tb)
    ys, yst = _ssd_norm_fwd(y_ssd, proj, ssd_norm_w, tr // 2)
    p_r = _mm1(yr, w_br, tm=tm, tn=1024, tk=2048, out_dtype=f32, name="branch_ret")
    p_s = _mm1(ys, w_bs, tm=tm, tn=1024, tk=2048, out_dtype=f32, name="branch_ssd")
    merged, mergedt = _merge_fwd(p_r, p_s, proj, tr)
    mo = _mm1(merged, w_o, tm=tm, tn=1024, tk=2048, out_dtype=f32, name="out_proj")
    dx2, dx2b, loss, g_norm_f = _final_fwd_bwd(x, mo, target, norm_f_w.reshape(1, D), tr)

    tkt = min(4096, S)
    wg = lambda at, b, name, tn=1024: _mm1(at, b, tm=min(1024, at.shape[0]), tn=tn, tk=tkt, out_dtype=f32, name=name)
    dm = _mm1(dx2b, w_o, tm=tm, tn=1024, tk=2048, out_dtype=f32, name="d_merged", tb=True)
    g_w_o = wg(mergedt, dx2b, "g_w_out")
    dp_r, dp_s, dproj = _merge_bwd(dm, p_r, p_s, proj, tr)
    dyr = _mm1(dp_r, w_br, tm=tm, tn=1024, tk=2048, out_dtype=f32, name="d_yr", tb=True)
    dys = _mm1(dp_s, w_bs, tm=tm, tn=1024, tk=2048, out_dtype=f32, name="d_ys", tb=True)
    g_w_br = wg(yrt, dp_r, "g_w_br_ret")
    g_w_bs = wg(yst, dp_s, "g_w_br_ssd")
    dy_ssd, dproj, g_ssd_norm = _ssd_norm_bwd(y_ssd, proj, ssd_norm_w, dys, dproj, tr // 2)
    dproj = _ret_bwd(proj, cos, sin, intra, qdec, kdec, cdec, y_ret, dyr, ret_states, dproj, tb)
    dpre, ddt_g, gsk, gal, gdb = _ssd_bwd(pre, dt_g, ac_g, act_g, sg_g, dskx, nega_g, dy_ssd, ssd_states, tb)
    dproj, gcw, gcb = _conv_bwd(dpre, proj, conv_wm, dproj, tb, 512)
    ddt = jnp.transpose(ddt_g, (1, 0, 2)).reshape(S, SSD_HEADS)
    ddt_p = jnp.pad(ddt, ((0, 0), (0, 128 - SSD_HEADS))).astype(bf16)

    dh = _mm([(dproj, 0, w_main, 0, 0, N_MAIN, N_MAIN // 8), (ddt_p, 0, w_dt, 0, 0, 128, 128)], S, D, tm=tm, tn=1024,
             out_dtype=f32, name="d_h", tb=True)
    g_main = wg(ht, dproj, "g_w_in_main")
    g_dt = wg(ht, ddt_p, "g_w_in_dt", tn=128)
    grad_x, g_norm1 = _norm1_bwd(x, norm1_w, dh, dx2, tr)

    seg = lambda v: jnp.sum(v.reshape(SSD_HEADS, SSD_GW // SSD_HPG), axis=1).reshape(1, SSD_HEADS)
    grads = dict(
        norm1_w=g_norm1, w_in_main=g_main, w_in_dt=g_dt,
        conv_w=_xbc_original(gcw), conv_b=_xbc_original(gcb),
        dt_bias=gdb.reshape(1, SSD_HEADS), a_log=gal.reshape(1, SSD_HEADS), d_skip=seg(gsk),
        ssd_norm_w=g_ssd_norm, w_br_ret=g_w_br, w_br_ssd=g_w_bs, w_out=g_w_o, norm_f_w=g_norm_f,
    )
    return loss, grad_x, grads


def _me():
    return lax.axis_index("x"), lax.axis_index("y"), lax.axis_index("c")


def _other_chips(x, y):
    return [(1 - x, y), (x, 1 - y), (1 - x, 1 - y)]


def _gather_weights(a, b, cw):
    R = a.shape[0]
    hr = R // 2

    def body(a_ref, b_ref, cw_ref, ga_ref, gb_ref, gc_ref, send_sems, recv_sems):
        x, y, c = _me()
        k = 2 * x + y
        sibling = (x, y, 1 - c)
        chips = _other_chips(x, y)

        def small(j, src_shard, to):
            return pltpu.make_async_remote_copy(
                src_ref=cw_ref, dst_ref=gc_ref.at[src_shard], send_sem=send_sems.at[12 + j], recv_sem=recv_sems.at[12 + j],
                device_id=to, device_id_type=MESH)

        def copies(j, src_shard, half, to, from_input):
            rows = pl.ds(half * hr, hr)
            out = []
            for t, (inp, g) in enumerate(((a_ref, ga_ref), (b_ref, gb_ref))):
                src = inp.at[rows, :] if from_input else g.at[src_shard, rows, :]
                out.append(pltpu.make_async_remote_copy(
                    src_ref=src, dst_ref=g.at[src_shard, rows, :], send_sem=send_sems.at[2 * j + t],
                    recv_sem=recv_sems.at[2 * j + t], device_id=to, device_id_type=MESH))
            return out

        first = []
        for j, chip in enumerate(chips):
            first += copies(j, k, c, (*chip, c), True)
            first.append(small(j, k, (*chip, c)))
        for cp in first:
            cp.start()
        passed = []
        for j, chip in enumerate(chips):
            kk = 2 * chip[0] + chip[1]
            for cp in copies(j, kk, c, (x, y, c), False):
                cp.wait_recv()
            fw = copies(3 + j, kk, c, sibling, False)
            for cp in fw:
                cp.start()
            passed += fw
        for j, chip in enumerate(chips):
            kk = 2 * chip[0] + chip[1]
            for cp in copies(3 + j, kk, 1 - c, (x, y, c), False):
                cp.wait_recv()
            small(j, kk, (x, y, c)).wait_recv()
        for cp in first + passed:
            cp.wait_send()

    return pl.pallas_call(
        body, name="gather_weights", in_specs=[ANY, ANY, ANY], out_specs=[ANY, ANY, ANY],
        out_shape=[jax.ShapeDtypeStruct((N_SHARD,) + a.shape, a.dtype), jax.ShapeDtypeStruct((N_SHARD,) + b.shape, b.dtype),
                   jax.ShapeDtypeStruct((N_SHARD,) + cw.shape, cw.dtype)],
        scratch_shapes=[pltpu.SemaphoreType.DMA((15,)), pltpu.SemaphoreType.DMA((15,))],
        compiler_params=pltpu.CompilerParams(has_side_effects=True),
    )(a, b, cw)


def _sibling_swap(arrs, name):
    n = len(arrs)
    slots = [(t, s) for t in range(n) for s in range(arrs[t].shape[0])]

    def body(*refs):
        ins, outs = refs[:n], refs[n:2 * n]
        send_sems, recv_sems = refs[2 * n], refs[2 * n + 1]
        x, y, c = _me()
        cps = [pltpu.make_async_remote_copy(src_ref=ins[t].at[s, 1 - c], dst_ref=outs[t].at[s], send_sem=send_sems.at[q],
                                            recv_sem=recv_sems.at[q], device_id=(x, y, 1 - c), device_id_type=MESH)
               for q, (t, s) in enumerate(slots)]
        for cp in cps:
            cp.start()
        for cp in cps:
            cp.wait()

    return pl.pallas_call(
        body, name=name, in_specs=[ANY] * n, out_specs=[ANY] * n,
        out_shape=[jax.ShapeDtypeStruct(a.shape[:1] + a.shape[2:], a.dtype) for a in arrs],
        scratch_shapes=[pltpu.SemaphoreType.DMA((len(slots),)), pltpu.SemaphoreType.DMA((len(slots),))],
        compiler_params=pltpu.CompilerParams(has_side_effects=True),
    )(*arrs)


def _chip_exchange(arrs, name):
    n = len(arrs)

    def body(*refs):
        ins, outs = refs[:n], refs[n:2 * n]
        send_sems, recv_sems = refs[2 * n], refs[2 * n + 1]
        x, y, c = _me()
        cps = []
        for j, chip in enumerate(_other_chips(x, y)):
            kk = 2 * chip[0] + chip[1]
            for t in range(n):
                cps.append(pltpu.make_async_remote_copy(
                    src_ref=ins[t].at[kk], dst_ref=outs[t].at[j], send_sem=send_sems.at[n * j + t],
                    recv_sem=recv_sems.at[n * j + t], device_id=(*chip, c), device_id_type=MESH))
        for cp in cps:
            cp.start()
        for cp in cps:
            cp.wait()

    return pl.pallas_call(
        body, name=name, in_specs=[ANY] * n, out_specs=[ANY] * n,
        out_shape=[jax.ShapeDtypeStruct((3,) + a.shape[1:], a.dtype) for a in arrs],
        scratch_shapes=[pltpu.SemaphoreType.DMA((3 * n,)), pltpu.SemaphoreType.DMA((3 * n,))],
        compiler_params=pltpu.CompilerParams(has_side_effects=True),
    )(*arrs)


def _send_to_sibling(arrs, name):
    n = len(arrs)

    def body(*refs):
        ins, outs = refs[:n], refs[n:2 * n]
        send_sems, recv_sems = refs[2 * n], refs[2 * n + 1]
        x, y, c = _me()
        sends = [pltpu.make_async_remote_copy(src_ref=ins[t], dst_ref=outs[t].at[c], send_sem=send_sems.at[t], recv_sem=recv_sems.at[t],
                                              device_id=(x, y, 1 - c), device_id_type=MESH) for t in range(n)]
        for cp in sends:
            cp.start()
        for t in range(n):
            pltpu.make_async_remote_copy(src_ref=ins[t], dst_ref=outs[t].at[1 - c], send_sem=send_sems.at[t], recv_sem=recv_sems.at[t],
                                         device_id=(x, y, c), device_id_type=MESH).wait_recv()
        for cp in sends:
            cp.wait_send()

    return pl.pallas_call(
        body, name=name, in_specs=[ANY] * n, out_specs=[ANY] * n,
        out_shape=[jax.ShapeDtypeStruct((2,) + a.shape, a.dtype) for a in arrs],
        scratch_shapes=[pltpu.SemaphoreType.DMA((n,)), pltpu.SemaphoreType.DMA((n,))],
        compiler_params=pltpu.CompilerParams(has_side_effects=True),
    )(*arrs)


def _gather_vec(v):
    n = v.shape[1]

    def body(v_ref, o_ref, send_sems, recv_sems):
        x, y, c = _me()
        me = 4 * x + 2 * y + c
        cps = []
        for j in range(1, 8):
            fx, fy, fc = (j >> 2) & 1, (j >> 1) & 1, j & 1
            peer = (x ^ fx, y ^ fy, c ^ fc)
            cps.append(pltpu.make_async_remote_copy(
                src_ref=v_ref, dst_ref=o_ref.at[pl.ds(me, 1), :], send_sem=send_sems.at[j - 1], recv_sem=recv_sems.at[j - 1],
                device_id=peer, device_id_type=MESH))
        for cp in cps:
            cp.start()
        for j in range(1, 8):
            fx, fy, fc = (j >> 2) & 1, (j >> 1) & 1, j & 1
            src = 4 * (x ^ fx) + 2 * (y ^ fy) + (c ^ fc)
            pltpu.make_async_remote_copy(
                src_ref=v_ref, dst_ref=o_ref.at[pl.ds(src, 1), :], send_sem=send_sems.at[j - 1], recv_sem=recv_sems.at[j - 1],
                device_id=(x, y, c), device_id_type=MESH).wait_recv()
        for cp in cps:
            cp.wait_send()

    return pl.pallas_call(
        body, name="gather_vec", in_specs=[ANY], out_specs=ANY, out_shape=jax.ShapeDtypeStruct((8, n), v.dtype),
        scratch_shapes=[pltpu.SemaphoreType.DMA((7,)), pltpu.SemaphoreType.DMA((7,))],
        compiler_params=pltpu.CompilerParams(has_side_effects=True),
    )(v)


def _pair_sum(g, r, name, tr):
    L, _, hr, C = g.shape

    def body(c_ref, g_ref, r_ref, o_ref):
        def strip(rows):
            o_ref[0, rows, :] = (g_ref[0, 0, rows, :] + r_ref[0, rows, :]).astype(bf16)
        _for_strips(tr, strip)

    grid_spec = pltpu.PrefetchScalarGridSpec(
        num_scalar_prefetch=1, grid=(L, hr // tr),
        in_specs=[pl.BlockSpec((1, 1, tr, C), lambda s, i, c_ref: (s, c_ref[0], i, 0)),
                  pl.BlockSpec((1, tr, C), lambda s, i, c_ref: (s, i, 0))],
        out_specs=pl.BlockSpec((1, tr, C), lambda s, i, c_ref: (s, i, 0)))
    c = lax.axis_index("c").reshape(1).astype(jnp.int32)
    return pl.pallas_call(body, name=name, grid_spec=grid_spec, out_shape=jax.ShapeDtypeStruct((L, hr, C), bf16),
                          compiler_params=_params(("parallel", "parallel")))(c, g, r)


def _own_sum(p, got, name):
    _, hr, C = p.shape
    tr = SUM_ROWS

    def body(k_ref, p_ref, got_ref, o_ref):
        def strip(rows):
            o_ref[rows, :] = ((p_ref[0, rows, :].astype(f32) + got_ref[0, rows, :].astype(f32)) + got_ref[1, rows, :].astype(f32)) \
                + got_ref[2, rows, :].astype(f32)
        _for_strips(tr, strip)

    grid_spec = pltpu.PrefetchScalarGridSpec(
        num_scalar_prefetch=1, grid=(hr // tr,),
        in_specs=[pl.BlockSpec((1, tr, C), lambda i, k_ref: (k_ref[0], i, 0)), pl.BlockSpec((3, tr, C), lambda i, k_ref: (0, i, 0))],
        out_specs=pl.BlockSpec((tr, C), lambda i, k_ref: (i, 0)))
    x, y, c = _me()
    k = (2 * x + y).reshape(1).astype(jnp.int32)
    return pl.pallas_call(body, name=name, grid_spec=grid_spec, out_shape=jax.ShapeDtypeStruct((hr, C), f32),
                          compiler_params=_params(("parallel",)))(k, p, got)


def _adamw(w, g, m, v, name, tr):
    R, C = w.shape

    def body(w_ref, g_ref, m_ref, v_ref, d_ref, nm_ref, nv_ref):
        def strip(rows):
            gv = g_ref[rows, :]
            mn = ADAM_B1 * m_ref[rows, :] + (1.0 - ADAM_B1) * gv
            vn = ADAM_B2 * v_ref[rows, :] + (1.0 - ADAM_B2) * (gv * gv)
            m_hat = mn / (1.0 - ADAM_B1 ** ADAM_STEP)
            v_hat = vn / (1.0 - ADAM_B2 ** ADAM_STEP)
            d_ref[rows, :] = -ADAM_LR * (m_hat / (jnp.sqrt(v_hat) + ADAM_EPS) + ADAM_WD * w_ref[rows, :])
            nm_ref[rows, :] = mn
            nv_ref[rows, :] = vn
        _for_strips(tr, strip, rs=min(8, tr))

    blk = pl.BlockSpec((tr, C), lambda i: (i, 0))
    o = jax.ShapeDtypeStruct((R, C), f32)
    return pl.pallas_call(body, name=name, grid=(R // tr,), in_specs=[blk] * 4, out_specs=[blk] * 3, out_shape=[o, o, o],
                          compiler_params=_params(("parallel",)))(w, g, m, v)


def _sum8(t):
    n = t.shape[1]

    def body(t_ref, o_ref):
        acc = t_ref[pl.ds(0, 1), :]
        for r in range(1, 8):
            acc = acc + t_ref[pl.ds(r, 1), :]
        o_ref[...] = acc

    return pl.pallas_call(body, name="sum_devices", out_shape=jax.ShapeDtypeStruct((1, n), f32))(t)


def _reduce_to_owner(g_main, g_dt, g_b):
    xi, yi, ci = _me()
    D = g_main.shape[0]
    hr = D // 2
    halves = lambda t: t.reshape((-1, 2, hr, t.shape[-1]))
    g_main, g_dt, g_b = halves(g_main), halves(g_dt), halves(g_b)
    r_main, r_dt, r_b = _sibling_swap([g_main, g_dt, g_b], "sibling_swap")
    p_main = _pair_sum(g_main, r_main, "pair_sum_main", SUM_ROWS // 4)
    p_dt = _pair_sum(g_dt, r_dt, "pair_sum_dt", SUM_ROWS)
    p_b = _pair_sum(g_b, r_b, "pair_sum_b", SUM_ROWS)
    p_in = jnp.transpose(_w_in_grad_full(p_main[0], p_dt[0]).reshape(hr, N_SHARD, W_IN_SHARD), (1, 0, 2))
    got_in, got_b = _chip_exchange([p_in, p_b], "chip_exchange")
    mine_in, mine_b = _own_sum(p_in, got_in, "own_sum_in"), _own_sum(p_b, got_b, "own_sum_b")
    both_in, both_b = _send_to_sibling([mine_in, mine_b], "share_halves")
    full = lambda both, mine: lax.dynamic_update_slice_in_dim(both, mine[None], ci, axis=0).reshape(2 * hr, mine.shape[1])
    return full(both_in, mine_in), full(both_b, mine_b)


def kernel(x, positions, norm1_w, w_in, conv_w, conv_b, dt_bias, a_log, d_skip, ssd_norm_w, w_br_ret, w_br_ssd, w_out, norm_f_w, loss_target, m_norm1_w, m_w_in, m_conv_w, m_conv_b, m_dt_bias, m_a_log, m_d_skip, m_ssd_norm_w, m_w_br_ret, m_w_br_ssd, m_w_out, m_norm_f_w, v_norm1_w, v_w_in, v_conv_w, v_conv_b, v_dt_bias, v_a_log, v_d_skip, v_ssd_norm_w, v_w_br_ret, v_w_br_ssd, v_w_out, v_norm_f_w):
    D = D_MODEL
    xi, yi, ci = _me()
    k = 2 * xi + yi
    me = 2 * k + ci
    weights = dict(norm1_w=norm1_w, w_in=w_in, conv_w=conv_w, conv_b=conv_b, dt_bias=dt_bias, a_log=a_log, d_skip=d_skip,
                   ssd_norm_w=ssd_norm_w, w_br_ret=w_br_ret, w_br_ssd=w_br_ssd, w_out=w_out, norm_f_w=norm_f_w)
    mom1 = dict(norm1_w=m_norm1_w, w_in=m_w_in, conv_w=m_conv_w, conv_b=m_conv_b, dt_bias=m_dt_bias, a_log=m_a_log, d_skip=m_d_skip,
                ssd_norm_w=m_ssd_norm_w, w_br_ret=m_w_br_ret, w_br_ssd=m_w_br_ssd, w_out=m_w_out, norm_f_w=m_norm_f_w)
    mom2 = dict(norm1_w=v_norm1_w, w_in=v_w_in, conv_w=v_conv_w, conv_b=v_conv_b, dt_bias=v_dt_bias, a_log=v_a_log, d_skip=v_d_skip,
                ssd_norm_w=v_ssd_norm_w, w_br_ret=v_w_br_ret, w_br_ssd=v_w_br_ssd, w_out=v_w_out, norm_f_w=v_norm_f_w)

    a_sh = w_in[0].astype(bf16)
    b_sh = jnp.concatenate([w_br_ret[0], w_br_ssd[0], w_out[0]], axis=0).astype(bf16)
    ga, gb, gc = _gather_weights(a_sh, b_sh, conv_w[0])
    own = lambda g, s: lax.dynamic_update_slice_in_dim(g, s[None], k, axis=0)
    ga, gb, gc = own(ga, a_sh), own(gb, b_sh), own(gc, conv_w[0])
    w_main, w_dt = _split_w_in(jnp.transpose(ga, (1, 0, 2)).reshape(D, IN_PROJ))
    w_br = gb[:, 0:512].reshape(2048, D)
    w_bs = gb[:, 512:1536].reshape(4096, D)
    w_o = gb[:, 1536:2048].reshape(2048, D)
    conv_full = jnp.transpose(gc, (1, 0, 2)).reshape(SSD_CONV, CONV_DIM)

    loss, grad_x, g = _local_step(x[0], positions[0], loss_target[0], norm1_w, w_main, w_dt, conv_full, conv_b, dt_bias, a_log,
                                  d_skip, ssd_norm_w, w_br, w_bs, w_o, norm_f_w)

    g_b = jnp.concatenate([g["w_br_ret"].reshape(N_SHARD, 512, D), g["w_br_ssd"].reshape(N_SHARD, 1024, D),
                           g["w_out"].reshape(N_SHARD, 512, D)], axis=1)
    grad_w_in, full_b = _reduce_to_owner(g["w_in_main"], g["w_in_dt"], g_b)
    grad_mats = dict(w_in=grad_w_in, w_br_ret=full_b[0:512], w_br_ssd=full_b[512:1536], w_out=full_b[1536:2048])

    small = [(n, weights[n].size) for n in ("norm1_w", "conv_b", "dt_bias", "a_log", "d_skip", "ssd_norm_w", "norm_f_w")]
    parts = [jnp.pad(loss.reshape(1, 1), ((0, 0), (0, 127)))] + [g[n].reshape(1, -1) for n, _ in small] + [g["conv_w"].reshape(1, -1)]
    vec = jnp.concatenate(parts, axis=1)
    nv = vec.shape[1]
    nvp = -(-nv // 128) * 128
    vec = jnp.pad(vec, ((0, 0), (0, nvp - nv)))
    total = _sum8(lax.dynamic_update_slice_in_dim(_gather_vec(vec), vec, me, axis=0))
    loss_out = total[0, 0]
    off = 128
    grad_small = {}
    for n, sz in small:
        grad_small[n] = total[:, off:off + sz]
        off += sz
    g_conv = total[:, off:off + SSD_CONV * CONV_DIM].reshape(SSD_CONV, CONV_DIM)
    g_conv = lax.dynamic_slice_in_dim(g_conv, k * (CONV_DIM // N_SHARD), CONV_DIM // N_SHARD, axis=1)
    grad_small["conv_w"] = g_conv.reshape(1, -1)

    upd = {}
    for n in ("w_in", "w_br_ret", "w_br_ssd", "w_out"):
        upd[n] = _adamw(weights[n][0], grad_mats[n], mom1[n][0], mom2[n][0], "adamw_" + n, SUM_ROWS)
    names_small = [n for n, _ in small] + ["conv_w"]
    flat = lambda d: jnp.concatenate([d[n].reshape(1, -1) for n in names_small], axis=1)
    ns = sum(weights[n].size for n in names_small)
    nsp = -(-ns // 128) * 128
    padv = lambda t: jnp.pad(t, ((0, 0), (0, nsp - ns)))
    ds, ms, vs = _adamw(padv(flat(weights)), padv(flat(grad_small)), padv(flat(mom1)),
                        jnp.pad(flat(mom2), ((0, 0), (0, nsp - ns)), constant_values=1.0), "adamw_small", 1)
    off = 0
    for n in names_small:
        sz = weights[n].size
        upd[n] = tuple(t[:, off:off + sz] for t in (ds, ms, vs))
        off += sz

    order = ["norm1_w", "w_in", "conv_w", "conv_b", "dt_bias", "a_log", "d_skip", "ssd_norm_w", "w_br_ret", "w_br_ssd", "w_out", "norm_f_w"]
    grads_out = {**grad_mats, **grad_small}
    shp = lambda n, t: t.reshape(weights[n].shape)
    return (loss_out, grad_x[None], *[shp(n, grads_out[n]) for n in order], *[shp(n, upd[n][0]) for n in order],
            *[shp(n, upd[n][1]) for n in order], *[shp(n, upd[n][2]) for n in order])
```

```python
import jax
import jax.numpy as jnp
import numpy as np
from jax import lax
from jax.experimental import pallas as pl
from jax.experimental.pallas import tpu as pltpu

f32 = jnp.float32
bf16 = jnp.bfloat16
HIGHEST = lax.Precision.HIGHEST
MESH = pl.DeviceIdType.MESH

D_MODEL = 2048
EPS = 1e-6
CHUNK = 64
RET_HEADS = 8
RET_DK = 256
RET_HW = 4 * RET_DK
ROPE_THETA = 10000.0
SSD_WIDTH = 4096
SSD_GROUPS = 8
SSD_STATE = 128
SSD_GW = 512
SSD_GC = SSD_GW + 2 * SSD_STATE
SSD_HPG = 8
SSD_CONV = 4
CONV_DIM = 6144
SSD_HEADS = 64
LS = 128

C_RET, C_Z, C_GATES, C_XBC = 0, 8192, 12288, 16384
N_MAIN = 22528
DT_OFF = 18432
IN_PROJ = 22592
N_SHARD = 4
W_IN_SHARD = IN_PROJ // N_SHARD

ADAM_LR, ADAM_B1, ADAM_B2, ADAM_EPS, ADAM_WD, ADAM_STEP = 0.001, 0.9, 0.999, 1e-08, 0.01, 10

VMEM_LIMIT = 56 * 1024 * 1024
SUM_ROWS = 128
ANY = pl.BlockSpec(memory_space=pl.ANY)


def _params(dims):
    return pltpu.CompilerParams(dimension_semantics=dims, vmem_limit_bytes=VMEM_LIMIT)


def _silu(x):
    return x * jax.nn.sigmoid(x)


def _dsilu(x):
    s = jax.nn.sigmoid(x)
    return s * (1.0 + x * (1.0 - s))


def _nt(a, b):
    return lax.dot_general(a, b, (((1,), (1,)), ((), ())), preferred_element_type=f32)


def _tn(a, b):
    return lax.dot_general(a, b, (((0,), (0,)), ((), ())), preferred_element_type=f32)


def _nn(a, b):
    return jnp.dot(a, b, preferred_element_type=f32)


def _hi(a, b):
    return jnp.dot(a, b, precision=HIGHEST, preferred_element_type=f32)


def _split(a):
    hi = a.astype(bf16)
    return hi, (a - hi.astype(f32)).astype(bf16)


def _sel_r(a, sel):
    hi, lo = _split(a)
    return _nn(hi, sel) + _nn(lo, sel)


def _sel_l(sel, a):
    hi, lo = _split(a)
    return _nn(sel, hi) + _nn(sel, lo)


def _xbc_group_major(t):
    R = t.shape[0]
    nb = SSD_GROUPS * SSD_STATE
    parts = [t[:, :SSD_WIDTH].reshape(R, SSD_GROUPS, SSD_GW), t[:, SSD_WIDTH:SSD_WIDTH + nb].reshape(R, SSD_GROUPS, SSD_STATE),
             t[:, SSD_WIDTH + nb:].reshape(R, SSD_GROUPS, SSD_STATE)]
    return jnp.concatenate(parts, axis=2).reshape(R, CONV_DIM)


def _xbc_original(t):
    R = t.shape[0]
    g = t.reshape(R, SSD_GROUPS, SSD_GC)
    parts = [g[:, :, :SSD_GW].reshape(R, SSD_WIDTH), g[:, :, SSD_GW:SSD_GW + SSD_STATE].reshape(R, SSD_GROUPS * SSD_STATE),
             g[:, :, SSD_GW + SSD_STATE:].reshape(R, SSD_GROUPS * SSD_STATE)]
    return jnp.concatenate(parts, axis=1)


def _split_w_in(w):
    D = w.shape[0]
    ret = jnp.transpose(w[:, :4 * 2048].reshape(D, 4, RET_HEADS, RET_DK), (0, 2, 1, 3)).reshape(D, 4 * 2048)
    w_dt = jnp.pad(w[:, DT_OFF:DT_OFF + SSD_HEADS], ((0, 0), (0, 128 - SSD_HEADS)))
    main = jnp.concatenate([ret, w[:, 8192:12288], w[:, DT_OFF + SSD_HEADS:], _xbc_group_major(w[:, 12288:DT_OFF])], axis=1)
    return main, w_dt


def _w_in_grad_full(g_main, g_dt):
    D = g_main.shape[0]
    ret = jnp.transpose(g_main[:, :C_Z].reshape(D, RET_HEADS, 4, RET_DK), (0, 2, 1, 3)).reshape(D, C_Z)
    return jnp.concatenate([ret, g_main[:, C_Z:C_GATES], _xbc_original(g_main[:, C_XBC:]), g_dt[:, :SSD_HEADS],
                            g_main[:, C_GATES:C_XBC]], axis=1)


def _mm(pairs, M, N, *, tm, tn, out_dtype, name, tb=False):
    P = len(pairs)
    nks = [K // tk for (_, _, _, _, _, K, tk) in pairs]
    starts = [int(s) for s in np.cumsum([0] + nks[:-1])]
    KT = int(sum(nks))
    in_specs, args = [], []
    for (a, a_cb, b, b_kb, b_nb, K, tk), s, nk in zip(pairs, starts, nks):
        def kk(k, s=s, nk=nk):
            return jnp.clip(k - s, 0, nk - 1)
        in_specs.append(pl.BlockSpec((tm, tk), lambda m, n, k, kk=kk, a_cb=a_cb: (m, a_cb + kk(k))))
        if tb:
            in_specs.append(pl.BlockSpec((tn, tk), lambda m, n, k, kk=kk, b_kb=b_kb, b_nb=b_nb: (b_nb + n, b_kb + kk(k))))
        else:
            in_specs.append(pl.BlockSpec((tk, tn), lambda m, n, k, kk=kk, b_kb=b_kb, b_nb=b_nb: (b_kb + kk(k), b_nb + n)))
        args += [a, b]

    def body(*refs):
        o_ref = refs[2 * P]
        k = pl.program_id(2)

        def prod(i):
            a = refs[2 * i][...].astype(bf16)
            b = refs[2 * i + 1][...].astype(bf16)
            return _nt(a, b) if tb else _nn(a, b)

        if KT == 1:
            o_ref[...] = prod(0).astype(out_dtype)
            return
        acc = refs[2 * P + 1]

        @pl.when(k == 0)
        def _():
            acc[...] = jnp.zeros_like(acc)

        for i in range(P):
            @pl.when((k >= starts[i]) & (k < starts[i] + nks[i]))
            def _(i=i):
                acc[...] += prod(i)

        @pl.when(k == KT - 1)
        def _():
            o_ref[...] = acc[...].astype(out_dtype)

    return pl.pallas_call(
        body, name=name, grid=(M // tm, N // tn, KT), in_specs=in_specs,
        out_specs=pl.BlockSpec((tm, tn), lambda m, n, k: (m, n)),
        out_shape=jax.ShapeDtypeStruct((M, N), out_dtype),
        scratch_shapes=[] if KT == 1 else [pltpu.VMEM((tm, tn), f32)],
        compiler_params=_params(("parallel", "parallel", "arbitrary")),
    )(*args)


def _mm1(a, b, *, tm, tn, tk, out_dtype, name, tb=False):
    M, K = a.shape
    N = b.shape[0] if tb else b.shape[1]
    return _mm([(a, 0, b, 0, 0, K, tk)], M, N, tm=tm, tn=tn, out_dtype=out_dtype, name=name, tb=tb)


RS = 16
CS = 32


def _for_strips(n_rows, fn, rs=RS, unroll=4):
    def step(s, carry):
        fn(pl.ds(pl.multiple_of(s * rs, rs), rs))
        return carry
    n = n_rows // rs
    lax.fori_loop(0, n, step, 0, unroll=min(unroll, n))


def _norm1_fwd(x, w, tr):
    S, D = x.shape

    def body(x_ref, w_ref, h_ref, ht_ref):
        def strip(rows):
            xv = x_ref[rows, :]
            r = lax.rsqrt(jnp.mean(xv * xv, axis=-1, keepdims=True) + EPS)
            h_ref[rows, :] = (xv * r * w_ref[...]).astype(bf16)
        _for_strips(tr, strip)
        ht_ref[...] = h_ref[...].T

    return pl.pallas_call(
        body, name="norm1_fwd", grid=(S // tr,),
        in_specs=[pl.BlockSpec((tr, D), lambda i: (i, 0)), pl.BlockSpec((1, D), lambda i: (0, 0))],
        out_specs=[pl.BlockSpec((tr, D), lambda i: (i, 0)), pl.BlockSpec((D, tr), lambda i: (0, i))],
        out_shape=[jax.ShapeDtypeStruct((S, D), bf16), jax.ShapeDtypeStruct((D, S), bf16)], compiler_params=_params(("parallel",)),
    )(x, w)


def _norm1_bwd(x, w, dh, dx2, tr):
    S, D = x.shape

    def body(x_ref, w_ref, dh_ref, dx2_ref, gx_ref, gw_ref, acc):
        @pl.when(pl.program_id(0) == 0)
        def _():
            acc[...] = jnp.zeros_like(acc)

        def strip(rows):
            xv = x_ref[rows, :]
            r = lax.rsqrt(jnp.mean(xv * xv, axis=-1, keepdims=True) + EPS)
            xh = xv * r
            dhv = dh_ref[rows, :]
            acc[...] += dhv * xh
            dxh = dhv * w_ref[...]
            gx_ref[rows, :] = dx2_ref[rows, :] + r * (dxh - xh * jnp.mean(dxh * xh, axis=-1, keepdims=True))
        _for_strips(tr, strip)

        @pl.when(pl.program_id(0) == S // tr - 1)
        def _():
            gw_ref[...] = jnp.sum(acc[...], axis=0, keepdims=True)

    row = pl.BlockSpec((tr, D), lambda i: (i, 0))
    vec = pl.BlockSpec((1, D), lambda i: (0, 0))
    return pl.pallas_call(
        body, name="norm1_bwd", grid=(S // tr,), in_specs=[row, vec, row, row], out_specs=[row, vec],
        out_shape=[jax.ShapeDtypeStruct((S, D), f32), jax.ShapeDtypeStruct((1, D), f32)],
        scratch_shapes=[pltpu.VMEM((RS, D), f32)], compiler_params=_params(("arbitrary",)),
    )(x, w, dh, dx2)


def _final_fwd_bwd(x, mo, target, wf, tr):
    S, D = x.shape

    def body(x_ref, mo_ref, t_ref, w_ref, dx2_ref, dx2b_ref, loss_ref, gw_ref, acc, lacc):
        @pl.when(pl.program_id(0) == 0)
        def _():
            acc[...] = jnp.zeros_like(acc)
            lacc[...] = jnp.zeros_like(lacc)

        def strip(rows):
            x2 = x_ref[rows, :] + mo_ref[rows, :]
            r = lax.rsqrt(jnp.mean(x2 * x2, axis=-1, keepdims=True) + EPS)
            xh = x2 * r
            wv = w_ref[...]
            err = xh * wv - t_ref[rows, :]
            lacc[...] += jnp.mean(err * err, axis=-1, keepdims=True)
            dy = err * (1.0 / D)
            acc[...] += dy * xh
            dxh = dy * wv
            dx2 = r * (dxh - xh * jnp.mean(dxh * xh, axis=-1, keepdims=True))
            dx2_ref[rows, :] = dx2
            dx2b_ref[rows, :] = dx2.astype(bf16)
        _for_strips(tr, strip)

        @pl.when(pl.program_id(0) == S // tr - 1)
        def _():
            gw_ref[...] = jnp.sum(acc[...], axis=0, keepdims=True)
            loss_ref[...] = 0.5 * jnp.sum(lacc[...], axis=0, keepdims=True)

    row = pl.BlockSpec((tr, D), lambda i: (i, 0))
    vec = pl.BlockSpec((1, D), lambda i: (0, 0))
    return pl.pallas_call(
        body, name="final_norm_loss", grid=(S // tr,), in_specs=[row, row, row, vec],
        out_specs=[row, row, pl.BlockSpec((1, 1), lambda i: (0, 0)), vec],
        out_shape=[jax.ShapeDtypeStruct((S, D), f32), jax.ShapeDtypeStruct((S, D), bf16), jax.ShapeDtypeStruct((1, 1), f32),
                   jax.ShapeDtypeStruct((1, D), f32)],
        scratch_shapes=[pltpu.VMEM((RS, D), f32), pltpu.VMEM((RS, 1), f32)], compiler_params=_params(("arbitrary",)),
    )(x, mo, target, wf)


def _merge_fwd(p_r, p_s, proj, tr):
    S, D = p_r.shape

    def body(pr_ref, ps_ref, g_ref, o_ref, ot_ref):
        def strip(rows):
            gr, gs = g_ref[rows, pl.ds(0, D)], g_ref[rows, pl.ds(D, D)]
            o_ref[rows, :] = (jax.nn.sigmoid(gr) * pr_ref[rows, :] + jax.nn.sigmoid(gs) * ps_ref[rows, :]).astype(bf16)
        _for_strips(tr, strip)
        ot_ref[...] = o_ref[...].T

    row = pl.BlockSpec((tr, D), lambda i: (i, 0))
    return pl.pallas_call(
        body, name="merge_fwd", grid=(S // tr,),
        in_specs=[row, row, pl.BlockSpec((tr, 2 * D), lambda i: (i, C_GATES // (2 * D)))],
        out_specs=[row, pl.BlockSpec((D, tr), lambda i: (0, i))],
        out_shape=[jax.ShapeDtypeStruct((S, D), bf16), jax.ShapeDtypeStruct((D, S), bf16)], compiler_params=_params(("parallel",)),
    )(p_r, p_s, proj)


def _merge_bwd(dm, p_r, p_s, proj, tr):
    S, D = p_r.shape

    def body(dm_ref, pr_ref, ps_ref, g_ref, dpr_ref, dps_ref, dproj_ref):
        def strip(rows):
            dmv = dm_ref[rows, :]
            sr = jax.nn.sigmoid(g_ref[rows, pl.ds(0, D)])
            ss = jax.nn.sigmoid(g_ref[rows, pl.ds(D, D)])
            dpr_ref[rows, :] = (dmv * sr).astype(bf16)
            dps_ref[rows, :] = (dmv * ss).astype(bf16)
            dproj_ref[rows, pl.ds(0, D)] = (dmv * pr_ref[rows, :] * sr * (1.0 - sr)).astype(bf16)
            dproj_ref[rows, pl.ds(D, D)] = (dmv * ps_ref[rows, :] * ss * (1.0 - ss)).astype(bf16)
        _for_strips(tr, strip)

    row = pl.BlockSpec((tr, D), lambda i: (i, 0))
    gates = pl.BlockSpec((tr, 2 * D), lambda i: (i, C_GATES // (2 * D)))
    o = jax.ShapeDtypeStruct((S, D), bf16)
    return pl.pallas_call(
        body, name="merge_bwd", grid=(S // tr,), in_specs=[row, row, row, gates],
        out_specs=[row, row, gates], out_shape=[o, o, jax.ShapeDtypeStruct((S, N_MAIN), bf16)],
        compiler_params=_params(("parallel",)),
    )(dm, p_r, p_s, proj)


def _ssd_norm_fwd(y, proj, w, tr):
    S, W = y.shape

    def body(y_ref, z_ref, w_ref, o_ref, ot_ref):
        def strip(rows):
            u = y_ref[rows, :] * _silu(z_ref[rows, :])
            r = lax.rsqrt(jnp.mean(u * u, axis=-1, keepdims=True) + EPS)
            o_ref[rows, :] = (u * r * w_ref[...]).astype(bf16)
        _for_strips(tr, strip)
        ot_ref[...] = o_ref[...].T

    row = pl.BlockSpec((tr, W), lambda i: (i, 0))
    return pl.pallas_call(
        body, name="ssd_norm_fwd", grid=(S // tr,),
        in_specs=[row, pl.BlockSpec((tr, W), lambda i: (i, C_Z // W)), pl.BlockSpec((1, W), lambda i: (0, 0))],
        out_specs=[row, pl.BlockSpec((W, tr), lambda i: (0, i))],
        out_shape=[jax.ShapeDtypeStruct((S, W), bf16), jax.ShapeDtypeStruct((W, S), bf16)], compiler_params=_params(("parallel",)),
    )(y, proj, w)


def _ssd_norm_bwd(y, proj, w, dys, dproj, tr):
    S, W = y.shape

    def body(y_ref, z_ref, w_ref, d_ref, _, dy_ref, dz_ref, gw_ref, acc):
        @pl.when(pl.program_id(0) == 0)
        def _():
            acc[...] = jnp.zeros_like(acc)

        def strip(rows):
            yv, zv, dv = y_ref[rows, :], z_ref[rows, :], d_ref[rows, :]
            sz = _silu(zv)
            u = yv * sz
            r = lax.rsqrt(jnp.mean(u * u, axis=-1, keepdims=True) + EPS)
            un = u * r
            acc[...] += dv * un
            dun = dv * w_ref[...]
            du = r * (dun - un * jnp.mean(dun * un, axis=-1, keepdims=True))
            dy_ref[rows, :] = du * sz
            dz_ref[rows, :] = (du * yv * _dsilu(zv)).astype(bf16)
        _for_strips(tr, strip)

        @pl.when(pl.program_id(0) == S // tr - 1)
        def _():
            gw_ref[...] = jnp.sum(acc[...], axis=0, keepdims=True)

    row = pl.BlockSpec((tr, W), lambda i: (i, 0))
    zcol = pl.BlockSpec((tr, W), lambda i: (i, C_Z // W))
    vec = pl.BlockSpec((1, W), lambda i: (0, 0))
    return pl.pallas_call(
        body, name="ssd_norm_bwd", grid=(S // tr,),
        in_specs=[row, zcol, vec, row, ANY], out_specs=[row, zcol, vec],
        out_shape=[jax.ShapeDtypeStruct((S, W), f32), jax.ShapeDtypeStruct(dproj.shape, bf16), jax.ShapeDtypeStruct((1, W), f32)],
        input_output_aliases={4: 1}, scratch_shapes=[pltpu.VMEM((RS, W), f32)], compiler_params=_params(("arbitrary",)),
    )(y, proj, w, dys, dproj)


def _rope(t, cos, sin):
    t1, t2 = t[:, :128], t[:, 128:]
    return jnp.concatenate([t1 * cos - t2 * sin, t2 * cos + t1 * sin], axis=1)


def _rope_t(d, cos, sin):
    d1, d2 = d[:, :128], d[:, 128:]
    return jnp.concatenate([d1 * cos + d2 * sin, d2 * cos - d1 * sin], axis=1)


def _ret_specs(tb, rev_nb=None):
    def blk(i):
        return i if rev_nb is None else rev_nb - 1 - i
    head = pl.BlockSpec((tb, RET_HW), lambda h, i: (blk(i), h))
    tab = pl.BlockSpec((tb, 128), lambda h, i: (blk(i), 0))
    mat = pl.BlockSpec((1, CHUNK, CHUNK), lambda h, i: (h, 0, 0))
    vec = pl.BlockSpec((1, CHUNK, 1), lambda h, i: (h, 0, 0))
    one = pl.BlockSpec((1, 1, 1), lambda h, i: (h, 0, 0))
    own = pl.BlockSpec((tb, RET_DK), lambda h, i: (blk(i), h))
    st = pl.BlockSpec((1, tb // CHUNK, RET_DK, RET_DK), lambda h, i: (h, blk(i), 0, 0))
    return head, tab, mat, vec, one, own, st


def _ret_fwd(proj, cos, sin, intra, qdec, kdec, cdec, tb):
    S = proj.shape[0]
    nc = S // CHUNK
    scale = RET_DK ** -0.5
    dk = RET_DK

    def body(p_ref, cos_ref, sin_ref, m_ref, qd_ref, kd_ref, cd_ref, y_ref, yr_ref, yrt_ref, st_ref, st):
        @pl.when(pl.program_id(1) == 0)
        def _():
            st[...] = jnp.zeros_like(st)

        mm, qd, kd, cd = m_ref[0], qd_ref[0], kd_ref[0], cd_ref[0]

        def chunk(c, carry):
            rows = pl.ds(pl.multiple_of(c * CHUNK, CHUNK), CHUNK)
            cs, sn = cos_ref[rows, :], sin_ref[rows, :]
            qr = _rope(p_ref[rows, pl.ds(0, dk)], cs, sn)
            kr = _rope(p_ref[rows, pl.ds(dk, dk)], cs, sn) * scale
            qb, kb, vb = qr.astype(bf16), kr.astype(bf16), p_ref[rows, pl.ds(2 * dk, dk)].astype(bf16)
            stb = st[...].astype(bf16)
            st_ref[0, c] = stb
            sc = (_nt(qb, kb) * mm).astype(bf16)
            y = _nn(sc, vb) + _nn(qb, stb) * qd
            st[...] = st[...] * cd + _tn((kr * kd).astype(bf16), vb)
            y_ref[rows, :] = y
            mu = jnp.mean(y, axis=-1, keepdims=True)
            yc = y - mu
            var = jnp.mean(yc * yc, axis=-1, keepdims=True)
            yr_ref[rows, :] = (yc * lax.rsqrt(var + EPS) * _silu(p_ref[rows, pl.ds(3 * dk, dk)])).astype(bf16)
            return carry

        lax.fori_loop(0, tb // CHUNK, chunk, 0, unroll=min(4, tb // CHUNK))
        yrt_ref[...] = yr_ref[...].T

    head, tab, mat, vec, one, own, stspec = _ret_specs(tb)
    return pl.pallas_call(
        body, name="ret_fwd", grid=(RET_HEADS, S // tb),
        in_specs=[head, tab, tab, mat, vec, vec, one],
        out_specs=[own, own, pl.BlockSpec((RET_DK, tb), lambda h, i: (h, i)), stspec],
        out_shape=[jax.ShapeDtypeStruct((S, 2048), f32), jax.ShapeDtypeStruct((S, 2048), bf16), jax.ShapeDtypeStruct((2048, S), bf16),
                   jax.ShapeDtypeStruct((RET_HEADS, nc, dk, dk), bf16)],
        scratch_shapes=[pltpu.VMEM((dk, dk), f32)], compiler_params=_params(("parallel", "arbitrary")),
    )(proj, cos, sin, intra, qdec, kdec, cdec)


def _ret_bwd(proj, cos, sin, intra, qdec, kdec, cdec, y, dyr, states, dproj, tb):
    S = proj.shape[0]
    nb = S // tb
    nck = tb // CHUNK
    scale = RET_DK ** -0.5
    dk = RET_DK

    def body(p_ref, cos_ref, sin_ref, m_ref, qd_ref, kd_ref, cd_ref, y_ref, dyr_ref, st_ref, _, o_ref, dst):
        @pl.when(pl.program_id(1) == 0)
        def _():
            dst[...] = jnp.zeros_like(dst)

        mm, qd, kd, cd = m_ref[0], qd_ref[0], kd_ref[0], cd_ref[0]

        def chunk(cc, carry):
            c = nck - 1 - cc
            rows = pl.ds(pl.multiple_of(c * CHUNK, CHUNK), CHUNK)
            cs, sn = cos_ref[rows, :], sin_ref[rows, :]
            qr = _rope(p_ref[rows, pl.ds(0, dk)], cs, sn)
            kr = _rope(p_ref[rows, pl.ds(dk, dk)], cs, sn) * scale
            qb, kb, vb = qr.astype(bf16), kr.astype(bf16), p_ref[rows, pl.ds(2 * dk, dk)].astype(bf16)
            kdb = (kr * kd).astype(bf16)
            stb = st_ref[0, c]
            yv, gv, dyrv = y_ref[rows, :], p_ref[rows, pl.ds(3 * dk, dk)], dyr_ref[rows, :]
            mu = jnp.mean(yv, axis=-1, keepdims=True)
            yc = yv - mu
            rstd = lax.rsqrt(jnp.mean(yc * yc, axis=-1, keepdims=True) + EPS)
            yn = yc * rstd
            o_ref[rows, pl.ds(3 * dk, dk)] = (dyrv * yn * _dsilu(gv)).astype(bf16)
            dyn = dyrv * _silu(gv)
            dy = rstd * (dyn - jnp.mean(dyn, axis=-1, keepdims=True) - yn * jnp.mean(dyn * yn, axis=-1, keepdims=True))
            dyb = dy.astype(bf16)
            dyqb = (dy * qd).astype(bf16)
            dstb = dst[...].astype(bf16)
            sct = (_nt(kb, qb) * mm).astype(bf16)
            ds = (_nt(dyb, vb) * mm).astype(bf16)
            dsT = (_nt(vb, dyb) * mm).astype(bf16)
            dv = _nn(sct, dyb) + _nn(kdb, dstb)
            dqr = _nn(ds, kb) + _nt(dyqb, stb)
            dkr = _nn(dsT, qb) + _nt(vb, dstb) * kd
            dst[...] = dst[...] * cd + _tn(qb, dyqb)
            o_ref[rows, pl.ds(0, dk)] = _rope_t(dqr, cs, sn).astype(bf16)
            o_ref[rows, pl.ds(dk, dk)] = (_rope_t(dkr, cs, sn) * scale).astype(bf16)
            o_ref[rows, pl.ds(2 * dk, dk)] = dv.astype(bf16)
            return carry

        lax.fori_loop(0, nck, chunk, 0, unroll=min(4, nck))

    head, tab, mat, vec, one, own, stspec = _ret_specs(tb, rev_nb=nb)
    return pl.pallas_call(
        body, name="ret_bwd", grid=(RET_HEADS, nb),
        in_specs=[head, tab, tab, mat, vec, vec, one, own, own, stspec, ANY],
        out_specs=head, out_shape=jax.ShapeDtypeStruct(dproj.shape, bf16), input_output_aliases={10: 0},
        scratch_shapes=[pltpu.VMEM((dk, dk), f32)], compiler_params=_params(("parallel", "arbitrary")),
    )(proj, cos, sin, intra, qdec, kdec, cdec, y, dyr, states, dproj)


def _conv_fwd(proj, conv_w, conv_b, tb, cw):
    S = proj.shape[0]
    off = C_XBC // cw

    def body(x_ref, halo_ref, w_ref, b_ref, o_ref, xe):
        xe[pl.ds(0, 8), :] = jnp.where(pl.program_id(1) == 0, 0.0, halo_ref[...])
        xe[pl.ds(8, CS), :] = x_ref[pl.ds(0, CS), :]
        ws = [w_ref[pl.ds(j, 1), :] for j in range(SSD_CONV)]
        for s in range(tb // CS):
            tap = (lambda j: xe[pl.ds(5 + j, CS), :]) if s == 0 else (lambda j, s=s: x_ref[pl.ds(s * CS - 3 + j, CS), :])
            acc = b_ref[...] + ws[0] * tap(0)
            for j in range(1, SSD_CONV):
                acc = acc + ws[j] * tap(j)
            o_ref[pl.ds(s * CS, CS), :] = acc

    return pl.pallas_call(
        body, name="conv_fwd", grid=(CONV_DIM // cw, S // tb),
        in_specs=[pl.BlockSpec((tb, cw), lambda j, i: (i, off + j)),
                  pl.BlockSpec((8, cw), lambda j, i: (jnp.maximum(i * (tb // 8) - 1, 0), off + j)),
                  pl.BlockSpec((SSD_CONV, cw), lambda j, i: (0, j)), pl.BlockSpec((1, cw), lambda j, i: (0, j))],
        out_specs=pl.BlockSpec((tb, cw), lambda j, i: (i, j)),
        out_shape=jax.ShapeDtypeStruct((S, CONV_DIM), f32),
        scratch_shapes=[pltpu.VMEM((CS + 8, cw), f32)], compiler_params=_params(("parallel", "arbitrary")),
    )(proj, proj, conv_w, conv_b)


def _conv_bwd(dpre, proj, conv_w, dproj, tb, cw):
    S, n = dpre.shape
    nb = S // tb
    xoff = C_XBC // cw

    def body(d_ref, dh_ref, x_ref, xh_ref, w_ref, _, dx_ref, gw_ref, gb_ref, de, xe, accw, accb):
        i = pl.program_id(1)

        @pl.when(i == 0)
        def _():
            accw[...] = jnp.zeros_like(accw)
            accb[...] = jnp.zeros_like(accb)

        ns = tb // CS
        de[pl.ds(0, CS), :] = d_ref[pl.ds(tb - CS, CS), :]
        de[pl.ds(CS, 8), :] = jnp.where(i == nb - 1, 0.0, dh_ref[...])
        xe[pl.ds(0, 8), :] = jnp.where(i == 0, 0.0, xh_ref[...])
        xe[pl.ds(8, CS), :] = x_ref[pl.ds(0, CS), :]
        ws = [w_ref[pl.ds(j, 1), :] for j in range(SSD_CONV)]
        fold = lambda p: sum(p[8 * q:8 * (q + 1)] for q in range(1, CS // 8)) + p[0:8]
        for s in range(ns):
            dv = d_ref[pl.ds(s * CS, CS), :]
            ahead = (lambda o: de[pl.ds(o, CS), :]) if s == ns - 1 else (lambda o, s=s: d_ref[pl.ds(s * CS + o, CS), :])
            xtap = (lambda j: xe[pl.ds(5 + j, CS), :]) if s == 0 else (lambda j, s=s: x_ref[pl.ds(s * CS - 3 + j, CS), :])
            acc = ws[SSD_CONV - 1] * dv
            for j in range(SSD_CONV - 1):
                acc = acc + ws[j] * ahead(3 - j)
            dx_ref[pl.ds(s * CS, CS), :] = acc.astype(bf16)
            accb[...] += fold(dv)
            for j in range(SSD_CONV):
                accw[j] += fold(dv * xtap(j))

        @pl.when(i == nb - 1)
        def _():
            gb_ref[...] = jnp.sum(accb[...], axis=0, keepdims=True)
            for j in range(SSD_CONV):
                gw_ref[pl.ds(j, 1), :] = jnp.sum(accw[j], axis=0, keepdims=True)

    return pl.pallas_call(
        body, name="conv_bwd", grid=(n // cw, nb),
        in_specs=[pl.BlockSpec((tb, cw), lambda j, i: (i, j)),
                  pl.BlockSpec((8, cw), lambda j, i: (jnp.minimum((i + 1) * (tb // 8), S // 8 - 1), j)),
                  pl.BlockSpec((tb, cw), lambda j, i: (i, xoff + j)),
                  pl.BlockSpec((8, cw), lambda j, i: (jnp.maximum(i * (tb // 8) - 1, 0), xoff + j)),
                  pl.BlockSpec((SSD_CONV, cw), lambda j, i: (0, j)), ANY],
        out_specs=[pl.BlockSpec((tb, cw), lambda j, i: (i, xoff + j)), pl.BlockSpec((SSD_CONV, cw), lambda j, i: (0, j)),
                   pl.BlockSpec((1, cw), lambda j, i: (0, j))],
        out_shape=[jax.ShapeDtypeStruct(dproj.shape, bf16), jax.ShapeDtypeStruct((SSD_CONV, n), f32), jax.ShapeDtypeStruct((1, n), f32)],
        input_output_aliases={5: 0},
        scratch_shapes=[pltpu.VMEM((CS + 8, cw), f32), pltpu.VMEM((CS + 8, cw), f32), pltpu.VMEM((SSD_CONV, 8, cw), f32),
                        pltpu.VMEM((8, cw), f32)],
        compiler_params=_params(("parallel", "arbitrary")),
    )(dpre, dpre, proj, proj, conv_w, dproj)


def _dt_prep(dt_raw, dt_bias, a_log, tb):
    S = dt_raw.shape[0]

    def body(r_ref, b_ref, al_ref, dt_ref, sg_ref, ac_ref):
        li = lax.broadcasted_iota(jnp.int32, (LS, LS), 0)
        si = lax.broadcasted_iota(jnp.int32, (LS, LS), 1)
        tri = (li >= si).astype(f32)
        neg_a = -jnp.exp(al_ref[...])
        for c in range(tb // LS):
            rows = pl.ds(c * LS, LS)
            xv = r_ref[rows, :] + b_ref[...]
            dtv = jax.nn.softplus(xv)
            dt_ref[rows, :] = dtv
            sg_ref[rows, :] = jax.nn.sigmoid(xv)
            ac_ref[rows, :] = _hi(tri, dtv * neg_a)

    row = pl.BlockSpec((tb, 128), lambda i: (i, 0))
    vec = pl.BlockSpec((1, 128), lambda i: (0, 0))
    o = jax.ShapeDtypeStruct((S, 128), f32)
    return pl.pallas_call(body, name="dt_prep", grid=(S // tb,), in_specs=[row, vec, vec], out_specs=[row, row, row],
                          out_shape=[o, o, o], compiler_params=_params(("parallel",)))(dt_raw, dt_bias, a_log)


def _group_major(t):
    S = t.shape[0]
    return jnp.transpose(t[:, :SSD_HEADS].reshape(S, SSD_GROUPS, SSD_HPG), (1, 0, 2))


def _group_major_t(t):
    S = t.shape[0]
    return jnp.transpose(t[:, :SSD_HEADS].reshape(S // LS, LS, SSD_GROUPS, SSD_HPG), (2, 0, 3, 1))


def _ssd_specs(tb, rev_nb=None):
    def blk(i):
        return i if rev_nb is None else rev_nb - 1 - i
    grp = pl.BlockSpec((tb, SSD_GC), lambda g, i: (blk(i), g))
    xs = pl.BlockSpec((tb, SSD_GW), lambda g, i: (blk(i), g))
    ph = pl.BlockSpec((1, tb, SSD_HPG), lambda g, i: (g, blk(i), 0))
    pht = pl.BlockSpec((1, tb // LS, SSD_HPG, LS), lambda g, i: (g, blk(i), 0, 0))
    gvec = pl.BlockSpec((1, 1, SSD_GW), lambda g, i: (g, 0, 0))
    ex = pl.BlockSpec((SSD_HPG, SSD_GW), lambda g, i: (0, 0))
    st = pl.BlockSpec((1, tb // LS, SSD_STATE, SSD_GW), lambda g, i: (g, blk(i), 0, 0))
    return grp, xs, ph, pht, gvec, ex, st


def _expander():
    return jnp.repeat(jnp.eye(SSD_HPG, dtype=f32), SSD_GW // SSD_HPG, axis=1).astype(bf16)


def _expand3(dt8, ac8, ex):
    stack = jnp.concatenate([dt8, jnp.exp(ac8), jnp.exp(ac8[LS - 1:LS, :] - ac8)], axis=0)
    wide = _sel_r(stack, ex)
    return wide[0:LS], wide[LS:2 * LS], wide[2 * LS:3 * LS]


def _ssd_fwd(pre, dt_g, ac_g, act_g, dskx, tb):
    S = pre.shape[0]
    nc = S // LS
    hd = SSD_GW // SSD_HPG

    def body(p_ref, dt_ref, ac_ref, act_ref, dsk_ref, ex_ref, y_ref, st_ref, st):
        @pl.when(pl.program_id(1) == 0)
        def _():
            st[...] = jnp.zeros_like(st)

        ex = ex_ref[...]
        li = lax.broadcasted_iota(jnp.int32, (LS, LS), 0)
        si = lax.broadcasted_iota(jnp.int32, (LS, LS), 1)
        causal = li >= si

        def chunk(c, carry):
            rows = pl.ds(pl.multiple_of(c * LS, LS), LS)
            xs = _silu(p_ref[rows, pl.ds(0, SSD_GW)])
            bcb = _silu(p_ref[rows, pl.ds(SSD_GW, SSD_STATE)]).astype(bf16)
            ccb = _silu(p_ref[rows, pl.ds(SSD_GW + SSD_STATE, SSD_STATE)]).astype(bf16)
            dt8, ac8, act = dt_ref[0, rows, :], ac_ref[0, rows, :], act_ref[0, c]
            dtx, eax, tailx = _expand3(dt8, ac8, ex)
            xdt = xs * dtx
            cb = _nt(ccb, bcb)
            stb = st[...].astype(bf16)
            st_ref[0, c] = stb
            xdtb = xdt.astype(bf16)
            outs = []
            for h in range(SSD_HPG):
                dec = jnp.exp(jnp.where(causal, ac8[:, h:h + 1] - act[h:h + 1, :], -1e30))
                outs.append(_nn((cb * dec).astype(bf16), xdtb[:, hd * h:hd * (h + 1)]))
            y_ref[rows, :] = jnp.concatenate(outs, axis=1) + _nn(ccb, stb) * eax + dsk_ref[0] * xs
            st[...] = st[...] * eax[LS - 1:LS, :] + _tn(bcb, (xdt * tailx).astype(bf16))
            return carry

        lax.fori_loop(0, tb // LS, chunk, 0)

    grp, xs, ph, pht, gvec, ex, stspec = _ssd_specs(tb)
    return pl.pallas_call(
        body, name="ssd_fwd", grid=(SSD_GROUPS, S // tb),
        in_specs=[grp, ph, ph, pht, gvec, ex], out_specs=[xs, stspec],
        out_shape=[jax.ShapeDtypeStruct((S, SSD_WIDTH), f32), jax.ShapeDtypeStruct((SSD_GROUPS, nc, SSD_STATE, SSD_GW), bf16)],
        scratch_shapes=[pltpu.VMEM((SSD_STATE, SSD_GW), f32)], compiler_params=_params(("parallel", "arbitrary")),
    )(pre, dt_g, ac_g, act_g, dskx, _expander())


def _ssd_bwd(pre, dt_g, ac_g, act_g, sg_g, dskx, nega_g, dy, states, tb):
    S = pre.shape[0]
    nb = S // tb
    nck = tb // LS
    hd = SSD_GW // SSD_HPG

    def body(p_ref, dt_ref, ac_ref, act_ref, sg_ref, dsk_ref, na_ref, ex_ref, ext_ref, dy_ref, st_ref,
             dp_ref, ddt_ref, gsk_ref, gal_ref, gdb_ref, dst, skacc):
        @pl.when(pl.program_id(1) == 0)
        def _():
            dst[...] = jnp.zeros_like(dst)
            skacc[...] = jnp.zeros_like(skacc)
            gal_ref[...] = jnp.zeros_like(gal_ref)
            gdb_ref[...] = jnp.zeros_like(gdb_ref)

        ex, ext = ex_ref[...], ext_ref[...]
        li = lax.broadcasted_iota(jnp.int32, (LS, LS), 0)
        si = lax.broadcasted_iota(jnp.int32, (LS, LS), 1)
        causal = li >= si
        anti = si >= li
        upper = anti.astype(bf16)
        last_row = (lax.broadcasted_iota(jnp.int32, (LS, 1), 0) == LS - 1).astype(f32)
        head_id = lax.broadcasted_iota(jnp.int32, (1, SSD_HPG), 1)
        neg_a = na_ref[0]
        dskv = dsk_ref[0]

        def chunk(cc, carry):
            c = nck - 1 - cc
            rows = pl.ds(pl.multiple_of(c * LS, LS), LS)
            px = p_ref[rows, pl.ds(0, SSD_GW)]
            pb = p_ref[rows, pl.ds(SSD_GW, SSD_STATE)]
            pc = p_ref[rows, pl.ds(SSD_GW + SSD_STATE, SSD_STATE)]
            xs = _silu(px)
            bcb = _silu(pb).astype(bf16)
            ccb = _silu(pc).astype(bf16)
            dt8, ac8, act = dt_ref[0, rows, :], ac_ref[0, rows, :], act_ref[0, c]
            dtx, eax, tailx = _expand3(dt8, ac8, ex)
            xdt = xs * dtx
            ex_last = eax[LS - 1:LS, :]
            stb = st_ref[0, c]
            dyv = dy_ref[rows, :]
            dyb = dyv.astype(bf16)
            xdtb = xdt.astype(bf16)
            skacc[...] += jnp.sum(dyv * xs, axis=0, keepdims=True)
            yinter = _nn(ccb, stb) * eax
            dzb = (dyv * eax).astype(bf16)
            dcc = _nt(dzb, stb)
            dstv = dst[...]
            dstb = dstv.astype(bf16)
            xt = xdt * tailx
            dxt = _nn(bcb, dstb)
            dbc = _nt(xt.astype(bf16), dstb)
            dxdt = dxt * tailx
            lastrow = jnp.sum(dxt * xt, axis=0, keepdims=True) + jnp.sum(dstv * stb.astype(f32), axis=0, keepdims=True) * ex_last
            dst[...] = dstv * ex_last + _tn(ccb, dzb)
            cb = _nt(ccb, bcb)
            cbt = _nt(bcb, ccb)
            dcb = jnp.zeros((LS, LS), f32)
            dcbt = jnp.zeros((LS, LS), f32)
            dac8 = jnp.zeros((LS, SSD_HPG), f32)
            dxin = []
            for h in range(SSD_HPG):
                sl = slice(hd * h, hd * (h + 1))
                col, rowv = ac8[:, h:h + 1], act[h:h + 1, :]
                dec = jnp.exp(jnp.where(causal, col - rowv, -1e30))
                dect = jnp.exp(jnp.where(anti, rowv - col, -1e30))
                gm, gmt = cb * dec, cbt * dect
                dgm, dgmt = _nt(dyb[:, sl], xdtb[:, sl]), _nt(xdtb[:, sl], dyb[:, sl])
                dxin.append(_nn(gmt.astype(bf16), dyb[:, sl]))
                dcb = dcb + dgm * dec
                dcbt = dcbt + dgmt * dect
                dcol = jnp.sum(dgm * gm, axis=1, keepdims=True) - jnp.sum(dgmt * gmt, axis=1, keepdims=True)
                dac8 = dac8 + dcol * (head_id == h).astype(f32)
            dxintra = jnp.concatenate(dxin, axis=1)
            dcc = dcc + _nn(dcb.astype(bf16), bcb)
            dbc = dbc + _nn(dcbt.astype(bf16), ccb)
            dxdt = dxdt + dxintra
            dacx = dyv * yinter - dxt * xt + last_row * lastrow
            red = _sel_r(jnp.concatenate([dacx, dxdt * xs], axis=0), ext)
            dac8 = dac8 + red[0:LS]
            da8 = _sel_l(upper, dac8)
            ddt8 = red[LS:2 * LS] + da8 * neg_a
            gal_ref[0] += jnp.sum(da8 * dt8 * neg_a, axis=0, keepdims=True)
            ddr = ddt8 * sg_ref[0, rows, :]
            ddt_ref[0, rows, :] = ddr
            gdb_ref[0] += jnp.sum(ddr, axis=0, keepdims=True)
            dp_ref[rows, pl.ds(0, SSD_GW)] = (dskv * dyv + dxdt * dtx) * _dsilu(px)
            dp_ref[rows, pl.ds(SSD_GW, SSD_STATE)] = dbc * _dsilu(pb)
            dp_ref[rows, pl.ds(SSD_GW + SSD_STATE, SSD_STATE)] = dcc * _dsilu(pc)
            return carry

        lax.fori_loop(0, nck, chunk, 0)

        @pl.when(pl.program_id(1) == nb - 1)
        def _():
            gsk_ref[0] = skacc[...]

    grp, xs, ph, pht, gvec, ex, stspec = _ssd_specs(tb, rev_nb=nb)
    small = pl.BlockSpec((1, 1, SSD_HPG), lambda g, i: (g, 0, 0))
    ext = pl.BlockSpec((SSD_GW, SSD_HPG), lambda g, i: (0, 0))
    sm = jax.ShapeDtypeStruct((SSD_GROUPS, 1, SSD_HPG), f32)
    expander = _expander()
    return pl.pallas_call(
        body, name="ssd_bwd", grid=(SSD_GROUPS, nb),
        in_specs=[grp, ph, ph, pht, ph, gvec, small, ex, ext, xs, stspec],
        out_specs=[grp, ph, gvec, small, small],
        out_shape=[jax.ShapeDtypeStruct((S, CONV_DIM), f32), jax.ShapeDtypeStruct((SSD_GROUPS, S, SSD_HPG), f32),
                   jax.ShapeDtypeStruct((SSD_GROUPS, 1, SSD_GW), f32), sm, sm],
        scratch_shapes=[pltpu.VMEM((SSD_STATE, SSD_GW), f32), pltpu.VMEM((1, SSD_GW), f32)],
        compiler_params=_params(("parallel", "arbitrary")),
    )(pre, dt_g, ac_g, act_g, sg_g, dskx, nega_g, expander, expander.T, dy, states)


def _tiles(S):
    return dict(tb=min(512, S), tr=min(256, S), tm=min(1024, S))


def _local_step(x, positions, target, norm1_w, w_main, w_dt, conv_w, conv_b, dt_bias, a_log, d_skip, ssd_norm_w,
                w_br, w_bs, w_o, norm_f_w):
    S, D = x.shape
    t = _tiles(S)
    tb, tr, tm = t["tb"], t["tr"], t["tm"]

    half = RET_DK // 2
    inv_freq = ROPE_THETA ** (-jnp.arange(half, dtype=f32) / half)
    ang = positions.astype(f32)[:, None] * inv_freq
    cos, sin = jnp.cos(ang), jnp.sin(ang)
    log_gamma = jnp.log1p(-(2.0 ** (-5.0 - jnp.arange(RET_HEADS, dtype=f32))))
    idx = jnp.arange(CHUNK, dtype=f32)
    intra = jnp.exp(jnp.abs(idx[:, None] - idx[None, :]) * log_gamma[:, None, None])
    qdec = jnp.exp((idx + 1.0)[None, :] * log_gamma[:, None])[:, :, None]
    kdec = jnp.exp((CHUNK - 1.0 - idx)[None, :] * log_gamma[:, None])[:, :, None]
    cdec = jnp.exp(CHUNK * log_gamma)[:, None, None]
    conv_wm, conv_bm = _xbc_group_major(conv_w), _xbc_group_major(conv_b)

    h, ht = _norm1_fwd(x, norm1_w, tr)
    proj = _mm1(h, w_main, tm=tm, tn=1024, tk=D, out_dtype=f32, name="proj_main")
    dt_raw = _mm1(h, w_dt, tm=tm, tn=128, tk=D, out_dtype=f32, name="proj_dt")
    y_ret, yr, yrt, ret_states = _ret_fwd(proj, cos, sin, intra, qdec, kdec, cdec, tb)
    pre = _conv_fwd(proj, conv_wm, conv_bm, min(1024, S), 512)
    pad64 = lambda v: jnp.pad(v, ((0, 0), (0, 128 - SSD_HEADS)))
    dt, sg, ac = _dt_prep(dt_raw, pad64(dt_bias), pad64(a_log), tb)
    dt_g, ac_g, sg_g, act_g = _group_major(dt), _group_major(ac), _group_major(sg), _group_major_t(ac)
    dskx = jnp.repeat(d_skip.reshape(SSD_GROUPS, 1, SSD_HPG), SSD_GW // SSD_HPG, axis=2)
    nega_g = (-jnp.exp(a_log)).reshape(SSD_GROUPS, 1, SSD_HPG)
    y_ssd, ssd_states = _ssd_fwd(pre, dt_g, ac_g, act_g, dskx, tb)
    ys, yst = _ssd_norm_fwd(y_ssd, proj, ssd_norm_w, tr // 2)
    p_r = _mm1(yr, w_br, tm=tm, tn=1024, tk=2048, out_dtype=f32, name="branch_ret")
    p_s = _mm1(ys, w_bs, tm=tm, tn=1024, tk=2048, out_dtype=f32, name="branch_ssd")
    merged, mergedt = _merge_fwd(p_r, p_s, proj, tr)
    mo = _mm1(merged, w_o, tm=tm, tn=1024, tk=2048, out_dtype=f32, name="out_proj")
    dx2, dx2b, loss, g_norm_f = _final_fwd_bwd(x, mo, target, norm_f_w.reshape(1, D), tr)

    tkt = min(4096, S)
    wg = lambda at, b, name, tn=1024: _mm1(at, b, tm=min(1024, at.shape[0]), tn=tn, tk=tkt, out_dtype=f32, name=name)
    dm = _mm1(dx2b, w_o, tm=tm, tn=1024, tk=2048, out_dtype=f32, name="d_merged", tb=True)
    g_w_o = wg(mergedt, dx2b, "g_w_out")
    dp_r, dp_s, dproj = _merge_bwd(dm, p_r, p_s, proj, tr)
    dyr = _mm1(dp_r, w_br, tm=tm, tn=1024, tk=2048, out_dtype=f32, name="d_yr", tb=True)
    dys = _mm1(dp_s, w_bs, tm=tm, tn=1024, tk=2048, out_dtype=f32, name="d_ys", tb=True)
    g_w_br = wg(yrt, dp_r, "g_w_br_ret")
    g_w_bs = wg(yst, dp_s, "g_w_br_ssd")
    dy_ssd, dproj, g_ssd_norm = _ssd_norm_bwd(y_ssd, proj, ssd_norm_w, dys, dproj, tr // 2)
    dproj = _ret_bwd(proj, cos, sin, intra, qdec, kdec, cdec, y_ret, dyr, ret_states, dproj, tb)
    dpre, ddt_g, gsk, gal, gdb = _ssd_bwd(pre, dt_g, ac_g, act_g, sg_g, dskx, nega_g, dy_ssd, ssd_states, tb)
    dproj, gcw, gcb = _conv_bwd(dpre, proj, conv_wm, dproj, min(1024, S), 512)
    ddt = jnp.transpose(ddt_g, (1, 0, 2)).reshape(S, SSD_HEADS)
    ddt_p = jnp.pad(ddt, ((0, 0), (0, 128 - SSD_HEADS))).astype(bf16)

    dh = _mm([(dproj, 0, w_main, 0, 0, N_MAIN, N_MAIN // 8), (ddt_p, 0, w_dt, 0, 0, 128, 128)], S, D, tm=tm, tn=1024,
             out_dtype=f32, name="d_h", tb=True)
    g_main = wg(ht, dproj, "g_w_in_main")
    g_dt = wg(ht, ddt_p, "g_w_in_dt", tn=128)
    grad_x, g_norm1 = _norm1_bwd(x, norm1_w, dh, dx2, tr)

    seg = lambda v: jnp.sum(v.reshape(SSD_HEADS, SSD_GW // SSD_HPG), axis=1).reshape(1, SSD_HEADS)
    grads = dict(
        norm1_w=g_norm1, w_in_main=g_main, w_in_dt=g_dt,
        conv_w=_xbc_original(gcw), conv_b=_xbc_original(gcb),
        dt_bias=gdb.reshape(1, SSD_HEADS), a_log=gal.reshape(1, SSD_HEADS), d_skip=seg(gsk),
        ssd_norm_w=g_ssd_norm, w_br_ret=g_w_br, w_br_ssd=g_w_bs, w_out=g_w_o, norm_f_w=g_norm_f,
    )
    return loss, grad_x, grads


def _me():
    return lax.axis_index("x"), lax.axis_index("y"), lax.axis_index("c")


def _other_chips(x, y):
    return [(1 - x, y), (x, 1 - y), (1 - x, 1 - y)]


def _gather_weights(a, b, cw):
    R = a.shape[0]
    hr = R // 2

    def body(a_ref, b_ref, cw_ref, ga_ref, gb_ref, gc_ref, send_sems, recv_sems):
        x, y, c = _me()
        k = 2 * x + y
        sibling = (x, y, 1 - c)
        chips = _other_chips(x, y)

        def small(j, src_shard, to):
            return pltpu.make_async_remote_copy(
                src_ref=cw_ref, dst_ref=gc_ref.at[src_shard], send_sem=send_sems.at[12 + j], recv_sem=recv_sems.at[12 + j],
                device_id=to, device_id_type=MESH)

        def copies(j, src_shard, half, to, from_input):
            rows = pl.ds(half * hr, hr)
            out = []
            for t, (inp, g) in enumerate(((a_ref, ga_ref), (b_ref, gb_ref))):
                src = inp.at[rows, :] if from_input else g.at[src_shard, rows, :]
                out.append(pltpu.make_async_remote_copy(
                    src_ref=src, dst_ref=g.at[src_shard, rows, :], send_sem=send_sems.at[2 * j + t],
                    recv_sem=recv_sems.at[2 * j + t], device_id=to, device_id_type=MESH))
            return out

        first = []
        for j, chip in enumerate(chips):
            first += copies(j, k, c, (*chip, c), True)
            first.append(small(j, k, (*chip, c)))
        for cp in first:
            cp.start()
        passed = []
        for j, chip in enumerate(chips):
            kk = 2 * chip[0] + chip[1]
            for cp in copies(j, kk, c, (x, y, c), False):
                cp.wait_recv()
            fw = copies(3 + j, kk, c, sibling, False)
            for cp in fw:
                cp.start()
            passed += fw
        for j, chip in enumerate(chips):
            kk = 2 * chip[0] + chip[1]
            for cp in copies(3 + j, kk, 1 - c, (x, y, c), False):
                cp.wait_recv()
            small(j, kk, (x, y, c)).wait_recv()
        for cp in first + passed:
            cp.wait_send()

    return pl.pallas_call(
        body, name="gather_weights", in_specs=[ANY, ANY, ANY], out_specs=[ANY, ANY, ANY],
        out_shape=[jax.ShapeDtypeStruct((N_SHARD,) + a.shape, a.dtype), jax.ShapeDtypeStruct((N_SHARD,) + b.shape, b.dtype),
                   jax.ShapeDtypeStruct((N_SHARD,) + cw.shape, cw.dtype)],
        scratch_shapes=[pltpu.SemaphoreType.DMA((15,)), pltpu.SemaphoreType.DMA((15,))],
        compiler_params=pltpu.CompilerParams(has_side_effects=True),
    )(a, b, cw)


def _sibling_swap(arrs, name):
    n = len(arrs)
    slots = [(t, s) for t in range(n) for s in range(arrs[t].shape[0])]

    def body(*refs):
        ins, outs = refs[:n], refs[n:2 * n]
        send_sems, recv_sems = refs[2 * n], refs[2 * n + 1]
        x, y, c = _me()
        cps = [pltpu.make_async_remote_copy(src_ref=ins[t].at[s, 1 - c], dst_ref=outs[t].at[s], send_sem=send_sems.at[q],
                                            recv_sem=recv_sems.at[q], device_id=(x, y, 1 - c), device_id_type=MESH)
               for q, (t, s) in enumerate(slots)]
        for cp in cps:
            cp.start()
        for cp in cps:
            cp.wait()

    return pl.pallas_call(
        body, name=name, in_specs=[ANY] * n, out_specs=[ANY] * n,
        out_shape=[jax.ShapeDtypeStruct(a.shape[:1] + a.shape[2:], a.dtype) for a in arrs],
        scratch_shapes=[pltpu.SemaphoreType.DMA((len(slots),)), pltpu.SemaphoreType.DMA((len(slots),))],
        compiler_params=pltpu.CompilerParams(has_side_effects=True),
    )(*arrs)


def _chip_exchange(arrs, name):
    n = len(arrs)

    def body(*refs):
        ins, outs = refs[:n], refs[n:2 * n]
        send_sems, recv_sems = refs[2 * n], refs[2 * n + 1]
        x, y, c = _me()
        cps = []
        for j, chip in enumerate(_other_chips(x, y)):
            kk = 2 * chip[0] + chip[1]
            for t in range(n):
                cps.append(pltpu.make_async_remote_copy(
                    src_ref=ins[t].at[kk], dst_ref=outs[t].at[j], send_sem=send_sems.at[n * j + t],
                    recv_sem=recv_sems.at[n * j + t], device_id=(*chip, c), device_id_type=MESH))
        for cp in cps:
            cp.start()
        for cp in cps:
            cp.wait()

    return pl.pallas_call(
        body, name=name, in_specs=[ANY] * n, out_specs=[ANY] * n,
        out_shape=[jax.ShapeDtypeStruct((3,) + a.shape[1:], a.dtype) for a in arrs],
        scratch_shapes=[pltpu.SemaphoreType.DMA((3 * n,)), pltpu.SemaphoreType.DMA((3 * n,))],
        compiler_params=pltpu.CompilerParams(has_side_effects=True),
    )(*arrs)


def _send_to_sibling(arrs, by_cols, name):
    n = len(arrs)

    def body(*refs):
        ins, outs = refs[:n], refs[n:2 * n]
        send_sems, recv_sems = refs[2 * n], refs[2 * n + 1]
        x, y, c = _me()

        def part(t, half):
            if by_cols[t]:
                w = arrs[t].shape[1]
                return outs[t].at[:, pl.ds(pl.multiple_of(half * w, 128), w)]
            return outs[t].at[half]

        sends = [pltpu.make_async_remote_copy(src_ref=ins[t], dst_ref=part(t, c), send_sem=send_sems.at[t], recv_sem=recv_sems.at[t],
                                              device_id=(x, y, 1 - c), device_id_type=MESH) for t in range(n)]
        for cp in sends:
            cp.start()
        for t in range(n):
            pltpu.make_async_remote_copy(src_ref=ins[t], dst_ref=part(t, 1 - c), send_sem=send_sems.at[t], recv_sem=recv_sems.at[t],
                                         device_id=(x, y, c), device_id_type=MESH).wait_recv()
        for cp in sends:
            cp.wait_send()

    return pl.pallas_call(
        body, name=name, in_specs=[ANY] * n, out_specs=[ANY] * n,
        out_shape=[jax.ShapeDtypeStruct((a.shape[0], 2 * a.shape[1]) if bc else (2,) + a.shape, a.dtype) for a, bc in zip(arrs, by_cols)],
        scratch_shapes=[pltpu.SemaphoreType.DMA((n,)), pltpu.SemaphoreType.DMA((n,))],
        compiler_params=pltpu.CompilerParams(has_side_effects=True),
    )(*arrs)


def _gather_vec(v):
    n = v.shape[1]

    def body(v_ref, o_ref, send_sems, recv_sems):
        x, y, c = _me()
        me = 4 * x + 2 * y + c
        cps = []
        for j in range(1, 8):
            fx, fy, fc = (j >> 2) & 1, (j >> 1) & 1, j & 1
            peer = (x ^ fx, y ^ fy, c ^ fc)
            cps.append(pltpu.make_async_remote_copy(
                src_ref=v_ref, dst_ref=o_ref.at[pl.ds(me, 1), :], send_sem=send_sems.at[j - 1], recv_sem=recv_sems.at[j - 1],
                device_id=peer, device_id_type=MESH))
        for cp in cps:
            cp.start()
        for j in range(1, 8):
            fx, fy, fc = (j >> 2) & 1, (j >> 1) & 1, j & 1
            src = 4 * (x ^ fx) + 2 * (y ^ fy) + (c ^ fc)
            pltpu.make_async_remote_copy(
                src_ref=v_ref, dst_ref=o_ref.at[pl.ds(src, 1), :], send_sem=send_sems.at[j - 1], recv_sem=recv_sems.at[j - 1],
                device_id=(x, y, c), device_id_type=MESH).wait_recv()
        for cp in cps:
            cp.wait_send()

    return pl.pallas_call(
        body, name="gather_vec", in_specs=[ANY], out_specs=ANY, out_shape=jax.ShapeDtypeStruct((8, n), v.dtype),
        scratch_shapes=[pltpu.SemaphoreType.DMA((7,)), pltpu.SemaphoreType.DMA((7,))],
        compiler_params=pltpu.CompilerParams(has_side_effects=True),
    )(v)


def _pair_sum(g, r, name, tr):
    L, _, hr, C = g.shape

    def body(c_ref, g_ref, r_ref, o_ref):
        def strip(rows):
            o_ref[0, rows, :] = (g_ref[0, 0, rows, :] + r_ref[0, rows, :]).astype(bf16)
        _for_strips(tr, strip)

    grid_spec = pltpu.PrefetchScalarGridSpec(
        num_scalar_prefetch=1, grid=(L, hr // tr),
        in_specs=[pl.BlockSpec((1, 1, tr, C), lambda s, i, c_ref: (s, c_ref[0], i, 0)),
                  pl.BlockSpec((1, tr, C), lambda s, i, c_ref: (s, i, 0))],
        out_specs=pl.BlockSpec((1, tr, C), lambda s, i, c_ref: (s, i, 0)))
    c = lax.axis_index("c").reshape(1).astype(jnp.int32)
    return pl.pallas_call(body, name=name, grid_spec=grid_spec, out_shape=jax.ShapeDtypeStruct((L, hr, C), bf16),
                          compiler_params=_params(("parallel", "parallel")))(c, g, r)


def _own_sum(p, got, name, transposed=False):
    _, hr, C = p.shape
    tr = SUM_ROWS
    c_full, c_pad = C // 128 * 128, -(-C // 128) * 128

    def total(p_ref, got_ref, rows):
        return ((p_ref[0, rows, :].astype(f32) + got_ref[0, rows, :].astype(f32)) + got_ref[1, rows, :].astype(f32)) \
            + got_ref[2, rows, :].astype(f32)

    def body(k_ref, p_ref, got_ref, o_ref):
        def strip(rows):
            o_ref[rows, :] = total(p_ref, got_ref, rows)
        _for_strips(tr, strip)

    def body_t(k_ref, p_ref, got_ref, o_ref, buf):
        if c_pad > c_full:
            buf[:, pl.ds(c_full, c_pad - c_full)] = jnp.zeros((tr, c_pad - c_full), f32)

        def strip(rows):
            buf[rows, pl.ds(0, C)] = total(p_ref, got_ref, rows)
        _for_strips(tr, strip)
        o_ref[...] = buf[...].T[:C]

    in_specs = [pl.BlockSpec((1, tr, C), lambda i, k_ref: (k_ref[0], i, 0)), pl.BlockSpec((3, tr, C), lambda i, k_ref: (0, i, 0))]
    x, y, c = _me()
    k = (2 * x + y).reshape(1).astype(jnp.int32)
    if transposed:
        grid_spec = pltpu.PrefetchScalarGridSpec(num_scalar_prefetch=1, grid=(hr // tr,), in_specs=in_specs,
                                                 out_specs=pl.BlockSpec((C, tr), lambda i, k_ref: (0, i)),
                                                 scratch_shapes=[pltpu.VMEM((tr, c_pad), f32)])
        return pl.pallas_call(body_t, name=name, grid_spec=grid_spec, out_shape=jax.ShapeDtypeStruct((C, hr), f32),
                              compiler_params=_params(("parallel",)))(k, p, got)
    grid_spec = pltpu.PrefetchScalarGridSpec(num_scalar_prefetch=1, grid=(hr // tr,), in_specs=in_specs,
                                             out_specs=pl.BlockSpec((tr, C), lambda i, k_ref: (i, 0)))
    return pl.pallas_call(body, name=name, grid_spec=grid_spec, out_shape=jax.ShapeDtypeStruct((hr, C), f32),
                          compiler_params=_params(("parallel",)))(k, p, got)


def _adamw(w, g, m, v, name, tr=None, tc=None):
    _, R, C = w.shape

    def body(w_ref, g_ref, m_ref, v_ref, d_ref, nm_ref, nv_ref):
        def strip(rows):
            gv = g_ref[0, rows, :]
            mn = ADAM_B1 * m_ref[0, rows, :] + (1.0 - ADAM_B1) * gv
            vn = ADAM_B2 * v_ref[0, rows, :] + (1.0 - ADAM_B2) * (gv * gv)
            m_hat = mn / (1.0 - ADAM_B1 ** ADAM_STEP)
            v_hat = vn / (1.0 - ADAM_B2 ** ADAM_STEP)
            d_ref[0, rows, :] = -ADAM_LR * (m_hat / (jnp.sqrt(v_hat) + ADAM_EPS) + ADAM_WD * w_ref[0, rows, :])
            nm_ref[0, rows, :] = mn
            nv_ref[0, rows, :] = vn
        n_rows = R if tc else tr
        _for_strips(n_rows, strip, rs=min(8, n_rows), unroll=2)

    if tc:
        blk, grid = pl.BlockSpec((1, R, tc), lambda i: (0, 0, i)), (C // tc,)
    else:
        blk, grid = pl.BlockSpec((1, tr, C), lambda i: (0, i, 0)), (R // tr,)
    o = jax.ShapeDtypeStruct((1, R, C), f32)
    return pl.pallas_call(body, name=name, grid=grid, in_specs=[blk] * 4, out_specs=[blk] * 3, out_shape=[o, o, o],
                          compiler_params=_params(("parallel",)))(w, g, m, v)


def _sum8(t):
    n = t.shape[1]

    def body(t_ref, o_ref):
        acc = t_ref[pl.ds(0, 1), :]
        for r in range(1, 8):
            acc = acc + t_ref[pl.ds(r, 1), :]
        o_ref[...] = acc

    return pl.pallas_call(body, name="sum_devices", out_shape=jax.ShapeDtypeStruct((1, n), f32))(t)


def _reduce_to_owner(g_main, g_dt, g_b):
    xi, yi, ci = _me()
    D = g_main.shape[0]
    hr = D // 2
    halves = lambda t: t.reshape((-1, 2, hr, t.shape[-1]))
    g_main, g_dt, g_b = halves(g_main), halves(g_dt), halves(g_b)
    r_main, r_dt, r_b = _sibling_swap([g_main, g_dt, g_b], "sibling_swap")
    p_main = _pair_sum(g_main, r_main, "pair_sum_main", SUM_ROWS // 4)
    p_dt = _pair_sum(g_dt, r_dt, "pair_sum_dt", SUM_ROWS)
    p_b = _pair_sum(g_b, r_b, "pair_sum_b", SUM_ROWS)
    p_in = jnp.transpose(_w_in_grad_full(p_main[0], p_dt[0]).reshape(hr, N_SHARD, W_IN_SHARD), (1, 0, 2))
    got_in, got_b = _chip_exchange([p_in, p_b], "chip_exchange")
    mine_in, mine_b = _own_sum(p_in, got_in, "own_sum_in", transposed=True), _own_sum(p_b, got_b, "own_sum_b")
    both_in, both_b = _send_to_sibling([mine_in, mine_b], [True, False], "share_halves")
    full_in_t = lax.dynamic_update_slice_in_dim(both_in, mine_in, ci * hr, axis=1)
    full_b = lax.dynamic_update_slice_in_dim(both_b, mine_b[None], ci, axis=0).reshape(2 * hr, mine_b.shape[1])
    return full_in_t, full_b


def kernel(x, positions, norm1_w, w_in, conv_w, conv_b, dt_bias, a_log, d_skip, ssd_norm_w, w_br_ret, w_br_ssd, w_out, norm_f_w, loss_target, m_norm1_w, m_w_in, m_conv_w, m_conv_b, m_dt_bias, m_a_log, m_d_skip, m_ssd_norm_w, m_w_br_ret, m_w_br_ssd, m_w_out, m_norm_f_w, v_norm1_w, v_w_in, v_conv_w, v_conv_b, v_dt_bias, v_a_log, v_d_skip, v_ssd_norm_w, v_w_br_ret, v_w_br_ssd, v_w_out, v_norm_f_w):
    D = D_MODEL
    xi, yi, ci = _me()
    k = 2 * xi + yi
    me = 2 * k + ci
    weights = dict(norm1_w=norm1_w, w_in=w_in, conv_w=conv_w, conv_b=conv_b, dt_bias=dt_bias, a_log=a_log, d_skip=d_skip,
                   ssd_norm_w=ssd_norm_w, w_br_ret=w_br_ret, w_br_ssd=w_br_ssd, w_out=w_out, norm_f_w=norm_f_w)
    mom1 = dict(norm1_w=m_norm1_w, w_in=m_w_in, conv_w=m_conv_w, conv_b=m_conv_b, dt_bias=m_dt_bias, a_log=m_a_log, d_skip=m_d_skip,
                ssd_norm_w=m_ssd_norm_w, w_br_ret=m_w_br_ret, w_br_ssd=m_w_br_ssd, w_out=m_w_out, norm_f_w=m_norm_f_w)
    mom2 = dict(norm1_w=v_norm1_w, w_in=v_w_in, conv_w=v_conv_w, conv_b=v_conv_b, dt_bias=v_dt_bias, a_log=v_a_log, d_skip=v_d_skip,
                ssd_norm_w=v_ssd_norm_w, w_br_ret=v_w_br_ret, w_br_ssd=v_w_br_ssd, w_out=v_w_out, norm_f_w=v_norm_f_w)

    a_sh = w_in[0].astype(bf16)
    b_sh = jnp.concatenate([w_br_ret[0], w_br_ssd[0], w_out[0]], axis=0).astype(bf16)
    ga, gb, gc = _gather_weights(a_sh, b_sh, conv_w[0])
    own = lambda g, s: lax.dynamic_update_slice_in_dim(g, s[None], k, axis=0)
    ga, gb, gc = own(ga, a_sh), own(gb, b_sh), own(gc, conv_w[0])
    w_main, w_dt = _split_w_in(jnp.transpose(ga, (1, 0, 2)).reshape(D, IN_PROJ))
    w_br = gb[:, 0:512].reshape(2048, D)
    w_bs = gb[:, 512:1536].reshape(4096, D)
    w_o = gb[:, 1536:2048].reshape(2048, D)
    conv_full = jnp.transpose(gc, (1, 0, 2)).reshape(SSD_CONV, CONV_DIM)

    loss, grad_x, g = _local_step(x[0], positions[0], loss_target[0], norm1_w, w_main, w_dt, conv_full, conv_b, dt_bias, a_log,
                                  d_skip, ssd_norm_w, w_br, w_bs, w_o, norm_f_w)

    g_b = jnp.concatenate([g["w_br_ret"].reshape(N_SHARD, 512, D), g["w_br_ssd"].reshape(N_SHARD, 1024, D),
                           g["w_out"].reshape(N_SHARD, 512, D)], axis=1)
    grad_w_in_t, full_b = _reduce_to_owner(g["w_in_main"], g["w_in_dt"], g_b)
    grad_mats = dict(w_br_ret=full_b[0:512], w_br_ssd=full_b[512:1536], w_out=full_b[1536:2048])

    small = [(n, weights[n].size) for n in ("norm1_w", "conv_b", "dt_bias", "a_log", "d_skip", "ssd_norm_w", "norm_f_w")]
    parts = [jnp.pad(loss.reshape(1, 1), ((0, 0), (0, 127)))] + [g[n].reshape(1, -1) for n, _ in small] + [g["conv_w"].reshape(1, -1)]
    vec = jnp.concatenate(parts, axis=1)
    nv = vec.shape[1]
    nvp = -(-nv // 128) * 128
    vec = jnp.pad(vec, ((0, 0), (0, nvp - nv)))
    total = _sum8(lax.dynamic_update_slice_in_dim(_gather_vec(vec), vec, me, axis=0))
    loss_out = total[0, 0]
    off = 128
    grad_small = {}
    for n, sz in small:
        grad_small[n] = total[:, off:off + sz]
        off += sz
    g_conv = total[:, off:off + SSD_CONV * CONV_DIM].reshape(SSD_CONV, CONV_DIM)
    g_conv = lax.dynamic_slice_in_dim(g_conv, k * (CONV_DIM // N_SHARD), CONV_DIM // N_SHARD, axis=1)
    grad_small["conv_w"] = g_conv.reshape(1, -1)

    upd = {}
    for n in ("w_br_ret", "w_br_ssd", "w_out"):
        upd[n] = _adamw(weights[n], grad_mats[n][None], mom1[n], mom2[n], "adamw_" + n, tr=SUM_ROWS)
    tp = lambda t: jnp.swapaxes(t, 1, 2)
    upd["w_in"] = tuple(tp(t) for t in _adamw(tp(w_in), grad_w_in_t[None], tp(m_w_in), tp(v_w_in), "adamw_w_in", tc=128))
    grad_mats["w_in"] = tp(grad_w_in_t[None])
    names_small = [n for n, _ in small] + ["conv_w"]
    flat = lambda d: jnp.concatenate([d[n].reshape(1, -1) for n in names_small], axis=1)
    ns = sum(weights[n].size for n in names_small)
    nsp = -(-ns // 128) * 128
    padv = lambda t: jnp.pad(t, ((0, 0), (0, nsp - ns)))
    small_upd = _adamw(padv(flat(weights))[None], padv(flat(grad_small))[None], padv(flat(mom1))[None],
                       jnp.pad(flat(mom2), ((0, 0), (0, nsp - ns)), constant_values=1.0)[None], "adamw_small", 1)
    off = 0
    for n in names_small:
        sz = weights[n].size
        upd[n] = tuple(t[0, :, off:off + sz] for t in small_upd)
        off += sz

    order = ["norm1_w", "w_in", "conv_w", "conv_b", "dt_bias", "a_log", "d_skip", "ssd_norm_w", "w_br_ret", "w_br_ssd", "w_out", "norm_f_w"]
    grads_out = {**grad_mats, **grad_small}
    shp = lambda n, t: t.reshape(weights[n].shape)
    return (loss_out, grad_x[None], *[shp(n, grads_out[n]) for n in order], *[shp(n, upd[n][0]) for n in order],
            *[shp(n, upd[n][1]) for n in order], *[shp(n, upd[n][2]) for n in order])
```

```python
import jax
import jax.numpy as jnp
import numpy as np
from jax import lax
from jax.experimental import pallas as pl
from jax.experimental.pallas import tpu as pltpu

f32 = jnp.float32
bf16 = jnp.bfloat16
HIGHEST = lax.Precision.HIGHEST
MESH = pl.DeviceIdType.MESH

D_MODEL = 2048
EPS = 1e-6
CHUNK = 64
RET_HEADS = 8
RET_DK = 256
RET_HW = 4 * RET_DK
ROPE_THETA = 10000.0
SSD_WIDTH = 4096
SSD_GROUPS = 8
SSD_STATE = 128
SSD_GW = 512
SSD_GC = SSD_GW + 2 * SSD_STATE
SSD_HPG = 8
SSD_CONV = 4
CONV_DIM = 6144
SSD_HEADS = 64
LS = 128

C_RET, C_Z, C_GATES, C_XBC = 0, 8192, 12288, 16384
N_MAIN = 22528
DT_OFF = 18432
IN_PROJ = 22592
N_SHARD = 4
W_IN_SHARD = IN_PROJ // N_SHARD

ADAM_LR, ADAM_B1, ADAM_B2, ADAM_EPS, ADAM_WD, ADAM_STEP = 0.001, 0.9, 0.999, 1e-08, 0.01, 10

VMEM_LIMIT = 56 * 1024 * 1024
SUM_ROWS = 128
ANY = pl.BlockSpec(memory_space=pl.ANY)


def _params(dims):
    return pltpu.CompilerParams(dimension_semantics=dims, vmem_limit_bytes=VMEM_LIMIT)


def _silu(x):
    return x * jax.nn.sigmoid(x)


def _dsilu(x):
    s = jax.nn.sigmoid(x)
    return s * (1.0 + x * (1.0 - s))


def _nt(a, b):
    return lax.dot_general(a, b, (((1,), (1,)), ((), ())), preferred_element_type=f32)


def _tn(a, b):
    return lax.dot_general(a, b, (((0,), (0,)), ((), ())), preferred_element_type=f32)


def _nn(a, b):
    return jnp.dot(a, b, preferred_element_type=f32)


def _hi(a, b):
    return jnp.dot(a, b, precision=HIGHEST, preferred_element_type=f32)


def _split(a):
    hi = a.astype(bf16)
    return hi, (a - hi.astype(f32)).astype(bf16)


def _sel_r(a, sel):
    hi, lo = _split(a)
    return _nn(hi, sel) + _nn(lo, sel)


def _sel_l(sel, a):
    hi, lo = _split(a)
    return _nn(sel, hi) + _nn(sel, lo)


def _xbc_group_major(t):
    R = t.shape[0]
    nb = SSD_GROUPS * SSD_STATE
    parts = [t[:, :SSD_WIDTH].reshape(R, SSD_GROUPS, SSD_GW), t[:, SSD_WIDTH:SSD_WIDTH + nb].reshape(R, SSD_GROUPS, SSD_STATE),
             t[:, SSD_WIDTH + nb:].reshape(R, SSD_GROUPS, SSD_STATE)]
    return jnp.concatenate(parts, axis=2).reshape(R, CONV_DIM)


def _xbc_original(t):
    R = t.shape[0]
    g = t.reshape(R, SSD_GROUPS, SSD_GC)
    parts = [g[:, :, :SSD_GW].reshape(R, SSD_WIDTH), g[:, :, SSD_GW:SSD_GW + SSD_STATE].reshape(R, SSD_GROUPS * SSD_STATE),
             g[:, :, SSD_GW + SSD_STATE:].reshape(R, SSD_GROUPS * SSD_STATE)]
    return jnp.concatenate(parts, axis=1)


def _split_w_in(w):
    D = w.shape[0]
    ret = jnp.transpose(w[:, :4 * 2048].reshape(D, 4, RET_HEADS, RET_DK), (0, 2, 1, 3)).reshape(D, 4 * 2048)
    w_dt = jnp.pad(w[:, DT_OFF:DT_OFF + SSD_HEADS], ((0, 0), (0, 128 - SSD_HEADS)))
    main = jnp.concatenate([ret, w[:, 8192:12288], w[:, DT_OFF + SSD_HEADS:], _xbc_group_major(w[:, 12288:DT_OFF])], axis=1)
    return main, w_dt


def _w_in_grad_full(g_main, g_dt):
    D = g_main.shape[0]
    ret = jnp.transpose(g_main[:, :C_Z].reshape(D, RET_HEADS, 4, RET_DK), (0, 2, 1, 3)).reshape(D, C_Z)
    return jnp.concatenate([ret, g_main[:, C_Z:C_GATES], _xbc_original(g_main[:, C_XBC:]), g_dt[:, :SSD_HEADS],
                            g_main[:, C_GATES:C_XBC]], axis=1)


def _mm(pairs, M, N, *, tm, tn, out_dtype, name, tb=False):
    P = len(pairs)
    nks = [K // tk for (_, _, _, _, _, K, tk) in pairs]
    starts = [int(s) for s in np.cumsum([0] + nks[:-1])]
    KT = int(sum(nks))
    in_specs, args = [], []
    for (a, a_cb, b, b_kb, b_nb, K, tk), s, nk in zip(pairs, starts, nks):
        def kk(k, s=s, nk=nk):
            return jnp.clip(k - s, 0, nk - 1)
        in_specs.append(pl.BlockSpec((tm, tk), lambda m, n, k, kk=kk, a_cb=a_cb: (m, a_cb + kk(k))))
        if tb:
            in_specs.append(pl.BlockSpec((tn, tk), lambda m, n, k, kk=kk, b_kb=b_kb, b_nb=b_nb: (b_nb + n, b_kb + kk(k))))
        else:
            in_specs.append(pl.BlockSpec((tk, tn), lambda m, n, k, kk=kk, b_kb=b_kb, b_nb=b_nb: (b_kb + kk(k), b_nb + n)))
        args += [a, b]

    def body(*refs):
        o_ref = refs[2 * P]
        k = pl.program_id(2)

        def prod(i):
            a = refs[2 * i][...].astype(bf16)
            b = refs[2 * i + 1][...].astype(bf16)
            return _nt(a, b) if tb else _nn(a, b)

        if KT == 1:
            o_ref[...] = prod(0).astype(out_dtype)
            return
        acc = refs[2 * P + 1]

        @pl.when(k == 0)
        def _():
            acc[...] = jnp.zeros_like(acc)

        for i in range(P):
            @pl.when((k >= starts[i]) & (k < starts[i] + nks[i]))
            def _(i=i):
                acc[...] += prod(i)

        @pl.when(k == KT - 1)
        def _():
            o_ref[...] = acc[...].astype(out_dtype)

    return pl.pallas_call(
        body, name=name, grid=(M // tm, N // tn, KT), in_specs=in_specs,
        out_specs=pl.BlockSpec((tm, tn), lambda m, n, k: (m, n)),
        out_shape=jax.ShapeDtypeStruct((M, N), out_dtype),
        scratch_shapes=[] if KT == 1 else [pltpu.VMEM((tm, tn), f32)],
        compiler_params=_params(("parallel", "parallel", "arbitrary")),
    )(*args)


def _mm1(a, b, *, tm, tn, tk, out_dtype, name, tb=False):
    M, K = a.shape
    N = b.shape[0] if tb else b.shape[1]
    return _mm([(a, 0, b, 0, 0, K, tk)], M, N, tm=tm, tn=tn, out_dtype=out_dtype, name=name, tb=tb)


RS = 16
CS = 32


def _for_strips(n_rows, fn, rs=RS, unroll=4):
    def step(s, carry):
        fn(pl.ds(pl.multiple_of(s * rs, rs), rs))
        return carry
    n = n_rows // rs
    lax.fori_loop(0, n, step, 0, unroll=min(unroll, n))


def _norm1_fwd(x, w, tr):
    S, D = x.shape

    def body(x_ref, w_ref, h_ref, ht_ref):
        def strip(rows):
            xv = x_ref[rows, :]
            r = lax.rsqrt(jnp.mean(xv * xv, axis=-1, keepdims=True) + EPS)
            h_ref[rows, :] = (xv * r * w_ref[...]).astype(bf16)
        _for_strips(tr, strip)
        ht_ref[...] = h_ref[...].T

    return pl.pallas_call(
        body, name="norm1_fwd", grid=(S // tr,),
        in_specs=[pl.BlockSpec((tr, D), lambda i: (i, 0)), pl.BlockSpec((1, D), lambda i: (0, 0))],
        out_specs=[pl.BlockSpec((tr, D), lambda i: (i, 0)), pl.BlockSpec((D, tr), lambda i: (0, i))],
        out_shape=[jax.ShapeDtypeStruct((S, D), bf16), jax.ShapeDtypeStruct((D, S), bf16)], compiler_params=_params(("parallel",)),
    )(x, w)


def _norm1_bwd(x, w, dh, dx2, tr):
    S, D = x.shape

    def body(x_ref, w_ref, dh_ref, dx2_ref, gx_ref, gw_ref, acc):
        @pl.when(pl.program_id(0) == 0)
        def _():
            acc[...] = jnp.zeros_like(acc)

        def strip(rows):
            xv = x_ref[rows, :]
            r = lax.rsqrt(jnp.mean(xv * xv, axis=-1, keepdims=True) + EPS)
            xh = xv * r
            dhv = dh_ref[rows, :]
            acc[...] += dhv * xh
            dxh = dhv * w_ref[...]
            gx_ref[rows, :] = dx2_ref[rows, :] + r * (dxh - xh * jnp.mean(dxh * xh, axis=-1, keepdims=True))
        _for_strips(tr, strip)

        @pl.when(pl.program_id(0) == S // tr - 1)
        def _():
            gw_ref[...] = jnp.sum(acc[...], axis=0, keepdims=True)

    row = pl.BlockSpec((tr, D), lambda i: (i, 0))
    vec = pl.BlockSpec((1, D), lambda i: (0, 0))
    return pl.pallas_call(
        body, name="norm1_bwd", grid=(S // tr,), in_specs=[row, vec, row, row], out_specs=[row, vec],
        out_shape=[jax.ShapeDtypeStruct((S, D), f32), jax.ShapeDtypeStruct((1, D), f32)],
        scratch_shapes=[pltpu.VMEM((RS, D), f32)], compiler_params=_params(("arbitrary",)),
    )(x, w, dh, dx2)


def _final_fwd_bwd(x, mo, target, wf, tr):
    S, D = x.shape

    def body(x_ref, mo_ref, t_ref, w_ref, dx2_ref, dx2b_ref, loss_ref, gw_ref, acc, lacc):
        @pl.when(pl.program_id(0) == 0)
        def _():
            acc[...] = jnp.zeros_like(acc)
            lacc[...] = jnp.zeros_like(lacc)

        def strip(rows):
            x2 = x_ref[rows, :] + mo_ref[rows, :]
            r = lax.rsqrt(jnp.mean(x2 * x2, axis=-1, keepdims=True) + EPS)
            xh = x2 * r
            wv = w_ref[...]
            err = xh * wv - t_ref[rows, :]
            lacc[...] += jnp.mean(err * err, axis=-1, keepdims=True)
            dy = err * (1.0 / D)
            acc[...] += dy * xh
            dxh = dy * wv
            dx2 = r * (dxh - xh * jnp.mean(dxh * xh, axis=-1, keepdims=True))
            dx2_ref[rows, :] = dx2
            dx2b_ref[rows, :] = dx2.astype(bf16)
        _for_strips(tr, strip)

        @pl.when(pl.program_id(0) == S // tr - 1)
        def _():
            gw_ref[...] = jnp.sum(acc[...], axis=0, keepdims=True)
            loss_ref[...] = 0.5 * jnp.sum(lacc[...], axis=0, keepdims=True)

    row = pl.BlockSpec((tr, D), lambda i: (i, 0))
    vec = pl.BlockSpec((1, D), lambda i: (0, 0))
    return pl.pallas_call(
        body, name="final_norm_loss", grid=(S // tr,), in_specs=[row, row, row, vec],
        out_specs=[row, row, pl.BlockSpec((1, 1), lambda i: (0, 0)), vec],
        out_shape=[jax.ShapeDtypeStruct((S, D), f32), jax.ShapeDtypeStruct((S, D), bf16), jax.ShapeDtypeStruct((1, 1), f32),
                   jax.ShapeDtypeStruct((1, D), f32)],
        scratch_shapes=[pltpu.VMEM((RS, D), f32), pltpu.VMEM((RS, 1), f32)], compiler_params=_params(("arbitrary",)),
    )(x, mo, target, wf)


def _merge_fwd(p_r, p_s, proj, tr):
    S, D = p_r.shape

    def body(pr_ref, ps_ref, g_ref, o_ref, ot_ref):
        def strip(rows):
            gr, gs = g_ref[rows, pl.ds(0, D)], g_ref[rows, pl.ds(D, D)]
            o_ref[rows, :] = (jax.nn.sigmoid(gr) * pr_ref[rows, :] + jax.nn.sigmoid(gs) * ps_ref[rows, :]).astype(bf16)
        _for_strips(tr, strip)
        ot_ref[...] = o_ref[...].T

    row = pl.BlockSpec((tr, D), lambda i: (i, 0))
    return pl.pallas_call(
        body, name="merge_fwd", grid=(S // tr,),
        in_specs=[row, row, pl.BlockSpec((tr, 2 * D), lambda i: (i, C_GATES // (2 * D)))],
        out_specs=[row, pl.BlockSpec((D, tr), lambda i: (0, i))],
        out_shape=[jax.ShapeDtypeStruct((S, D), bf16), jax.ShapeDtypeStruct((D, S), bf16)], compiler_params=_params(("parallel",)),
    )(p_r, p_s, proj)


def _merge_bwd(dm, p_r, p_s, proj, tr):
    S, D = p_r.shape

    def body(dm_ref, pr_ref, ps_ref, g_ref, dpr_ref, dps_ref, dproj_ref):
        def strip(rows):
            dmv = dm_ref[rows, :]
            sr = jax.nn.sigmoid(g_ref[rows, pl.ds(0, D)])
            ss = jax.nn.sigmoid(g_ref[rows, pl.ds(D, D)])
            dpr_ref[rows, :] = (dmv * sr).astype(bf16)
            dps_ref[rows, :] = (dmv * ss).astype(bf16)
            dproj_ref[rows, pl.ds(0, D)] = (dmv * pr_ref[rows, :] * sr * (1.0 - sr)).astype(bf16)
            dproj_ref[rows, pl.ds(D, D)] = (dmv * ps_ref[rows, :] * ss * (1.0 - ss)).astype(bf16)
        _for_strips(tr, strip)

    row = pl.BlockSpec((tr, D), lambda i: (i, 0))
    gates = pl.BlockSpec((tr, 2 * D), lambda i: (i, C_GATES // (2 * D)))
    o = jax.ShapeDtypeStruct((S, D), bf16)
    return pl.pallas_call(
        body, name="merge_bwd", grid=(S // tr,), in_specs=[row, row, row, gates],
        out_specs=[row, row, gates], out_shape=[o, o, jax.ShapeDtypeStruct((S, N_MAIN), bf16)],
        compiler_params=_params(("parallel",)),
    )(dm, p_r, p_s, proj)


def _ssd_norm_fwd(y, proj, w, tr):
    S, W = y.shape

    def body(y_ref, z_ref, w_ref, o_ref, ot_ref):
        def strip(rows):
            u = y_ref[rows, :] * _silu(z_ref[rows, :])
            r = lax.rsqrt(jnp.mean(u * u, axis=-1, keepdims=True) + EPS)
            o_ref[rows, :] = (u * r * w_ref[...]).astype(bf16)
        _for_strips(tr, strip)
        ot_ref[...] = o_ref[...].T

    row = pl.BlockSpec((tr, W), lambda i: (i, 0))
    return pl.pallas_call(
        body, name="ssd_norm_fwd", grid=(S // tr,),
        in_specs=[row, pl.BlockSpec((tr, W), lambda i: (i, C_Z // W)), pl.BlockSpec((1, W), lambda i: (0, 0))],
        out_specs=[row, pl.BlockSpec((W, tr), lambda i: (0, i))],
        out_shape=[jax.ShapeDtypeStruct((S, W), bf16), jax.ShapeDtypeStruct((W, S), bf16)], compiler_params=_params(("parallel",)),
    )(y, proj, w)


def _ssd_norm_bwd(y, proj, w, dys, dproj, tr):
    S, W = y.shape

    def body(y_ref, z_ref, w_ref, d_ref, _, dy_ref, dz_ref, gw_ref, acc):
        @pl.when(pl.program_id(0) == 0)
        def _():
            acc[...] = jnp.zeros_like(acc)

        def strip(rows):
            yv, zv, dv = y_ref[rows, :], z_ref[rows, :], d_ref[rows, :]
            sz = _silu(zv)
            u = yv * sz
            r = lax.rsqrt(jnp.mean(u * u, axis=-1, keepdims=True) + EPS)
            un = u * r
            acc[...] += dv * un
            dun = dv * w_ref[...]
            du = r * (dun - un * jnp.mean(dun * un, axis=-1, keepdims=True))
            dy_ref[rows, :] = du * sz
            dz_ref[rows, :] = (du * yv * _dsilu(zv)).astype(bf16)
        _for_strips(tr, strip)

        @pl.when(pl.program_id(0) == S // tr - 1)
        def _():
            gw_ref[...] = jnp.sum(acc[...], axis=0, keepdims=True)

    row = pl.BlockSpec((tr, W), lambda i: (i, 0))
    zcol = pl.BlockSpec((tr, W), lambda i: (i, C_Z // W))
    vec = pl.BlockSpec((1, W), lambda i: (0, 0))
    return pl.pallas_call(
        body, name="ssd_norm_bwd", grid=(S // tr,),
        in_specs=[row, zcol, vec, row, ANY], out_specs=[row, zcol, vec],
        out_shape=[jax.ShapeDtypeStruct((S, W), f32), jax.ShapeDtypeStruct(dproj.shape, bf16), jax.ShapeDtypeStruct((1, W), f32)],
        input_output_aliases={4: 1}, scratch_shapes=[pltpu.VMEM((RS, W), f32)], compiler_params=_params(("arbitrary",)),
    )(y, proj, w, dys, dproj)


def _rope(t, cos, sin):
    t1, t2 = t[:, :128], t[:, 128:]
    return jnp.concatenate([t1 * cos - t2 * sin, t2 * cos + t1 * sin], axis=1)


def _rope_t(d, cos, sin):
    d1, d2 = d[:, :128], d[:, 128:]
    return jnp.concatenate([d1 * cos + d2 * sin, d2 * cos - d1 * sin], axis=1)


def _ret_specs(tb, rev_nb=None):
    def blk(i):
        return i if rev_nb is None else rev_nb - 1 - i
    head = pl.BlockSpec((tb, RET_HW), lambda h, i: (blk(i), h))
    tab = pl.BlockSpec((tb, 128), lambda h, i: (blk(i), 0))
    mat = pl.BlockSpec((1, CHUNK, CHUNK), lambda h, i: (h, 0, 0))
    vec = pl.BlockSpec((1, CHUNK, 1), lambda h, i: (h, 0, 0))
    one = pl.BlockSpec((1, 1, 1), lambda h, i: (h, 0, 0))
    own = pl.BlockSpec((tb, RET_DK), lambda h, i: (blk(i), h))
    st = pl.BlockSpec((1, tb // CHUNK, RET_DK, RET_DK), lambda h, i: (h, blk(i), 0, 0))
    return head, tab, mat, vec, one, own, st


def _ret_fwd(proj, cos, sin, intra, qdec, kdec, cdec, tb):
    S = proj.shape[0]
    nc = S // CHUNK
    scale = RET_DK ** -0.5
    dk = RET_DK

    def body(p_ref, cos_ref, sin_ref, m_ref, qd_ref, kd_ref, cd_ref, y_ref, yr_ref, yrt_ref, st_ref, st):
        @pl.when(pl.program_id(1) == 0)
        def _():
            st[...] = jnp.zeros_like(st)

        mm, qd, kd, cd = m_ref[0], qd_ref[0], kd_ref[0], cd_ref[0]

        def chunk(c, carry):
            rows = pl.ds(pl.multiple_of(c * CHUNK, CHUNK), CHUNK)
            cs, sn = cos_ref[rows, :], sin_ref[rows, :]
            qr = _rope(p_ref[rows, pl.ds(0, dk)], cs, sn)
            kr = _rope(p_ref[rows, pl.ds(dk, dk)], cs, sn) * scale
            qb, kb, vb = qr.astype(bf16), kr.astype(bf16), p_ref[rows, pl.ds(2 * dk, dk)].astype(bf16)
            stb = st[...].astype(bf16)
            st_ref[0, c] = stb
            sc = (_nt(qb, kb) * mm).astype(bf16)
            y = _nn(sc, vb) + _nn(qb, stb) * qd
            st[...] = st[...] * cd + _tn((kr * kd).astype(bf16), vb)
            y_ref[rows, :] = y
            mu = jnp.mean(y, axis=-1, keepdims=True)
            yc = y - mu
            var = jnp.mean(yc * yc, axis=-1, keepdims=True)
            yr_ref[rows, :] = (yc * lax.rsqrt(var + EPS) * _silu(p_ref[rows, pl.ds(3 * dk, dk)])).astype(bf16)
            return carry

        lax.fori_loop(0, tb // CHUNK, chunk, 0, unroll=min(4, tb // CHUNK))
        yrt_ref[...] = yr_ref[...].T

    head, tab, mat, vec, one, own, stspec = _ret_specs(tb)
    return pl.pallas_call(
        body, name="ret_fwd", grid=(RET_HEADS, S // tb),
        in_specs=[head, tab, tab, mat, vec, vec, one],
        out_specs=[own, own, pl.BlockSpec((RET_DK, tb), lambda h, i: (h, i)), stspec],
        out_shape=[jax.ShapeDtypeStruct((S, 2048), f32), jax.ShapeDtypeStruct((S, 2048), bf16), jax.ShapeDtypeStruct((2048, S), bf16),
                   jax.ShapeDtypeStruct((RET_HEADS, nc, dk, dk), bf16)],
        scratch_shapes=[pltpu.VMEM((dk, dk), f32)], compiler_params=_params(("parallel", "arbitrary")),
    )(proj, cos, sin, intra, qdec, kdec, cdec)


def _ret_bwd(proj, cos, sin, intra, qdec, kdec, cdec, y, dyr, states, dproj, tb):
    S = proj.shape[0]
    nb = S // tb
    nck = tb // CHUNK
    scale = RET_DK ** -0.5
    dk = RET_DK

    def body(p_ref, cos_ref, sin_ref, m_ref, qd_ref, kd_ref, cd_ref, y_ref, dyr_ref, st_ref, _, o_ref, dst):
        @pl.when(pl.program_id(1) == 0)
        def _():
            dst[...] = jnp.zeros_like(dst)

        mm, qd, kd, cd = m_ref[0], qd_ref[0], kd_ref[0], cd_ref[0]

        def chunk(cc, carry):
            c = nck - 1 - cc
            rows = pl.ds(pl.multiple_of(c * CHUNK, CHUNK), CHUNK)
            cs, sn = cos_ref[rows, :], sin_ref[rows, :]
            qr = _rope(p_ref[rows, pl.ds(0, dk)], cs, sn)
            kr = _rope(p_ref[rows, pl.ds(dk, dk)], cs, sn) * scale
            qb, kb, vb = qr.astype(bf16), kr.astype(bf16), p_ref[rows, pl.ds(2 * dk, dk)].astype(bf16)
            kdb = (kr * kd).astype(bf16)
            stb = st_ref[0, c]
            yv, gv, dyrv = y_ref[rows, :], p_ref[rows, pl.ds(3 * dk, dk)], dyr_ref[rows, :]
            mu = jnp.mean(yv, axis=-1, keepdims=True)
            yc = yv - mu
            rstd = lax.rsqrt(jnp.mean(yc * yc, axis=-1, keepdims=True) + EPS)
            yn = yc * rstd
            o_ref[rows, pl.ds(3 * dk, dk)] = (dyrv * yn * _dsilu(gv)).astype(bf16)
            dyn = dyrv * _silu(gv)
            dy = rstd * (dyn - jnp.mean(dyn, axis=-1, keepdims=True) - yn * jnp.mean(dyn * yn, axis=-1, keepdims=True))
            dyb = dy.astype(bf16)
            dyqb = (dy * qd).astype(bf16)
            dstb = dst[...].astype(bf16)
            sct = (_nt(kb, qb) * mm).astype(bf16)
            ds = (_nt(dyb, vb) * mm).astype(bf16)
            dsT = (_nt(vb, dyb) * mm).astype(bf16)
            dv = _nn(sct, dyb) + _nn(kdb, dstb)
            dqr = _nn(ds, kb) + _nt(dyqb, stb)
            dkr = _nn(dsT, qb) + _nt(vb, dstb) * kd
            dst[...] = dst[...] * cd + _tn(qb, dyqb)
            o_ref[rows, pl.ds(0, dk)] = _rope_t(dqr, cs, sn).astype(bf16)
            o_ref[rows, pl.ds(dk, dk)] = (_rope_t(dkr, cs, sn) * scale).astype(bf16)
            o_ref[rows, pl.ds(2 * dk, dk)] = dv.astype(bf16)
            return carry

        lax.fori_loop(0, nck, chunk, 0, unroll=min(4, nck))

    head, tab, mat, vec, one, own, stspec = _ret_specs(tb, rev_nb=nb)
    return pl.pallas_call(
        body, name="ret_bwd", grid=(RET_HEADS, nb),
        in_specs=[head, tab, tab, mat, vec, vec, one, own, own, stspec, ANY],
        out_specs=head, out_shape=jax.ShapeDtypeStruct(dproj.shape, bf16), input_output_aliases={10: 0},
        scratch_shapes=[pltpu.VMEM((dk, dk), f32)], compiler_params=_params(("parallel", "arbitrary")),
    )(proj, cos, sin, intra, qdec, kdec, cdec, y, dyr, states, dproj)


def _conv_fwd(proj, conv_w, conv_b, tb, cw):
    S = proj.shape[0]
    off = C_XBC // cw

    def body(x_ref, halo_ref, w_ref, b_ref, o_ref, xe):
        xe[pl.ds(0, 8), :] = jnp.where(pl.program_id(1) == 0, 0.0, halo_ref[...])
        xe[pl.ds(8, CS), :] = x_ref[pl.ds(0, CS), :]
        ws = [w_ref[pl.ds(j, 1), :] for j in range(SSD_CONV)]
        for s in range(tb // CS):
            tap = (lambda j: xe[pl.ds(5 + j, CS), :]) if s == 0 else (lambda j, s=s: x_ref[pl.ds(s * CS - 3 + j, CS), :])
            acc = b_ref[...] + ws[0] * tap(0)
            for j in range(1, SSD_CONV):
                acc = acc + ws[j] * tap(j)
            o_ref[pl.ds(s * CS, CS), :] = acc

    return pl.pallas_call(
        body, name="conv_fwd", grid=(CONV_DIM // cw, S // tb),
        in_specs=[pl.BlockSpec((tb, cw), lambda j, i: (i, off + j)),
                  pl.BlockSpec((8, cw), lambda j, i: (jnp.maximum(i * (tb // 8) - 1, 0), off + j)),
                  pl.BlockSpec((SSD_CONV, cw), lambda j, i: (0, j)), pl.BlockSpec((1, cw), lambda j, i: (0, j))],
        out_specs=pl.BlockSpec((tb, cw), lambda j, i: (i, j)),
        out_shape=jax.ShapeDtypeStruct((S, CONV_DIM), f32),
        scratch_shapes=[pltpu.VMEM((CS + 8, cw), f32)], compiler_params=_params(("parallel", "arbitrary")),
    )(proj, proj, conv_w, conv_b)


def _conv_bwd(dpre, proj, conv_w, dproj, tb, cw):
    S, n = dpre.shape
    nb = S // tb
    xoff = C_XBC // cw

    def body(d_ref, dh_ref, x_ref, xh_ref, w_ref, _, dx_ref, gw_ref, gb_ref, de, xe, accw, accb):
        i = pl.program_id(1)

        @pl.when(i == 0)
        def _():
            accw[...] = jnp.zeros_like(accw)
            accb[...] = jnp.zeros_like(accb)

        ns = tb // CS
        de[pl.ds(0, CS), :] = d_ref[pl.ds(tb - CS, CS), :]
        de[pl.ds(CS, 8), :] = jnp.where(i == nb - 1, 0.0, dh_ref[...])
        xe[pl.ds(0, 8), :] = jnp.where(i == 0, 0.0, xh_ref[...])
        xe[pl.ds(8, CS), :] = x_ref[pl.ds(0, CS), :]
        ws = [w_ref[pl.ds(j, 1), :] for j in range(SSD_CONV)]
        fold = lambda p: sum(p[8 * q:8 * (q + 1)] for q in range(1, CS // 8)) + p[0:8]
        for s in range(ns):
            dv = d_ref[pl.ds(s * CS, CS), :]
            ahead = (lambda o: de[pl.ds(o, CS), :]) if s == ns - 1 else (lambda o, s=s: d_ref[pl.ds(s * CS + o, CS), :])
            xtap = (lambda j: xe[pl.ds(5 + j, CS), :]) if s == 0 else (lambda j, s=s: x_ref[pl.ds(s * CS - 3 + j, CS), :])
            acc = ws[SSD_CONV - 1] * dv
            for j in range(SSD_CONV - 1):
                acc = acc + ws[j] * ahead(3 - j)
            dx_ref[pl.ds(s * CS, CS), :] = acc.astype(bf16)
            accb[...] += fold(dv)
            for j in range(SSD_CONV):
                accw[j] += fold(dv * xtap(j))

        @pl.when(i == nb - 1)
        def _():
            gb_ref[...] = jnp.sum(accb[...], axis=0, keepdims=True)
            for j in range(SSD_CONV):
                gw_ref[pl.ds(j, 1), :] = jnp.sum(accw[j], axis=0, keepdims=True)

    return pl.pallas_call(
        body, name="conv_bwd", grid=(n // cw, nb),
        in_specs=[pl.BlockSpec((tb, cw), lambda j, i: (i, j)),
                  pl.BlockSpec((8, cw), lambda j, i: (jnp.minimum((i + 1) * (tb // 8), S // 8 - 1), j)),
                  pl.BlockSpec((tb, cw), lambda j, i: (i, xoff + j)),
                  pl.BlockSpec((8, cw), lambda j, i: (jnp.maximum(i * (tb // 8) - 1, 0), xoff + j)),
                  pl.BlockSpec((SSD_CONV, cw), lambda j, i: (0, j)), ANY],
        out_specs=[pl.BlockSpec((tb, cw), lambda j, i: (i, xoff + j)), pl.BlockSpec((SSD_CONV, cw), lambda j, i: (0, j)),
                   pl.BlockSpec((1, cw), lambda j, i: (0, j))],
        out_shape=[jax.ShapeDtypeStruct(dproj.shape, bf16), jax.ShapeDtypeStruct((SSD_CONV, n), f32), jax.ShapeDtypeStruct((1, n), f32)],
        input_output_aliases={5: 0},
        scratch_shapes=[pltpu.VMEM((CS + 8, cw), f32), pltpu.VMEM((CS + 8, cw), f32), pltpu.VMEM((SSD_CONV, 8, cw), f32),
                        pltpu.VMEM((8, cw), f32)],
        compiler_params=_params(("parallel", "arbitrary")),
    )(dpre, dpre, proj, proj, conv_w, dproj)


def _dt_prep(dt_raw, dt_bias, a_log, tb):
    S = dt_raw.shape[0]

    def body(r_ref, b_ref, al_ref, dt_ref, sg_ref, ac_ref):
        li = lax.broadcasted_iota(jnp.int32, (LS, LS), 0)
        si = lax.broadcasted_iota(jnp.int32, (LS, LS), 1)
        tri = (li >= si).astype(f32)
        neg_a = -jnp.exp(al_ref[...])
        for c in range(tb // LS):
            rows = pl.ds(c * LS, LS)
            xv = r_ref[rows, :] + b_ref[...]
            dtv = jax.nn.softplus(xv)
            dt_ref[rows, :] = dtv
            sg_ref[rows, :] = jax.nn.sigmoid(xv)
            ac_ref[rows, :] = _hi(tri, dtv * neg_a)

    row = pl.BlockSpec((tb, 128), lambda i: (i, 0))
    vec = pl.BlockSpec((1, 128), lambda i: (0, 0))
    o = jax.ShapeDtypeStruct((S, 128), f32)
    return pl.pallas_call(body, name="dt_prep", grid=(S // tb,), in_specs=[row, vec, vec], out_specs=[row, row, row],
                          out_shape=[o, o, o], compiler_params=_params(("parallel",)))(dt_raw, dt_bias, a_log)


def _group_major(t):
    S = t.shape[0]
    return jnp.transpose(t[:, :SSD_HEADS].reshape(S, SSD_GROUPS, SSD_HPG), (1, 0, 2))


def _group_major_t(t):
    S = t.shape[0]
    return jnp.transpose(t[:, :SSD_HEADS].reshape(S // LS, LS, SSD_GROUPS, SSD_HPG), (2, 0, 3, 1))


def _ssd_specs(tb, rev_nb=None):
    def blk(i):
        return i if rev_nb is None else rev_nb - 1 - i
    grp = pl.BlockSpec((tb, SSD_GC), lambda g, i: (blk(i), g))
    xs = pl.BlockSpec((tb, SSD_GW), lambda g, i: (blk(i), g))
    ph = pl.BlockSpec((1, tb, SSD_HPG), lambda g, i: (g, blk(i), 0))
    pht = pl.BlockSpec((1, tb // LS, SSD_HPG, LS), lambda g, i: (g, blk(i), 0, 0))
    gvec = pl.BlockSpec((1, 1, SSD_GW), lambda g, i: (g, 0, 0))
    ex = pl.BlockSpec((SSD_HPG, SSD_GW), lambda g, i: (0, 0))
    st = pl.BlockSpec((1, tb // LS, SSD_STATE, SSD_GW), lambda g, i: (g, blk(i), 0, 0))
    return grp, xs, ph, pht, gvec, ex, st


def _expander():
    return jnp.repeat(jnp.eye(SSD_HPG, dtype=f32), SSD_GW // SSD_HPG, axis=1).astype(bf16)


def _expand3(dt8, ac8, ex):
    stack = jnp.concatenate([dt8, jnp.exp(ac8), jnp.exp(ac8[LS - 1:LS, :] - ac8)], axis=0)
    wide = _sel_r(stack, ex)
    return wide[0:LS], wide[LS:2 * LS], wide[2 * LS:3 * LS]


def _ssd_fwd(pre, dt_g, ac_g, act_g, dskx, tb):
    S = pre.shape[0]
    nc = S // LS
    hd = SSD_GW // SSD_HPG

    def body(p_ref, dt_ref, ac_ref, act_ref, dsk_ref, ex_ref, y_ref, st_ref, st):
        @pl.when(pl.program_id(1) == 0)
        def _():
            st[...] = jnp.zeros_like(st)

        ex = ex_ref[...]
        li = lax.broadcasted_iota(jnp.int32, (LS, LS), 0)
        si = lax.broadcasted_iota(jnp.int32, (LS, LS), 1)
        causal = li >= si

        def chunk(c, carry):
            rows = pl.ds(pl.multiple_of(c * LS, LS), LS)
            xs = _silu(p_ref[rows, pl.ds(0, SSD_GW)])
            bcb = _silu(p_ref[rows, pl.ds(SSD_GW, SSD_STATE)]).astype(bf16)
            ccb = _silu(p_ref[rows, pl.ds(SSD_GW + SSD_STATE, SSD_STATE)]).astype(bf16)
            dt8, ac8, act = dt_ref[0, rows, :], ac_ref[0, rows, :], act_ref[0, c]
            dtx, eax, tailx = _expand3(dt8, ac8, ex)
            xdt = xs * dtx
            cb = _nt(ccb, bcb)
            stb = st[...].astype(bf16)
            st_ref[0, c] = stb
            xdtb = xdt.astype(bf16)
            outs = []
            for h in range(SSD_HPG):
                dec = jnp.exp(jnp.where(causal, ac8[:, h:h + 1] - act[h:h + 1, :], -1e30))
                outs.append(_nn((cb * dec).astype(bf16), xdtb[:, hd * h:hd * (h + 1)]))
            y_ref[rows, :] = jnp.concatenate(outs, axis=1) + _nn(ccb, stb) * eax + dsk_ref[0] * xs
            st[...] = st[...] * eax[LS - 1:LS, :] + _tn(bcb, (xdt * tailx).astype(bf16))
            return carry

        lax.fori_loop(0, tb // LS, chunk, 0)

    grp, xs, ph, pht, gvec, ex, stspec = _ssd_specs(tb)
    return pl.pallas_call(
        body, name="ssd_fwd", grid=(SSD_GROUPS, S // tb),
        in_specs=[grp, ph, ph, pht, gvec, ex], out_specs=[xs, stspec],
        out_shape=[jax.ShapeDtypeStruct((S, SSD_WIDTH), f32), jax.ShapeDtypeStruct((SSD_GROUPS, nc, SSD_STATE, SSD_GW), bf16)],
        scratch_shapes=[pltpu.VMEM((SSD_STATE, SSD_GW), f32)], compiler_params=_params(("parallel", "arbitrary")),
    )(pre, dt_g, ac_g, act_g, dskx, _expander())


def _ssd_bwd(pre, dt_g, ac_g, act_g, sg_g, dskx, nega_g, dy, states, tb):
    S = pre.shape[0]
    nb = S // tb
    nck = tb // LS
    hd = SSD_GW // SSD_HPG

    def body(p_ref, dt_ref, ac_ref, act_ref, sg_ref, dsk_ref, na_ref, ex_ref, ext_ref, dy_ref, st_ref,
             dp_ref, ddt_ref, gsk_ref, gal_ref, gdb_ref, dst, skacc):
        @pl.when(pl.program_id(1) == 0)
        def _():
            dst[...] = jnp.zeros_like(dst)
            skacc[...] = jnp.zeros_like(skacc)
            gal_ref[...] = jnp.zeros_like(gal_ref)
            gdb_ref[...] = jnp.zeros_like(gdb_ref)

        ex, ext = ex_ref[...], ext_ref[...]
        li = lax.broadcasted_iota(jnp.int32, (LS, LS), 0)
        si = lax.broadcasted_iota(jnp.int32, (LS, LS), 1)
        causal = li >= si
        anti = si >= li
        upper = anti.astype(bf16)
        last_row = (lax.broadcasted_iota(jnp.int32, (LS, 1), 0) == LS - 1).astype(f32)
        head_id = lax.broadcasted_iota(jnp.int32, (1, SSD_HPG), 1)
        neg_a = na_ref[0]
        dskv = dsk_ref[0]

        def chunk(cc, carry):
            c = nck - 1 - cc
            rows = pl.ds(pl.multiple_of(c * LS, LS), LS)
            px = p_ref[rows, pl.ds(0, SSD_GW)]
            pb = p_ref[rows, pl.ds(SSD_GW, SSD_STATE)]
            pc = p_ref[rows, pl.ds(SSD_GW + SSD_STATE, SSD_STATE)]
            xs = _silu(px)
            bcb = _silu(pb).astype(bf16)
            ccb = _silu(pc).astype(bf16)
            dt8, ac8, act = dt_ref[0, rows, :], ac_ref[0, rows, :], act_ref[0, c]
            dtx, eax, tailx = _expand3(dt8, ac8, ex)
            xdt = xs * dtx
            ex_last = eax[LS - 1:LS, :]
            stb = st_ref[0, c]
            dyv = dy_ref[rows, :]
            dyb = dyv.astype(bf16)
            xdtb = xdt.astype(bf16)
            skacc[...] += jnp.sum(dyv * xs, axis=0, keepdims=True)
            yinter = _nn(ccb, stb) * eax
            dzb = (dyv * eax).astype(bf16)
            dcc = _nt(dzb, stb)
            dstv = dst[...]
            dstb = dstv.astype(bf16)
            xt = xdt * tailx
            dxt = _nn(bcb, dstb)
            dbc = _nt(xt.astype(bf16), dstb)
            dxdt = dxt * tailx
            lastrow = jnp.sum(dxt * xt, axis=0, keepdims=True) + jnp.sum(dstv * stb.astype(f32), axis=0, keepdims=True) * ex_last
            dst[...] = dstv * ex_last + _tn(ccb, dzb)
            cb = _nt(ccb, bcb)
            cbt = _nt(bcb, ccb)
            dcb = jnp.zeros((LS, LS), f32)
            dcbt = jnp.zeros((LS, LS), f32)
            dac8 = jnp.zeros((LS, SSD_HPG), f32)
            dxin = []
            for h in range(SSD_HPG):
                sl = slice(hd * h, hd * (h + 1))
                col, rowv = ac8[:, h:h + 1], act[h:h + 1, :]
                dec = jnp.exp(jnp.where(causal, col - rowv, -1e30))
                dect = jnp.exp(jnp.where(anti, rowv - col, -1e30))
                gm, gmt = cb * dec, cbt * dect
                dgm, dgmt = _nt(dyb[:, sl], xdtb[:, sl]), _nt(xdtb[:, sl], dyb[:, sl])
                dxin.append(_nn(gmt.astype(bf16), dyb[:, sl]))
                dcb = dcb + dgm * dec
                dcbt = dcbt + dgmt * dect
                dcol = jnp.sum(dgm * gm, axis=1, keepdims=True) - jnp.sum(dgmt * gmt, axis=1, keepdims=True)
                dac8 = dac8 + dcol * (head_id == h).astype(f32)
            dxintra = jnp.concatenate(dxin, axis=1)
            dcc = dcc + _nn(dcb.astype(bf16), bcb)
            dbc = dbc + _nn(dcbt.astype(bf16), ccb)
            dxdt = dxdt + dxintra
            dacx = dyv * yinter - dxt * xt + last_row * lastrow
            red = _sel_r(jnp.concatenate([dacx, dxdt * xs], axis=0), ext)
            dac8 = dac8 + red[0:LS]
            da8 = _sel_l(upper, dac8)
            ddt8 = red[LS:2 * LS] + da8 * neg_a
            gal_ref[0] += jnp.sum(da8 * dt8 * neg_a, axis=0, keepdims=True)
            ddr = ddt8 * sg_ref[0, rows, :]
            ddt_ref[0, rows, :] = ddr
            gdb_ref[0] += jnp.sum(ddr, axis=0, keepdims=True)
            dp_ref[rows, pl.ds(0, SSD_GW)] = (dskv * dyv + dxdt * dtx) * _dsilu(px)
            dp_ref[rows, pl.ds(SSD_GW, SSD_STATE)] = dbc * _dsilu(pb)
            dp_ref[rows, pl.ds(SSD_GW + SSD_STATE, SSD_STATE)] = dcc * _dsilu(pc)
            return carry

        lax.fori_loop(0, nck, chunk, 0)

        @pl.when(pl.program_id(1) == nb - 1)
        def _():
            gsk_ref[0] = skacc[...]

    grp, xs, ph, pht, gvec, ex, stspec = _ssd_specs(tb, rev_nb=nb)
    small = pl.BlockSpec((1, 1, SSD_HPG), lambda g, i: (g, 0, 0))
    ext = pl.BlockSpec((SSD_GW, SSD_HPG), lambda g, i: (0, 0))
    sm = jax.ShapeDtypeStruct((SSD_GROUPS, 1, SSD_HPG), f32)
    expander = _expander()
    return pl.pallas_call(
        body, name="ssd_bwd", grid=(SSD_GROUPS, nb),
        in_specs=[grp, ph, ph, pht, ph, gvec, small, ex, ext, xs, stspec],
        out_specs=[grp, ph, gvec, small, small],
        out_shape=[jax.ShapeDtypeStruct((S, CONV_DIM), f32), jax.ShapeDtypeStruct((SSD_GROUPS, S, SSD_HPG), f32),
                   jax.ShapeDtypeStruct((SSD_GROUPS, 1, SSD_GW), f32), sm, sm],
        scratch_shapes=[pltpu.VMEM((SSD_STATE, SSD_GW), f32), pltpu.VMEM((1, SSD_GW), f32)],
        compiler_params=_params(("parallel", "arbitrary")),
    )(pre, dt_g, ac_g, act_g, sg_g, dskx, nega_g, expander, expander.T, dy, states)


def _tiles(S):
    return dict(tb=min(512, S), tr=min(256, S), tm=min(1024, S))


def _local_step(x, positions, target, norm1_w, w_main, w_dt, conv_w, conv_b, dt_bias, a_log, d_skip, ssd_norm_w,
                w_br, w_bs, w_o, norm_f_w, on_weight_grads):
    S, D = x.shape
    t = _tiles(S)
    tb, tr, tm = t["tb"], t["tr"], t["tm"]

    half = RET_DK // 2
    inv_freq = ROPE_THETA ** (-jnp.arange(half, dtype=f32) / half)
    ang = positions.astype(f32)[:, None] * inv_freq
    cos, sin = jnp.cos(ang), jnp.sin(ang)
    log_gamma = jnp.log1p(-(2.0 ** (-5.0 - jnp.arange(RET_HEADS, dtype=f32))))
    idx = jnp.arange(CHUNK, dtype=f32)
    intra = jnp.exp(jnp.abs(idx[:, None] - idx[None, :]) * log_gamma[:, None, None])
    qdec = jnp.exp((idx + 1.0)[None, :] * log_gamma[:, None])[:, :, None]
    kdec = jnp.exp((CHUNK - 1.0 - idx)[None, :] * log_gamma[:, None])[:, :, None]
    cdec = jnp.exp(CHUNK * log_gamma)[:, None, None]
    conv_wm, conv_bm = _xbc_group_major(conv_w), _xbc_group_major(conv_b)

    h, ht = _norm1_fwd(x, norm1_w, tr)
    proj = _mm1(h, w_main, tm=tm, tn=1024, tk=D, out_dtype=f32, name="proj_main")
    dt_raw = _mm1(h, w_dt, tm=tm, tn=128, tk=D, out_dtype=f32, name="proj_dt")
    y_ret, yr, yrt, ret_states = _ret_fwd(proj, cos, sin, intra, qdec, kdec, cdec, tb)
    pre = _conv_fwd(proj, conv_wm, conv_bm, min(1024, S), 512)
    pad64 = lambda v: jnp.pad(v, ((0, 0), (0, 128 - SSD_HEADS)))
    dt, sg, ac = _dt_prep(dt_raw, pad64(dt_bias), pad64(a_log), tb)
    dt_g, ac_g, sg_g, act_g = _group_major(dt), _group_major(ac), _group_major(sg), _group_major_t(ac)
    dskx = jnp.repeat(d_skip.reshape(SSD_GROUPS, 1, SSD_HPG), SSD_GW // SSD_HPG, axis=2)
    nega_g = (-jnp.exp(a_log)).reshape(SSD_GROUPS, 1, SSD_HPG)
    y_ssd, ssd_states = _ssd_fwd(pre, dt_g, ac_g, act_g, dskx, tb)
    ys, yst = _ssd_norm_fwd(y_ssd, proj, ssd_norm_w, tr // 2)
    p_r = _mm1(yr, w_br, tm=tm, tn=1024, tk=2048, out_dtype=f32, name="branch_ret")
    p_s = _mm1(ys, w_bs, tm=tm, tn=1024, tk=2048, out_dtype=f32, name="branch_ssd")
    merged, mergedt = _merge_fwd(p_r, p_s, proj, tr)
    mo = _mm1(merged, w_o, tm=tm, tn=1024, tk=2048, out_dtype=f32, name="out_proj")
    dx2, dx2b, loss, g_norm_f = _final_fwd_bwd(x, mo, target, norm_f_w.reshape(1, D), tr)

    tkt = min(4096, S)
    wg = lambda at, b, name, tn=1024: _mm1(at, b, tm=min(1024, at.shape[0]), tn=tn, tk=tkt, out_dtype=f32, name=name)
    dm = _mm1(dx2b, w_o, tm=tm, tn=1024, tk=2048, out_dtype=f32, name="d_merged", tb=True)
    g_w_o = wg(mergedt, dx2b, "g_w_out")
    dp_r, dp_s, dproj = _merge_bwd(dm, p_r, p_s, proj, tr)
    dyr = _mm1(dp_r, w_br, tm=tm, tn=1024, tk=2048, out_dtype=f32, name="d_yr", tb=True)
    dys = _mm1(dp_s, w_bs, tm=tm, tn=1024, tk=2048, out_dtype=f32, name="d_ys", tb=True)
    g_w_br = wg(yrt, dp_r, "g_w_br_ret")
    g_w_bs = wg(yst, dp_s, "g_w_br_ssd")
    dy_ssd, dproj, g_ssd_norm = _ssd_norm_bwd(y_ssd, proj, ssd_norm_w, dys, dproj, tr // 2)
    dproj = _ret_bwd(proj, cos, sin, intra, qdec, kdec, cdec, y_ret, dyr, ret_states, dproj, tb)
    dpre, ddt_g, gsk, gal, gdb = _ssd_bwd(pre, dt_g, ac_g, act_g, sg_g, dskx, nega_g, dy_ssd, ssd_states, tb)
    dproj, gcw, gcb = _conv_bwd(dpre, proj, conv_wm, dproj, min(1024, S), 512)
    ddt = jnp.transpose(ddt_g, (1, 0, 2)).reshape(S, SSD_HEADS)
    ddt_p = jnp.pad(ddt, ((0, 0), (0, 128 - SSD_HEADS))).astype(bf16)

    g_main = wg(ht, dproj, "g_w_in_main")
    g_dt = wg(ht, ddt_p, "g_w_in_dt", tn=128)
    reduce_state = on_weight_grads(g_main, g_dt, g_w_br, g_w_bs, g_w_o)
    ddt_p = ddt_p + reduce_state[-1][0, 0].astype(bf16)
    dh = _mm([(dproj, 0, w_main, 0, 0, N_MAIN, N_MAIN // 8), (ddt_p, 0, w_dt, 0, 0, 128, 128)], S, D, tm=tm, tn=1024,
             out_dtype=f32, name="d_h", tb=True)
    grad_x, g_norm1 = _norm1_bwd(x, norm1_w, dh, dx2, tr)

    seg = lambda v: jnp.sum(v.reshape(SSD_HEADS, SSD_GW // SSD_HPG), axis=1).reshape(1, SSD_HEADS)
    grads = dict(
        norm1_w=g_norm1, w_in_main=g_main, w_in_dt=g_dt,
        conv_w=_xbc_original(gcw), conv_b=_xbc_original(gcb),
        dt_bias=gdb.reshape(1, SSD_HEADS), a_log=gal.reshape(1, SSD_HEADS), d_skip=seg(gsk),
        ssd_norm_w=g_ssd_norm, w_br_ret=g_w_br, w_br_ssd=g_w_bs, w_out=g_w_o, norm_f_w=g_norm_f,
    )
    return loss, grad_x, grads, reduce_state


def _me():
    return lax.axis_index("x"), lax.axis_index("y"), lax.axis_index("c")


def _other_chips(x, y):
    return [(1 - x, y), (x, 1 - y), (1 - x, 1 - y)]


def _gather_weights(a, b, cw):
    R = a.shape[0]
    hr = R // 2

    def body(a_ref, b_ref, cw_ref, ga_ref, gb_ref, gc_ref, send_sems, recv_sems):
        x, y, c = _me()
        k = 2 * x + y
        sibling = (x, y, 1 - c)
        chips = _other_chips(x, y)

        def small(j, src_shard, to):
            return pltpu.make_async_remote_copy(
                src_ref=cw_ref, dst_ref=gc_ref.at[src_shard], send_sem=send_sems.at[12 + j], recv_sem=recv_sems.at[12 + j],
                device_id=to, device_id_type=MESH)

        def copies(j, src_shard, half, to, from_input):
            rows = pl.ds(half * hr, hr)
            out = []
            for t, (inp, g) in enumerate(((a_ref, ga_ref), (b_ref, gb_ref))):
                src = inp.at[rows, :] if from_input else g.at[src_shard, rows, :]
                out.append(pltpu.make_async_remote_copy(
                    src_ref=src, dst_ref=g.at[src_shard, rows, :], send_sem=send_sems.at[2 * j + t],
                    recv_sem=recv_sems.at[2 * j + t], device_id=to, device_id_type=MESH))
            return out

        first = []
        for j, chip in enumerate(chips):
            first += copies(j, k, c, (*chip, c), True)
            first.append(small(j, k, (*chip, c)))
        for cp in first:
            cp.start()
        passed = []
        for j, chip in enumerate(chips):
            kk = 2 * chip[0] + chip[1]
            for cp in copies(j, kk, c, (x, y, c), False):
                cp.wait_recv()
            fw = copies(3 + j, kk, c, sibling, False)
            for cp in fw:
                cp.start()
            passed += fw
        for j, chip in enumerate(chips):
            kk = 2 * chip[0] + chip[1]
            for cp in copies(3 + j, kk, 1 - c, (x, y, c), False):
                cp.wait_recv()
            small(j, kk, (x, y, c)).wait_recv()
        for cp in first + passed:
            cp.wait_send()

    return pl.pallas_call(
        body, name="gather_weights", in_specs=[ANY, ANY, ANY], out_specs=[ANY, ANY, ANY],
        out_shape=[jax.ShapeDtypeStruct((N_SHARD,) + a.shape, a.dtype), jax.ShapeDtypeStruct((N_SHARD,) + b.shape, b.dtype),
                   jax.ShapeDtypeStruct((N_SHARD,) + cw.shape, cw.dtype)],
        scratch_shapes=[pltpu.SemaphoreType.DMA((15,)), pltpu.SemaphoreType.DMA((15,))],
        compiler_params=pltpu.CompilerParams(has_side_effects=True),
    )(a, b, cw)


def _sibling_swap(arrs, name):
    n = len(arrs)
    slots = [(t, s) for t in range(n) for s in range(arrs[t].shape[0])]

    def body(*refs):
        ins, outs = refs[:n], refs[n:2 * n]
        send_sems, recv_sems = refs[2 * n], refs[2 * n + 1]
        x, y, c = _me()
        cps = [pltpu.make_async_remote_copy(src_ref=ins[t].at[s, 1 - c], dst_ref=outs[t].at[s], send_sem=send_sems.at[q],
                                            recv_sem=recv_sems.at[q], device_id=(x, y, 1 - c), device_id_type=MESH)
               for q, (t, s) in enumerate(slots)]
        for cp in cps:
            cp.start()
        for cp in cps:
            cp.wait()

    return pl.pallas_call(
        body, name=name, in_specs=[ANY] * n, out_specs=[ANY] * n,
        out_shape=[jax.ShapeDtypeStruct(a.shape[:1] + a.shape[2:], a.dtype) for a in arrs],
        scratch_shapes=[pltpu.SemaphoreType.DMA((len(slots),)), pltpu.SemaphoreType.DMA((len(slots),))],
        compiler_params=pltpu.CompilerParams(has_side_effects=True),
    )(*arrs)


HBM = pl.BlockSpec(memory_space=pltpu.HBM)
SEM = pl.BlockSpec(memory_space=pltpu.SEMAPHORE)
DATAFLOW = pltpu.SideEffectType.DATAFLOW_SIDE_EFFECTING


def _exchange_copies(ins, lands, send_sems, recv_sems):
    n = len(ins)
    x, y, c = _me()
    cps = []
    for j, chip in enumerate(_other_chips(x, y)):
        kk = 2 * chip[0] + chip[1]
        for t in range(n):
            cps.append(pltpu.make_async_remote_copy(
                src_ref=ins[t].at[kk], dst_ref=lands[t].at[j], send_sem=send_sems.at[n * j + t],
                recv_sem=recv_sems.at[n * j + t], device_id=(*chip, c), device_id_type=MESH))
    return cps


def _chip_exchange_start(arrs):
    n = len(arrs)
    lands = [lax.empty((3,) + a.shape[1:], a.dtype) for a in arrs]

    def body(*refs):
        ins, lands_in = refs[:n], refs[n:2 * n]
        send_sems, recv_sems = refs[2 * n], refs[2 * n + 1]
        token = refs[4 * n + 2]
        for cp in _exchange_copies(ins, lands_in, send_sems, recv_sems):
            cp.start()
        token[...] = jnp.zeros_like(token)

    hbm = lambda a: pltpu.HBM(a.shape, a.dtype)
    out = pl.pallas_call(
        body, name="chip_exchange_start", in_specs=[HBM] * (2 * n),
        out_specs=(SEM, SEM, *[HBM] * (2 * n), pl.BlockSpec(memory_space=pltpu.VMEM)),
        out_shape=(pltpu.SemaphoreType.DMA((3 * n,)), pltpu.SemaphoreType.DMA((3 * n,)), *[hbm(a) for a in arrs],
                   *[hbm(a) for a in lands], jax.ShapeDtypeStruct((8, 128), f32)),
        input_output_aliases={t: 2 + t for t in range(2 * n)},
        compiler_params=pltpu.CompilerParams(has_side_effects=DATAFLOW),
    )(*[pltpu.with_memory_space_constraint(a, pltpu.HBM) for a in list(arrs) + lands])
    return out[0], out[1], list(out[2:2 + n]), list(out[2 + n:2 + 2 * n]), out[2 + 2 * n]


def _chip_exchange_wait(send_sems, recv_sems, srcs, lands, after):
    n = len(srcs)

    def body(*refs):
        ins, lands_in = refs[:n], refs[n:2 * n]
        send_sems_ref, recv_sems_ref = refs[2 * n], refs[2 * n + 1]
        for cp in _exchange_copies(ins, lands_in, send_sems_ref, recv_sems_ref):
            cp.wait_send()
            cp.wait_recv()

    hbm = lambda a: pltpu.HBM(a.shape, a.dtype)
    out = pl.pallas_call(
        body, name="chip_exchange_wait", in_specs=[HBM] * (2 * n) + [SEM, SEM, ANY],
        out_specs=[HBM] * (2 * n), out_shape=[hbm(a) for a in list(srcs) + list(lands)],
        input_output_aliases={t: t for t in range(2 * n)},
        compiler_params=pltpu.CompilerParams(has_side_effects=DATAFLOW),
    )(*srcs, *lands, send_sems, recv_sems, after)
    return list(out[:n]), list(out[n:])


def _share_halves(bufs, by_cols, name):
    n = len(bufs)

    def body(*refs):
        ins, outs = refs[:n], refs[n:2 * n]
        send_sems, recv_sems = refs[2 * n], refs[2 * n + 1]
        x, y, c = _me()

        def part(ref, t, half):
            if by_cols[t]:
                w = bufs[t].shape[1] // 2
                return ref.at[:, pl.ds(pl.multiple_of(half * w, 128), w)]
            return ref.at[half]

        sends = [pltpu.make_async_remote_copy(src_ref=part(ins[t], t, c), dst_ref=part(outs[t], t, c), send_sem=send_sems.at[t],
                                              recv_sem=recv_sems.at[t], device_id=(x, y, 1 - c), device_id_type=MESH) for t in range(n)]
        for cp in sends:
            cp.start()
        for t in range(n):
            pltpu.make_async_remote_copy(src_ref=part(ins[t], t, c), dst_ref=part(outs[t], t, 1 - c), send_sem=send_sems.at[t],
                                         recv_sem=recv_sems.at[t], device_id=(x, y, c), device_id_type=MESH).wait_recv()
        for cp in sends:
            cp.wait_send()

    return pl.pallas_call(
        body, name=name, in_specs=[ANY] * n, out_specs=[ANY] * n,
        out_shape=[jax.ShapeDtypeStruct(a.shape, a.dtype) for a in bufs], input_output_aliases={t: t for t in range(n)},
        scratch_shapes=[pltpu.SemaphoreType.DMA((n,)), pltpu.SemaphoreType.DMA((n,))],
        compiler_params=pltpu.CompilerParams(has_side_effects=True),
    )(*bufs)


def _gather_vec(v):
    n = v.shape[1]

    def body(v_ref, o_ref, send_sems, recv_sems):
        x, y, c = _me()
        me = 4 * x + 2 * y + c
        cps = []
        for j in range(1, 8):
            fx, fy, fc = (j >> 2) & 1, (j >> 1) & 1, j & 1
            peer = (x ^ fx, y ^ fy, c ^ fc)
            cps.append(pltpu.make_async_remote_copy(
                src_ref=v_ref, dst_ref=o_ref.at[pl.ds(me, 1), :], send_sem=send_sems.at[j - 1], recv_sem=recv_sems.at[j - 1],
                device_id=peer, device_id_type=MESH))
        for cp in cps:
            cp.start()
        for j in range(1, 8):
            fx, fy, fc = (j >> 2) & 1, (j >> 1) & 1, j & 1
            src = 4 * (x ^ fx) + 2 * (y ^ fy) + (c ^ fc)
            pltpu.make_async_remote_copy(
                src_ref=v_ref, dst_ref=o_ref.at[pl.ds(src, 1), :], send_sem=send_sems.at[j - 1], recv_sem=recv_sems.at[j - 1],
                device_id=(x, y, c), device_id_type=MESH).wait_recv()
        for cp in cps:
            cp.wait_send()

    return pl.pallas_call(
        body, name="gather_vec", in_specs=[ANY], out_specs=ANY, out_shape=jax.ShapeDtypeStruct((8, n), v.dtype),
        scratch_shapes=[pltpu.SemaphoreType.DMA((7,)), pltpu.SemaphoreType.DMA((7,))],
        compiler_params=pltpu.CompilerParams(has_side_effects=True),
    )(v)


def _pair_sum(g, r, name, tr):
    L, _, hr, C = g.shape

    def body(c_ref, g_ref, r_ref, o_ref):
        def strip(rows):
            o_ref[0, rows, :] = (g_ref[0, 0, rows, :] + r_ref[0, rows, :]).astype(bf16)
        _for_strips(tr, strip)

    grid_spec = pltpu.PrefetchScalarGridSpec(
        num_scalar_prefetch=1, grid=(L, hr // tr),
        in_specs=[pl.BlockSpec((1, 1, tr, C), lambda s, i, c_ref: (s, c_ref[0], i, 0)),
                  pl.BlockSpec((1, tr, C), lambda s, i, c_ref: (s, i, 0))],
        out_specs=pl.BlockSpec((1, tr, C), lambda s, i, c_ref: (s, i, 0)))
    c = lax.axis_index("c").reshape(1).astype(jnp.int32)
    return pl.pallas_call(body, name=name, grid_spec=grid_spec, out_shape=jax.ShapeDtypeStruct((L, hr, C), bf16),
                          compiler_params=_params(("parallel", "parallel")))(c, g, r)


def _own_sum(p, got, name, transposed=False):
    _, hr, C = p.shape
    tr = SUM_ROWS
    c_full, c_pad = C // 128 * 128, -(-C // 128) * 128

    def total(p_ref, got_ref, rows):
        return ((p_ref[0, rows, :].astype(f32) + got_ref[0, rows, :].astype(f32)) + got_ref[1, rows, :].astype(f32)) \
            + got_ref[2, rows, :].astype(f32)

    def body(idx_ref, p_ref, got_ref, o_ref):
        def strip(rows):
            o_ref[0, rows, :] = total(p_ref, got_ref, rows)
        _for_strips(tr, strip)

    def body_t(idx_ref, p_ref, got_ref, o_ref, buf):
        if c_pad > c_full:
            buf[:, pl.ds(c_full, c_pad - c_full)] = jnp.zeros((tr, c_pad - c_full), f32)

        def strip(rows):
            buf[rows, pl.ds(0, C)] = total(p_ref, got_ref, rows)
        _for_strips(tr, strip)
        o_ref[...] = buf[...].T[:C]

    in_specs = [pl.BlockSpec((1, tr, C), lambda i, idx: (idx[0], i, 0)), pl.BlockSpec((3, tr, C), lambda i, idx: (0, i, 0))]
    x, y, c = _me()
    idx = jnp.stack([2 * x + y, c]).astype(jnp.int32)
    if transposed:
        grid_spec = pltpu.PrefetchScalarGridSpec(num_scalar_prefetch=1, grid=(hr // tr,), in_specs=in_specs,
                                                 out_specs=pl.BlockSpec((C, tr), lambda i, idx: (0, idx[1] * (hr // tr) + i)),
                                                 scratch_shapes=[pltpu.VMEM((tr, c_pad), f32)])
        return pl.pallas_call(body_t, name=name, grid_spec=grid_spec, out_shape=jax.ShapeDtypeStruct((C, 2 * hr), f32),
                              compiler_params=_params(("parallel",)))(idx, p, got)
    grid_spec = pltpu.PrefetchScalarGridSpec(num_scalar_prefetch=1, grid=(hr // tr,), in_specs=in_specs,
                                             out_specs=pl.BlockSpec((1, tr, C), lambda i, idx: (idx[1], i, 0)))
    return pl.pallas_call(body, name=name, grid_spec=grid_spec, out_shape=jax.ShapeDtypeStruct((2, hr, C), f32),
                          compiler_params=_params(("parallel",)))(idx, p, got)


def _adamw(w, g, m, v, name, tr=None, tc=None):
    _, R, C = w.shape

    def body(w_ref, g_ref, m_ref, v_ref, d_ref, nm_ref, nv_ref):
        def strip(rows):
            gv = g_ref[0, rows, :]
            mn = ADAM_B1 * m_ref[0, rows, :] + (1.0 - ADAM_B1) * gv
            vn = ADAM_B2 * v_ref[0, rows, :] + (1.0 - ADAM_B2) * (gv * gv)
            m_hat = mn / (1.0 - ADAM_B1 ** ADAM_STEP)
            v_hat = vn / (1.0 - ADAM_B2 ** ADAM_STEP)
            d_ref[0, rows, :] = -ADAM_LR * (m_hat / (jnp.sqrt(v_hat) + ADAM_EPS) + ADAM_WD * w_ref[0, rows, :])
            nm_ref[0, rows, :] = mn
            nv_ref[0, rows, :] = vn
        n_rows = R if tc else tr
        _for_strips(n_rows, strip, rs=min(8, n_rows), unroll=2)

    if tc:
        blk, grid = pl.BlockSpec((1, R, tc), lambda i: (0, 0, i)), (C // tc,)
    else:
        blk, grid = pl.BlockSpec((1, tr, C), lambda i: (0, i, 0)), (R // tr,)
    o = jax.ShapeDtypeStruct((1, R, C), f32)
    return pl.pallas_call(body, name=name, grid=grid, in_specs=[blk] * 4, out_specs=[blk] * 3, out_shape=[o, o, o],
                          compiler_params=_params(("parallel",)))(w, g, m, v)


def _sum8(t):
    n = t.shape[1]

    def body(t_ref, o_ref):
        acc = t_ref[pl.ds(0, 1), :]
        for r in range(1, 8):
            acc = acc + t_ref[pl.ds(r, 1), :]
        o_ref[...] = acc

    return pl.pallas_call(body, name="sum_devices", out_shape=jax.ShapeDtypeStruct((1, n), f32))(t)


def _reduce_start(g_main, g_dt, g_b):
    hr = g_main.shape[0] // 2
    halves = lambda t: t.reshape((-1, 2, hr, t.shape[-1]))
    g_main, g_dt, g_b = halves(g_main), halves(g_dt), halves(g_b)
    r_main, r_dt, r_b = _sibling_swap([g_main, g_dt, g_b], "sibling_swap")
    p_main = _pair_sum(g_main, r_main, "pair_sum_main", SUM_ROWS // 4)
    p_dt = _pair_sum(g_dt, r_dt, "pair_sum_dt", SUM_ROWS)
    p_b = _pair_sum(g_b, r_b, "pair_sum_b", SUM_ROWS)
    p_in = jnp.transpose(_w_in_grad_full(p_main[0], p_dt[0]).reshape(hr, N_SHARD, W_IN_SHARD), (1, 0, 2))
    return _chip_exchange_start([p_in, p_b])


def _reduce_finish(state, after):
    send_sems, recv_sems, srcs, lands, _ = state
    (p_in, p_b), (got_in, got_b) = _chip_exchange_wait(send_sems, recv_sems, srcs, lands, after)
    mine_in, mine_b = _own_sum(p_in, got_in, "own_sum_in", transposed=True), _own_sum(p_b, got_b, "own_sum_b")
    full_in_t, full_b = _share_halves([mine_in, mine_b], [True, False], "share_halves")
    return full_in_t, full_b.reshape(-1, full_b.shape[-1])


def kernel(x, positions, norm1_w, w_in, conv_w, conv_b, dt_bias, a_log, d_skip, ssd_norm_w, w_br_ret, w_br_ssd, w_out, norm_f_w, loss_target, m_norm1_w, m_w_in, m_conv_w, m_conv_b, m_dt_bias, m_a_log, m_d_skip, m_ssd_norm_w, m_w_br_ret, m_w_br_ssd, m_w_out, m_norm_f_w, v_norm1_w, v_w_in, v_conv_w, v_conv_b, v_dt_bias, v_a_log, v_d_skip, v_ssd_norm_w, v_w_br_ret, v_w_br_ssd, v_w_out, v_norm_f_w):
    D = D_MODEL
    xi, yi, ci = _me()
    k = 2 * xi + yi
    me = 2 * k + ci
    weights = dict(norm1_w=norm1_w, w_in=w_in, conv_w=conv_w, conv_b=conv_b, dt_bias=dt_bias, a_log=a_log, d_skip=d_skip,
                   ssd_norm_w=ssd_norm_w, w_br_ret=w_br_ret, w_br_ssd=w_br_ssd, w_out=w_out, norm_f_w=norm_f_w)
    mom1 = dict(norm1_w=m_norm1_w, w_in=m_w_in, conv_w=m_conv_w, conv_b=m_conv_b, dt_bias=m_dt_bias, a_log=m_a_log, d_skip=m_d_skip,
                ssd_norm_w=m_ssd_norm_w, w_br_ret=m_w_br_ret, w_br_ssd=m_w_br_ssd, w_out=m_w_out, norm_f_w=m_norm_f_w)
    mom2 = dict(norm1_w=v_norm1_w, w_in=v_w_in, conv_w=v_conv_w, conv_b=v_conv_b, dt_bias=v_dt_bias, a_log=v_a_log, d_skip=v_d_skip,
                ssd_norm_w=v_ssd_norm_w, w_br_ret=v_w_br_ret, w_br_ssd=v_w_br_ssd, w_out=v_w_out, norm_f_w=v_norm_f_w)

    a_sh = w_in[0].astype(bf16)
    b_sh = jnp.concatenate([w_br_ret[0], w_br_ssd[0], w_out[0]], axis=0).astype(bf16)
    ga, gb, gc = _gather_weights(a_sh, b_sh, conv_w[0])
    own = lambda g, s: lax.dynamic_update_slice_in_dim(g, s[None], k, axis=0)
    ga, gb, gc = own(ga, a_sh), own(gb, b_sh), own(gc, conv_w[0])
    w_main, w_dt = _split_w_in(jnp.transpose(ga, (1, 0, 2)).reshape(D, IN_PROJ))
    w_br = gb[:, 0:512].reshape(2048, D)
    w_bs = gb[:, 512:1536].reshape(4096, D)
    w_o = gb[:, 1536:2048].reshape(2048, D)
    conv_full = jnp.transpose(gc, (1, 0, 2)).reshape(SSD_CONV, CONV_DIM)

    def start_reduce(g_main, g_dt, g_w_br, g_w_bs, g_w_o):
        g_b = jnp.concatenate([g_w_br.reshape(N_SHARD, 512, D), g_w_bs.reshape(N_SHARD, 1024, D), g_w_o.reshape(N_SHARD, 512, D)],
                              axis=1)
        return _reduce_start(g_main, g_dt, g_b)

    loss, grad_x, g, reduce_state = _local_step(x[0], positions[0], loss_target[0], norm1_w, w_main, w_dt, conv_full, conv_b, dt_bias,
                                                a_log, d_skip, ssd_norm_w, w_br, w_bs, w_o, norm_f_w, start_reduce)

    grad_w_in_t, full_b = _reduce_finish(reduce_state, g["norm1_w"])
    grad_mats = dict(w_br_ret=full_b[0:512], w_br_ssd=full_b[512:1536], w_out=full_b[1536:2048])

    small = [(n, weights[n].size) for n in ("norm1_w", "conv_b", "dt_bias", "a_log", "d_skip", "ssd_norm_w", "norm_f_w")]
    parts = [jnp.pad(loss.reshape(1, 1), ((0, 0), (0, 127)))] + [g[n].reshape(1, -1) for n, _ in small] + [g["conv_w"].reshape(1, -1)]
    vec = jnp.concatenate(parts, axis=1)
    nv = vec.shape[1]
    nvp = -(-nv // 128) * 128
    vec = jnp.pad(vec, ((0, 0), (0, nvp - nv)))
    total = _sum8(lax.dynamic_update_slice_in_dim(_gather_vec(vec), vec, me, axis=0))
    loss_out = total[0, 0]
    off = 128
    grad_small = {}
    for n, sz in small:
        grad_small[n] = total[:, off:off + sz]
        off += sz
    g_conv = total[:, off:off + SSD_CONV * CONV_DIM].reshape(SSD_CONV, CONV_DIM)
    g_conv = lax.dynamic_slice_in_dim(g_conv, k * (CONV_DIM // N_SHARD), CONV_DIM // N_SHARD, axis=1)
    grad_small["conv_w"] = g_conv.reshape(1, -1)

    upd = {}
    for n in ("w_br_ret", "w_br_ssd", "w_out"):
        upd[n] = _adamw(weights[n], grad_mats[n][None], mom1[n], mom2[n], "adamw_" + n, tr=SUM_ROWS)
    tp = lambda t: jnp.swapaxes(t, 1, 2)
    upd["w_in"] = tuple(tp(t) for t in _adamw(tp(w_in), grad_w_in_t[None], tp(m_w_in), tp(v_w_in), "adamw_w_in", tc=128))
    grad_mats["w_in"] = tp(grad_w_in_t[None])
    names_small = [n for n, _ in small] + ["conv_w"]
    flat = lambda d: jnp.concatenate([d[n].reshape(1, -1) for n in names_small], axis=1)
    ns = sum(weights[n].size for n in names_small)
    nsp = -(-ns // 128) * 128
    padv = lambda t: jnp.pad(t, ((0, 0), (0, nsp - ns)))
    small_upd = _adamw(padv(flat(weights))[None], padv(flat(grad_small))[None], padv(flat(mom1))[None],
                       jnp.pad(flat(mom2), ((0, 0), (0, nsp - ns)), constant_values=1.0)[None], "adamw_small", 1)
    off = 0
    for n in names_small:
        sz = weights[n].size
        upd[n] = tuple(t[0, :, off:off + sz] for t in small_upd)
        off += sz

    order = ["norm1_w", "w_in", "conv_w", "conv_b", "dt_bias", "a_log", "d_skip", "ssd_norm_w", "w_br_ret", "w_br_ssd", "w_out", "norm_f_w"]
    grads_out = {**grad_mats, **grad_small}
    shp = lambda n, t: t.reshape(weights[n].shape)
    return (loss_out, grad_x[None], *[shp(n, grads_out[n]) for n in order], *[shp(n, upd[n][0]) for n in order],
            *[shp(n, upd[n][1]) for n in order], *[shp(n, upd[n][2]) for n in order])
```

```python
import jax
import jax.numpy as jnp
import numpy as np
from jax import lax
from jax.experimental import pallas as pl
from jax.experimental.pallas import tpu as pltpu

f32 = jnp.float32
bf16 = jnp.bfloat16
HIGHEST = lax.Precision.HIGHEST
MESH = pl.DeviceIdType.MESH

D_MODEL = 2048
EPS = 1e-6
CHUNK = 64
RET_HEADS = 8
RET_DK = 256
RET_HW = 4 * RET_DK
ROPE_THETA = 10000.0
SSD_WIDTH = 4096
SSD_GROUPS = 8
SSD_STATE = 128
SSD_GW = 512
SSD_GC = SSD_GW + 2 * SSD_STATE
SSD_HPG = 8
SSD_CONV = 4
CONV_DIM = 6144
SSD_HEADS = 64
LS = 128

C_RET, C_Z, C_GATES, C_XBC = 0, 8192, 12288, 16384
N_MAIN = 22528
DT_OFF = 18432
IN_PROJ = 22592
N_SHARD = 4
W_IN_SHARD = IN_PROJ // N_SHARD

ADAM_LR, ADAM_B1, ADAM_B2, ADAM_EPS, ADAM_WD, ADAM_STEP = 0.001, 0.9, 0.999, 1e-08, 0.01, 10

VMEM_LIMIT = 56 * 1024 * 1024
SUM_ROWS = 128
ANY = pl.BlockSpec(memory_space=pl.ANY)


def _params(dims):
    return pltpu.CompilerParams(dimension_semantics=dims, vmem_limit_bytes=VMEM_LIMIT)


def _silu(x):
    return x * jax.nn.sigmoid(x)


def _dsilu(x):
    s = jax.nn.sigmoid(x)
    return s * (1.0 + x * (1.0 - s))


def _nt(a, b):
    return lax.dot_general(a, b, (((1,), (1,)), ((), ())), preferred_element_type=f32)


def _tn(a, b):
    return lax.dot_general(a, b, (((0,), (0,)), ((), ())), preferred_element_type=f32)


def _nn(a, b):
    return jnp.dot(a, b, preferred_element_type=f32)


def _hi(a, b):
    return jnp.dot(a, b, precision=HIGHEST, preferred_element_type=f32)


def _split(a):
    hi = a.astype(bf16)
    return hi, (a - hi.astype(f32)).astype(bf16)


def _sel_r(a, sel):
    hi, lo = _split(a)
    return _nn(hi, sel) + _nn(lo, sel)


def _sel_l(sel, a):
    hi, lo = _split(a)
    return _nn(sel, hi) + _nn(sel, lo)


def _rows_to_cols(t, eye):
    hi = t.astype(bf16)
    r1 = t - hi.astype(f32)
    mid = r1.astype(bf16)
    lo = (r1 - mid.astype(f32)).astype(bf16)
    return _nt(eye, hi) + _nt(eye, mid) + _nt(eye, lo)


def _xbc_group_major(t):
    R = t.shape[0]
    nb = SSD_GROUPS * SSD_STATE
    parts = [t[:, :SSD_WIDTH].reshape(R, SSD_GROUPS, SSD_GW), t[:, SSD_WIDTH:SSD_WIDTH + nb].reshape(R, SSD_GROUPS, SSD_STATE),
             t[:, SSD_WIDTH + nb:].reshape(R, SSD_GROUPS, SSD_STATE)]
    return jnp.concatenate(parts, axis=2).reshape(R, CONV_DIM)


def _xbc_original(t):
    R = t.shape[0]
    g = t.reshape(R, SSD_GROUPS, SSD_GC)
    parts = [g[:, :, :SSD_GW].reshape(R, SSD_WIDTH), g[:, :, SSD_GW:SSD_GW + SSD_STATE].reshape(R, SSD_GROUPS * SSD_STATE),
             g[:, :, SSD_GW + SSD_STATE:].reshape(R, SSD_GROUPS * SSD_STATE)]
    return jnp.concatenate(parts, axis=1)


def _split_w_in(w):
    D = w.shape[0]
    ret = jnp.transpose(w[:, :4 * 2048].reshape(D, 4, RET_HEADS, RET_DK), (0, 2, 1, 3)).reshape(D, 4 * 2048)
    w_dt = jnp.pad(w[:, DT_OFF:DT_OFF + SSD_HEADS], ((0, 0), (0, 128 - SSD_HEADS)))
    main = jnp.concatenate([ret, w[:, 8192:12288], w[:, DT_OFF + SSD_HEADS:], _xbc_group_major(w[:, 12288:DT_OFF])], axis=1)
    return main, w_dt


def _w_in_grad_full(g_main, g_dt):
    D = g_main.shape[0]
    ret = jnp.transpose(g_main[:, :C_Z].reshape(D, RET_HEADS, 4, RET_DK), (0, 2, 1, 3)).reshape(D, C_Z)
    return jnp.concatenate([ret, g_main[:, C_Z:C_GATES], _xbc_original(g_main[:, C_XBC:]), g_dt[:, :SSD_HEADS],
                            g_main[:, C_GATES:C_XBC]], axis=1)


def _mm(pairs, M, N, *, tm, tn, out_dtype, name, tb=False):
    P = len(pairs)
    nks = [K // tk for (_, _, _, _, _, K, tk) in pairs]
    starts = [int(s) for s in np.cumsum([0] + nks[:-1])]
    KT = int(sum(nks))
    in_specs, args = [], []
    for (a, a_cb, b, b_kb, b_nb, K, tk), s, nk in zip(pairs, starts, nks):
        def kk(k, s=s, nk=nk):
            return jnp.clip(k - s, 0, nk - 1)
        in_specs.append(pl.BlockSpec((tm, tk), lambda m, n, k, kk=kk, a_cb=a_cb: (m, a_cb + kk(k))))
        if tb:
            in_specs.append(pl.BlockSpec((tn, tk), lambda m, n, k, kk=kk, b_kb=b_kb, b_nb=b_nb: (b_nb + n, b_kb + kk(k))))
        else:
            in_specs.append(pl.BlockSpec((tk, tn), lambda m, n, k, kk=kk, b_kb=b_kb, b_nb=b_nb: (b_kb + kk(k), b_nb + n)))
        args += [a, b]

    def body(*refs):
        o_ref = refs[2 * P]
        k = pl.program_id(2)

        def prod(i):
            a = refs[2 * i][...].astype(bf16)
            b = refs[2 * i + 1][...].astype(bf16)
            return _nt(a, b) if tb else _nn(a, b)

        if KT == 1:
            o_ref[...] = prod(0).astype(out_dtype)
            return
        acc = refs[2 * P + 1]

        @pl.when(k == 0)
        def _():
            acc[...] = jnp.zeros_like(acc)

        for i in range(P):
            @pl.when((k >= starts[i]) & (k < starts[i] + nks[i]))
            def _(i=i):
                acc[...] += prod(i)

        @pl.when(k == KT - 1)
        def _():
            o_ref[...] = acc[...].astype(out_dtype)

    return pl.pallas_call(
        body, name=name, grid=(M // tm, N // tn, KT), in_specs=in_specs,
        out_specs=pl.BlockSpec((tm, tn), lambda m, n, k: (m, n)),
        out_shape=jax.ShapeDtypeStruct((M, N), out_dtype),
        scratch_shapes=[] if KT == 1 else [pltpu.VMEM((tm, tn), f32)],
        compiler_params=_params(("parallel", "parallel", "arbitrary")),
    )(*args)


def _mm1(a, b, *, tm, tn, tk, out_dtype, name, tb=False):
    M, K = a.shape
    N = b.shape[0] if tb else b.shape[1]
    return _mm([(a, 0, b, 0, 0, K, tk)], M, N, tm=tm, tn=tn, out_dtype=out_dtype, name=name, tb=tb)


RS = 16
CS = 32


def _for_strips(n_rows, fn, rs=RS, unroll=4):
    def step(s, carry):
        fn(pl.ds(pl.multiple_of(s * rs, rs), rs))
        return carry
    n = n_rows // rs
    lax.fori_loop(0, n, step, 0, unroll=min(unroll, n))


def _norm1_fwd(x, w, tr):
    S, D = x.shape

    def body(x_ref, w_ref, h_ref, ht_ref):
        def strip(rows):
            xv = x_ref[rows, :]
            r = lax.rsqrt(jnp.mean(xv * xv, axis=-1, keepdims=True) + EPS)
            h_ref[rows, :] = (xv * r * w_ref[...]).astype(bf16)
        _for_strips(tr, strip)
        ht_ref[...] = h_ref[...].T

    return pl.pallas_call(
        body, name="norm1_fwd", grid=(S // tr,),
        in_specs=[pl.BlockSpec((tr, D), lambda i: (i, 0)), pl.BlockSpec((1, D), lambda i: (0, 0))],
        out_specs=[pl.BlockSpec((tr, D), lambda i: (i, 0)), pl.BlockSpec((D, tr), lambda i: (0, i))],
        out_shape=[jax.ShapeDtypeStruct((S, D), bf16), jax.ShapeDtypeStruct((D, S), bf16)], compiler_params=_params(("parallel",)),
    )(x, w)


def _norm1_bwd(x, w, dh, dx2, tr):
    S, D = x.shape

    def body(x_ref, w_ref, dh_ref, dx2_ref, gx_ref, gw_ref, acc):
        @pl.when(pl.program_id(0) == 0)
        def _():
            acc[...] = jnp.zeros_like(acc)

        def strip(rows):
            xv = x_ref[rows, :]
            r = lax.rsqrt(jnp.mean(xv * xv, axis=-1, keepdims=True) + EPS)
            xh = xv * r
            dhv = dh_ref[rows, :]
            acc[...] += dhv * xh
            dxh = dhv * w_ref[...]
            gx_ref[rows, :] = dx2_ref[rows, :] + r * (dxh - xh * jnp.mean(dxh * xh, axis=-1, keepdims=True))
        _for_strips(tr, strip)

        @pl.when(pl.program_id(0) == S // tr - 1)
        def _():
            gw_ref[...] = jnp.sum(acc[...], axis=0, keepdims=True)

    row = pl.BlockSpec((tr, D), lambda i: (i, 0))
    vec = pl.BlockSpec((1, D), lambda i: (0, 0))
    return pl.pallas_call(
        body, name="norm1_bwd", grid=(S // tr,), in_specs=[row, vec, row, row], out_specs=[row, vec],
        out_shape=[jax.ShapeDtypeStruct((S, D), f32), jax.ShapeDtypeStruct((1, D), f32)],
        scratch_shapes=[pltpu.VMEM((RS, D), f32)], compiler_params=_params(("arbitrary",)),
    )(x, w, dh, dx2)


def _final_fwd_bwd(x, mo, target, wf, tr):
    S, D = x.shape

    def body(x_ref, mo_ref, t_ref, w_ref, dx2_ref, dx2b_ref, loss_ref, gw_ref, acc, lacc):
        @pl.when(pl.program_id(0) == 0)
        def _():
            acc[...] = jnp.zeros_like(acc)
            lacc[...] = jnp.zeros_like(lacc)

        def strip(rows):
            x2 = x_ref[rows, :] + mo_ref[rows, :]
            r = lax.rsqrt(jnp.mean(x2 * x2, axis=-1, keepdims=True) + EPS)
            xh = x2 * r
            wv = w_ref[...]
            err = xh * wv - t_ref[rows, :]
            lacc[...] += jnp.mean(err * err, axis=-1, keepdims=True)
            dy = err * (1.0 / D)
            acc[...] += dy * xh
            dxh = dy * wv
            dx2 = r * (dxh - xh * jnp.mean(dxh * xh, axis=-1, keepdims=True))
            dx2_ref[rows, :] = dx2
            dx2b_ref[rows, :] = dx2.astype(bf16)
        _for_strips(tr, strip)

        @pl.when(pl.program_id(0) == S // tr - 1)
        def _():
            gw_ref[...] = jnp.sum(acc[...], axis=0, keepdims=True)
            loss_ref[...] = 0.5 * jnp.sum(lacc[...], axis=0, keepdims=True)

    row = pl.BlockSpec((tr, D), lambda i: (i, 0))
    vec = pl.BlockSpec((1, D), lambda i: (0, 0))
    return pl.pallas_call(
        body, name="final_norm_loss", grid=(S // tr,), in_specs=[row, row, row, vec],
        out_specs=[row, row, pl.BlockSpec((1, 1), lambda i: (0, 0)), vec],
        out_shape=[jax.ShapeDtypeStruct((S, D), f32), jax.ShapeDtypeStruct((S, D), bf16), jax.ShapeDtypeStruct((1, 1), f32),
                   jax.ShapeDtypeStruct((1, D), f32)],
        scratch_shapes=[pltpu.VMEM((RS, D), f32), pltpu.VMEM((RS, 1), f32)], compiler_params=_params(("arbitrary",)),
    )(x, mo, target, wf)


def _merge_fwd(p_r, p_s, proj, tr):
    S, D = p_r.shape

    def body(pr_ref, ps_ref, g_ref, o_ref, ot_ref):
        def strip(rows):
            gr, gs = g_ref[rows, pl.ds(0, D)], g_ref[rows, pl.ds(D, D)]
            o_ref[rows, :] = (jax.nn.sigmoid(gr) * pr_ref[rows, :] + jax.nn.sigmoid(gs) * ps_ref[rows, :]).astype(bf16)
        _for_strips(tr, strip)
        ot_ref[...] = o_ref[...].T

    row = pl.BlockSpec((tr, D), lambda i: (i, 0))
    return pl.pallas_call(
        body, name="merge_fwd", grid=(S // tr,),
        in_specs=[row, row, pl.BlockSpec((tr, 2 * D), lambda i: (i, C_GATES // (2 * D)))],
        out_specs=[row, pl.BlockSpec((D, tr), lambda i: (0, i))],
        out_shape=[jax.ShapeDtypeStruct((S, D), bf16), jax.ShapeDtypeStruct((D, S), bf16)], compiler_params=_params(("parallel",)),
    )(p_r, p_s, proj)


def _merge_bwd(dm, p_r, p_s, proj, tr):
    S, D = p_r.shape

    def body(dm_ref, pr_ref, ps_ref, g_ref, dpr_ref, dps_ref, dproj_ref):
        def strip(rows):
            dmv = dm_ref[rows, :]
            sr = jax.nn.sigmoid(g_ref[rows, pl.ds(0, D)])
            ss = jax.nn.sigmoid(g_ref[rows, pl.ds(D, D)])
            dpr_ref[rows, :] = (dmv * sr).astype(bf16)
            dps_ref[rows, :] = (dmv * ss).astype(bf16)
            dproj_ref[rows, pl.ds(0, D)] = (dmv * pr_ref[rows, :] * sr * (1.0 - sr)).astype(bf16)
            dproj_ref[rows, pl.ds(D, D)] = (dmv * ps_ref[rows, :] * ss * (1.0 - ss)).astype(bf16)
        _for_strips(tr, strip)

    row = pl.BlockSpec((tr, D), lambda i: (i, 0))
    gates = pl.BlockSpec((tr, 2 * D), lambda i: (i, C_GATES // (2 * D)))
    o = jax.ShapeDtypeStruct((S, D), bf16)
    return pl.pallas_call(
        body, name="merge_bwd", grid=(S // tr,), in_specs=[row, row, row, gates],
        out_specs=[row, row, gates], out_shape=[o, o, jax.ShapeDtypeStruct((S, N_MAIN), bf16)],
        compiler_params=_params(("parallel",)),
    )(dm, p_r, p_s, proj)


def _ssd_norm_fwd(y, proj, w, tr):
    S, W = y.shape

    def body(y_ref, z_ref, w_ref, o_ref, ot_ref):
        def strip(rows):
            u = y_ref[rows, :] * _silu(z_ref[rows, :])
            r = lax.rsqrt(jnp.mean(u * u, axis=-1, keepdims=True) + EPS)
            o_ref[rows, :] = (u * r * w_ref[...]).astype(bf16)
        _for_strips(tr, strip)
        ot_ref[...] = o_ref[...].T

    row = pl.BlockSpec((tr, W), lambda i: (i, 0))
    return pl.pallas_call(
        body, name="ssd_norm_fwd", grid=(S // tr,),
        in_specs=[row, pl.BlockSpec((tr, W), lambda i: (i, C_Z // W)), pl.BlockSpec((1, W), lambda i: (0, 0))],
        out_specs=[row, pl.BlockSpec((W, tr), lambda i: (0, i))],
        out_shape=[jax.ShapeDtypeStruct((S, W), bf16), jax.ShapeDtypeStruct((W, S), bf16)], compiler_params=_params(("parallel",)),
    )(y, proj, w)


def _ssd_norm_bwd(y, proj, w, dys, dproj, tr):
    S, W = y.shape

    def body(y_ref, z_ref, w_ref, d_ref, _, dy_ref, dz_ref, gw_ref, acc):
        @pl.when(pl.program_id(0) == 0)
        def _():
            acc[...] = jnp.zeros_like(acc)

        def strip(rows):
            yv, zv, dv = y_ref[rows, :], z_ref[rows, :], d_ref[rows, :]
            sz = _silu(zv)
            u = yv * sz
            r = lax.rsqrt(jnp.mean(u * u, axis=-1, keepdims=True) + EPS)
            un = u * r
            acc[...] += dv * un
            dun = dv * w_ref[...]
            du = r * (dun - un * jnp.mean(dun * un, axis=-1, keepdims=True))
            dy_ref[rows, :] = du * sz
            dz_ref[rows, :] = (du * yv * _dsilu(zv)).astype(bf16)
        _for_strips(tr, strip)

        @pl.when(pl.program_id(0) == S // tr - 1)
        def _():
            gw_ref[...] = jnp.sum(acc[...], axis=0, keepdims=True)

    row = pl.BlockSpec((tr, W), lambda i: (i, 0))
    zcol = pl.BlockSpec((tr, W), lambda i: (i, C_Z // W))
    vec = pl.BlockSpec((1, W), lambda i: (0, 0))
    return pl.pallas_call(
        body, name="ssd_norm_bwd", grid=(S // tr,),
        in_specs=[row, zcol, vec, row, ANY], out_specs=[row, zcol, vec],
        out_shape=[jax.ShapeDtypeStruct((S, W), f32), jax.ShapeDtypeStruct(dproj.shape, bf16), jax.ShapeDtypeStruct((1, W), f32)],
        input_output_aliases={4: 1}, scratch_shapes=[pltpu.VMEM((RS, W), f32)], compiler_params=_params(("arbitrary",)),
    )(y, proj, w, dys, dproj)


def _rope(t, cos, sin):
    t1, t2 = t[:, :128], t[:, 128:]
    return jnp.concatenate([t1 * cos - t2 * sin, t2 * cos + t1 * sin], axis=1)


def _rope_t(d, cos, sin):
    d1, d2 = d[:, :128], d[:, 128:]
    return jnp.concatenate([d1 * cos + d2 * sin, d2 * cos - d1 * sin], axis=1)


def _ret_specs(tb, rev_nb=None):
    def blk(i):
        return i if rev_nb is None else rev_nb - 1 - i
    head = pl.BlockSpec((tb, RET_HW), lambda h, i: (blk(i), h))
    tab = pl.BlockSpec((tb, 128), lambda h, i: (blk(i), 0))
    mat = pl.BlockSpec((1, CHUNK, CHUNK), lambda h, i: (h, 0, 0))
    vec = pl.BlockSpec((1, CHUNK, 1), lambda h, i: (h, 0, 0))
    one = pl.BlockSpec((1, 1, 1), lambda h, i: (h, 0, 0))
    own = pl.BlockSpec((tb, RET_DK), lambda h, i: (blk(i), h))
    st = pl.BlockSpec((1, tb // CHUNK, RET_DK, RET_DK), lambda h, i: (h, blk(i), 0, 0))
    return head, tab, mat, vec, one, own, st


def _ret_fwd(proj, cos, sin, intra, qdec, kdec, cdec, tb):
    S = proj.shape[0]
    nc = S // CHUNK
    scale = RET_DK ** -0.5
    dk = RET_DK

    def body(p_ref, cos_ref, sin_ref, m_ref, qd_ref, kd_ref, cd_ref, y_ref, yr_ref, yrt_ref, st_ref, st):
        @pl.when(pl.program_id(1) == 0)
        def _():
            st[...] = jnp.zeros_like(st)

        mm, qd, kd, cd = m_ref[0], qd_ref[0], kd_ref[0], cd_ref[0]

        def chunk(c, carry):
            rows = pl.ds(pl.multiple_of(c * CHUNK, CHUNK), CHUNK)
            cs, sn = cos_ref[rows, :], sin_ref[rows, :]
            qr = _rope(p_ref[rows, pl.ds(0, dk)], cs, sn)
            kr = _rope(p_ref[rows, pl.ds(dk, dk)], cs, sn) * scale
            qb, kb, vb = qr.astype(bf16), kr.astype(bf16), p_ref[rows, pl.ds(2 * dk, dk)].astype(bf16)
            stb = st[...].astype(bf16)
            st_ref[0, c] = stb
            sc = (_nt(qb, kb) * mm).astype(bf16)
            y = _nn(sc, vb) + _nn(qb, stb) * qd
            st[...] = st[...] * cd + _tn((kr * kd).astype(bf16), vb)
            y_ref[rows, :] = y
            mu = jnp.mean(y, axis=-1, keepdims=True)
            yc = y - mu
            var = jnp.mean(yc * yc, axis=-1, keepdims=True)
            yr_ref[rows, :] = (yc * lax.rsqrt(var + EPS) * _silu(p_ref[rows, pl.ds(3 * dk, dk)])).astype(bf16)
            return carry

        lax.fori_loop(0, tb // CHUNK, chunk, 0, unroll=min(4, tb // CHUNK))
        yrt_ref[...] = yr_ref[...].T

    head, tab, mat, vec, one, own, stspec = _ret_specs(tb)
    return pl.pallas_call(
        body, name="ret_fwd", grid=(RET_HEADS, S // tb),
        in_specs=[head, tab, tab, mat, vec, vec, one],
        out_specs=[own, own, pl.BlockSpec((RET_DK, tb), lambda h, i: (h, i)), stspec],
        out_shape=[jax.ShapeDtypeStruct((S, 2048), f32), jax.ShapeDtypeStruct((S, 2048), bf16), jax.ShapeDtypeStruct((2048, S), bf16),
                   jax.ShapeDtypeStruct((RET_HEADS, nc, dk, dk), bf16)],
        scratch_shapes=[pltpu.VMEM((dk, dk), f32)], compiler_params=_params(("parallel", "arbitrary")),
    )(proj, cos, sin, intra, qdec, kdec, cdec)


def _ret_bwd(proj, cos, sin, intra, qdec, kdec, cdec, y, dyr, states, dproj, tb):
    S = proj.shape[0]
    nb = S // tb
    nck = tb // CHUNK
    scale = RET_DK ** -0.5
    dk = RET_DK

    def body(p_ref, cos_ref, sin_ref, m_ref, qd_ref, kd_ref, cd_ref, y_ref, dyr_ref, st_ref, _, o_ref, dst):
        @pl.when(pl.program_id(1) == 0)
        def _():
            dst[...] = jnp.zeros_like(dst)

        mm, qd, kd, cd = m_ref[0], qd_ref[0], kd_ref[0], cd_ref[0]

        def chunk(cc, carry):
            c = nck - 1 - cc
            rows = pl.ds(pl.multiple_of(c * CHUNK, CHUNK), CHUNK)
            cs, sn = cos_ref[rows, :], sin_ref[rows, :]
            qr = _rope(p_ref[rows, pl.ds(0, dk)], cs, sn)
            kr = _rope(p_ref[rows, pl.ds(dk, dk)], cs, sn) * scale
            qb, kb, vb = qr.astype(bf16), kr.astype(bf16), p_ref[rows, pl.ds(2 * dk, dk)].astype(bf16)
            kdb = (kr * kd).astype(bf16)
            stb = st_ref[0, c]
            yv, gv, dyrv = y_ref[rows, :], p_ref[rows, pl.ds(3 * dk, dk)], dyr_ref[rows, :]
            mu = jnp.mean(yv, axis=-1, keepdims=True)
            yc = yv - mu
            rstd = lax.rsqrt(jnp.mean(yc * yc, axis=-1, keepdims=True) + EPS)
            yn = yc * rstd
            o_ref[rows, pl.ds(3 * dk, dk)] = (dyrv * yn * _dsilu(gv)).astype(bf16)
            dyn = dyrv * _silu(gv)
            dy = rstd * (dyn - jnp.mean(dyn, axis=-1, keepdims=True) - yn * jnp.mean(dyn * yn, axis=-1, keepdims=True))
            dyb = dy.astype(bf16)
            dyqb = (dy * qd).astype(bf16)
            dstb = dst[...].astype(bf16)
            sct = (_nt(kb, qb) * mm).astype(bf16)
            ds = (_nt(dyb, vb) * mm).astype(bf16)
            dsT = (_nt(vb, dyb) * mm).astype(bf16)
            dv = _nn(sct, dyb) + _nn(kdb, dstb)
            dqr = _nn(ds, kb) + _nt(dyqb, stb)
            dkr = _nn(dsT, qb) + _nt(vb, dstb) * kd
            dst[...] = dst[...] * cd + _tn(qb, dyqb)
            o_ref[rows, pl.ds(0, dk)] = _rope_t(dqr, cs, sn).astype(bf16)
            o_ref[rows, pl.ds(dk, dk)] = (_rope_t(dkr, cs, sn) * scale).astype(bf16)
            o_ref[rows, pl.ds(2 * dk, dk)] = dv.astype(bf16)
            return carry

        lax.fori_loop(0, nck, chunk, 0, unroll=min(4, nck))

    head, tab, mat, vec, one, own, stspec = _ret_specs(tb, rev_nb=nb)
    return pl.pallas_call(
        body, name="ret_bwd", grid=(RET_HEADS, nb),
        in_specs=[head, tab, tab, mat, vec, vec, one, own, own, stspec, ANY],
        out_specs=head, out_shape=jax.ShapeDtypeStruct(dproj.shape, bf16), input_output_aliases={10: 0},
        scratch_shapes=[pltpu.VMEM((dk, dk), f32)], compiler_params=_params(("parallel", "arbitrary")),
    )(proj, cos, sin, intra, qdec, kdec, cdec, y, dyr, states, dproj)


def _conv_fwd(proj, conv_w, conv_b, tb, cw):
    S = proj.shape[0]
    off = C_XBC // cw

    def body(x_ref, halo_ref, w_ref, b_ref, o_ref, xe):
        xe[pl.ds(0, 8), :] = jnp.where(pl.program_id(1) == 0, 0.0, halo_ref[...])
        xe[pl.ds(8, CS), :] = x_ref[pl.ds(0, CS), :]
        ws = [w_ref[pl.ds(j, 1), :] for j in range(SSD_CONV)]
        for s in range(tb // CS):
            tap = (lambda j: xe[pl.ds(5 + j, CS), :]) if s == 0 else (lambda j, s=s: x_ref[pl.ds(s * CS - 3 + j, CS), :])
            acc = b_ref[...] + ws[0] * tap(0)
            for j in range(1, SSD_CONV):
                acc = acc + ws[j] * tap(j)
            o_ref[pl.ds(s * CS, CS), :] = acc

    return pl.pallas_call(
        body, name="conv_fwd", grid=(CONV_DIM // cw, S // tb),
        in_specs=[pl.BlockSpec((tb, cw), lambda j, i: (i, off + j)),
                  pl.BlockSpec((8, cw), lambda j, i: (jnp.maximum(i * (tb // 8) - 1, 0), off + j)),
                  pl.BlockSpec((SSD_CONV, cw), lambda j, i: (0, j)), pl.BlockSpec((1, cw), lambda j, i: (0, j))],
        out_specs=pl.BlockSpec((tb, cw), lambda j, i: (i, j)),
        out_shape=jax.ShapeDtypeStruct((S, CONV_DIM), f32),
        scratch_shapes=[pltpu.VMEM((CS + 8, cw), f32)], compiler_params=_params(("parallel", "arbitrary")),
    )(proj, proj, conv_w, conv_b)


def _conv_bwd(dpre, proj, conv_w, dproj, tb, cw):
    S, n = dpre.shape
    nb = S // tb
    xoff = C_XBC // cw

    def body(d_ref, dh_ref, x_ref, xh_ref, w_ref, _, dx_ref, gw_ref, gb_ref, de, xe, accw, accb):
        i = pl.program_id(1)

        @pl.when(i == 0)
        def _():
            accw[...] = jnp.zeros_like(accw)
            accb[...] = jnp.zeros_like(accb)

        ns = tb // CS
        de[pl.ds(0, CS), :] = d_ref[pl.ds(tb - CS, CS), :]
        de[pl.ds(CS, 8), :] = jnp.where(i == nb - 1, 0.0, dh_ref[...])
        xe[pl.ds(0, 8), :] = jnp.where(i == 0, 0.0, xh_ref[...])
        xe[pl.ds(8, CS), :] = x_ref[pl.ds(0, CS), :]
        ws = [w_ref[pl.ds(j, 1), :] for j in range(SSD_CONV)]
        fold = lambda p: sum(p[8 * q:8 * (q + 1)] for q in range(1, CS // 8)) + p[0:8]
        for s in range(ns):
            dv = d_ref[pl.ds(s * CS, CS), :]
            ahead = (lambda o: de[pl.ds(o, CS), :]) if s == ns - 1 else (lambda o, s=s: d_ref[pl.ds(s * CS + o, CS), :])
            xtap = (lambda j: xe[pl.ds(5 + j, CS), :]) if s == 0 else (lambda j, s=s: x_ref[pl.ds(s * CS - 3 + j, CS), :])
            acc = ws[SSD_CONV - 1] * dv
            for j in range(SSD_CONV - 1):
                acc = acc + ws[j] * ahead(3 - j)
            dx_ref[pl.ds(s * CS, CS), :] = acc.astype(bf16)
            accb[...] += fold(dv)
            for j in range(SSD_CONV):
                accw[j] += fold(dv * xtap(j))

        @pl.when(i == nb - 1)
        def _():
            gb_ref[...] = jnp.sum(accb[...], axis=0, keepdims=True)
            for j in range(SSD_CONV):
                gw_ref[pl.ds(j, 1), :] = jnp.sum(accw[j], axis=0, keepdims=True)

    return pl.pallas_call(
        body, name="conv_bwd", grid=(n // cw, nb),
        in_specs=[pl.BlockSpec((tb, cw), lambda j, i: (i, j)),
                  pl.BlockSpec((8, cw), lambda j, i: (jnp.minimum((i + 1) * (tb // 8), S // 8 - 1), j)),
                  pl.BlockSpec((tb, cw), lambda j, i: (i, xoff + j)),
                  pl.BlockSpec((8, cw), lambda j, i: (jnp.maximum(i * (tb // 8) - 1, 0), xoff + j)),
                  pl.BlockSpec((SSD_CONV, cw), lambda j, i: (0, j)), ANY],
        out_specs=[pl.BlockSpec((tb, cw), lambda j, i: (i, xoff + j)), pl.BlockSpec((SSD_CONV, cw), lambda j, i: (0, j)),
                   pl.BlockSpec((1, cw), lambda j, i: (0, j))],
        out_shape=[jax.ShapeDtypeStruct(dproj.shape, bf16), jax.ShapeDtypeStruct((SSD_CONV, n), f32), jax.ShapeDtypeStruct((1, n), f32)],
        input_output_aliases={5: 0},
        scratch_shapes=[pltpu.VMEM((CS + 8, cw), f32), pltpu.VMEM((CS + 8, cw), f32), pltpu.VMEM((SSD_CONV, 8, cw), f32),
                        pltpu.VMEM((8, cw), f32)],
        compiler_params=_params(("parallel", "arbitrary")),
    )(dpre, dpre, proj, proj, conv_w, dproj)


def _dt_prep(dt_raw, dt_bias, a_log, tb):
    S = dt_raw.shape[0]

    def body(r_ref, b_ref, al_ref, dt_ref, sg_ref, ac_ref):
        li = lax.broadcasted_iota(jnp.int32, (LS, LS), 0)
        si = lax.broadcasted_iota(jnp.int32, (LS, LS), 1)
        tri = (li >= si).astype(f32)
        neg_a = -jnp.exp(al_ref[...])
        for c in range(tb // LS):
            rows = pl.ds(c * LS, LS)
            xv = r_ref[rows, :] + b_ref[...]
            dtv = jax.nn.softplus(xv)
            dt_ref[rows, :] = dtv
            sg_ref[rows, :] = jax.nn.sigmoid(xv)
            ac_ref[rows, :] = _hi(tri, dtv * neg_a)

    row = pl.BlockSpec((tb, 128), lambda i: (i, 0))
    vec = pl.BlockSpec((1, 128), lambda i: (0, 0))
    o = jax.ShapeDtypeStruct((S, 128), f32)
    return pl.pallas_call(body, name="dt_prep", grid=(S // tb,), in_specs=[row, vec, vec], out_specs=[row, row, row],
                          out_shape=[o, o, o], compiler_params=_params(("parallel",)))(dt_raw, dt_bias, a_log)


def _group_major(t):
    S = t.shape[0]
    return jnp.transpose(t[:, :SSD_HEADS].reshape(S, SSD_GROUPS, SSD_HPG), (1, 0, 2))


def _group_major_t(t):
    S = t.shape[0]
    return jnp.transpose(t[:, :SSD_HEADS].reshape(S // LS, LS, SSD_GROUPS, SSD_HPG), (2, 0, 3, 1))


def _ssd_specs(tb, rev_nb=None):
    def blk(i):
        return i if rev_nb is None else rev_nb - 1 - i
    grp = pl.BlockSpec((tb, SSD_GC), lambda g, i: (blk(i), g))
    xs = pl.BlockSpec((tb, SSD_GW), lambda g, i: (blk(i), g))
    ph = pl.BlockSpec((1, tb, SSD_HPG), lambda g, i: (g, blk(i), 0))
    pht = pl.BlockSpec((1, tb // LS, SSD_HPG, LS), lambda g, i: (g, blk(i), 0, 0))
    gvec = pl.BlockSpec((1, 1, SSD_GW), lambda g, i: (g, 0, 0))
    ex = pl.BlockSpec((SSD_HPG, SSD_GW), lambda g, i: (0, 0))
    st = pl.BlockSpec((1, tb // LS, SSD_STATE, SSD_GW), lambda g, i: (g, blk(i), 0, 0))
    return grp, xs, ph, pht, gvec, ex, st


def _expander():
    return jnp.repeat(jnp.eye(SSD_HPG, dtype=f32), SSD_GW // SSD_HPG, axis=1).astype(bf16)


def _expand3(dt8, ac8, ex):
    stack = jnp.concatenate([dt8, jnp.exp(ac8), jnp.exp(ac8[LS - 1:LS, :] - ac8)], axis=0)
    wide = _sel_r(stack, ex)
    return wide[0:LS], wide[LS:2 * LS], wide[2 * LS:3 * LS]


def _ssd_fwd(pre, dt_g, ac_g, act_g, dskx, tb):
    S = pre.shape[0]
    nc = S // LS
    hd = SSD_GW // SSD_HPG

    def body(p_ref, dt_ref, ac_ref, act_ref, dsk_ref, ex_ref, y_ref, st_ref, st):
        @pl.when(pl.program_id(1) == 0)
        def _():
            st[...] = jnp.zeros_like(st)

        ex = ex_ref[...]
        li = lax.broadcasted_iota(jnp.int32, (LS, LS), 0)
        si = lax.broadcasted_iota(jnp.int32, (LS, LS), 1)
        causal = li >= si

        def chunk(c, carry):
            rows = pl.ds(pl.multiple_of(c * LS, LS), LS)
            xs = _silu(p_ref[rows, pl.ds(0, SSD_GW)])
            bcb = _silu(p_ref[rows, pl.ds(SSD_GW, SSD_STATE)]).astype(bf16)
            ccb = _silu(p_ref[rows, pl.ds(SSD_GW + SSD_STATE, SSD_STATE)]).astype(bf16)
            dt8, ac8, act = dt_ref[0, rows, :], ac_ref[0, rows, :], act_ref[0, c]
            dtx, eax, tailx = _expand3(dt8, ac8, ex)
            xdt = xs * dtx
            cb = _nt(ccb, bcb)
            stb = st[...].astype(bf16)
            st_ref[0, c] = stb
            xdtb = xdt.astype(bf16)
            outs = []
            for h in range(SSD_HPG):
                dec = jnp.exp(jnp.where(causal, ac8[:, h:h + 1] - act[h:h + 1, :], -1e30))
                outs.append(_nn((cb * dec).astype(bf16), xdtb[:, hd * h:hd * (h + 1)]))
            y_ref[rows, :] = jnp.concatenate(outs, axis=1) + _nn(ccb, stb) * eax + dsk_ref[0] * xs
            st[...] = st[...] * eax[LS - 1:LS, :] + _tn(bcb, (xdt * tailx).astype(bf16))
            return carry

        lax.fori_loop(0, tb // LS, chunk, 0, unroll=min(4, tb // LS))

    grp, xs, ph, pht, gvec, ex, stspec = _ssd_specs(tb)
    return pl.pallas_call(
        body, name="ssd_fwd", grid=(SSD_GROUPS, S // tb),
        in_specs=[grp, ph, ph, pht, gvec, ex], out_specs=[xs, stspec],
        out_shape=[jax.ShapeDtypeStruct((S, SSD_WIDTH), f32), jax.ShapeDtypeStruct((SSD_GROUPS, nc, SSD_STATE, SSD_GW), bf16)],
        scratch_shapes=[pltpu.VMEM((SSD_STATE, SSD_GW), f32)], compiler_params=_params(("parallel", "arbitrary")),
    )(pre, dt_g, ac_g, act_g, dskx, _expander())


def _ssd_bwd(pre, dt_g, ac_g, act_g, sg_g, dskx, nega_g, dy, states, tb):
    S = pre.shape[0]
    nb = S // tb
    nck = tb // LS
    hd = SSD_GW // SSD_HPG

    def body(p_ref, dt_ref, ac_ref, act_ref, sg_ref, dsk_ref, na_ref, ex_ref, ext_ref, dy_ref, st_ref,
             dp_ref, ddt_ref, gsk_ref, gal_ref, gdb_ref, dst, skacc):
        @pl.when(pl.program_id(1) == 0)
        def _():
            dst[...] = jnp.zeros_like(dst)
            skacc[...] = jnp.zeros_like(skacc)
            gal_ref[...] = jnp.zeros_like(gal_ref)
            gdb_ref[...] = jnp.zeros_like(gdb_ref)

        ex, ext = ex_ref[...], ext_ref[...]
        li = lax.broadcasted_iota(jnp.int32, (LS, LS), 0)
        si = lax.broadcasted_iota(jnp.int32, (LS, LS), 1)
        causal = li >= si
        anti = si >= li
        upper = anti.astype(bf16)
        eye = (si == li).astype(bf16)
        last_row = (lax.broadcasted_iota(jnp.int32, (LS, 1), 0) == LS - 1).astype(f32)
        head_id = lax.broadcasted_iota(jnp.int32, (1, SSD_HPG), 1)
        head_col = lax.broadcasted_iota(jnp.int32, (SSD_HPG, 1), 0)
        neg_a = na_ref[0]
        dskv = dsk_ref[0]

        def chunk(cc, carry):
            c = nck - 1 - cc
            rows = pl.ds(pl.multiple_of(c * LS, LS), LS)
            px = p_ref[rows, pl.ds(0, SSD_GW)]
            pb = p_ref[rows, pl.ds(SSD_GW, SSD_STATE)]
            pc = p_ref[rows, pl.ds(SSD_GW + SSD_STATE, SSD_STATE)]
            sgx, sgb, sgc = jax.nn.sigmoid(px), jax.nn.sigmoid(pb), jax.nn.sigmoid(pc)
            xs = px * sgx
            bcb = (pb * sgb).astype(bf16)
            ccb = (pc * sgc).astype(bf16)
            dt8, ac8, act = dt_ref[0, rows, :], ac_ref[0, rows, :], act_ref[0, c]
            dtx, eax, tailx = _expand3(dt8, ac8, ex)
            xdt = xs * dtx
            ex_last = eax[LS - 1:LS, :]
            stb = st_ref[0, c]
            dyv = dy_ref[rows, :]
            dyb = dyv.astype(bf16)
            xdtb = xdt.astype(bf16)
            skacc[...] += jnp.sum(dyv * xs, axis=0, keepdims=True)
            yinter = _nn(ccb, stb) * eax
            dzb = (dyv * eax).astype(bf16)
            dcc = _nt(dzb, stb)
            dstv = dst[...]
            dstb = dstv.astype(bf16)
            xt = xdt * tailx
            dxt = _nn(bcb, dstb)
            dbc = _nt(xt.astype(bf16), dstb)
            dxdt = dxt * tailx
            lastrow = jnp.sum(dxt * xt, axis=0, keepdims=True) + jnp.sum(dstv * stb.astype(f32), axis=0, keepdims=True) * ex_last
            dst[...] = dstv * ex_last + _tn(ccb, dzb)
            cb = _nt(ccb, bcb)
            cbt = _nt(bcb, ccb)
            dcb = jnp.zeros((LS, LS), f32)
            dac8 = jnp.zeros((LS, SSD_HPG), f32)
            dact = jnp.zeros((SSD_HPG, LS), f32)
            dxin = []
            for h in range(SSD_HPG):
                sl = slice(hd * h, hd * (h + 1))
                col, rowv = ac8[:, h:h + 1], act[h:h + 1, :]
                dec = jnp.exp(jnp.where(causal, col - rowv, -1e30))
                dect = jnp.exp(jnp.where(anti, rowv - col, -1e30))
                gm = cb * dec
                dgm = _nt(dyb[:, sl], xdtb[:, sl])
                dxin.append(_nn((cbt * dect).astype(bf16), dyb[:, sl]))
                dcb = dcb + dgm * dec
                w = dgm * gm
                dac8 = dac8 + jnp.sum(w, axis=1, keepdims=True) * (head_id == h).astype(f32)
                dact = dact + (head_col == h).astype(f32) * jnp.sum(w, axis=0, keepdims=True)
            dxintra = jnp.concatenate(dxin, axis=1)
            dcbb = dcb.astype(bf16)
            dcc = dcc + _nn(dcbb, bcb)
            dbc = dbc + _tn(dcbb, ccb)
            dxdt = dxdt + dxintra
            dacx = dyv * yinter - dxt * xt + last_row * lastrow
            red = _sel_r(jnp.concatenate([dacx, dxdt * xs], axis=0), ext)
            dac8 = dac8 - _rows_to_cols(dact, eye) + red[0:LS]
            da8 = _sel_l(upper, dac8)
            ddt8 = red[LS:2 * LS] + da8 * neg_a
            gal_ref[0] += jnp.sum(da8 * dt8 * neg_a, axis=0, keepdims=True)
            ddr = ddt8 * sg_ref[0, rows, :]
            ddt_ref[0, rows, :] = ddr
            gdb_ref[0] += jnp.sum(ddr, axis=0, keepdims=True)
            dsilu = lambda p, s: s * (1.0 + p * (1.0 - s))
            dp_ref[rows, pl.ds(0, SSD_GW)] = (dskv * dyv + dxdt * dtx) * dsilu(px, sgx)
            dp_ref[rows, pl.ds(SSD_GW, SSD_STATE)] = dbc * dsilu(pb, sgb)
            dp_ref[rows, pl.ds(SSD_GW + SSD_STATE, SSD_STATE)] = dcc * dsilu(pc, sgc)
            return carry

        lax.fori_loop(0, nck, chunk, 0, unroll=min(4, nck))

        @pl.when(pl.program_id(1) == nb - 1)
        def _():
            gsk_ref[0] = skacc[...]

    grp, xs, ph, pht, gvec, ex, stspec = _ssd_specs(tb, rev_nb=nb)
    small = pl.BlockSpec((1, 1, SSD_HPG), lambda g, i: (g, 0, 0))
    ext = pl.BlockSpec((SSD_GW, SSD_HPG), lambda g, i: (0, 0))
    sm = jax.ShapeDtypeStruct((SSD_GROUPS, 1, SSD_HPG), f32)
    expander = _expander()
    return pl.pallas_call(
        body, name="ssd_bwd", grid=(SSD_GROUPS, nb),
        in_specs=[grp, ph, ph, pht, ph, gvec, small, ex, ext, xs, stspec],
        out_specs=[grp, ph, gvec, small, small],
        out_shape=[jax.ShapeDtypeStruct((S, CONV_DIM), f32), jax.ShapeDtypeStruct((SSD_GROUPS, S, SSD_HPG), f32),
                   jax.ShapeDtypeStruct((SSD_GROUPS, 1, SSD_GW), f32), sm, sm],
        scratch_shapes=[pltpu.VMEM((SSD_STATE, SSD_GW), f32), pltpu.VMEM((1, SSD_GW), f32)],
        compiler_params=_params(("parallel", "arbitrary")),
    )(pre, dt_g, ac_g, act_g, sg_g, dskx, nega_g, expander, expander.T, dy, states)


def _tiles(S):
    return dict(tb=min(512, S), tr=min(256, S), tm=min(1024, S))


def _local_step(x, positions, target, norm1_w, w_main, w_dt, conv_w, conv_b, dt_bias, a_log, d_skip, ssd_norm_w,
                late_token, late_weights, norm_f_w, on_weight_grads):
    S, D = x.shape
    t = _tiles(S)
    tb, tr, tm = t["tb"], t["tr"], t["tm"]

    half = RET_DK // 2
    inv_freq = ROPE_THETA ** (-jnp.arange(half, dtype=f32) / half)
    ang = positions.astype(f32)[:, None] * inv_freq
    cos, sin = jnp.cos(ang), jnp.sin(ang)
    log_gamma = jnp.log1p(-(2.0 ** (-5.0 - jnp.arange(RET_HEADS, dtype=f32))))
    idx = jnp.arange(CHUNK, dtype=f32)
    intra = jnp.exp(jnp.abs(idx[:, None] - idx[None, :]) * log_gamma[:, None, None])
    qdec = jnp.exp((idx + 1.0)[None, :] * log_gamma[:, None])[:, :, None]
    kdec = jnp.exp((CHUNK - 1.0 - idx)[None, :] * log_gamma[:, None])[:, :, None]
    cdec = jnp.exp(CHUNK * log_gamma)[:, None, None]
    conv_wm, conv_bm = _xbc_group_major(conv_w), _xbc_group_major(conv_b)

    h, ht = _norm1_fwd(x, norm1_w + late_token[0, 0], tr)
    proj = _mm1(h, w_main, tm=tm, tn=1024, tk=D, out_dtype=f32, name="proj_main")
    dt_raw = _mm1(h, w_dt, tm=tm, tn=128, tk=D, out_dtype=f32, name="proj_dt")
    y_ret, yr, yrt, ret_states = _ret_fwd(proj, cos, sin, intra, qdec, kdec, cdec, tb)
    pre = _conv_fwd(proj, conv_wm, conv_bm, min(1024, S), 512)
    pad64 = lambda v: jnp.pad(v, ((0, 0), (0, 128 - SSD_HEADS)))
    dt, sg, ac = _dt_prep(dt_raw, pad64(dt_bias), pad64(a_log), tb)
    dt_g, ac_g, sg_g, act_g = _group_major(dt), _group_major(ac), _group_major(sg), _group_major_t(ac)
    dskx = jnp.repeat(d_skip.reshape(SSD_GROUPS, 1, SSD_HPG), SSD_GW // SSD_HPG, axis=2)
    nega_g = (-jnp.exp(a_log)).reshape(SSD_GROUPS, 1, SSD_HPG)
    y_ssd, ssd_states = _ssd_fwd(pre, dt_g, ac_g, act_g, dskx, tb)
    ys, yst = _ssd_norm_fwd(y_ssd, proj, ssd_norm_w, tr // 2)
    w_br, w_bs, w_o = late_weights(ys)
    p_r = _mm1(yr, w_br, tm=tm, tn=1024, tk=2048, out_dtype=f32, name="branch_ret")
    p_s = _mm1(ys, w_bs, tm=tm, tn=1024, tk=2048, out_dtype=f32, name="branch_ssd")
    merged, mergedt = _merge_fwd(p_r, p_s, proj, tr)
    mo = _mm1(merged, w_o, tm=tm, tn=1024, tk=2048, out_dtype=f32, name="out_proj")
    dx2, dx2b, loss, g_norm_f = _final_fwd_bwd(x, mo, target, norm_f_w.reshape(1, D), tr)

    tkt = min(4096, S)
    wg = lambda at, b, name, tn=1024: _mm1(at, b, tm=min(1024, at.shape[0]), tn=tn, tk=tkt, out_dtype=f32, name=name)
    dm = _mm1(dx2b, w_o, tm=tm, tn=1024, tk=2048, out_dtype=f32, name="d_merged", tb=True)
    g_w_o = wg(mergedt, dx2b, "g_w_out")
    dp_r, dp_s, dproj = _merge_bwd(dm, p_r, p_s, proj, tr)
    dyr = _mm1(dp_r, w_br, tm=tm, tn=1024, tk=2048, out_dtype=f32, name="d_yr", tb=True)
    dys = _mm1(dp_s, w_bs, tm=tm, tn=1024, tk=2048, out_dtype=f32, name="d_ys", tb=True)
    g_w_br = wg(yrt, dp_r, "g_w_br_ret")
    g_w_bs = wg(yst, dp_s, "g_w_br_ssd")
    dy_ssd, dproj, g_ssd_norm = _ssd_norm_bwd(y_ssd, proj, ssd_norm_w, dys, dproj, tr // 2)
    dproj = _ret_bwd(proj, cos, sin, intra, qdec, kdec, cdec, y_ret, dyr, ret_states, dproj, tb)
    dpre, ddt_g, gsk, gal, gdb = _ssd_bwd(pre, dt_g, ac_g, act_g, sg_g, dskx, nega_g, dy_ssd, ssd_states, tb)
    dproj, gcw, gcb = _conv_bwd(dpre, proj, conv_wm, dproj, min(1024, S), 512)
    ddt = jnp.transpose(ddt_g, (1, 0, 2)).reshape(S, SSD_HEADS)
    ddt_p = jnp.pad(ddt, ((0, 0), (0, 128 - SSD_HEADS))).astype(bf16)

    g_main = wg(ht, dproj, "g_w_in_main")
    g_dt = wg(ht, ddt_p, "g_w_in_dt", tn=128)
    reduce_state = on_weight_grads(g_main, g_dt, g_w_br, g_w_bs, g_w_o)
    ddt_p = ddt_p + reduce_state[-1][0, 0].astype(bf16)
    dh = _mm([(dproj, 0, w_main, 0, 0, N_MAIN, N_MAIN // 8), (ddt_p, 0, w_dt, 0, 0, 128, 128)], S, D, tm=tm, tn=1024,
             out_dtype=f32, name="d_h", tb=True)
    grad_x, g_norm1 = _norm1_bwd(x, norm1_w, dh, dx2, tr)

    seg = lambda v: jnp.sum(v.reshape(SSD_HEADS, SSD_GW // SSD_HPG), axis=1).reshape(1, SSD_HEADS)
    grads = dict(
        norm1_w=g_norm1, w_in_main=g_main, w_in_dt=g_dt,
        conv_w=_xbc_original(gcw), conv_b=_xbc_original(gcb),
        dt_bias=gdb.reshape(1, SSD_HEADS), a_log=gal.reshape(1, SSD_HEADS), d_skip=seg(gsk),
        ssd_norm_w=g_ssd_norm, w_br_ret=g_w_br, w_br_ssd=g_w_bs, w_out=g_w_o, norm_f_w=g_norm_f,
    )
    return loss, grad_x, grads, reduce_state


def _me():
    return lax.axis_index("x"), lax.axis_index("y"), lax.axis_index("c")


def _other_chips(x, y):
    return [(1 - x, y), (x, 1 - y), (1 - x, 1 - y)]


def _gather_weights(a, cw):
    R = a.shape[0]
    hr = R // 2

    def body(a_ref, cw_ref, ga_ref, gc_ref, send_sems, recv_sems):
        x, y, c = _me()
        k = 2 * x + y
        sibling = (x, y, 1 - c)
        chips = _other_chips(x, y)

        def small(j, src_shard, to):
            return pltpu.make_async_remote_copy(
                src_ref=cw_ref, dst_ref=gc_ref.at[src_shard], send_sem=send_sems.at[6 + j], recv_sem=recv_sems.at[6 + j],
                device_id=to, device_id_type=MESH)

        def copies(j, src_shard, half, to, from_input):
            rows = pl.ds(half * hr, hr)
            src = a_ref.at[rows, :] if from_input else ga_ref.at[src_shard, rows, :]
            return [pltpu.make_async_remote_copy(
                src_ref=src, dst_ref=ga_ref.at[src_shard, rows, :], send_sem=send_sems.at[j], recv_sem=recv_sems.at[j],
                device_id=to, device_id_type=MESH)]

        first = []
        for j, chip in enumerate(chips):
            first += copies(j, k, c, (*chip, c), True)
            first.append(small(j, k, (*chip, c)))
        for cp in first:
            cp.start()
        passed = []
        for j, chip in enumerate(chips):
            kk = 2 * chip[0] + chip[1]
            for cp in copies(j, kk, c, (x, y, c), False):
                cp.wait_recv()
            fw = copies(3 + j, kk, c, sibling, False)
            for cp in fw:
                cp.start()
            passed += fw
        for j, chip in enumerate(chips):
            kk = 2 * chip[0] + chip[1]
            for cp in copies(3 + j, kk, 1 - c, (x, y, c), False):
                cp.wait_recv()
            small(j, kk, (x, y, c)).wait_recv()
        for cp in first + passed:
            cp.wait_send()

    return pl.pallas_call(
        body, name="gather_weights", in_specs=[ANY, ANY], out_specs=[ANY, ANY],
        out_shape=[jax.ShapeDtypeStruct((N_SHARD,) + a.shape, a.dtype), jax.ShapeDtypeStruct((N_SHARD,) + cw.shape, cw.dtype)],
        scratch_shapes=[pltpu.SemaphoreType.DMA((9,)), pltpu.SemaphoreType.DMA((9,))],
        compiler_params=pltpu.CompilerParams(has_side_effects=True),
    )(a, cw)


def _gather_late_copies(src, land, send_sems, recv_sems):
    x, y, c = _me()
    k = 2 * x + y
    return [pltpu.make_async_remote_copy(src_ref=src, dst_ref=land.at[k], send_sem=send_sems.at[j], recv_sem=recv_sems.at[j],
                                         device_id=(*chip, c), device_id_type=MESH) for j, chip in enumerate(_other_chips(x, y))]


def _gather_late_start(b):
    land = lax.empty((N_SHARD,) + b.shape, b.dtype)

    def body(b_ref, land_ref, send_sems, recv_sems, b_thru, land_thru, token):
        for cp in _gather_late_copies(b_ref, land_ref, send_sems, recv_sems):
            cp.start()
        token[...] = jnp.zeros_like(token)

    return pl.pallas_call(
        body, name="gather_late_start", in_specs=[HBM, HBM],
        out_specs=(SEM, SEM, HBM, HBM, pl.BlockSpec(memory_space=pltpu.VMEM)),
        out_shape=(pltpu.SemaphoreType.DMA((3,)), pltpu.SemaphoreType.DMA((3,)), pltpu.HBM(b.shape, b.dtype),
                   pltpu.HBM(land.shape, land.dtype), jax.ShapeDtypeStruct((8, 128), f32)),
        input_output_aliases={0: 2, 1: 3}, compiler_params=pltpu.CompilerParams(has_side_effects=DATAFLOW),
    )(pltpu.with_memory_space_constraint(b, pltpu.HBM), pltpu.with_memory_space_constraint(land, pltpu.HBM))


def _gather_late_wait(send_sems, recv_sems, src, land, after):
    def body(b_ref, land_ref, send_sems_ref, recv_sems_ref, after_ref, b_dead, land_out):
        x, y, c = _me()
        for j, chip in enumerate(_other_chips(x, y)):
            kk = 2 * chip[0] + chip[1]
            cp = pltpu.make_async_remote_copy(src_ref=b_ref, dst_ref=land_ref.at[kk], send_sem=send_sems_ref.at[j],
                                              recv_sem=recv_sems_ref.at[j], device_id=(x, y, c), device_id_type=MESH)
            cp.wait_send()
            cp.wait_recv()

    return pl.pallas_call(
        body, name="gather_late_wait", in_specs=[HBM, HBM, SEM, SEM, ANY], out_specs=[HBM, HBM],
        out_shape=[pltpu.HBM(src.shape, src.dtype), pltpu.HBM(land.shape, land.dtype)], input_output_aliases={0: 0, 1: 1},
        compiler_params=pltpu.CompilerParams(has_side_effects=DATAFLOW),
    )(src, land, send_sems, recv_sems, after)[1]


def _sibling_swap(arrs, name):
    n = len(arrs)
    slots = [(t, s) for t in range(n) for s in range(arrs[t].shape[0])]

    def body(*refs):
        ins, outs = refs[:n], refs[n:2 * n]
        send_sems, recv_sems = refs[2 * n], refs[2 * n + 1]
        x, y, c = _me()
        cps = [pltpu.make_async_remote_copy(src_ref=ins[t].at[s, 1 - c], dst_ref=outs[t].at[s], send_sem=send_sems.at[q],
                                            recv_sem=recv_sems.at[q], device_id=(x, y, 1 - c), device_id_type=MESH)
               for q, (t, s) in enumerate(slots)]
        for cp in cps:
            cp.start()
        for cp in cps:
            cp.wait()

    return pl.pallas_call(
        body, name=name, in_specs=[ANY] * n, out_specs=[ANY] * n,
        out_shape=[jax.ShapeDtypeStruct(a.shape[:1] + a.shape[2:], a.dtype) for a in arrs],
        scratch_shapes=[pltpu.SemaphoreType.DMA((len(slots),)), pltpu.SemaphoreType.DMA((len(slots),))],
        compiler_params=pltpu.CompilerParams(has_side_effects=True),
    )(*arrs)


HBM = pl.BlockSpec(memory_space=pltpu.HBM)
SEM = pl.BlockSpec(memory_space=pltpu.SEMAPHORE)
DATAFLOW = pltpu.SideEffectType.DATAFLOW_SIDE_EFFECTING


def _exchange_copies(ins, lands, send_sems, recv_sems):
    n = len(ins)
    x, y, c = _me()
    cps = []
    for j, chip in enumerate(_other_chips(x, y)):
        kk = 2 * chip[0] + chip[1]
        for t in range(n):
            cps.append(pltpu.make_async_remote_copy(
                src_ref=ins[t].at[kk], dst_ref=lands[t].at[j], send_sem=send_sems.at[n * j + t],
                recv_sem=recv_sems.at[n * j + t], device_id=(*chip, c), device_id_type=MESH))
    return cps


def _chip_exchange_start(arrs):
    n = len(arrs)
    lands = [lax.empty((3,) + a.shape[1:], a.dtype) for a in arrs]

    def body(*refs):
        ins, lands_in = refs[:n], refs[n:2 * n]
        send_sems, recv_sems = refs[2 * n], refs[2 * n + 1]
        token = refs[4 * n + 2]
        for cp in _exchange_copies(ins, lands_in, send_sems, recv_sems):
            cp.start()
        token[...] = jnp.zeros_like(token)

    hbm = lambda a: pltpu.HBM(a.shape, a.dtype)
    out = pl.pallas_call(
        body, name="chip_exchange_start", in_specs=[HBM] * (2 * n),
        out_specs=(SEM, SEM, *[HBM] * (2 * n), pl.BlockSpec(memory_space=pltpu.VMEM)),
        out_shape=(pltpu.SemaphoreType.DMA((3 * n,)), pltpu.SemaphoreType.DMA((3 * n,)), *[hbm(a) for a in arrs],
                   *[hbm(a) for a in lands], jax.ShapeDtypeStruct((8, 128), f32)),
        input_output_aliases={t: 2 + t for t in range(2 * n)},
        compiler_params=pltpu.CompilerParams(has_side_effects=DATAFLOW),
    )(*[pltpu.with_memory_space_constraint(a, pltpu.HBM) for a in list(arrs) + lands])
    return out[0], out[1], list(out[2:2 + n]), list(out[2 + n:2 + 2 * n]), out[2 + 2 * n]


def _chip_exchange_wait(send_sems, recv_sems, srcs, lands, after):
    n = len(srcs)

    def body(*refs):
        ins, lands_in = refs[:n], refs[n:2 * n]
        send_sems_ref, recv_sems_ref = refs[2 * n], refs[2 * n + 1]
        for cp in _exchange_copies(ins, lands_in, send_sems_ref, recv_sems_ref):
            cp.wait_send()
            cp.wait_recv()

    hbm = lambda a: pltpu.HBM(a.shape, a.dtype)
    out = pl.pallas_call(
        body, name="chip_exchange_wait", in_specs=[HBM] * (2 * n) + [SEM, SEM, ANY],
        out_specs=[HBM] * (2 * n), out_shape=[hbm(a) for a in list(srcs) + list(lands)],
        input_output_aliases={t: t for t in range(2 * n)},
        compiler_params=pltpu.CompilerParams(has_side_effects=DATAFLOW),
    )(*srcs, *lands, send_sems, recv_sems, after)
    return list(out[:n]), list(out[n:])


def _share_halves(bufs, by_cols, name):
    n = len(bufs)

    def body(*refs):
        ins, outs = refs[:n], refs[n:2 * n]
        send_sems, recv_sems = refs[2 * n], refs[2 * n + 1]
        x, y, c = _me()

        def part(ref, t, half):
            if by_cols[t]:
                w = bufs[t].shape[1] // 2
                return ref.at[:, pl.ds(pl.multiple_of(half * w, 128), w)]
            return ref.at[half]

        sends = [pltpu.make_async_remote_copy(src_ref=part(ins[t], t, c), dst_ref=part(outs[t], t, c), send_sem=send_sems.at[t],
                                              recv_sem=recv_sems.at[t], device_id=(x, y, 1 - c), device_id_type=MESH) for t in range(n)]
        for cp in sends:
            cp.start()
        for t in range(n):
            pltpu.make_async_remote_copy(src_ref=part(ins[t], t, c), dst_ref=part(outs[t], t, 1 - c), send_sem=send_sems.at[t],
                                         recv_sem=recv_sems.at[t], device_id=(x, y, c), device_id_type=MESH).wait_recv()
        for cp in sends:
            cp.wait_send()

    return pl.pallas_call(
        body, name=name, in_specs=[ANY] * n, out_specs=[ANY] * n,
        out_shape=[jax.ShapeDtypeStruct(a.shape, a.dtype) for a in bufs], input_output_aliases={t: t for t in range(n)},
        scratch_shapes=[pltpu.SemaphoreType.DMA((n,)), pltpu.SemaphoreType.DMA((n,))],
        compiler_params=pltpu.CompilerParams(has_side_effects=True),
    )(*bufs)


def _gather_vec(v):
    n = v.shape[1]

    def body(v_ref, o_ref, send_sems, recv_sems):
        x, y, c = _me()
        me = 4 * x + 2 * y + c
        cps = []
        for j in range(1, 8):
            fx, fy, fc = (j >> 2) & 1, (j >> 1) & 1, j & 1
            peer = (x ^ fx, y ^ fy, c ^ fc)
            cps.append(pltpu.make_async_remote_copy(
                src_ref=v_ref, dst_ref=o_ref.at[pl.ds(me, 1), :], send_sem=send_sems.at[j - 1], recv_sem=recv_sems.at[j - 1],
                device_id=peer, device_id_type=MESH))
        for cp in cps:
            cp.start()
        for j in range(1, 8):
            fx, fy, fc = (j >> 2) & 1, (j >> 1) & 1, j & 1
            src = 4 * (x ^ fx) + 2 * (y ^ fy) + (c ^ fc)
            pltpu.make_async_remote_copy(
                src_ref=v_ref, dst_ref=o_ref.at[pl.ds(src, 1), :], send_sem=send_sems.at[j - 1], recv_sem=recv_sems.at[j - 1],
                device_id=(x, y, c), device_id_type=MESH).wait_recv()
        for cp in cps:
            cp.wait_send()

    return pl.pallas_call(
        body, name="gather_vec", in_specs=[ANY], out_specs=ANY, out_shape=jax.ShapeDtypeStruct((8, n), v.dtype),
        scratch_shapes=[pltpu.SemaphoreType.DMA((7,)), pltpu.SemaphoreType.DMA((7,))],
        compiler_params=pltpu.CompilerParams(has_side_effects=True),
    )(v)


def _pair_sum(g, r, name, tr):
    L, _, hr, C = g.shape

    def body(c_ref, g_ref, r_ref, o_ref):
        def strip(rows):
            o_ref[0, rows, :] = (g_ref[0, 0, rows, :] + r_ref[0, rows, :]).astype(bf16)
        _for_strips(tr, strip)

    grid_spec = pltpu.PrefetchScalarGridSpec(
        num_scalar_prefetch=1, grid=(L, hr // tr),
        in_specs=[pl.BlockSpec((1, 1, tr, C), lambda s, i, c_ref: (s, c_ref[0], i, 0)),
                  pl.BlockSpec((1, tr, C), lambda s, i, c_ref: (s, i, 0))],
        out_specs=pl.BlockSpec((1, tr, C), lambda s, i, c_ref: (s, i, 0)))
    c = lax.axis_index("c").reshape(1).astype(jnp.int32)
    return pl.pallas_call(body, name=name, grid_spec=grid_spec, out_shape=jax.ShapeDtypeStruct((L, hr, C), bf16),
                          compiler_params=_params(("parallel", "parallel")))(c, g, r)


def _own_sum(p, got, name, transposed=False):
    _, hr, C = p.shape
    tr = SUM_ROWS
    c_full, c_pad = C // 128 * 128, -(-C // 128) * 128

    def total(p_ref, got_ref, rows):
        return ((p_ref[0, rows, :].astype(f32) + got_ref[0, rows, :].astype(f32)) + got_ref[1, rows, :].astype(f32)) \
            + got_ref[2, rows, :].astype(f32)

    def body(idx_ref, p_ref, got_ref, o_ref):
        def strip(rows):
            o_ref[0, rows, :] = total(p_ref, got_ref, rows)
        _for_strips(tr, strip)

    def body_t(idx_ref, p_ref, got_ref, o_ref, buf):
        if c_pad > c_full:
            buf[:, pl.ds(c_full, c_pad - c_full)] = jnp.zeros((tr, c_pad - c_full), f32)

        def strip(rows):
            buf[rows, pl.ds(0, C)] = total(p_ref, got_ref, rows)
        _for_strips(tr, strip)
        o_ref[...] = buf[...].T[:C]

    in_specs = [pl.BlockSpec((1, tr, C), lambda i, idx: (idx[0], i, 0)), pl.BlockSpec((3, tr, C), lambda i, idx: (0, i, 0))]
    x, y, c = _me()
    idx = jnp.stack([2 * x + y, c]).astype(jnp.int32)
    if transposed:
        grid_spec = pltpu.PrefetchScalarGridSpec(num_scalar_prefetch=1, grid=(hr // tr,), in_specs=in_specs,
                                                 out_specs=pl.BlockSpec((C, tr), lambda i, idx: (0, idx[1] * (hr // tr) + i)),
                                                 scratch_shapes=[pltpu.VMEM((tr, c_pad), f32)])
        return pl.pallas_call(body_t, name=name, grid_spec=grid_spec, out_shape=jax.ShapeDtypeStruct((C, 2 * hr), f32),
                              compiler_params=_params(("parallel",)))(idx, p, got)
    grid_spec = pltpu.PrefetchScalarGridSpec(num_scalar_prefetch=1, grid=(hr // tr,), in_specs=in_specs,
                                             out_specs=pl.BlockSpec((1, tr, C), lambda i, idx: (idx[1], i, 0)))
    return pl.pallas_call(body, name=name, grid_spec=grid_spec, out_shape=jax.ShapeDtypeStruct((2, hr, C), f32),
                          compiler_params=_params(("parallel",)))(idx, p, got)


def _adamw(w, g, m, v, name, tr):
    _, R, C = w.shape
    rs = min(8, tr)

    def body(w_ref, g_ref, m_ref, v_ref, d_ref, nm_ref, nv_ref):
        def strip(s, carry):
            rows = pl.ds(pl.multiple_of(s * rs, rs), rs)
            gv = g_ref[0, rows, :]
            mn = ADAM_B1 * m_ref[0, rows, :] + (1.0 - ADAM_B1) * gv
            vn = ADAM_B2 * v_ref[0, rows, :] + (1.0 - ADAM_B2) * (gv * gv)
            m_hat = mn / (1.0 - ADAM_B1 ** ADAM_STEP)
            v_hat = vn / (1.0 - ADAM_B2 ** ADAM_STEP)
            d_ref[0, rows, :] = -ADAM_LR * (m_hat / (jnp.sqrt(v_hat) + ADAM_EPS) + ADAM_WD * w_ref[0, rows, :])
            nm_ref[0, rows, :] = mn
            nv_ref[0, rows, :] = vn
            return carry

        if R % tr == 0:
            lax.fori_loop(0, tr // rs, strip, 0, unroll=min(2, tr // rs))
        else:
            lax.fori_loop(0, jnp.minimum(tr, R - pl.program_id(0) * tr) // rs, strip, 0)

    blk, grid = pl.BlockSpec((1, tr, C), lambda i: (0, i, 0)), (-(-R // tr),)
    o = jax.ShapeDtypeStruct((1, R, C), f32)
    return pl.pallas_call(body, name=name, grid=grid, in_specs=[blk] * 4, out_specs=[blk] * 3, out_shape=[o, o, o],
                          compiler_params=_params(("parallel",)))(w, g, m, v)


def _sum8(t):
    n = t.shape[1]

    def body(t_ref, o_ref):
        acc = t_ref[pl.ds(0, 1), :]
        for r in range(1, 8):
            acc = acc + t_ref[pl.ds(r, 1), :]
        o_ref[...] = acc

    return pl.pallas_call(body, name="sum_devices", out_shape=jax.ShapeDtypeStruct((1, n), f32))(t)


def _reduce_start(g_main, g_dt, g_b):
    hr = g_main.shape[0] // 2
    halves = lambda t: t.reshape((-1, 2, hr, t.shape[-1]))
    g_main, g_dt, g_b = halves(g_main), halves(g_dt), halves(g_b)
    r_main, r_dt, r_b = _sibling_swap([g_main, g_dt, g_b], "sibling_swap")
    p_main = _pair_sum(g_main, r_main, "pair_sum_main", SUM_ROWS // 4)
    p_dt = _pair_sum(g_dt, r_dt, "pair_sum_dt", SUM_ROWS)
    p_b = _pair_sum(g_b, r_b, "pair_sum_b", SUM_ROWS)
    p_in = jnp.transpose(_w_in_grad_full(p_main[0], p_dt[0]).reshape(hr, N_SHARD, W_IN_SHARD), (1, 0, 2))
    return _chip_exchange_start([p_in, p_b])


def _reduce_finish(state, after):
    send_sems, recv_sems, srcs, lands, _ = state
    (p_in, p_b), (got_in, got_b) = _chip_exchange_wait(send_sems, recv_sems, srcs, lands, after)
    mine_in, mine_b = _own_sum(p_in, got_in, "own_sum_in", transposed=True), _own_sum(p_b, got_b, "own_sum_b")
    full_in_t, full_b = _share_halves([mine_in, mine_b], [True, False], "share_halves")
    return full_in_t, full_b.reshape(-1, full_b.shape[-1])


def kernel(x, positions, norm1_w, w_in, conv_w, conv_b, dt_bias, a_log, d_skip, ssd_norm_w, w_br_ret, w_br_ssd, w_out, norm_f_w, loss_target, m_norm1_w, m_w_in, m_conv_w, m_conv_b, m_dt_bias, m_a_log, m_d_skip, m_ssd_norm_w, m_w_br_ret, m_w_br_ssd, m_w_out, m_norm_f_w, v_norm1_w, v_w_in, v_conv_w, v_conv_b, v_dt_bias, v_a_log, v_d_skip, v_ssd_norm_w, v_w_br_ret, v_w_br_ssd, v_w_out, v_norm_f_w):
    D = D_MODEL
    xi, yi, ci = _me()
    k = 2 * xi + yi
    me = 2 * k + ci
    weights = dict(norm1_w=norm1_w, w_in=w_in, conv_w=conv_w, conv_b=conv_b, dt_bias=dt_bias, a_log=a_log, d_skip=d_skip,
                   ssd_norm_w=ssd_norm_w, w_br_ret=w_br_ret, w_br_ssd=w_br_ssd, w_out=w_out, norm_f_w=norm_f_w)
    mom1 = dict(norm1_w=m_norm1_w, w_in=m_w_in, conv_w=m_conv_w, conv_b=m_conv_b, dt_bias=m_dt_bias, a_log=m_a_log, d_skip=m_d_skip,
                ssd_norm_w=m_ssd_norm_w, w_br_ret=m_w_br_ret, w_br_ssd=m_w_br_ssd, w_out=m_w_out, norm_f_w=m_norm_f_w)
    mom2 = dict(norm1_w=v_norm1_w, w_in=v_w_in, conv_w=v_conv_w, conv_b=v_conv_b, dt_bias=v_dt_bias, a_log=v_a_log, d_skip=v_d_skip,
                ssd_norm_w=v_ssd_norm_w, w_br_ret=v_w_br_ret, w_br_ssd=v_w_br_ssd, w_out=v_w_out, norm_f_w=v_norm_f_w)

    a_sh = w_in[0].astype(bf16)
    b_sh = jnp.concatenate([w_br_ret[0], w_br_ssd[0], w_out[0]], axis=0).astype(bf16)
    ga, gc = _gather_weights(a_sh, conv_w[0])
    ga, b_late = lax.optimization_barrier((ga, b_sh))
    late_send, late_recv, late_src, late_land, late_token = _gather_late_start(b_late)
    own = lambda g, s: lax.dynamic_update_slice_in_dim(g, s[None], k, axis=0)
    ga, gc = own(ga, a_sh), own(gc, conv_w[0])
    w_main, w_dt = _split_w_in(jnp.transpose(ga, (1, 0, 2)).reshape(D, IN_PROJ))
    conv_full = jnp.transpose(gc, (1, 0, 2)).reshape(SSD_CONV, CONV_DIM)

    def late_weights(after):
        gb = own(_gather_late_wait(late_send, late_recv, late_src, late_land, after), b_sh)
        return gb[:, 0:512].reshape(2048, D), gb[:, 512:1536].reshape(4096, D), gb[:, 1536:2048].reshape(2048, D)

    def start_reduce(g_main, g_dt, g_w_br, g_w_bs, g_w_o):
        g_b = jnp.concatenate([g_w_br.reshape(N_SHARD, 512, D), g_w_bs.reshape(N_SHARD, 1024, D), g_w_o.reshape(N_SHARD, 512, D)],
                              axis=1)
        return _reduce_start(g_main, g_dt, g_b)

    loss, grad_x, g, reduce_state = _local_step(x[0], positions[0], loss_target[0], norm1_w, w_main, w_dt, conv_full, conv_b, dt_bias,
                                                a_log, d_skip, ssd_norm_w, late_token, late_weights, norm_f_w, start_reduce)

    grad_w_in_t, full_b = _reduce_finish(reduce_state, g["norm1_w"])
    grad_mats = dict(w_br_ret=full_b[0:512], w_br_ssd=full_b[512:1536], w_out=full_b[1536:2048])

    small = [(n, weights[n].size) for n in ("norm1_w", "conv_b", "dt_bias", "a_log", "d_skip", "ssd_norm_w", "norm_f_w")]
    parts = [jnp.pad(loss.reshape(1, 1), ((0, 0), (0, 127)))] + [g[n].reshape(1, -1) for n, _ in small] + [g["conv_w"].reshape(1, -1)]
    vec = jnp.concatenate(parts, axis=1)
    nv = vec.shape[1]
    nvp = -(-nv // 128) * 128
    vec = jnp.pad(vec, ((0, 0), (0, nvp - nv)))
    total = _sum8(lax.dynamic_update_slice_in_dim(_gather_vec(vec), vec, me, axis=0))
    loss_out = total[0, 0]
    off = 128
    grad_small = {}
    for n, sz in small:
        grad_small[n] = total[:, off:off + sz]
        off += sz
    g_conv = total[:, off:off + SSD_CONV * CONV_DIM].reshape(SSD_CONV, CONV_DIM)
    g_conv = lax.dynamic_slice_in_dim(g_conv, k * (CONV_DIM // N_SHARD), CONV_DIM // N_SHARD, axis=1)
    grad_small["conv_w"] = g_conv.reshape(1, -1)

    upd = {}
    for n in ("w_br_ret", "w_br_ssd", "w_out"):
        upd[n] = _adamw(weights[n], grad_mats[n][None], mom1[n], mom2[n], "adamw_" + n, tr=SUM_ROWS)
    tp = lambda t: jnp.swapaxes(t, 1, 2)
    upd["w_in"] = tuple(tp(t) for t in _adamw(tp(w_in), grad_w_in_t[None], tp(m_w_in), tp(v_w_in), "adamw_w_in", tr=256))
    grad_mats["w_in"] = tp(grad_w_in_t[None])
    names_small = [n for n, _ in small] + ["conv_w"]
    flat = lambda d: jnp.concatenate([d[n].reshape(1, -1) for n in names_small], axis=1)
    ns = sum(weights[n].size for n in names_small)
    nsp = -(-ns // 128) * 128
    padv = lambda t: jnp.pad(t, ((0, 0), (0, nsp - ns)))
    small_upd = _adamw(padv(flat(weights))[None], padv(flat(grad_small))[None], padv(flat(mom1))[None],
                       jnp.pad(flat(mom2), ((0, 0), (0, nsp - ns)), constant_values=1.0)[None], "adamw_small", 1)
    off = 0
    for n in names_small:
        sz = weights[n].size
        upd[n] = tuple(t[0, :, off:off + sz] for t in small_upd)
        off += sz

    order = ["norm1_w", "w_in", "conv_w", "conv_b", "dt_bias", "a_log", "d_skip", "ssd_norm_w", "w_br_ret", "w_br_ssd", "w_out", "norm_f_w"]
    grads_out = {**grad_mats, **grad_small}
    shp = lambda n, t: t.reshape(weights[n].shape)
    return (loss_out, grad_x[None], *[shp(n, grads_out[n]) for n in order], *[shp(n, upd[n][0]) for n in order],
            *[shp(n, upd[n][1]) for n in order], *[shp(n, upd[n][2]) for n in order])
```

```python
import jax
import jax.numpy as jnp
import numpy as np
from jax import lax
from jax.experimental import pallas as pl
from jax.experimental.pallas import tpu as pltpu

f32 = jnp.float32
bf16 = jnp.bfloat16
HIGHEST = lax.Precision.HIGHEST
MESH = pl.DeviceIdType.MESH

D_MODEL = 2048
EPS = 1e-6
CHUNK = 64
RET_HEADS = 8
RET_DK = 256
RET_HW = 4 * RET_DK
ROPE_THETA = 10000.0
SSD_WIDTH = 4096
SSD_GROUPS = 8
SSD_STATE = 128
SSD_GW = 512
SSD_GC = SSD_GW + 2 * SSD_STATE
SSD_HPG = 8
SSD_CONV = 4
CONV_DIM = 6144
SSD_HEADS = 64
LS = 128

C_RET, C_Z, C_GATES, C_XBC = 0, 8192, 12288, 16384
N_MAIN = 22528
DT_OFF = 18432
IN_PROJ = 22592
N_SHARD = 4
W_IN_SHARD = IN_PROJ // N_SHARD

ADAM_LR, ADAM_B1, ADAM_B2, ADAM_EPS, ADAM_WD, ADAM_STEP = 0.001, 0.9, 0.999, 1e-08, 0.01, 10

VMEM_LIMIT = 56 * 1024 * 1024
SUM_ROWS = 128
ANY = pl.BlockSpec(memory_space=pl.ANY)


def _params(dims):
    return pltpu.CompilerParams(dimension_semantics=dims, vmem_limit_bytes=VMEM_LIMIT)


def _silu(x):
    return x * jax.nn.sigmoid(x)


def _dsilu(x):
    s = jax.nn.sigmoid(x)
    return s * (1.0 + x * (1.0 - s))


def _nt(a, b):
    return lax.dot_general(a, b, (((1,), (1,)), ((), ())), preferred_element_type=f32)


def _tn(a, b):
    return lax.dot_general(a, b, (((0,), (0,)), ((), ())), preferred_element_type=f32)


def _nn(a, b):
    return jnp.dot(a, b, preferred_element_type=f32)


def _hi(a, b):
    return jnp.dot(a, b, precision=HIGHEST, preferred_element_type=f32)


def _split(a):
    hi = a.astype(bf16)
    return hi, (a - hi.astype(f32)).astype(bf16)


def _sel_r(a, sel):
    hi, lo = _split(a)
    return _nn(hi, sel) + _nn(lo, sel)


def _sel_l(sel, a):
    hi, lo = _split(a)
    return _nn(sel, hi) + _nn(sel, lo)


def _rows_to_cols(t, eye):
    hi = t.astype(bf16)
    r1 = t - hi.astype(f32)
    mid = r1.astype(bf16)
    lo = (r1 - mid.astype(f32)).astype(bf16)
    return _nt(eye, hi) + _nt(eye, mid) + _nt(eye, lo)


def _xbc_group_major(t):
    R = t.shape[0]
    nb = SSD_GROUPS * SSD_STATE
    parts = [t[:, :SSD_WIDTH].reshape(R, SSD_GROUPS, SSD_GW), t[:, SSD_WIDTH:SSD_WIDTH + nb].reshape(R, SSD_GROUPS, SSD_STATE),
             t[:, SSD_WIDTH + nb:].reshape(R, SSD_GROUPS, SSD_STATE)]
    return jnp.concatenate(parts, axis=2).reshape(R, CONV_DIM)


def _xbc_original(t):
    R = t.shape[0]
    g = t.reshape(R, SSD_GROUPS, SSD_GC)
    parts = [g[:, :, :SSD_GW].reshape(R, SSD_WIDTH), g[:, :, SSD_GW:SSD_GW + SSD_STATE].reshape(R, SSD_GROUPS * SSD_STATE),
             g[:, :, SSD_GW + SSD_STATE:].reshape(R, SSD_GROUPS * SSD_STATE)]
    return jnp.concatenate(parts, axis=1)


def _split_w_in(w):
    D = w.shape[0]
    ret = jnp.transpose(w[:, :4 * 2048].reshape(D, 4, RET_HEADS, RET_DK), (0, 2, 1, 3)).reshape(D, 4 * 2048)
    w_dt = jnp.pad(w[:, DT_OFF:DT_OFF + SSD_HEADS], ((0, 0), (0, 128 - SSD_HEADS)))
    main = jnp.concatenate([ret, w[:, 8192:12288], w[:, DT_OFF + SSD_HEADS:], _xbc_group_major(w[:, 12288:DT_OFF])], axis=1)
    return main, w_dt


def _w_in_grad_full(g_main, g_dt):
    D = g_main.shape[0]
    ret = jnp.transpose(g_main[:, :C_Z].reshape(D, RET_HEADS, 4, RET_DK), (0, 2, 1, 3)).reshape(D, C_Z)
    return jnp.concatenate([ret, g_main[:, C_Z:C_GATES], _xbc_original(g_main[:, C_XBC:]), g_dt[:, :SSD_HEADS],
                            g_main[:, C_GATES:C_XBC]], axis=1)


def _mm(pairs, M, N, *, tm, tn, out_dtype, name, tb=False):
    P = len(pairs)
    nks = [K // tk for (_, _, _, _, _, K, tk) in pairs]
    starts = [int(s) for s in np.cumsum([0] + nks[:-1])]
    KT = int(sum(nks))
    in_specs, args = [], []
    for (a, a_cb, b, b_kb, b_nb, K, tk), s, nk in zip(pairs, starts, nks):
        def kk(k, s=s, nk=nk):
            return jnp.clip(k - s, 0, nk - 1)
        in_specs.append(pl.BlockSpec((tm, tk), lambda m, n, k, kk=kk, a_cb=a_cb: (m, a_cb + kk(k))))
        if tb:
            in_specs.append(pl.BlockSpec((tn, tk), lambda m, n, k, kk=kk, b_kb=b_kb, b_nb=b_nb: (b_nb + n, b_kb + kk(k))))
        else:
            in_specs.append(pl.BlockSpec((tk, tn), lambda m, n, k, kk=kk, b_kb=b_kb, b_nb=b_nb: (b_kb + kk(k), b_nb + n)))
        args += [a, b]

    def body(*refs):
        o_ref = refs[2 * P]
        k = pl.program_id(2)

        def prod(i):
            a = refs[2 * i][...].astype(bf16)
            b = refs[2 * i + 1][...].astype(bf16)
            return _nt(a, b) if tb else _nn(a, b)

        if KT == 1:
            o_ref[...] = prod(0).astype(out_dtype)
            return
        acc = refs[2 * P + 1]

        @pl.when(k == 0)
        def _():
            acc[...] = jnp.zeros_like(acc)

        for i in range(P):
            @pl.when((k >= starts[i]) & (k < starts[i] + nks[i]))
            def _(i=i):
                acc[...] += prod(i)

        @pl.when(k == KT - 1)
        def _():
            o_ref[...] = acc[...].astype(out_dtype)

    return pl.pallas_call(
        body, name=name, grid=(M // tm, N // tn, KT), in_specs=in_specs,
        out_specs=pl.BlockSpec((tm, tn), lambda m, n, k: (m, n)),
        out_shape=jax.ShapeDtypeStruct((M, N), out_dtype),
        scratch_shapes=[] if KT == 1 else [pltpu.VMEM((tm, tn), f32)],
        compiler_params=_params(("parallel", "parallel", "arbitrary")),
    )(*args)


def _mm1(a, b, *, tm, tn, tk, out_dtype, name, tb=False):
    M, K = a.shape
    N = b.shape[0] if tb else b.shape[1]
    return _mm([(a, 0, b, 0, 0, K, tk)], M, N, tm=tm, tn=tn, out_dtype=out_dtype, name=name, tb=tb)


RS = 16
CS = 32


def _for_strips(n_rows, fn, rs=RS, unroll=4):
    def step(s, carry):
        fn(pl.ds(pl.multiple_of(s * rs, rs), rs))
        return carry
    n = n_rows // rs
    lax.fori_loop(0, n, step, 0, unroll=min(unroll, n))


def _norm1_fwd(x, w, tr):
    S, D = x.shape

    def body(x_ref, w_ref, h_ref, ht_ref):
        def strip(rows):
            xv = x_ref[rows, :]
            r = lax.rsqrt(jnp.mean(xv * xv, axis=-1, keepdims=True) + EPS)
            h_ref[rows, :] = (xv * r * w_ref[...]).astype(bf16)
        _for_strips(tr, strip)
        ht_ref[...] = h_ref[...].T

    return pl.pallas_call(
        body, name="norm1_fwd", grid=(S // tr,),
        in_specs=[pl.BlockSpec((tr, D), lambda i: (i, 0)), pl.BlockSpec((1, D), lambda i: (0, 0))],
        out_specs=[pl.BlockSpec((tr, D), lambda i: (i, 0)), pl.BlockSpec((D, tr), lambda i: (0, i))],
        out_shape=[jax.ShapeDtypeStruct((S, D), bf16), jax.ShapeDtypeStruct((D, S), bf16)], compiler_params=_params(("parallel",)),
    )(x, w)


def _norm1_bwd(x, w, dh, dx2, tr):
    S, D = x.shape

    def body(x_ref, w_ref, dh_ref, dx2_ref, gx_ref, gw_ref, acc):
        @pl.when(pl.program_id(0) == 0)
        def _():
            acc[...] = jnp.zeros_like(acc)

        def strip(rows):
            xv = x_ref[rows, :]
            r = lax.rsqrt(jnp.mean(xv * xv, axis=-1, keepdims=True) + EPS)
            xh = xv * r
            dhv = dh_ref[rows, :]
            acc[...] += dhv * xh
            dxh = dhv * w_ref[...]
            gx_ref[rows, :] = dx2_ref[rows, :] + r * (dxh - xh * jnp.mean(dxh * xh, axis=-1, keepdims=True))
        _for_strips(tr, strip)

        @pl.when(pl.program_id(0) == S // tr - 1)
        def _():
            gw_ref[...] = jnp.sum(acc[...], axis=0, keepdims=True)

    row = pl.BlockSpec((tr, D), lambda i: (i, 0))
    vec = pl.BlockSpec((1, D), lambda i: (0, 0))
    return pl.pallas_call(
        body, name="norm1_bwd", grid=(S // tr,), in_specs=[row, vec, row, row], out_specs=[row, vec],
        out_shape=[jax.ShapeDtypeStruct((S, D), f32), jax.ShapeDtypeStruct((1, D), f32)],
        scratch_shapes=[pltpu.VMEM((RS, D), f32)], compiler_params=_params(("arbitrary",)),
    )(x, w, dh, dx2)


def _final_fwd_bwd(x, mo, target, wf, tr):
    S, D = x.shape

    def body(x_ref, mo_ref, t_ref, w_ref, dx2_ref, dx2b_ref, loss_ref, gw_ref, acc, lacc):
        @pl.when(pl.program_id(0) == 0)
        def _():
            acc[...] = jnp.zeros_like(acc)
            lacc[...] = jnp.zeros_like(lacc)

        def strip(rows):
            x2 = x_ref[rows, :] + mo_ref[rows, :]
            r = lax.rsqrt(jnp.mean(x2 * x2, axis=-1, keepdims=True) + EPS)
            xh = x2 * r
            wv = w_ref[...]
            err = xh * wv - t_ref[rows, :]
            lacc[...] += jnp.mean(err * err, axis=-1, keepdims=True)
            dy = err * (1.0 / D)
            acc[...] += dy * xh
            dxh = dy * wv
            dx2 = r * (dxh - xh * jnp.mean(dxh * xh, axis=-1, keepdims=True))
            dx2_ref[rows, :] = dx2
            dx2b_ref[rows, :] = dx2.astype(bf16)
        _for_strips(tr, strip)

        @pl.when(pl.program_id(0) == S // tr - 1)
        def _():
            gw_ref[...] = jnp.sum(acc[...], axis=0, keepdims=True)
            loss_ref[...] = 0.5 * jnp.sum(lacc[...], axis=0, keepdims=True)

    row = pl.BlockSpec((tr, D), lambda i: (i, 0))
    vec = pl.BlockSpec((1, D), lambda i: (0, 0))
    return pl.pallas_call(
        body, name="final_norm_loss", grid=(S // tr,), in_specs=[row, row, row, vec],
        out_specs=[row, row, pl.BlockSpec((1, 1), lambda i: (0, 0)), vec],
        out_shape=[jax.ShapeDtypeStruct((S, D), f32), jax.ShapeDtypeStruct((S, D), bf16), jax.ShapeDtypeStruct((1, 1), f32),
                   jax.ShapeDtypeStruct((1, D), f32)],
        scratch_shapes=[pltpu.VMEM((RS, D), f32), pltpu.VMEM((RS, 1), f32)], compiler_params=_params(("arbitrary",)),
    )(x, mo, target, wf)


def _merge_fwd(p_r, p_s, proj, tr):
    S, D = p_r.shape

    def body(pr_ref, ps_ref, g_ref, o_ref, ot_ref):
        def strip(rows):
            gr, gs = g_ref[rows, pl.ds(0, D)], g_ref[rows, pl.ds(D, D)]
            o_ref[rows, :] = (jax.nn.sigmoid(gr) * pr_ref[rows, :] + jax.nn.sigmoid(gs) * ps_ref[rows, :]).astype(bf16)
        _for_strips(tr, strip)
        ot_ref[...] = o_ref[...].T

    row = pl.BlockSpec((tr, D), lambda i: (i, 0))
    return pl.pallas_call(
        body, name="merge_fwd", grid=(S // tr,),
        in_specs=[row, row, pl.BlockSpec((tr, 2 * D), lambda i: (i, C_GATES // (2 * D)))],
        out_specs=[row, pl.BlockSpec((D, tr), lambda i: (0, i))],
        out_shape=[jax.ShapeDtypeStruct((S, D), bf16), jax.ShapeDtypeStruct((D, S), bf16)], compiler_params=_params(("parallel",)),
    )(p_r, p_s, proj)


def _merge_bwd(dm, p_r, p_s, proj, tr):
    S, D = p_r.shape

    def body(dm_ref, pr_ref, ps_ref, g_ref, dpr_ref, dps_ref, dproj_ref):
        def strip(rows):
            dmv = dm_ref[rows, :]
            sr = jax.nn.sigmoid(g_ref[rows, pl.ds(0, D)])
            ss = jax.nn.sigmoid(g_ref[rows, pl.ds(D, D)])
            dpr_ref[rows, :] = (dmv * sr).astype(bf16)
            dps_ref[rows, :] = (dmv * ss).astype(bf16)
            dproj_ref[rows, pl.ds(0, D)] = (dmv * pr_ref[rows, :] * sr * (1.0 - sr)).astype(bf16)
            dproj_ref[rows, pl.ds(D, D)] = (dmv * ps_ref[rows, :] * ss * (1.0 - ss)).astype(bf16)
        _for_strips(tr, strip)

    row = pl.BlockSpec((tr, D), lambda i: (i, 0))
    gates = pl.BlockSpec((tr, 2 * D), lambda i: (i, C_GATES // (2 * D)))
    o = jax.ShapeDtypeStruct((S, D), bf16)
    return pl.pallas_call(
        body, name="merge_bwd", grid=(S // tr,), in_specs=[row, row, row, gates],
        out_specs=[row, row, gates], out_shape=[o, o, jax.ShapeDtypeStruct((S, N_MAIN), bf16)],
        compiler_params=_params(("parallel",)),
    )(dm, p_r, p_s, proj)


def _ssd_norm_fwd(y, proj, w, tr):
    S, W = y.shape

    def body(y_ref, z_ref, w_ref, o_ref, ot_ref):
        def strip(rows):
            u = y_ref[rows, :] * _silu(z_ref[rows, :])
            r = lax.rsqrt(jnp.mean(u * u, axis=-1, keepdims=True) + EPS)
            o_ref[rows, :] = (u * r * w_ref[...]).astype(bf16)
        _for_strips(tr, strip)
        ot_ref[...] = o_ref[...].T

    row = pl.BlockSpec((tr, W), lambda i: (i, 0))
    return pl.pallas_call(
        body, name="ssd_norm_fwd", grid=(S // tr,),
        in_specs=[row, pl.BlockSpec((tr, W), lambda i: (i, C_Z // W)), pl.BlockSpec((1, W), lambda i: (0, 0))],
        out_specs=[row, pl.BlockSpec((W, tr), lambda i: (0, i))],
        out_shape=[jax.ShapeDtypeStruct((S, W), bf16), jax.ShapeDtypeStruct((W, S), bf16)], compiler_params=_params(("parallel",)),
    )(y, proj, w)


def _ssd_norm_bwd(y, proj, w, dys, dproj, tr):
    S, W = y.shape

    def body(y_ref, z_ref, w_ref, d_ref, _, dy_ref, dz_ref, gw_ref, acc):
        @pl.when(pl.program_id(0) == 0)
        def _():
            acc[...] = jnp.zeros_like(acc)

        def strip(rows):
            yv, zv, dv = y_ref[rows, :], z_ref[rows, :], d_ref[rows, :]
            sz = _silu(zv)
            u = yv * sz
            r = lax.rsqrt(jnp.mean(u * u, axis=-1, keepdims=True) + EPS)
            un = u * r
            acc[...] += dv * un
            dun = dv * w_ref[...]
            du = r * (dun - un * jnp.mean(dun * un, axis=-1, keepdims=True))
            dy_ref[rows, :] = du * sz
            dz_ref[rows, :] = (du * yv * _dsilu(zv)).astype(bf16)
        _for_strips(tr, strip)

        @pl.when(pl.program_id(0) == S // tr - 1)
        def _():
            gw_ref[...] = jnp.sum(acc[...], axis=0, keepdims=True)

    row = pl.BlockSpec((tr, W), lambda i: (i, 0))
    zcol = pl.BlockSpec((tr, W), lambda i: (i, C_Z // W))
    vec = pl.BlockSpec((1, W), lambda i: (0, 0))
    return pl.pallas_call(
        body, name="ssd_norm_bwd", grid=(S // tr,),
        in_specs=[row, zcol, vec, row, ANY], out_specs=[row, zcol, vec],
        out_shape=[jax.ShapeDtypeStruct((S, W), f32), jax.ShapeDtypeStruct(dproj.shape, bf16), jax.ShapeDtypeStruct((1, W), f32)],
        input_output_aliases={4: 1}, scratch_shapes=[pltpu.VMEM((RS, W), f32)], compiler_params=_params(("arbitrary",)),
    )(y, proj, w, dys, dproj)


def _rope(t, cos, sin):
    t1, t2 = t[:, :128], t[:, 128:]
    return jnp.concatenate([t1 * cos - t2 * sin, t2 * cos + t1 * sin], axis=1)


def _rope_t(d, cos, sin):
    d1, d2 = d[:, :128], d[:, 128:]
    return jnp.concatenate([d1 * cos + d2 * sin, d2 * cos - d1 * sin], axis=1)


def _ret_specs(tb, rev_nb=None):
    def blk(i):
        return i if rev_nb is None else rev_nb - 1 - i
    head = pl.BlockSpec((tb, RET_HW), lambda h, i: (blk(i), h))
    tab = pl.BlockSpec((tb, 128), lambda h, i: (blk(i), 0))
    mat = pl.BlockSpec((1, CHUNK, CHUNK), lambda h, i: (h, 0, 0))
    vec = pl.BlockSpec((1, CHUNK, 1), lambda h, i: (h, 0, 0))
    one = pl.BlockSpec((1, 1, 1), lambda h, i: (h, 0, 0))
    own = pl.BlockSpec((tb, RET_DK), lambda h, i: (blk(i), h))
    st = pl.BlockSpec((1, tb // CHUNK, RET_DK, RET_DK), lambda h, i: (h, blk(i), 0, 0))
    return head, tab, mat, vec, one, own, st


def _ret_fwd(proj, cos, sin, intra, qdec, kdec, cdec, tb):
    S = proj.shape[0]
    nc = S // CHUNK
    scale = RET_DK ** -0.5
    dk = RET_DK

    def body(p_ref, cos_ref, sin_ref, m_ref, qd_ref, kd_ref, cd_ref, y_ref, yr_ref, yrt_ref, st_ref, st):
        @pl.when(pl.program_id(1) == 0)
        def _():
            st[...] = jnp.zeros_like(st)

        mm, qd, kd, cd = m_ref[0], qd_ref[0], kd_ref[0], cd_ref[0]

        def chunk(c, carry):
            rows = pl.ds(pl.multiple_of(c * CHUNK, CHUNK), CHUNK)
            cs, sn = cos_ref[rows, :], sin_ref[rows, :]
            qr = _rope(p_ref[rows, pl.ds(0, dk)], cs, sn)
            kr = _rope(p_ref[rows, pl.ds(dk, dk)], cs, sn) * scale
            qb, kb, vb = qr.astype(bf16), kr.astype(bf16), p_ref[rows, pl.ds(2 * dk, dk)].astype(bf16)
            stb = st[...].astype(bf16)
            st_ref[0, c] = stb
            sc = (_nt(qb, kb) * mm).astype(bf16)
            y = _nn(sc, vb) + _nn(qb, stb) * qd
            st[...] = st[...] * cd + _tn((kr * kd).astype(bf16), vb)
            y_ref[rows, :] = y
            mu = jnp.mean(y, axis=-1, keepdims=True)
            yc = y - mu
            var = jnp.mean(yc * yc, axis=-1, keepdims=True)
            yr_ref[rows, :] = (yc * lax.rsqrt(var + EPS) * _silu(p_ref[rows, pl.ds(3 * dk, dk)])).astype(bf16)
            return carry

        lax.fori_loop(0, tb // CHUNK, chunk, 0, unroll=min(4, tb // CHUNK))
        yrt_ref[...] = yr_ref[...].T

    head, tab, mat, vec, one, own, stspec = _ret_specs(tb)
    return pl.pallas_call(
        body, name="ret_fwd", grid=(RET_HEADS, S // tb),
        in_specs=[head, tab, tab, mat, vec, vec, one],
        out_specs=[own, own, pl.BlockSpec((RET_DK, tb), lambda h, i: (h, i)), stspec],
        out_shape=[jax.ShapeDtypeStruct((S, 2048), f32), jax.ShapeDtypeStruct((S, 2048), bf16), jax.ShapeDtypeStruct((2048, S), bf16),
                   jax.ShapeDtypeStruct((RET_HEADS, nc, dk, dk), bf16)],
        scratch_shapes=[pltpu.VMEM((dk, dk), f32)], compiler_params=_params(("parallel", "arbitrary")),
    )(proj, cos, sin, intra, qdec, kdec, cdec)


def _ret_bwd(proj, cos, sin, intra, qdec, kdec, cdec, y, dyr, states, dproj, tb):
    S = proj.shape[0]
    nb = S // tb
    nck = tb // CHUNK
    scale = RET_DK ** -0.5
    dk = RET_DK

    def body(p_ref, cos_ref, sin_ref, m_ref, qd_ref, kd_ref, cd_ref, y_ref, dyr_ref, st_ref, _, o_ref, dst):
        @pl.when(pl.program_id(1) == 0)
        def _():
            dst[...] = jnp.zeros_like(dst)

        mm, qd, kd, cd = m_ref[0], qd_ref[0], kd_ref[0], cd_ref[0]

        def chunk(cc, carry):
            c = nck - 1 - cc
            rows = pl.ds(pl.multiple_of(c * CHUNK, CHUNK), CHUNK)
            cs, sn = cos_ref[rows, :], sin_ref[rows, :]
            qr = _rope(p_ref[rows, pl.ds(0, dk)], cs, sn)
            kr = _rope(p_ref[rows, pl.ds(dk, dk)], cs, sn) * scale
            qb, kb, vb = qr.astype(bf16), kr.astype(bf16), p_ref[rows, pl.ds(2 * dk, dk)].astype(bf16)
            kdb = (kr * kd).astype(bf16)
            stb = st_ref[0, c]
            yv, gv, dyrv = y_ref[rows, :], p_ref[rows, pl.ds(3 * dk, dk)], dyr_ref[rows, :]
            mu = jnp.mean(yv, axis=-1, keepdims=True)
            yc = yv - mu
            rstd = lax.rsqrt(jnp.mean(yc * yc, axis=-1, keepdims=True) + EPS)
            yn = yc * rstd
            o_ref[rows, pl.ds(3 * dk, dk)] = (dyrv * yn * _dsilu(gv)).astype(bf16)
            dyn = dyrv * _silu(gv)
            dy = rstd * (dyn - jnp.mean(dyn, axis=-1, keepdims=True) - yn * jnp.mean(dyn * yn, axis=-1, keepdims=True))
            dyb = dy.astype(bf16)
            dyqb = (dy * qd).astype(bf16)
            dstb = dst[...].astype(bf16)
            sct = (_nt(kb, qb) * mm).astype(bf16)
            ds = (_nt(dyb, vb) * mm).astype(bf16)
            dsT = (_nt(vb, dyb) * mm).astype(bf16)
            dv = _nn(sct, dyb) + _nn(kdb, dstb)
            dqr = _nn(ds, kb) + _nt(dyqb, stb)
            dkr = _nn(dsT, qb) + _nt(vb, dstb) * kd
            dst[...] = dst[...] * cd + _tn(qb, dyqb)
            o_ref[rows, pl.ds(0, dk)] = _rope_t(dqr, cs, sn).astype(bf16)
            o_ref[rows, pl.ds(dk, dk)] = (_rope_t(dkr, cs, sn) * scale).astype(bf16)
            o_ref[rows, pl.ds(2 * dk, dk)] = dv.astype(bf16)
            return carry

        lax.fori_loop(0, nck, chunk, 0, unroll=min(4, nck))

    head, tab, mat, vec, one, own, stspec = _ret_specs(tb, rev_nb=nb)
    return pl.pallas_call(
        body, name="ret_bwd", grid=(RET_HEADS, nb),
        in_specs=[head, tab, tab, mat, vec, vec, one, own, own, stspec, ANY],
        out_specs=head, out_shape=jax.ShapeDtypeStruct(dproj.shape, bf16), input_output_aliases={10: 0},
        scratch_shapes=[pltpu.VMEM((dk, dk), f32)], compiler_params=_params(("parallel", "arbitrary")),
    )(proj, cos, sin, intra, qdec, kdec, cdec, y, dyr, states, dproj)


def _conv_fwd(proj, conv_w, conv_b, tb, cw):
    S = proj.shape[0]
    off = C_XBC // cw

    def body(x_ref, halo_ref, w_ref, b_ref, o_ref, xe):
        xe[pl.ds(0, 8), :] = jnp.where(pl.program_id(1) == 0, 0.0, halo_ref[...])
        xe[pl.ds(8, CS), :] = x_ref[pl.ds(0, CS), :]
        ws = [w_ref[pl.ds(j, 1), :] for j in range(SSD_CONV)]
        for s in range(tb // CS):
            tap = (lambda j: xe[pl.ds(5 + j, CS), :]) if s == 0 else (lambda j, s=s: x_ref[pl.ds(s * CS - 3 + j, CS), :])
            acc = b_ref[...] + ws[0] * tap(0)
            for j in range(1, SSD_CONV):
                acc = acc + ws[j] * tap(j)
            o_ref[pl.ds(s * CS, CS), :] = acc

    return pl.pallas_call(
        body, name="conv_fwd", grid=(CONV_DIM // cw, S // tb),
        in_specs=[pl.BlockSpec((tb, cw), lambda j, i: (i, off + j)),
                  pl.BlockSpec((8, cw), lambda j, i: (jnp.maximum(i * (tb // 8) - 1, 0), off + j)),
                  pl.BlockSpec((SSD_CONV, cw), lambda j, i: (0, j)), pl.BlockSpec((1, cw), lambda j, i: (0, j))],
        out_specs=pl.BlockSpec((tb, cw), lambda j, i: (i, j)),
        out_shape=jax.ShapeDtypeStruct((S, CONV_DIM), f32),
        scratch_shapes=[pltpu.VMEM((CS + 8, cw), f32)], compiler_params=_params(("parallel", "arbitrary")),
    )(proj, proj, conv_w, conv_b)


def _conv_bwd(dpre, proj, conv_w, dproj, tb, cw):
    S, n = dpre.shape
    nb = S // tb
    xoff = C_XBC // cw

    def body(d_ref, dh_ref, x_ref, xh_ref, w_ref, _, dx_ref, gw_ref, gb_ref, de, xe, accw, accb):
        i = pl.program_id(1)

        @pl.when(i == 0)
        def _():
            accw[...] = jnp.zeros_like(accw)
            accb[...] = jnp.zeros_like(accb)

        ns = tb // CS
        de[pl.ds(0, CS), :] = d_ref[pl.ds(tb - CS, CS), :]
        de[pl.ds(CS, 8), :] = jnp.where(i == nb - 1, 0.0, dh_ref[...])
        xe[pl.ds(0, 8), :] = jnp.where(i == 0, 0.0, xh_ref[...])
        xe[pl.ds(8, CS), :] = x_ref[pl.ds(0, CS), :]
        ws = [w_ref[pl.ds(j, 1), :] for j in range(SSD_CONV)]
        fold = lambda p: sum(p[8 * q:8 * (q + 1)] for q in range(1, CS // 8)) + p[0:8]
        for s in range(ns):
            dv = d_ref[pl.ds(s * CS, CS), :]
            ahead = (lambda o: de[pl.ds(o, CS), :]) if s == ns - 1 else (lambda o, s=s: d_ref[pl.ds(s * CS + o, CS), :])
            xtap = (lambda j: xe[pl.ds(5 + j, CS), :]) if s == 0 else (lambda j, s=s: x_ref[pl.ds(s * CS - 3 + j, CS), :])
            acc = ws[SSD_CONV - 1] * dv
            for j in range(SSD_CONV - 1):
                acc = acc + ws[j] * ahead(3 - j)
            dx_ref[pl.ds(s * CS, CS), :] = acc.astype(bf16)
            accb[...] += fold(dv)
            for j in range(SSD_CONV):
                accw[j] += fold(dv * xtap(j))

        @pl.when(i == nb - 1)
        def _():
            gb_ref[...] = jnp.sum(accb[...], axis=0, keepdims=True)
            for j in range(SSD_CONV):
                gw_ref[pl.ds(j, 1), :] = jnp.sum(accw[j], axis=0, keepdims=True)

    return pl.pallas_call(
        body, name="conv_bwd", grid=(n // cw, nb),
        in_specs=[pl.BlockSpec((tb, cw), lambda j, i: (i, j)),
                  pl.BlockSpec((8, cw), lambda j, i: (jnp.minimum((i + 1) * (tb // 8), S // 8 - 1), j)),
                  pl.BlockSpec((tb, cw), lambda j, i: (i, xoff + j)),
                  pl.BlockSpec((8, cw), lambda j, i: (jnp.maximum(i * (tb // 8) - 1, 0), xoff + j)),
                  pl.BlockSpec((SSD_CONV, cw), lambda j, i: (0, j)), ANY],
        out_specs=[pl.BlockSpec((tb, cw), lambda j, i: (i, xoff + j)), pl.BlockSpec((SSD_CONV, cw), lambda j, i: (0, j)),
                   pl.BlockSpec((1, cw), lambda j, i: (0, j))],
        out_shape=[jax.ShapeDtypeStruct(dproj.shape, bf16), jax.ShapeDtypeStruct((SSD_CONV, n), f32), jax.ShapeDtypeStruct((1, n), f32)],
        input_output_aliases={5: 0},
        scratch_shapes=[pltpu.VMEM((CS + 8, cw), f32), pltpu.VMEM((CS + 8, cw), f32), pltpu.VMEM((SSD_CONV, 8, cw), f32),
                        pltpu.VMEM((8, cw), f32)],
        compiler_params=_params(("parallel", "arbitrary")),
    )(dpre, dpre, proj, proj, conv_w, dproj)


def _dt_prep(dt_raw, dt_bias, a_log, tb):
    S = dt_raw.shape[0]

    def body(r_ref, b_ref, al_ref, dt_ref, sg_ref, ac_ref):
        li = lax.broadcasted_iota(jnp.int32, (LS, LS), 0)
        si = lax.broadcasted_iota(jnp.int32, (LS, LS), 1)
        tri = (li >= si).astype(f32)
        neg_a = -jnp.exp(al_ref[...])
        for c in range(tb // LS):
            rows = pl.ds(c * LS, LS)
            xv = r_ref[rows, :] + b_ref[...]
            dtv = jax.nn.softplus(xv)
            dt_ref[rows, :] = dtv
            sg_ref[rows, :] = jax.nn.sigmoid(xv)
            ac_ref[rows, :] = _hi(tri, dtv * neg_a)

    row = pl.BlockSpec((tb, 128), lambda i: (i, 0))
    vec = pl.BlockSpec((1, 128), lambda i: (0, 0))
    o = jax.ShapeDtypeStruct((S, 128), f32)
    return pl.pallas_call(body, name="dt_prep", grid=(S // tb,), in_specs=[row, vec, vec], out_specs=[row, row, row],
                          out_shape=[o, o, o], compiler_params=_params(("parallel",)))(dt_raw, dt_bias, a_log)


def _group_major(t):
    S = t.shape[0]
    return jnp.transpose(t[:, :SSD_HEADS].reshape(S, SSD_GROUPS, SSD_HPG), (1, 0, 2))


def _group_major_t(t):
    S = t.shape[0]
    return jnp.transpose(t[:, :SSD_HEADS].reshape(S // LS, LS, SSD_GROUPS, SSD_HPG), (2, 0, 3, 1))


def _ssd_specs(tb, rev_nb=None):
    def blk(i):
        return i if rev_nb is None else rev_nb - 1 - i
    grp = pl.BlockSpec((tb, SSD_GC), lambda g, i: (blk(i), g))
    xs = pl.BlockSpec((tb, SSD_GW), lambda g, i: (blk(i), g))
    ph = pl.BlockSpec((1, tb, SSD_HPG), lambda g, i: (g, blk(i), 0))
    pht = pl.BlockSpec((1, tb // LS, SSD_HPG, LS), lambda g, i: (g, blk(i), 0, 0))
    gvec = pl.BlockSpec((1, 1, SSD_GW), lambda g, i: (g, 0, 0))
    ex = pl.BlockSpec((SSD_HPG, SSD_GW), lambda g, i: (0, 0))
    st = pl.BlockSpec((1, tb // LS, SSD_STATE, SSD_GW), lambda g, i: (g, blk(i), 0, 0))
    return grp, xs, ph, pht, gvec, ex, st


def _expander():
    return jnp.repeat(jnp.eye(SSD_HPG, dtype=f32), SSD_GW // SSD_HPG, axis=1).astype(bf16)


def _expand3(dt8, ac8, ex):
    stack = jnp.concatenate([dt8, jnp.exp(ac8), jnp.exp(ac8[LS - 1:LS, :] - ac8)], axis=0)
    wide = _sel_r(stack, ex)
    return wide[0:LS], wide[LS:2 * LS], wide[2 * LS:3 * LS]


def _ssd_fwd(pre, dt_g, ac_g, act_g, dskx, tb):
    S = pre.shape[0]
    nc = S // LS
    hd = SSD_GW // SSD_HPG

    def body(p_ref, dt_ref, ac_ref, act_ref, dsk_ref, ex_ref, y_ref, st_ref, st):
        @pl.when(pl.program_id(1) == 0)
        def _():
            st[...] = jnp.zeros_like(st)

        ex = ex_ref[...]
        li = lax.broadcasted_iota(jnp.int32, (LS, LS), 0)
        si = lax.broadcasted_iota(jnp.int32, (LS, LS), 1)
        causal = li >= si

        def chunk(c, carry):
            rows = pl.ds(pl.multiple_of(c * LS, LS), LS)
            xs = _silu(p_ref[rows, pl.ds(0, SSD_GW)])
            bcb = _silu(p_ref[rows, pl.ds(SSD_GW, SSD_STATE)]).astype(bf16)
            ccb = _silu(p_ref[rows, pl.ds(SSD_GW + SSD_STATE, SSD_STATE)]).astype(bf16)
            dt8, ac8, act = dt_ref[0, rows, :], ac_ref[0, rows, :], act_ref[0, c]
            dtx, eax, tailx = _expand3(dt8, ac8, ex)
            xdt = xs * dtx
            cb = _nt(ccb, bcb)
            stb = st[...].astype(bf16)
            st_ref[0, c] = stb
            xdtb = xdt.astype(bf16)
            outs = []
            for h in range(SSD_HPG):
                dec = jnp.exp(jnp.where(causal, ac8[:, h:h + 1] - act[h:h + 1, :], -1e30))
                outs.append(_nn((cb * dec).astype(bf16), xdtb[:, hd * h:hd * (h + 1)]))
            y_ref[rows, :] = jnp.concatenate(outs, axis=1) + _nn(ccb, stb) * eax + dsk_ref[0] * xs
            st[...] = st[...] * eax[LS - 1:LS, :] + _tn(bcb, (xdt * tailx).astype(bf16))
            return carry

        lax.fori_loop(0, tb // LS, chunk, 0, unroll=min(4, tb // LS))

    grp, xs, ph, pht, gvec, ex, stspec = _ssd_specs(tb)
    return pl.pallas_call(
        body, name="ssd_fwd", grid=(SSD_GROUPS, S // tb),
        in_specs=[grp, ph, ph, pht, gvec, ex], out_specs=[xs, stspec],
        out_shape=[jax.ShapeDtypeStruct((S, SSD_WIDTH), f32), jax.ShapeDtypeStruct((SSD_GROUPS, nc, SSD_STATE, SSD_GW), bf16)],
        scratch_shapes=[pltpu.VMEM((SSD_STATE, SSD_GW), f32)], compiler_params=_params(("parallel", "arbitrary")),
    )(pre, dt_g, ac_g, act_g, dskx, _expander())


def _ssd_bwd(pre, dt_g, ac_g, act_g, sg_g, dskx, nega_g, dy, states, tb):
    S = pre.shape[0]
    nb = S // tb
    nck = tb // LS
    hd = SSD_GW // SSD_HPG

    def body(p_ref, dt_ref, ac_ref, act_ref, sg_ref, dsk_ref, na_ref, ex_ref, ext_ref, dy_ref, st_ref,
             dp_ref, ddt_ref, gsk_ref, gal_ref, gdb_ref, dst, skacc):
        @pl.when(pl.program_id(1) == 0)
        def _():
            dst[...] = jnp.zeros_like(dst)
            skacc[...] = jnp.zeros_like(skacc)
            gal_ref[...] = jnp.zeros_like(gal_ref)
            gdb_ref[...] = jnp.zeros_like(gdb_ref)

        ex, ext = ex_ref[...], ext_ref[...]
        li = lax.broadcasted_iota(jnp.int32, (LS, LS), 0)
        si = lax.broadcasted_iota(jnp.int32, (LS, LS), 1)
        causal = li >= si
        anti = si >= li
        upper = anti.astype(bf16)
        eye = (si == li).astype(bf16)
        last_row = (lax.broadcasted_iota(jnp.int32, (LS, 1), 0) == LS - 1).astype(f32)
        head_id = lax.broadcasted_iota(jnp.int32, (1, SSD_HPG), 1)
        head_col = lax.broadcasted_iota(jnp.int32, (SSD_HPG, 1), 0)
        neg_a = na_ref[0]
        dskv = dsk_ref[0]

        def chunk(cc, carry):
            c = nck - 1 - cc
            rows = pl.ds(pl.multiple_of(c * LS, LS), LS)
            px = p_ref[rows, pl.ds(0, SSD_GW)]
            pb = p_ref[rows, pl.ds(SSD_GW, SSD_STATE)]
            pc = p_ref[rows, pl.ds(SSD_GW + SSD_STATE, SSD_STATE)]
            sgx, sgb, sgc = jax.nn.sigmoid(px), jax.nn.sigmoid(pb), jax.nn.sigmoid(pc)
            xs = px * sgx
            bcb = (pb * sgb).astype(bf16)
            ccb = (pc * sgc).astype(bf16)
            dt8, ac8, act = dt_ref[0, rows, :], ac_ref[0, rows, :], act_ref[0, c]
            dtx, eax, tailx = _expand3(dt8, ac8, ex)
            xdt = xs * dtx
            ex_last = eax[LS - 1:LS, :]
            stb = st_ref[0, c]
            dyv = dy_ref[rows, :]
            dyb = dyv.astype(bf16)
            xdtb = xdt.astype(bf16)
            skacc[...] += jnp.sum(dyv * xs, axis=0, keepdims=True)
            yinter = _nn(ccb, stb) * eax
            dzb = (dyv * eax).astype(bf16)
            dcc = _nt(dzb, stb)
            dstv = dst[...]
            dstb = dstv.astype(bf16)
            xt = xdt * tailx
            dxt = _nn(bcb, dstb)
            dbc = _nt(xt.astype(bf16), dstb)
            dxdt = dxt * tailx
            lastrow = jnp.sum(dxt * xt, axis=0, keepdims=True) + jnp.sum(dstv * stb.astype(f32), axis=0, keepdims=True) * ex_last
            dst[...] = dstv * ex_last + _tn(ccb, dzb)
            cb = _nt(ccb, bcb)
            cbt = _nt(bcb, ccb)
            dcb = jnp.zeros((LS, LS), f32)
            dac8 = jnp.zeros((LS, SSD_HPG), f32)
            dact = jnp.zeros((SSD_HPG, LS), f32)
            dxin = []
            for h in range(SSD_HPG):
                sl = slice(hd * h, hd * (h + 1))
                col, rowv = ac8[:, h:h + 1], act[h:h + 1, :]
                dec = jnp.exp(jnp.where(causal, col - rowv, -1e30))
                dect = jnp.exp(jnp.where(anti, rowv - col, -1e30))
                gm = cb * dec
                dgm = _nt(dyb[:, sl], xdtb[:, sl])
                dxin.append(_nn((cbt * dect).astype(bf16), dyb[:, sl]))
                dcb = dcb + dgm * dec
                w = dgm * gm
                dac8 = dac8 + jnp.sum(w, axis=1, keepdims=True) * (head_id == h).astype(f32)
                dact = dact + (head_col == h).astype(f32) * jnp.sum(w, axis=0, keepdims=True)
            dxintra = jnp.concatenate(dxin, axis=1)
            dcbb = dcb.astype(bf16)
            dcc = dcc + _nn(dcbb, bcb)
            dbc = dbc + _tn(dcbb, ccb)
            dxdt = dxdt + dxintra
            dacx = dyv * yinter - dxt * xt + last_row * lastrow
            red = _sel_r(jnp.concatenate([dacx, dxdt * xs], axis=0), ext)
            dac8 = dac8 - _rows_to_cols(dact, eye) + red[0:LS]
            da8 = _sel_l(upper, dac8)
            ddt8 = red[LS:2 * LS] + da8 * neg_a
            gal_ref[0] += jnp.sum(da8 * dt8 * neg_a, axis=0, keepdims=True)
            ddr = ddt8 * sg_ref[0, rows, :]
            ddt_ref[0, rows, :] = ddr
            gdb_ref[0] += jnp.sum(ddr, axis=0, keepdims=True)
            dsilu = lambda p, s: s * (1.0 + p * (1.0 - s))
            dp_ref[rows, pl.ds(0, SSD_GW)] = (dskv * dyv + dxdt * dtx) * dsilu(px, sgx)
            dp_ref[rows, pl.ds(SSD_GW, SSD_STATE)] = dbc * dsilu(pb, sgb)
            dp_ref[rows, pl.ds(SSD_GW + SSD_STATE, SSD_STATE)] = dcc * dsilu(pc, sgc)
            return carry

        lax.fori_loop(0, nck, chunk, 0, unroll=min(4, nck))

        @pl.when(pl.program_id(1) == nb - 1)
        def _():
            gsk_ref[0] = skacc[...]

    grp, xs, ph, pht, gvec, ex, stspec = _ssd_specs(tb, rev_nb=nb)
    small = pl.BlockSpec((1, 1, SSD_HPG), lambda g, i: (g, 0, 0))
    ext = pl.BlockSpec((SSD_GW, SSD_HPG), lambda g, i: (0, 0))
    sm = jax.ShapeDtypeStruct((SSD_GROUPS, 1, SSD_HPG), f32)
    expander = _expander()
    return pl.pallas_call(
        body, name="ssd_bwd", grid=(SSD_GROUPS, nb),
        in_specs=[grp, ph, ph, pht, ph, gvec, small, ex, ext, xs, stspec],
        out_specs=[grp, ph, gvec, small, small],
        out_shape=[jax.ShapeDtypeStruct((S, CONV_DIM), f32), jax.ShapeDtypeStruct((SSD_GROUPS, S, SSD_HPG), f32),
                   jax.ShapeDtypeStruct((SSD_GROUPS, 1, SSD_GW), f32), sm, sm],
        scratch_shapes=[pltpu.VMEM((SSD_STATE, SSD_GW), f32), pltpu.VMEM((1, SSD_GW), f32)],
        compiler_params=_params(("parallel", "arbitrary")),
    )(pre, dt_g, ac_g, act_g, sg_g, dskx, nega_g, expander, expander.T, dy, states)


def _tiles(S):
    return dict(tb=min(512, S), tr=min(256, S), tm=min(1024, S))


def _local_step(x, positions, target, norm1_w, w_main, w_dt, conv_w, conv_b, dt_bias, a_log, d_skip, ssd_norm_w,
                late_token, late_weights, norm_f_w, reducer):
    S, D = x.shape
    t = _tiles(S)
    tb, tr, tm = t["tb"], t["tr"], t["tm"]

    half = RET_DK // 2
    inv_freq = ROPE_THETA ** (-jnp.arange(half, dtype=f32) / half)
    ang = positions.astype(f32)[:, None] * inv_freq
    cos, sin = jnp.cos(ang), jnp.sin(ang)
    log_gamma = jnp.log1p(-(2.0 ** (-5.0 - jnp.arange(RET_HEADS, dtype=f32))))
    idx = jnp.arange(CHUNK, dtype=f32)
    intra = jnp.exp(jnp.abs(idx[:, None] - idx[None, :]) * log_gamma[:, None, None])
    qdec = jnp.exp((idx + 1.0)[None, :] * log_gamma[:, None])[:, :, None]
    kdec = jnp.exp((CHUNK - 1.0 - idx)[None, :] * log_gamma[:, None])[:, :, None]
    cdec = jnp.exp(CHUNK * log_gamma)[:, None, None]
    conv_wm, conv_bm = _xbc_group_major(conv_w), _xbc_group_major(conv_b)

    h, ht = _norm1_fwd(x, norm1_w + late_token[0, 0], tr)
    proj = _mm1(h, w_main, tm=tm, tn=1024, tk=D, out_dtype=f32, name="proj_main")
    dt_raw = _mm1(h, w_dt, tm=tm, tn=128, tk=D, out_dtype=f32, name="proj_dt")
    y_ret, yr, yrt, ret_states = _ret_fwd(proj, cos, sin, intra, qdec, kdec, cdec, tb)
    pre = _conv_fwd(proj, conv_wm, conv_bm, min(1024, S), 512)
    pad64 = lambda v: jnp.pad(v, ((0, 0), (0, 128 - SSD_HEADS)))
    dt, sg, ac = _dt_prep(dt_raw, pad64(dt_bias), pad64(a_log), tb)
    dt_g, ac_g, sg_g, act_g = _group_major(dt), _group_major(ac), _group_major(sg), _group_major_t(ac)
    dskx = jnp.repeat(d_skip.reshape(SSD_GROUPS, 1, SSD_HPG), SSD_GW // SSD_HPG, axis=2)
    nega_g = (-jnp.exp(a_log)).reshape(SSD_GROUPS, 1, SSD_HPG)
    y_ssd, ssd_states = _ssd_fwd(pre, dt_g, ac_g, act_g, dskx, tb)
    ys, yst = _ssd_norm_fwd(y_ssd, proj, ssd_norm_w, tr // 2)
    w_br, w_bs, w_o = late_weights(ys)
    p_r = _mm1(yr, w_br, tm=tm, tn=1024, tk=2048, out_dtype=f32, name="branch_ret")
    p_s = _mm1(ys, w_bs, tm=tm, tn=1024, tk=2048, out_dtype=f32, name="branch_ssd")
    merged, mergedt = _merge_fwd(p_r, p_s, proj, tr)
    mo = _mm1(merged, w_o, tm=tm, tn=1024, tk=2048, out_dtype=f32, name="out_proj")
    dx2, dx2b, loss, g_norm_f = _final_fwd_bwd(x, mo, target, norm_f_w.reshape(1, D), tr)

    tkt = min(4096, S)
    wg = lambda at, b, name, tn=1024: _mm1(at, b, tm=min(1024, at.shape[0]), tn=tn, tk=tkt, out_dtype=f32, name=name)
    dm = _mm1(dx2b, w_o, tm=tm, tn=1024, tk=2048, out_dtype=f32, name="d_merged", tb=True)
    g_w_o = wg(mergedt, dx2b, "g_w_out")
    dp_r, dp_s, dproj = _merge_bwd(dm, p_r, p_s, proj, tr)
    dyr = _mm1(dp_r, w_br, tm=tm, tn=1024, tk=2048, out_dtype=f32, name="d_yr", tb=True)
    dys = _mm1(dp_s, w_bs, tm=tm, tn=1024, tk=2048, out_dtype=f32, name="d_ys", tb=True)
    g_w_br = wg(yrt, dp_r, "g_w_br_ret")
    g_w_bs = wg(yst, dp_s, "g_w_br_ssd")
    dy_ssd, dproj, g_ssd_norm = _ssd_norm_bwd(y_ssd, proj, ssd_norm_w, dys, dproj, tr // 2)
    dproj = _ret_bwd(proj, cos, sin, intra, qdec, kdec, cdec, y_ret, dyr, ret_states, dproj, tb)
    dpre, ddt_g, gsk, gal, gdb = _ssd_bwd(pre, dt_g, ac_g, act_g, sg_g, dskx, nega_g, dy_ssd, ssd_states, tb)
    dproj, gcw, gcb = _conv_bwd(dpre, proj, conv_wm, dproj, min(1024, S), 512)
    ddt = jnp.transpose(ddt_g, (1, 0, 2)).reshape(S, SSD_HEADS)
    ddt_p = jnp.pad(ddt, ((0, 0), (0, 128 - SSD_HEADS))).astype(bf16)

    ht_sib, ht_own = reducer.rows(ht)
    gs_main = wg(ht_sib, dproj, "g_w_in_main_sib")
    gs_dt = wg(ht_sib, ddt_p, "g_w_in_dt_sib", tn=128)
    swap_state = reducer.first(gs_main, gs_dt, g_w_br, g_w_bs, g_w_o)
    ddt_p = ddt_p + swap_state[-1][0, 0].astype(bf16)
    go_main = wg(ht_own, dproj, "g_w_in_main_own")
    go_dt = wg(ht_own, ddt_p, "g_w_in_dt_own", tn=128)
    reduce_state = reducer.second(swap_state, go_main, go_dt)
    ddt_p = ddt_p + reduce_state[-1][0, 0].astype(bf16)
    dh = _mm([(dproj, 0, w_main, 0, 0, N_MAIN, N_MAIN // 8), (ddt_p, 0, w_dt, 0, 0, 128, 128)], S, D, tm=tm, tn=1024,
             out_dtype=f32, name="d_h", tb=True)
    grad_x, g_norm1 = _norm1_bwd(x, norm1_w, dh, dx2, tr)

    seg = lambda v: jnp.sum(v.reshape(SSD_HEADS, SSD_GW // SSD_HPG), axis=1).reshape(1, SSD_HEADS)
    grads = dict(
        norm1_w=g_norm1, w_in_main=(gs_main, go_main), w_in_dt=(gs_dt, go_dt),
        conv_w=_xbc_original(gcw), conv_b=_xbc_original(gcb),
        dt_bias=gdb.reshape(1, SSD_HEADS), a_log=gal.reshape(1, SSD_HEADS), d_skip=seg(gsk),
        ssd_norm_w=g_ssd_norm, w_br_ret=g_w_br, w_br_ssd=g_w_bs, w_out=g_w_o, norm_f_w=g_norm_f,
    )
    return loss, grad_x, grads, reduce_state


def _me():
    return lax.axis_index("x"), lax.axis_index("y"), lax.axis_index("c")


def _other_chips(x, y):
    return [(1 - x, y), (x, 1 - y), (1 - x, 1 - y)]


def _gather_weights(a, cw):
    R = a.shape[0]
    hr = R // 2

    def body(a_ref, cw_ref, ga_ref, gc_ref, send_sems, recv_sems):
        x, y, c = _me()
        k = 2 * x + y
        sibling = (x, y, 1 - c)
        chips = _other_chips(x, y)

        def small(j, src_shard, to):
            return pltpu.make_async_remote_copy(
                src_ref=cw_ref, dst_ref=gc_ref.at[src_shard], send_sem=send_sems.at[6 + j], recv_sem=recv_sems.at[6 + j],
                device_id=to, device_id_type=MESH)

        def copies(j, src_shard, half, to, from_input):
            rows = pl.ds(half * hr, hr)
            src = a_ref.at[rows, :] if from_input else ga_ref.at[src_shard, rows, :]
            return [pltpu.make_async_remote_copy(
                src_ref=src, dst_ref=ga_ref.at[src_shard, rows, :], send_sem=send_sems.at[j], recv_sem=recv_sems.at[j],
                device_id=to, device_id_type=MESH)]

        first = []
        for j, chip in enumerate(chips):
            first += copies(j, k, c, (*chip, c), True)
            first.append(small(j, k, (*chip, c)))
        for cp in first:
            cp.start()
        passed = []
        for j, chip in enumerate(chips):
            kk = 2 * chip[0] + chip[1]
            for cp in copies(j, kk, c, (x, y, c), False):
                cp.wait_recv()
            fw = copies(3 + j, kk, c, sibling, False)
            for cp in fw:
                cp.start()
            passed += fw
        for j, chip in enumerate(chips):
            kk = 2 * chip[0] + chip[1]
            for cp in copies(3 + j, kk, 1 - c, (x, y, c), False):
                cp.wait_recv()
            small(j, kk, (x, y, c)).wait_recv()
        for cp in first + passed:
            cp.wait_send()

    return pl.pallas_call(
        body, name="gather_weights", in_specs=[ANY, ANY], out_specs=[ANY, ANY],
        out_shape=[jax.ShapeDtypeStruct((N_SHARD,) + a.shape, a.dtype), jax.ShapeDtypeStruct((N_SHARD,) + cw.shape, cw.dtype)],
        scratch_shapes=[pltpu.SemaphoreType.DMA((9,)), pltpu.SemaphoreType.DMA((9,))],
        compiler_params=pltpu.CompilerParams(has_side_effects=True),
    )(a, cw)


def _gather_late_copies(src, land, send_sems, recv_sems):
    x, y, c = _me()
    k = 2 * x + y
    return [pltpu.make_async_remote_copy(src_ref=src, dst_ref=land.at[k], send_sem=send_sems.at[j], recv_sem=recv_sems.at[j],
                                         device_id=(*chip, c), device_id_type=MESH) for j, chip in enumerate(_other_chips(x, y))]


def _gather_late_start(b):
    land = lax.empty((N_SHARD,) + b.shape, b.dtype)

    def body(b_ref, land_ref, send_sems, recv_sems, b_thru, land_thru, token):
        for cp in _gather_late_copies(b_ref, land_ref, send_sems, recv_sems):
            cp.start()
        token[...] = jnp.zeros_like(token)

    return pl.pallas_call(
        body, name="gather_late_start", in_specs=[HBM, HBM],
        out_specs=(SEM, SEM, HBM, HBM, pl.BlockSpec(memory_space=pltpu.VMEM)),
        out_shape=(pltpu.SemaphoreType.DMA((3,)), pltpu.SemaphoreType.DMA((3,)), pltpu.HBM(b.shape, b.dtype),
                   pltpu.HBM(land.shape, land.dtype), jax.ShapeDtypeStruct((8, 128), f32)),
        input_output_aliases={0: 2, 1: 3}, compiler_params=pltpu.CompilerParams(has_side_effects=DATAFLOW),
    )(pltpu.with_memory_space_constraint(b, pltpu.HBM), pltpu.with_memory_space_constraint(land, pltpu.HBM))


def _gather_late_wait(send_sems, recv_sems, src, land, after):
    def body(b_ref, land_ref, send_sems_ref, recv_sems_ref, after_ref, b_dead, land_out):
        x, y, c = _me()
        for j, chip in enumerate(_other_chips(x, y)):
            kk = 2 * chip[0] + chip[1]
            cp = pltpu.make_async_remote_copy(src_ref=b_ref, dst_ref=land_ref.at[kk], send_sem=send_sems_ref.at[j],
                                              recv_sem=recv_sems_ref.at[j], device_id=(x, y, c), device_id_type=MESH)
            cp.wait_send()
            cp.wait_recv()

    return pl.pallas_call(
        body, name="gather_late_wait", in_specs=[HBM, HBM, SEM, SEM, ANY], out_specs=[HBM, HBM],
        out_shape=[pltpu.HBM(src.shape, src.dtype), pltpu.HBM(land.shape, land.dtype)], input_output_aliases={0: 0, 1: 1},
        compiler_params=pltpu.CompilerParams(has_side_effects=DATAFLOW),
    )(src, land, send_sems, recv_sems, after)[1]


HBM = pl.BlockSpec(memory_space=pltpu.HBM)
SEM = pl.BlockSpec(memory_space=pltpu.SEMAPHORE)
DATAFLOW = pltpu.SideEffectType.DATAFLOW_SIDE_EFFECTING


def _swap_copies(srcs, lands, send_sems, recv_sems):
    x, y, c = _me()

    def cp(src, dst, q):
        return pltpu.make_async_remote_copy(src_ref=src, dst_ref=dst, send_sem=send_sems.at[q], recv_sem=recv_sems.at[q],
                                            device_id=(x, y, 1 - c), device_id_type=MESH)

    return [cp(srcs[0], lands[0], 0), cp(srcs[1], lands[1], 1)] + [cp(srcs[2].at[s, 1 - c], lands[2].at[s], 2 + s) for s in range(N_SHARD)]


def _sibling_swap_start(g_main, g_dt, g_b):
    srcs = [g_main, g_dt, g_b]
    lands = [lax.empty(g_main.shape, g_main.dtype), lax.empty(g_dt.shape, g_dt.dtype),
             lax.empty(g_b.shape[:1] + g_b.shape[2:], g_b.dtype)]

    def body(*refs):
        for cp in _swap_copies(refs[0:3], refs[3:6], refs[6], refs[7]):
            cp.start()
        refs[14][...] = jnp.zeros_like(refs[14])

    hbm = lambda a: pltpu.HBM(a.shape, a.dtype)
    out = pl.pallas_call(
        body, name="sibling_swap_start", in_specs=[HBM] * 6,
        out_specs=(SEM, SEM, *[HBM] * 6, pl.BlockSpec(memory_space=pltpu.VMEM)),
        out_shape=(pltpu.SemaphoreType.DMA((2 + N_SHARD,)), pltpu.SemaphoreType.DMA((2 + N_SHARD,)), *[hbm(a) for a in srcs + lands],
                   jax.ShapeDtypeStruct((8, 128), f32)),
        input_output_aliases={t: 2 + t for t in range(6)}, compiler_params=pltpu.CompilerParams(has_side_effects=DATAFLOW),
    )(*[pltpu.with_memory_space_constraint(a, pltpu.HBM) for a in srcs + lands])
    return out[0], out[1], list(out[2:5]), list(out[5:8]), out[8]


def _sibling_swap_wait(send_sems, recv_sems, srcs, lands, after):
    def body(*refs):
        for cp in _swap_copies(refs[0:3], refs[3:6], refs[6], refs[7]):
            cp.wait_send()
            cp.wait_recv()

    hbm = lambda a: pltpu.HBM(a.shape, a.dtype)
    out = pl.pallas_call(
        body, name="sibling_swap_wait", in_specs=[HBM] * 6 + [SEM, SEM, ANY], out_specs=[HBM] * 6,
        out_shape=[hbm(a) for a in list(srcs) + list(lands)], input_output_aliases={t: t for t in range(6)},
        compiler_params=pltpu.CompilerParams(has_side_effects=DATAFLOW),
    )(*srcs, *lands, send_sems, recv_sems, after)
    return list(out[:3]), list(out[3:])


def _exchange_copies(ins, lands, send_sems, recv_sems):
    n = len(ins)
    x, y, c = _me()
    cps = []
    for j, chip in enumerate(_other_chips(x, y)):
        kk = 2 * chip[0] + chip[1]
        for t in range(n):
            cps.append(pltpu.make_async_remote_copy(
                src_ref=ins[t].at[kk], dst_ref=lands[t].at[j], send_sem=send_sems.at[n * j + t],
                recv_sem=recv_sems.at[n * j + t], device_id=(*chip, c), device_id_type=MESH))
    return cps


def _chip_exchange_start(arrs):
    n = len(arrs)
    lands = [lax.empty((3,) + a.shape[1:], a.dtype) for a in arrs]

    def body(*refs):
        ins, lands_in = refs[:n], refs[n:2 * n]
        send_sems, recv_sems = refs[2 * n], refs[2 * n + 1]
        token = refs[4 * n + 2]
        for cp in _exchange_copies(ins, lands_in, send_sems, recv_sems):
            cp.start()
        token[...] = jnp.zeros_like(token)

    hbm = lambda a: pltpu.HBM(a.shape, a.dtype)
    out = pl.pallas_call(
        body, name="chip_exchange_start", in_specs=[HBM] * (2 * n),
        out_specs=(SEM, SEM, *[HBM] * (2 * n), pl.BlockSpec(memory_space=pltpu.VMEM)),
        out_shape=(pltpu.SemaphoreType.DMA((3 * n,)), pltpu.SemaphoreType.DMA((3 * n,)), *[hbm(a) for a in arrs],
                   *[hbm(a) for a in lands], jax.ShapeDtypeStruct((8, 128), f32)),
        input_output_aliases={t: 2 + t for t in range(2 * n)},
        compiler_params=pltpu.CompilerParams(has_side_effects=DATAFLOW),
    )(*[pltpu.with_memory_space_constraint(a, pltpu.HBM) for a in list(arrs) + lands])
    return out[0], out[1], list(out[2:2 + n]), list(out[2 + n:2 + 2 * n]), out[2 + 2 * n]


def _chip_exchange_wait(send_sems, recv_sems, srcs, lands, after):
    n = len(srcs)

    def body(*refs):
        ins, lands_in = refs[:n], refs[n:2 * n]
        send_sems_ref, recv_sems_ref = refs[2 * n], refs[2 * n + 1]
        for cp in _exchange_copies(ins, lands_in, send_sems_ref, recv_sems_ref):
            cp.wait_send()
            cp.wait_recv()

    hbm = lambda a: pltpu.HBM(a.shape, a.dtype)
    out = pl.pallas_call(
        body, name="chip_exchange_wait", in_specs=[HBM] * (2 * n) + [SEM, SEM, ANY],
        out_specs=[HBM] * (2 * n), out_shape=[hbm(a) for a in list(srcs) + list(lands)],
        input_output_aliases={t: t for t in range(2 * n)},
        compiler_params=pltpu.CompilerParams(has_side_effects=DATAFLOW),
    )(*srcs, *lands, send_sems, recv_sems, after)
    return list(out[:n]), list(out[n:])


def _share_halves(bufs, by_cols, name):
    n = len(bufs)

    def body(*refs):
        ins, outs = refs[:n], refs[n:2 * n]
        send_sems, recv_sems = refs[2 * n], refs[2 * n + 1]
        x, y, c = _me()

        def part(ref, t, half):
            if by_cols[t]:
                w = bufs[t].shape[1] // 2
                return ref.at[:, pl.ds(pl.multiple_of(half * w, 128), w)]
            return ref.at[half]

        sends = [pltpu.make_async_remote_copy(src_ref=part(ins[t], t, c), dst_ref=part(outs[t], t, c), send_sem=send_sems.at[t],
                                              recv_sem=recv_sems.at[t], device_id=(x, y, 1 - c), device_id_type=MESH) for t in range(n)]
        for cp in sends:
            cp.start()
        for t in range(n):
            pltpu.make_async_remote_copy(src_ref=part(ins[t], t, c), dst_ref=part(outs[t], t, 1 - c), send_sem=send_sems.at[t],
                                         recv_sem=recv_sems.at[t], device_id=(x, y, c), device_id_type=MESH).wait_recv()
        for cp in sends:
            cp.wait_send()

    return pl.pallas_call(
        body, name=name, in_specs=[ANY] * n, out_specs=[ANY] * n,
        out_shape=[jax.ShapeDtypeStruct(a.shape, a.dtype) for a in bufs], input_output_aliases={t: t for t in range(n)},
        scratch_shapes=[pltpu.SemaphoreType.DMA((n,)), pltpu.SemaphoreType.DMA((n,))],
        compiler_params=pltpu.CompilerParams(has_side_effects=True),
    )(*bufs)


def _gather_vec(v):
    n = v.shape[1]

    def body(v_ref, o_ref, send_sems, recv_sems):
        x, y, c = _me()
        me = 4 * x + 2 * y + c
        cps = []
        for j in range(1, 8):
            fx, fy, fc = (j >> 2) & 1, (j >> 1) & 1, j & 1
            peer = (x ^ fx, y ^ fy, c ^ fc)
            cps.append(pltpu.make_async_remote_copy(
                src_ref=v_ref, dst_ref=o_ref.at[pl.ds(me, 1), :], send_sem=send_sems.at[j - 1], recv_sem=recv_sems.at[j - 1],
                device_id=peer, device_id_type=MESH))
        for cp in cps:
            cp.start()
        for j in range(1, 8):
            fx, fy, fc = (j >> 2) & 1, (j >> 1) & 1, j & 1
            src = 4 * (x ^ fx) + 2 * (y ^ fy) + (c ^ fc)
            pltpu.make_async_remote_copy(
                src_ref=v_ref, dst_ref=o_ref.at[pl.ds(src, 1), :], send_sem=send_sems.at[j - 1], recv_sem=recv_sems.at[j - 1],
                device_id=(x, y, c), device_id_type=MESH).wait_recv()
        for cp in cps:
            cp.wait_send()

    return pl.pallas_call(
        body, name="gather_vec", in_specs=[ANY], out_specs=ANY, out_shape=jax.ShapeDtypeStruct((8, n), v.dtype),
        scratch_shapes=[pltpu.SemaphoreType.DMA((7,)), pltpu.SemaphoreType.DMA((7,))],
        compiler_params=pltpu.CompilerParams(has_side_effects=True),
    )(v)


def _pair_sum(g, r, name, tr):
    L, hr, C = r.shape
    both_halves = g.ndim == 4

    def body(c_ref, g_ref, r_ref, o_ref):
        def strip(rows):
            gv = g_ref[0, 0, rows, :] if both_halves else g_ref[0, rows, :]
            o_ref[0, rows, :] = (gv + r_ref[0, rows, :]).astype(bf16)
        _for_strips(tr, strip)

    g_spec = (pl.BlockSpec((1, 1, tr, C), lambda s, i, c_ref: (s, c_ref[0], i, 0)) if both_halves
              else pl.BlockSpec((1, tr, C), lambda s, i, c_ref: (s, i, 0)))
    grid_spec = pltpu.PrefetchScalarGridSpec(
        num_scalar_prefetch=1, grid=(L, hr // tr),
        in_specs=[g_spec, pl.BlockSpec((1, tr, C), lambda s, i, c_ref: (s, i, 0))],
        out_specs=pl.BlockSpec((1, tr, C), lambda s, i, c_ref: (s, i, 0)))
    c = lax.axis_index("c").reshape(1).astype(jnp.int32)
    return pl.pallas_call(body, name=name, grid_spec=grid_spec, out_shape=jax.ShapeDtypeStruct((L, hr, C), bf16),
                          compiler_params=_params(("parallel", "parallel")))(c, g, r)


def _own_sum(p, got, name, transposed=False):
    _, hr, C = p.shape
    tr = SUM_ROWS
    c_full, c_pad = C // 128 * 128, -(-C // 128) * 128

    def total(p_ref, got_ref, rows):
        return ((p_ref[0, rows, :].astype(f32) + got_ref[0, rows, :].astype(f32)) + got_ref[1, rows, :].astype(f32)) \
            + got_ref[2, rows, :].astype(f32)

    def body(idx_ref, p_ref, got_ref, o_ref):
        def strip(rows):
            o_ref[0, rows, :] = total(p_ref, got_ref, rows)
        _for_strips(tr, strip)

    def body_t(idx_ref, p_ref, got_ref, o_ref, buf):
        if c_pad > c_full:
            buf[:, pl.ds(c_full, c_pad - c_full)] = jnp.zeros((tr, c_pad - c_full), f32)

        def strip(rows):
            buf[rows, pl.ds(0, C)] = total(p_ref, got_ref, rows)
        _for_strips(tr, strip)
        o_ref[...] = buf[...].T[:C]

    in_specs = [pl.BlockSpec((1, tr, C), lambda i, idx: (idx[0], i, 0)), pl.BlockSpec((3, tr, C), lambda i, idx: (0, i, 0))]
    x, y, c = _me()
    idx = jnp.stack([2 * x + y, c]).astype(jnp.int32)
    if transposed:
        grid_spec = pltpu.PrefetchScalarGridSpec(num_scalar_prefetch=1, grid=(hr // tr,), in_specs=in_specs,
                                                 out_specs=pl.BlockSpec((C, tr), lambda i, idx: (0, idx[1] * (hr // tr) + i)),
                                                 scratch_shapes=[pltpu.VMEM((tr, c_pad), f32)])
        return pl.pallas_call(body_t, name=name, grid_spec=grid_spec, out_shape=jax.ShapeDtypeStruct((C, 2 * hr), f32),
                              compiler_params=_params(("parallel",)))(idx, p, got)
    grid_spec = pltpu.PrefetchScalarGridSpec(num_scalar_prefetch=1, grid=(hr // tr,), in_specs=in_specs,
                                             out_specs=pl.BlockSpec((1, tr, C), lambda i, idx: (idx[1], i, 0)))
    return pl.pallas_call(body, name=name, grid_spec=grid_spec, out_shape=jax.ShapeDtypeStruct((2, hr, C), f32),
                          compiler_params=_params(("parallel",)))(idx, p, got)


def _adamw(w, g, m, v, name, tr):
    _, R, C = w.shape
    rs = min(8, tr)

    def body(w_ref, g_ref, m_ref, v_ref, d_ref, nm_ref, nv_ref):
        def strip(s, carry):
            rows = pl.ds(pl.multiple_of(s * rs, rs), rs)
            gv = g_ref[0, rows, :]
            mn = ADAM_B1 * m_ref[0, rows, :] + (1.0 - ADAM_B1) * gv
            vn = ADAM_B2 * v_ref[0, rows, :] + (1.0 - ADAM_B2) * (gv * gv)
            m_hat = mn / (1.0 - ADAM_B1 ** ADAM_STEP)
            v_hat = vn / (1.0 - ADAM_B2 ** ADAM_STEP)
            d_ref[0, rows, :] = -ADAM_LR * (m_hat / (jnp.sqrt(v_hat) + ADAM_EPS) + ADAM_WD * w_ref[0, rows, :])
            nm_ref[0, rows, :] = mn
            nv_ref[0, rows, :] = vn
            return carry

        if R % tr == 0:
            lax.fori_loop(0, tr // rs, strip, 0, unroll=min(2, tr // rs))
        else:
            lax.fori_loop(0, jnp.minimum(tr, R - pl.program_id(0) * tr) // rs, strip, 0)

    blk, grid = pl.BlockSpec((1, tr, C), lambda i: (0, i, 0)), (-(-R // tr),)
    o = jax.ShapeDtypeStruct((1, R, C), f32)
    return pl.pallas_call(body, name=name, grid=grid, in_specs=[blk] * 4, out_specs=[blk] * 3, out_shape=[o, o, o],
                          compiler_params=_params(("parallel",)))(w, g, m, v)


def _sum8(t):
    n = t.shape[1]

    def body(t_ref, o_ref):
        acc = t_ref[pl.ds(0, 1), :]
        for r in range(1, 8):
            acc = acc + t_ref[pl.ds(r, 1), :]
        o_ref[...] = acc

    return pl.pallas_call(body, name="sum_devices", out_shape=jax.ShapeDtypeStruct((1, n), f32))(t)


def _reduce_swap_start(g_main, g_dt, g_b):
    hr = g_main.shape[0]
    return _sibling_swap_start(g_main, g_dt, g_b.reshape(N_SHARD, 2, hr, g_b.shape[-1]))


def _reduce_start(swap_state, g_main, g_dt):
    hr = g_main.shape[0]
    send_sems, recv_sems, srcs, lands, _ = swap_state
    srcs, (r_main, r_dt, r_b) = _sibling_swap_wait(send_sems, recv_sems, srcs, lands, g_dt)
    p_main = _pair_sum(g_main[None], r_main[None], "pair_sum_main", SUM_ROWS // 4)
    p_dt = _pair_sum(g_dt[None], r_dt[None], "pair_sum_dt", SUM_ROWS)
    p_b = _pair_sum(srcs[2], r_b, "pair_sum_b", SUM_ROWS)
    p_in = jnp.transpose(_w_in_grad_full(p_main[0], p_dt[0]).reshape(hr, N_SHARD, W_IN_SHARD), (1, 0, 2))
    return _chip_exchange_start([p_in, p_b])


def _reduce_finish(state, after):
    send_sems, recv_sems, srcs, lands, _ = state
    (p_in, p_b), (got_in, got_b) = _chip_exchange_wait(send_sems, recv_sems, srcs, lands, after)
    mine_in, mine_b = _own_sum(p_in, got_in, "own_sum_in", transposed=True), _own_sum(p_b, got_b, "own_sum_b")
    full_in_t, full_b = _share_halves([mine_in, mine_b], [True, False], "share_halves")
    return full_in_t, full_b.reshape(-1, full_b.shape[-1])


def kernel(x, positions, norm1_w, w_in, conv_w, conv_b, dt_bias, a_log, d_skip, ssd_norm_w, w_br_ret, w_br_ssd, w_out, norm_f_w, loss_target, m_norm1_w, m_w_in, m_conv_w, m_conv_b, m_dt_bias, m_a_log, m_d_skip, m_ssd_norm_w, m_w_br_ret, m_w_br_ssd, m_w_out, m_norm_f_w, v_norm1_w, v_w_in, v_conv_w, v_conv_b, v_dt_bias, v_a_log, v_d_skip, v_ssd_norm_w, v_w_br_ret, v_w_br_ssd, v_w_out, v_norm_f_w):
    D = D_MODEL
    xi, yi, ci = _me()
    k = 2 * xi + yi
    me = 2 * k + ci
    weights = dict(norm1_w=norm1_w, w_in=w_in, conv_w=conv_w, conv_b=conv_b, dt_bias=dt_bias, a_log=a_log, d_skip=d_skip,
                   ssd_norm_w=ssd_norm_w, w_br_ret=w_br_ret, w_br_ssd=w_br_ssd, w_out=w_out, norm_f_w=norm_f_w)
    mom1 = dict(norm1_w=m_norm1_w, w_in=m_w_in, conv_w=m_conv_w, conv_b=m_conv_b, dt_bias=m_dt_bias, a_log=m_a_log, d_skip=m_d_skip,
                ssd_norm_w=m_ssd_norm_w, w_br_ret=m_w_br_ret, w_br_ssd=m_w_br_ssd, w_out=m_w_out, norm_f_w=m_norm_f_w)
    mom2 = dict(norm1_w=v_norm1_w, w_in=v_w_in, conv_w=v_conv_w, conv_b=v_conv_b, dt_bias=v_dt_bias, a_log=v_a_log, d_skip=v_d_skip,
                ssd_norm_w=v_ssd_norm_w, w_br_ret=v_w_br_ret, w_br_ssd=v_w_br_ssd, w_out=v_w_out, norm_f_w=v_norm_f_w)

    a_sh = w_in[0].astype(bf16)
    b_sh = jnp.concatenate([w_br_ret[0], w_br_ssd[0], w_out[0]], axis=0).astype(bf16)
    ga, gc = _gather_weights(a_sh, conv_w[0])
    ga, b_late = lax.optimization_barrier((ga, b_sh))
    late_send, late_recv, late_src, late_land, late_token = _gather_late_start(b_late)
    own = lambda g, s: lax.dynamic_update_slice_in_dim(g, s[None], k, axis=0)
    ga, gc = own(ga, a_sh), own(gc, conv_w[0])
    w_main, w_dt = _split_w_in(jnp.transpose(ga, (1, 0, 2)).reshape(D, IN_PROJ))
    conv_full = jnp.transpose(gc, (1, 0, 2)).reshape(SSD_CONV, CONV_DIM)

    def late_weights(after):
        gb = own(_gather_late_wait(late_send, late_recv, late_src, late_land, after), b_sh)
        return gb[:, 0:512].reshape(2048, D), gb[:, 512:1536].reshape(4096, D), gb[:, 1536:2048].reshape(2048, D)

    class Reducer:
        @staticmethod
        def rows(ht):
            half = lambda i: lax.dynamic_slice_in_dim(ht, i * (D // 2), D // 2, axis=0)
            return half(1 - ci), half(ci)

        @staticmethod
        def first(g_main, g_dt, g_w_br, g_w_bs, g_w_o):
            g_b = jnp.concatenate([g_w_br.reshape(N_SHARD, 512, D), g_w_bs.reshape(N_SHARD, 1024, D),
                                   g_w_o.reshape(N_SHARD, 512, D)], axis=1)
            return _reduce_swap_start(g_main, g_dt, g_b)

        second = staticmethod(_reduce_start)

    loss, grad_x, g, reduce_state = _local_step(x[0], positions[0], loss_target[0], norm1_w, w_main, w_dt, conv_full, conv_b, dt_bias,
                                                a_log, d_skip, ssd_norm_w, late_token, late_weights, norm_f_w, Reducer)

    grad_w_in_t, full_b = _reduce_finish(reduce_state, g["norm1_w"])
    grad_mats = dict(w_br_ret=full_b[0:512], w_br_ssd=full_b[512:1536], w_out=full_b[1536:2048])

    small = [(n, weights[n].size) for n in ("norm1_w", "conv_b", "dt_bias", "a_log", "d_skip", "ssd_norm_w", "norm_f_w")]
    parts = [jnp.pad(loss.reshape(1, 1), ((0, 0), (0, 127)))] + [g[n].reshape(1, -1) for n, _ in small] + [g["conv_w"].reshape(1, -1)]
    vec = jnp.concatenate(parts, axis=1)
    nv = vec.shape[1]
    nvp = -(-nv // 128) * 128
    vec = jnp.pad(vec, ((0, 0), (0, nvp - nv)))
    total = _sum8(lax.dynamic_update_slice_in_dim(_gather_vec(vec), vec, me, axis=0))
    loss_out = total[0, 0]
    off = 128
    grad_small = {}
    for n, sz in small:
        grad_small[n] = total[:, off:off + sz]
        off += sz
    g_conv = total[:, off:off + SSD_CONV * CONV_DIM].reshape(SSD_CONV, CONV_DIM)
    g_conv = lax.dynamic_slice_in_dim(g_conv, k * (CONV_DIM // N_SHARD), CONV_DIM // N_SHARD, axis=1)
    grad_small["conv_w"] = g_conv.reshape(1, -1)

    upd = {}
    for n in ("w_br_ret", "w_br_ssd", "w_out"):
        upd[n] = _adamw(weights[n], grad_mats[n][None], mom1[n], mom2[n], "adamw_" + n, tr=SUM_ROWS)
    tp = lambda t: jnp.swapaxes(t, 1, 2)
    upd["w_in"] = tuple(tp(t) for t in _adamw(tp(w_in), grad_w_in_t[None], tp(m_w_in), tp(v_w_in), "adamw_w_in", tr=256))
    grad_mats["w_in"] = tp(grad_w_in_t[None])
    names_small = [n for n, _ in small] + ["conv_w"]
    flat = lambda d: jnp.concatenate([d[n].reshape(1, -1) for n in names_small], axis=1)
    ns = sum(weights[n].size for n in names_small)
    nsp = -(-ns // 128) * 128
    padv = lambda t: jnp.pad(t, ((0, 0), (0, nsp - ns)))
    small_upd = _adamw(padv(flat(weights))[None], padv(flat(grad_small))[None], padv(flat(mom1))[None],
                       jnp.pad(flat(mom2), ((0, 0), (0, nsp - ns)), constant_values=1.0)[None], "adamw_small", 1)
    off = 0
    for n in names_small:
        sz = weights[n].size
        upd[n] = tuple(t[0, :, off:off + sz] for t in small_upd)
        off += sz

    order = ["norm1_w", "w_in", "conv_w", "conv_b", "dt_bias", "a_log", "d_skip", "ssd_norm_w", "w_br_ret", "w_br_ssd", "w_out", "norm_f_w"]
    grads_out = {**grad_mats, **grad_small}
    shp = lambda n, t: t.reshape(weights[n].shape)
    return (loss_out, grad_x[None], *[shp(n, grads_out[n]) for n in order], *[shp(n, upd[n][0]) for n in order],
            *[shp(n, upd[n][1]) for n in order], *[shp(n, upd[n][2]) for n in order])
```

```python
import jax
import jax.numpy as jnp
import numpy as np
from jax import lax
from jax.experimental import pallas as pl
from jax.experimental.pallas import tpu as pltpu

f32 = jnp.float32
bf16 = jnp.bfloat16
HIGHEST = lax.Precision.HIGHEST
MESH = pl.DeviceIdType.MESH

D_MODEL = 2048
EPS = 1e-6
CHUNK = 64
RET_HEADS = 8
RET_DK = 256
RET_HW = 4 * RET_DK
ROPE_THETA = 10000.0
SSD_WIDTH = 4096
SSD_GROUPS = 8
SSD_STATE = 128
SSD_GW = 512
SSD_GC = SSD_GW + 2 * SSD_STATE
SSD_HPG = 8
SSD_CONV = 4
CONV_DIM = 6144
SSD_HEADS = 64
LS = 128

C_RET, C_Z, C_GATES, C_XBC = 0, 8192, 12288, 16384
N_MAIN = 22528
DT_OFF = 18432
IN_PROJ = 22592
N_SHARD = 4
W_IN_SHARD = IN_PROJ // N_SHARD

ADAM_LR, ADAM_B1, ADAM_B2, ADAM_EPS, ADAM_WD, ADAM_STEP = 0.001, 0.9, 0.999, 1e-08, 0.01, 10

VMEM_LIMIT = 56 * 1024 * 1024
SUM_ROWS = 128
ANY = pl.BlockSpec(memory_space=pl.ANY)


def _params(dims):
    return pltpu.CompilerParams(dimension_semantics=dims, vmem_limit_bytes=VMEM_LIMIT)


def _silu(x):
    return x * jax.nn.sigmoid(x)


def _dsilu(x):
    s = jax.nn.sigmoid(x)
    return s * (1.0 + x * (1.0 - s))


def _nt(a, b):
    return lax.dot_general(a, b, (((1,), (1,)), ((), ())), preferred_element_type=f32)


def _tn(a, b):
    return lax.dot_general(a, b, (((0,), (0,)), ((), ())), preferred_element_type=f32)


def _nn(a, b):
    return jnp.dot(a, b, preferred_element_type=f32)


def _hi(a, b):
    return jnp.dot(a, b, precision=HIGHEST, preferred_element_type=f32)


def _split(a):
    hi = a.astype(bf16)
    return hi, (a - hi.astype(f32)).astype(bf16)


def _sel_r(a, sel):
    hi, lo = _split(a)
    return _nn(hi, sel) + _nn(lo, sel)


def _sel_l(sel, a):
    hi, lo = _split(a)
    return _nn(sel, hi) + _nn(sel, lo)


def _rows_to_cols(t, eye):
    hi = t.astype(bf16)
    r1 = t - hi.astype(f32)
    mid = r1.astype(bf16)
    lo = (r1 - mid.astype(f32)).astype(bf16)
    return _nt(eye, hi) + _nt(eye, mid) + _nt(eye, lo)


def _xbc_group_major(t):
    R = t.shape[0]
    nb = SSD_GROUPS * SSD_STATE
    parts = [t[:, :SSD_WIDTH].reshape(R, SSD_GROUPS, SSD_GW), t[:, SSD_WIDTH:SSD_WIDTH + nb].reshape(R, SSD_GROUPS, SSD_STATE),
             t[:, SSD_WIDTH + nb:].reshape(R, SSD_GROUPS, SSD_STATE)]
    return jnp.concatenate(parts, axis=2).reshape(R, CONV_DIM)


def _xbc_original(t):
    R = t.shape[0]
    g = t.reshape(R, SSD_GROUPS, SSD_GC)
    parts = [g[:, :, :SSD_GW].reshape(R, SSD_WIDTH), g[:, :, SSD_GW:SSD_GW + SSD_STATE].reshape(R, SSD_GROUPS * SSD_STATE),
             g[:, :, SSD_GW + SSD_STATE:].reshape(R, SSD_GROUPS * SSD_STATE)]
    return jnp.concatenate(parts, axis=1)


def _split_w_in(w):
    D = w.shape[0]
    ret = jnp.transpose(w[:, :4 * 2048].reshape(D, 4, RET_HEADS, RET_DK), (0, 2, 1, 3)).reshape(D, 4 * 2048)
    w_dt = jnp.pad(w[:, DT_OFF:DT_OFF + SSD_HEADS], ((0, 0), (0, 128 - SSD_HEADS)))
    main = jnp.concatenate([ret, w[:, 8192:12288], w[:, DT_OFF + SSD_HEADS:], _xbc_group_major(w[:, 12288:DT_OFF])], axis=1)
    return main, w_dt


def _w_in_grad_full(g_main, g_dt):
    D = g_main.shape[0]
    ret = jnp.transpose(g_main[:, :C_Z].reshape(D, RET_HEADS, 4, RET_DK), (0, 2, 1, 3)).reshape(D, C_Z)
    return jnp.concatenate([ret, g_main[:, C_Z:C_GATES], _xbc_original(g_main[:, C_XBC:]), g_dt[:, :SSD_HEADS],
                            g_main[:, C_GATES:C_XBC]], axis=1)


def _mm(pairs, M, N, *, tm, tn, out_dtype, name, tb=False):
    P = len(pairs)
    nks = [K // tk for (_, _, _, _, _, K, tk) in pairs]
    starts = [int(s) for s in np.cumsum([0] + nks[:-1])]
    KT = int(sum(nks))
    in_specs, args = [], []
    for (a, a_cb, b, b_kb, b_nb, K, tk), s, nk in zip(pairs, starts, nks):
        def kk(k, s=s, nk=nk):
            return jnp.clip(k - s, 0, nk - 1)
        in_specs.append(pl.BlockSpec((tm, tk), lambda m, n, k, kk=kk, a_cb=a_cb: (m, a_cb + kk(k))))
        if tb:
            in_specs.append(pl.BlockSpec((tn, tk), lambda m, n, k, kk=kk, b_kb=b_kb, b_nb=b_nb: (b_nb + n, b_kb + kk(k))))
        else:
            in_specs.append(pl.BlockSpec((tk, tn), lambda m, n, k, kk=kk, b_kb=b_kb, b_nb=b_nb: (b_kb + kk(k), b_nb + n)))
        args += [a, b]

    def body(*refs):
        o_ref = refs[2 * P]
        k = pl.program_id(2)

        def prod(i):
            a = refs[2 * i][...].astype(bf16)
            b = refs[2 * i + 1][...].astype(bf16)
            return _nt(a, b) if tb else _nn(a, b)

        if KT == 1:
            o_ref[...] = prod(0).astype(out_dtype)
            return
        acc = refs[2 * P + 1]

        @pl.when(k == 0)
        def _():
            acc[...] = jnp.zeros_like(acc)

        for i in range(P):
            @pl.when((k >= starts[i]) & (k < starts[i] + nks[i]))
            def _(i=i):
                acc[...] += prod(i)

        @pl.when(k == KT - 1)
        def _():
            o_ref[...] = acc[...].astype(out_dtype)

    return pl.pallas_call(
        body, name=name, grid=(M // tm, N // tn, KT), in_specs=in_specs,
        out_specs=pl.BlockSpec((tm, tn), lambda m, n, k: (m, n)),
        out_shape=jax.ShapeDtypeStruct((M, N), out_dtype),
        scratch_shapes=[] if KT == 1 else [pltpu.VMEM((tm, tn), f32)],
        compiler_params=_params(("parallel", "parallel", "arbitrary")),
    )(*args)


def _mm1(a, b, *, tm, tn, tk, out_dtype, name, tb=False):
    M, K = a.shape
    N = b.shape[0] if tb else b.shape[1]
    return _mm([(a, 0, b, 0, 0, K, tk)], M, N, tm=tm, tn=tn, out_dtype=out_dtype, name=name, tb=tb)


RS = 16
CS = 32


def _for_strips(n_rows, fn, rs=RS, unroll=4):
    def step(s, carry):
        fn(pl.ds(pl.multiple_of(s * rs, rs), rs))
        return carry
    n = n_rows // rs
    lax.fori_loop(0, n, step, 0, unroll=min(unroll, n))


def _norm1_fwd(x, w, tr):
    S, D = x.shape

    def body(x_ref, w_ref, h_ref, ht_ref):
        def strip(rows):
            xv = x_ref[rows, :]
            r = lax.rsqrt(jnp.mean(xv * xv, axis=-1, keepdims=True) + EPS)
            h_ref[rows, :] = (xv * r * w_ref[...]).astype(bf16)
        _for_strips(tr, strip)
        ht_ref[...] = h_ref[...].T

    return pl.pallas_call(
        body, name="norm1_fwd", grid=(S // tr,),
        in_specs=[pl.BlockSpec((tr, D), lambda i: (i, 0)), pl.BlockSpec((1, D), lambda i: (0, 0))],
        out_specs=[pl.BlockSpec((tr, D), lambda i: (i, 0)), pl.BlockSpec((D, tr), lambda i: (0, i))],
        out_shape=[jax.ShapeDtypeStruct((S, D), bf16), jax.ShapeDtypeStruct((D, S), bf16)], compiler_params=_params(("parallel",)),
    )(x, w)


def _norm1_bwd(x, w, dh, dx2, tr):
    S, D = x.shape

    def body(x_ref, w_ref, dh_ref, dx2_ref, gx_ref, gw_ref, acc):
        @pl.when(pl.program_id(0) == 0)
        def _():
            acc[...] = jnp.zeros_like(acc)

        def strip(rows):
            xv = x_ref[rows, :]
            r = lax.rsqrt(jnp.mean(xv * xv, axis=-1, keepdims=True) + EPS)
            xh = xv * r
            dhv = dh_ref[rows, :]
            acc[...] += dhv * xh
            dxh = dhv * w_ref[...]
            gx_ref[rows, :] = dx2_ref[rows, :] + r * (dxh - xh * jnp.mean(dxh * xh, axis=-1, keepdims=True))
        _for_strips(tr, strip)

        @pl.when(pl.program_id(0) == S // tr - 1)
        def _():
            gw_ref[...] = jnp.sum(acc[...], axis=0, keepdims=True)

    row = pl.BlockSpec((tr, D), lambda i: (i, 0))
    vec = pl.BlockSpec((1, D), lambda i: (0, 0))
    return pl.pallas_call(
        body, name="norm1_bwd", grid=(S // tr,), in_specs=[row, vec, row, row], out_specs=[row, vec],
        out_shape=[jax.ShapeDtypeStruct((S, D), f32), jax.ShapeDtypeStruct((1, D), f32)],
        scratch_shapes=[pltpu.VMEM((RS, D), f32)], compiler_params=_params(("arbitrary",)),
    )(x, w, dh, dx2)


def _final_fwd_bwd(x, mo, target, wf, tr):
    S, D = x.shape

    def body(x_ref, mo_ref, t_ref, w_ref, dx2_ref, dx2b_ref, loss_ref, gw_ref, acc, lacc):
        @pl.when(pl.program_id(0) == 0)
        def _():
            acc[...] = jnp.zeros_like(acc)
            lacc[...] = jnp.zeros_like(lacc)

        def strip(rows):
            x2 = x_ref[rows, :] + mo_ref[rows, :]
            r = lax.rsqrt(jnp.mean(x2 * x2, axis=-1, keepdims=True) + EPS)
            xh = x2 * r
            wv = w_ref[...]
            err = xh * wv - t_ref[rows, :]
            lacc[...] += jnp.mean(err * err, axis=-1, keepdims=True)
            dy = err * (1.0 / D)
            acc[...] += dy * xh
            dxh = dy * wv
            dx2 = r * (dxh - xh * jnp.mean(dxh * xh, axis=-1, keepdims=True))
            dx2_ref[rows, :] = dx2
            dx2b_ref[rows, :] = dx2.astype(bf16)
        _for_strips(tr, strip)

        @pl.when(pl.program_id(0) == S // tr - 1)
        def _():
            gw_ref[...] = jnp.sum(acc[...], axis=0, keepdims=True)
            loss_ref[...] = 0.5 * jnp.sum(lacc[...], axis=0, keepdims=True)

    row = pl.BlockSpec((tr, D), lambda i: (i, 0))
    vec = pl.BlockSpec((1, D), lambda i: (0, 0))
    return pl.pallas_call(
        body, name="final_norm_loss", grid=(S // tr,), in_specs=[row, row, row, vec],
        out_specs=[row, row, pl.BlockSpec((1, 1), lambda i: (0, 0)), vec],
        out_shape=[jax.ShapeDtypeStruct((S, D), f32), jax.ShapeDtypeStruct((S, D), bf16), jax.ShapeDtypeStruct((1, 1), f32),
                   jax.ShapeDtypeStruct((1, D), f32)],
        scratch_shapes=[pltpu.VMEM((RS, D), f32), pltpu.VMEM((RS, 1), f32)], compiler_params=_params(("arbitrary",)),
    )(x, mo, target, wf)


def _merge_fwd(p_r, p_s, proj, tr):
    S, D = p_r.shape

    def body(pr_ref, ps_ref, g_ref, o_ref, ot_ref):
        def strip(rows):
            gr, gs = g_ref[rows, pl.ds(0, D)], g_ref[rows, pl.ds(D, D)]
            o_ref[rows, :] = (jax.nn.sigmoid(gr) * pr_ref[rows, :] + jax.nn.sigmoid(gs) * ps_ref[rows, :]).astype(bf16)
        _for_strips(tr, strip)
        ot_ref[...] = o_ref[...].T

    row = pl.BlockSpec((tr, D), lambda i: (i, 0))
    return pl.pallas_call(
        body, name="merge_fwd", grid=(S // tr,),
        in_specs=[row, row, pl.BlockSpec((tr, 2 * D), lambda i: (i, C_GATES // (2 * D)))],
        out_specs=[row, pl.BlockSpec((D, tr), lambda i: (0, i))],
        out_shape=[jax.ShapeDtypeStruct((S, D), bf16), jax.ShapeDtypeStruct((D, S), bf16)], compiler_params=_params(("parallel",)),
    )(p_r, p_s, proj)


def _merge_bwd(dm, p_r, p_s, proj, tr):
    S, D = p_r.shape

    def body(dm_ref, pr_ref, ps_ref, g_ref, dpr_ref, dps_ref, dproj_ref):
        def strip(rows):
            dmv = dm_ref[rows, :]
            sr = jax.nn.sigmoid(g_ref[rows, pl.ds(0, D)])
            ss = jax.nn.sigmoid(g_ref[rows, pl.ds(D, D)])
            dpr_ref[rows, :] = (dmv * sr).astype(bf16)
            dps_ref[rows, :] = (dmv * ss).astype(bf16)
            dproj_ref[rows, pl.ds(0, D)] = (dmv * pr_ref[rows, :] * sr * (1.0 - sr)).astype(bf16)
            dproj_ref[rows, pl.ds(D, D)] = (dmv * ps_ref[rows, :] * ss * (1.0 - ss)).astype(bf16)
        _for_strips(tr, strip)

    row = pl.BlockSpec((tr, D), lambda i: (i, 0))
    gates = pl.BlockSpec((tr, 2 * D), lambda i: (i, C_GATES // (2 * D)))
    o = jax.ShapeDtypeStruct((S, D), bf16)
    return pl.pallas_call(
        body, name="merge_bwd", grid=(S // tr,), in_specs=[row, row, row, gates],
        out_specs=[row, row, gates], out_shape=[o, o, jax.ShapeDtypeStruct((S, N_MAIN), bf16)],
        compiler_params=_params(("parallel",)),
    )(dm, p_r, p_s, proj)


def _ssd_norm_fwd(y, proj, w, tr):
    S, W = y.shape

    def body(y_ref, z_ref, w_ref, o_ref, ot_ref):
        def strip(rows):
            u = y_ref[rows, :] * _silu(z_ref[rows, :])
            r = lax.rsqrt(jnp.mean(u * u, axis=-1, keepdims=True) + EPS)
            o_ref[rows, :] = (u * r * w_ref[...]).astype(bf16)
        _for_strips(tr, strip)
        ot_ref[...] = o_ref[...].T

    row = pl.BlockSpec((tr, W), lambda i: (i, 0))
    return pl.pallas_call(
        body, name="ssd_norm_fwd", grid=(S // tr,),
        in_specs=[row, pl.BlockSpec((tr, W), lambda i: (i, C_Z // W)), pl.BlockSpec((1, W), lambda i: (0, 0))],
        out_specs=[row, pl.BlockSpec((W, tr), lambda i: (0, i))],
        out_shape=[jax.ShapeDtypeStruct((S, W), bf16), jax.ShapeDtypeStruct((W, S), bf16)], compiler_params=_params(("parallel",)),
    )(y, proj, w)


def _ssd_norm_bwd(y, proj, w, dys, dproj, tr):
    S, W = y.shape

    def body(y_ref, z_ref, w_ref, d_ref, _, dy_ref, dz_ref, gw_ref, acc):
        @pl.when(pl.program_id(0) == 0)
        def _():
            acc[...] = jnp.zeros_like(acc)

        def strip(rows):
            yv, zv, dv = y_ref[rows, :], z_ref[rows, :], d_ref[rows, :]
            sz = _silu(zv)
            u = yv * sz
            r = lax.rsqrt(jnp.mean(u * u, axis=-1, keepdims=True) + EPS)
            un = u * r
            acc[...] += dv * un
            dun = dv * w_ref[...]
            du = r * (dun - un * jnp.mean(dun * un, axis=-1, keepdims=True))
            dy_ref[rows, :] = du * sz
            dz_ref[rows, :] = (du * yv * _dsilu(zv)).astype(bf16)
        _for_strips(tr, strip)

        @pl.when(pl.program_id(0) == S // tr - 1)
        def _():
            gw_ref[...] = jnp.sum(acc[...], axis=0, keepdims=True)

    row = pl.BlockSpec((tr, W), lambda i: (i, 0))
    zcol = pl.BlockSpec((tr, W), lambda i: (i, C_Z // W))
    vec = pl.BlockSpec((1, W), lambda i: (0, 0))
    return pl.pallas_call(
        body, name="ssd_norm_bwd", grid=(S // tr,),
        in_specs=[row, zcol, vec, row, ANY], out_specs=[row, zcol, vec],
        out_shape=[jax.ShapeDtypeStruct((S, W), f32), jax.ShapeDtypeStruct(dproj.shape, bf16), jax.ShapeDtypeStruct((1, W), f32)],
        input_output_aliases={4: 1}, scratch_shapes=[pltpu.VMEM((RS, W), f32)], compiler_params=_params(("arbitrary",)),
    )(y, proj, w, dys, dproj)


def _rope(t, cos, sin):
    t1, t2 = t[:, :128], t[:, 128:]
    return jnp.concatenate([t1 * cos - t2 * sin, t2 * cos + t1 * sin], axis=1)


def _rope_t(d, cos, sin):
    d1, d2 = d[:, :128], d[:, 128:]
    return jnp.concatenate([d1 * cos + d2 * sin, d2 * cos - d1 * sin], axis=1)


def _ret_specs(tb, rev_nb=None):
    def blk(i):
        return i if rev_nb is None else rev_nb - 1 - i
    head = pl.BlockSpec((tb, RET_HW), lambda h, i: (blk(i), h))
    tab = pl.BlockSpec((tb, 128), lambda h, i: (blk(i), 0))
    mat = pl.BlockSpec((1, CHUNK, CHUNK), lambda h, i: (h, 0, 0))
    vec = pl.BlockSpec((1, CHUNK, 1), lambda h, i: (h, 0, 0))
    one = pl.BlockSpec((1, 1, 1), lambda h, i: (h, 0, 0))
    own = pl.BlockSpec((tb, RET_DK), lambda h, i: (blk(i), h))
    st = pl.BlockSpec((1, tb // CHUNK, RET_DK, RET_DK), lambda h, i: (h, blk(i), 0, 0))
    return head, tab, mat, vec, one, own, st


def _ret_fwd(proj, cos, sin, intra, qdec, kdec, cdec, tb):
    S = proj.shape[0]
    nc = S // CHUNK
    scale = RET_DK ** -0.5
    dk = RET_DK

    def body(p_ref, cos_ref, sin_ref, m_ref, qd_ref, kd_ref, cd_ref, y_ref, yr_ref, yrt_ref, st_ref, st):
        @pl.when(pl.program_id(1) == 0)
        def _():
            st[...] = jnp.zeros_like(st)

        mm, qd, kd, cd = m_ref[0], qd_ref[0], kd_ref[0], cd_ref[0]

        def chunk(c, carry):
            rows = pl.ds(pl.multiple_of(c * CHUNK, CHUNK), CHUNK)
            cs, sn = cos_ref[rows, :], sin_ref[rows, :]
            qr = _rope(p_ref[rows, pl.ds(0, dk)], cs, sn)
            kr = _rope(p_ref[rows, pl.ds(dk, dk)], cs, sn) * scale
            qb, kb, vb = qr.astype(bf16), kr.astype(bf16), p_ref[rows, pl.ds(2 * dk, dk)].astype(bf16)
            stb = st[...].astype(bf16)
            st_ref[0, c] = stb
            sc = (_nt(qb, kb) * mm).astype(bf16)
            y = _nn(sc, vb) + _nn(qb, stb) * qd
            st[...] = st[...] * cd + _tn((kr * kd).astype(bf16), vb)
            y_ref[rows, :] = y
            mu = jnp.mean(y, axis=-1, keepdims=True)
            yc = y - mu
            var = jnp.mean(yc * yc, axis=-1, keepdims=True)
            yr_ref[rows, :] = (yc * lax.rsqrt(var + EPS) * _silu(p_ref[rows, pl.ds(3 * dk, dk)])).astype(bf16)
            return carry

        lax.fori_loop(0, tb // CHUNK, chunk, 0, unroll=min(4, tb // CHUNK))
        yrt_ref[...] = yr_ref[...].T

    head, tab, mat, vec, one, own, stspec = _ret_specs(tb)
    return pl.pallas_call(
        body, name="ret_fwd", grid=(RET_HEADS, S // tb),
        in_specs=[head, tab, tab, mat, vec, vec, one],
        out_specs=[own, own, pl.BlockSpec((RET_DK, tb), lambda h, i: (h, i)), stspec],
        out_shape=[jax.ShapeDtypeStruct((S, 2048), f32), jax.ShapeDtypeStruct((S, 2048), bf16), jax.ShapeDtypeStruct((2048, S), bf16),
                   jax.ShapeDtypeStruct((RET_HEADS, nc, dk, dk), bf16)],
        scratch_shapes=[pltpu.VMEM((dk, dk), f32)], compiler_params=_params(("parallel", "arbitrary")),
    )(proj, cos, sin, intra, qdec, kdec, cdec)


def _ret_bwd(proj, cos, sin, intra, qdec, kdec, cdec, y, dyr, states, dproj, tb):
    S = proj.shape[0]
    nb = S // tb
    nck = tb // CHUNK
    scale = RET_DK ** -0.5
    dk = RET_DK

    def body(p_ref, cos_ref, sin_ref, m_ref, qd_ref, kd_ref, cd_ref, y_ref, dyr_ref, st_ref, _, o_ref, dst):
        @pl.when(pl.program_id(1) == 0)
        def _():
            dst[...] = jnp.zeros_like(dst)

        mm, qd, kd, cd = m_ref[0], qd_ref[0], kd_ref[0], cd_ref[0]

        def chunk(cc, carry):
            c = nck - 1 - cc
            rows = pl.ds(pl.multiple_of(c * CHUNK, CHUNK), CHUNK)
            cs, sn = cos_ref[rows, :], sin_ref[rows, :]
            qr = _rope(p_ref[rows, pl.ds(0, dk)], cs, sn)
            kr = _rope(p_ref[rows, pl.ds(dk, dk)], cs, sn) * scale
            qb, kb, vb = qr.astype(bf16), kr.astype(bf16), p_ref[rows, pl.ds(2 * dk, dk)].astype(bf16)
            kdb = (kr * kd).astype(bf16)
            stb = st_ref[0, c]
            yv, gv, dyrv = y_ref[rows, :], p_ref[rows, pl.ds(3 * dk, dk)], dyr_ref[rows, :]
            mu = jnp.mean(yv, axis=-1, keepdims=True)
            yc = yv - mu
            rstd = lax.rsqrt(jnp.mean(yc * yc, axis=-1, keepdims=True) + EPS)
            yn = yc * rstd
            o_ref[rows, pl.ds(3 * dk, dk)] = (dyrv * yn * _dsilu(gv)).astype(bf16)
            dyn = dyrv * _silu(gv)
            dy = rstd * (dyn - jnp.mean(dyn, axis=-1, keepdims=True) - yn * jnp.mean(dyn * yn, axis=-1, keepdims=True))
            dyb = dy.astype(bf16)
            dyqb = (dy * qd).astype(bf16)
            dstb = dst[...].astype(bf16)
            sct = (_nt(kb, qb) * mm).astype(bf16)
            ds = (_nt(dyb, vb) * mm).astype(bf16)
            dsT = (_nt(vb, dyb) * mm).astype(bf16)
            dv = _nn(sct, dyb) + _nn(kdb, dstb)
            dqr = _nn(ds, kb) + _nt(dyqb, stb)
            dkr = _nn(dsT, qb) + _nt(vb, dstb) * kd
            dst[...] = dst[...] * cd + _tn(qb, dyqb)
            o_ref[rows, pl.ds(0, dk)] = _rope_t(dqr, cs, sn).astype(bf16)
            o_ref[rows, pl.ds(dk, dk)] = (_rope_t(dkr, cs, sn) * scale).astype(bf16)
            o_ref[rows, pl.ds(2 * dk, dk)] = dv.astype(bf16)
            return carry

        lax.fori_loop(0, nck, chunk, 0, unroll=min(4, nck))

    head, tab, mat, vec, one, own, stspec = _ret_specs(tb, rev_nb=nb)
    return pl.pallas_call(
        body, name="ret_bwd", grid=(RET_HEADS, nb),
        in_specs=[head, tab, tab, mat, vec, vec, one, own, own, stspec, ANY],
        out_specs=head, out_shape=jax.ShapeDtypeStruct(dproj.shape, bf16), input_output_aliases={10: 0},
        scratch_shapes=[pltpu.VMEM((dk, dk), f32)], compiler_params=_params(("parallel", "arbitrary")),
    )(proj, cos, sin, intra, qdec, kdec, cdec, y, dyr, states, dproj)


def _conv_fwd(proj, conv_w, conv_b, tb, cw):
    S = proj.shape[0]
    off = C_XBC // cw

    def body(x_ref, halo_ref, w_ref, b_ref, o_ref, xe):
        xe[pl.ds(0, 8), :] = jnp.where(pl.program_id(1) == 0, 0.0, halo_ref[...])
        xe[pl.ds(8, CS), :] = x_ref[pl.ds(0, CS), :]
        ws = [w_ref[pl.ds(j, 1), :] for j in range(SSD_CONV)]
        for s in range(tb // CS):
            tap = (lambda j: xe[pl.ds(5 + j, CS), :]) if s == 0 else (lambda j, s=s: x_ref[pl.ds(s * CS - 3 + j, CS), :])
            acc = b_ref[...] + ws[0] * tap(0)
            for j in range(1, SSD_CONV):
                acc = acc + ws[j] * tap(j)
            o_ref[pl.ds(s * CS, CS), :] = acc

    return pl.pallas_call(
        body, name="conv_fwd", grid=(CONV_DIM // cw, S // tb),
        in_specs=[pl.BlockSpec((tb, cw), lambda j, i: (i, off + j)),
                  pl.BlockSpec((8, cw), lambda j, i: (jnp.maximum(i * (tb // 8) - 1, 0), off + j)),
                  pl.BlockSpec((SSD_CONV, cw), lambda j, i: (0, j)), pl.BlockSpec((1, cw), lambda j, i: (0, j))],
        out_specs=pl.BlockSpec((tb, cw), lambda j, i: (i, j)),
        out_shape=jax.ShapeDtypeStruct((S, CONV_DIM), f32),
        scratch_shapes=[pltpu.VMEM((CS + 8, cw), f32)], compiler_params=_params(("parallel", "arbitrary")),
    )(proj, proj, conv_w, conv_b)


def _conv_bwd(dpre, proj, conv_w, dproj, tb, cw):
    S, n = dpre.shape
    nb = S // tb
    xoff = C_XBC // cw

    def body(d_ref, dh_ref, x_ref, xh_ref, w_ref, _, dx_ref, gw_ref, gb_ref, de, xe, accw, accb):
        i = pl.program_id(1)

        @pl.when(i == 0)
        def _():
            accw[...] = jnp.zeros_like(accw)
            accb[...] = jnp.zeros_like(accb)

        ns = tb // CS
        de[pl.ds(0, CS), :] = d_ref[pl.ds(tb - CS, CS), :]
        de[pl.ds(CS, 8), :] = jnp.where(i == nb - 1, 0.0, dh_ref[...])
        xe[pl.ds(0, 8), :] = jnp.where(i == 0, 0.0, xh_ref[...])
        xe[pl.ds(8, CS), :] = x_ref[pl.ds(0, CS), :]
        ws = [w_ref[pl.ds(j, 1), :] for j in range(SSD_CONV)]
        fold = lambda p: sum(p[8 * q:8 * (q + 1)] for q in range(1, CS // 8)) + p[0:8]
        for s in range(ns):
            dv = d_ref[pl.ds(s * CS, CS), :]
            ahead = (lambda o: de[pl.ds(o, CS), :]) if s == ns - 1 else (lambda o, s=s: d_ref[pl.ds(s * CS + o, CS), :])
            xtap = (lambda j: xe[pl.ds(5 + j, CS), :]) if s == 0 else (lambda j, s=s: x_ref[pl.ds(s * CS - 3 + j, CS), :])
            acc = ws[SSD_CONV - 1] * dv
            for j in range(SSD_CONV - 1):
                acc = acc + ws[j] * ahead(3 - j)
            dx_ref[pl.ds(s * CS, CS), :] = acc.astype(bf16)
            accb[...] += fold(dv)
            for j in range(SSD_CONV):
                accw[j] += fold(dv * xtap(j))

        @pl.when(i == nb - 1)
        def _():
            gb_ref[...] = jnp.sum(accb[...], axis=0, keepdims=True)
            for j in range(SSD_CONV):
                gw_ref[pl.ds(j, 1), :] = jnp.sum(accw[j], axis=0, keepdims=True)

    return pl.pallas_call(
        body, name="conv_bwd", grid=(n // cw, nb),
        in_specs=[pl.BlockSpec((tb, cw), lambda j, i: (i, j)),
                  pl.BlockSpec((8, cw), lambda j, i: (jnp.minimum((i + 1) * (tb // 8), S // 8 - 1), j)),
                  pl.BlockSpec((tb, cw), lambda j, i: (i, xoff + j)),
                  pl.BlockSpec((8, cw), lambda j, i: (jnp.maximum(i * (tb // 8) - 1, 0), xoff + j)),
                  pl.BlockSpec((SSD_CONV, cw), lambda j, i: (0, j)), ANY],
        out_specs=[pl.BlockSpec((tb, cw), lambda j, i: (i, xoff + j)), pl.BlockSpec((SSD_CONV, cw), lambda j, i: (0, j)),
                   pl.BlockSpec((1, cw), lambda j, i: (0, j))],
        out_shape=[jax.ShapeDtypeStruct(dproj.shape, bf16), jax.ShapeDtypeStruct((SSD_CONV, n), f32), jax.ShapeDtypeStruct((1, n), f32)],
        input_output_aliases={5: 0},
        scratch_shapes=[pltpu.VMEM((CS + 8, cw), f32), pltpu.VMEM((CS + 8, cw), f32), pltpu.VMEM((SSD_CONV, 8, cw), f32),
                        pltpu.VMEM((8, cw), f32)],
        compiler_params=_params(("parallel", "arbitrary")),
    )(dpre, dpre, proj, proj, conv_w, dproj)


def _dt_prep(dt_raw, dt_bias, a_log, tb):
    S = dt_raw.shape[0]

    def body(r_ref, b_ref, al_ref, dt_ref, sg_ref, ac_ref):
        li = lax.broadcasted_iota(jnp.int32, (LS, LS), 0)
        si = lax.broadcasted_iota(jnp.int32, (LS, LS), 1)
        tri = (li >= si).astype(f32)
        neg_a = -jnp.exp(al_ref[...])
        for c in range(tb // LS):
            rows = pl.ds(c * LS, LS)
            xv = r_ref[rows, :] + b_ref[...]
            dtv = jax.nn.softplus(xv)
            dt_ref[rows, :] = dtv
            sg_ref[rows, :] = jax.nn.sigmoid(xv)
            ac_ref[rows, :] = _hi(tri, dtv * neg_a)

    row = pl.BlockSpec((tb, 128), lambda i: (i, 0))
    vec = pl.BlockSpec((1, 128), lambda i: (0, 0))
    o = jax.ShapeDtypeStruct((S, 128), f32)
    return pl.pallas_call(body, name="dt_prep", grid=(S // tb,), in_specs=[row, vec, vec], out_specs=[row, row, row],
                          out_shape=[o, o, o], compiler_params=_params(("parallel",)))(dt_raw, dt_bias, a_log)


def _group_major(t):
    S = t.shape[0]
    return jnp.transpose(t[:, :SSD_HEADS].reshape(S, SSD_GROUPS, SSD_HPG), (1, 0, 2))


def _group_major_t(t):
    S = t.shape[0]
    return jnp.transpose(t[:, :SSD_HEADS].reshape(S // LS, LS, SSD_GROUPS, SSD_HPG), (2, 0, 3, 1))


def _ssd_specs(tb, rev_nb=None):
    def blk(i):
        return i if rev_nb is None else rev_nb - 1 - i
    grp = pl.BlockSpec((tb, SSD_GC), lambda g, i: (blk(i), g))
    xs = pl.BlockSpec((tb, SSD_GW), lambda g, i: (blk(i), g))
    ph = pl.BlockSpec((1, tb, SSD_HPG), lambda g, i: (g, blk(i), 0))
    pht = pl.BlockSpec((1, tb // LS, SSD_HPG, LS), lambda g, i: (g, blk(i), 0, 0))
    gvec = pl.BlockSpec((1, 1, SSD_GW), lambda g, i: (g, 0, 0))
    ex = pl.BlockSpec((SSD_HPG, SSD_GW), lambda g, i: (0, 0))
    st = pl.BlockSpec((1, tb // LS, SSD_STATE, SSD_GW), lambda g, i: (g, blk(i), 0, 0))
    return grp, xs, ph, pht, gvec, ex, st


def _expander():
    return jnp.repeat(jnp.eye(SSD_HPG, dtype=f32), SSD_GW // SSD_HPG, axis=1).astype(bf16)


def _expand3(dt8, ac8, ex):
    stack = jnp.concatenate([dt8, jnp.exp(ac8), jnp.exp(ac8[LS - 1:LS, :] - ac8)], axis=0)
    wide = _sel_r(stack, ex)
    return wide[0:LS], wide[LS:2 * LS], wide[2 * LS:3 * LS]


def _ssd_fwd(pre, dt_g, ac_g, act_g, dskx, tb):
    S = pre.shape[0]
    nc = S // LS
    hd = SSD_GW // SSD_HPG

    def body(p_ref, dt_ref, ac_ref, act_ref, dsk_ref, ex_ref, y_ref, st_ref, st):
        @pl.when(pl.program_id(1) == 0)
        def _():
            st[...] = jnp.zeros_like(st)

        ex = ex_ref[...]
        li = lax.broadcasted_iota(jnp.int32, (LS, LS), 0)
        si = lax.broadcasted_iota(jnp.int32, (LS, LS), 1)
        causal = li >= si

        def chunk(c, carry):
            rows = pl.ds(pl.multiple_of(c * LS, LS), LS)
            xs = _silu(p_ref[rows, pl.ds(0, SSD_GW)])
            bcb = _silu(p_ref[rows, pl.ds(SSD_GW, SSD_STATE)]).astype(bf16)
            ccb = _silu(p_ref[rows, pl.ds(SSD_GW + SSD_STATE, SSD_STATE)]).astype(bf16)
            dt8, ac8, act = dt_ref[0, rows, :], ac_ref[0, rows, :], act_ref[0, c]
            dtx, eax, tailx = _expand3(dt8, ac8, ex)
            xdt = xs * dtx
            cb = _nt(ccb, bcb)
            stb = st[...].astype(bf16)
            st_ref[0, c] = stb
            xdtb = xdt.astype(bf16)
            outs = []
            for h in range(SSD_HPG):
                dec = jnp.exp(jnp.where(causal, ac8[:, h:h + 1] - act[h:h + 1, :], -1e30))
                outs.append(_nn((cb * dec).astype(bf16), xdtb[:, hd * h:hd * (h + 1)]))
            y_ref[rows, :] = jnp.concatenate(outs, axis=1) + _nn(ccb, stb) * eax + dsk_ref[0] * xs
            st[...] = st[...] * eax[LS - 1:LS, :] + _tn(bcb, (xdt * tailx).astype(bf16))
            return carry

        lax.fori_loop(0, tb // LS, chunk, 0, unroll=min(4, tb // LS))

    grp, xs, ph, pht, gvec, ex, stspec = _ssd_specs(tb)
    return pl.pallas_call(
        body, name="ssd_fwd", grid=(SSD_GROUPS, S // tb),
        in_specs=[grp, ph, ph, pht, gvec, ex], out_specs=[xs, stspec],
        out_shape=[jax.ShapeDtypeStruct((S, SSD_WIDTH), f32), jax.ShapeDtypeStruct((SSD_GROUPS, nc, SSD_STATE, SSD_GW), bf16)],
        scratch_shapes=[pltpu.VMEM((SSD_STATE, SSD_GW), f32)], compiler_params=_params(("parallel", "arbitrary")),
    )(pre, dt_g, ac_g, act_g, dskx, _expander())


def _ssd_bwd(pre, dt_g, ac_g, act_g, sg_g, dskx, nega_g, dy, states, tb):
    S = pre.shape[0]
    nb = S // tb
    nck = tb // LS
    hd = SSD_GW // SSD_HPG

    def body(p_ref, dt_ref, ac_ref, act_ref, sg_ref, dsk_ref, na_ref, ex_ref, ext_ref, dy_ref, st_ref,
             dp_ref, ddt_ref, gsk_ref, gal_ref, gdb_ref, dst, skacc):
        @pl.when(pl.program_id(1) == 0)
        def _():
            dst[...] = jnp.zeros_like(dst)
            skacc[...] = jnp.zeros_like(skacc)
            gal_ref[...] = jnp.zeros_like(gal_ref)
            gdb_ref[...] = jnp.zeros_like(gdb_ref)

        ex, ext = ex_ref[...], ext_ref[...]
        li = lax.broadcasted_iota(jnp.int32, (LS, LS), 0)
        si = lax.broadcasted_iota(jnp.int32, (LS, LS), 1)
        causal = li >= si
        anti = si >= li
        upper = anti.astype(bf16)
        eye = (si == li).astype(bf16)
        last_row = (lax.broadcasted_iota(jnp.int32, (LS, 1), 0) == LS - 1).astype(f32)
        head_id = lax.broadcasted_iota(jnp.int32, (1, SSD_HPG), 1)
        head_col = lax.broadcasted_iota(jnp.int32, (SSD_HPG, 1), 0)
        neg_a = na_ref[0]
        dskv = dsk_ref[0]

        def chunk(cc, carry):
            c = nck - 1 - cc
            rows = pl.ds(pl.multiple_of(c * LS, LS), LS)
            px = p_ref[rows, pl.ds(0, SSD_GW)]
            pb = p_ref[rows, pl.ds(SSD_GW, SSD_STATE)]
            pc = p_ref[rows, pl.ds(SSD_GW + SSD_STATE, SSD_STATE)]
            sgx, sgb, sgc = jax.nn.sigmoid(px), jax.nn.sigmoid(pb), jax.nn.sigmoid(pc)
            xs = px * sgx
            bcb = (pb * sgb).astype(bf16)
            ccb = (pc * sgc).astype(bf16)
            dt8, ac8, act = dt_ref[0, rows, :], ac_ref[0, rows, :], act_ref[0, c]
            dtx, eax, tailx = _expand3(dt8, ac8, ex)
            xdt = xs * dtx
            ex_last = eax[LS - 1:LS, :]
            stb = st_ref[0, c]
            dyv = dy_ref[rows, :]
            dyb = dyv.astype(bf16)
            xdtb = xdt.astype(bf16)
            skacc[...] += jnp.sum(dyv * xs, axis=0, keepdims=True)
            yinter = _nn(ccb, stb) * eax
            dzb = (dyv * eax).astype(bf16)
            dcc = _nt(dzb, stb)
            dstv = dst[...]
            dstb = dstv.astype(bf16)
            xt = xdt * tailx
            dxt = _nn(bcb, dstb)
            dbc = _nt(xt.astype(bf16), dstb)
            dxdt = dxt * tailx
            lastrow = jnp.sum(dxt * xt, axis=0, keepdims=True) + jnp.sum(dstv * stb.astype(f32), axis=0, keepdims=True) * ex_last
            dst[...] = dstv * ex_last + _tn(ccb, dzb)
            cb = _nt(ccb, bcb)
            cbt = _nt(bcb, ccb)
            dcb = jnp.zeros((LS, LS), f32)
            dac8 = jnp.zeros((LS, SSD_HPG), f32)
            dact = jnp.zeros((SSD_HPG, LS), f32)
            dxin = []
            for h in range(SSD_HPG):
                sl = slice(hd * h, hd * (h + 1))
                col, rowv = ac8[:, h:h + 1], act[h:h + 1, :]
                dec = jnp.exp(jnp.where(causal, col - rowv, -1e30))
                dect = jnp.exp(jnp.where(anti, rowv - col, -1e30))
                gm = cb * dec
                dgm = _nt(dyb[:, sl], xdtb[:, sl])
                dxin.append(_nn((cbt * dect).astype(bf16), dyb[:, sl]))
                dcb = dcb + dgm * dec
                w = dgm * gm
                dac8 = dac8 + jnp.sum(w, axis=1, keepdims=True) * (head_id == h).astype(f32)
                dact = dact + (head_col == h).astype(f32) * jnp.sum(w, axis=0, keepdims=True)
            dxintra = jnp.concatenate(dxin, axis=1)
            dcbb = dcb.astype(bf16)
            dcc = dcc + _nn(dcbb, bcb)
            dbc = dbc + _tn(dcbb, ccb)
            dxdt = dxdt + dxintra
            dacx = dyv * yinter - dxt * xt + last_row * lastrow
            red = _sel_r(jnp.concatenate([dacx, dxdt * xs], axis=0), ext)
            dac8 = dac8 - _rows_to_cols(dact, eye) + red[0:LS]
            da8 = _sel_l(upper, dac8)
            ddt8 = red[LS:2 * LS] + da8 * neg_a
            gal_ref[0] += jnp.sum(da8 * dt8 * neg_a, axis=0, keepdims=True)
            ddr = ddt8 * sg_ref[0, rows, :]
            ddt_ref[0, rows, :] = ddr
            gdb_ref[0] += jnp.sum(ddr, axis=0, keepdims=True)
            dsilu = lambda p, s: s * (1.0 + p * (1.0 - s))
            dp_ref[rows, pl.ds(0, SSD_GW)] = (dskv * dyv + dxdt * dtx) * dsilu(px, sgx)
            dp_ref[rows, pl.ds(SSD_GW, SSD_STATE)] = dbc * dsilu(pb, sgb)
            dp_ref[rows, pl.ds(SSD_GW + SSD_STATE, SSD_STATE)] = dcc * dsilu(pc, sgc)
            return carry

        lax.fori_loop(0, nck, chunk, 0, unroll=min(4, nck))

        @pl.when(pl.program_id(1) == nb - 1)
        def _():
            gsk_ref[0] = skacc[...]

    grp, xs, ph, pht, gvec, ex, stspec = _ssd_specs(tb, rev_nb=nb)
    small = pl.BlockSpec((1, 1, SSD_HPG), lambda g, i: (g, 0, 0))
    ext = pl.BlockSpec((SSD_GW, SSD_HPG), lambda g, i: (0, 0))
    sm = jax.ShapeDtypeStruct((SSD_GROUPS, 1, SSD_HPG), f32)
    expander = _expander()
    return pl.pallas_call(
        body, name="ssd_bwd", grid=(SSD_GROUPS, nb),
        in_specs=[grp, ph, ph, pht, ph, gvec, small, ex, ext, xs, stspec],
        out_specs=[grp, ph, gvec, small, small],
        out_shape=[jax.ShapeDtypeStruct((S, CONV_DIM), f32), jax.ShapeDtypeStruct((SSD_GROUPS, S, SSD_HPG), f32),
                   jax.ShapeDtypeStruct((SSD_GROUPS, 1, SSD_GW), f32), sm, sm],
        scratch_shapes=[pltpu.VMEM((SSD_STATE, SSD_GW), f32), pltpu.VMEM((1, SSD_GW), f32)],
        compiler_params=_params(("parallel", "arbitrary")),
    )(pre, dt_g, ac_g, act_g, sg_g, dskx, nega_g, expander, expander.T, dy, states)


def _tiles(S):
    return dict(tb=min(512, S), tr=min(256, S), tm=min(1024, S))


def _local_step(x, positions, target, norm1_w, w_main, w_dt, conv_w, conv_b, dt_bias, a_log, d_skip, ssd_norm_w,
                late_token, late_weights, norm_f_w, reducer):
    S, D = x.shape
    t = _tiles(S)
    tb, tr, tm = t["tb"], t["tr"], t["tm"]

    half = RET_DK // 2
    inv_freq = ROPE_THETA ** (-jnp.arange(half, dtype=f32) / half)
    ang = positions.astype(f32)[:, None] * inv_freq
    cos, sin = jnp.cos(ang), jnp.sin(ang)
    log_gamma = jnp.log1p(-(2.0 ** (-5.0 - jnp.arange(RET_HEADS, dtype=f32))))
    idx = jnp.arange(CHUNK, dtype=f32)
    intra = jnp.exp(jnp.abs(idx[:, None] - idx[None, :]) * log_gamma[:, None, None])
    qdec = jnp.exp((idx + 1.0)[None, :] * log_gamma[:, None])[:, :, None]
    kdec = jnp.exp((CHUNK - 1.0 - idx)[None, :] * log_gamma[:, None])[:, :, None]
    cdec = jnp.exp(CHUNK * log_gamma)[:, None, None]
    conv_wm, conv_bm = _xbc_group_major(conv_w), _xbc_group_major(conv_b)

    h, ht = _norm1_fwd(x, norm1_w + late_token[0, 0], tr)
    proj = _mm1(h, w_main, tm=tm, tn=1024, tk=D, out_dtype=f32, name="proj_main")
    dt_raw = _mm1(h, w_dt, tm=tm, tn=128, tk=D, out_dtype=f32, name="proj_dt")
    y_ret, yr, yrt, ret_states = _ret_fwd(proj, cos, sin, intra, qdec, kdec, cdec, tb)
    pre = _conv_fwd(proj, conv_wm, conv_bm, min(1024, S), 512)
    pad64 = lambda v: jnp.pad(v, ((0, 0), (0, 128 - SSD_HEADS)))
    dt, sg, ac = _dt_prep(dt_raw, pad64(dt_bias), pad64(a_log), tb)
    dt_g, ac_g, sg_g, act_g = _group_major(dt), _group_major(ac), _group_major(sg), _group_major_t(ac)
    dskx = jnp.repeat(d_skip.reshape(SSD_GROUPS, 1, SSD_HPG), SSD_GW // SSD_HPG, axis=2)
    nega_g = (-jnp.exp(a_log)).reshape(SSD_GROUPS, 1, SSD_HPG)
    y_ssd, ssd_states = _ssd_fwd(pre, dt_g, ac_g, act_g, dskx, tb)
    ys, yst = _ssd_norm_fwd(y_ssd, proj, ssd_norm_w, tr // 2)
    w_br, w_bs, w_o = late_weights(ys)
    p_r = _mm1(yr, w_br, tm=tm, tn=1024, tk=2048, out_dtype=f32, name="branch_ret")
    p_s = _mm1(ys, w_bs, tm=tm, tn=1024, tk=2048, out_dtype=f32, name="branch_ssd")
    merged, mergedt = _merge_fwd(p_r, p_s, proj, tr)
    mo = _mm1(merged, w_o, tm=tm, tn=1024, tk=2048, out_dtype=f32, name="out_proj")
    dx2, dx2b, loss, g_norm_f = _final_fwd_bwd(x, mo, target, norm_f_w.reshape(1, D), tr)

    tkt = min(4096, S)
    wg = lambda at, b, name, tn=1024: _mm1(at, b, tm=min(1024, at.shape[0]), tn=tn, tk=tkt, out_dtype=f32, name=name)
    dm = _mm1(dx2b, w_o, tm=tm, tn=1024, tk=2048, out_dtype=f32, name="d_merged", tb=True)
    g_w_o = wg(mergedt, dx2b, "g_w_out")
    dp_r, dp_s, dproj = _merge_bwd(dm, p_r, p_s, proj, tr)
    dyr = _mm1(dp_r, w_br, tm=tm, tn=1024, tk=2048, out_dtype=f32, name="d_yr", tb=True)
    dys = _mm1(dp_s, w_bs, tm=tm, tn=1024, tk=2048, out_dtype=f32, name="d_ys", tb=True)
    g_w_br = wg(yrt, dp_r, "g_w_br_ret")
    g_w_bs = wg(yst, dp_s, "g_w_br_ssd")
    dy_ssd, dproj, g_ssd_norm = _ssd_norm_bwd(y_ssd, proj, ssd_norm_w, dys, dproj, tr // 2)
    dproj = _ret_bwd(proj, cos, sin, intra, qdec, kdec, cdec, y_ret, dyr, ret_states, dproj, tb)
    dpre, ddt_g, gsk, gal, gdb = _ssd_bwd(pre, dt_g, ac_g, act_g, sg_g, dskx, nega_g, dy_ssd, ssd_states, tb)
    dproj, gcw, gcb = _conv_bwd(dpre, proj, conv_wm, dproj, min(1024, S), 512)
    ddt = jnp.transpose(ddt_g, (1, 0, 2)).reshape(S, SSD_HEADS)
    ddt_p = jnp.pad(ddt, ((0, 0), (0, 128 - SSD_HEADS))).astype(bf16)

    ht_sib, ht_own = reducer.rows(ht)
    gs_main = wg(ht_sib, dproj, "g_w_in_main_sib")
    gs_dt = wg(ht_sib, ddt_p, "g_w_in_dt_sib", tn=128)
    swap_state = reducer.first(gs_main, gs_dt, g_w_br, g_w_bs, g_w_o)
    ddt_p = ddt_p + swap_state[-1][0, 0].astype(bf16)
    go_main = wg(ht_own, dproj, "g_w_in_main_own")
    go_dt = wg(ht_own, ddt_p, "g_w_in_dt_own", tn=128)
    reduce_state = reducer.second(swap_state, go_main, go_dt)
    ddt_p = ddt_p + reduce_state[-1][0, 0].astype(bf16)
    dh = _mm([(dproj, 0, w_main, 0, 0, N_MAIN, N_MAIN // 8), (ddt_p, 0, w_dt, 0, 0, 128, 128)], S, D, tm=tm, tn=1024,
             out_dtype=f32, name="d_h", tb=True)
    grad_x, g_norm1 = _norm1_bwd(x, norm1_w, dh, dx2, tr)

    seg = lambda v: jnp.sum(v.reshape(SSD_HEADS, SSD_GW // SSD_HPG), axis=1).reshape(1, SSD_HEADS)
    grads = dict(
        norm1_w=g_norm1, w_in_main=(gs_main, go_main), w_in_dt=(gs_dt, go_dt),
        conv_w=_xbc_original(gcw), conv_b=_xbc_original(gcb),
        dt_bias=gdb.reshape(1, SSD_HEADS), a_log=gal.reshape(1, SSD_HEADS), d_skip=seg(gsk),
        ssd_norm_w=g_ssd_norm, w_br_ret=g_w_br, w_br_ssd=g_w_bs, w_out=g_w_o, norm_f_w=g_norm_f,
    )
    return loss, grad_x, grads, reduce_state


def _me():
    return lax.axis_index("x"), lax.axis_index("y"), lax.axis_index("c")


def _other_chips(x, y):
    return [(1 - x, y), (x, 1 - y), (1 - x, 1 - y)]


def _gather_weights(a, cw):
    R = a.shape[0]
    hr, hq = R // 2, R // 4

    def body(a_ref, cw_ref, ga_ref, gc_ref, send_sems, recv_sems):
        x, y, c = _me()
        me, sibling = (x, y, c), (x, y, 1 - c)
        nx, ny = (1 - x, y, c), (x, 1 - y, c)
        k, kx, ky, kd = 2 * x + y, 2 * (1 - x) + y, 2 * x + (1 - y), 2 * (1 - x) + (1 - y)

        def rows(half, q):
            return pl.ds(pl.multiple_of(half * hr + q * hq, 8), hq)

        def cp(sem, shard, half, q, to, src=None):
            dst = ga_ref.at[shard, rows(half, q), :]
            return pltpu.make_async_remote_copy(src_ref=dst if src is None else src, dst_ref=dst, send_sem=send_sems.at[sem],
                                                recv_sem=recv_sems.at[sem], device_id=to, device_id_type=MESH)

        def small(j, src_shard, to):
            return pltpu.make_async_remote_copy(
                src_ref=cw_ref, dst_ref=gc_ref.at[src_shard], send_sem=send_sems.at[12 + j], recv_sem=recv_sems.at[12 + j],
                device_id=to, device_id_type=MESH)

        own = lambda q: a_ref.at[rows(c, q), :]
        smalls = [small(j, k, (*chip, c)) for j, chip in enumerate(_other_chips(x, y))]
        sends = [cp(0, k, c, 0, nx, own(0)), cp(2, k, c, 1, ny, own(1)), cp(1, k, c, 1, nx, own(1)), cp(3, k, c, 0, ny, own(0))]
        for s in sends + smalls:
            s.start()
        arrivals = [(0, kx, 0, (4, ny)), (2, ky, 1, (5, nx)), (1, kx, 1, None), (3, ky, 0, None), (4, kd, 0, None), (5, kd, 1, None)]
        for sem, shard, q, onward in arrivals:
            cp(sem, shard, c, q, me).wait_recv()
            if onward is not None:
                sends.append(cp(onward[0], shard, c, q, onward[1]))
                sends[-1].start()
            sends.append(cp(6 + sem, shard, c, q, sibling))
            sends[-1].start()
        for sem, shard, q, _ in arrivals:
            cp(6 + sem, shard, 1 - c, q, me).wait_recv()
        for j, chip in enumerate(_other_chips(x, y)):
            small(j, 2 * chip[0] + chip[1], me).wait_recv()
        for s in sends + smalls:
            s.wait_send()

    return pl.pallas_call(
        body, name="gather_weights", in_specs=[ANY, ANY], out_specs=[ANY, ANY],
        out_shape=[jax.ShapeDtypeStruct((N_SHARD,) + a.shape, a.dtype), jax.ShapeDtypeStruct((N_SHARD,) + cw.shape, cw.dtype)],
        scratch_shapes=[pltpu.SemaphoreType.DMA((15,)), pltpu.SemaphoreType.DMA((15,))],
        compiler_params=pltpu.CompilerParams(has_side_effects=True),
    )(a, cw)


def _gather_late_copies(src, land, send_sems, recv_sems):
    x, y, c = _me()
    k = 2 * x + y
    return [pltpu.make_async_remote_copy(src_ref=src, dst_ref=land.at[k], send_sem=send_sems.at[j], recv_sem=recv_sems.at[j],
                                         device_id=(*chip, c), device_id_type=MESH) for j, chip in enumerate(_other_chips(x, y))]


def _gather_late_start(b):
    land = lax.empty((N_SHARD,) + b.shape, b.dtype)

    def body(b_ref, land_ref, send_sems, recv_sems, b_thru, land_thru, token):
        for cp in _gather_late_copies(b_ref, land_ref, send_sems, recv_sems):
            cp.start()
        token[...] = jnp.zeros_like(token)

    return pl.pallas_call(
        body, name="gather_late_start", in_specs=[HBM, HBM],
        out_specs=(SEM, SEM, HBM, HBM, pl.BlockSpec(memory_space=pltpu.VMEM)),
        out_shape=(pltpu.SemaphoreType.DMA((3,)), pltpu.SemaphoreType.DMA((3,)), pltpu.HBM(b.shape, b.dtype),
                   pltpu.HBM(land.shape, land.dtype), jax.ShapeDtypeStruct((8, 128), f32)),
        input_output_aliases={0: 2, 1: 3}, compiler_params=pltpu.CompilerParams(has_side_effects=DATAFLOW),
    )(pltpu.with_memory_space_constraint(b, pltpu.HBM), pltpu.with_memory_space_constraint(land, pltpu.HBM))


def _gather_late_wait(send_sems, recv_sems, src, land, after):
    def body(b_ref, land_ref, send_sems_ref, recv_sems_ref, after_ref, b_dead, land_out):
        x, y, c = _me()
        for j, chip in enumerate(_other_chips(x, y)):
            kk = 2 * chip[0] + chip[1]
            cp = pltpu.make_async_remote_copy(src_ref=b_ref, dst_ref=land_ref.at[kk], send_sem=send_sems_ref.at[j],
                                              recv_sem=recv_sems_ref.at[j], device_id=(x, y, c), device_id_type=MESH)
            cp.wait_send()
            cp.wait_recv()

    return pl.pallas_call(
        body, name="gather_late_wait", in_specs=[HBM, HBM, SEM, SEM, ANY], out_specs=[HBM, HBM],
        out_shape=[pltpu.HBM(src.shape, src.dtype), pltpu.HBM(land.shape, land.dtype)], input_output_aliases={0: 0, 1: 1},
        compiler_params=pltpu.CompilerParams(has_side_effects=DATAFLOW),
    )(src, land, send_sems, recv_sems, after)[1]


HBM = pl.BlockSpec(memory_space=pltpu.HBM)
SEM = pl.BlockSpec(memory_space=pltpu.SEMAPHORE)
DATAFLOW = pltpu.SideEffectType.DATAFLOW_SIDE_EFFECTING


def _swap_copies(srcs, lands, send_sems, recv_sems):
    x, y, c = _me()

    def cp(src, dst, q):
        return pltpu.make_async_remote_copy(src_ref=src, dst_ref=dst, send_sem=send_sems.at[q], recv_sem=recv_sems.at[q],
                                            device_id=(x, y, 1 - c), device_id_type=MESH)

    return [cp(srcs[0], lands[0], 0), cp(srcs[1], lands[1], 1)] + [cp(srcs[2].at[s, 1 - c], lands[2].at[s], 2 + s) for s in range(N_SHARD)]


def _sibling_swap_start(g_main, g_dt, g_b):
    srcs = [g_main, g_dt, g_b]
    lands = [lax.empty(g_main.shape, g_main.dtype), lax.empty(g_dt.shape, g_dt.dtype),
             lax.empty(g_b.shape[:1] + g_b.shape[2:], g_b.dtype)]

    def body(*refs):
        for cp in _swap_copies(refs[0:3], refs[3:6], refs[6], refs[7]):
            cp.start()
        refs[14][...] = jnp.zeros_like(refs[14])

    hbm = lambda a: pltpu.HBM(a.shape, a.dtype)
    out = pl.pallas_call(
        body, name="sibling_swap_start", in_specs=[HBM] * 6,
        out_specs=(SEM, SEM, *[HBM] * 6, pl.BlockSpec(memory_space=pltpu.VMEM)),
        out_shape=(pltpu.SemaphoreType.DMA((2 + N_SHARD,)), pltpu.SemaphoreType.DMA((2 + N_SHARD,)), *[hbm(a) for a in srcs + lands],
                   jax.ShapeDtypeStruct((8, 128), f32)),
        input_output_aliases={t: 2 + t for t in range(6)}, compiler_params=pltpu.CompilerParams(has_side_effects=DATAFLOW),
    )(*[pltpu.with_memory_space_constraint(a, pltpu.HBM) for a in srcs + lands])
    return out[0], out[1], list(out[2:5]), list(out[5:8]), out[8]


def _sibling_swap_wait(send_sems, recv_sems, srcs, lands, after):
    def body(*refs):
        for cp in _swap_copies(refs[0:3], refs[3:6], refs[6], refs[7]):
            cp.wait_send()
            cp.wait_recv()

    hbm = lambda a: pltpu.HBM(a.shape, a.dtype)
    out = pl.pallas_call(
        body, name="sibling_swap_wait", in_specs=[HBM] * 6 + [SEM, SEM, ANY], out_specs=[HBM] * 6,
        out_shape=[hbm(a) for a in list(srcs) + list(lands)], input_output_aliases={t: t for t in range(6)},
        compiler_params=pltpu.CompilerParams(has_side_effects=DATAFLOW),
    )(*srcs, *lands, send_sems, recv_sems, after)
    return list(out[:3]), list(out[3:])


def _exchange_copies(ins, lands, send_sems, recv_sems):
    n = len(ins)
    x, y, c = _me()
    cps = []
    for j, chip in enumerate(_other_chips(x, y)):
        kk = 2 * chip[0] + chip[1]
        for t in range(n):
            cps.append(pltpu.make_async_remote_copy(
                src_ref=ins[t].at[kk], dst_ref=lands[t].at[j], send_sem=send_sems.at[n * j + t],
                recv_sem=recv_sems.at[n * j + t], device_id=(*chip, c), device_id_type=MESH))
    return cps


def _chip_exchange_start(arrs):
    n = len(arrs)
    lands = [lax.empty((3,) + a.shape[1:], a.dtype) for a in arrs]

    def body(*refs):
        ins, lands_in = refs[:n], refs[n:2 * n]
        send_sems, recv_sems = refs[2 * n], refs[2 * n + 1]
        token = refs[4 * n + 2]
        for cp in _exchange_copies(ins, lands_in, send_sems, recv_sems):
            cp.start()
        token[...] = jnp.zeros_like(token)

    hbm = lambda a: pltpu.HBM(a.shape, a.dtype)
    out = pl.pallas_call(
        body, name="chip_exchange_start", in_specs=[HBM] * (2 * n),
        out_specs=(SEM, SEM, *[HBM] * (2 * n), pl.BlockSpec(memory_space=pltpu.VMEM)),
        out_shape=(pltpu.SemaphoreType.DMA((3 * n,)), pltpu.SemaphoreType.DMA((3 * n,)), *[hbm(a) for a in arrs],
                   *[hbm(a) for a in lands], jax.ShapeDtypeStruct((8, 128), f32)),
        input_output_aliases={t: 2 + t for t in range(2 * n)},
        compiler_params=pltpu.CompilerParams(has_side_effects=DATAFLOW),
    )(*[pltpu.with_memory_space_constraint(a, pltpu.HBM) for a in list(arrs) + lands])
    return out[0], out[1], list(out[2:2 + n]), list(out[2 + n:2 + 2 * n]), out[2 + 2 * n]


def _chip_exchange_wait(send_sems, recv_sems, srcs, lands, after):
    n = len(srcs)

    def body(*refs):
        ins, lands_in = refs[:n], refs[n:2 * n]
        send_sems_ref, recv_sems_ref = refs[2 * n], refs[2 * n + 1]
        for cp in _exchange_copies(ins, lands_in, send_sems_ref, recv_sems_ref):
            cp.wait_send()
            cp.wait_recv()

    hbm = lambda a: pltpu.HBM(a.shape, a.dtype)
    out = pl.pallas_call(
        body, name="chip_exchange_wait", in_specs=[HBM] * (2 * n) + [SEM, SEM, ANY],
        out_specs=[HBM] * (2 * n), out_shape=[hbm(a) for a in list(srcs) + list(lands)],
        input_output_aliases={t: t for t in range(2 * n)},
        compiler_params=pltpu.CompilerParams(has_side_effects=DATAFLOW),
    )(*srcs, *lands, send_sems, recv_sems, after)
    return list(out[:n]), list(out[n:])


def _share_halves(bufs, by_cols, name):
    n = len(bufs)

    def body(*refs):
        ins, outs = refs[:n], refs[n:2 * n]
        send_sems, recv_sems = refs[2 * n], refs[2 * n + 1]
        x, y, c = _me()

        def part(ref, t, half):
            if by_cols[t]:
                w = bufs[t].shape[1] // 2
                return ref.at[:, pl.ds(pl.multiple_of(half * w, 128), w)]
            return ref.at[half]

        sends = [pltpu.make_async_remote_copy(src_ref=part(ins[t], t, c), dst_ref=part(outs[t], t, c), send_sem=send_sems.at[t],
                                              recv_sem=recv_sems.at[t], device_id=(x, y, 1 - c), device_id_type=MESH) for t in range(n)]
        for cp in sends:
            cp.start()
        for t in range(n):
            pltpu.make_async_remote_copy(src_ref=part(ins[t], t, c), dst_ref=part(outs[t], t, 1 - c), send_sem=send_sems.at[t],
                                         recv_sem=recv_sems.at[t], device_id=(x, y, c), device_id_type=MESH).wait_recv()
        for cp in sends:
            cp.wait_send()

    return pl.pallas_call(
        body, name=name, in_specs=[ANY] * n, out_specs=[ANY] * n,
        out_shape=[jax.ShapeDtypeStruct(a.shape, a.dtype) for a in bufs], input_output_aliases={t: t for t in range(n)},
        scratch_shapes=[pltpu.SemaphoreType.DMA((n,)), pltpu.SemaphoreType.DMA((n,))],
        compiler_params=pltpu.CompilerParams(has_side_effects=True),
    )(*bufs)


def _gather_vec(v):
    n = v.shape[1]

    def body(v_ref, o_ref, send_sems, recv_sems):
        x, y, c = _me()
        me = 4 * x + 2 * y + c
        cps = []
        for j in range(1, 8):
            fx, fy, fc = (j >> 2) & 1, (j >> 1) & 1, j & 1
            peer = (x ^ fx, y ^ fy, c ^ fc)
            cps.append(pltpu.make_async_remote_copy(
                src_ref=v_ref, dst_ref=o_ref.at[pl.ds(me, 1), :], send_sem=send_sems.at[j - 1], recv_sem=recv_sems.at[j - 1],
                device_id=peer, device_id_type=MESH))
        for cp in cps:
            cp.start()
        for j in range(1, 8):
            fx, fy, fc = (j >> 2) & 1, (j >> 1) & 1, j & 1
            src = 4 * (x ^ fx) + 2 * (y ^ fy) + (c ^ fc)
            pltpu.make_async_remote_copy(
                src_ref=v_ref, dst_ref=o_ref.at[pl.ds(src, 1), :], send_sem=send_sems.at[j - 1], recv_sem=recv_sems.at[j - 1],
                device_id=(x, y, c), device_id_type=MESH).wait_recv()
        for cp in cps:
            cp.wait_send()

    return pl.pallas_call(
        body, name="gather_vec", in_specs=[ANY], out_specs=ANY, out_shape=jax.ShapeDtypeStruct((8, n), v.dtype),
        scratch_shapes=[pltpu.SemaphoreType.DMA((7,)), pltpu.SemaphoreType.DMA((7,))],
        compiler_params=pltpu.CompilerParams(has_side_effects=True),
    )(v)


def _pair_sum(g, r, name, tr):
    L, hr, C = r.shape
    both_halves = g.ndim == 4

    def body(c_ref, g_ref, r_ref, o_ref):
        def strip(rows):
            gv = g_ref[0, 0, rows, :] if both_halves else g_ref[0, rows, :]
            o_ref[0, rows, :] = (gv + r_ref[0, rows, :]).astype(bf16)
        _for_strips(tr, strip)

    g_spec = (pl.BlockSpec((1, 1, tr, C), lambda s, i, c_ref: (s, c_ref[0], i, 0)) if both_halves
              else pl.BlockSpec((1, tr, C), lambda s, i, c_ref: (s, i, 0)))
    grid_spec = pltpu.PrefetchScalarGridSpec(
        num_scalar_prefetch=1, grid=(L, hr // tr),
        in_specs=[g_spec, pl.BlockSpec((1, tr, C), lambda s, i, c_ref: (s, i, 0))],
        out_specs=pl.BlockSpec((1, tr, C), lambda s, i, c_ref: (s, i, 0)))
    c = lax.axis_index("c").reshape(1).astype(jnp.int32)
    return pl.pallas_call(body, name=name, grid_spec=grid_spec, out_shape=jax.ShapeDtypeStruct((L, hr, C), bf16),
                          compiler_params=_params(("parallel", "parallel")))(c, g, r)


def _own_sum(p, got, name, transposed=False):
    _, hr, C = p.shape
    tr = SUM_ROWS
    c_full, c_pad = C // 128 * 128, -(-C // 128) * 128

    def total(p_ref, got_ref, rows):
        return ((p_ref[0, rows, :].astype(f32) + got_ref[0, rows, :].astype(f32)) + got_ref[1, rows, :].astype(f32)) \
            + got_ref[2, rows, :].astype(f32)

    def body(idx_ref, p_ref, got_ref, o_ref):
        def strip(rows):
            o_ref[0, rows, :] = total(p_ref, got_ref, rows)
        _for_strips(tr, strip)

    def body_t(idx_ref, p_ref, got_ref, o_ref, buf):
        if c_pad > c_full:
            buf[:, pl.ds(c_full, c_pad - c_full)] = jnp.zeros((tr, c_pad - c_full), f32)

        def strip(rows):
            buf[rows, pl.ds(0, C)] = total(p_ref, got_ref, rows)
        _for_strips(tr, strip)
        o_ref[...] = buf[...].T[:C]

    in_specs = [pl.BlockSpec((1, tr, C), lambda i, idx: (idx[0], i, 0)), pl.BlockSpec((3, tr, C), lambda i, idx: (0, i, 0))]
    x, y, c = _me()
    idx = jnp.stack([2 * x + y, c]).astype(jnp.int32)
    if transposed:
        grid_spec = pltpu.PrefetchScalarGridSpec(num_scalar_prefetch=1, grid=(hr // tr,), in_specs=in_specs,
                                                 out_specs=pl.BlockSpec((C, tr), lambda i, idx: (0, idx[1] * (hr // tr) + i)),
                                                 scratch_shapes=[pltpu.VMEM((tr, c_pad), f32)])
        return pl.pallas_call(body_t, name=name, grid_spec=grid_spec, out_shape=jax.ShapeDtypeStruct((C, 2 * hr), f32),
                              compiler_params=_params(("parallel",)))(idx, p, got)
    grid_spec = pltpu.PrefetchScalarGridSpec(num_scalar_prefetch=1, grid=(hr // tr,), in_specs=in_specs,
                                             out_specs=pl.BlockSpec((1, tr, C), lambda i, idx: (idx[1], i, 0)))
    return pl.pallas_call(body, name=name, grid_spec=grid_spec, out_shape=jax.ShapeDtypeStruct((2, hr, C), f32),
                          compiler_params=_params(("parallel",)))(idx, p, got)


def _adamw(w, g, m, v, name, tr):
    _, R, C = w.shape
    rs = min(8, tr)

    def body(w_ref, g_ref, m_ref, v_ref, d_ref, nm_ref, nv_ref):
        def strip(s, carry):
            rows = pl.ds(pl.multiple_of(s * rs, rs), rs)
            gv = g_ref[0, rows, :]
            mn = ADAM_B1 * m_ref[0, rows, :] + (1.0 - ADAM_B1) * gv
            vn = ADAM_B2 * v_ref[0, rows, :] + (1.0 - ADAM_B2) * (gv * gv)
            m_hat = mn / (1.0 - ADAM_B1 ** ADAM_STEP)
            v_hat = vn / (1.0 - ADAM_B2 ** ADAM_STEP)
            d_ref[0, rows, :] = -ADAM_LR * (m_hat / (jnp.sqrt(v_hat) + ADAM_EPS) + ADAM_WD * w_ref[0, rows, :])
            nm_ref[0, rows, :] = mn
            nv_ref[0, rows, :] = vn
            return carry

        if R % tr == 0:
            lax.fori_loop(0, tr // rs, strip, 0, unroll=min(2, tr // rs))
        else:
            lax.fori_loop(0, jnp.minimum(tr, R - pl.program_id(0) * tr) // rs, strip, 0)

    blk, grid = pl.BlockSpec((1, tr, C), lambda i: (0, i, 0)), (-(-R // tr),)
    o = jax.ShapeDtypeStruct((1, R, C), f32)
    return pl.pallas_call(body, name=name, grid=grid, in_specs=[blk] * 4, out_specs=[blk] * 3, out_shape=[o, o, o],
                          compiler_params=_params(("parallel",)))(w, g, m, v)


def _sum8(t):
    n = t.shape[1]

    def body(t_ref, o_ref):
        acc = t_ref[pl.ds(0, 1), :]
        for r in range(1, 8):
            acc = acc + t_ref[pl.ds(r, 1), :]
        o_ref[...] = acc

    return pl.pallas_call(body, name="sum_devices", out_shape=jax.ShapeDtypeStruct((1, n), f32))(t)


def _reduce_swap_start(g_main, g_dt, g_b):
    hr = g_main.shape[0]
    return _sibling_swap_start(g_main, g_dt, g_b.reshape(N_SHARD, 2, hr, g_b.shape[-1]))


def _reduce_start(swap_state, g_main, g_dt):
    hr = g_main.shape[0]
    send_sems, recv_sems, srcs, lands, _ = swap_state
    srcs, (r_main, r_dt, r_b) = _sibling_swap_wait(send_sems, recv_sems, srcs, lands, g_dt)
    p_main = _pair_sum(g_main[None], r_main[None], "pair_sum_main", SUM_ROWS // 4)
    p_dt = _pair_sum(g_dt[None], r_dt[None], "pair_sum_dt", SUM_ROWS)
    p_b = _pair_sum(srcs[2], r_b, "pair_sum_b", SUM_ROWS)
    p_in = jnp.transpose(_w_in_grad_full(p_main[0], p_dt[0]).reshape(hr, N_SHARD, W_IN_SHARD), (1, 0, 2))
    return _chip_exchange_start([p_in, p_b])


def _reduce_finish(state, after):
    send_sems, recv_sems, srcs, lands, _ = state
    (p_in, p_b), (got_in, got_b) = _chip_exchange_wait(send_sems, recv_sems, srcs, lands, after)
    mine_in, mine_b = _own_sum(p_in, got_in, "own_sum_in", transposed=True), _own_sum(p_b, got_b, "own_sum_b")
    full_in_t, full_b = _share_halves([mine_in, mine_b], [True, False], "share_halves")
    return full_in_t, full_b.reshape(-1, full_b.shape[-1])


def kernel(x, positions, norm1_w, w_in, conv_w, conv_b, dt_bias, a_log, d_skip, ssd_norm_w, w_br_ret, w_br_ssd, w_out, norm_f_w, loss_target, m_norm1_w, m_w_in, m_conv_w, m_conv_b, m_dt_bias, m_a_log, m_d_skip, m_ssd_norm_w, m_w_br_ret, m_w_br_ssd, m_w_out, m_norm_f_w, v_norm1_w, v_w_in, v_conv_w, v_conv_b, v_dt_bias, v_a_log, v_d_skip, v_ssd_norm_w, v_w_br_ret, v_w_br_ssd, v_w_out, v_norm_f_w):
    D = D_MODEL
    xi, yi, ci = _me()
    k = 2 * xi + yi
    me = 2 * k + ci
    weights = dict(norm1_w=norm1_w, w_in=w_in, conv_w=conv_w, conv_b=conv_b, dt_bias=dt_bias, a_log=a_log, d_skip=d_skip,
                   ssd_norm_w=ssd_norm_w, w_br_ret=w_br_ret, w_br_ssd=w_br_ssd, w_out=w_out, norm_f_w=norm_f_w)
    mom1 = dict(norm1_w=m_norm1_w, w_in=m_w_in, conv_w=m_conv_w, conv_b=m_conv_b, dt_bias=m_dt_bias, a_log=m_a_log, d_skip=m_d_skip,
                ssd_norm_w=m_ssd_norm_w, w_br_ret=m_w_br_ret, w_br_ssd=m_w_br_ssd, w_out=m_w_out, norm_f_w=m_norm_f_w)
    mom2 = dict(norm1_w=v_norm1_w, w_in=v_w_in, conv_w=v_conv_w, conv_b=v_conv_b, dt_bias=v_dt_bias, a_log=v_a_log, d_skip=v_d_skip,
                ssd_norm_w=v_ssd_norm_w, w_br_ret=v_w_br_ret, w_br_ssd=v_w_br_ssd, w_out=v_w_out, norm_f_w=v_norm_f_w)

    a_sh = w_in[0].astype(bf16)
    b_sh = jnp.concatenate([w_br_ret[0], w_br_ssd[0], w_out[0]], axis=0).astype(bf16)
    ga, gc = _gather_weights(a_sh, conv_w[0])
    ga, b_late = lax.optimization_barrier((ga, b_sh))
    late_send, late_recv, late_src, late_land, late_token = _gather_late_start(b_late)
    own = lambda g, s: lax.dynamic_update_slice_in_dim(g, s[None], k, axis=0)
    ga, gc = own(ga, a_sh), own(gc, conv_w[0])
    w_main, w_dt = _split_w_in(jnp.transpose(ga, (1, 0, 2)).reshape(D, IN_PROJ))
    conv_full = jnp.transpose(gc, (1, 0, 2)).reshape(SSD_CONV, CONV_DIM)

    def late_weights(after):
        gb = own(_gather_late_wait(late_send, late_recv, late_src, late_land, after), b_sh)
        return gb[:, 0:512].reshape(2048, D), gb[:, 512:1536].reshape(4096, D), gb[:, 1536:2048].reshape(2048, D)

    class Reducer:
        @staticmethod
        def rows(ht):
            half = lambda i: lax.dynamic_slice_in_dim(ht, i * (D // 2), D // 2, axis=0)
            return half(1 - ci), half(ci)

        @staticmethod
        def first(g_main, g_dt, g_w_br, g_w_bs, g_w_o):
            g_b = jnp.concatenate([g_w_br.reshape(N_SHARD, 512, D), g_w_bs.reshape(N_SHARD, 1024, D),
                                   g_w_o.reshape(N_SHARD, 512, D)], axis=1)
            return _reduce_swap_start(g_main, g_dt, g_b)

        second = staticmethod(_reduce_start)

    loss, grad_x, g, reduce_state = _local_step(x[0], positions[0], loss_target[0], norm1_w, w_main, w_dt, conv_full, conv_b, dt_bias,
                                                a_log, d_skip, ssd_norm_w, late_token, late_weights, norm_f_w, Reducer)

    grad_w_in_t, full_b = _reduce_finish(reduce_state, g["norm1_w"])
    grad_mats = dict(w_br_ret=full_b[0:512], w_br_ssd=full_b[512:1536], w_out=full_b[1536:2048])

    small = [(n, weights[n].size) for n in ("norm1_w", "conv_b", "dt_bias", "a_log", "d_skip", "ssd_norm_w", "norm_f_w")]
    parts = [jnp.pad(loss.reshape(1, 1), ((0, 0), (0, 127)))] + [g[n].reshape(1, -1) for n, _ in small] + [g["conv_w"].reshape(1, -1)]
    vec = jnp.concatenate(parts, axis=1)
    nv = vec.shape[1]
    nvp = -(-nv // 128) * 128
    vec = jnp.pad(vec, ((0, 0), (0, nvp - nv)))
    total = _sum8(lax.dynamic_update_slice_in_dim(_gather_vec(vec), vec, me, axis=0))
    loss_out = total[0, 0]
    off = 128
    grad_small = {}
    for n, sz in small:
        grad_small[n] = total[:, off:off + sz]
        off += sz
    g_conv = total[:, off:off + SSD_CONV * CONV_DIM].reshape(SSD_CONV, CONV_DIM)
    g_conv = lax.dynamic_slice_in_dim(g_conv, k * (CONV_DIM // N_SHARD), CONV_DIM // N_SHARD, axis=1)
    grad_small["conv_w"] = g_conv.reshape(1, -1)

    upd = {}
    for n in ("w_br_ret", "w_br_ssd", "w_out"):
        upd[n] = _adamw(weights[n], grad_mats[n][None], mom1[n], mom2[n], "adamw_" + n, tr=SUM_ROWS)
    tp = lambda t: jnp.swapaxes(t, 1, 2)
    upd["w_in"] = tuple(tp(t) for t in _adamw(tp(w_in), grad_w_in_t[None], tp(m_w_in), tp(v_w_in), "adamw_w_in", tr=256))
    grad_mats["w_in"] = tp(grad_w_in_t[None])
    names_small = [n for n, _ in small] + ["conv_w"]
    flat = lambda d: jnp.concatenate([d[n].reshape(1, -1) for n in names_small], axis=1)
    ns = sum(weights[n].size for n in names_small)
    nsp = -(-ns // 128) * 128
    padv = lambda t: jnp.pad(t, ((0, 0), (0, nsp - ns)))
    small_upd = _adamw(padv(flat(weights))[None], padv(flat(grad_small))[None], padv(flat(mom1))[None],
                       jnp.pad(flat(mom2), ((0, 0), (0, nsp - ns)), constant_values=1.0)[None], "adamw_small", 1)
    off = 0
    for n in names_small:
        sz = weights[n].size
        upd[n] = tuple(t[0, :, off:off + sz] for t in small_upd)
        off += sz

    order = ["norm1_w", "w_in", "conv_w", "conv_b", "dt_bias", "a_log", "d_skip", "ssd_norm_w", "w_br_ret", "w_br_ssd", "w_out", "norm_f_w"]
    grads_out = {**grad_mats, **grad_small}
    shp = lambda n, t: t.reshape(weights[n].shape)
    return (loss_out, grad_x[None], *[shp(n, grads_out[n]) for n in order], *[shp(n, upd[n][0]) for n in order],
            *[shp(n, upd[n][1]) for n in order], *[shp(n, upd[n][2]) for n in order])
```

```python
import jax
import jax.numpy as jnp
import numpy as np
from jax import lax
from jax.experimental import pallas as pl
from jax.experimental.pallas import tpu as pltpu

f32 = jnp.float32
bf16 = jnp.bfloat16
HIGHEST = lax.Precision.HIGHEST
MESH = pl.DeviceIdType.MESH

D_MODEL = 2048
EPS = 1e-6
CHUNK = 64
RET_HEADS = 8
RET_DK = 256
RET_HW = 4 * RET_DK
ROPE_THETA = 10000.0
SSD_WIDTH = 4096
SSD_GROUPS = 8
SSD_STATE = 128
SSD_GW = 512
SSD_GC = SSD_GW + 2 * SSD_STATE
SSD_HPG = 8
SSD_CONV = 4
CONV_DIM = 6144
SSD_HEADS = 64
LS = 128

C_RET, C_Z, C_GATES, C_XBC = 0, 8192, 12288, 16384
N_MAIN = 22528
DT_OFF = 18432
IN_PROJ = 22592
N_SHARD = 4
W_IN_SHARD = IN_PROJ // N_SHARD

ADAM_LR, ADAM_B1, ADAM_B2, ADAM_EPS, ADAM_WD, ADAM_STEP = 0.001, 0.9, 0.999, 1e-08, 0.01, 10

VMEM_LIMIT = 56 * 1024 * 1024
SUM_ROWS = 128
ANY = pl.BlockSpec(memory_space=pl.ANY)


def _params(dims):
    return pltpu.CompilerParams(dimension_semantics=dims, vmem_limit_bytes=VMEM_LIMIT)


def _silu(x):
    return x * jax.nn.sigmoid(x)


def _dsilu(x):
    s = jax.nn.sigmoid(x)
    return s * (1.0 + x * (1.0 - s))


def _nt(a, b):
    return lax.dot_general(a, b, (((1,), (1,)), ((), ())), preferred_element_type=f32)


def _tn(a, b):
    return lax.dot_general(a, b, (((0,), (0,)), ((), ())), preferred_element_type=f32)


def _nn(a, b):
    return jnp.dot(a, b, preferred_element_type=f32)


def _hi(a, b):
    return jnp.dot(a, b, precision=HIGHEST, preferred_element_type=f32)


def _split(a):
    hi = a.astype(bf16)
    return hi, (a - hi.astype(f32)).astype(bf16)


def _sel_r(a, sel):
    hi, lo = _split(a)
    return _nn(hi, sel) + _nn(lo, sel)


def _sel_l(sel, a):
    hi, lo = _split(a)
    return _nn(sel, hi) + _nn(sel, lo)


def _rows_to_cols(t, eye):
    hi = t.astype(bf16)
    r1 = t - hi.astype(f32)
    mid = r1.astype(bf16)
    lo = (r1 - mid.astype(f32)).astype(bf16)
    return _nt(eye, hi) + _nt(eye, mid) + _nt(eye, lo)


def _xbc_group_major(t):
    R = t.shape[0]
    nb = SSD_GROUPS * SSD_STATE
    parts = [t[:, :SSD_WIDTH].reshape(R, SSD_GROUPS, SSD_GW), t[:, SSD_WIDTH:SSD_WIDTH + nb].reshape(R, SSD_GROUPS, SSD_STATE),
             t[:, SSD_WIDTH + nb:].reshape(R, SSD_GROUPS, SSD_STATE)]
    return jnp.concatenate(parts, axis=2).reshape(R, CONV_DIM)


def _xbc_original(t):
    R = t.shape[0]
    g = t.reshape(R, SSD_GROUPS, SSD_GC)
    parts = [g[:, :, :SSD_GW].reshape(R, SSD_WIDTH), g[:, :, SSD_GW:SSD_GW + SSD_STATE].reshape(R, SSD_GROUPS * SSD_STATE),
             g[:, :, SSD_GW + SSD_STATE:].reshape(R, SSD_GROUPS * SSD_STATE)]
    return jnp.concatenate(parts, axis=1)


def _main_segments():
    segs = []
    for h in range(RET_HEADS):
        segs += [(base + RET_DK * h, RET_DK) for base in (0, 2048, 4096, 6144)]
    segs += [(8192, SSD_WIDTH), (DT_OFF + SSD_HEADS, 2 * D_MODEL)]
    nb = SSD_GROUPS * SSD_STATE
    for g in range(SSD_GROUPS):
        segs += [(12288 + SSD_GW * g, SSD_GW), (12288 + SSD_WIDTH + SSD_STATE * g, SSD_STATE),
                 (12288 + SSD_WIDTH + nb + SSD_STATE * g, SSD_STATE)]
    return segs


def _w_main_from_shards(shards):
    def cols(lo, hi):
        out = []
        while lo < hi:
            s = lo // W_IN_SHARD
            top = min(hi, (s + 1) * W_IN_SHARD)
            out.append(shards[s][:, lo - s * W_IN_SHARD:top - s * W_IN_SHARD])
            lo = top
        return out

    main = jnp.concatenate([p for s, n in _main_segments() for p in cols(s, s + n)], axis=1)
    w_dt = jnp.pad(jnp.concatenate(cols(DT_OFF, DT_OFF + SSD_HEADS), axis=1), ((0, 0), (0, 128 - SSD_HEADS)))
    return main, w_dt


def _w_in_grad_full(g_main, g_dt):
    D = g_main.shape[0]
    ret = jnp.transpose(g_main[:, :C_Z].reshape(D, RET_HEADS, 4, RET_DK), (0, 2, 1, 3)).reshape(D, C_Z)
    return jnp.concatenate([ret, g_main[:, C_Z:C_GATES], _xbc_original(g_main[:, C_XBC:]), g_dt[:, :SSD_HEADS],
                            g_main[:, C_GATES:C_XBC]], axis=1)


def _mm(pairs, M, N, *, tm, tn, out_dtype, name, tb=False, row_off=None):
    P = len(pairs)
    nks = [K // tk for (_, _, _, _, _, K, tk) in pairs]
    starts = [int(s) for s in np.cumsum([0] + nks[:-1])]
    KT = int(sum(nks))
    npf = 0 if row_off is None else 1
    in_specs, args = [], []
    for (a, a_cb, b, b_kb, b_nb, K, tk), s, nk in zip(pairs, starts, nks):
        def kk(k, s=s, nk=nk):
            return jnp.clip(k - s, 0, nk - 1)
        in_specs.append(pl.BlockSpec((tm, tk), lambda m, n, k, *pf, kk=kk, a_cb=a_cb: (m + (pf[0][0] if pf else 0), a_cb + kk(k))))
        if tb:
            in_specs.append(pl.BlockSpec((tn, tk), lambda m, n, k, *pf, kk=kk, b_kb=b_kb, b_nb=b_nb: (b_nb + n, b_kb + kk(k))))
        else:
            in_specs.append(pl.BlockSpec((tk, tn), lambda m, n, k, *pf, kk=kk, b_kb=b_kb, b_nb=b_nb: (b_kb + kk(k), b_nb + n)))
        args += [a, b]

    def body(*refs):
        refs = refs[npf:]
        o_ref = refs[2 * P]
        k = pl.program_id(2)

        def prod(i):
            a = refs[2 * i][...].astype(bf16)
            b = refs[2 * i + 1][...].astype(bf16)
            return _nt(a, b) if tb else _nn(a, b)

        if KT == 1:
            o_ref[...] = prod(0).astype(out_dtype)
            return
        acc = refs[2 * P + 1]

        @pl.when(k == 0)
        def _():
            acc[...] = jnp.zeros_like(acc)

        for i in range(P):
            @pl.when((k >= starts[i]) & (k < starts[i] + nks[i]))
            def _(i=i):
                acc[...] += prod(i)

        @pl.when(k == KT - 1)
        def _():
            o_ref[...] = acc[...].astype(out_dtype)

    grid_spec = pltpu.PrefetchScalarGridSpec(
        num_scalar_prefetch=npf, grid=(M // tm, N // tn, KT), in_specs=in_specs,
        out_specs=pl.BlockSpec((tm, tn), lambda m, n, k, *pf: (m, n)),
        scratch_shapes=[] if KT == 1 else [pltpu.VMEM((tm, tn), f32)])
    return pl.pallas_call(
        body, name=name, grid_spec=grid_spec, out_shape=jax.ShapeDtypeStruct((M, N), out_dtype),
        compiler_params=_params(("parallel", "parallel", "arbitrary")),
    )(*([] if row_off is None else [row_off]), *args)


def _mm1(a, b, *, tm, tn, tk, out_dtype, name, tb=False):
    M, K = a.shape
    N = b.shape[0] if tb else b.shape[1]
    return _mm([(a, 0, b, 0, 0, K, tk)], M, N, tm=tm, tn=tn, out_dtype=out_dtype, name=name, tb=tb)


RS = 16
CS = 32


def _for_strips(n_rows, fn, rs=RS, unroll=4):
    def step(s, carry):
        fn(pl.ds(pl.multiple_of(s * rs, rs), rs))
        return carry
    n = n_rows // rs
    lax.fori_loop(0, n, step, 0, unroll=min(unroll, n))


def _norm1_fwd(x, w, tr):
    S, D = x.shape

    def body(x_ref, w_ref, h_ref, ht_ref):
        def strip(rows):
            xv = x_ref[rows, :]
            r = lax.rsqrt(jnp.mean(xv * xv, axis=-1, keepdims=True) + EPS)
            h_ref[rows, :] = (xv * r * w_ref[...]).astype(bf16)
        _for_strips(tr, strip)
        ht_ref[...] = h_ref[...].T

    return pl.pallas_call(
        body, name="norm1_fwd", grid=(S // tr,),
        in_specs=[pl.BlockSpec((tr, D), lambda i: (i, 0)), pl.BlockSpec((1, D), lambda i: (0, 0))],
        out_specs=[pl.BlockSpec((tr, D), lambda i: (i, 0)), pl.BlockSpec((D, tr), lambda i: (0, i))],
        out_shape=[jax.ShapeDtypeStruct((S, D), bf16), jax.ShapeDtypeStruct((D, S), bf16)], compiler_params=_params(("parallel",)),
    )(x, w)


def _norm1_bwd(x, w, dh, dx2, tr):
    S, D = x.shape

    def body(x_ref, w_ref, dh_ref, dx2_ref, gx_ref, gw_ref, acc):
        @pl.when(pl.program_id(0) == 0)
        def _():
            acc[...] = jnp.zeros_like(acc)

        def strip(rows):
            xv = x_ref[rows, :]
            r = lax.rsqrt(jnp.mean(xv * xv, axis=-1, keepdims=True) + EPS)
            xh = xv * r
            dhv = dh_ref[rows, :]
            acc[...] += dhv * xh
            dxh = dhv * w_ref[...]
            gx_ref[rows, :] = dx2_ref[rows, :] + r * (dxh - xh * jnp.mean(dxh * xh, axis=-1, keepdims=True))
        _for_strips(tr, strip)

        @pl.when(pl.program_id(0) == S // tr - 1)
        def _():
            gw_ref[...] = jnp.sum(acc[...], axis=0, keepdims=True)

    row = pl.BlockSpec((tr, D), lambda i: (i, 0))
    vec = pl.BlockSpec((1, D), lambda i: (0, 0))
    return pl.pallas_call(
        body, name="norm1_bwd", grid=(S // tr,), in_specs=[row, vec, row, row], out_specs=[row, vec],
        out_shape=[jax.ShapeDtypeStruct((S, D), f32), jax.ShapeDtypeStruct((1, D), f32)],
        scratch_shapes=[pltpu.VMEM((RS, D), f32)], compiler_params=_params(("arbitrary",)),
    )(x, w, dh, dx2)


def _final_fwd_bwd(x, mo, target, wf, tr):
    S, D = x.shape

    def body(x_ref, mo_ref, t_ref, w_ref, dx2_ref, dx2b_ref, loss_ref, gw_ref, acc, lacc):
        @pl.when(pl.program_id(0) == 0)
        def _():
            acc[...] = jnp.zeros_like(acc)
            lacc[...] = jnp.zeros_like(lacc)

        def strip(rows):
            x2 = x_ref[rows, :] + mo_ref[rows, :]
            r = lax.rsqrt(jnp.mean(x2 * x2, axis=-1, keepdims=True) + EPS)
            xh = x2 * r
            wv = w_ref[...]
            err = xh * wv - t_ref[rows, :]
            lacc[...] += jnp.mean(err * err, axis=-1, keepdims=True)
            dy = err * (1.0 / D)
            acc[...] += dy * xh
            dxh = dy * wv
            dx2 = r * (dxh - xh * jnp.mean(dxh * xh, axis=-1, keepdims=True))
            dx2_ref[rows, :] = dx2
            dx2b_ref[rows, :] = dx2.astype(bf16)
        _for_strips(tr, strip)

        @pl.when(pl.program_id(0) == S // tr - 1)
        def _():
            gw_ref[...] = jnp.sum(acc[...], axis=0, keepdims=True)
            loss_ref[...] = 0.5 * jnp.sum(lacc[...], axis=0, keepdims=True)

    row = pl.BlockSpec((tr, D), lambda i: (i, 0))
    vec = pl.BlockSpec((1, D), lambda i: (0, 0))
    return pl.pallas_call(
        body, name="final_norm_loss", grid=(S // tr,), in_specs=[row, row, row, vec],
        out_specs=[row, row, pl.BlockSpec((1, 1), lambda i: (0, 0)), vec],
        out_shape=[jax.ShapeDtypeStruct((S, D), f32), jax.ShapeDtypeStruct((S, D), bf16), jax.ShapeDtypeStruct((1, 1), f32),
                   jax.ShapeDtypeStruct((1, D), f32)],
        scratch_shapes=[pltpu.VMEM((RS, D), f32), pltpu.VMEM((RS, 1), f32)], compiler_params=_params(("arbitrary",)),
    )(x, mo, target, wf)


def _merge_fwd(p_r, p_s, proj, tr):
    S, D = p_r.shape

    def body(pr_ref, ps_ref, g_ref, o_ref, ot_ref):
        def strip(rows):
            gr, gs = g_ref[rows, pl.ds(0, D)], g_ref[rows, pl.ds(D, D)]
            o_ref[rows, :] = (jax.nn.sigmoid(gr) * pr_ref[rows, :] + jax.nn.sigmoid(gs) * ps_ref[rows, :]).astype(bf16)
        _for_strips(tr, strip)
        ot_ref[...] = o_ref[...].T

    row = pl.BlockSpec((tr, D), lambda i: (i, 0))
    return pl.pallas_call(
        body, name="merge_fwd", grid=(S // tr,),
        in_specs=[row, row, pl.BlockSpec((tr, 2 * D), lambda i: (i, C_GATES // (2 * D)))],
        out_specs=[row, pl.BlockSpec((D, tr), lambda i: (0, i))],
        out_shape=[jax.ShapeDtypeStruct((S, D), bf16), jax.ShapeDtypeStruct((D, S), bf16)], compiler_params=_params(("parallel",)),
    )(p_r, p_s, proj)


def _merge_bwd(dm, p_r, p_s, proj, tr):
    S, D = p_r.shape

    def body(dm_ref, pr_ref, ps_ref, g_ref, dpr_ref, dps_ref, dproj_ref):
        def strip(rows):
            dmv = dm_ref[rows, :]
            sr = jax.nn.sigmoid(g_ref[rows, pl.ds(0, D)])
            ss = jax.nn.sigmoid(g_ref[rows, pl.ds(D, D)])
            dpr_ref[rows, :] = (dmv * sr).astype(bf16)
            dps_ref[rows, :] = (dmv * ss).astype(bf16)
            dproj_ref[rows, pl.ds(0, D)] = (dmv * pr_ref[rows, :] * sr * (1.0 - sr)).astype(bf16)
            dproj_ref[rows, pl.ds(D, D)] = (dmv * ps_ref[rows, :] * ss * (1.0 - ss)).astype(bf16)
        _for_strips(tr, strip)

    row = pl.BlockSpec((tr, D), lambda i: (i, 0))
    gates = pl.BlockSpec((tr, 2 * D), lambda i: (i, C_GATES // (2 * D)))
    o = jax.ShapeDtypeStruct((S, D), bf16)
    return pl.pallas_call(
        body, name="merge_bwd", grid=(S // tr,), in_specs=[row, row, row, gates],
        out_specs=[row, row, gates], out_shape=[o, o, jax.ShapeDtypeStruct((S, N_MAIN), bf16)],
        compiler_params=_params(("parallel",)),
    )(dm, p_r, p_s, proj)


def _ssd_norm_fwd(y, proj, w, tr):
    S, W = y.shape

    def body(y_ref, z_ref, w_ref, o_ref, ot_ref):
        def strip(rows):
            u = y_ref[rows, :] * _silu(z_ref[rows, :])
            r = lax.rsqrt(jnp.mean(u * u, axis=-1, keepdims=True) + EPS)
            o_ref[rows, :] = (u * r * w_ref[...]).astype(bf16)
        _for_strips(tr, strip)
        ot_ref[...] = o_ref[...].T

    row = pl.BlockSpec((tr, W), lambda i: (i, 0))
    return pl.pallas_call(
        body, name="ssd_norm_fwd", grid=(S // tr,),
        in_specs=[row, pl.BlockSpec((tr, W), lambda i: (i, C_Z // W)), pl.BlockSpec((1, W), lambda i: (0, 0))],
        out_specs=[row, pl.BlockSpec((W, tr), lambda i: (0, i))],
        out_shape=[jax.ShapeDtypeStruct((S, W), bf16), jax.ShapeDtypeStruct((W, S), bf16)], compiler_params=_params(("parallel",)),
    )(y, proj, w)


def _ssd_norm_bwd(y, proj, w, dys, dproj, tr):
    S, W = y.shape

    def body(y_ref, z_ref, w_ref, d_ref, _, dy_ref, dz_ref, gw_ref, acc):
        @pl.when(pl.program_id(0) == 0)
        def _():
            acc[...] = jnp.zeros_like(acc)

        def strip(rows):
            yv, zv, dv = y_ref[rows, :], z_ref[rows, :], d_ref[rows, :]
            sz = _silu(zv)
            u = yv * sz
            r = lax.rsqrt(jnp.mean(u * u, axis=-1, keepdims=True) + EPS)
            un = u * r
            acc[...] += dv * un
            dun = dv * w_ref[...]
            du = r * (dun - un * jnp.mean(dun * un, axis=-1, keepdims=True))
            dy_ref[rows, :] = du * sz
            dz_ref[rows, :] = (du * yv * _dsilu(zv)).astype(bf16)
        _for_strips(tr, strip)

        @pl.when(pl.program_id(0) == S // tr - 1)
        def _():
            gw_ref[...] = jnp.sum(acc[...], axis=0, keepdims=True)

    row = pl.BlockSpec((tr, W), lambda i: (i, 0))
    zcol = pl.BlockSpec((tr, W), lambda i: (i, C_Z // W))
    vec = pl.BlockSpec((1, W), lambda i: (0, 0))
    return pl.pallas_call(
        body, name="ssd_norm_bwd", grid=(S // tr,),
        in_specs=[row, zcol, vec, row, ANY], out_specs=[row, zcol, vec],
        out_shape=[jax.ShapeDtypeStruct((S, W), f32), jax.ShapeDtypeStruct(dproj.shape, bf16), jax.ShapeDtypeStruct((1, W), f32)],
        input_output_aliases={4: 1}, scratch_shapes=[pltpu.VMEM((RS, W), f32)], compiler_params=_params(("arbitrary",)),
    )(y, proj, w, dys, dproj)


def _rope(t, cos, sin):
    t1, t2 = t[:, :128], t[:, 128:]
    return jnp.concatenate([t1 * cos - t2 * sin, t2 * cos + t1 * sin], axis=1)


def _rope_t(d, cos, sin):
    d1, d2 = d[:, :128], d[:, 128:]
    return jnp.concatenate([d1 * cos + d2 * sin, d2 * cos - d1 * sin], axis=1)


def _ret_specs(tb, rev_nb=None):
    def blk(i):
        return i if rev_nb is None else rev_nb - 1 - i
    head = pl.BlockSpec((tb, RET_HW), lambda h, i: (blk(i), h))
    tab = pl.BlockSpec((tb, 128), lambda h, i: (blk(i), 0))
    mat = pl.BlockSpec((1, CHUNK, CHUNK), lambda h, i: (h, 0, 0))
    vec = pl.BlockSpec((1, CHUNK, 1), lambda h, i: (h, 0, 0))
    one = pl.BlockSpec((1, 1, 1), lambda h, i: (h, 0, 0))
    own = pl.BlockSpec((tb, RET_DK), lambda h, i: (blk(i), h))
    st = pl.BlockSpec((1, tb // CHUNK, RET_DK, RET_DK), lambda h, i: (h, blk(i), 0, 0))
    return head, tab, mat, vec, one, own, st


def _ret_fwd(proj, cos, sin, intra, qdec, kdec, cdec, tb):
    S = proj.shape[0]
    nc = S // CHUNK
    scale = RET_DK ** -0.5
    dk = RET_DK

    def body(p_ref, cos_ref, sin_ref, m_ref, qd_ref, kd_ref, cd_ref, y_ref, yr_ref, yrt_ref, st_ref, st):
        @pl.when(pl.program_id(1) == 0)
        def _():
            st[...] = jnp.zeros_like(st)

        mm, qd, kd, cd = m_ref[0], qd_ref[0], kd_ref[0], cd_ref[0]

        def chunk(c, carry):
            rows = pl.ds(pl.multiple_of(c * CHUNK, CHUNK), CHUNK)
            cs, sn = cos_ref[rows, :], sin_ref[rows, :]
            qr = _rope(p_ref[rows, pl.ds(0, dk)], cs, sn)
            kr = _rope(p_ref[rows, pl.ds(dk, dk)], cs, sn) * scale
            qb, kb, vb = qr.astype(bf16), kr.astype(bf16), p_ref[rows, pl.ds(2 * dk, dk)].astype(bf16)
            stb = st[...].astype(bf16)
            st_ref[0, c] = stb
            sc = (_nt(qb, kb) * mm).astype(bf16)
            y = _nn(sc, vb) + _nn(qb, stb) * qd
            st[...] = st[...] * cd + _tn((kr * kd).astype(bf16), vb)
            y_ref[rows, :] = y
            mu = jnp.mean(y, axis=-1, keepdims=True)
            yc = y - mu
            var = jnp.mean(yc * yc, axis=-1, keepdims=True)
            yr_ref[rows, :] = (yc * lax.rsqrt(var + EPS) * _silu(p_ref[rows, pl.ds(3 * dk, dk)])).astype(bf16)
            return carry

        lax.fori_loop(0, tb // CHUNK, chunk, 0, unroll=min(4, tb // CHUNK))
        yrt_ref[...] = yr_ref[...].T

    head, tab, mat, vec, one, own, stspec = _ret_specs(tb)
    return pl.pallas_call(
        body, name="ret_fwd", grid=(RET_HEADS, S // tb),
        in_specs=[head, tab, tab, mat, vec, vec, one],
        out_specs=[own, own, pl.BlockSpec((RET_DK, tb), lambda h, i: (h, i)), stspec],
        out_shape=[jax.ShapeDtypeStruct((S, 2048), f32), jax.ShapeDtypeStruct((S, 2048), bf16), jax.ShapeDtypeStruct((2048, S), bf16),
                   jax.ShapeDtypeStruct((RET_HEADS, nc, dk, dk), bf16)],
        scratch_shapes=[pltpu.VMEM((dk, dk), f32)], compiler_params=_params(("parallel", "arbitrary")),
    )(proj, cos, sin, intra, qdec, kdec, cdec)


def _ret_bwd(proj, cos, sin, intra, qdec, kdec, cdec, y, dyr, states, dproj, tb):
    S = proj.shape[0]
    nb = S // tb
    nck = tb // CHUNK
    scale = RET_DK ** -0.5
    dk = RET_DK

    def body(p_ref, cos_ref, sin_ref, m_ref, qd_ref, kd_ref, cd_ref, y_ref, dyr_ref, st_ref, _, o_ref, dst):
        @pl.when(pl.program_id(1) == 0)
        def _():
            dst[...] = jnp.zeros_like(dst)

        mm, qd, kd, cd = m_ref[0], qd_ref[0], kd_ref[0], cd_ref[0]

        def chunk(cc, carry):
            c = nck - 1 - cc
            rows = pl.ds(pl.multiple_of(c * CHUNK, CHUNK), CHUNK)
            cs, sn = cos_ref[rows, :], sin_ref[rows, :]
            qr = _rope(p_ref[rows, pl.ds(0, dk)], cs, sn)
            kr = _rope(p_ref[rows, pl.ds(dk, dk)], cs, sn) * scale
            qb, kb, vb = qr.astype(bf16), kr.astype(bf16), p_ref[rows, pl.ds(2 * dk, dk)].astype(bf16)
            kdb = (kr * kd).astype(bf16)
            stb = st_ref[0, c]
            yv, gv, dyrv = y_ref[rows, :], p_ref[rows, pl.ds(3 * dk, dk)], dyr_ref[rows, :]
            mu = jnp.mean(yv, axis=-1, keepdims=True)
            yc = yv - mu
            rstd = lax.rsqrt(jnp.mean(yc * yc, axis=-1, keepdims=True) + EPS)
            yn = yc * rstd
            o_ref[rows, pl.ds(3 * dk, dk)] = (dyrv * yn * _dsilu(gv)).astype(bf16)
            dyn = dyrv * _silu(gv)
            dy = rstd * (dyn - jnp.mean(dyn, axis=-1, keepdims=True) - yn * jnp.mean(dyn * yn, axis=-1, keepdims=True))
            dyb = dy.astype(bf16)
            dyqb = (dy * qd).astype(bf16)
            dstb = dst[...].astype(bf16)
            sct = (_nt(kb, qb) * mm).astype(bf16)
            ds = (_nt(dyb, vb) * mm).astype(bf16)
            dsT = (_nt(vb, dyb) * mm).astype(bf16)
            dv = _nn(sct, dyb) + _nn(kdb, dstb)
            dqr = _nn(ds, kb) + _nt(dyqb, stb)
            dkr = _nn(dsT, qb) + _nt(vb, dstb) * kd
            dst[...] = dst[...] * cd + _tn(qb, dyqb)
            o_ref[rows, pl.ds(0, dk)] = _rope_t(dqr, cs, sn).astype(bf16)
            o_ref[rows, pl.ds(dk, dk)] = (_rope_t(dkr, cs, sn) * scale).astype(bf16)
            o_ref[rows, pl.ds(2 * dk, dk)] = dv.astype(bf16)
            return carry

        lax.fori_loop(0, nck, chunk, 0, unroll=min(4, nck))

    head, tab, mat, vec, one, own, stspec = _ret_specs(tb, rev_nb=nb)
    return pl.pallas_call(
        body, name="ret_bwd", grid=(RET_HEADS, nb),
        in_specs=[head, tab, tab, mat, vec, vec, one, own, own, stspec, ANY],
        out_specs=head, out_shape=jax.ShapeDtypeStruct(dproj.shape, bf16), input_output_aliases={10: 0},
        scratch_shapes=[pltpu.VMEM((dk, dk), f32)], compiler_params=_params(("parallel", "arbitrary")),
    )(proj, cos, sin, intra, qdec, kdec, cdec, y, dyr, states, dproj)


def _conv_fwd(proj, conv_w, conv_b, tb, cw):
    S = proj.shape[0]
    off = C_XBC // cw

    def body(x_ref, halo_ref, w_ref, b_ref, o_ref, xe):
        xe[pl.ds(0, 8), :] = jnp.where(pl.program_id(1) == 0, 0.0, halo_ref[...])
        xe[pl.ds(8, CS), :] = x_ref[pl.ds(0, CS), :]
        ws = [w_ref[pl.ds(j, 1), :] for j in range(SSD_CONV)]
        for s in range(tb // CS):
            tap = (lambda j: xe[pl.ds(5 + j, CS), :]) if s == 0 else (lambda j, s=s: x_ref[pl.ds(s * CS - 3 + j, CS), :])
            acc = b_ref[...] + ws[0] * tap(0)
            for j in range(1, SSD_CONV):
                acc = acc + ws[j] * tap(j)
            o_ref[pl.ds(s * CS, CS), :] = acc

    return pl.pallas_call(
        body, name="conv_fwd", grid=(CONV_DIM // cw, S // tb),
        in_specs=[pl.BlockSpec((tb, cw), lambda j, i: (i, off + j)),
                  pl.BlockSpec((8, cw), lambda j, i: (jnp.maximum(i * (tb // 8) - 1, 0), off + j)),
                  pl.BlockSpec((SSD_CONV, cw), lambda j, i: (0, j)), pl.BlockSpec((1, cw), lambda j, i: (0, j))],
        out_specs=pl.BlockSpec((tb, cw), lambda j, i: (i, j)),
        out_shape=jax.ShapeDtypeStruct((S, CONV_DIM), f32),
        scratch_shapes=[pltpu.VMEM((CS + 8, cw), f32)], compiler_params=_params(("parallel", "arbitrary")),
    )(proj, proj, conv_w, conv_b)


def _conv_bwd(dpre, proj, conv_w, dproj, tb, cw):
    S, n = dpre.shape
    nb = S // tb
    xoff = C_XBC // cw

    def body(d_ref, dh_ref, x_ref, xh_ref, w_ref, _, dx_ref, gw_ref, gb_ref, de, xe, accw, accb):
        i = pl.program_id(1)

        @pl.when(i == 0)
        def _():
            accw[...] = jnp.zeros_like(accw)
            accb[...] = jnp.zeros_like(accb)

        ns = tb // CS
        de[pl.ds(0, CS), :] = d_ref[pl.ds(tb - CS, CS), :]
        de[pl.ds(CS, 8), :] = jnp.where(i == nb - 1, 0.0, dh_ref[...])
        xe[pl.ds(0, 8), :] = jnp.where(i == 0, 0.0, xh_ref[...])
        xe[pl.ds(8, CS), :] = x_ref[pl.ds(0, CS), :]
        ws = [w_ref[pl.ds(j, 1), :] for j in range(SSD_CONV)]
        fold = lambda p: sum(p[8 * q:8 * (q + 1)] for q in range(1, CS // 8)) + p[0:8]
        for s in range(ns):
            dv = d_ref[pl.ds(s * CS, CS), :]
            ahead = (lambda o: de[pl.ds(o, CS), :]) if s == ns - 1 else (lambda o, s=s: d_ref[pl.ds(s * CS + o, CS), :])
            xtap = (lambda j: xe[pl.ds(5 + j, CS), :]) if s == 0 else (lambda j, s=s: x_ref[pl.ds(s * CS - 3 + j, CS), :])
            acc = ws[SSD_CONV - 1] * dv
            for j in range(SSD_CONV - 1):
                acc = acc + ws[j] * ahead(3 - j)
            dx_ref[pl.ds(s * CS, CS), :] = acc.astype(bf16)
            accb[...] += fold(dv)
            for j in range(SSD_CONV):
                accw[j] += fold(dv * xtap(j))

        @pl.when(i == nb - 1)
        def _():
            gb_ref[...] = jnp.sum(accb[...], axis=0, keepdims=True)
            for j in range(SSD_CONV):
                gw_ref[pl.ds(j, 1), :] = jnp.sum(accw[j], axis=0, keepdims=True)

    return pl.pallas_call(
        body, name="conv_bwd", grid=(n // cw, nb),
        in_specs=[pl.BlockSpec((tb, cw), lambda j, i: (i, j)),
                  pl.BlockSpec((8, cw), lambda j, i: (jnp.minimum((i + 1) * (tb // 8), S // 8 - 1), j)),
                  pl.BlockSpec((tb, cw), lambda j, i: (i, xoff + j)),
                  pl.BlockSpec((8, cw), lambda j, i: (jnp.maximum(i * (tb // 8) - 1, 0), xoff + j)),
                  pl.BlockSpec((SSD_CONV, cw), lambda j, i: (0, j)), ANY],
        out_specs=[pl.BlockSpec((tb, cw), lambda j, i: (i, xoff + j)), pl.BlockSpec((SSD_CONV, cw), lambda j, i: (0, j)),
                   pl.BlockSpec((1, cw), lambda j, i: (0, j))],
        out_shape=[jax.ShapeDtypeStruct(dproj.shape, bf16), jax.ShapeDtypeStruct((SSD_CONV, n), f32), jax.ShapeDtypeStruct((1, n), f32)],
        input_output_aliases={5: 0},
        scratch_shapes=[pltpu.VMEM((CS + 8, cw), f32), pltpu.VMEM((CS + 8, cw), f32), pltpu.VMEM((SSD_CONV, 8, cw), f32),
                        pltpu.VMEM((8, cw), f32)],
        compiler_params=_params(("parallel", "arbitrary")),
    )(dpre, dpre, proj, proj, conv_w, dproj)


def _dt_prep(dt_raw, dt_bias, a_log, tb):
    S = dt_raw.shape[0]

    def body(r_ref, b_ref, al_ref, dt_ref, sg_ref, ac_ref):
        li = lax.broadcasted_iota(jnp.int32, (LS, LS), 0)
        si = lax.broadcasted_iota(jnp.int32, (LS, LS), 1)
        tri = (li >= si).astype(f32)
        neg_a = -jnp.exp(al_ref[...])
        for c in range(tb // LS):
            rows = pl.ds(c * LS, LS)
            xv = r_ref[rows, :] + b_ref[...]
            dtv = jax.nn.softplus(xv)
            dt_ref[rows, :] = dtv
            sg_ref[rows, :] = jax.nn.sigmoid(xv)
            ac_ref[rows, :] = _hi(tri, dtv * neg_a)

    row = pl.BlockSpec((tb, 128), lambda i: (i, 0))
    vec = pl.BlockSpec((1, 128), lambda i: (0, 0))
    o = jax.ShapeDtypeStruct((S, 128), f32)
    return pl.pallas_call(body, name="dt_prep", grid=(S // tb,), in_specs=[row, vec, vec], out_specs=[row, row, row],
                          out_shape=[o, o, o], compiler_params=_params(("parallel",)))(dt_raw, dt_bias, a_log)


def _group_major(t):
    S = t.shape[0]
    return jnp.transpose(t[:, :SSD_HEADS].reshape(S, SSD_GROUPS, SSD_HPG), (1, 0, 2))


def _group_major_t(t):
    S = t.shape[0]
    return jnp.transpose(t[:, :SSD_HEADS].reshape(S // LS, LS, SSD_GROUPS, SSD_HPG), (2, 0, 3, 1))


def _ssd_specs(tb, rev_nb=None):
    def blk(i):
        return i if rev_nb is None else rev_nb - 1 - i
    grp = pl.BlockSpec((tb, SSD_GC), lambda g, i: (blk(i), g))
    xs = pl.BlockSpec((tb, SSD_GW), lambda g, i: (blk(i), g))
    ph = pl.BlockSpec((1, tb, SSD_HPG), lambda g, i: (g, blk(i), 0))
    pht = pl.BlockSpec((1, tb // LS, SSD_HPG, LS), lambda g, i: (g, blk(i), 0, 0))
    gvec = pl.BlockSpec((1, 1, SSD_GW), lambda g, i: (g, 0, 0))
    ex = pl.BlockSpec((SSD_HPG, SSD_GW), lambda g, i: (0, 0))
    st = pl.BlockSpec((1, tb // LS, SSD_STATE, SSD_GW), lambda g, i: (g, blk(i), 0, 0))
    return grp, xs, ph, pht, gvec, ex, st


def _expander():
    return jnp.repeat(jnp.eye(SSD_HPG, dtype=f32), SSD_GW // SSD_HPG, axis=1).astype(bf16)


def _expand3(dt8, ac8, ex):
    stack = jnp.concatenate([dt8, jnp.exp(ac8), jnp.exp(ac8[LS - 1:LS, :] - ac8)], axis=0)
    wide = _sel_r(stack, ex)
    return wide[0:LS], wide[LS:2 * LS], wide[2 * LS:3 * LS]


def _ssd_fwd(pre, dt_g, ac_g, act_g, dskx, tb):
    S = pre.shape[0]
    nc = S // LS
    hd = SSD_GW // SSD_HPG

    def body(p_ref, dt_ref, ac_ref, act_ref, dsk_ref, ex_ref, y_ref, st_ref, st):
        @pl.when(pl.program_id(1) == 0)
        def _():
            st[...] = jnp.zeros_like(st)

        ex = ex_ref[...]
        li = lax.broadcasted_iota(jnp.int32, (LS, LS), 0)
        si = lax.broadcasted_iota(jnp.int32, (LS, LS), 1)
        causal = li >= si

        def chunk(c, carry):
            rows = pl.ds(pl.multiple_of(c * LS, LS), LS)
            xs = _silu(p_ref[rows, pl.ds(0, SSD_GW)])
            bcb = _silu(p_ref[rows, pl.ds(SSD_GW, SSD_STATE)]).astype(bf16)
            ccb = _silu(p_ref[rows, pl.ds(SSD_GW + SSD_STATE, SSD_STATE)]).astype(bf16)
            dt8, ac8, act = dt_ref[0, rows, :], ac_ref[0, rows, :], act_ref[0, c]
            dtx, eax, tailx = _expand3(dt8, ac8, ex)
            xdt = xs * dtx
            cb = _nt(ccb, bcb)
            stb = st[...].astype(bf16)
            st_ref[0, c] = stb
            xdtb = xdt.astype(bf16)
            outs = []
            for h in range(SSD_HPG):
                dec = jnp.exp(jnp.where(causal, ac8[:, h:h + 1] - act[h:h + 1, :], -1e30))
                outs.append(_nn((cb * dec).astype(bf16), xdtb[:, hd * h:hd * (h + 1)]))
            y_ref[rows, :] = jnp.concatenate(outs, axis=1) + _nn(ccb, stb) * eax + dsk_ref[0] * xs
            st[...] = st[...] * eax[LS - 1:LS, :] + _tn(bcb, (xdt * tailx).astype(bf16))
            return carry

        lax.fori_loop(0, tb // LS, chunk, 0, unroll=min(4, tb // LS))

    grp, xs, ph, pht, gvec, ex, stspec = _ssd_specs(tb)
    return pl.pallas_call(
        body, name="ssd_fwd", grid=(SSD_GROUPS, S // tb),
        in_specs=[grp, ph, ph, pht, gvec, ex], out_specs=[xs, stspec],
        out_shape=[jax.ShapeDtypeStruct((S, SSD_WIDTH), f32), jax.ShapeDtypeStruct((SSD_GROUPS, nc, SSD_STATE, SSD_GW), bf16)],
        scratch_shapes=[pltpu.VMEM((SSD_STATE, SSD_GW), f32)], compiler_params=_params(("parallel", "arbitrary")),
    )(pre, dt_g, ac_g, act_g, dskx, _expander())


def _ssd_bwd(pre, dt_g, ac_g, act_g, sg_g, dskx, nega_g, dy, states, tb):
    S = pre.shape[0]
    nb = S // tb
    nck = tb // LS
    hd = SSD_GW // SSD_HPG

    def body(p_ref, dt_ref, ac_ref, act_ref, sg_ref, dsk_ref, na_ref, ex_ref, ext_ref, dy_ref, st_ref,
             dp_ref, ddt_ref, gsk_ref, gal_ref, gdb_ref, dst, skacc):
        @pl.when(pl.program_id(1) == 0)
        def _():
            dst[...] = jnp.zeros_like(dst)
            skacc[...] = jnp.zeros_like(skacc)
            gal_ref[...] = jnp.zeros_like(gal_ref)
            gdb_ref[...] = jnp.zeros_like(gdb_ref)

        ex, ext = ex_ref[...], ext_ref[...]
        li = lax.broadcasted_iota(jnp.int32, (LS, LS), 0)
        si = lax.broadcasted_iota(jnp.int32, (LS, LS), 1)
        causal = li >= si
        anti = si >= li
        upper = anti.astype(bf16)
        eye = (si == li).astype(bf16)
        last_row = (lax.broadcasted_iota(jnp.int32, (LS, 1), 0) == LS - 1).astype(f32)
        head_id = lax.broadcasted_iota(jnp.int32, (1, SSD_HPG), 1)
        head_col = lax.broadcasted_iota(jnp.int32, (SSD_HPG, 1), 0)
        neg_a = na_ref[0]
        dskv = dsk_ref[0]

        def chunk(cc, carry):
            c = nck - 1 - cc
            rows = pl.ds(pl.multiple_of(c * LS, LS), LS)
            px = p_ref[rows, pl.ds(0, SSD_GW)]
            pb = p_ref[rows, pl.ds(SSD_GW, SSD_STATE)]
            pc = p_ref[rows, pl.ds(SSD_GW + SSD_STATE, SSD_STATE)]
            sgx, sgb, sgc = jax.nn.sigmoid(px), jax.nn.sigmoid(pb), jax.nn.sigmoid(pc)
            xs = px * sgx
            bcb = (pb * sgb).astype(bf16)
            ccb = (pc * sgc).astype(bf16)
            dt8, ac8, act = dt_ref[0, rows, :], ac_ref[0, rows, :], act_ref[0, c]
            dtx, eax, tailx = _expand3(dt8, ac8, ex)
            xdt = xs * dtx
            ex_last = eax[LS - 1:LS, :]
            stb = st_ref[0, c]
            dyv = dy_ref[rows, :]
            dyb = dyv.astype(bf16)
            xdtb = xdt.astype(bf16)
            skacc[...] += jnp.sum(dyv * xs, axis=0, keepdims=True)
            yinter = _nn(ccb, stb) * eax
            dzb = (dyv * eax).astype(bf16)
            dcc = _nt(dzb, stb)
            dstv = dst[...]
            dstb = dstv.astype(bf16)
            xt = xdt * tailx
            dxt = _nn(bcb, dstb)
            dbc = _nt(xt.astype(bf16), dstb)
            dxdt = dxt * tailx
            lastrow = jnp.sum(dxt * xt, axis=0, keepdims=True) + jnp.sum(dstv * stb.astype(f32), axis=0, keepdims=True) * ex_last
            dst[...] = dstv * ex_last + _tn(ccb, dzb)
            cb = _nt(ccb, bcb)
            cbt = _nt(bcb, ccb)
            dcb = jnp.zeros((LS, LS), f32)
            dac8 = jnp.zeros((LS, SSD_HPG), f32)
            dact = jnp.zeros((SSD_HPG, LS), f32)
            dxin = []
            for h in range(SSD_HPG):
                sl = slice(hd * h, hd * (h + 1))
                col, rowv = ac8[:, h:h + 1], act[h:h + 1, :]
                dec = jnp.exp(jnp.where(causal, col - rowv, -1e30))
                dect = jnp.exp(jnp.where(anti, rowv - col, -1e30))
                gm = cb * dec
                dgm = _nt(dyb[:, sl], xdtb[:, sl])
                dxin.append(_nn((cbt * dect).astype(bf16), dyb[:, sl]))
                dcb = dcb + dgm * dec
                w = dgm * gm
                dac8 = dac8 + jnp.sum(w, axis=1, keepdims=True) * (head_id == h).astype(f32)
                dact = dact + (head_col == h).astype(f32) * jnp.sum(w, axis=0, keepdims=True)
            dxintra = jnp.concatenate(dxin, axis=1)
            dcbb = dcb.astype(bf16)
            dcc = dcc + _nn(dcbb, bcb)
            dbc = dbc + _tn(dcbb, ccb)
            dxdt = dxdt + dxintra
            dacx = dyv * yinter - dxt * xt + last_row * lastrow
            red = _sel_r(jnp.concatenate([dacx, dxdt * xs], axis=0), ext)
            dac8 = dac8 - _rows_to_cols(dact, eye) + red[0:LS]
            da8 = _sel_l(upper, dac8)
            ddt8 = red[LS:2 * LS] + da8 * neg_a
            gal_ref[0] += jnp.sum(da8 * dt8 * neg_a, axis=0, keepdims=True)
            ddr = ddt8 * sg_ref[0, rows, :]
            ddt_ref[0, rows, :] = ddr
            gdb_ref[0] += jnp.sum(ddr, axis=0, keepdims=True)
            dsilu = lambda p, s: s * (1.0 + p * (1.0 - s))
            dp_ref[rows, pl.ds(0, SSD_GW)] = (dskv * dyv + dxdt * dtx) * dsilu(px, sgx)
            dp_ref[rows, pl.ds(SSD_GW, SSD_STATE)] = dbc * dsilu(pb, sgb)
            dp_ref[rows, pl.ds(SSD_GW + SSD_STATE, SSD_STATE)] = dcc * dsilu(pc, sgc)
            return carry

        lax.fori_loop(0, nck, chunk, 0, unroll=min(4, nck))

        @pl.when(pl.program_id(1) == nb - 1)
        def _():
            gsk_ref[0] = skacc[...]

    grp, xs, ph, pht, gvec, ex, stspec = _ssd_specs(tb, rev_nb=nb)
    small = pl.BlockSpec((1, 1, SSD_HPG), lambda g, i: (g, 0, 0))
    ext = pl.BlockSpec((SSD_GW, SSD_HPG), lambda g, i: (0, 0))
    sm = jax.ShapeDtypeStruct((SSD_GROUPS, 1, SSD_HPG), f32)
    expander = _expander()
    return pl.pallas_call(
        body, name="ssd_bwd", grid=(SSD_GROUPS, nb),
        in_specs=[grp, ph, ph, pht, ph, gvec, small, ex, ext, xs, stspec],
        out_specs=[grp, ph, gvec, small, small],
        out_shape=[jax.ShapeDtypeStruct((S, CONV_DIM), f32), jax.ShapeDtypeStruct((SSD_GROUPS, S, SSD_HPG), f32),
                   jax.ShapeDtypeStruct((SSD_GROUPS, 1, SSD_GW), f32), sm, sm],
        scratch_shapes=[pltpu.VMEM((SSD_STATE, SSD_GW), f32), pltpu.VMEM((1, SSD_GW), f32)],
        compiler_params=_params(("parallel", "arbitrary")),
    )(pre, dt_g, ac_g, act_g, sg_g, dskx, nega_g, expander, expander.T, dy, states)


def _tiles(S):
    return dict(tb=min(512, S), tr=min(256, S), tm=min(1024, S))


def _local_step(x, positions, target, norm1_w, w_main, w_dt, conv_w, conv_b, dt_bias, a_log, d_skip, ssd_norm_w,
                late_token, late_weights, norm_f_w, reducer):
    S, D = x.shape
    t = _tiles(S)
    tb, tr, tm = t["tb"], t["tr"], t["tm"]

    half = RET_DK // 2
    inv_freq = ROPE_THETA ** (-jnp.arange(half, dtype=f32) / half)
    ang = positions.astype(f32)[:, None] * inv_freq
    cos, sin = jnp.cos(ang), jnp.sin(ang)
    log_gamma = jnp.log1p(-(2.0 ** (-5.0 - jnp.arange(RET_HEADS, dtype=f32))))
    idx = jnp.arange(CHUNK, dtype=f32)
    intra = jnp.exp(jnp.abs(idx[:, None] - idx[None, :]) * log_gamma[:, None, None])
    qdec = jnp.exp((idx + 1.0)[None, :] * log_gamma[:, None])[:, :, None]
    kdec = jnp.exp((CHUNK - 1.0 - idx)[None, :] * log_gamma[:, None])[:, :, None]
    cdec = jnp.exp(CHUNK * log_gamma)[:, None, None]
    conv_wm, conv_bm = _xbc_group_major(conv_w), _xbc_group_major(conv_b)

    h, ht = _norm1_fwd(x, norm1_w + late_token[0, 0], tr)
    proj = _mm1(h, w_main, tm=tm, tn=1024, tk=D, out_dtype=f32, name="proj_main")
    dt_raw = _mm1(h, w_dt, tm=tm, tn=128, tk=D, out_dtype=f32, name="proj_dt")
    y_ret, yr, yrt, ret_states = _ret_fwd(proj, cos, sin, intra, qdec, kdec, cdec, tb)
    pre = _conv_fwd(proj, conv_wm, conv_bm, min(1024, S), 512)
    pad64 = lambda v: jnp.pad(v, ((0, 0), (0, 128 - SSD_HEADS)))
    dt, sg, ac = _dt_prep(dt_raw, pad64(dt_bias), pad64(a_log), tb)
    dt_g, ac_g, sg_g, act_g = _group_major(dt), _group_major(ac), _group_major(sg), _group_major_t(ac)
    dskx = jnp.repeat(d_skip.reshape(SSD_GROUPS, 1, SSD_HPG), SSD_GW // SSD_HPG, axis=2)
    nega_g = (-jnp.exp(a_log)).reshape(SSD_GROUPS, 1, SSD_HPG)
    y_ssd, ssd_states = _ssd_fwd(pre, dt_g, ac_g, act_g, dskx, tb)
    ys, yst = _ssd_norm_fwd(y_ssd, proj, ssd_norm_w, tr // 2)
    w_br, w_bs, w_o = late_weights(ys)
    p_r = _mm1(yr, w_br, tm=tm, tn=1024, tk=2048, out_dtype=f32, name="branch_ret")
    p_s = _mm1(ys, w_bs, tm=tm, tn=1024, tk=2048, out_dtype=f32, name="branch_ssd")
    merged, mergedt = _merge_fwd(p_r, p_s, proj, tr)
    mo = _mm1(merged, w_o, tm=tm, tn=1024, tk=2048, out_dtype=f32, name="out_proj")
    dx2, dx2b, loss, g_norm_f = _final_fwd_bwd(x, mo, target, norm_f_w.reshape(1, D), tr)

    tkt = min(4096, S)
    wg = lambda at, b, name, tn=1024: _mm1(at, b, tm=min(1024, at.shape[0]), tn=tn, tk=tkt, out_dtype=f32, name=name)
    dm = _mm1(dx2b, w_o, tm=tm, tn=1024, tk=2048, out_dtype=f32, name="d_merged", tb=True)
    g_w_o = wg(mergedt, dx2b, "g_w_out")
    dp_r, dp_s, dproj = _merge_bwd(dm, p_r, p_s, proj, tr)
    dyr = _mm1(dp_r, w_br, tm=tm, tn=1024, tk=2048, out_dtype=f32, name="d_yr", tb=True)
    dys = _mm1(dp_s, w_bs, tm=tm, tn=1024, tk=2048, out_dtype=f32, name="d_ys", tb=True)
    g_w_br = wg(yrt, dp_r, "g_w_br_ret")
    g_w_bs = wg(yst, dp_s, "g_w_br_ssd")
    dy_ssd, dproj, g_ssd_norm = _ssd_norm_bwd(y_ssd, proj, ssd_norm_w, dys, dproj, tr // 2)
    dproj = _ret_bwd(proj, cos, sin, intra, qdec, kdec, cdec, y_ret, dyr, ret_states, dproj, tb)
    dpre, ddt_g, gsk, gal, gdb = _ssd_bwd(pre, dt_g, ac_g, act_g, sg_g, dskx, nega_g, dy_ssd, ssd_states, tb)
    dproj, gcw, gcb = _conv_bwd(dpre, proj, conv_wm, dproj, min(1024, S), 512)
    ddt = jnp.transpose(ddt_g, (1, 0, 2)).reshape(S, SSD_HEADS)
    ddt_p = jnp.pad(ddt, ((0, 0), (0, 128 - SSD_HEADS))).astype(bf16)

    hr = D // 2
    wg_half = lambda off, b, name, tn=1024: _mm([(ht, 0, b, 0, 0, S, tkt)], hr, b.shape[1], tm=hr, tn=tn, out_dtype=f32,
                                                 name=name, row_off=off)
    off_sib, off_own = reducer.halves()
    gs_main = wg_half(off_sib, dproj, "g_w_in_main_sib")
    gs_dt = wg_half(off_sib, ddt_p, "g_w_in_dt_sib", tn=128)
    swap_state = reducer.first(gs_main, gs_dt, g_w_br, g_w_bs, g_w_o)
    ddt_p = ddt_p + swap_state[-1][0, 0].astype(bf16)
    go_main = wg_half(off_own, dproj, "g_w_in_main_own")
    go_dt = wg_half(off_own, ddt_p, "g_w_in_dt_own", tn=128)
    reduce_state = reducer.second(swap_state, go_main, go_dt)
    ddt_p = ddt_p + reduce_state[-1][0, 0].astype(bf16)
    dh = _mm([(dproj, 0, w_main, 0, 0, N_MAIN, N_MAIN // 8), (ddt_p, 0, w_dt, 0, 0, 128, 128)], S, D, tm=tm, tn=1024,
             out_dtype=f32, name="d_h", tb=True)
    grad_x, g_norm1 = _norm1_bwd(x, norm1_w, dh, dx2, tr)

    seg = lambda v: jnp.sum(v.reshape(SSD_HEADS, SSD_GW // SSD_HPG), axis=1).reshape(1, SSD_HEADS)
    grads = dict(
        norm1_w=g_norm1, w_in_main=(gs_main, go_main), w_in_dt=(gs_dt, go_dt),
        conv_w=_xbc_original(gcw), conv_b=_xbc_original(gcb),
        dt_bias=gdb.reshape(1, SSD_HEADS), a_log=gal.reshape(1, SSD_HEADS), d_skip=seg(gsk),
        ssd_norm_w=g_ssd_norm, w_br_ret=g_w_br, w_br_ssd=g_w_bs, w_out=g_w_o, norm_f_w=g_norm_f,
    )
    return loss, grad_x, grads, reduce_state


def _me():
    return lax.axis_index("x"), lax.axis_index("y"), lax.axis_index("c")


def _other_chips(x, y):
    return [(1 - x, y), (x, 1 - y), (1 - x, 1 - y)]


def _gather_weights(a, cw):
    R = a.shape[0]
    hr, hq = R // 2, R // 4

    def body(a_ref, cw_ref, ga_ref, gc_ref, send_sems, recv_sems):
        x, y, c = _me()
        me, sibling = (x, y, c), (x, y, 1 - c)
        nx, ny = (1 - x, y, c), (x, 1 - y, c)
        k, kx, ky, kd = 2 * x + y, 2 * (1 - x) + y, 2 * x + (1 - y), 2 * (1 - x) + (1 - y)

        def rows(half, q):
            return pl.ds(pl.multiple_of(half * hr + q * hq, 8), hq)

        def cp(sem, shard, half, q, to, src=None):
            dst = ga_ref.at[shard, rows(half, q), :]
            return pltpu.make_async_remote_copy(src_ref=dst if src is None else src, dst_ref=dst, send_sem=send_sems.at[sem],
                                                recv_sem=recv_sems.at[sem], device_id=to, device_id_type=MESH)

        def small(j, src_shard, to):
            return pltpu.make_async_remote_copy(
                src_ref=cw_ref, dst_ref=gc_ref.at[src_shard], send_sem=send_sems.at[12 + j], recv_sem=recv_sems.at[12 + j],
                device_id=to, device_id_type=MESH)

        own = lambda q: a_ref.at[rows(c, q), :]
        smalls = [small(j, k, (*chip, c)) for j, chip in enumerate(_other_chips(x, y))]
        sends = [cp(0, k, c, 0, nx, own(0)), cp(2, k, c, 1, ny, own(1)), cp(1, k, c, 1, nx, own(1)), cp(3, k, c, 0, ny, own(0))]
        for s in sends + smalls:
            s.start()
        arrivals = [(0, kx, 0, (4, ny)), (2, ky, 1, (5, nx)), (1, kx, 1, None), (3, ky, 0, None), (4, kd, 0, None), (5, kd, 1, None)]
        for sem, shard, q, onward in arrivals:
            cp(sem, shard, c, q, me).wait_recv()
            if onward is not None:
                sends.append(cp(onward[0], shard, c, q, onward[1]))
                sends[-1].start()
            sends.append(cp(6 + sem, shard, c, q, sibling))
            sends[-1].start()
        for sem, shard, q, _ in arrivals:
            cp(6 + sem, shard, 1 - c, q, me).wait_recv()
        for j, chip in enumerate(_other_chips(x, y)):
            small(j, 2 * chip[0] + chip[1], me).wait_recv()
        for s in sends + smalls:
            s.wait_send()

    return pl.pallas_call(
        body, name="gather_weights", in_specs=[ANY, ANY], out_specs=[ANY, ANY],
        out_shape=[jax.ShapeDtypeStruct((N_SHARD,) + a.shape, a.dtype), jax.ShapeDtypeStruct((N_SHARD,) + cw.shape, cw.dtype)],
        scratch_shapes=[pltpu.SemaphoreType.DMA((15,)), pltpu.SemaphoreType.DMA((15,))],
        compiler_params=pltpu.CompilerParams(has_side_effects=True),
    )(a, cw)


def _gather_late_copies(src, land, send_sems, recv_sems):
    x, y, c = _me()
    k = 2 * x + y
    return [pltpu.make_async_remote_copy(src_ref=src, dst_ref=land.at[k], send_sem=send_sems.at[j], recv_sem=recv_sems.at[j],
                                         device_id=(*chip, c), device_id_type=MESH) for j, chip in enumerate(_other_chips(x, y))]


def _gather_late_start(b):
    land = lax.empty((N_SHARD,) + b.shape, b.dtype)

    def body(b_ref, land_ref, send_sems, recv_sems, b_thru, land_thru, token):
        for cp in _gather_late_copies(b_ref, land_ref, send_sems, recv_sems):
            cp.start()
        token[...] = jnp.zeros_like(token)

    return pl.pallas_call(
        body, name="gather_late_start", in_specs=[HBM, HBM],
        out_specs=(SEM, SEM, HBM, HBM, pl.BlockSpec(memory_space=pltpu.VMEM)),
        out_shape=(pltpu.SemaphoreType.DMA((3,)), pltpu.SemaphoreType.DMA((3,)), pltpu.HBM(b.shape, b.dtype),
                   pltpu.HBM(land.shape, land.dtype), jax.ShapeDtypeStruct((8, 128), f32)),
        input_output_aliases={0: 2, 1: 3}, compiler_params=pltpu.CompilerParams(has_side_effects=DATAFLOW),
    )(pltpu.with_memory_space_constraint(b, pltpu.HBM), pltpu.with_memory_space_constraint(land, pltpu.HBM))


def _gather_late_wait(send_sems, recv_sems, src, land, after):
    def body(b_ref, land_ref, send_sems_ref, recv_sems_ref, after_ref, b_dead, land_out):
        x, y, c = _me()
        for j, chip in enumerate(_other_chips(x, y)):
            kk = 2 * chip[0] + chip[1]
            cp = pltpu.make_async_remote_copy(src_ref=b_ref, dst_ref=land_ref.at[kk], send_sem=send_sems_ref.at[j],
                                              recv_sem=recv_sems_ref.at[j], device_id=(x, y, c), device_id_type=MESH)
            cp.wait_send()
            cp.wait_recv()

    return pl.pallas_call(
        body, name="gather_late_wait", in_specs=[HBM, HBM, SEM, SEM, ANY], out_specs=[HBM, HBM],
        out_shape=[pltpu.HBM(src.shape, src.dtype), pltpu.HBM(land.shape, land.dtype)], input_output_aliases={0: 0, 1: 1},
        compiler_params=pltpu.CompilerParams(has_side_effects=DATAFLOW),
    )(src, land, send_sems, recv_sems, after)[1]


HBM = pl.BlockSpec(memory_space=pltpu.HBM)
SEM = pl.BlockSpec(memory_space=pltpu.SEMAPHORE)
DATAFLOW = pltpu.SideEffectType.DATAFLOW_SIDE_EFFECTING


def _swap_copies(srcs, lands, send_sems, recv_sems):
    x, y, c = _me()

    def cp(src, dst, q):
        return pltpu.make_async_remote_copy(src_ref=src, dst_ref=dst, send_sem=send_sems.at[q], recv_sem=recv_sems.at[q],
                                            device_id=(x, y, 1 - c), device_id_type=MESH)

    return [cp(srcs[0], lands[0], 0), cp(srcs[1], lands[1], 1)] + [cp(srcs[2].at[s, 1 - c], lands[2].at[s], 2 + s) for s in range(N_SHARD)]


def _sibling_swap_start(g_main, g_dt, g_b):
    srcs = [g_main, g_dt, g_b]
    lands = [lax.empty(g_main.shape, g_main.dtype), lax.empty(g_dt.shape, g_dt.dtype),
             lax.empty(g_b.shape[:1] + g_b.shape[2:], g_b.dtype)]

    def body(*refs):
        for cp in _swap_copies(refs[0:3], refs[3:6], refs[6], refs[7]):
            cp.start()
        refs[14][...] = jnp.zeros_like(refs[14])

    hbm = lambda a: pltpu.HBM(a.shape, a.dtype)
    out = pl.pallas_call(
        body, name="sibling_swap_start", in_specs=[HBM] * 6,
        out_specs=(SEM, SEM, *[HBM] * 6, pl.BlockSpec(memory_space=pltpu.VMEM)),
        out_shape=(pltpu.SemaphoreType.DMA((2 + N_SHARD,)), pltpu.SemaphoreType.DMA((2 + N_SHARD,)), *[hbm(a) for a in srcs + lands],
                   jax.ShapeDtypeStruct((8, 128), f32)),
        input_output_aliases={t: 2 + t for t in range(6)}, compiler_params=pltpu.CompilerParams(has_side_effects=DATAFLOW),
    )(*[pltpu.with_memory_space_constraint(a, pltpu.HBM) for a in srcs + lands])
    return out[0], out[1], list(out[2:5]), list(out[5:8]), out[8]


def _sibling_swap_wait(send_sems, recv_sems, srcs, lands, after):
    def body(*refs):
        for cp in _swap_copies(refs[0:3], refs[3:6], refs[6], refs[7]):
            cp.wait_send()
            cp.wait_recv()

    hbm = lambda a: pltpu.HBM(a.shape, a.dtype)
    out = pl.pallas_call(
        body, name="sibling_swap_wait", in_specs=[HBM] * 6 + [SEM, SEM, ANY], out_specs=[HBM] * 6,
        out_shape=[hbm(a) for a in list(srcs) + list(lands)], input_output_aliases={t: t for t in range(6)},
        compiler_params=pltpu.CompilerParams(has_side_effects=DATAFLOW),
    )(*srcs, *lands, send_sems, recv_sems, after)
    return list(out[:3]), list(out[3:])


def _exchange_copies(ins, lands, send_sems, recv_sems):
    n = len(ins)
    x, y, c = _me()
    cps = []
    for j, chip in enumerate(_other_chips(x, y)):
        kk = 2 * chip[0] + chip[1]
        for t in range(n):
            cps.append(pltpu.make_async_remote_copy(
                src_ref=ins[t].at[kk], dst_ref=lands[t].at[j], send_sem=send_sems.at[n * j + t],
                recv_sem=recv_sems.at[n * j + t], device_id=(*chip, c), device_id_type=MESH))
    return cps


def _chip_exchange_start(arrs):
    n = len(arrs)
    lands = [lax.empty((3,) + a.shape[1:], a.dtype) for a in arrs]

    def body(*refs):
        ins, lands_in = refs[:n], refs[n:2 * n]
        send_sems, recv_sems = refs[2 * n], refs[2 * n + 1]
        token = refs[4 * n + 2]
        for cp in _exchange_copies(ins, lands_in, send_sems, recv_sems):
            cp.start()
        token[...] = jnp.zeros_like(token)

    hbm = lambda a: pltpu.HBM(a.shape, a.dtype)
    out = pl.pallas_call(
        body, name="chip_exchange_start", in_specs=[HBM] * (2 * n),
        out_specs=(SEM, SEM, *[HBM] * (2 * n), pl.BlockSpec(memory_space=pltpu.VMEM)),
        out_shape=(pltpu.SemaphoreType.DMA((3 * n,)), pltpu.SemaphoreType.DMA((3 * n,)), *[hbm(a) for a in arrs],
                   *[hbm(a) for a in lands], jax.ShapeDtypeStruct((8, 128), f32)),
        input_output_aliases={t: 2 + t for t in range(2 * n)},
        compiler_params=pltpu.CompilerParams(has_side_effects=DATAFLOW),
    )(*[pltpu.with_memory_space_constraint(a, pltpu.HBM) for a in list(arrs) + lands])
    return out[0], out[1], list(out[2:2 + n]), list(out[2 + n:2 + 2 * n]), out[2 + 2 * n]


def _chip_exchange_wait(send_sems, recv_sems, srcs, lands, after):
    n = len(srcs)

    def body(*refs):
        ins, lands_in = refs[:n], refs[n:2 * n]
        send_sems_ref, recv_sems_ref = refs[2 * n], refs[2 * n + 1]
        for cp in _exchange_copies(ins, lands_in, send_sems_ref, recv_sems_ref):
            cp.wait_send()
            cp.wait_recv()

    hbm = lambda a: pltpu.HBM(a.shape, a.dtype)
    out = pl.pallas_call(
        body, name="chip_exchange_wait", in_specs=[HBM] * (2 * n) + [SEM, SEM, ANY],
        out_specs=[HBM] * (2 * n), out_shape=[hbm(a) for a in list(srcs) + list(lands)],
        input_output_aliases={t: t for t in range(2 * n)},
        compiler_params=pltpu.CompilerParams(has_side_effects=DATAFLOW),
    )(*srcs, *lands, send_sems, recv_sems, after)
    return list(out[:n]), list(out[n:])


def _share_halves(bufs, by_cols, name):
    n = len(bufs)

    def body(*refs):
        ins, outs = refs[:n], refs[n:2 * n]
        send_sems, recv_sems = refs[2 * n], refs[2 * n + 1]
        x, y, c = _me()

        def part(ref, t, half):
            if by_cols[t]:
                w = bufs[t].shape[1] // 2
                return ref.at[:, pl.ds(pl.multiple_of(half * w, 128), w)]
            return ref.at[half]

        sends = [pltpu.make_async_remote_copy(src_ref=part(ins[t], t, c), dst_ref=part(outs[t], t, c), send_sem=send_sems.at[t],
                                              recv_sem=recv_sems.at[t], device_id=(x, y, 1 - c), device_id_type=MESH) for t in range(n)]
        for cp in sends:
            cp.start()
        for t in range(n):
            pltpu.make_async_remote_copy(src_ref=part(ins[t], t, c), dst_ref=part(outs[t], t, 1 - c), send_sem=send_sems.at[t],
                                         recv_sem=recv_sems.at[t], device_id=(x, y, c), device_id_type=MESH).wait_recv()
        for cp in sends:
            cp.wait_send()

    return pl.pallas_call(
        body, name=name, in_specs=[ANY] * n, out_specs=[ANY] * n,
        out_shape=[jax.ShapeDtypeStruct(a.shape, a.dtype) for a in bufs], input_output_aliases={t: t for t in range(n)},
        scratch_shapes=[pltpu.SemaphoreType.DMA((n,)), pltpu.SemaphoreType.DMA((n,))],
        compiler_params=pltpu.CompilerParams(has_side_effects=True),
    )(*bufs)


def _gather_vec(v):
    n = v.shape[1]

    def body(v_ref, o_ref, send_sems, recv_sems):
        x, y, c = _me()
        me = 4 * x + 2 * y + c
        cps = []
        for j in range(1, 8):
            fx, fy, fc = (j >> 2) & 1, (j >> 1) & 1, j & 1
            peer = (x ^ fx, y ^ fy, c ^ fc)
            cps.append(pltpu.make_async_remote_copy(
                src_ref=v_ref, dst_ref=o_ref.at[pl.ds(me, 1), :], send_sem=send_sems.at[j - 1], recv_sem=recv_sems.at[j - 1],
                device_id=peer, device_id_type=MESH))
        for cp in cps:
            cp.start()
        for j in range(1, 8):
            fx, fy, fc = (j >> 2) & 1, (j >> 1) & 1, j & 1
            src = 4 * (x ^ fx) + 2 * (y ^ fy) + (c ^ fc)
            pltpu.make_async_remote_copy(
                src_ref=v_ref, dst_ref=o_ref.at[pl.ds(src, 1), :], send_sem=send_sems.at[j - 1], recv_sem=recv_sems.at[j - 1],
                device_id=(x, y, c), device_id_type=MESH).wait_recv()
        for cp in cps:
            cp.wait_send()

    return pl.pallas_call(
        body, name="gather_vec", in_specs=[ANY], out_specs=ANY, out_shape=jax.ShapeDtypeStruct((8, n), v.dtype),
        scratch_shapes=[pltpu.SemaphoreType.DMA((7,)), pltpu.SemaphoreType.DMA((7,))],
        compiler_params=pltpu.CompilerParams(has_side_effects=True),
    )(v)


def _pair_sum(g, r, name, tr):
    L, hr, C = r.shape
    both_halves = g.ndim == 4

    def body(c_ref, g_ref, r_ref, o_ref):
        def strip(rows):
            gv = g_ref[0, 0, rows, :] if both_halves else g_ref[0, rows, :]
            o_ref[0, rows, :] = (gv + r_ref[0, rows, :]).astype(bf16)
        _for_strips(tr, strip)

    g_spec = (pl.BlockSpec((1, 1, tr, C), lambda s, i, c_ref: (s, c_ref[0], i, 0)) if both_halves
              else pl.BlockSpec((1, tr, C), lambda s, i, c_ref: (s, i, 0)))
    grid_spec = pltpu.PrefetchScalarGridSpec(
        num_scalar_prefetch=1, grid=(L, hr // tr),
        in_specs=[g_spec, pl.BlockSpec((1, tr, C), lambda s, i, c_ref: (s, i, 0))],
        out_specs=pl.BlockSpec((1, tr, C), lambda s, i, c_ref: (s, i, 0)))
    c = lax.axis_index("c").reshape(1).astype(jnp.int32)
    return pl.pallas_call(body, name=name, grid_spec=grid_spec, out_shape=jax.ShapeDtypeStruct((L, hr, C), bf16),
                          compiler_params=_params(("parallel", "parallel")))(c, g, r)


def _own_sum(p, got, name, transposed=False):
    _, hr, C = p.shape
    tr = SUM_ROWS
    c_full, c_pad = C // 128 * 128, -(-C // 128) * 128

    def total(p_ref, got_ref, rows):
        return ((p_ref[0, rows, :].astype(f32) + got_ref[0, rows, :].astype(f32)) + got_ref[1, rows, :].astype(f32)) \
            + got_ref[2, rows, :].astype(f32)

    def body(idx_ref, p_ref, got_ref, o_ref):
        def strip(rows):
            o_ref[0, rows, :] = total(p_ref, got_ref, rows)
        _for_strips(tr, strip)

    def body_t(idx_ref, p_ref, got_ref, o_ref, buf):
        if c_pad > c_full:
            buf[:, pl.ds(c_full, c_pad - c_full)] = jnp.zeros((tr, c_pad - c_full), f32)

        def strip(rows):
            buf[rows, pl.ds(0, C)] = total(p_ref, got_ref, rows)
        _for_strips(tr, strip)
        o_ref[...] = buf[...].T[:C]

    in_specs = [pl.BlockSpec((1, tr, C), lambda i, idx: (idx[0], i, 0)), pl.BlockSpec((3, tr, C), lambda i, idx: (0, i, 0))]
    x, y, c = _me()
    idx = jnp.stack([2 * x + y, c]).astype(jnp.int32)
    if transposed:
        grid_spec = pltpu.PrefetchScalarGridSpec(num_scalar_prefetch=1, grid=(hr // tr,), in_specs=in_specs,
                                                 out_specs=pl.BlockSpec((C, tr), lambda i, idx: (0, idx[1] * (hr // tr) + i)),
                                                 scratch_shapes=[pltpu.VMEM((tr, c_pad), f32)])
        return pl.pallas_call(body_t, name=name, grid_spec=grid_spec, out_shape=jax.ShapeDtypeStruct((C, 2 * hr), f32),
                              compiler_params=_params(("parallel",)))(idx, p, got)
    grid_spec = pltpu.PrefetchScalarGridSpec(num_scalar_prefetch=1, grid=(hr // tr,), in_specs=in_specs,
                                             out_specs=pl.BlockSpec((1, tr, C), lambda i, idx: (idx[1], i, 0)))
    return pl.pallas_call(body, name=name, grid_spec=grid_spec, out_shape=jax.ShapeDtypeStruct((2, hr, C), f32),
                          compiler_params=_params(("parallel",)))(idx, p, got)


def _adamw(w, g, m, v, name, tr):
    _, R, C = w.shape
    rs = min(8, tr)

    def body(w_ref, g_ref, m_ref, v_ref, d_ref, nm_ref, nv_ref):
        def strip(s, carry):
            rows = pl.ds(pl.multiple_of(s * rs, rs), rs)
            gv = g_ref[0, rows, :]
            mn = ADAM_B1 * m_ref[0, rows, :] + (1.0 - ADAM_B1) * gv
            vn = ADAM_B2 * v_ref[0, rows, :] + (1.0 - ADAM_B2) * (gv * gv)
            m_hat = mn / (1.0 - ADAM_B1 ** ADAM_STEP)
            v_hat = vn / (1.0 - ADAM_B2 ** ADAM_STEP)
            d_ref[0, rows, :] = -ADAM_LR * (m_hat / (jnp.sqrt(v_hat) + ADAM_EPS) + ADAM_WD * w_ref[0, rows, :])
            nm_ref[0, rows, :] = mn
            nv_ref[0, rows, :] = vn
            return carry

        if R % tr == 0:
            lax.fori_loop(0, tr // rs, strip, 0, unroll=min(2, tr // rs))
        else:
            lax.fori_loop(0, jnp.minimum(tr, R - pl.program_id(0) * tr) // rs, strip, 0)

    blk, grid = pl.BlockSpec((1, tr, C), lambda i: (0, i, 0)), (-(-R // tr),)
    o = jax.ShapeDtypeStruct((1, R, C), f32)
    return pl.pallas_call(body, name=name, grid=grid, in_specs=[blk] * 4, out_specs=[blk] * 3, out_shape=[o, o, o],
                          compiler_params=_params(("parallel",)))(w, g, m, v)


def _sum8(t):
    n = t.shape[1]

    def body(t_ref, o_ref):
        acc = t_ref[pl.ds(0, 1), :]
        for r in range(1, 8):
            acc = acc + t_ref[pl.ds(r, 1), :]
        o_ref[...] = acc

    return pl.pallas_call(body, name="sum_devices", out_shape=jax.ShapeDtypeStruct((1, n), f32))(t)


def _reduce_swap_start(g_main, g_dt, g_b):
    hr = g_main.shape[0]
    return _sibling_swap_start(g_main, g_dt, g_b.reshape(N_SHARD, 2, hr, g_b.shape[-1]))


def _reduce_start(swap_state, g_main, g_dt):
    hr = g_main.shape[0]
    send_sems, recv_sems, srcs, lands, _ = swap_state
    srcs, (r_main, r_dt, r_b) = _sibling_swap_wait(send_sems, recv_sems, srcs, lands, g_dt)
    p_main = _pair_sum(g_main[None], r_main[None], "pair_sum_main", SUM_ROWS // 4)
    p_dt = _pair_sum(g_dt[None], r_dt[None], "pair_sum_dt", SUM_ROWS)
    p_b = _pair_sum(srcs[2], r_b, "pair_sum_b", SUM_ROWS)
    p_in = jnp.transpose(_w_in_grad_full(p_main[0], p_dt[0]).reshape(hr, N_SHARD, W_IN_SHARD), (1, 0, 2))
    return _chip_exchange_start([p_in, p_b])


def _reduce_finish(state, after):
    send_sems, recv_sems, srcs, lands, _ = state
    (p_in, p_b), (got_in, got_b) = _chip_exchange_wait(send_sems, recv_sems, srcs, lands, after)
    mine_in, mine_b = _own_sum(p_in, got_in, "own_sum_in", transposed=True), _own_sum(p_b, got_b, "own_sum_b")
    full_in_t, full_b = _share_halves([mine_in, mine_b], [True, False], "share_halves")
    return full_in_t, full_b.reshape(-1, full_b.shape[-1])


def kernel(x, positions, norm1_w, w_in, conv_w, conv_b, dt_bias, a_log, d_skip, ssd_norm_w, w_br_ret, w_br_ssd, w_out, norm_f_w, loss_target, m_norm1_w, m_w_in, m_conv_w, m_conv_b, m_dt_bias, m_a_log, m_d_skip, m_ssd_norm_w, m_w_br_ret, m_w_br_ssd, m_w_out, m_norm_f_w, v_norm1_w, v_w_in, v_conv_w, v_conv_b, v_dt_bias, v_a_log, v_d_skip, v_ssd_norm_w, v_w_br_ret, v_w_br_ssd, v_w_out, v_norm_f_w):
    D = D_MODEL
    xi, yi, ci = _me()
    k = 2 * xi + yi
    me = 2 * k + ci
    weights = dict(norm1_w=norm1_w, w_in=w_in, conv_w=conv_w, conv_b=conv_b, dt_bias=dt_bias, a_log=a_log, d_skip=d_skip,
                   ssd_norm_w=ssd_norm_w, w_br_ret=w_br_ret, w_br_ssd=w_br_ssd, w_out=w_out, norm_f_w=norm_f_w)
    mom1 = dict(norm1_w=m_norm1_w, w_in=m_w_in, conv_w=m_conv_w, conv_b=m_conv_b, dt_bias=m_dt_bias, a_log=m_a_log, d_skip=m_d_skip,
                ssd_norm_w=m_ssd_norm_w, w_br_ret=m_w_br_ret, w_br_ssd=m_w_br_ssd, w_out=m_w_out, norm_f_w=m_norm_f_w)
    mom2 = dict(norm1_w=v_norm1_w, w_in=v_w_in, conv_w=v_conv_w, conv_b=v_conv_b, dt_bias=v_dt_bias, a_log=v_a_log, d_skip=v_d_skip,
                ssd_norm_w=v_ssd_norm_w, w_br_ret=v_w_br_ret, w_br_ssd=v_w_br_ssd, w_out=v_w_out, norm_f_w=v_norm_f_w)

    a_sh = w_in[0].astype(bf16)
    b_sh = jnp.concatenate([w_br_ret[0], w_br_ssd[0], w_out[0]], axis=0).astype(bf16)
    ga, gc = _gather_weights(a_sh, conv_w[0])
    ga, b_late = lax.optimization_barrier((ga, b_sh))
    late_send, late_recv, late_src, late_land, late_token = _gather_late_start(b_late)
    own = lambda g, s: lax.dynamic_update_slice_in_dim(g, s[None], k, axis=0)
    ga, gc = own(ga, a_sh), own(gc, conv_w[0])
    w_main, w_dt = _w_main_from_shards(ga)
    conv_full = jnp.transpose(gc, (1, 0, 2)).reshape(SSD_CONV, CONV_DIM)

    def late_weights(after):
        gb = own(_gather_late_wait(late_send, late_recv, late_src, late_land, after), b_sh)
        return gb[:, 0:512].reshape(2048, D), gb[:, 512:1536].reshape(4096, D), gb[:, 1536:2048].reshape(2048, D)

    class Reducer:
        @staticmethod
        def halves():
            return (1 - ci).reshape(1).astype(jnp.int32), ci.reshape(1).astype(jnp.int32)

        @staticmethod
        def first(g_main, g_dt, g_w_br, g_w_bs, g_w_o):
            g_b = jnp.concatenate([g_w_br.reshape(N_SHARD, 512, D), g_w_bs.reshape(N_SHARD, 1024, D),
                                   g_w_o.reshape(N_SHARD, 512, D)], axis=1)
            return _reduce_swap_start(g_main, g_dt, g_b)

        second = staticmethod(_reduce_start)

    loss, grad_x, g, reduce_state = _local_step(x[0], positions[0], loss_target[0], norm1_w, w_main, w_dt, conv_full, conv_b, dt_bias,
                                                a_log, d_skip, ssd_norm_w, late_token, late_weights, norm_f_w, Reducer)

    grad_w_in_t, full_b = _reduce_finish(reduce_state, g["norm1_w"])
    grad_mats = dict(w_br_ret=full_b[0:512], w_br_ssd=full_b[512:1536], w_out=full_b[1536:2048])

    small = [(n, weights[n].size) for n in ("norm1_w", "conv_b", "dt_bias", "a_log", "d_skip", "ssd_norm_w", "norm_f_w")]
    parts = [jnp.pad(loss.reshape(1, 1), ((0, 0), (0, 127)))] + [g[n].reshape(1, -1) for n, _ in small] + [g["conv_w"].reshape(1, -1)]
    vec = jnp.concatenate(parts, axis=1)
    nv = vec.shape[1]
    nvp = -(-nv // 128) * 128
    vec = jnp.pad(vec, ((0, 0), (0, nvp - nv)))
    total = _sum8(lax.dynamic_update_slice_in_dim(_gather_vec(vec), vec, me, axis=0))
    loss_out = total[0, 0]
    off = 128
    grad_small = {}
    for n, sz in small:
        grad_small[n] = total[:, off:off + sz]
        off += sz
    g_conv = total[:, off:off + SSD_CONV * CONV_DIM].reshape(SSD_CONV, CONV_DIM)
    g_conv = lax.dynamic_slice_in_dim(g_conv, k * (CONV_DIM // N_SHARD), CONV_DIM // N_SHARD, axis=1)
    grad_small["conv_w"] = g_conv.reshape(1, -1)

    upd = {}
    for n in ("w_br_ret", "w_br_ssd", "w_out"):
        upd[n] = _adamw(weights[n], grad_mats[n][None], mom1[n], mom2[n], "adamw_" + n, tr=SUM_ROWS)
    tp = lambda t: jnp.swapaxes(t, 1, 2)
    upd["w_in"] = tuple(tp(t) for t in _adamw(tp(w_in), grad_w_in_t[None], tp(m_w_in), tp(v_w_in), "adamw_w_in", tr=256))
    grad_mats["w_in"] = tp(grad_w_in_t[None])
    names_small = [n for n, _ in small] + ["conv_w"]
    flat = lambda d: jnp.concatenate([d[n].reshape(1, -1) for n in names_small], axis=1)
    ns = sum(weights[n].size for n in names_small)
    nsp = -(-ns // 128) * 128
    padv = lambda t: jnp.pad(t, ((0, 0), (0, nsp - ns)))
    small_upd = _adamw(padv(flat(weights))[None], padv(flat(grad_small))[None], padv(flat(mom1))[None],
                       jnp.pad(flat(mom2), ((0, 0), (0, nsp - ns)), constant_values=1.0)[None], "adamw_small", 1)
    off = 0
    for n in names_small:
        sz = weights[n].size
        upd[n] = tuple(t[0, :, off:off + sz] for t in small_upd)
        off += sz

    order = ["norm1_w", "w_in", "conv_w", "conv_b", "dt_bias", "a_log", "d_skip", "ssd_norm_w", "w_br_ret", "w_br_ssd", "w_out", "norm_f_w"]
    grads_out = {**grad_mats, **grad_small}
    shp = lambda n, t: t.reshape(weights[n].shape)
    return (loss_out, grad_x[None], *[shp(n, grads_out[n]) for n in order], *[shp(n, upd[n][0]) for n in order],
            *[shp(n, upd[n][1]) for n in order], *[shp(n, upd[n][2]) for n in order])
```

```python
import jax
import jax.numpy as jnp
import numpy as np
from jax import lax
from jax.experimental import pallas as pl
from jax.experimental.pallas import tpu as pltpu

f32 = jnp.float32
bf16 = jnp.bfloat16
HIGHEST = lax.Precision.HIGHEST
MESH = pl.DeviceIdType.MESH

D_MODEL = 2048
EPS = 1e-6
CHUNK = 64
RET_HEADS = 8
RET_DK = 256
RET_HW = 4 * RET_DK
ROPE_THETA = 10000.0
SSD_WIDTH = 4096
SSD_GROUPS = 8
SSD_STATE = 128
SSD_GW = 512
SSD_GC = SSD_GW + 2 * SSD_STATE
SSD_HPG = 8
SSD_CONV = 4
CONV_DIM = 6144
SSD_HEADS = 64
LS = 128

C_RET, C_Z, C_GATES, C_XBC = 0, 8192, 12288, 16384
N_MAIN = 22528
DT_OFF = 18432
IN_PROJ = 22592
N_SHARD = 4
W_IN_SHARD = IN_PROJ // N_SHARD

ADAM_LR, ADAM_B1, ADAM_B2, ADAM_EPS, ADAM_WD, ADAM_STEP = 0.001, 0.9, 0.999, 1e-08, 0.01, 10

VMEM_LIMIT = 56 * 1024 * 1024
SUM_ROWS = 128
ANY = pl.BlockSpec(memory_space=pl.ANY)


def _params(dims):
    return pltpu.CompilerParams(dimension_semantics=dims, vmem_limit_bytes=VMEM_LIMIT)


def _silu(x):
    return x * jax.nn.sigmoid(x)


def _dsilu(x):
    s = jax.nn.sigmoid(x)
    return s * (1.0 + x * (1.0 - s))


def _nt(a, b):
    return lax.dot_general(a, b, (((1,), (1,)), ((), ())), preferred_element_type=f32)


def _tn(a, b):
    return lax.dot_general(a, b, (((0,), (0,)), ((), ())), preferred_element_type=f32)


def _nn(a, b):
    return jnp.dot(a, b, preferred_element_type=f32)


def _hi(a, b):
    return jnp.dot(a, b, precision=HIGHEST, preferred_element_type=f32)


def _split(a):
    hi = a.astype(bf16)
    return hi, (a - hi.astype(f32)).astype(bf16)


def _sel_r(a, sel):
    hi, lo = _split(a)
    return _nn(hi, sel) + _nn(lo, sel)


def _sel_l(sel, a):
    hi, lo = _split(a)
    return _nn(sel, hi) + _nn(sel, lo)


def _rows_to_cols(t, eye):
    hi = t.astype(bf16)
    r1 = t - hi.astype(f32)
    mid = r1.astype(bf16)
    lo = (r1 - mid.astype(f32)).astype(bf16)
    return _nt(eye, hi) + _nt(eye, mid) + _nt(eye, lo)


def _xbc_group_major(t):
    R = t.shape[0]
    nb = SSD_GROUPS * SSD_STATE
    parts = [t[:, :SSD_WIDTH].reshape(R, SSD_GROUPS, SSD_GW), t[:, SSD_WIDTH:SSD_WIDTH + nb].reshape(R, SSD_GROUPS, SSD_STATE),
             t[:, SSD_WIDTH + nb:].reshape(R, SSD_GROUPS, SSD_STATE)]
    return jnp.concatenate(parts, axis=2).reshape(R, CONV_DIM)


def _xbc_original(t):
    R = t.shape[0]
    g = t.reshape(R, SSD_GROUPS, SSD_GC)
    parts = [g[:, :, :SSD_GW].reshape(R, SSD_WIDTH), g[:, :, SSD_GW:SSD_GW + SSD_STATE].reshape(R, SSD_GROUPS * SSD_STATE),
             g[:, :, SSD_GW + SSD_STATE:].reshape(R, SSD_GROUPS * SSD_STATE)]
    return jnp.concatenate(parts, axis=1)


def _main_segments():
    segs = []
    for h in range(RET_HEADS):
        segs += [(base + RET_DK * h, RET_DK) for base in (0, 2048, 4096, 6144)]
    segs += [(8192, SSD_WIDTH), (DT_OFF + SSD_HEADS, 2 * D_MODEL)]
    nb = SSD_GROUPS * SSD_STATE
    for g in range(SSD_GROUPS):
        segs += [(12288 + SSD_GW * g, SSD_GW), (12288 + SSD_WIDTH + SSD_STATE * g, SSD_STATE),
                 (12288 + SSD_WIDTH + nb + SSD_STATE * g, SSD_STATE)]
    return segs


def _w_main_from_shards(shards):
    def cols(lo, hi):
        out = []
        while lo < hi:
            s = lo // W_IN_SHARD
            top = min(hi, (s + 1) * W_IN_SHARD)
            out.append(shards[s][:, lo - s * W_IN_SHARD:top - s * W_IN_SHARD])
            lo = top
        return out

    main = jnp.concatenate([p for s, n in _main_segments() for p in cols(s, s + n)], axis=1)
    w_dt = jnp.pad(jnp.concatenate(cols(DT_OFF, DT_OFF + SSD_HEADS), axis=1), ((0, 0), (0, 128 - SSD_HEADS)))
    return main, w_dt


def _w_in_grad_full(g_main, g_dt):
    D = g_main.shape[0]
    ret = jnp.transpose(g_main[:, :C_Z].reshape(D, RET_HEADS, 4, RET_DK), (0, 2, 1, 3)).reshape(D, C_Z)
    return jnp.concatenate([ret, g_main[:, C_Z:C_GATES], _xbc_original(g_main[:, C_XBC:]), g_dt[:, :SSD_HEADS],
                            g_main[:, C_GATES:C_XBC]], axis=1)


def _mm(pairs, M, N, *, tm, tn, out_dtype, name, tb=False, row_off=None):
    P = len(pairs)
    nks = [K // tk for (_, _, _, _, _, K, tk) in pairs]
    starts = [int(s) for s in np.cumsum([0] + nks[:-1])]
    KT = int(sum(nks))
    npf = 0 if row_off is None else 1
    in_specs, args = [], []
    for (a, a_cb, b, b_kb, b_nb, K, tk), s, nk in zip(pairs, starts, nks):
        def kk(k, s=s, nk=nk):
            return jnp.clip(k - s, 0, nk - 1)
        in_specs.append(pl.BlockSpec((tm, tk), lambda m, n, k, *pf, kk=kk, a_cb=a_cb: (m + (pf[0][0] if pf else 0), a_cb + kk(k))))
        if tb:
            in_specs.append(pl.BlockSpec((tn, tk), lambda m, n, k, *pf, kk=kk, b_kb=b_kb, b_nb=b_nb: (b_nb + n, b_kb + kk(k))))
        else:
            in_specs.append(pl.BlockSpec((tk, tn), lambda m, n, k, *pf, kk=kk, b_kb=b_kb, b_nb=b_nb: (b_kb + kk(k), b_nb + n)))
        args += [a, b]

    def body(*refs):
        refs = refs[npf:]
        o_ref = refs[2 * P]
        k = pl.program_id(2)

        def prod(i):
            a = refs[2 * i][...].astype(bf16)
            b = refs[2 * i + 1][...].astype(bf16)
            return _nt(a, b) if tb else _nn(a, b)

        if KT == 1:
            o_ref[...] = prod(0).astype(out_dtype)
            return
        acc = refs[2 * P + 1]

        @pl.when(k == 0)
        def _():
            acc[...] = jnp.zeros_like(acc)

        for i in range(P):
            @pl.when((k >= starts[i]) & (k < starts[i] + nks[i]))
            def _(i=i):
                acc[...] += prod(i)

        @pl.when(k == KT - 1)
        def _():
            o_ref[...] = acc[...].astype(out_dtype)

    grid_spec = pltpu.PrefetchScalarGridSpec(
        num_scalar_prefetch=npf, grid=(M // tm, N // tn, KT), in_specs=in_specs,
        out_specs=pl.BlockSpec((tm, tn), lambda m, n, k, *pf: (m, n)),
        scratch_shapes=[] if KT == 1 else [pltpu.VMEM((tm, tn), f32)])
    return pl.pallas_call(
        body, name=name, grid_spec=grid_spec, out_shape=jax.ShapeDtypeStruct((M, N), out_dtype),
        compiler_params=_params(("parallel", "parallel", "arbitrary")),
    )(*([] if row_off is None else [row_off]), *args)


def _mm1(a, b, *, tm, tn, tk, out_dtype, name, tb=False):
    M, K = a.shape
    N = b.shape[0] if tb else b.shape[1]
    return _mm([(a, 0, b, 0, 0, K, tk)], M, N, tm=tm, tn=tn, out_dtype=out_dtype, name=name, tb=tb)


RS = 16
CS = 32


def _for_strips(n_rows, fn, rs=RS, unroll=4):
    def step(s, carry):
        fn(pl.ds(pl.multiple_of(s * rs, rs), rs))
        return carry
    n = n_rows // rs
    lax.fori_loop(0, n, step, 0, unroll=min(unroll, n))


def _norm1_fwd(x, w, tr):
    S, D = x.shape

    def body(x_ref, w_ref, h_ref, ht_ref):
        def strip(rows):
            xv = x_ref[rows, :]
            r = lax.rsqrt(jnp.mean(xv * xv, axis=-1, keepdims=True) + EPS)
            h_ref[rows, :] = (xv * r * w_ref[...]).astype(bf16)
        _for_strips(tr, strip)
        ht_ref[...] = h_ref[...].T

    return pl.pallas_call(
        body, name="norm1_fwd", grid=(S // tr,),
        in_specs=[pl.BlockSpec((tr, D), lambda i: (i, 0)), pl.BlockSpec((1, D), lambda i: (0, 0))],
        out_specs=[pl.BlockSpec((tr, D), lambda i: (i, 0)), pl.BlockSpec((D, tr), lambda i: (0, i))],
        out_shape=[jax.ShapeDtypeStruct((S, D), bf16), jax.ShapeDtypeStruct((D, S), bf16)], compiler_params=_params(("parallel",)),
    )(x, w)


def _norm1_bwd(x, w, dh, dx2, tr):
    S, D = x.shape

    def body(x_ref, w_ref, dh_ref, dx2_ref, gx_ref, gw_ref, acc):
        @pl.when(pl.program_id(0) == 0)
        def _():
            acc[...] = jnp.zeros_like(acc)

        def strip(rows):
            xv = x_ref[rows, :]
            r = lax.rsqrt(jnp.mean(xv * xv, axis=-1, keepdims=True) + EPS)
            xh = xv * r
            dhv = dh_ref[rows, :]
            acc[...] += dhv * xh
            dxh = dhv * w_ref[...]
            gx_ref[rows, :] = dx2_ref[rows, :] + r * (dxh - xh * jnp.mean(dxh * xh, axis=-1, keepdims=True))
        _for_strips(tr, strip)

        @pl.when(pl.program_id(0) == S // tr - 1)
        def _():
            gw_ref[...] = jnp.sum(acc[...], axis=0, keepdims=True)

    row = pl.BlockSpec((tr, D), lambda i: (i, 0))
    vec = pl.BlockSpec((1, D), lambda i: (0, 0))
    return pl.pallas_call(
        body, name="norm1_bwd", grid=(S // tr,), in_specs=[row, vec, row, row], out_specs=[row, vec],
        out_shape=[jax.ShapeDtypeStruct((S, D), f32), jax.ShapeDtypeStruct((1, D), f32)],
        scratch_shapes=[pltpu.VMEM((RS, D), f32)], compiler_params=_params(("arbitrary",)),
    )(x, w, dh, dx2)


def _final_fwd_bwd(x, mo, target, wf, tr):
    S, D = x.shape

    def body(x_ref, mo_ref, t_ref, w_ref, dx2_ref, dx2b_ref, loss_ref, gw_ref, acc, lacc):
        @pl.when(pl.program_id(0) == 0)
        def _():
            acc[...] = jnp.zeros_like(acc)
            lacc[...] = jnp.zeros_like(lacc)

        def strip(rows):
            x2 = x_ref[rows, :] + mo_ref[rows, :]
            r = lax.rsqrt(jnp.mean(x2 * x2, axis=-1, keepdims=True) + EPS)
            xh = x2 * r
            wv = w_ref[...]
            err = xh * wv - t_ref[rows, :]
            lacc[...] += jnp.mean(err * err, axis=-1, keepdims=True)
            dy = err * (1.0 / D)
            acc[...] += dy * xh
            dxh = dy * wv
            dx2 = r * (dxh - xh * jnp.mean(dxh * xh, axis=-1, keepdims=True))
            dx2_ref[rows, :] = dx2
            dx2b_ref[rows, :] = dx2.astype(bf16)
        _for_strips(tr, strip)

        @pl.when(pl.program_id(0) == S // tr - 1)
        def _():
            gw_ref[...] = jnp.sum(acc[...], axis=0, keepdims=True)
            loss_ref[...] = 0.5 * jnp.sum(lacc[...], axis=0, keepdims=True)

    row = pl.BlockSpec((tr, D), lambda i: (i, 0))
    vec = pl.BlockSpec((1, D), lambda i: (0, 0))
    return pl.pallas_call(
        body, name="final_norm_loss", grid=(S // tr,), in_specs=[row, row, row, vec],
        out_specs=[row, row, pl.BlockSpec((1, 1), lambda i: (0, 0)), vec],
        out_shape=[jax.ShapeDtypeStruct((S, D), f32), jax.ShapeDtypeStruct((S, D), bf16), jax.ShapeDtypeStruct((1, 1), f32),
                   jax.ShapeDtypeStruct((1, D), f32)],
        scratch_shapes=[pltpu.VMEM((RS, D), f32), pltpu.VMEM((RS, 1), f32)], compiler_params=_params(("arbitrary",)),
    )(x, mo, target, wf)


def _merge_fwd(p_r, p_s, proj, tr):
    S, D = p_r.shape

    def body(pr_ref, ps_ref, g_ref, o_ref, ot_ref):
        def strip(rows):
            gr, gs = g_ref[rows, pl.ds(0, D)], g_ref[rows, pl.ds(D, D)]
            o_ref[rows, :] = (jax.nn.sigmoid(gr) * pr_ref[rows, :] + jax.nn.sigmoid(gs) * ps_ref[rows, :]).astype(bf16)
        _for_strips(tr, strip)
        ot_ref[...] = o_ref[...].T

    row = pl.BlockSpec((tr, D), lambda i: (i, 0))
    return pl.pallas_call(
        body, name="merge_fwd", grid=(S // tr,),
        in_specs=[row, row, pl.BlockSpec((tr, 2 * D), lambda i: (i, C_GATES // (2 * D)))],
        out_specs=[row, pl.BlockSpec((D, tr), lambda i: (0, i))],
        out_shape=[jax.ShapeDtypeStruct((S, D), bf16), jax.ShapeDtypeStruct((D, S), bf16)], compiler_params=_params(("parallel",)),
    )(p_r, p_s, proj)


def _merge_bwd(dm, p_r, p_s, proj, tr):
    S, D = p_r.shape

    def body(dm_ref, pr_ref, ps_ref, g_ref, dpr_ref, dps_ref, dproj_ref):
        def strip(rows):
            dmv = dm_ref[rows, :]
            sr = jax.nn.sigmoid(g_ref[rows, pl.ds(0, D)])
            ss = jax.nn.sigmoid(g_ref[rows, pl.ds(D, D)])
            dpr_ref[rows, :] = (dmv * sr).astype(bf16)
            dps_ref[rows, :] = (dmv * ss).astype(bf16)
            dproj_ref[rows, pl.ds(0, D)] = (dmv * pr_ref[rows, :] * sr * (1.0 - sr)).astype(bf16)
            dproj_ref[rows, pl.ds(D, D)] = (dmv * ps_ref[rows, :] * ss * (1.0 - ss)).astype(bf16)
        _for_strips(tr, strip)

    row = pl.BlockSpec((tr, D), lambda i: (i, 0))
    gates = pl.BlockSpec((tr, 2 * D), lambda i: (i, C_GATES // (2 * D)))
    o = jax.ShapeDtypeStruct((S, D), bf16)
    return pl.pallas_call(
        body, name="merge_bwd", grid=(S // tr,), in_specs=[row, row, row, gates],
        out_specs=[row, row, gates], out_shape=[o, o, jax.ShapeDtypeStruct((S, N_MAIN), bf16)],
        compiler_params=_params(("parallel",)),
    )(dm, p_r, p_s, proj)


def _ssd_norm_fwd(y, proj, w, tr):
    S, W = y.shape

    def body(y_ref, z_ref, w_ref, o_ref, ot_ref):
        def strip(rows):
            u = y_ref[rows, :] * _silu(z_ref[rows, :])
            r = lax.rsqrt(jnp.mean(u * u, axis=-1, keepdims=True) + EPS)
            o_ref[rows, :] = (u * r * w_ref[...]).astype(bf16)
        _for_strips(tr, strip)
        ot_ref[...] = o_ref[...].T

    row = pl.BlockSpec((tr, W), lambda i: (i, 0))
    return pl.pallas_call(
        body, name="ssd_norm_fwd", grid=(S // tr,),
        in_specs=[row, pl.BlockSpec((tr, W), lambda i: (i, C_Z // W)), pl.BlockSpec((1, W), lambda i: (0, 0))],
        out_specs=[row, pl.BlockSpec((W, tr), lambda i: (0, i))],
        out_shape=[jax.ShapeDtypeStruct((S, W), bf16), jax.ShapeDtypeStruct((W, S), bf16)], compiler_params=_params(("parallel",)),
    )(y, proj, w)


def _ssd_norm_bwd(y, proj, w, dys, dproj, tr):
    S, W = y.shape

    def body(y_ref, z_ref, w_ref, d_ref, _, dy_ref, dz_ref, gw_ref, acc):
        @pl.when(pl.program_id(0) == 0)
        def _():
            acc[...] = jnp.zeros_like(acc)

        def strip(rows):
            yv, zv, dv = y_ref[rows, :], z_ref[rows, :], d_ref[rows, :]
            sz = _silu(zv)
            u = yv * sz
            r = lax.rsqrt(jnp.mean(u * u, axis=-1, keepdims=True) + EPS)
            un = u * r
            acc[...] += dv * un
            dun = dv * w_ref[...]
            du = r * (dun - un * jnp.mean(dun * un, axis=-1, keepdims=True))
            dy_ref[rows, :] = (du * sz).astype(bf16)
            dz_ref[rows, :] = (du * yv * _dsilu(zv)).astype(bf16)
        _for_strips(tr, strip)

        @pl.when(pl.program_id(0) == S // tr - 1)
        def _():
            gw_ref[...] = jnp.sum(acc[...], axis=0, keepdims=True)

    row = pl.BlockSpec((tr, W), lambda i: (i, 0))
    zcol = pl.BlockSpec((tr, W), lambda i: (i, C_Z // W))
    vec = pl.BlockSpec((1, W), lambda i: (0, 0))
    return pl.pallas_call(
        body, name="ssd_norm_bwd", grid=(S // tr,),
        in_specs=[row, zcol, vec, row, ANY], out_specs=[row, zcol, vec],
        out_shape=[jax.ShapeDtypeStruct((S, W), bf16), jax.ShapeDtypeStruct(dproj.shape, bf16), jax.ShapeDtypeStruct((1, W), f32)],
        input_output_aliases={4: 1}, scratch_shapes=[pltpu.VMEM((RS, W), f32)], compiler_params=_params(("arbitrary",)),
    )(y, proj, w, dys, dproj)


def _rope(t, cos, sin):
    t1, t2 = t[:, :128], t[:, 128:]
    return jnp.concatenate([t1 * cos - t2 * sin, t2 * cos + t1 * sin], axis=1)


def _rope_t(d, cos, sin):
    d1, d2 = d[:, :128], d[:, 128:]
    return jnp.concatenate([d1 * cos + d2 * sin, d2 * cos - d1 * sin], axis=1)


def _ret_specs(tb, rev_nb=None):
    def blk(i):
        return i if rev_nb is None else rev_nb - 1 - i
    head = pl.BlockSpec((tb, RET_HW), lambda h, i: (blk(i), h))
    tab = pl.BlockSpec((tb, 128), lambda h, i: (blk(i), 0))
    mat = pl.BlockSpec((1, CHUNK, CHUNK), lambda h, i: (h, 0, 0))
    vec = pl.BlockSpec((1, CHUNK, 1), lambda h, i: (h, 0, 0))
    one = pl.BlockSpec((1, 1, 1), lambda h, i: (h, 0, 0))
    own = pl.BlockSpec((tb, RET_DK), lambda h, i: (blk(i), h))
    st = pl.BlockSpec((1, tb // CHUNK, RET_DK, RET_DK), lambda h, i: (h, blk(i), 0, 0))
    return head, tab, mat, vec, one, own, st


def _ret_fwd(proj, cos, sin, intra, qdec, kdec, cdec, tb):
    S = proj.shape[0]
    nc = S // CHUNK
    scale = RET_DK ** -0.5
    dk = RET_DK

    def body(p_ref, cos_ref, sin_ref, m_ref, qd_ref, kd_ref, cd_ref, y_ref, yr_ref, yrt_ref, st_ref, st):
        @pl.when(pl.program_id(1) == 0)
        def _():
            st[...] = jnp.zeros_like(st)

        mm, qd, kd, cd = m_ref[0], qd_ref[0], kd_ref[0], cd_ref[0]

        def chunk(c, carry):
            rows = pl.ds(pl.multiple_of(c * CHUNK, CHUNK), CHUNK)
            cs, sn = cos_ref[rows, :], sin_ref[rows, :]
            qr = _rope(p_ref[rows, pl.ds(0, dk)], cs, sn)
            kr = _rope(p_ref[rows, pl.ds(dk, dk)], cs, sn) * scale
            qb, kb, vb = qr.astype(bf16), kr.astype(bf16), p_ref[rows, pl.ds(2 * dk, dk)].astype(bf16)
            stb = st[...].astype(bf16)
            st_ref[0, c] = stb
            sc = (_nt(qb, kb) * mm).astype(bf16)
            y = _nn(sc, vb) + _nn(qb, stb) * qd
            st[...] = st[...] * cd + _tn((kr * kd).astype(bf16), vb)
            y_ref[rows, :] = y
            mu = jnp.mean(y, axis=-1, keepdims=True)
            yc = y - mu
            var = jnp.mean(yc * yc, axis=-1, keepdims=True)
            yr_ref[rows, :] = (yc * lax.rsqrt(var + EPS) * _silu(p_ref[rows, pl.ds(3 * dk, dk)])).astype(bf16)
            return carry

        lax.fori_loop(0, tb // CHUNK, chunk, 0, unroll=min(4, tb // CHUNK))
        yrt_ref[...] = yr_ref[...].T

    head, tab, mat, vec, one, own, stspec = _ret_specs(tb)
    return pl.pallas_call(
        body, name="ret_fwd", grid=(RET_HEADS, S // tb),
        in_specs=[head, tab, tab, mat, vec, vec, one],
        out_specs=[own, own, pl.BlockSpec((RET_DK, tb), lambda h, i: (h, i)), stspec],
        out_shape=[jax.ShapeDtypeStruct((S, 2048), f32), jax.ShapeDtypeStruct((S, 2048), bf16), jax.ShapeDtypeStruct((2048, S), bf16),
                   jax.ShapeDtypeStruct((RET_HEADS, nc, dk, dk), bf16)],
        scratch_shapes=[pltpu.VMEM((dk, dk), f32)], compiler_params=_params(("parallel", "arbitrary")),
    )(proj, cos, sin, intra, qdec, kdec, cdec)


def _ret_bwd(proj, cos, sin, intra, qdec, kdec, cdec, y, dyr, states, dproj, tb):
    S = proj.shape[0]
    nb = S // tb
    nck = tb // CHUNK
    scale = RET_DK ** -0.5
    dk = RET_DK

    def body(p_ref, cos_ref, sin_ref, m_ref, qd_ref, kd_ref, cd_ref, y_ref, dyr_ref, st_ref, _, o_ref, dst):
        @pl.when(pl.program_id(1) == 0)
        def _():
            dst[...] = jnp.zeros_like(dst)

        mm, qd, kd, cd = m_ref[0], qd_ref[0], kd_ref[0], cd_ref[0]

        def chunk(cc, carry):
            c = nck - 1 - cc
            rows = pl.ds(pl.multiple_of(c * CHUNK, CHUNK), CHUNK)
            cs, sn = cos_ref[rows, :], sin_ref[rows, :]
            qr = _rope(p_ref[rows, pl.ds(0, dk)], cs, sn)
            kr = _rope(p_ref[rows, pl.ds(dk, dk)], cs, sn) * scale
            qb, kb, vb = qr.astype(bf16), kr.astype(bf16), p_ref[rows, pl.ds(2 * dk, dk)].astype(bf16)
            kdb = (kr * kd).astype(bf16)
            stb = st_ref[0, c]
            yv, gv, dyrv = y_ref[rows, :], p_ref[rows, pl.ds(3 * dk, dk)], dyr_ref[rows, :]
            mu = jnp.mean(yv, axis=-1, keepdims=True)
            yc = yv - mu
            rstd = lax.rsqrt(jnp.mean(yc * yc, axis=-1, keepdims=True) + EPS)
            yn = yc * rstd
            o_ref[rows, pl.ds(3 * dk, dk)] = (dyrv * yn * _dsilu(gv)).astype(bf16)
            dyn = dyrv * _silu(gv)
            dy = rstd * (dyn - jnp.mean(dyn, axis=-1, keepdims=True) - yn * jnp.mean(dyn * yn, axis=-1, keepdims=True))
            dyb = dy.astype(bf16)
            dyqb = (dy * qd).astype(bf16)
            dstb = dst[...].astype(bf16)
            sct = (_nt(kb, qb) * mm).astype(bf16)
            ds = (_nt(dyb, vb) * mm).astype(bf16)
            dsT = (_nt(vb, dyb) * mm).astype(bf16)
            dv = _nn(sct, dyb) + _nn(kdb, dstb)
            dqr = _nn(ds, kb) + _nt(dyqb, stb)
            dkr = _nn(dsT, qb) + _nt(vb, dstb) * kd
            dst[...] = dst[...] * cd + _tn(qb, dyqb)
            o_ref[rows, pl.ds(0, dk)] = _rope_t(dqr, cs, sn).astype(bf16)
            o_ref[rows, pl.ds(dk, dk)] = (_rope_t(dkr, cs, sn) * scale).astype(bf16)
            o_ref[rows, pl.ds(2 * dk, dk)] = dv.astype(bf16)
            return carry

        lax.fori_loop(0, nck, chunk, 0, unroll=min(4, nck))

    head, tab, mat, vec, one, own, stspec = _ret_specs(tb, rev_nb=nb)
    return pl.pallas_call(
        body, name="ret_bwd", grid=(RET_HEADS, nb),
        in_specs=[head, tab, tab, mat, vec, vec, one, own, own, stspec, ANY],
        out_specs=head, out_shape=jax.ShapeDtypeStruct(dproj.shape, bf16), input_output_aliases={10: 0},
        scratch_shapes=[pltpu.VMEM((dk, dk), f32)], compiler_params=_params(("parallel", "arbitrary")),
    )(proj, cos, sin, intra, qdec, kdec, cdec, y, dyr, states, dproj)


def _conv_fwd(proj, conv_w, conv_b, tb, cw):
    S = proj.shape[0]
    off = C_XBC // cw

    def body(x_ref, halo_ref, w_ref, b_ref, o_ref, xe):
        xe[pl.ds(0, 8), :] = jnp.where(pl.program_id(1) == 0, 0.0, halo_ref[...])
        xe[pl.ds(8, CS), :] = x_ref[pl.ds(0, CS), :]
        ws = [w_ref[pl.ds(j, 1), :] for j in range(SSD_CONV)]
        for s in range(tb // CS):
            tap = (lambda j: xe[pl.ds(5 + j, CS), :]) if s == 0 else (lambda j, s=s: x_ref[pl.ds(s * CS - 3 + j, CS), :])
            acc = b_ref[...] + ws[0] * tap(0)
            for j in range(1, SSD_CONV):
                acc = acc + ws[j] * tap(j)
            o_ref[pl.ds(s * CS, CS), :] = acc

    return pl.pallas_call(
        body, name="conv_fwd", grid=(CONV_DIM // cw, S // tb),
        in_specs=[pl.BlockSpec((tb, cw), lambda j, i: (i, off + j)),
                  pl.BlockSpec((8, cw), lambda j, i: (jnp.maximum(i * (tb // 8) - 1, 0), off + j)),
                  pl.BlockSpec((SSD_CONV, cw), lambda j, i: (0, j)), pl.BlockSpec((1, cw), lambda j, i: (0, j))],
        out_specs=pl.BlockSpec((tb, cw), lambda j, i: (i, j)),
        out_shape=jax.ShapeDtypeStruct((S, CONV_DIM), f32),
        scratch_shapes=[pltpu.VMEM((CS + 8, cw), f32)], compiler_params=_params(("parallel", "arbitrary")),
    )(proj, proj, conv_w, conv_b)


def _conv_bwd(dpre, proj, conv_w, dproj, tb, cw):
    S, n = dpre.shape
    nb = S // tb
    xoff = C_XBC // cw

    def body(d_ref, dh_ref, x_ref, xh_ref, w_ref, _, dx_ref, gw_ref, gb_ref, de, xe, accw, accb):
        i = pl.program_id(1)

        @pl.when(i == 0)
        def _():
            accw[...] = jnp.zeros_like(accw)
            accb[...] = jnp.zeros_like(accb)

        ns = tb // CS
        de[pl.ds(0, CS), :] = d_ref[pl.ds(tb - CS, CS), :]
        de[pl.ds(CS, 8), :] = jnp.where(i == nb - 1, 0.0, dh_ref[...])
        xe[pl.ds(0, 8), :] = jnp.where(i == 0, 0.0, xh_ref[...])
        xe[pl.ds(8, CS), :] = x_ref[pl.ds(0, CS), :]
        ws = [w_ref[pl.ds(j, 1), :] for j in range(SSD_CONV)]
        fold = lambda p: sum(p[8 * q:8 * (q + 1)] for q in range(1, CS // 8)) + p[0:8]
        for s in range(ns):
            dv = d_ref[pl.ds(s * CS, CS), :]
            ahead = (lambda o: de[pl.ds(o, CS), :]) if s == ns - 1 else (lambda o, s=s: d_ref[pl.ds(s * CS + o, CS), :])
            xtap = (lambda j: xe[pl.ds(5 + j, CS), :]) if s == 0 else (lambda j, s=s: x_ref[pl.ds(s * CS - 3 + j, CS), :])
            acc = ws[SSD_CONV - 1] * dv
            for j in range(SSD_CONV - 1):
                acc = acc + ws[j] * ahead(3 - j)
            dx_ref[pl.ds(s * CS, CS), :] = acc.astype(bf16)
            accb[...] += fold(dv)
            for j in range(SSD_CONV):
                accw[j] += fold(dv * xtap(j))

        @pl.when(i == nb - 1)
        def _():
            gb_ref[...] = jnp.sum(accb[...], axis=0, keepdims=True)
            for j in range(SSD_CONV):
                gw_ref[pl.ds(j, 1), :] = jnp.sum(accw[j], axis=0, keepdims=True)

    return pl.pallas_call(
        body, name="conv_bwd", grid=(n // cw, nb),
        in_specs=[pl.BlockSpec((tb, cw), lambda j, i: (i, j)),
                  pl.BlockSpec((8, cw), lambda j, i: (jnp.minimum((i + 1) * (tb // 8), S // 8 - 1), j)),
                  pl.BlockSpec((tb, cw), lambda j, i: (i, xoff + j)),
                  pl.BlockSpec((8, cw), lambda j, i: (jnp.maximum(i * (tb // 8) - 1, 0), xoff + j)),
                  pl.BlockSpec((SSD_CONV, cw), lambda j, i: (0, j)), ANY],
        out_specs=[pl.BlockSpec((tb, cw), lambda j, i: (i, xoff + j)), pl.BlockSpec((SSD_CONV, cw), lambda j, i: (0, j)),
                   pl.BlockSpec((1, cw), lambda j, i: (0, j))],
        out_shape=[jax.ShapeDtypeStruct(dproj.shape, bf16), jax.ShapeDtypeStruct((SSD_CONV, n), f32), jax.ShapeDtypeStruct((1, n), f32)],
        input_output_aliases={5: 0},
        scratch_shapes=[pltpu.VMEM((CS + 8, cw), f32), pltpu.VMEM((CS + 8, cw), f32), pltpu.VMEM((SSD_CONV, 8, cw), f32),
                        pltpu.VMEM((8, cw), f32)],
        compiler_params=_params(("parallel", "arbitrary")),
    )(dpre, dpre, proj, proj, conv_w, dproj)


def _dt_prep(dt_raw, dt_bias, a_log, tb):
    S = dt_raw.shape[0]

    def body(r_ref, b_ref, al_ref, dt_ref, sg_ref, ac_ref):
        li = lax.broadcasted_iota(jnp.int32, (LS, LS), 0)
        si = lax.broadcasted_iota(jnp.int32, (LS, LS), 1)
        tri = (li >= si).astype(f32)
        neg_a = -jnp.exp(al_ref[...])
        for c in range(tb // LS):
            rows = pl.ds(c * LS, LS)
            xv = r_ref[rows, :] + b_ref[...]
            dtv = jax.nn.softplus(xv)
            dt_ref[rows, :] = dtv
            sg_ref[rows, :] = jax.nn.sigmoid(xv)
            ac_ref[rows, :] = _hi(tri, dtv * neg_a)

    row = pl.BlockSpec((tb, 128), lambda i: (i, 0))
    vec = pl.BlockSpec((1, 128), lambda i: (0, 0))
    o = jax.ShapeDtypeStruct((S, 128), f32)
    return pl.pallas_call(body, name="dt_prep", grid=(S // tb,), in_specs=[row, vec, vec], out_specs=[row, row, row],
                          out_shape=[o, o, o], compiler_params=_params(("parallel",)))(dt_raw, dt_bias, a_log)


def _group_major(t):
    S = t.shape[0]
    return jnp.transpose(t[:, :SSD_HEADS].reshape(S, SSD_GROUPS, SSD_HPG), (1, 0, 2))


def _group_major_t(t):
    S = t.shape[0]
    return jnp.transpose(t[:, :SSD_HEADS].reshape(S // LS, LS, SSD_GROUPS, SSD_HPG), (2, 0, 3, 1))


def _ssd_specs(tb, rev_nb=None):
    def blk(i):
        return i if rev_nb is None else rev_nb - 1 - i
    grp = pl.BlockSpec((tb, SSD_GC), lambda g, i: (blk(i), g))
    xs = pl.BlockSpec((tb, SSD_GW), lambda g, i: (blk(i), g))
    ph = pl.BlockSpec((1, tb, SSD_HPG), lambda g, i: (g, blk(i), 0))
    pht = pl.BlockSpec((1, tb // LS, SSD_HPG, LS), lambda g, i: (g, blk(i), 0, 0))
    gvec = pl.BlockSpec((1, 1, SSD_GW), lambda g, i: (g, 0, 0))
    ex = pl.BlockSpec((SSD_HPG, SSD_GW), lambda g, i: (0, 0))
    st = pl.BlockSpec((1, tb // LS, SSD_STATE, SSD_GW), lambda g, i: (g, blk(i), 0, 0))
    return grp, xs, ph, pht, gvec, ex, st


def _expander():
    return jnp.repeat(jnp.eye(SSD_HPG, dtype=f32), SSD_GW // SSD_HPG, axis=1).astype(bf16)


def _expand3(dt8, ac8, ex):
    stack = jnp.concatenate([dt8, jnp.exp(ac8), jnp.exp(ac8[LS - 1:LS, :] - ac8)], axis=0)
    wide = _sel_r(stack, ex)
    return wide[0:LS], wide[LS:2 * LS], wide[2 * LS:3 * LS]


def _ssd_fwd(pre, dt_g, ac_g, act_g, dskx, tb):
    S = pre.shape[0]
    nc = S // LS
    hd = SSD_GW // SSD_HPG

    def body(p_ref, dt_ref, ac_ref, act_ref, dsk_ref, ex_ref, y_ref, st_ref, st):
        @pl.when(pl.program_id(1) == 0)
        def _():
            st[...] = jnp.zeros_like(st)

        ex = ex_ref[...]
        li = lax.broadcasted_iota(jnp.int32, (LS, LS), 0)
        si = lax.broadcasted_iota(jnp.int32, (LS, LS), 1)
        causal = li >= si

        def chunk(c, carry):
            rows = pl.ds(pl.multiple_of(c * LS, LS), LS)
            xs = _silu(p_ref[rows, pl.ds(0, SSD_GW)])
            bcb = _silu(p_ref[rows, pl.ds(SSD_GW, SSD_STATE)]).astype(bf16)
            ccb = _silu(p_ref[rows, pl.ds(SSD_GW + SSD_STATE, SSD_STATE)]).astype(bf16)
            dt8, ac8, act = dt_ref[0, rows, :], ac_ref[0, rows, :], act_ref[0, c]
            dtx, eax, tailx = _expand3(dt8, ac8, ex)
            xdt = xs * dtx
            cb = _nt(ccb, bcb)
            stb = st[...].astype(bf16)
            st_ref[0, c] = stb
            xdtb = xdt.astype(bf16)
            outs = []
            for h in range(SSD_HPG):
                dec = jnp.exp(jnp.where(causal, ac8[:, h:h + 1] - act[h:h + 1, :], -1e30))
                outs.append(_nn((cb * dec).astype(bf16), xdtb[:, hd * h:hd * (h + 1)]))
            y_ref[rows, :] = jnp.concatenate(outs, axis=1) + _nn(ccb, stb) * eax + dsk_ref[0] * xs
            st[...] = st[...] * eax[LS - 1:LS, :] + _tn(bcb, (xdt * tailx).astype(bf16))
            return carry

        lax.fori_loop(0, tb // LS, chunk, 0, unroll=min(4, tb // LS))

    grp, xs, ph, pht, gvec, ex, stspec = _ssd_specs(tb)
    return pl.pallas_call(
        body, name="ssd_fwd", grid=(SSD_GROUPS, S // tb),
        in_specs=[grp, ph, ph, pht, gvec, ex], out_specs=[xs, stspec],
        out_shape=[jax.ShapeDtypeStruct((S, SSD_WIDTH), f32), jax.ShapeDtypeStruct((SSD_GROUPS, nc, SSD_STATE, SSD_GW), bf16)],
        scratch_shapes=[pltpu.VMEM((SSD_STATE, SSD_GW), f32)], compiler_params=_params(("parallel", "arbitrary")),
    )(pre, dt_g, ac_g, act_g, dskx, _expander())


def _ssd_bwd(pre, dt_g, ac_g, act_g, sg_g, dskx, nega_g, dy, states, tb):
    S = pre.shape[0]
    nb = S // tb
    nck = tb // LS
    hd = SSD_GW // SSD_HPG

    def body(p_ref, dt_ref, ac_ref, act_ref, sg_ref, dsk_ref, na_ref, ex_ref, ext_ref, dy_ref, st_ref,
             dp_ref, ddt_ref, gsk_ref, gal_ref, gdb_ref, dst, skacc):
        @pl.when(pl.program_id(1) == 0)
        def _():
            dst[...] = jnp.zeros_like(dst)
            skacc[...] = jnp.zeros_like(skacc)
            gal_ref[...] = jnp.zeros_like(gal_ref)
            gdb_ref[...] = jnp.zeros_like(gdb_ref)

        ex, ext = ex_ref[...], ext_ref[...]
        li = lax.broadcasted_iota(jnp.int32, (LS, LS), 0)
        si = lax.broadcasted_iota(jnp.int32, (LS, LS), 1)
        causal = li >= si
        anti = si >= li
        upper = anti.astype(bf16)
        eye = (si == li).astype(bf16)
        last_row = (lax.broadcasted_iota(jnp.int32, (LS, 1), 0) == LS - 1).astype(f32)
        head_id = lax.broadcasted_iota(jnp.int32, (1, SSD_HPG), 1)
        head_col = lax.broadcasted_iota(jnp.int32, (SSD_HPG, 1), 0)
        neg_a = na_ref[0]
        dskv = dsk_ref[0]

        def chunk(cc, carry):
            c = nck - 1 - cc
            rows = pl.ds(pl.multiple_of(c * LS, LS), LS)
            px = p_ref[rows, pl.ds(0, SSD_GW)]
            pb = p_ref[rows, pl.ds(SSD_GW, SSD_STATE)]
            pc = p_ref[rows, pl.ds(SSD_GW + SSD_STATE, SSD_STATE)]
            sgx, sgb, sgc = jax.nn.sigmoid(px), jax.nn.sigmoid(pb), jax.nn.sigmoid(pc)
            xs = px * sgx
            bcb = (pb * sgb).astype(bf16)
            ccb = (pc * sgc).astype(bf16)
            dt8, ac8, act = dt_ref[0, rows, :], ac_ref[0, rows, :], act_ref[0, c]
            dtx, eax, tailx = _expand3(dt8, ac8, ex)
            xdt = xs * dtx
            ex_last = eax[LS - 1:LS, :]
            stb = st_ref[0, c]
            dyv = dy_ref[rows, :]
            dyb = dyv.astype(bf16)
            xdtb = xdt.astype(bf16)
            skacc[...] += jnp.sum(dyv * xs, axis=0, keepdims=True)
            yinter = _nn(ccb, stb) * eax
            dzb = (dyv * eax).astype(bf16)
            dcc = _nt(dzb, stb)
            dstv = dst[...]
            dstb = dstv.astype(bf16)
            xt = xdt * tailx
            dxt = _nn(bcb, dstb)
            dbc = _nt(xt.astype(bf16), dstb)
            dxdt = dxt * tailx
            lastrow = jnp.sum(dxt * xt, axis=0, keepdims=True) + jnp.sum(dstv * stb.astype(f32), axis=0, keepdims=True) * ex_last
            dst[...] = dstv * ex_last + _tn(ccb, dzb)
            cb = _nt(ccb, bcb)
            cbt = _nt(bcb, ccb)
            dcb = jnp.zeros((LS, LS), f32)
            dac8 = jnp.zeros((LS, SSD_HPG), f32)
            dact = jnp.zeros((SSD_HPG, LS), f32)
            dxin = []
            for h in range(SSD_HPG):
                sl = slice(hd * h, hd * (h + 1))
                col, rowv = ac8[:, h:h + 1], act[h:h + 1, :]
                dec = jnp.exp(jnp.where(causal, col - rowv, -1e30))
                dect = jnp.exp(jnp.where(anti, rowv - col, -1e30))
                gm = cb * dec
                dgm = _nt(dyb[:, sl], xdtb[:, sl])
                dxin.append(_nn((cbt * dect).astype(bf16), dyb[:, sl]))
                dcb = dcb + dgm * dec
                w = dgm * gm
                dac8 = dac8 + jnp.sum(w, axis=1, keepdims=True) * (head_id == h).astype(f32)
                dact = dact + (head_col == h).astype(f32) * jnp.sum(w, axis=0, keepdims=True)
            dxintra = jnp.concatenate(dxin, axis=1)
            dcbb = dcb.astype(bf16)
            dcc = dcc + _nn(dcbb, bcb)
            dbc = dbc + _tn(dcbb, ccb)
            dxdt = dxdt + dxintra
            dacx = dyv * yinter - dxt * xt + last_row * lastrow
            red = _sel_r(jnp.concatenate([dacx, dxdt * xs], axis=0), ext)
            dac8 = dac8 - _rows_to_cols(dact, eye) + red[0:LS]
            da8 = _sel_l(upper, dac8)
            ddt8 = red[LS:2 * LS] + da8 * neg_a
            gal_ref[0] += jnp.sum(da8 * dt8 * neg_a, axis=0, keepdims=True)
            ddr = ddt8 * sg_ref[0, rows, :]
            ddt_ref[0, rows, :] = ddr
            gdb_ref[0] += jnp.sum(ddr, axis=0, keepdims=True)
            dsilu = lambda p, s: s * (1.0 + p * (1.0 - s))
            dp_ref[rows, pl.ds(0, SSD_GW)] = (dskv * dyv + dxdt * dtx) * dsilu(px, sgx)
            dp_ref[rows, pl.ds(SSD_GW, SSD_STATE)] = dbc * dsilu(pb, sgb)
            dp_ref[rows, pl.ds(SSD_GW + SSD_STATE, SSD_STATE)] = dcc * dsilu(pc, sgc)
            return carry

        lax.fori_loop(0, nck, chunk, 0, unroll=min(4, nck))

        @pl.when(pl.program_id(1) == nb - 1)
        def _():
            gsk_ref[0] = skacc[...]

    grp, xs, ph, pht, gvec, ex, stspec = _ssd_specs(tb, rev_nb=nb)
    small = pl.BlockSpec((1, 1, SSD_HPG), lambda g, i: (g, 0, 0))
    ext = pl.BlockSpec((SSD_GW, SSD_HPG), lambda g, i: (0, 0))
    sm = jax.ShapeDtypeStruct((SSD_GROUPS, 1, SSD_HPG), f32)
    expander = _expander()
    return pl.pallas_call(
        body, name="ssd_bwd", grid=(SSD_GROUPS, nb),
        in_specs=[grp, ph, ph, pht, ph, gvec, small, ex, ext, xs, stspec],
        out_specs=[grp, ph, gvec, small, small],
        out_shape=[jax.ShapeDtypeStruct((S, CONV_DIM), f32), jax.ShapeDtypeStruct((SSD_GROUPS, S, SSD_HPG), f32),
                   jax.ShapeDtypeStruct((SSD_GROUPS, 1, SSD_GW), f32), sm, sm],
        scratch_shapes=[pltpu.VMEM((SSD_STATE, SSD_GW), f32), pltpu.VMEM((1, SSD_GW), f32)],
        compiler_params=_params(("parallel", "arbitrary")),
    )(pre, dt_g, ac_g, act_g, sg_g, dskx, nega_g, expander, expander.T, dy, states)


def _tiles(S):
    return dict(tb=min(512, S), tr=min(256, S), tm=min(1024, S))


def _local_step(x, positions, target, norm1_w, w_main, w_dt, conv_w, conv_b, dt_bias, a_log, d_skip, ssd_norm_w,
                late_token, late_weights, norm_f_w, reducer):
    S, D = x.shape
    t = _tiles(S)
    tb, tr, tm = t["tb"], t["tr"], t["tm"]

    half = RET_DK // 2
    inv_freq = ROPE_THETA ** (-jnp.arange(half, dtype=f32) / half)
    ang = positions.astype(f32)[:, None] * inv_freq
    cos, sin = jnp.cos(ang), jnp.sin(ang)
    log_gamma = jnp.log1p(-(2.0 ** (-5.0 - jnp.arange(RET_HEADS, dtype=f32))))
    idx = jnp.arange(CHUNK, dtype=f32)
    intra = jnp.exp(jnp.abs(idx[:, None] - idx[None, :]) * log_gamma[:, None, None])
    qdec = jnp.exp((idx + 1.0)[None, :] * log_gamma[:, None])[:, :, None]
    kdec = jnp.exp((CHUNK - 1.0 - idx)[None, :] * log_gamma[:, None])[:, :, None]
    cdec = jnp.exp(CHUNK * log_gamma)[:, None, None]
    conv_wm, conv_bm = _xbc_group_major(conv_w), _xbc_group_major(conv_b)

    h, ht = _norm1_fwd(x, norm1_w + late_token[0, 0], tr)
    proj = _mm1(h, w_main, tm=tm, tn=1024, tk=D, out_dtype=f32, name="proj_main")
    dt_raw = _mm1(h, w_dt, tm=tm, tn=128, tk=D, out_dtype=f32, name="proj_dt")
    y_ret, yr, yrt, ret_states = _ret_fwd(proj, cos, sin, intra, qdec, kdec, cdec, tb)
    pre = _conv_fwd(proj, conv_wm, conv_bm, min(1024, S), 512)
    pad64 = lambda v: jnp.pad(v, ((0, 0), (0, 128 - SSD_HEADS)))
    dt, sg, ac = _dt_prep(dt_raw, pad64(dt_bias), pad64(a_log), tb)
    dt_g, ac_g, sg_g, act_g = _group_major(dt), _group_major(ac), _group_major(sg), _group_major_t(ac)
    dskx = jnp.repeat(d_skip.reshape(SSD_GROUPS, 1, SSD_HPG), SSD_GW // SSD_HPG, axis=2)
    nega_g = (-jnp.exp(a_log)).reshape(SSD_GROUPS, 1, SSD_HPG)
    y_ssd, ssd_states = _ssd_fwd(pre, dt_g, ac_g, act_g, dskx, tb)
    ys, yst = _ssd_norm_fwd(y_ssd, proj, ssd_norm_w, tr // 2)
    w_br, w_bs, w_o = late_weights(ys)
    p_r = _mm1(yr, w_br, tm=tm, tn=1024, tk=2048, out_dtype=bf16, name="branch_ret")
    p_s = _mm1(ys, w_bs, tm=tm, tn=1024, tk=2048, out_dtype=bf16, name="branch_ssd")
    merged, mergedt = _merge_fwd(p_r, p_s, proj, tr)
    mo = _mm1(merged, w_o, tm=tm, tn=1024, tk=2048, out_dtype=bf16, name="out_proj")
    dx2, dx2b, loss, g_norm_f = _final_fwd_bwd(x, mo, target, norm_f_w.reshape(1, D), tr)

    tkt = min(4096, S)
    wg = lambda at, b, name, tn=1024: _mm1(at, b, tm=min(1024, at.shape[0]), tn=tn, tk=tkt, out_dtype=f32, name=name)
    dm = _mm1(dx2b, w_o, tm=tm, tn=1024, tk=2048, out_dtype=bf16, name="d_merged", tb=True)
    g_w_o = wg(mergedt, dx2b, "g_w_out")
    dp_r, dp_s, dproj = _merge_bwd(dm, p_r, p_s, proj, tr)
    dyr = _mm1(dp_r, w_br, tm=tm, tn=1024, tk=2048, out_dtype=bf16, name="d_yr", tb=True)
    dys = _mm1(dp_s, w_bs, tm=tm, tn=1024, tk=2048, out_dtype=bf16, name="d_ys", tb=True)
    g_w_br = wg(yrt, dp_r, "g_w_br_ret")
    g_w_bs = wg(yst, dp_s, "g_w_br_ssd")
    dy_ssd, dproj, g_ssd_norm = _ssd_norm_bwd(y_ssd, proj, ssd_norm_w, dys, dproj, tr // 2)
    dproj = _ret_bwd(proj, cos, sin, intra, qdec, kdec, cdec, y_ret, dyr, ret_states, dproj, tb)
    dpre, ddt_g, gsk, gal, gdb = _ssd_bwd(pre, dt_g, ac_g, act_g, sg_g, dskx, nega_g, dy_ssd, ssd_states, tb)
    dproj, gcw, gcb = _conv_bwd(dpre, proj, conv_wm, dproj, min(1024, S), 512)
    ddt = jnp.transpose(ddt_g, (1, 0, 2)).reshape(S, SSD_HEADS)
    ddt_p = jnp.pad(ddt, ((0, 0), (0, 128 - SSD_HEADS))).astype(bf16)

    hr = D // 2
    wg_half = lambda off, b, name, tn=1024: _mm([(ht, 0, b, 0, 0, S, tkt)], hr, b.shape[1], tm=hr, tn=tn, out_dtype=f32,
                                                 name=name, row_off=off)
    off_sib, off_own = reducer.halves()
    gs_main = wg_half(off_sib, dproj, "g_w_in_main_sib")
    gs_dt = wg_half(off_sib, ddt_p, "g_w_in_dt_sib", tn=128)
    swap_state = reducer.first(gs_main, gs_dt, g_w_br, g_w_bs, g_w_o)
    ddt_p = ddt_p + swap_state[-1][0, 0].astype(bf16)
    go_main = wg_half(off_own, dproj, "g_w_in_main_own")
    go_dt = wg_half(off_own, ddt_p, "g_w_in_dt_own", tn=128)
    reduce_state = reducer.second(swap_state, go_main, go_dt)
    ddt_p = ddt_p + reduce_state[-1][0, 0].astype(bf16)
    dh = _mm([(dproj, 0, w_main, 0, 0, N_MAIN, N_MAIN // 8), (ddt_p, 0, w_dt, 0, 0, 128, 128)], S, D, tm=tm, tn=1024,
             out_dtype=bf16, name="d_h", tb=True)
    grad_x, g_norm1 = _norm1_bwd(x, norm1_w, dh, dx2, tr)

    seg = lambda v: jnp.sum(v.reshape(SSD_HEADS, SSD_GW // SSD_HPG), axis=1).reshape(1, SSD_HEADS)
    grads = dict(
        norm1_w=g_norm1, w_in_main=(gs_main, go_main), w_in_dt=(gs_dt, go_dt),
        conv_w=_xbc_original(gcw), conv_b=_xbc_original(gcb),
        dt_bias=gdb.reshape(1, SSD_HEADS), a_log=gal.reshape(1, SSD_HEADS), d_skip=seg(gsk),
        ssd_norm_w=g_ssd_norm, w_br_ret=g_w_br, w_br_ssd=g_w_bs, w_out=g_w_o, norm_f_w=g_norm_f,
    )
    return loss, grad_x, grads, reduce_state


def _me():
    return lax.axis_index("x"), lax.axis_index("y"), lax.axis_index("c")


def _other_chips(x, y):
    return [(1 - x, y), (x, 1 - y), (1 - x, 1 - y)]


def _gather_weights(a, cw):
    R = a.shape[0]
    hr, hq = R // 2, R // 4

    def body(a_ref, cw_ref, ga_ref, gc_ref, send_sems, recv_sems):
        x, y, c = _me()
        me, sibling = (x, y, c), (x, y, 1 - c)
        nx, ny = (1 - x, y, c), (x, 1 - y, c)
        k, kx, ky, kd = 2 * x + y, 2 * (1 - x) + y, 2 * x + (1 - y), 2 * (1 - x) + (1 - y)

        def rows(half, q):
            return pl.ds(pl.multiple_of(half * hr + q * hq, 8), hq)

        def cp(sem, shard, half, q, to, src=None):
            dst = ga_ref.at[shard, rows(half, q), :]
            return pltpu.make_async_remote_copy(src_ref=dst if src is None else src, dst_ref=dst, send_sem=send_sems.at[sem],
                                                recv_sem=recv_sems.at[sem], device_id=to, device_id_type=MESH)

        def small(j, src_shard, to):
            return pltpu.make_async_remote_copy(
                src_ref=cw_ref, dst_ref=gc_ref.at[src_shard], send_sem=send_sems.at[12 + j], recv_sem=recv_sems.at[12 + j],
                device_id=to, device_id_type=MESH)

        own = lambda q: a_ref.at[rows(c, q), :]
        smalls = [small(j, k, (*chip, c)) for j, chip in enumerate(_other_chips(x, y))]
        sends = [cp(0, k, c, 0, nx, own(0)), cp(2, k, c, 1, ny, own(1)), cp(1, k, c, 1, nx, own(1)), cp(3, k, c, 0, ny, own(0))]
        for s in sends + smalls:
            s.start()
        arrivals = [(0, kx, 0, (4, ny)), (2, ky, 1, (5, nx)), (1, kx, 1, None), (3, ky, 0, None), (4, kd, 0, None), (5, kd, 1, None)]
        for sem, shard, q, onward in arrivals:
            cp(sem, shard, c, q, me).wait_recv()
            if onward is not None:
                sends.append(cp(onward[0], shard, c, q, onward[1]))
                sends[-1].start()
            sends.append(cp(6 + sem, shard, c, q, sibling))
            sends[-1].start()
        for sem, shard, q, _ in arrivals:
            cp(6 + sem, shard, 1 - c, q, me).wait_recv()
        for j, chip in enumerate(_other_chips(x, y)):
            small(j, 2 * chip[0] + chip[1], me).wait_recv()
        for s in sends + smalls:
            s.wait_send()

    return pl.pallas_call(
        body, name="gather_weights", in_specs=[ANY, ANY], out_specs=[ANY, ANY],
        out_shape=[jax.ShapeDtypeStruct((N_SHARD,) + a.shape, a.dtype), jax.ShapeDtypeStruct((N_SHARD,) + cw.shape, cw.dtype)],
        scratch_shapes=[pltpu.SemaphoreType.DMA((15,)), pltpu.SemaphoreType.DMA((15,))],
        compiler_params=pltpu.CompilerParams(has_side_effects=True),
    )(a, cw)


def _gather_late_copies(src, land, send_sems, recv_sems):
    x, y, c = _me()
    k = 2 * x + y
    return [pltpu.make_async_remote_copy(src_ref=src, dst_ref=land.at[k], send_sem=send_sems.at[j], recv_sem=recv_sems.at[j],
                                         device_id=(*chip, c), device_id_type=MESH) for j, chip in enumerate(_other_chips(x, y))]


def _gather_late_start(b):
    land = lax.empty((N_SHARD,) + b.shape, b.dtype)

    def body(b_ref, land_ref, send_sems, recv_sems, b_thru, land_thru, token):
        for cp in _gather_late_copies(b_ref, land_ref, send_sems, recv_sems):
            cp.start()
        token[...] = jnp.zeros_like(token)

    return pl.pallas_call(
        body, name="gather_late_start", in_specs=[HBM, HBM],
        out_specs=(SEM, SEM, HBM, HBM, pl.BlockSpec(memory_space=pltpu.VMEM)),
        out_shape=(pltpu.SemaphoreType.DMA((3,)), pltpu.SemaphoreType.DMA((3,)), pltpu.HBM(b.shape, b.dtype),
                   pltpu.HBM(land.shape, land.dtype), jax.ShapeDtypeStruct((8, 128), f32)),
        input_output_aliases={0: 2, 1: 3}, compiler_params=pltpu.CompilerParams(has_side_effects=DATAFLOW),
    )(pltpu.with_memory_space_constraint(b, pltpu.HBM), pltpu.with_memory_space_constraint(land, pltpu.HBM))


def _gather_late_wait(send_sems, recv_sems, src, land, after):
    def body(b_ref, land_ref, send_sems_ref, recv_sems_ref, after_ref, b_dead, land_out):
        x, y, c = _me()
        for j, chip in enumerate(_other_chips(x, y)):
            kk = 2 * chip[0] + chip[1]
            cp = pltpu.make_async_remote_copy(src_ref=b_ref, dst_ref=land_ref.at[kk], send_sem=send_sems_ref.at[j],
                                              recv_sem=recv_sems_ref.at[j], device_id=(x, y, c), device_id_type=MESH)
            cp.wait_send()
            cp.wait_recv()

    return pl.pallas_call(
        body, name="gather_late_wait", in_specs=[HBM, HBM, SEM, SEM, ANY], out_specs=[HBM, HBM],
        out_shape=[pltpu.HBM(src.shape, src.dtype), pltpu.HBM(land.shape, land.dtype)], input_output_aliases={0: 0, 1: 1},
        compiler_params=pltpu.CompilerParams(has_side_effects=DATAFLOW),
    )(src, land, send_sems, recv_sems, after)[1]


HBM = pl.BlockSpec(memory_space=pltpu.HBM)
SEM = pl.BlockSpec(memory_space=pltpu.SEMAPHORE)
DATAFLOW = pltpu.SideEffectType.DATAFLOW_SIDE_EFFECTING


def _swap_copies(srcs, lands, send_sems, recv_sems):
    x, y, c = _me()

    def cp(src, dst, q):
        return pltpu.make_async_remote_copy(src_ref=src, dst_ref=dst, send_sem=send_sems.at[q], recv_sem=recv_sems.at[q],
                                            device_id=(x, y, 1 - c), device_id_type=MESH)

    return [cp(srcs[0], lands[0], 0), cp(srcs[1], lands[1], 1)] + [cp(srcs[2].at[s, 1 - c], lands[2].at[s], 2 + s) for s in range(N_SHARD)]


def _sibling_swap_start(g_main, g_dt, g_b):
    srcs = [g_main, g_dt, g_b]
    lands = [lax.empty(g_main.shape, g_main.dtype), lax.empty(g_dt.shape, g_dt.dtype),
             lax.empty(g_b.shape[:1] + g_b.shape[2:], g_b.dtype)]

    def body(*refs):
        for cp in _swap_copies(refs[0:3], refs[3:6], refs[6], refs[7]):
            cp.start()
        refs[14][...] = jnp.zeros_like(refs[14])

    hbm = lambda a: pltpu.HBM(a.shape, a.dtype)
    out = pl.pallas_call(
        body, name="sibling_swap_start", in_specs=[HBM] * 6,
        out_specs=(SEM, SEM, *[HBM] * 6, pl.BlockSpec(memory_space=pltpu.VMEM)),
        out_shape=(pltpu.SemaphoreType.DMA((2 + N_SHARD,)), pltpu.SemaphoreType.DMA((2 + N_SHARD,)), *[hbm(a) for a in srcs + lands],
                   jax.ShapeDtypeStruct((8, 128), f32)),
        input_output_aliases={t: 2 + t for t in range(6)}, compiler_params=pltpu.CompilerParams(has_side_effects=DATAFLOW),
    )(*[pltpu.with_memory_space_constraint(a, pltpu.HBM) for a in srcs + lands])
    return out[0], out[1], list(out[2:5]), list(out[5:8]), out[8]


def _sibling_swap_wait(send_sems, recv_sems, srcs, lands, after):
    def body(*refs):
        for cp in _swap_copies(refs[0:3], refs[3:6], refs[6], refs[7]):
            cp.wait_send()
            cp.wait_recv()

    hbm = lambda a: pltpu.HBM(a.shape, a.dtype)
    out = pl.pallas_call(
        body, name="sibling_swap_wait", in_specs=[HBM] * 6 + [SEM, SEM, ANY], out_specs=[HBM] * 6,
        out_shape=[hbm(a) for a in list(srcs) + list(lands)], input_output_aliases={t: t for t in range(6)},
        compiler_params=pltpu.CompilerParams(has_side_effects=DATAFLOW),
    )(*srcs, *lands, send_sems, recv_sems, after)
    return list(out[:3]), list(out[3:])


def _exchange_copies(ins, lands, send_sems, recv_sems):
    n = len(ins)
    x, y, c = _me()
    cps = []
    for j, chip in enumerate(_other_chips(x, y)):
        kk = 2 * chip[0] + chip[1]
        for t in range(n):
            cps.append(pltpu.make_async_remote_copy(
                src_ref=ins[t].at[kk], dst_ref=lands[t].at[j], send_sem=send_sems.at[n * j + t],
                recv_sem=recv_sems.at[n * j + t], device_id=(*chip, c), device_id_type=MESH))
    return cps


def _chip_exchange_start(arrs):
    n = len(arrs)
    lands = [lax.empty((3,) + a.shape[1:], a.dtype) for a in arrs]

    def body(*refs):
        ins, lands_in = refs[:n], refs[n:2 * n]
        send_sems, recv_sems = refs[2 * n], refs[2 * n + 1]
        token = refs[4 * n + 2]
        for cp in _exchange_copies(ins, lands_in, send_sems, recv_sems):
            cp.start()
        token[...] = jnp.zeros_like(token)

    hbm = lambda a: pltpu.HBM(a.shape, a.dtype)
    out = pl.pallas_call(
        body, name="chip_exchange_start", in_specs=[HBM] * (2 * n),
        out_specs=(SEM, SEM, *[HBM] * (2 * n), pl.BlockSpec(memory_space=pltpu.VMEM)),
        out_shape=(pltpu.SemaphoreType.DMA((3 * n,)), pltpu.SemaphoreType.DMA((3 * n,)), *[hbm(a) for a in arrs],
                   *[hbm(a) for a in lands], jax.ShapeDtypeStruct((8, 128), f32)),
        input_output_aliases={t: 2 + t for t in range(2 * n)},
        compiler_params=pltpu.CompilerParams(has_side_effects=DATAFLOW),
    )(*[pltpu.with_memory_space_constraint(a, pltpu.HBM) for a in list(arrs) + lands])
    return out[0], out[1], list(out[2:2 + n]), list(out[2 + n:2 + 2 * n]), out[2 + 2 * n]


def _chip_exchange_wait(send_sems, recv_sems, srcs, lands, after):
    n = len(srcs)

    def body(*refs):
        ins, lands_in = refs[:n], refs[n:2 * n]
        send_sems_ref, recv_sems_ref = refs[2 * n], refs[2 * n + 1]
        for cp in _exchange_copies(ins, lands_in, send_sems_ref, recv_sems_ref):
            cp.wait_send()
            cp.wait_recv()

    hbm = lambda a: pltpu.HBM(a.shape, a.dtype)
    out = pl.pallas_call(
        body, name="chip_exchange_wait", in_specs=[HBM] * (2 * n) + [SEM, SEM, ANY],
        out_specs=[HBM] * (2 * n), out_shape=[hbm(a) for a in list(srcs) + list(lands)],
        input_output_aliases={t: t for t in range(2 * n)},
        compiler_params=pltpu.CompilerParams(has_side_effects=DATAFLOW),
    )(*srcs, *lands, send_sems, recv_sems, after)
    return list(out[:n]), list(out[n:])


def _share_halves(bufs, by_cols, name):
    n = len(bufs)

    def body(*refs):
        ins, outs = refs[:n], refs[n:2 * n]
        send_sems, recv_sems = refs[2 * n], refs[2 * n + 1]
        x, y, c = _me()

        def part(ref, t, half):
            if by_cols[t]:
                w = bufs[t].shape[1] // 2
                return ref.at[:, pl.ds(pl.multiple_of(half * w, 128), w)]
            return ref.at[half]

        sends = [pltpu.make_async_remote_copy(src_ref=part(ins[t], t, c), dst_ref=part(outs[t], t, c), send_sem=send_sems.at[t],
                                              recv_sem=recv_sems.at[t], device_id=(x, y, 1 - c), device_id_type=MESH) for t in range(n)]
        for cp in sends:
            cp.start()
        for t in range(n):
            pltpu.make_async_remote_copy(src_ref=part(ins[t], t, c), dst_ref=part(outs[t], t, 1 - c), send_sem=send_sems.at[t],
                                         recv_sem=recv_sems.at[t], device_id=(x, y, c), device_id_type=MESH).wait_recv()
        for cp in sends:
            cp.wait_send()

    return pl.pallas_call(
        body, name=name, in_specs=[ANY] * n, out_specs=[ANY] * n,
        out_shape=[jax.ShapeDtypeStruct(a.shape, a.dtype) for a in bufs], input_output_aliases={t: t for t in range(n)},
        scratch_shapes=[pltpu.SemaphoreType.DMA((n,)), pltpu.SemaphoreType.DMA((n,))],
        compiler_params=pltpu.CompilerParams(has_side_effects=True),
    )(*bufs)


def _gather_vec(v):
    n = v.shape[1]

    def body(v_ref, o_ref, send_sems, recv_sems):
        x, y, c = _me()
        me = 4 * x + 2 * y + c
        cps = []
        for j in range(1, 8):
            fx, fy, fc = (j >> 2) & 1, (j >> 1) & 1, j & 1
            peer = (x ^ fx, y ^ fy, c ^ fc)
            cps.append(pltpu.make_async_remote_copy(
                src_ref=v_ref, dst_ref=o_ref.at[pl.ds(me, 1), :], send_sem=send_sems.at[j - 1], recv_sem=recv_sems.at[j - 1],
                device_id=peer, device_id_type=MESH))
        for cp in cps:
            cp.start()
        for j in range(1, 8):
            fx, fy, fc = (j >> 2) & 1, (j >> 1) & 1, j & 1
            src = 4 * (x ^ fx) + 2 * (y ^ fy) + (c ^ fc)
            pltpu.make_async_remote_copy(
                src_ref=v_ref, dst_ref=o_ref.at[pl.ds(src, 1), :], send_sem=send_sems.at[j - 1], recv_sem=recv_sems.at[j - 1],
                device_id=(x, y, c), device_id_type=MESH).wait_recv()
        for cp in cps:
            cp.wait_send()

    return pl.pallas_call(
        body, name="gather_vec", in_specs=[ANY], out_specs=ANY, out_shape=jax.ShapeDtypeStruct((8, n), v.dtype),
        scratch_shapes=[pltpu.SemaphoreType.DMA((7,)), pltpu.SemaphoreType.DMA((7,))],
        compiler_params=pltpu.CompilerParams(has_side_effects=True),
    )(v)


def _pair_sum(g, r, name, tr):
    L, hr, C = r.shape
    both_halves = g.ndim == 4

    def body(c_ref, g_ref, r_ref, o_ref):
        def strip(rows):
            gv = g_ref[0, 0, rows, :] if both_halves else g_ref[0, rows, :]
            o_ref[0, rows, :] = (gv + r_ref[0, rows, :]).astype(bf16)
        _for_strips(tr, strip)

    g_spec = (pl.BlockSpec((1, 1, tr, C), lambda s, i, c_ref: (s, c_ref[0], i, 0)) if both_halves
              else pl.BlockSpec((1, tr, C), lambda s, i, c_ref: (s, i, 0)))
    grid_spec = pltpu.PrefetchScalarGridSpec(
        num_scalar_prefetch=1, grid=(L, hr // tr),
        in_specs=[g_spec, pl.BlockSpec((1, tr, C), lambda s, i, c_ref: (s, i, 0))],
        out_specs=pl.BlockSpec((1, tr, C), lambda s, i, c_ref: (s, i, 0)))
    c = lax.axis_index("c").reshape(1).astype(jnp.int32)
    return pl.pallas_call(body, name=name, grid_spec=grid_spec, out_shape=jax.ShapeDtypeStruct((L, hr, C), bf16),
                          compiler_params=_params(("parallel", "parallel")))(c, g, r)


def _own_sum(p, got, name, transposed=False):
    _, hr, C = p.shape
    tr = SUM_ROWS
    c_full, c_pad = C // 128 * 128, -(-C // 128) * 128

    def total(p_ref, got_ref, rows):
        return ((p_ref[0, rows, :].astype(f32) + got_ref[0, rows, :].astype(f32)) + got_ref[1, rows, :].astype(f32)) \
            + got_ref[2, rows, :].astype(f32)

    def body(idx_ref, p_ref, got_ref, o_ref):
        def strip(rows):
            o_ref[0, rows, :] = total(p_ref, got_ref, rows)
        _for_strips(tr, strip)

    def body_t(idx_ref, p_ref, got_ref, o_ref, buf):
        if c_pad > c_full:
            buf[:, pl.ds(c_full, c_pad - c_full)] = jnp.zeros((tr, c_pad - c_full), f32)

        def strip(rows):
            buf[rows, pl.ds(0, C)] = total(p_ref, got_ref, rows)
        _for_strips(tr, strip)
        o_ref[...] = buf[...].T[:C]

    in_specs = [pl.BlockSpec((1, tr, C), lambda i, idx: (idx[0], i, 0)), pl.BlockSpec((3, tr, C), lambda i, idx: (0, i, 0))]
    x, y, c = _me()
    idx = jnp.stack([2 * x + y, c]).astype(jnp.int32)
    if transposed:
        grid_spec = pltpu.PrefetchScalarGridSpec(num_scalar_prefetch=1, grid=(hr // tr,), in_specs=in_specs,
                                                 out_specs=pl.BlockSpec((C, tr), lambda i, idx: (0, idx[1] * (hr // tr) + i)),
                                                 scratch_shapes=[pltpu.VMEM((tr, c_pad), f32)])
        return pl.pallas_call(body_t, name=name, grid_spec=grid_spec, out_shape=jax.ShapeDtypeStruct((C, 2 * hr), f32),
                              compiler_params=_params(("parallel",)))(idx, p, got)
    grid_spec = pltpu.PrefetchScalarGridSpec(num_scalar_prefetch=1, grid=(hr // tr,), in_specs=in_specs,
                                             out_specs=pl.BlockSpec((1, tr, C), lambda i, idx: (idx[1], i, 0)))
    return pl.pallas_call(body, name=name, grid_spec=grid_spec, out_shape=jax.ShapeDtypeStruct((2, hr, C), f32),
                          compiler_params=_params(("parallel",)))(idx, p, got)


def _adamw(w, g, m, v, name, tr):
    _, R, C = w.shape
    rs = min(8, tr)

    def body(w_ref, g_ref, m_ref, v_ref, d_ref, nm_ref, nv_ref):
        def strip(s, carry):
            rows = pl.ds(pl.multiple_of(s * rs, rs), rs)
            gv = g_ref[0, rows, :]
            mn = ADAM_B1 * m_ref[0, rows, :] + (1.0 - ADAM_B1) * gv
            vn = ADAM_B2 * v_ref[0, rows, :] + (1.0 - ADAM_B2) * (gv * gv)
            m_hat = mn / (1.0 - ADAM_B1 ** ADAM_STEP)
            v_hat = vn / (1.0 - ADAM_B2 ** ADAM_STEP)
            d_ref[0, rows, :] = -ADAM_LR * (m_hat / (jnp.sqrt(v_hat) + ADAM_EPS) + ADAM_WD * w_ref[0, rows, :])
            nm_ref[0, rows, :] = mn
            nv_ref[0, rows, :] = vn
            return carry

        if R % tr == 0:
            lax.fori_loop(0, tr // rs, strip, 0, unroll=min(2, tr // rs))
        else:
            lax.fori_loop(0, jnp.minimum(tr, R - pl.program_id(0) * tr) // rs, strip, 0)

    blk, grid = pl.BlockSpec((1, tr, C), lambda i: (0, i, 0)), (-(-R // tr),)
    o = jax.ShapeDtypeStruct((1, R, C), f32)
    return pl.pallas_call(body, name=name, grid=grid, in_specs=[blk] * 4, out_specs=[blk] * 3, out_shape=[o, o, o],
                          compiler_params=_params(("parallel",)))(w, g, m, v)


def _sum8(t):
    n = t.shape[1]

    def body(t_ref, o_ref):
        acc = t_ref[pl.ds(0, 1), :]
        for r in range(1, 8):
            acc = acc + t_ref[pl.ds(r, 1), :]
        o_ref[...] = acc

    return pl.pallas_call(body, name="sum_devices", out_shape=jax.ShapeDtypeStruct((1, n), f32))(t)


def _reduce_swap_start(g_main, g_dt, g_b):
    hr = g_main.shape[0]
    return _sibling_swap_start(g_main, g_dt, g_b.reshape(N_SHARD, 2, hr, g_b.shape[-1]))


def _reduce_start(swap_state, g_main, g_dt):
    hr = g_main.shape[0]
    send_sems, recv_sems, srcs, lands, _ = swap_state
    srcs, (r_main, r_dt, r_b) = _sibling_swap_wait(send_sems, recv_sems, srcs, lands, g_dt)
    p_main = _pair_sum(g_main[None], r_main[None], "pair_sum_main", SUM_ROWS // 4)
    p_dt = _pair_sum(g_dt[None], r_dt[None], "pair_sum_dt", SUM_ROWS)
    p_b = _pair_sum(srcs[2], r_b, "pair_sum_b", SUM_ROWS)
    p_in = jnp.transpose(_w_in_grad_full(p_main[0], p_dt[0]).reshape(hr, N_SHARD, W_IN_SHARD), (1, 0, 2))
    return _chip_exchange_start([p_in, p_b])


def _reduce_finish(state, after):
    send_sems, recv_sems, srcs, lands, _ = state
    (p_in, p_b), (got_in, got_b) = _chip_exchange_wait(send_sems, recv_sems, srcs, lands, after)
    mine_in, mine_b = _own_sum(p_in, got_in, "own_sum_in", transposed=True), _own_sum(p_b, got_b, "own_sum_b")
    full_in_t, full_b = _share_halves([mine_in, mine_b], [True, False], "share_halves")
    return full_in_t, full_b.reshape(-1, full_b.shape[-1])


def kernel(x, positions, norm1_w, w_in, conv_w, conv_b, dt_bias, a_log, d_skip, ssd_norm_w, w_br_ret, w_br_ssd, w_out, norm_f_w, loss_target, m_norm1_w, m_w_in, m_conv_w, m_conv_b, m_dt_bias, m_a_log, m_d_skip, m_ssd_norm_w, m_w_br_ret, m_w_br_ssd, m_w_out, m_norm_f_w, v_norm1_w, v_w_in, v_conv_w, v_conv_b, v_dt_bias, v_a_log, v_d_skip, v_ssd_norm_w, v_w_br_ret, v_w_br_ssd, v_w_out, v_norm_f_w):
    D = D_MODEL
    xi, yi, ci = _me()
    k = 2 * xi + yi
    me = 2 * k + ci
    weights = dict(norm1_w=norm1_w, w_in=w_in, conv_w=conv_w, conv_b=conv_b, dt_bias=dt_bias, a_log=a_log, d_skip=d_skip,
                   ssd_norm_w=ssd_norm_w, w_br_ret=w_br_ret, w_br_ssd=w_br_ssd, w_out=w_out, norm_f_w=norm_f_w)
    mom1 = dict(norm1_w=m_norm1_w, w_in=m_w_in, conv_w=m_conv_w, conv_b=m_conv_b, dt_bias=m_dt_bias, a_log=m_a_log, d_skip=m_d_skip,
                ssd_norm_w=m_ssd_norm_w, w_br_ret=m_w_br_ret, w_br_ssd=m_w_br_ssd, w_out=m_w_out, norm_f_w=m_norm_f_w)
    mom2 = dict(norm1_w=v_norm1_w, w_in=v_w_in, conv_w=v_conv_w, conv_b=v_conv_b, dt_bias=v_dt_bias, a_log=v_a_log, d_skip=v_d_skip,
                ssd_norm_w=v_ssd_norm_w, w_br_ret=v_w_br_ret, w_br_ssd=v_w_br_ssd, w_out=v_w_out, norm_f_w=v_norm_f_w)

    a_sh = w_in[0].astype(bf16)
    b_sh = jnp.concatenate([w_br_ret[0], w_br_ssd[0], w_out[0]], axis=0).astype(bf16)
    ga, gc = _gather_weights(a_sh, conv_w[0])
    ga, b_late = lax.optimization_barrier((ga, b_sh))
    late_send, late_recv, late_src, late_land, late_token = _gather_late_start(b_late)
    own = lambda g, s: lax.dynamic_update_slice_in_dim(g, s[None], k, axis=0)
    ga, gc = own(ga, a_sh), own(gc, conv_w[0])
    w_main, w_dt = _w_main_from_shards(ga)
    conv_full = jnp.transpose(gc, (1, 0, 2)).reshape(SSD_CONV, CONV_DIM)

    def late_weights(after):
        gb = own(_gather_late_wait(late_send, late_recv, late_src, late_land, after), b_sh)
        return gb[:, 0:512].reshape(2048, D), gb[:, 512:1536].reshape(4096, D), gb[:, 1536:2048].reshape(2048, D)

    class Reducer:
        @staticmethod
        def halves():
            return (1 - ci).reshape(1).astype(jnp.int32), ci.reshape(1).astype(jnp.int32)

        @staticmethod
        def first(g_main, g_dt, g_w_br, g_w_bs, g_w_o):
            g_b = jnp.concatenate([g_w_br.reshape(N_SHARD, 512, D), g_w_bs.reshape(N_SHARD, 1024, D),
                                   g_w_o.reshape(N_SHARD, 512, D)], axis=1)
            return _reduce_swap_start(g_main, g_dt, g_b)

        second = staticmethod(_reduce_start)

    loss, grad_x, g, reduce_state = _local_step(x[0], positions[0], loss_target[0], norm1_w, w_main, w_dt, conv_full, conv_b, dt_bias,
                                                a_log, d_skip, ssd_norm_w, late_token, late_weights, norm_f_w, Reducer)

    grad_w_in_t, full_b = _reduce_finish(reduce_state, g["norm1_w"])
    grad_mats = dict(w_br_ret=full_b[0:512], w_br_ssd=full_b[512:1536], w_out=full_b[1536:2048])

    small = [(n, weights[n].size) for n in ("norm1_w", "conv_b", "dt_bias", "a_log", "d_skip", "ssd_norm_w", "norm_f_w")]
    parts = [jnp.pad(loss.reshape(1, 1), ((0, 0), (0, 127)))] + [g[n].reshape(1, -1) for n, _ in small] + [g["conv_w"].reshape(1, -1)]
    vec = jnp.concatenate(parts, axis=1)
    nv = vec.shape[1]
    nvp = -(-nv // 128) * 128
    vec = jnp.pad(vec, ((0, 0), (0, nvp - nv)))
    total = _sum8(lax.dynamic_update_slice_in_dim(_gather_vec(vec), vec, me, axis=0))
    loss_out = total[0, 0]
    off = 128
    grad_small = {}
    for n, sz in small:
        grad_small[n] = total[:, off:off + sz]
        off += sz
    g_conv = total[:, off:off + SSD_CONV * CONV_DIM].reshape(SSD_CONV, CONV_DIM)
    g_conv = lax.dynamic_slice_in_dim(g_conv, k * (CONV_DIM // N_SHARD), CONV_DIM // N_SHARD, axis=1)
    grad_small["conv_w"] = g_conv.reshape(1, -1)

    upd = {}
    for n in ("w_br_ret", "w_br_ssd", "w_out"):
        upd[n] = _adamw(weights[n], grad_mats[n][None], mom1[n], mom2[n], "adamw_" + n, tr=SUM_ROWS)
    tp = lambda t: jnp.swapaxes(t, 1, 2)
    upd["w_in"] = tuple(tp(t) for t in _adamw(tp(w_in), grad_w_in_t[None], tp(m_w_in), tp(v_w_in), "adamw_w_in", tr=256))
    grad_mats["w_in"] = tp(grad_w_in_t[None])
    names_small = [n for n, _ in small] + ["conv_w"]
    flat = lambda d: jnp.concatenate([d[n].reshape(1, -1) for n in names_small], axis=1)
    ns = sum(weights[n].size for n in names_small)
    nsp = -(-ns // 128) * 128
    padv = lambda t: jnp.pad(t, ((0, 0), (0, nsp - ns)))
    small_upd = _adamw(padv(flat(weights))[None], padv(flat(grad_small))[None], padv(flat(mom1))[None],
                       jnp.pad(flat(mom2), ((0, 0), (0, nsp - ns)), constant_values=1.0)[None], "adamw_small", 1)
    off = 0
    for n in names_small:
        sz = weights[n].size
        upd[n] = tuple(t[0, :, off:off + sz] for t in small_upd)
        off += sz

    order = ["norm1_w", "w_in", "conv_w", "conv_b", "dt_bias", "a_log", "d_skip", "ssd_norm_w", "w_br_ret", "w_br_ssd", "w_out", "norm_f_w"]
    grads_out = {**grad_mats, **grad_small}
    shp = lambda n, t: t.reshape(weights[n].shape)
    return (loss_out, grad_x[None], *[shp(n, grads_out[n]) for n in order], *[shp(n, upd[n][0]) for n in order],
            *[shp(n, upd[n][1]) for n in order], *[shp(n, upd[n][2]) for n in order])
```

```python
import jax
import jax.numpy as jnp
import numpy as np
from jax import lax
from jax.experimental import pallas as pl
from jax.experimental.pallas import tpu as pltpu

f32 = jnp.float32
bf16 = jnp.bfloat16
HIGHEST = lax.Precision.HIGHEST
MESH = pl.DeviceIdType.MESH

D_MODEL = 2048
EPS = 1e-6
CHUNK = 64
RET_HEADS = 8
RET_DK = 256
RET_HW = 4 * RET_DK
RET_HP = 2
RET_UNROLL_FWD, RET_UNROLL_BWD = 4, 8
ROPE_THETA = 10000.0
SSD_WIDTH = 4096
SSD_GROUPS = 8
SSD_STATE = 128
SSD_GW = 512
SSD_GC = SSD_GW + 2 * SSD_STATE
SSD_HPG = 8
SSD_CONV = 4
CONV_DIM = 6144
SSD_HEADS = 64
LS = 128

C_RET, C_Z, C_GATES, C_XBC = 0, 8192, 12288, 16384
N_MAIN = 22528
DT_OFF = 18432
IN_PROJ = 22592
N_SHARD = 4
W_IN_SHARD = IN_PROJ // N_SHARD

ADAM_LR, ADAM_B1, ADAM_B2, ADAM_EPS, ADAM_WD, ADAM_STEP = 0.001, 0.9, 0.999, 1e-08, 0.01, 10

VMEM_LIMIT = 56 * 1024 * 1024
SUM_ROWS = 128
ANY = pl.BlockSpec(memory_space=pl.ANY)


def _params(dims):
    return pltpu.CompilerParams(dimension_semantics=dims, vmem_limit_bytes=VMEM_LIMIT)


def _silu(x):
    return x * jax.nn.sigmoid(x)


def _dsilu(x):
    s = jax.nn.sigmoid(x)
    return s * (1.0 + x * (1.0 - s))


def _nt(a, b):
    return lax.dot_general(a, b, (((1,), (1,)), ((), ())), preferred_element_type=f32)


def _tn(a, b):
    return lax.dot_general(a, b, (((0,), (0,)), ((), ())), preferred_element_type=f32)


def _nn(a, b):
    return jnp.dot(a, b, preferred_element_type=f32)


def _hi(a, b):
    return jnp.dot(a, b, precision=HIGHEST, preferred_element_type=f32)


def _split(a):
    hi = a.astype(bf16)
    return hi, (a - hi.astype(f32)).astype(bf16)


def _sel_r(a, sel):
    hi, lo = _split(a)
    return _nn(hi, sel) + _nn(lo, sel)


def _sel_l(sel, a):
    hi, lo = _split(a)
    return _nn(sel, hi) + _nn(sel, lo)


def _rows_to_cols(t, eye):
    hi = t.astype(bf16)
    r1 = t - hi.astype(f32)
    mid = r1.astype(bf16)
    lo = (r1 - mid.astype(f32)).astype(bf16)
    return _nt(eye, hi) + _nt(eye, mid) + _nt(eye, lo)


def _xbc_group_major(t):
    R = t.shape[0]
    nb = SSD_GROUPS * SSD_STATE
    parts = [t[:, :SSD_WIDTH].reshape(R, SSD_GROUPS, SSD_GW), t[:, SSD_WIDTH:SSD_WIDTH + nb].reshape(R, SSD_GROUPS, SSD_STATE),
             t[:, SSD_WIDTH + nb:].reshape(R, SSD_GROUPS, SSD_STATE)]
    return jnp.concatenate(parts, axis=2).reshape(R, CONV_DIM)


def _xbc_original(t):
    R = t.shape[0]
    g = t.reshape(R, SSD_GROUPS, SSD_GC)
    parts = [g[:, :, :SSD_GW].reshape(R, SSD_WIDTH), g[:, :, SSD_GW:SSD_GW + SSD_STATE].reshape(R, SSD_GROUPS * SSD_STATE),
             g[:, :, SSD_GW + SSD_STATE:].reshape(R, SSD_GROUPS * SSD_STATE)]
    return jnp.concatenate(parts, axis=1)


def _main_segments():
    segs = []
    for h in range(RET_HEADS):
        segs += [(base + RET_DK * h, RET_DK) for base in (0, 2048, 4096, 6144)]
    segs += [(8192, SSD_WIDTH), (DT_OFF + SSD_HEADS, 2 * D_MODEL)]
    nb = SSD_GROUPS * SSD_STATE
    for g in range(SSD_GROUPS):
        segs += [(12288 + SSD_GW * g, SSD_GW), (12288 + SSD_WIDTH + SSD_STATE * g, SSD_STATE),
                 (12288 + SSD_WIDTH + nb + SSD_STATE * g, SSD_STATE)]
    return segs


def _w_main_from_shards(shards):
    def cols(lo, hi):
        out = []
        while lo < hi:
            s = lo // W_IN_SHARD
            top = min(hi, (s + 1) * W_IN_SHARD)
            out.append(shards[s][:, lo - s * W_IN_SHARD:top - s * W_IN_SHARD])
            lo = top
        return out

    main = jnp.concatenate([p for s, n in _main_segments() for p in cols(s, s + n)], axis=1)
    w_dt = jnp.pad(jnp.concatenate(cols(DT_OFF, DT_OFF + SSD_HEADS), axis=1), ((0, 0), (0, 128 - SSD_HEADS)))
    return main, w_dt


def _w_in_grad_full(g_main, g_dt):
    D = g_main.shape[0]
    ret = jnp.transpose(g_main[:, :C_Z].reshape(D, RET_HEADS, 4, RET_DK), (0, 2, 1, 3)).reshape(D, C_Z)
    return jnp.concatenate([ret, g_main[:, C_Z:C_GATES], _xbc_original(g_main[:, C_XBC:]), g_dt[:, :SSD_HEADS],
                            g_main[:, C_GATES:C_XBC]], axis=1)


def _mm(pairs, M, N, *, tm, tn, out_dtype, name, tb=False, row_off=None):
    P = len(pairs)
    nks = [K // tk for (_, _, _, _, _, K, tk) in pairs]
    starts = [int(s) for s in np.cumsum([0] + nks[:-1])]
    KT = int(sum(nks))
    npf = 0 if row_off is None else 1
    in_specs, args = [], []
    for (a, a_cb, b, b_kb, b_nb, K, tk), s, nk in zip(pairs, starts, nks):
        def kk(k, s=s, nk=nk):
            return jnp.clip(k - s, 0, nk - 1)
        in_specs.append(pl.BlockSpec((tm, tk), lambda m, n, k, *pf, kk=kk, a_cb=a_cb: (m + (pf[0][0] if pf else 0), a_cb + kk(k))))
        if tb:
            in_specs.append(pl.BlockSpec((tn, tk), lambda m, n, k, *pf, kk=kk, b_kb=b_kb, b_nb=b_nb: (b_nb + n, b_kb + kk(k))))
        else:
            in_specs.append(pl.BlockSpec((tk, tn), lambda m, n, k, *pf, kk=kk, b_kb=b_kb, b_nb=b_nb: (b_kb + kk(k), b_nb + n)))
        args += [a, b]

    def body(*refs):
        refs = refs[npf:]
        o_ref = refs[2 * P]
        k = pl.program_id(2)

        def prod(i):
            a = refs[2 * i][...].astype(bf16)
            b = refs[2 * i + 1][...].astype(bf16)
            return _nt(a, b) if tb else _nn(a, b)

        if KT == 1:
            o_ref[...] = prod(0).astype(out_dtype)
            return
        acc = refs[2 * P + 1]

        @pl.when(k == 0)
        def _():
            acc[...] = jnp.zeros_like(acc)

        for i in range(P):
            @pl.when((k >= starts[i]) & (k < starts[i] + nks[i]))
            def _(i=i):
                acc[...] += prod(i)

        @pl.when(k == KT - 1)
        def _():
            o_ref[...] = acc[...].astype(out_dtype)

    grid_spec = pltpu.PrefetchScalarGridSpec(
        num_scalar_prefetch=npf, grid=(M // tm, N // tn, KT), in_specs=in_specs,
        out_specs=pl.BlockSpec((tm, tn), lambda m, n, k, *pf: (m, n)),
        scratch_shapes=[] if KT == 1 else [pltpu.VMEM((tm, tn), f32)])
    return pl.pallas_call(
        body, name=name, grid_spec=grid_spec, out_shape=jax.ShapeDtypeStruct((M, N), out_dtype),
        compiler_params=_params(("parallel", "parallel", "arbitrary")),
    )(*([] if row_off is None else [row_off]), *args)


def _mm1(a, b, *, tm, tn, tk, out_dtype, name, tb=False):
    M, K = a.shape
    N = b.shape[0] if tb else b.shape[1]
    return _mm([(a, 0, b, 0, 0, K, tk)], M, N, tm=tm, tn=tn, out_dtype=out_dtype, name=name, tb=tb)


RS = 16
CS = 32


def _for_strips(n_rows, fn, rs=RS, unroll=4):
    def step(s, carry):
        fn(pl.ds(pl.multiple_of(s * rs, rs), rs))
        return carry
    n = n_rows // rs
    lax.fori_loop(0, n, step, 0, unroll=min(unroll, n))


def _norm1_fwd(x, w, tr):
    S, D = x.shape

    def body(x_ref, w_ref, h_ref, ht_ref):
        def strip(rows):
            xv = x_ref[rows, :]
            r = lax.rsqrt(jnp.mean(xv * xv, axis=-1, keepdims=True) + EPS)
            h_ref[rows, :] = (xv * r * w_ref[...]).astype(bf16)
        _for_strips(tr, strip)
        ht_ref[...] = h_ref[...].T

    return pl.pallas_call(
        body, name="norm1_fwd", grid=(S // tr,),
        in_specs=[pl.BlockSpec((tr, D), lambda i: (i, 0)), pl.BlockSpec((1, D), lambda i: (0, 0))],
        out_specs=[pl.BlockSpec((tr, D), lambda i: (i, 0)), pl.BlockSpec((D, tr), lambda i: (0, i))],
        out_shape=[jax.ShapeDtypeStruct((S, D), bf16), jax.ShapeDtypeStruct((D, S), bf16)], compiler_params=_params(("parallel",)),
    )(x, w)


def _norm1_bwd(x, w, dh, dx2, tr):
    S, D = x.shape

    def body(x_ref, w_ref, dh_ref, dx2_ref, gx_ref, gw_ref, acc):
        @pl.when(pl.program_id(0) == 0)
        def _():
            acc[...] = jnp.zeros_like(acc)

        def strip(rows):
            xv = x_ref[rows, :]
            r = lax.rsqrt(jnp.mean(xv * xv, axis=-1, keepdims=True) + EPS)
            xh = xv * r
            dhv = dh_ref[rows, :]
            acc[...] += dhv * xh
            dxh = dhv * w_ref[...]
            gx_ref[rows, :] = dx2_ref[rows, :] + r * (dxh - xh * jnp.mean(dxh * xh, axis=-1, keepdims=True))
        _for_strips(tr, strip)

        @pl.when(pl.program_id(0) == S // tr - 1)
        def _():
            gw_ref[...] = jnp.sum(acc[...], axis=0, keepdims=True)

    row = pl.BlockSpec((tr, D), lambda i: (i, 0))
    vec = pl.BlockSpec((1, D), lambda i: (0, 0))
    return pl.pallas_call(
        body, name="norm1_bwd", grid=(S // tr,), in_specs=[row, vec, row, row], out_specs=[row, vec],
        out_shape=[jax.ShapeDtypeStruct((S, D), f32), jax.ShapeDtypeStruct((1, D), f32)],
        scratch_shapes=[pltpu.VMEM((RS, D), f32)], compiler_params=_params(("arbitrary",)),
    )(x, w, dh, dx2)


def _final_fwd_bwd(x, mo, target, wf, tr):
    S, D = x.shape

    def body(x_ref, mo_ref, t_ref, w_ref, dx2_ref, dx2b_ref, loss_ref, gw_ref, acc, lacc):
        @pl.when(pl.program_id(0) == 0)
        def _():
            acc[...] = jnp.zeros_like(acc)
            lacc[...] = jnp.zeros_like(lacc)

        def strip(rows):
            x2 = x_ref[rows, :] + mo_ref[rows, :]
            r = lax.rsqrt(jnp.mean(x2 * x2, axis=-1, keepdims=True) + EPS)
            xh = x2 * r
            wv = w_ref[...]
            err = xh * wv - t_ref[rows, :]
            lacc[...] += jnp.mean(err * err, axis=-1, keepdims=True)
            dy = err * (1.0 / D)
            acc[...] += dy * xh
            dxh = dy * wv
            dx2 = r * (dxh - xh * jnp.mean(dxh * xh, axis=-1, keepdims=True))
            dx2_ref[rows, :] = dx2
            dx2b_ref[rows, :] = dx2.astype(bf16)
        _for_strips(tr, strip)

        @pl.when(pl.program_id(0) == S // tr - 1)
        def _():
            gw_ref[...] = jnp.sum(acc[...], axis=0, keepdims=True)
            loss_ref[...] = 0.5 * jnp.sum(lacc[...], axis=0, keepdims=True)

    row = pl.BlockSpec((tr, D), lambda i: (i, 0))
    vec = pl.BlockSpec((1, D), lambda i: (0, 0))
    return pl.pallas_call(
        body, name="final_norm_loss", grid=(S // tr,), in_specs=[row, row, row, vec],
        out_specs=[row, row, pl.BlockSpec((1, 1), lambda i: (0, 0)), vec],
        out_shape=[jax.ShapeDtypeStruct((S, D), f32), jax.ShapeDtypeStruct((S, D), bf16), jax.ShapeDtypeStruct((1, 1), f32),
                   jax.ShapeDtypeStruct((1, D), f32)],
        scratch_shapes=[pltpu.VMEM((RS, D), f32), pltpu.VMEM((RS, 1), f32)], compiler_params=_params(("arbitrary",)),
    )(x, mo, target, wf)


def _merge_fwd(p_r, p_s, proj, tr):
    S, D = p_r.shape

    def body(pr_ref, ps_ref, g_ref, o_ref, ot_ref):
        def strip(rows):
            gr, gs = g_ref[rows, pl.ds(0, D)], g_ref[rows, pl.ds(D, D)]
            o_ref[rows, :] = (jax.nn.sigmoid(gr) * pr_ref[rows, :] + jax.nn.sigmoid(gs) * ps_ref[rows, :]).astype(bf16)
        _for_strips(tr, strip)
        ot_ref[...] = o_ref[...].T

    row = pl.BlockSpec((tr, D), lambda i: (i, 0))
    return pl.pallas_call(
        body, name="merge_fwd", grid=(S // tr,),
        in_specs=[row, row, pl.BlockSpec((tr, 2 * D), lambda i: (i, C_GATES // (2 * D)))],
        out_specs=[row, pl.BlockSpec((D, tr), lambda i: (0, i))],
        out_shape=[jax.ShapeDtypeStruct((S, D), bf16), jax.ShapeDtypeStruct((D, S), bf16)], compiler_params=_params(("parallel",)),
    )(p_r, p_s, proj)


def _merge_bwd(dm, p_r, p_s, proj, tr):
    S, D = p_r.shape

    def body(dm_ref, pr_ref, ps_ref, g_ref, dpr_ref, dps_ref, dproj_ref):
        def strip(rows):
            dmv = dm_ref[rows, :]
            sr = jax.nn.sigmoid(g_ref[rows, pl.ds(0, D)])
            ss = jax.nn.sigmoid(g_ref[rows, pl.ds(D, D)])
            dpr_ref[rows, :] = (dmv * sr).astype(bf16)
            dps_ref[rows, :] = (dmv * ss).astype(bf16)
            dproj_ref[rows, pl.ds(0, D)] = (dmv * pr_ref[rows, :] * sr * (1.0 - sr)).astype(bf16)
            dproj_ref[rows, pl.ds(D, D)] = (dmv * ps_ref[rows, :] * ss * (1.0 - ss)).astype(bf16)
        _for_strips(tr, strip)

    row = pl.BlockSpec((tr, D), lambda i: (i, 0))
    gates = pl.BlockSpec((tr, 2 * D), lambda i: (i, C_GATES // (2 * D)))
    o = jax.ShapeDtypeStruct((S, D), bf16)
    return pl.pallas_call(
        body, name="merge_bwd", grid=(S // tr,), in_specs=[row, row, row, gates],
        out_specs=[row, row, gates], out_shape=[o, o, jax.ShapeDtypeStruct((S, N_MAIN), bf16)],
        compiler_params=_params(("parallel",)),
    )(dm, p_r, p_s, proj)


def _ssd_norm_fwd(y, proj, w, tr):
    S, W = y.shape

    def body(y_ref, z_ref, w_ref, o_ref, ot_ref):
        def strip(rows):
            u = y_ref[rows, :] * _silu(z_ref[rows, :])
            r = lax.rsqrt(jnp.mean(u * u, axis=-1, keepdims=True) + EPS)
            o_ref[rows, :] = (u * r * w_ref[...]).astype(bf16)
        _for_strips(tr, strip)
        ot_ref[...] = o_ref[...].T

    row = pl.BlockSpec((tr, W), lambda i: (i, 0))
    return pl.pallas_call(
        body, name="ssd_norm_fwd", grid=(S // tr,),
        in_specs=[row, pl.BlockSpec((tr, W), lambda i: (i, C_Z // W)), pl.BlockSpec((1, W), lambda i: (0, 0))],
        out_specs=[row, pl.BlockSpec((W, tr), lambda i: (0, i))],
        out_shape=[jax.ShapeDtypeStruct((S, W), bf16), jax.ShapeDtypeStruct((W, S), bf16)], compiler_params=_params(("parallel",)),
    )(y, proj, w)


def _ssd_norm_bwd(y, proj, w, dys, dproj, tr):
    S, W = y.shape

    def body(y_ref, z_ref, w_ref, d_ref, _, dy_ref, dz_ref, gw_ref, acc):
        @pl.when(pl.program_id(0) == 0)
        def _():
            acc[...] = jnp.zeros_like(acc)

        def strip(rows):
            yv, zv, dv = y_ref[rows, :], z_ref[rows, :], d_ref[rows, :]
            sz = _silu(zv)
            u = yv * sz
            r = lax.rsqrt(jnp.mean(u * u, axis=-1, keepdims=True) + EPS)
            un = u * r
            acc[...] += dv * un
            dun = dv * w_ref[...]
            du = r * (dun - un * jnp.mean(dun * un, axis=-1, keepdims=True))
            dy_ref[rows, :] = (du * sz).astype(bf16)
            dz_ref[rows, :] = (du * yv * _dsilu(zv)).astype(bf16)
        _for_strips(tr, strip)

        @pl.when(pl.program_id(0) == S // tr - 1)
        def _():
            gw_ref[...] = jnp.sum(acc[...], axis=0, keepdims=True)

    row = pl.BlockSpec((tr, W), lambda i: (i, 0))
    zcol = pl.BlockSpec((tr, W), lambda i: (i, C_Z // W))
    vec = pl.BlockSpec((1, W), lambda i: (0, 0))
    return pl.pallas_call(
        body, name="ssd_norm_bwd", grid=(S // tr,),
        in_specs=[row, zcol, vec, row, ANY], out_specs=[row, zcol, vec],
        out_shape=[jax.ShapeDtypeStruct((S, W), bf16), jax.ShapeDtypeStruct(dproj.shape, bf16), jax.ShapeDtypeStruct((1, W), f32)],
        input_output_aliases={4: 1}, scratch_shapes=[pltpu.VMEM((RS, W), f32)], compiler_params=_params(("arbitrary",)),
    )(y, proj, w, dys, dproj)


def _rope(t, cos, sin):
    t1, t2 = t[:, :128], t[:, 128:]
    return jnp.concatenate([t1 * cos - t2 * sin, t2 * cos + t1 * sin], axis=1)


def _rope_t(d, cos, sin):
    d1, d2 = d[:, :128], d[:, 128:]
    return jnp.concatenate([d1 * cos + d2 * sin, d2 * cos - d1 * sin], axis=1)


def _ret_specs(tb, rev_nb=None):
    def blk(i):
        return i if rev_nb is None else rev_nb - 1 - i
    head = pl.BlockSpec((tb, RET_HP * RET_HW), lambda h, i: (blk(i), h))
    tab = pl.BlockSpec((tb, 128), lambda h, i: (blk(i), 0))
    mat = pl.BlockSpec((RET_HP, CHUNK, CHUNK), lambda h, i: (h, 0, 0))
    vec = pl.BlockSpec((RET_HP, CHUNK, 1), lambda h, i: (h, 0, 0))
    one = pl.BlockSpec((RET_HP, 1, 1), lambda h, i: (h, 0, 0))
    own = pl.BlockSpec((tb, RET_HP * RET_DK), lambda h, i: (blk(i), h))
    st = pl.BlockSpec((RET_HP, tb // CHUNK, RET_DK, RET_DK), lambda h, i: (h, blk(i), 0, 0))
    return head, tab, mat, vec, one, own, st


def _ret_fwd(proj, cos, sin, intra, qdec, kdec, cdec, tb):
    S = proj.shape[0]
    nc = S // CHUNK
    scale = RET_DK ** -0.5
    dk = RET_DK

    def body(p_ref, cos_ref, sin_ref, m_ref, qd_ref, kd_ref, cd_ref, y_ref, yr_ref, yrt_ref, st_ref, st):
        @pl.when(pl.program_id(1) == 0)
        def _():
            st[...] = jnp.zeros_like(st)

        def head_chunk(hh, c, rows, cs, sn):
            mm, qd, kd, cd = m_ref[hh], qd_ref[hh], kd_ref[hh], cd_ref[hh]
            col = lambda j: pl.ds(hh * RET_HW + j * dk, dk)
            own = pl.ds(hh * dk, dk)
            qr = _rope(p_ref[rows, col(0)], cs, sn)
            kr = _rope(p_ref[rows, col(1)], cs, sn) * scale
            qb, kb, vb = qr.astype(bf16), kr.astype(bf16), p_ref[rows, col(2)].astype(bf16)
            stb = st[hh].astype(bf16)
            st_ref[hh, c] = stb
            sc = (_nt(qb, kb) * mm).astype(bf16)
            y = _nn(sc, vb) + _nn(qb, stb) * qd
            st[hh] = st[hh] * cd + _tn((kr * kd).astype(bf16), vb)
            y_ref[rows, own] = y
            mu = jnp.mean(y, axis=-1, keepdims=True)
            yc = y - mu
            var = jnp.mean(yc * yc, axis=-1, keepdims=True)
            yr_ref[rows, own] = (yc * lax.rsqrt(var + EPS) * _silu(p_ref[rows, col(3)])).astype(bf16)

        def chunk(c, carry):
            rows = pl.ds(pl.multiple_of(c * CHUNK, CHUNK), CHUNK)
            cs, sn = cos_ref[rows, :], sin_ref[rows, :]
            for hh in range(RET_HP):
                head_chunk(hh, c, rows, cs, sn)
            return carry

        lax.fori_loop(0, tb // CHUNK, chunk, 0, unroll=min(RET_UNROLL_FWD, tb // CHUNK))
        yrt_ref[...] = yr_ref[...].T

    head, tab, mat, vec, one, own, stspec = _ret_specs(tb)
    return pl.pallas_call(
        body, name="ret_fwd", grid=(RET_HEADS // RET_HP, S // tb),
        in_specs=[head, tab, tab, mat, vec, vec, one],
        out_specs=[own, own, pl.BlockSpec((RET_HP * RET_DK, tb), lambda h, i: (h, i)), stspec],
        out_shape=[jax.ShapeDtypeStruct((S, 2048), f32), jax.ShapeDtypeStruct((S, 2048), bf16), jax.ShapeDtypeStruct((2048, S), bf16),
                   jax.ShapeDtypeStruct((RET_HEADS, nc, dk, dk), bf16)],
        scratch_shapes=[pltpu.VMEM((RET_HP, dk, dk), f32)], compiler_params=_params(("parallel", "arbitrary")),
    )(proj, cos, sin, intra, qdec, kdec, cdec)


def _ret_bwd(proj, cos, sin, intra, qdec, kdec, cdec, y, dyr, states, dproj, tb):
    S = proj.shape[0]
    nb = S // tb
    nck = tb // CHUNK
    scale = RET_DK ** -0.5
    dk = RET_DK

    def body(p_ref, cos_ref, sin_ref, m_ref, qd_ref, kd_ref, cd_ref, y_ref, dyr_ref, st_ref, _, o_ref, dst):
        @pl.when(pl.program_id(1) == 0)
        def _():
            dst[...] = jnp.zeros_like(dst)

        def head_chunk(hh, c, rows, cs, sn):
            mm, qd, kd, cd = m_ref[hh], qd_ref[hh], kd_ref[hh], cd_ref[hh]
            col = lambda j: pl.ds(hh * RET_HW + j * dk, dk)
            own = pl.ds(hh * dk, dk)
            qr = _rope(p_ref[rows, col(0)], cs, sn)
            kr = _rope(p_ref[rows, col(1)], cs, sn) * scale
            qb, kb, vb = qr.astype(bf16), kr.astype(bf16), p_ref[rows, col(2)].astype(bf16)
            kdb = (kr * kd).astype(bf16)
            stb = st_ref[hh, c]
            yv, gv, dyrv = y_ref[rows, own], p_ref[rows, col(3)], dyr_ref[rows, own]
            mu = jnp.mean(yv, axis=-1, keepdims=True)
            yc = yv - mu
            rstd = lax.rsqrt(jnp.mean(yc * yc, axis=-1, keepdims=True) + EPS)
            yn = yc * rstd
            o_ref[rows, col(3)] = (dyrv * yn * _dsilu(gv)).astype(bf16)
            dyn = dyrv * _silu(gv)
            dy = rstd * (dyn - jnp.mean(dyn, axis=-1, keepdims=True) - yn * jnp.mean(dyn * yn, axis=-1, keepdims=True))
            dyb = dy.astype(bf16)
            dyqb = (dy * qd).astype(bf16)
            dstb = dst[hh].astype(bf16)
            sct =(_nt(kb, qb) * mm).astype(bf16)
            ds = (_nt(dyb, vb) * mm).astype(bf16)
            dsT = (_nt(vb, dyb) * mm).astype(bf16)
            dv = _nn(sct, dyb) + _nn(kdb, dstb)
            dqr = _nn(ds, kb) + _nt(dyqb, stb)
            dkr = _nn(dsT, qb) + _nt(vb, dstb) * kd
            dst[hh] = dst[hh] * cd + _tn(qb, dyqb)
            o_ref[rows, col(0)] = _rope_t(dqr, cs, sn).astype(bf16)
            o_ref[rows, col(1)] = (_rope_t(dkr, cs, sn) * scale).astype(bf16)
            o_ref[rows, col(2)] = dv.astype(bf16)

        def chunk(cc, carry):
            c = nck - 1 - cc
            rows = pl.ds(pl.multiple_of(c * CHUNK, CHUNK), CHUNK)
            cs, sn = cos_ref[rows, :], sin_ref[rows, :]
            for hh in range(RET_HP):
                head_chunk(hh, c, rows, cs, sn)
            return carry

        lax.fori_loop(0, nck, chunk, 0, unroll=min(RET_UNROLL_BWD, nck))

    head, tab, mat, vec, one, own, stspec = _ret_specs(tb, rev_nb=nb)
    return pl.pallas_call(
        body, name="ret_bwd", grid=(RET_HEADS // RET_HP, nb),
        in_specs=[head, tab, tab, mat, vec, vec, one, own, own, stspec, ANY],
        out_specs=head, out_shape=jax.ShapeDtypeStruct(dproj.shape, bf16), input_output_aliases={10: 0},
        scratch_shapes=[pltpu.VMEM((RET_HP, dk, dk), f32)], compiler_params=_params(("parallel", "arbitrary")),
    )(proj, cos, sin, intra, qdec, kdec, cdec, y, dyr, states, dproj)


def _conv_fwd(proj, conv_w, conv_b, tb, cw):
    S = proj.shape[0]
    off = C_XBC // cw

    def body(x_ref, halo_ref, w_ref, b_ref, o_ref, xe):
        xe[pl.ds(0, 8), :] = jnp.where(pl.program_id(1) == 0, 0.0, halo_ref[...])
        xe[pl.ds(8, CS), :] = x_ref[pl.ds(0, CS), :]
        ws = [w_ref[pl.ds(j, 1), :] for j in range(SSD_CONV)]
        for s in range(tb // CS):
            tap = (lambda j: xe[pl.ds(5 + j, CS), :]) if s == 0 else (lambda j, s=s: x_ref[pl.ds(s * CS - 3 + j, CS), :])
            acc = b_ref[...] + ws[0] * tap(0)
            for j in range(1, SSD_CONV):
                acc = acc + ws[j] * tap(j)
            o_ref[pl.ds(s * CS, CS), :] = acc

    return pl.pallas_call(
        body, name="conv_fwd", grid=(CONV_DIM // cw, S // tb),
        in_specs=[pl.BlockSpec((tb, cw), lambda j, i: (i, off + j)),
                  pl.BlockSpec((8, cw), lambda j, i: (jnp.maximum(i * (tb // 8) - 1, 0), off + j)),
                  pl.BlockSpec((SSD_CONV, cw), lambda j, i: (0, j)), pl.BlockSpec((1, cw), lambda j, i: (0, j))],
        out_specs=pl.BlockSpec((tb, cw), lambda j, i: (i, j)),
        out_shape=jax.ShapeDtypeStruct((S, CONV_DIM), f32),
        scratch_shapes=[pltpu.VMEM((CS + 8, cw), f32)], compiler_params=_params(("parallel", "arbitrary")),
    )(proj, proj, conv_w, conv_b)


def _conv_bwd(dpre, proj, conv_w, dproj, tb, cw):
    S, n = dpre.shape
    nb = S // tb
    xoff = C_XBC // cw

    def body(d_ref, dh_ref, x_ref, xh_ref, w_ref, _, dx_ref, gw_ref, gb_ref, de, xe, accw, accb):
        i = pl.program_id(1)

        @pl.when(i == 0)
        def _():
            accw[...] = jnp.zeros_like(accw)
            accb[...] = jnp.zeros_like(accb)

        ns = tb // CS
        de[pl.ds(0, CS), :] = d_ref[pl.ds(tb - CS, CS), :]
        de[pl.ds(CS, 8), :] = jnp.where(i == nb - 1, 0.0, dh_ref[...])
        xe[pl.ds(0, 8), :] = jnp.where(i == 0, 0.0, xh_ref[...])
        xe[pl.ds(8, CS), :] = x_ref[pl.ds(0, CS), :]
        ws = [w_ref[pl.ds(j, 1), :] for j in range(SSD_CONV)]
        fold = lambda p: sum(p[8 * q:8 * (q + 1)] for q in range(1, CS // 8)) + p[0:8]
        for s in range(ns):
            dv = d_ref[pl.ds(s * CS, CS), :]
            ahead = (lambda o: de[pl.ds(o, CS), :]) if s == ns - 1 else (lambda o, s=s: d_ref[pl.ds(s * CS + o, CS), :])
            xtap = (lambda j: xe[pl.ds(5 + j, CS), :]) if s == 0 else (lambda j, s=s: x_ref[pl.ds(s * CS - 3 + j, CS), :])
            acc = ws[SSD_CONV - 1] * dv
            for j in range(SSD_CONV - 1):
                acc = acc + ws[j] * ahead(3 - j)
            dx_ref[pl.ds(s * CS, CS), :] = acc.astype(bf16)
            accb[...] += fold(dv)
            for j in range(SSD_CONV):
                accw[j] += fold(dv * xtap(j))

        @pl.when(i == nb - 1)
        def _():
            gb_ref[...] = jnp.sum(accb[...], axis=0, keepdims=True)
            for j in range(SSD_CONV):
                gw_ref[pl.ds(j, 1), :] = jnp.sum(accw[j], axis=0, keepdims=True)

    return pl.pallas_call(
        body, name="conv_bwd", grid=(n // cw, nb),
        in_specs=[pl.BlockSpec((tb, cw), lambda j, i: (i, j)),
                  pl.BlockSpec((8, cw), lambda j, i: (jnp.minimum((i + 1) * (tb // 8), S // 8 - 1), j)),
                  pl.BlockSpec((tb, cw), lambda j, i: (i, xoff + j)),
                  pl.BlockSpec((8, cw), lambda j, i: (jnp.maximum(i * (tb // 8) - 1, 0), xoff + j)),
                  pl.BlockSpec((SSD_CONV, cw), lambda j, i: (0, j)), ANY],
        out_specs=[pl.BlockSpec((tb, cw), lambda j, i: (i, xoff + j)), pl.BlockSpec((SSD_CONV, cw), lambda j, i: (0, j)),
                   pl.BlockSpec((1, cw), lambda j, i: (0, j))],
        out_shape=[jax.ShapeDtypeStruct(dproj.shape, bf16), jax.ShapeDtypeStruct((SSD_CONV, n), f32), jax.ShapeDtypeStruct((1, n), f32)],
        input_output_aliases={5: 0},
        scratch_shapes=[pltpu.VMEM((CS + 8, cw), f32), pltpu.VMEM((CS + 8, cw), f32), pltpu.VMEM((SSD_CONV, 8, cw), f32),
                        pltpu.VMEM((8, cw), f32)],
        compiler_params=_params(("parallel", "arbitrary")),
    )(dpre, dpre, proj, proj, conv_w, dproj)


def _dt_prep(dt_raw, dt_bias, a_log, tb):
    S = dt_raw.shape[0]

    def body(r_ref, b_ref, al_ref, dt_ref, sg_ref, ac_ref):
        li = lax.broadcasted_iota(jnp.int32, (LS, LS), 0)
        si = lax.broadcasted_iota(jnp.int32, (LS, LS), 1)
        tri = (li >= si).astype(f32)
        neg_a = -jnp.exp(al_ref[...])
        for c in range(tb // LS):
            rows = pl.ds(c * LS, LS)
            xv = r_ref[rows, :] + b_ref[...]
            dtv = jax.nn.softplus(xv)
            dt_ref[rows, :] = dtv
            sg_ref[rows, :] = jax.nn.sigmoid(xv)
            ac_ref[rows, :] = _hi(tri, dtv * neg_a)

    row = pl.BlockSpec((tb, 128), lambda i: (i, 0))
    vec = pl.BlockSpec((1, 128), lambda i: (0, 0))
    o = jax.ShapeDtypeStruct((S, 128), f32)
    return pl.pallas_call(body, name="dt_prep", grid=(S // tb,), in_specs=[row, vec, vec], out_specs=[row, row, row],
                          out_shape=[o, o, o], compiler_params=_params(("parallel",)))(dt_raw, dt_bias, a_log)


def _group_major(t):
    S = t.shape[0]
    return jnp.transpose(t[:, :SSD_HEADS].reshape(S, SSD_GROUPS, SSD_HPG), (1, 0, 2))


def _group_major_t(t):
    S = t.shape[0]
    return jnp.transpose(t[:, :SSD_HEADS].reshape(S // LS, LS, SSD_GROUPS, SSD_HPG), (2, 0, 3, 1))


def _ssd_specs(tb, rev_nb=None):
    def blk(i):
        return i if rev_nb is None else rev_nb - 1 - i
    grp = pl.BlockSpec((tb, SSD_GC), lambda g, i: (blk(i), g))
    xs = pl.BlockSpec((tb, SSD_GW), lambda g, i: (blk(i), g))
    ph = pl.BlockSpec((1, tb, SSD_HPG), lambda g, i: (g, blk(i), 0))
    pht = pl.BlockSpec((1, tb // LS, SSD_HPG, LS), lambda g, i: (g, blk(i), 0, 0))
    gvec = pl.BlockSpec((1, 1, SSD_GW), lambda g, i: (g, 0, 0))
    ex = pl.BlockSpec((SSD_HPG, SSD_GW), lambda g, i: (0, 0))
    st = pl.BlockSpec((1, tb // LS, SSD_STATE, SSD_GW), lambda g, i: (g, blk(i), 0, 0))
    return grp, xs, ph, pht, gvec, ex, st


def _expander():
    return jnp.repeat(jnp.eye(SSD_HPG, dtype=f32), SSD_GW // SSD_HPG, axis=1).astype(bf16)


def _expand3(dt8, ac8, ex):
    stack = jnp.concatenate([dt8, jnp.exp(ac8), jnp.exp(ac8[LS - 1:LS, :] - ac8)], axis=0)
    wide = _sel_r(stack, ex)
    return wide[0:LS], wide[LS:2 * LS], wide[2 * LS:3 * LS]


def _ssd_fwd(pre, dt_g, ac_g, act_g, dskx, tb):
    S = pre.shape[0]
    nc = S // LS
    hd = SSD_GW // SSD_HPG

    def body(p_ref, dt_ref, ac_ref, act_ref, dsk_ref, ex_ref, y_ref, st_ref, st):
        @pl.when(pl.program_id(1) == 0)
        def _():
            st[...] = jnp.zeros_like(st)

        ex = ex_ref[...]
        li = lax.broadcasted_iota(jnp.int32, (LS, LS), 0)
        si = lax.broadcasted_iota(jnp.int32, (LS, LS), 1)
        causal = li >= si

        def chunk(c, carry):
            rows = pl.ds(pl.multiple_of(c * LS, LS), LS)
            xs = _silu(p_ref[rows, pl.ds(0, SSD_GW)])
            bcb = _silu(p_ref[rows, pl.ds(SSD_GW, SSD_STATE)]).astype(bf16)
            ccb = _silu(p_ref[rows, pl.ds(SSD_GW + SSD_STATE, SSD_STATE)]).astype(bf16)
            dt8, ac8, act = dt_ref[0, rows, :], ac_ref[0, rows, :], act_ref[0, c]
            dtx, eax, tailx = _expand3(dt8, ac8, ex)
            xdt = xs * dtx
            cb = _nt(ccb, bcb)
            stb = st[...].astype(bf16)
            st_ref[0, c] = stb
            xdtb = xdt.astype(bf16)
            outs = []
            for h in range(SSD_HPG):
                dec = jnp.exp(jnp.where(causal, ac8[:, h:h + 1] - act[h:h + 1, :], -1e30))
                outs.append(_nn((cb * dec).astype(bf16), xdtb[:, hd * h:hd * (h + 1)]))
            y_ref[rows, :] = (jnp.concatenate(outs, axis=1) + _nn(ccb, stb) * eax + dsk_ref[0] * xs).astype(bf16)
            st[...] = st[...] * eax[LS - 1:LS, :] + _tn(bcb, (xdt * tailx).astype(bf16))
            return carry

        lax.fori_loop(0, tb // LS, chunk, 0, unroll=min(4, tb // LS))

    grp, xs, ph, pht, gvec, ex, stspec = _ssd_specs(tb)
    return pl.pallas_call(
        body, name="ssd_fwd", grid=(SSD_GROUPS, S // tb),
        in_specs=[grp, ph, ph, pht, gvec, ex], out_specs=[xs, stspec],
        out_shape=[jax.ShapeDtypeStruct((S, SSD_WIDTH), bf16), jax.ShapeDtypeStruct((SSD_GROUPS, nc, SSD_STATE, SSD_GW), bf16)],
        scratch_shapes=[pltpu.VMEM((SSD_STATE, SSD_GW), f32)], compiler_params=_params(("parallel", "arbitrary")),
    )(pre, dt_g, ac_g, act_g, dskx, _expander())


def _ssd_bwd(pre, dt_g, ac_g, act_g, sg_g, dskx, nega_g, dy, states, tb):
    S = pre.shape[0]
    nb = S // tb
    nck = tb // LS
    hd = SSD_GW // SSD_HPG

    def body(p_ref, dt_ref, ac_ref, act_ref, sg_ref, dsk_ref, na_ref, ex_ref, ext_ref, dy_ref, st_ref,
             dp_ref, ddt_ref, gsk_ref, gal_ref, gdb_ref, dst, skacc):
        @pl.when(pl.program_id(1) == 0)
        def _():
            dst[...] = jnp.zeros_like(dst)
            skacc[...] = jnp.zeros_like(skacc)
            gal_ref[...] = jnp.zeros_like(gal_ref)
            gdb_ref[...] = jnp.zeros_like(gdb_ref)

        ex, ext = ex_ref[...], ext_ref[...]
        li = lax.broadcasted_iota(jnp.int32, (LS, LS), 0)
        si = lax.broadcasted_iota(jnp.int32, (LS, LS), 1)
        causal = li >= si
        anti = si >= li
        upper = anti.astype(bf16)
        eye = (si == li).astype(bf16)
        last_row = (lax.broadcasted_iota(jnp.int32, (LS, 1), 0) == LS - 1).astype(f32)
        head_id = lax.broadcasted_iota(jnp.int32, (1, SSD_HPG), 1)
        head_col = lax.broadcasted_iota(jnp.int32, (SSD_HPG, 1), 0)
        neg_a = na_ref[0]
        dskv = dsk_ref[0]

        def chunk(cc, carry):
            c = nck - 1 - cc
            rows = pl.ds(pl.multiple_of(c * LS, LS), LS)
            px = p_ref[rows, pl.ds(0, SSD_GW)]
            pb = p_ref[rows, pl.ds(SSD_GW, SSD_STATE)]
            pc = p_ref[rows, pl.ds(SSD_GW + SSD_STATE, SSD_STATE)]
            sgx, sgb, sgc = jax.nn.sigmoid(px), jax.nn.sigmoid(pb), jax.nn.sigmoid(pc)
            xs = px * sgx
            bcb = (pb * sgb).astype(bf16)
            ccb = (pc * sgc).astype(bf16)
            dt8, ac8, act = dt_ref[0, rows, :], ac_ref[0, rows, :], act_ref[0, c]
            dtx, eax, tailx = _expand3(dt8, ac8, ex)
            xdt = xs * dtx
            ex_last = eax[LS - 1:LS, :]
            stb = st_ref[0, c]
            dyv = dy_ref[rows, :]
            dyb = dyv.astype(bf16)
            xdtb = xdt.astype(bf16)
            skacc[...] += jnp.sum(dyv * xs, axis=0, keepdims=True)
            yinter = _nn(ccb, stb) * eax
            dzb = (dyv * eax).astype(bf16)
            dcc = _nt(dzb, stb)
            dstv = dst[...]
            dstb = dstv.astype(bf16)
            xt = xdt * tailx
            dxt = _nn(bcb, dstb)
            dbc = _nt(xt.astype(bf16), dstb)
            dxdt = dxt * tailx
            lastrow = jnp.sum(dxt * xt, axis=0, keepdims=True) + jnp.sum(dstv * stb.astype(f32), axis=0, keepdims=True) * ex_last
            dst[...] = dstv * ex_last + _tn(ccb, dzb)
            cb = _nt(ccb, bcb)
            cbt = _nt(bcb, ccb)
            dcb = jnp.zeros((LS, LS), f32)
            dac8 = jnp.zeros((LS, SSD_HPG), f32)
            dact = jnp.zeros((SSD_HPG, LS), f32)
            dxin = []
            for h in range(SSD_HPG):
                sl = slice(hd * h, hd * (h + 1))
                col, rowv = ac8[:, h:h + 1], act[h:h + 1, :]
                dec = jnp.exp(jnp.where(causal, col - rowv, -1e30))
                dect = jnp.exp(jnp.where(anti, rowv - col, -1e30))
                gm = cb * dec
                dgm = _nt(dyb[:, sl], xdtb[:, sl])
                dxin.append(_nn((cbt * dect).astype(bf16), dyb[:, sl]))
                dcb = dcb + dgm * dec
                w = dgm * gm
                dac8 = dac8 + jnp.sum(w, axis=1, keepdims=True) * (head_id == h).astype(f32)
                dact = dact + (head_col == h).astype(f32) * jnp.sum(w, axis=0, keepdims=True)
            dxintra = jnp.concatenate(dxin, axis=1)
            dcbb = dcb.astype(bf16)
            dcc = dcc + _nn(dcbb, bcb)
            dbc = dbc + _tn(dcbb, ccb)
            dxdt = dxdt + dxintra
            dacx = dyv * yinter - dxt * xt + last_row * lastrow
            red = _sel_r(jnp.concatenate([dacx, dxdt * xs], axis=0), ext)
            dac8 = dac8 - _rows_to_cols(dact, eye) + red[0:LS]
            da8 = _sel_l(upper, dac8)
            ddt8 = red[LS:2 * LS] + da8 * neg_a
            gal_ref[0] += jnp.sum(da8 * dt8 * neg_a, axis=0, keepdims=True)
            ddr = ddt8 * sg_ref[0, rows, :]
            ddt_ref[0, rows, :] = ddr
            gdb_ref[0] += jnp.sum(ddr, axis=0, keepdims=True)
            dsilu = lambda p, s: s * (1.0 + p * (1.0 - s))
            dp_ref[rows, pl.ds(0, SSD_GW)] = (dskv * dyv + dxdt * dtx) * dsilu(px, sgx)
            dp_ref[rows, pl.ds(SSD_GW, SSD_STATE)] = dbc * dsilu(pb, sgb)
            dp_ref[rows, pl.ds(SSD_GW + SSD_STATE, SSD_STATE)] = dcc * dsilu(pc, sgc)
            return carry

        lax.fori_loop(0, nck, chunk, 0, unroll=min(4, nck))

        @pl.when(pl.program_id(1) == nb - 1)
        def _():
            gsk_ref[0] = skacc[...]

    grp, xs, ph, pht, gvec, ex, stspec = _ssd_specs(tb, rev_nb=nb)
    small = pl.BlockSpec((1, 1, SSD_HPG), lambda g, i: (g, 0, 0))
    ext = pl.BlockSpec((SSD_GW, SSD_HPG), lambda g, i: (0, 0))
    sm = jax.ShapeDtypeStruct((SSD_GROUPS, 1, SSD_HPG), f32)
    expander = _expander()
    return pl.pallas_call(
        body, name="ssd_bwd", grid=(SSD_GROUPS, nb),
        in_specs=[grp, ph, ph, pht, ph, gvec, small, ex, ext, xs, stspec],
        out_specs=[grp, ph, gvec, small, small],
        out_shape=[jax.ShapeDtypeStruct((S, CONV_DIM), f32), jax.ShapeDtypeStruct((SSD_GROUPS, S, SSD_HPG), f32),
                   jax.ShapeDtypeStruct((SSD_GROUPS, 1, SSD_GW), f32), sm, sm],
        scratch_shapes=[pltpu.VMEM((SSD_STATE, SSD_GW), f32), pltpu.VMEM((1, SSD_GW), f32)],
        compiler_params=_params(("parallel", "arbitrary")),
    )(pre, dt_g, ac_g, act_g, sg_g, dskx, nega_g, expander, expander.T, dy, states)


def _tiles(S):
    return dict(tb=min(512, S), tr=min(256, S), tm=min(1024, S))


def _local_step(x, positions, target, norm1_w, w_main, w_dt, conv_w, conv_b, dt_bias, a_log, d_skip, ssd_norm_w,
                late_token, late_weights, norm_f_w, reducer):
    S, D = x.shape
    t = _tiles(S)
    tb, tr, tm = t["tb"], t["tr"], t["tm"]

    half = RET_DK // 2
    inv_freq = ROPE_THETA ** (-jnp.arange(half, dtype=f32) / half)
    ang = positions.astype(f32)[:, None] * inv_freq
    cos, sin = jnp.cos(ang), jnp.sin(ang)
    log_gamma = jnp.log1p(-(2.0 ** (-5.0 - jnp.arange(RET_HEADS, dtype=f32))))
    idx = jnp.arange(CHUNK, dtype=f32)
    intra = jnp.exp(jnp.abs(idx[:, None] - idx[None, :]) * log_gamma[:, None, None])
    qdec = jnp.exp((idx + 1.0)[None, :] * log_gamma[:, None])[:, :, None]
    kdec = jnp.exp((CHUNK - 1.0 - idx)[None, :] * log_gamma[:, None])[:, :, None]
    cdec = jnp.exp(CHUNK * log_gamma)[:, None, None]
    conv_wm, conv_bm = _xbc_group_major(conv_w), _xbc_group_major(conv_b)

    h, ht = _norm1_fwd(x, norm1_w + late_token[0, 0], tr)
    proj = _mm1(h, w_main, tm=tm, tn=1024, tk=D, out_dtype=f32, name="proj_main")
    dt_raw = _mm1(h, w_dt, tm=tm, tn=128, tk=D, out_dtype=f32, name="proj_dt")
    y_ret, yr, yrt, ret_states = _ret_fwd(proj, cos, sin, intra, qdec, kdec, cdec, tb)
    pre = _conv_fwd(proj, conv_wm, conv_bm, min(1024, S), 512)
    pad64 = lambda v: jnp.pad(v, ((0, 0), (0, 128 - SSD_HEADS)))
    dt, sg, ac = _dt_prep(dt_raw, pad64(dt_bias), pad64(a_log), tb)
    dt_g, ac_g, sg_g, act_g = _group_major(dt), _group_major(ac), _group_major(sg), _group_major_t(ac)
    dskx = jnp.repeat(d_skip.reshape(SSD_GROUPS, 1, SSD_HPG), SSD_GW // SSD_HPG, axis=2)
    nega_g = (-jnp.exp(a_log)).reshape(SSD_GROUPS, 1, SSD_HPG)
    y_ssd, ssd_states = _ssd_fwd(pre, dt_g, ac_g, act_g, dskx, tb)
    ys, yst = _ssd_norm_fwd(y_ssd, proj, ssd_norm_w, tr // 2)
    w_br, w_bs, w_o = late_weights(ys)
    p_r = _mm1(yr, w_br, tm=tm, tn=1024, tk=2048, out_dtype=bf16, name="branch_ret")
    p_s = _mm1(ys, w_bs, tm=tm, tn=1024, tk=2048, out_dtype=bf16, name="branch_ssd")
    merged, mergedt = _merge_fwd(p_r, p_s, proj, tr)
    mo = _mm1(merged, w_o, tm=tm, tn=1024, tk=2048, out_dtype=bf16, name="out_proj")
    dx2, dx2b, loss, g_norm_f = _final_fwd_bwd(x, mo, target, norm_f_w.reshape(1, D), tr)

    tkt = min(4096, S)
    wg = lambda at, b, name, tn=1024: _mm1(at, b, tm=min(1024, at.shape[0]), tn=tn, tk=tkt, out_dtype=f32, name=name)
    dm = _mm1(dx2b, w_o, tm=tm, tn=1024, tk=2048, out_dtype=bf16, name="d_merged", tb=True)
    g_w_o = wg(mergedt, dx2b, "g_w_out")
    dp_r, dp_s, dproj = _merge_bwd(dm, p_r, p_s, proj, tr)
    dyr = _mm1(dp_r, w_br, tm=tm, tn=1024, tk=2048, out_dtype=bf16, name="d_yr", tb=True)
    dys = _mm1(dp_s, w_bs, tm=tm, tn=1024, tk=2048, out_dtype=bf16, name="d_ys", tb=True)
    g_w_br = wg(yrt, dp_r, "g_w_br_ret")
    g_w_bs = wg(yst, dp_s, "g_w_br_ssd")
    dy_ssd, dproj, g_ssd_norm = _ssd_norm_bwd(y_ssd, proj, ssd_norm_w, dys, dproj, tr // 2)
    dproj = _ret_bwd(proj, cos, sin, intra, qdec, kdec, cdec, y_ret, dyr, ret_states, dproj, tb)
    dpre, ddt_g, gsk, gal, gdb = _ssd_bwd(pre, dt_g, ac_g, act_g, sg_g, dskx, nega_g, dy_ssd, ssd_states, tb)
    dproj, gcw, gcb = _conv_bwd(dpre, proj, conv_wm, dproj, min(1024, S), 512)
    ddt = jnp.transpose(ddt_g, (1, 0, 2)).reshape(S, SSD_HEADS)
    ddt_p = jnp.pad(ddt, ((0, 0), (0, 128 - SSD_HEADS))).astype(bf16)

    hr = D // 2
    wg_half = lambda off, b, name, tn=1024: _mm([(ht, 0, b, 0, 0, S, tkt)], hr, b.shape[1], tm=hr, tn=tn, out_dtype=f32,
                                                 name=name, row_off=off)
    off_sib, off_own = reducer.halves()
    gs_main = wg_half(off_sib, dproj, "g_w_in_main_sib")
    gs_dt = wg_half(off_sib, ddt_p, "g_w_in_dt_sib", tn=128)
    swap_state = reducer.first(gs_main, gs_dt, g_w_br, g_w_bs, g_w_o)
    ddt_p = ddt_p + swap_state[-1][0, 0].astype(bf16)
    go_main = wg_half(off_own, dproj, "g_w_in_main_own")
    go_dt = wg_half(off_own, ddt_p, "g_w_in_dt_own", tn=128)
    reduce_state = reducer.second(swap_state, go_main, go_dt)
    ddt_p = ddt_p + reduce_state[-1][0, 0].astype(bf16)
    dh = _mm([(dproj, 0, w_main, 0, 0, N_MAIN, N_MAIN // 8), (ddt_p, 0, w_dt, 0, 0, 128, 128)], S, D, tm=tm, tn=1024,
             out_dtype=bf16, name="d_h", tb=True)
    grad_x, g_norm1 = _norm1_bwd(x, norm1_w, dh, dx2, tr)

    seg = lambda v: jnp.sum(v.reshape(SSD_HEADS, SSD_GW // SSD_HPG), axis=1).reshape(1, SSD_HEADS)
    grads = dict(
        norm1_w=g_norm1, w_in_main=(gs_main, go_main), w_in_dt=(gs_dt, go_dt),
        conv_w=_xbc_original(gcw), conv_b=_xbc_original(gcb),
        dt_bias=gdb.reshape(1, SSD_HEADS), a_log=gal.reshape(1, SSD_HEADS), d_skip=seg(gsk),
        ssd_norm_w=g_ssd_norm, w_br_ret=g_w_br, w_br_ssd=g_w_bs, w_out=g_w_o, norm_f_w=g_norm_f,
    )
    return loss, grad_x, grads, reduce_state


def _me():
    return lax.axis_index("x"), lax.axis_index("y"), lax.axis_index("c")


def _other_chips(x, y):
    return [(1 - x, y), (x, 1 - y), (1 - x, 1 - y)]


def _gather_weights(a, cw):
    R = a.shape[0]
    hr, hq = R // 2, R // 4

    def body(a_ref, cw_ref, ga_ref, gc_ref, send_sems, recv_sems):
        x, y, c = _me()
        me, sibling = (x, y, c), (x, y, 1 - c)
        nx, ny = (1 - x, y, c), (x, 1 - y, c)
        k, kx, ky, kd = 2 * x + y, 2 * (1 - x) + y, 2 * x + (1 - y), 2 * (1 - x) + (1 - y)

        def rows(half, q):
            return pl.ds(pl.multiple_of(half * hr + q * hq, 8), hq)

        def cp(sem, shard, half, q, to, src=None):
            dst = ga_ref.at[shard, rows(half, q), :]
            return pltpu.make_async_remote_copy(src_ref=dst if src is None else src, dst_ref=dst, send_sem=send_sems.at[sem],
                                                recv_sem=recv_sems.at[sem], device_id=to, device_id_type=MESH)

        def small(j, src_shard, to):
            return pltpu.make_async_remote_copy(
                src_ref=cw_ref, dst_ref=gc_ref.at[src_shard], send_sem=send_sems.at[12 + j], recv_sem=recv_sems.at[12 + j],
                device_id=to, device_id_type=MESH)

        own = lambda q: a_ref.at[rows(c, q), :]
        smalls = [small(j, k, (*chip, c)) for j, chip in enumerate(_other_chips(x, y))]
        sends = [cp(0, k, c, 0, nx, own(0)), cp(2, k, c, 1, ny, own(1)), cp(1, k, c, 1, nx, own(1)), cp(3, k, c, 0, ny, own(0))]
        for s in sends + smalls:
            s.start()
        arrivals = [(0, kx, 0, (4, ny)), (2, ky, 1, (5, nx)), (1, kx, 1, None), (3, ky, 0, None), (4, kd, 0, None), (5, kd, 1, None)]
        for sem, shard, q, onward in arrivals:
            cp(sem, shard, c, q, me).wait_recv()
            if onward is not None:
                sends.append(cp(onward[0], shard, c, q, onward[1]))
                sends[-1].start()
            sends.append(cp(6 + sem, shard, c, q, sibling))
            sends[-1].start()
        for sem, shard, q, _ in arrivals:
            cp(6 + sem, shard, 1 - c, q, me).wait_recv()
        for j, chip in enumerate(_other_chips(x, y)):
            small(j, 2 * chip[0] + chip[1], me).wait_recv()
        for s in sends + smalls:
            s.wait_send()

    return pl.pallas_call(
        body, name="gather_weights", in_specs=[ANY, ANY], out_specs=[ANY, ANY],
        out_shape=[jax.ShapeDtypeStruct((N_SHARD,) + a.shape, a.dtype), jax.ShapeDtypeStruct((N_SHARD,) + cw.shape, cw.dtype)],
        scratch_shapes=[pltpu.SemaphoreType.DMA((15,)), pltpu.SemaphoreType.DMA((15,))],
        compiler_params=pltpu.CompilerParams(has_side_effects=True),
    )(a, cw)


def _gather_late_copies(src, land, send_sems, recv_sems):
    x, y, c = _me()
    k = 2 * x + y
    return [pltpu.make_async_remote_copy(src_ref=src, dst_ref=land.at[k], send_sem=send_sems.at[j], recv_sem=recv_sems.at[j],
                                         device_id=(*chip, c), device_id_type=MESH) for j, chip in enumerate(_other_chips(x, y))]


def _gather_late_start(b):
    land = lax.empty((N_SHARD,) + b.shape, b.dtype)

    def body(b_ref, land_ref, send_sems, recv_sems, b_thru, land_thru, token):
        for cp in _gather_late_copies(b_ref, land_ref, send_sems, recv_sems):
            cp.start()
        token[...] = jnp.zeros_like(token)

    return pl.pallas_call(
        body, name="gather_late_start", in_specs=[HBM, HBM],
        out_specs=(SEM, SEM, HBM, HBM, pl.BlockSpec(memory_space=pltpu.VMEM)),
        out_shape=(pltpu.SemaphoreType.DMA((3,)), pltpu.SemaphoreType.DMA((3,)), pltpu.HBM(b.shape, b.dtype),
                   pltpu.HBM(land.shape, land.dtype), jax.ShapeDtypeStruct((8, 128), f32)),
        input_output_aliases={0: 2, 1: 3}, compiler_params=pltpu.CompilerParams(has_side_effects=DATAFLOW),
    )(pltpu.with_memory_space_constraint(b, pltpu.HBM), pltpu.with_memory_space_constraint(land, pltpu.HBM))


def _gather_late_wait(send_sems, recv_sems, src, land, after):
    def body(b_ref, land_ref, send_sems_ref, recv_sems_ref, after_ref, b_dead, land_out):
        x, y, c = _me()
        for j, chip in enumerate(_other_chips(x, y)):
            kk = 2 * chip[0] + chip[1]
            cp = pltpu.make_async_remote_copy(src_ref=b_ref, dst_ref=land_ref.at[kk], send_sem=send_sems_ref.at[j],
                                              recv_sem=recv_sems_ref.at[j], device_id=(x, y, c), device_id_type=MESH)
            cp.wait_send()
            cp.wait_recv()

    return pl.pallas_call(
        body, name="gather_late_wait", in_specs=[HBM, HBM, SEM, SEM, ANY], out_specs=[HBM, HBM],
        out_shape=[pltpu.HBM(src.shape, src.dtype), pltpu.HBM(land.shape, land.dtype)], input_output_aliases={0: 0, 1: 1},
        compiler_params=pltpu.CompilerParams(has_side_effects=DATAFLOW),
    )(src, land, send_sems, recv_sems, after)[1]


HBM = pl.BlockSpec(memory_space=pltpu.HBM)
SEM = pl.BlockSpec(memory_space=pltpu.SEMAPHORE)
DATAFLOW = pltpu.SideEffectType.DATAFLOW_SIDE_EFFECTING


def _swap_copies(srcs, lands, send_sems, recv_sems):
    x, y, c = _me()

    def cp(src, dst, q):
        return pltpu.make_async_remote_copy(src_ref=src, dst_ref=dst, send_sem=send_sems.at[q], recv_sem=recv_sems.at[q],
                                            device_id=(x, y, 1 - c), device_id_type=MESH)

    return [cp(srcs[0], lands[0], 0), cp(srcs[1], lands[1], 1)] + [cp(srcs[2].at[s, 1 - c], lands[2].at[s], 2 + s) for s in range(N_SHARD)]


def _sibling_swap_start(g_main, g_dt, g_b):
    srcs = [g_main, g_dt, g_b]
    lands = [lax.empty(g_main.shape, g_main.dtype), lax.empty(g_dt.shape, g_dt.dtype),
             lax.empty(g_b.shape[:1] + g_b.shape[2:], g_b.dtype)]

    def body(*refs):
        for cp in _swap_copies(refs[0:3], refs[3:6], refs[6], refs[7]):
            cp.start()
        refs[14][...] = jnp.zeros_like(refs[14])

    hbm = lambda a: pltpu.HBM(a.shape, a.dtype)
    out = pl.pallas_call(
        body, name="sibling_swap_start", in_specs=[HBM] * 6,
        out_specs=(SEM, SEM, *[HBM] * 6, pl.BlockSpec(memory_space=pltpu.VMEM)),
        out_shape=(pltpu.SemaphoreType.DMA((2 + N_SHARD,)), pltpu.SemaphoreType.DMA((2 + N_SHARD,)), *[hbm(a) for a in srcs + lands],
                   jax.ShapeDtypeStruct((8, 128), f32)),
        input_output_aliases={t: 2 + t for t in range(6)}, compiler_params=pltpu.CompilerParams(has_side_effects=DATAFLOW),
    )(*[pltpu.with_memory_space_constraint(a, pltpu.HBM) for a in srcs + lands])
    return out[0], out[1], list(out[2:5]), list(out[5:8]), out[8]


def _sibling_swap_wait(send_sems, recv_sems, srcs, lands, after):
    def body(*refs):
        for cp in _swap_copies(refs[0:3], refs[3:6], refs[6], refs[7]):
            cp.wait_send()
            cp.wait_recv()

    hbm = lambda a: pltpu.HBM(a.shape, a.dtype)
    out = pl.pallas_call(
        body, name="sibling_swap_wait", in_specs=[HBM] * 6 + [SEM, SEM, ANY], out_specs=[HBM] * 6,
        out_shape=[hbm(a) for a in list(srcs) + list(lands)], input_output_aliases={t: t for t in range(6)},
        compiler_params=pltpu.CompilerParams(has_side_effects=DATAFLOW),
    )(*srcs, *lands, send_sems, recv_sems, after)
    return list(out[:3]), list(out[3:])


def _exchange_copies(ins, lands, send_sems, recv_sems):
    n = len(ins)
    x, y, c = _me()
    cps = []
    for j, chip in enumerate(_other_chips(x, y)):
        kk = 2 * chip[0] + chip[1]
        for t in range(n):
            cps.append(pltpu.make_async_remote_copy(
                src_ref=ins[t].at[kk], dst_ref=lands[t].at[j], send_sem=send_sems.at[n * j + t],
                recv_sem=recv_sems.at[n * j + t], device_id=(*chip, c), device_id_type=MESH))
    return cps


def _chip_exchange_start(arrs):
    n = len(arrs)
    lands = [lax.empty((3,) + a.shape[1:], a.dtype) for a in arrs]

    def body(*refs):
        ins, lands_in = refs[:n], refs[n:2 * n]
        send_sems, recv_sems = refs[2 * n], refs[2 * n + 1]
        token = refs[4 * n + 2]
        for cp in _exchange_copies(ins, lands_in, send_sems, recv_sems):
            cp.start()
        token[...] = jnp.zeros_like(token)

    hbm = lambda a: pltpu.HBM(a.shape, a.dtype)
    out = pl.pallas_call(
        body, name="chip_exchange_start", in_specs=[HBM] * (2 * n),
        out_specs=(SEM, SEM, *[HBM] * (2 * n), pl.BlockSpec(memory_space=pltpu.VMEM)),
        out_shape=(pltpu.SemaphoreType.DMA((3 * n,)), pltpu.SemaphoreType.DMA((3 * n,)), *[hbm(a) for a in arrs],
                   *[hbm(a) for a in lands], jax.ShapeDtypeStruct((8, 128), f32)),
        input_output_aliases={t: 2 + t for t in range(2 * n)},
        compiler_params=pltpu.CompilerParams(has_side_effects=DATAFLOW),
    )(*[pltpu.with_memory_space_constraint(a, pltpu.HBM) for a in list(arrs) + lands])
    return out[0], out[1], list(out[2:2 + n]), list(out[2 + n:2 + 2 * n]), out[2 + 2 * n]


def _chip_exchange_wait(send_sems, recv_sems, srcs, lands, after):
    n = len(srcs)

    def body(*refs):
        ins, lands_in = refs[:n], refs[n:2 * n]
        send_sems_ref, recv_sems_ref = refs[2 * n], refs[2 * n + 1]
        for cp in _exchange_copies(ins, lands_in, send_sems_ref, recv_sems_ref):
            cp.wait_send()
            cp.wait_recv()

    hbm = lambda a: pltpu.HBM(a.shape, a.dtype)
    out = pl.pallas_call(
        body, name="chip_exchange_wait", in_specs=[HBM] * (2 * n) + [SEM, SEM, ANY],
        out_specs=[HBM] * (2 * n), out_shape=[hbm(a) for a in list(srcs) + list(lands)],
        input_output_aliases={t: t for t in range(2 * n)},
        compiler_params=pltpu.CompilerParams(has_side_effects=DATAFLOW),
    )(*srcs, *lands, send_sems, recv_sems, after)
    return list(out[:n]), list(out[n:])


def _share_halves(bufs, by_cols, name):
    n = len(bufs)

    def body(*refs):
        ins, outs = refs[:n], refs[n:2 * n]
        send_sems, recv_sems = refs[2 * n], refs[2 * n + 1]
        x, y, c = _me()

        def part(ref, t, half):
            if by_cols[t]:
                w = bufs[t].shape[1] // 2
                return ref.at[:, pl.ds(pl.multiple_of(half * w, 128), w)]
            return ref.at[half]

        sends = [pltpu.make_async_remote_copy(src_ref=part(ins[t], t, c), dst_ref=part(outs[t], t, c), send_sem=send_sems.at[t],
                                              recv_sem=recv_sems.at[t], device_id=(x, y, 1 - c), device_id_type=MESH) for t in range(n)]
        for cp in sends:
            cp.start()
        for t in range(n):
            pltpu.make_async_remote_copy(src_ref=part(ins[t], t, c), dst_ref=part(outs[t], t, 1 - c), send_sem=send_sems.at[t],
                                         recv_sem=recv_sems.at[t], device_id=(x, y, c), device_id_type=MESH).wait_recv()
        for cp in sends:
            cp.wait_send()

    return pl.pallas_call(
        body, name=name, in_specs=[ANY] * n, out_specs=[ANY] * n,
        out_shape=[jax.ShapeDtypeStruct(a.shape, a.dtype) for a in bufs], input_output_aliases={t: t for t in range(n)},
        scratch_shapes=[pltpu.SemaphoreType.DMA((n,)), pltpu.SemaphoreType.DMA((n,))],
        compiler_params=pltpu.CompilerParams(has_side_effects=True),
    )(*bufs)


def _gather_vec(v):
    n = v.shape[1]

    def body(v_ref, o_ref, send_sems, recv_sems):
        x, y, c = _me()
        me = 4 * x + 2 * y + c
        cps = []
        for j in range(1, 8):
            fx, fy, fc = (j >> 2) & 1, (j >> 1) & 1, j & 1
            peer = (x ^ fx, y ^ fy, c ^ fc)
            cps.append(pltpu.make_async_remote_copy(
                src_ref=v_ref, dst_ref=o_ref.at[pl.ds(me, 1), :], send_sem=send_sems.at[j - 1], recv_sem=recv_sems.at[j - 1],
                device_id=peer, device_id_type=MESH))
        for cp in cps:
            cp.start()
        for j in range(1, 8):
            fx, fy, fc = (j >> 2) & 1, (j >> 1) & 1, j & 1
            src = 4 * (x ^ fx) + 2 * (y ^ fy) + (c ^ fc)
            pltpu.make_async_remote_copy(
                src_ref=v_ref, dst_ref=o_ref.at[pl.ds(src, 1), :], send_sem=send_sems.at[j - 1], recv_sem=recv_sems.at[j - 1],
                device_id=(x, y, c), device_id_type=MESH).wait_recv()
        for cp in cps:
            cp.wait_send()

    return pl.pallas_call(
        body, name="gather_vec", in_specs=[ANY], out_specs=ANY, out_shape=jax.ShapeDtypeStruct((8, n), v.dtype),
        scratch_shapes=[pltpu.SemaphoreType.DMA((7,)), pltpu.SemaphoreType.DMA((7,))],
        compiler_params=pltpu.CompilerParams(has_side_effects=True),
    )(v)


def _pair_sum(g, r, name, tr):
    L, hr, C = r.shape
    both_halves = g.ndim == 4

    def body(c_ref, g_ref, r_ref, o_ref):
        def strip(rows):
            gv = g_ref[0, 0, rows, :] if both_halves else g_ref[0, rows, :]
            o_ref[0, rows, :] = (gv + r_ref[0, rows, :]).astype(bf16)
        _for_strips(tr, strip)

    g_spec = (pl.BlockSpec((1, 1, tr, C), lambda s, i, c_ref: (s, c_ref[0], i, 0)) if both_halves
              else pl.BlockSpec((1, tr, C), lambda s, i, c_ref: (s, i, 0)))
    grid_spec = pltpu.PrefetchScalarGridSpec(
        num_scalar_prefetch=1, grid=(L, hr // tr),
        in_specs=[g_spec, pl.BlockSpec((1, tr, C), lambda s, i, c_ref: (s, i, 0))],
        out_specs=pl.BlockSpec((1, tr, C), lambda s, i, c_ref: (s, i, 0)))
    c = lax.axis_index("c").reshape(1).astype(jnp.int32)
    return pl.pallas_call(body, name=name, grid_spec=grid_spec, out_shape=jax.ShapeDtypeStruct((L, hr, C), bf16),
                          compiler_params=_params(("parallel", "parallel")))(c, g, r)


def _own_sum(p, got, name, transposed=False):
    _, hr, C = p.shape
    tr = SUM_ROWS
    c_full, c_pad = C // 128 * 128, -(-C // 128) * 128

    def total(p_ref, got_ref, rows):
        return ((p_ref[0, rows, :].astype(f32) + got_ref[0, rows, :].astype(f32)) + got_ref[1, rows, :].astype(f32)) \
            + got_ref[2, rows, :].astype(f32)

    def body(idx_ref, p_ref, got_ref, o_ref):
        def strip(rows):
            o_ref[0, rows, :] = total(p_ref, got_ref, rows)
        _for_strips(tr, strip)

    def body_t(idx_ref, p_ref, got_ref, o_ref, buf):
        if c_pad > c_full:
            buf[:, pl.ds(c_full, c_pad - c_full)] = jnp.zeros((tr, c_pad - c_full), f32)

        def strip(rows):
            buf[rows, pl.ds(0, C)] = total(p_ref, got_ref, rows)
        _for_strips(tr, strip)
        o_ref[...] = buf[...].T[:C]

    in_specs = [pl.BlockSpec((1, tr, C), lambda i, idx: (idx[0], i, 0)), pl.BlockSpec((3, tr, C), lambda i, idx: (0, i, 0))]
    x, y, c = _me()
    idx = jnp.stack([2 * x + y, c]).astype(jnp.int32)
    if transposed:
        grid_spec = pltpu.PrefetchScalarGridSpec(num_scalar_prefetch=1, grid=(hr // tr,), in_specs=in_specs,
                                                 out_specs=pl.BlockSpec((C, tr), lambda i, idx: (0, idx[1] * (hr // tr) + i)),
                                                 scratch_shapes=[pltpu.VMEM((tr, c_pad), f32)])
        return pl.pallas_call(body_t, name=name, grid_spec=grid_spec, out_shape=jax.ShapeDtypeStruct((C, 2 * hr), f32),
                              compiler_params=_params(("parallel",)))(idx, p, got)
    grid_spec = pltpu.PrefetchScalarGridSpec(num_scalar_prefetch=1, grid=(hr // tr,), in_specs=in_specs,
                                             out_specs=pl.BlockSpec((1, tr, C), lambda i, idx: (idx[1], i, 0)))
    return pl.pallas_call(body, name=name, grid_spec=grid_spec, out_shape=jax.ShapeDtypeStruct((2, hr, C), f32),
                          compiler_params=_params(("parallel",)))(idx, p, got)


def _adamw(w, g, m, v, name, tr):
    _, R, C = w.shape
    rs = min(8, tr)

    def body(w_ref, g_ref, m_ref, v_ref, d_ref, nm_ref, nv_ref):
        def strip(s, carry):
            rows = pl.ds(pl.multiple_of(s * rs, rs), rs)
            gv = g_ref[0, rows, :]
            mn = ADAM_B1 * m_ref[0, rows, :] + (1.0 - ADAM_B1) * gv
            vn = ADAM_B2 * v_ref[0, rows, :] + (1.0 - ADAM_B2) * (gv * gv)
            m_hat = mn / (1.0 - ADAM_B1 ** ADAM_STEP)
            v_hat = vn / (1.0 - ADAM_B2 ** ADAM_STEP)
            d_ref[0, rows, :] = -ADAM_LR * (m_hat / (jnp.sqrt(v_hat) + ADAM_EPS) + ADAM_WD * w_ref[0, rows, :])
            nm_ref[0, rows, :] = mn
            nv_ref[0, rows, :] = vn
            return carry

        if R % tr == 0:
            lax.fori_loop(0, tr // rs, strip, 0, unroll=min(2, tr // rs))
        else:
            lax.fori_loop(0, jnp.minimum(tr, R - pl.program_id(0) * tr) // rs, strip, 0)

    blk, grid = pl.BlockSpec((1, tr, C), lambda i: (0, i, 0)), (-(-R // tr),)
    o = jax.ShapeDtypeStruct((1, R, C), f32)
    return pl.pallas_call(body, name=name, grid=grid, in_specs=[blk] * 4, out_specs=[blk] * 3, out_shape=[o, o, o],
                          compiler_params=_params(("parallel",)))(w, g, m, v)


def _sum8(t):
    n = t.shape[1]

    def body(t_ref, o_ref):
        acc = t_ref[pl.ds(0, 1), :]
        for r in range(1, 8):
            acc = acc + t_ref[pl.ds(r, 1), :]
        o_ref[...] = acc

    return pl.pallas_call(body, name="sum_devices", out_shape=jax.ShapeDtypeStruct((1, n), f32))(t)


def _reduce_swap_start(g_main, g_dt, g_b):
    hr = g_main.shape[0]
    return _sibling_swap_start(g_main, g_dt, g_b.reshape(N_SHARD, 2, hr, g_b.shape[-1]))


def _reduce_start(swap_state, g_main, g_dt):
    hr = g_main.shape[0]
    send_sems, recv_sems, srcs, lands, _ = swap_state
    srcs, (r_main, r_dt, r_b) = _sibling_swap_wait(send_sems, recv_sems, srcs, lands, g_dt)
    p_main = _pair_sum(g_main[None], r_main[None], "pair_sum_main", SUM_ROWS // 4)
    p_dt = _pair_sum(g_dt[None], r_dt[None], "pair_sum_dt", SUM_ROWS)
    p_b = _pair_sum(srcs[2], r_b, "pair_sum_b", SUM_ROWS)
    p_in = jnp.transpose(_w_in_grad_full(p_main[0], p_dt[0]).reshape(hr, N_SHARD, W_IN_SHARD), (1, 0, 2))
    return _chip_exchange_start([p_in, p_b])


def _reduce_finish(state, after):
    send_sems, recv_sems, srcs, lands, _ = state
    (p_in, p_b), (got_in, got_b) = _chip_exchange_wait(send_sems, recv_sems, srcs, lands, after)
    mine_in, mine_b = _own_sum(p_in, got_in, "own_sum_in", transposed=True), _own_sum(p_b, got_b, "own_sum_b")
    full_in_t, full_b = _share_halves([mine_in, mine_b], [True, False], "share_halves")
    return full_in_t, full_b.reshape(-1, full_b.shape[-1])


def kernel(x, positions, norm1_w, w_in, conv_w, conv_b, dt_bias, a_log, d_skip, ssd_norm_w, w_br_ret, w_br_ssd, w_out, norm_f_w, loss_target, m_norm1_w, m_w_in, m_conv_w, m_conv_b, m_dt_bias, m_a_log, m_d_skip, m_ssd_norm_w, m_w_br_ret, m_w_br_ssd, m_w_out, m_norm_f_w, v_norm1_w, v_w_in, v_conv_w, v_conv_b, v_dt_bias, v_a_log, v_d_skip, v_ssd_norm_w, v_w_br_ret, v_w_br_ssd, v_w_out, v_norm_f_w):
    D = D_MODEL
    xi, yi, ci = _me()
    k = 2 * xi + yi
    me = 2 * k + ci
    weights = dict(norm1_w=norm1_w, w_in=w_in, conv_w=conv_w, conv_b=conv_b, dt_bias=dt_bias, a_log=a_log, d_skip=d_skip,
                   ssd_norm_w=ssd_norm_w, w_br_ret=w_br_ret, w_br_ssd=w_br_ssd, w_out=w_out, norm_f_w=norm_f_w)
    mom1 = dict(norm1_w=m_norm1_w, w_in=m_w_in, conv_w=m_conv_w, conv_b=m_conv_b, dt_bias=m_dt_bias, a_log=m_a_log, d_skip=m_d_skip,
                ssd_norm_w=m_ssd_norm_w, w_br_ret=m_w_br_ret, w_br_ssd=m_w_br_ssd, w_out=m_w_out, norm_f_w=m_norm_f_w)
    mom2 = dict(norm1_w=v_norm1_w, w_in=v_w_in, conv_w=v_conv_w, conv_b=v_conv_b, dt_bias=v_dt_bias, a_log=v_a_log, d_skip=v_d_skip,
                ssd_norm_w=v_ssd_norm_w, w_br_ret=v_w_br_ret, w_br_ssd=v_w_br_ssd, w_out=v_w_out, norm_f_w=v_norm_f_w)

    a_sh = w_in[0].astype(bf16)
    b_sh = jnp.concatenate([w_br_ret[0], w_br_ssd[0], w_out[0]], axis=0).astype(bf16)
    ga, gc = _gather_weights(a_sh, conv_w[0])
    ga, b_late = lax.optimization_barrier((ga, b_sh))
    late_send, late_recv, late_src, late_land, late_token = _gather_late_start(b_late)
    own = lambda g, s: lax.dynamic_update_slice_in_dim(g, s[None], k, axis=0)
    ga, gc = own(ga, a_sh), own(gc, conv_w[0])
    w_main, w_dt = _w_main_from_shards(ga)
    conv_full = jnp.transpose(gc, (1, 0, 2)).reshape(SSD_CONV, CONV_DIM)

    def late_weights(after):
        gb = own(_gather_late_wait(late_send, late_recv, late_src, late_land, after), b_sh)
        return gb[:, 0:512].reshape(2048, D), gb[:, 512:1536].reshape(4096, D), gb[:, 1536:2048].reshape(2048, D)

    class Reducer:
        @staticmethod
        def halves():
            return (1 - ci).reshape(1).astype(jnp.int32), ci.reshape(1).astype(jnp.int32)

        @staticmethod
        def first(g_main, g_dt, g_w_br, g_w_bs, g_w_o):
            g_b = jnp.concatenate([g_w_br.reshape(N_SHARD, 512, D), g_w_bs.reshape(N_SHARD, 1024, D),
                                   g_w_o.reshape(N_SHARD, 512, D)], axis=1)
            return _reduce_swap_start(g_main, g_dt, g_b)

        second = staticmethod(_reduce_start)

    loss, grad_x, g, reduce_state = _local_step(x[0], positions[0], loss_target[0], norm1_w, w_main, w_dt, conv_full, conv_b, dt_bias,
                                                a_log, d_skip, ssd_norm_w, late_token, late_weights, norm_f_w, Reducer)

    grad_w_in_t, full_b = _reduce_finish(reduce_state, g["norm1_w"])
    grad_mats = dict(w_br_ret=full_b[0:512], w_br_ssd=full_b[512:1536], w_out=full_b[1536:2048])

    small = [(n, weights[n].size) for n in ("norm1_w", "conv_b", "dt_bias", "a_log", "d_skip", "ssd_norm_w", "norm_f_w")]
    parts = [jnp.pad(loss.reshape(1, 1), ((0, 0), (0, 127)))] + [g[n].reshape(1, -1) for n, _ in small] + [g["conv_w"].reshape(1, -1)]
    vec = jnp.concatenate(parts, axis=1)
    nv = vec.shape[1]
    nvp = -(-nv // 128) * 128
    vec = jnp.pad(vec, ((0, 0), (0, nvp - nv)))
    total = _sum8(lax.dynamic_update_slice_in_dim(_gather_vec(vec), vec, me, axis=0))
    loss_out = total[0, 0]
    off = 128
    grad_small = {}
    for n, sz in small:
        grad_small[n] = total[:, off:off + sz]
        off += sz
    g_conv = total[:, off:off + SSD_CONV * CONV_DIM].reshape(SSD_CONV, CONV_DIM)
    g_conv = lax.dynamic_slice_in_dim(g_conv, k * (CONV_DIM // N_SHARD), CONV_DIM // N_SHARD, axis=1)
    grad_small["conv_w"] = g_conv.reshape(1, -1)

    upd = {}
    for n in ("w_br_ret", "w_br_ssd", "w_out"):
        upd[n] = _adamw(weights[n], grad_mats[n][None], mom1[n], mom2[n], "adamw_" + n, tr=SUM_ROWS)
    tp = lambda t: jnp.swapaxes(t, 1, 2)
    upd["w_in"] = tuple(tp(t) for t in _adamw(tp(w_in), grad_w_in_t[None], tp(m_w_in), tp(v_w_in), "adamw_w_in", tr=256))
    grad_mats["w_in"] = tp(grad_w_in_t[None])
    names_small = [n for n, _ in small] + ["conv_w"]
    flat = lambda d: jnp.concatenate([d[n].reshape(1, -1) for n in names_small], axis=1)
    ns = sum(weights[n].size for n in names_small)
    nsp = -(-ns // 128) * 128
    padv = lambda t: jnp.pad(t, ((0, 0), (0, nsp - ns)))
    small_upd = _adamw(padv(flat(weights))[None], padv(flat(grad_small))[None], padv(flat(mom1))[None],
                       jnp.pad(flat(mom2), ((0, 0), (0, nsp - ns)), constant_values=1.0)[None], "adamw_small", 1)
    off = 0
    for n in names_small:
        sz = weights[n].size
        upd[n] = tuple(t[0, :, off:off + sz] for t in small_upd)
        off += sz

    order = ["norm1_w", "w_in", "conv_w", "conv_b", "dt_bias", "a_log", "d_skip", "ssd_norm_w", "w_br_ret", "w_br_ssd", "w_out", "norm_f_w"]
    grads_out = {**grad_mats, **grad_small}
    shp = lambda n, t: t.reshape(weights[n].shape)
    return (loss_out, grad_x[None], *[shp(n, grads_out[n]) for n in order], *[shp(n, upd[n][0]) for n in order],
            *[shp(n, upd[n][1]) for n in order], *[shp(n, upd[n][2]) for n in order])
```

```python
import jax
import jax.numpy as jnp
import numpy as np
from jax import lax
from jax.experimental import pallas as pl
from jax.experimental.pallas import tpu as pltpu

f32 = jnp.float32
bf16 = jnp.bfloat16
HIGHEST = lax.Precision.HIGHEST
MESH = pl.DeviceIdType.MESH

D_MODEL = 2048
EPS = 1e-6
CHUNK = 64
RET_HEADS = 8
RET_DK = 256
RET_HW = 4 * RET_DK
RET_HP = 2
RET_UNROLL_FWD, RET_UNROLL_BWD = 4, 8
ROPE_THETA = 10000.0
SSD_WIDTH = 4096
SSD_GROUPS = 8
SSD_STATE = 128
SSD_GW = 512
SSD_GC = SSD_GW + 2 * SSD_STATE
SSD_HPG = 8
SSD_CONV = 4
CONV_DIM = 6144
SSD_HEADS = 64
LS = 128

C_RET, C_Z, C_GATES, C_XBC = 0, 8192, 12288, 16384
N_MAIN = 22528
DT_OFF = 18432
IN_PROJ = 22592
N_SHARD = 4
W_IN_SHARD = IN_PROJ // N_SHARD

ADAM_LR, ADAM_B1, ADAM_B2, ADAM_EPS, ADAM_WD, ADAM_STEP = 0.001, 0.9, 0.999, 1e-08, 0.01, 10

VMEM_LIMIT = 56 * 1024 * 1024
SUM_ROWS = 128
ANY = pl.BlockSpec(memory_space=pl.ANY)


def _params(dims):
    return pltpu.CompilerParams(dimension_semantics=dims, vmem_limit_bytes=VMEM_LIMIT)


def _silu(x):
    return x * jax.nn.sigmoid(x)


def _dsilu(x):
    s = jax.nn.sigmoid(x)
    return s * (1.0 + x * (1.0 - s))


def _nt(a, b):
    return lax.dot_general(a, b, (((1,), (1,)), ((), ())), preferred_element_type=f32)


def _tn(a, b):
    return lax.dot_general(a, b, (((0,), (0,)), ((), ())), preferred_element_type=f32)


def _nn(a, b):
    return jnp.dot(a, b, preferred_element_type=f32)


def _hi(a, b):
    return jnp.dot(a, b, precision=HIGHEST, preferred_element_type=f32)


def _split(a):
    hi = a.astype(bf16)
    return hi, (a - hi.astype(f32)).astype(bf16)


def _sel_r(a, sel):
    hi, lo = _split(a)
    return _nn(hi, sel) + _nn(lo, sel)


def _sel_l(sel, a):
    hi, lo = _split(a)
    return _nn(sel, hi) + _nn(sel, lo)


def _rows_to_cols(t, eye):
    hi = t.astype(bf16)
    r1 = t - hi.astype(f32)
    mid = r1.astype(bf16)
    lo = (r1 - mid.astype(f32)).astype(bf16)
    return _nt(eye, hi) + _nt(eye, mid) + _nt(eye, lo)


def _xbc_group_major(t):
    R = t.shape[0]
    nb = SSD_GROUPS * SSD_STATE
    parts = [t[:, :SSD_WIDTH].reshape(R, SSD_GROUPS, SSD_GW), t[:, SSD_WIDTH:SSD_WIDTH + nb].reshape(R, SSD_GROUPS, SSD_STATE),
             t[:, SSD_WIDTH + nb:].reshape(R, SSD_GROUPS, SSD_STATE)]
    return jnp.concatenate(parts, axis=2).reshape(R, CONV_DIM)


def _xbc_original(t):
    R = t.shape[0]
    g = t.reshape(R, SSD_GROUPS, SSD_GC)
    parts = [g[:, :, :SSD_GW].reshape(R, SSD_WIDTH), g[:, :, SSD_GW:SSD_GW + SSD_STATE].reshape(R, SSD_GROUPS * SSD_STATE),
             g[:, :, SSD_GW + SSD_STATE:].reshape(R, SSD_GROUPS * SSD_STATE)]
    return jnp.concatenate(parts, axis=1)


def _main_segments():
    segs = []
    for h in range(RET_HEADS):
        segs += [(base + RET_DK * h, RET_DK) for base in (0, 2048, 4096, 6144)]
    segs += [(8192, SSD_WIDTH), (DT_OFF + SSD_HEADS, 2 * D_MODEL)]
    nb = SSD_GROUPS * SSD_STATE
    for g in range(SSD_GROUPS):
        segs += [(12288 + SSD_GW * g, SSD_GW), (12288 + SSD_WIDTH + SSD_STATE * g, SSD_STATE),
                 (12288 + SSD_WIDTH + nb + SSD_STATE * g, SSD_STATE)]
    return segs


def _w_main_from_shards(shards):
    def cols(lo, hi):
        out = []
        while lo < hi:
            s = lo // W_IN_SHARD
            top = min(hi, (s + 1) * W_IN_SHARD)
            out.append(shards[s][:, lo - s * W_IN_SHARD:top - s * W_IN_SHARD])
            lo = top
        return out

    main = jnp.concatenate([p for s, n in _main_segments() for p in cols(s, s + n)], axis=1)
    w_dt = jnp.pad(jnp.concatenate(cols(DT_OFF, DT_OFF + SSD_HEADS), axis=1), ((0, 0), (0, 128 - SSD_HEADS)))
    return main, w_dt


def _w_in_grad_full(g_main, g_dt):
    D = g_main.shape[0]
    ret = jnp.transpose(g_main[:, :C_Z].reshape(D, RET_HEADS, 4, RET_DK), (0, 2, 1, 3)).reshape(D, C_Z)
    return jnp.concatenate([ret, g_main[:, C_Z:C_GATES], _xbc_original(g_main[:, C_XBC:]), g_dt[:, :SSD_HEADS],
                            g_main[:, C_GATES:C_XBC]], axis=1)


def _mm(pairs, M, N, *, tm, tn, out_dtype, name, tb=False, row_off=None):
    P = len(pairs)
    nks = [K // tk for (_, _, _, _, _, K, tk) in pairs]
    starts = [int(s) for s in np.cumsum([0] + nks[:-1])]
    KT = int(sum(nks))
    npf = 0 if row_off is None else 1
    in_specs, args = [], []
    for (a, a_cb, b, b_kb, b_nb, K, tk), s, nk in zip(pairs, starts, nks):
        def kk(k, s=s, nk=nk):
            return jnp.clip(k - s, 0, nk - 1)
        in_specs.append(pl.BlockSpec((tm, tk), lambda m, n, k, *pf, kk=kk, a_cb=a_cb: (m + (pf[0][0] if pf else 0), a_cb + kk(k))))
        if tb:
            in_specs.append(pl.BlockSpec((tn, tk), lambda m, n, k, *pf, kk=kk, b_kb=b_kb, b_nb=b_nb: (b_nb + n, b_kb + kk(k))))
        else:
            in_specs.append(pl.BlockSpec((tk, tn), lambda m, n, k, *pf, kk=kk, b_kb=b_kb, b_nb=b_nb: (b_kb + kk(k), b_nb + n)))
        args += [a, b]

    def body(*refs):
        refs = refs[npf:]
        o_ref = refs[2 * P]
        k = pl.program_id(2)

        def prod(i):
            a = refs[2 * i][...].astype(bf16)
            b = refs[2 * i + 1][...].astype(bf16)
            return _nt(a, b) if tb else _nn(a, b)

        if KT == 1:
            o_ref[...] = prod(0).astype(out_dtype)
            return
        acc = refs[2 * P + 1]

        @pl.when(k == 0)
        def _():
            acc[...] = jnp.zeros_like(acc)

        for i in range(P):
            @pl.when((k >= starts[i]) & (k < starts[i] + nks[i]))
            def _(i=i):
                acc[...] += prod(i)

        @pl.when(k == KT - 1)
        def _():
            o_ref[...] = acc[...].astype(out_dtype)

    grid_spec = pltpu.PrefetchScalarGridSpec(
        num_scalar_prefetch=npf, grid=(M // tm, N // tn, KT), in_specs=in_specs,
        out_specs=pl.BlockSpec((tm, tn), lambda m, n, k, *pf: (m, n)),
        scratch_shapes=[] if KT == 1 else [pltpu.VMEM((tm, tn), f32)])
    return pl.pallas_call(
        body, name=name, grid_spec=grid_spec, out_shape=jax.ShapeDtypeStruct((M, N), out_dtype),
        compiler_params=_params(("parallel", "parallel", "arbitrary")),
    )(*([] if row_off is None else [row_off]), *args)


def _mm1(a, b, *, tm, tn, tk, out_dtype, name, tb=False):
    M, K = a.shape
    N = b.shape[0] if tb else b.shape[1]
    return _mm([(a, 0, b, 0, 0, K, tk)], M, N, tm=tm, tn=tn, out_dtype=out_dtype, name=name, tb=tb)


RS = 16
CS = 32


def _for_strips(n_rows, fn, rs=RS, unroll=4):
    def step(s, carry):
        fn(pl.ds(pl.multiple_of(s * rs, rs), rs))
        return carry
    n = n_rows // rs
    lax.fori_loop(0, n, step, 0, unroll=min(unroll, n))


def _norm1_fwd(x, w, tr):
    S, D = x.shape

    def body(x_ref, w_ref, h_ref, ht_ref):
        def strip(rows):
            xv = x_ref[rows, :]
            r = lax.rsqrt(jnp.mean(xv * xv, axis=-1, keepdims=True) + EPS)
            h_ref[rows, :] = (xv * r * w_ref[...]).astype(bf16)
        _for_strips(tr, strip)
        ht_ref[...] = h_ref[...].T

    return pl.pallas_call(
        body, name="norm1_fwd", grid=(S // tr,),
        in_specs=[pl.BlockSpec((tr, D), lambda i: (i, 0)), pl.BlockSpec((1, D), lambda i: (0, 0))],
        out_specs=[pl.BlockSpec((tr, D), lambda i: (i, 0)), pl.BlockSpec((D, tr), lambda i: (0, i))],
        out_shape=[jax.ShapeDtypeStruct((S, D), bf16), jax.ShapeDtypeStruct((D, S), bf16)], compiler_params=_params(("parallel",)),
    )(x, w)


def _norm1_bwd(x, w, dh, dx2, tr):
    S, D = x.shape

    def body(x_ref, w_ref, dh_ref, dx2_ref, gx_ref, gw_ref, acc):
        @pl.when(pl.program_id(0) == 0)
        def _():
            acc[...] = jnp.zeros_like(acc)

        def strip(rows):
            xv = x_ref[rows, :]
            r = lax.rsqrt(jnp.mean(xv * xv, axis=-1, keepdims=True) + EPS)
            xh = xv * r
            dhv = dh_ref[rows, :]
            acc[...] += dhv * xh
            dxh = dhv * w_ref[...]
            gx_ref[rows, :] = dx2_ref[rows, :] + r * (dxh - xh * jnp.mean(dxh * xh, axis=-1, keepdims=True))
        _for_strips(tr, strip)

        @pl.when(pl.program_id(0) == S // tr - 1)
        def _():
            gw_ref[...] = jnp.sum(acc[...], axis=0, keepdims=True)

    row = pl.BlockSpec((tr, D), lambda i: (i, 0))
    vec = pl.BlockSpec((1, D), lambda i: (0, 0))
    return pl.pallas_call(
        body, name="norm1_bwd", grid=(S // tr,), in_specs=[row, vec, row, row], out_specs=[row, vec],
        out_shape=[jax.ShapeDtypeStruct((S, D), f32), jax.ShapeDtypeStruct((1, D), f32)],
        scratch_shapes=[pltpu.VMEM((RS, D), f32)], compiler_params=_params(("arbitrary",)),
    )(x, w, dh, dx2)


def _final_fwd_bwd(x, mo, target, wf, tr):
    S, D = x.shape

    def body(x_ref, mo_ref, t_ref, w_ref, dx2_ref, dx2b_ref, loss_ref, gw_ref, acc, lacc):
        @pl.when(pl.program_id(0) == 0)
        def _():
            acc[...] = jnp.zeros_like(acc)
            lacc[...] = jnp.zeros_like(lacc)

        def strip(rows):
            x2 = x_ref[rows, :] + mo_ref[rows, :]
            r = lax.rsqrt(jnp.mean(x2 * x2, axis=-1, keepdims=True) + EPS)
            xh = x2 * r
            wv = w_ref[...]
            err = xh * wv - t_ref[rows, :]
            lacc[...] += jnp.mean(err * err, axis=-1, keepdims=True)
            dy = err * (1.0 / D)
            acc[...] += dy * xh
            dxh = dy * wv
            dx2 = r * (dxh - xh * jnp.mean(dxh * xh, axis=-1, keepdims=True))
            dx2_ref[rows, :] = dx2
            dx2b_ref[rows, :] = dx2.astype(bf16)
        _for_strips(tr, strip)

        @pl.when(pl.program_id(0) == S // tr - 1)
        def _():
            gw_ref[...] = jnp.sum(acc[...], axis=0, keepdims=True)
            loss_ref[...] = 0.5 * jnp.sum(lacc[...], axis=0, keepdims=True)

    row = pl.BlockSpec((tr, D), lambda i: (i, 0))
    vec = pl.BlockSpec((1, D), lambda i: (0, 0))
    return pl.pallas_call(
        body, name="final_norm_loss", grid=(S // tr,), in_specs=[row, row, row, vec],
        out_specs=[row, row, pl.BlockSpec((1, 1), lambda i: (0, 0)), vec],
        out_shape=[jax.ShapeDtypeStruct((S, D), f32), jax.ShapeDtypeStruct((S, D), bf16), jax.ShapeDtypeStruct((1, 1), f32),
                   jax.ShapeDtypeStruct((1, D), f32)],
        scratch_shapes=[pltpu.VMEM((RS, D), f32), pltpu.VMEM((RS, 1), f32)], compiler_params=_params(("arbitrary",)),
    )(x, mo, target, wf)


def _merge_fwd(p_r, p_s, proj, tr):
    S, D = p_r.shape

    def body(pr_ref, ps_ref, g_ref, o_ref, ot_ref):
        def strip(rows):
            gr, gs = g_ref[rows, pl.ds(0, D)].astype(f32), g_ref[rows, pl.ds(D, D)].astype(f32)
            o_ref[rows, :] = (jax.nn.sigmoid(gr) * pr_ref[rows, :] + jax.nn.sigmoid(gs) * ps_ref[rows, :]).astype(bf16)
        _for_strips(tr, strip)
        ot_ref[...] = o_ref[...].T

    row = pl.BlockSpec((tr, D), lambda i: (i, 0))
    return pl.pallas_call(
        body, name="merge_fwd", grid=(S // tr,),
        in_specs=[row, row, pl.BlockSpec((tr, 2 * D), lambda i: (i, C_GATES // (2 * D)))],
        out_specs=[row, pl.BlockSpec((D, tr), lambda i: (0, i))],
        out_shape=[jax.ShapeDtypeStruct((S, D), bf16), jax.ShapeDtypeStruct((D, S), bf16)], compiler_params=_params(("parallel",)),
    )(p_r, p_s, proj)


def _merge_bwd(dm, p_r, p_s, proj, tr):
    S, D = p_r.shape

    def body(dm_ref, pr_ref, ps_ref, g_ref, dpr_ref, dps_ref, dproj_ref):
        def strip(rows):
            dmv = dm_ref[rows, :]
            sr = jax.nn.sigmoid(g_ref[rows, pl.ds(0, D)].astype(f32))
            ss = jax.nn.sigmoid(g_ref[rows, pl.ds(D, D)].astype(f32))
            dpr_ref[rows, :] = (dmv * sr).astype(bf16)
            dps_ref[rows, :] = (dmv * ss).astype(bf16)
            dproj_ref[rows, pl.ds(0, D)] = (dmv * pr_ref[rows, :] * sr * (1.0 - sr)).astype(bf16)
            dproj_ref[rows, pl.ds(D, D)] = (dmv * ps_ref[rows, :] * ss * (1.0 - ss)).astype(bf16)
        _for_strips(tr, strip)

    row = pl.BlockSpec((tr, D), lambda i: (i, 0))
    gates = pl.BlockSpec((tr, 2 * D), lambda i: (i, C_GATES // (2 * D)))
    o = jax.ShapeDtypeStruct((S, D), bf16)
    return pl.pallas_call(
        body, name="merge_bwd", grid=(S // tr,), in_specs=[row, row, row, gates],
        out_specs=[row, row, gates], out_shape=[o, o, jax.ShapeDtypeStruct((S, N_MAIN), bf16)],
        compiler_params=_params(("parallel",)),
    )(dm, p_r, p_s, proj)


def _ssd_norm_fwd(y, proj, w, tr):
    S, W = y.shape

    def body(y_ref, z_ref, w_ref, o_ref, ot_ref):
        def strip(rows):
            u = y_ref[rows, :] * _silu(z_ref[rows, :].astype(f32))
            r = lax.rsqrt(jnp.mean(u * u, axis=-1, keepdims=True) + EPS)
            o_ref[rows, :] = (u * r * w_ref[...]).astype(bf16)
        _for_strips(tr, strip)
        ot_ref[...] = o_ref[...].T

    row = pl.BlockSpec((tr, W), lambda i: (i, 0))
    return pl.pallas_call(
        body, name="ssd_norm_fwd", grid=(S // tr,),
        in_specs=[row, pl.BlockSpec((tr, W), lambda i: (i, C_Z // W)), pl.BlockSpec((1, W), lambda i: (0, 0))],
        out_specs=[row, pl.BlockSpec((W, tr), lambda i: (0, i))],
        out_shape=[jax.ShapeDtypeStruct((S, W), bf16), jax.ShapeDtypeStruct((W, S), bf16)], compiler_params=_params(("parallel",)),
    )(y, proj, w)


def _ssd_norm_bwd(y, proj, w, dys, dproj, tr):
    S, W = y.shape

    def body(y_ref, z_ref, w_ref, d_ref, _, dy_ref, dz_ref, gw_ref, acc):
        @pl.when(pl.program_id(0) == 0)
        def _():
            acc[...] = jnp.zeros_like(acc)

        def strip(rows):
            yv, zv, dv = y_ref[rows, :], z_ref[rows, :].astype(f32), d_ref[rows, :]
            sz = _silu(zv)
            u = yv * sz
            r = lax.rsqrt(jnp.mean(u * u, axis=-1, keepdims=True) + EPS)
            un = u * r
            acc[...] += dv * un
            dun = dv * w_ref[...]
            du = r * (dun - un * jnp.mean(dun * un, axis=-1, keepdims=True))
            dy_ref[rows, :] = (du * sz).astype(bf16)
            dz_ref[rows, :] = (du * yv * _dsilu(zv)).astype(bf16)
        _for_strips(tr, strip)

        @pl.when(pl.program_id(0) == S // tr - 1)
        def _():
            gw_ref[...] = jnp.sum(acc[...], axis=0, keepdims=True)

    row = pl.BlockSpec((tr, W), lambda i: (i, 0))
    zcol = pl.BlockSpec((tr, W), lambda i: (i, C_Z // W))
    vec = pl.BlockSpec((1, W), lambda i: (0, 0))
    return pl.pallas_call(
        body, name="ssd_norm_bwd", grid=(S // tr,),
        in_specs=[row, zcol, vec, row, ANY], out_specs=[row, zcol, vec],
        out_shape=[jax.ShapeDtypeStruct((S, W), bf16), jax.ShapeDtypeStruct(dproj.shape, bf16), jax.ShapeDtypeStruct((1, W), f32)],
        input_output_aliases={4: 1}, scratch_shapes=[pltpu.VMEM((RS, W), f32)], compiler_params=_params(("arbitrary",)),
    )(y, proj, w, dys, dproj)


def _rope(t, cos, sin):
    t1, t2 = t[:, :128], t[:, 128:]
    return jnp.concatenate([t1 * cos - t2 * sin, t2 * cos + t1 * sin], axis=1)


def _rope_t(d, cos, sin):
    d1, d2 = d[:, :128], d[:, 128:]
    return jnp.concatenate([d1 * cos + d2 * sin, d2 * cos - d1 * sin], axis=1)


def _ret_specs(tb, rev_nb=None):
    def blk(i):
        return i if rev_nb is None else rev_nb - 1 - i
    head = pl.BlockSpec((tb, RET_HP * RET_HW), lambda h, i: (blk(i), h))
    tab = pl.BlockSpec((tb, 128), lambda h, i: (blk(i), 0))
    mat = pl.BlockSpec((RET_HP, CHUNK, CHUNK), lambda h, i: (h, 0, 0))
    vec = pl.BlockSpec((RET_HP, CHUNK, 1), lambda h, i: (h, 0, 0))
    one = pl.BlockSpec((RET_HP, 1, 1), lambda h, i: (h, 0, 0))
    own = pl.BlockSpec((tb, RET_HP * RET_DK), lambda h, i: (blk(i), h))
    st = pl.BlockSpec((RET_HP, tb // CHUNK, RET_DK, RET_DK), lambda h, i: (h, blk(i), 0, 0))
    return head, tab, mat, vec, one, own, st


def _ret_fwd(proj, cos, sin, intra, qdec, kdec, cdec, tb):
    S = proj.shape[0]
    nc = S // CHUNK
    scale = RET_DK ** -0.5
    dk = RET_DK

    def body(p_ref, cos_ref, sin_ref, m_ref, qd_ref, kd_ref, cd_ref, y_ref, yr_ref, yrt_ref, st_ref, st):
        @pl.when(pl.program_id(1) == 0)
        def _():
            st[...] = jnp.zeros_like(st)

        def head_chunk(hh, c, rows, cs, sn):
            mm, qd, kd, cd = m_ref[hh], qd_ref[hh], kd_ref[hh], cd_ref[hh]
            col = lambda j: pl.ds(hh * RET_HW + j * dk, dk)
            own = pl.ds(hh * dk, dk)
            qr = _rope(p_ref[rows, col(0)].astype(f32), cs, sn)
            kr = _rope(p_ref[rows, col(1)].astype(f32), cs, sn) * scale
            qb, kb, vb = qr.astype(bf16), kr.astype(bf16), p_ref[rows, col(2)].astype(bf16)
            stb = st[hh].astype(bf16)
            st_ref[hh, c] = stb
            sc = (_nt(qb, kb) * mm).astype(bf16)
            y = _nn(sc, vb) + _nn(qb, stb) * qd
            st[hh] = st[hh] * cd + _tn((kr * kd).astype(bf16), vb)
            y_ref[rows, own] = y
            mu = jnp.mean(y, axis=-1, keepdims=True)
            yc = y - mu
            var = jnp.mean(yc * yc, axis=-1, keepdims=True)
            yr_ref[rows, own] = (yc * lax.rsqrt(var + EPS) * _silu(p_ref[rows, col(3)].astype(f32))).astype(bf16)

        def chunk(c, carry):
            rows = pl.ds(pl.multiple_of(c * CHUNK, CHUNK), CHUNK)
            cs, sn = cos_ref[rows, :], sin_ref[rows, :]
            for hh in range(RET_HP):
                head_chunk(hh, c, rows, cs, sn)
            return carry

        lax.fori_loop(0, tb // CHUNK, chunk, 0, unroll=min(RET_UNROLL_FWD, tb // CHUNK))
        yrt_ref[...] = yr_ref[...].T

    head, tab, mat, vec, one, own, stspec = _ret_specs(tb)
    return pl.pallas_call(
        body, name="ret_fwd", grid=(RET_HEADS // RET_HP, S // tb),
        in_specs=[head, tab, tab, mat, vec, vec, one],
        out_specs=[own, own, pl.BlockSpec((RET_HP * RET_DK, tb), lambda h, i: (h, i)), stspec],
        out_shape=[jax.ShapeDtypeStruct((S, 2048), f32), jax.ShapeDtypeStruct((S, 2048), bf16), jax.ShapeDtypeStruct((2048, S), bf16),
                   jax.ShapeDtypeStruct((RET_HEADS, nc, dk, dk), bf16)],
        scratch_shapes=[pltpu.VMEM((RET_HP, dk, dk), f32)], compiler_params=_params(("parallel", "arbitrary")),
    )(proj, cos, sin, intra, qdec, kdec, cdec)


def _ret_bwd(proj, cos, sin, intra, qdec, kdec, cdec, y, dyr, states, dproj, tb):
    S = proj.shape[0]
    nb = S // tb
    nck = tb // CHUNK
    scale = RET_DK ** -0.5
    dk = RET_DK

    def body(p_ref, cos_ref, sin_ref, m_ref, qd_ref, kd_ref, cd_ref, y_ref, dyr_ref, st_ref, _, o_ref, dst):
        @pl.when(pl.program_id(1) == 0)
        def _():
            dst[...] = jnp.zeros_like(dst)

        def head_chunk(hh, c, rows, cs, sn):
            mm, qd, kd, cd = m_ref[hh], qd_ref[hh], kd_ref[hh], cd_ref[hh]
            col = lambda j: pl.ds(hh * RET_HW + j * dk, dk)
            own = pl.ds(hh * dk, dk)
            qr = _rope(p_ref[rows, col(0)].astype(f32), cs, sn)
            kr = _rope(p_ref[rows, col(1)].astype(f32), cs, sn) * scale
            qb, kb, vb = qr.astype(bf16), kr.astype(bf16), p_ref[rows, col(2)].astype(bf16)
            kdb = (kr * kd).astype(bf16)
            stb = st_ref[hh, c]
            yv, gv, dyrv = y_ref[rows, own], p_ref[rows, col(3)].astype(f32), dyr_ref[rows, own]
            mu = jnp.mean(yv, axis=-1, keepdims=True)
            yc = yv - mu
            rstd = lax.rsqrt(jnp.mean(yc * yc, axis=-1, keepdims=True) + EPS)
            yn = yc * rstd
            o_ref[rows, col(3)] = (dyrv * yn * _dsilu(gv)).astype(bf16)
            dyn = dyrv * _silu(gv)
            dy = rstd * (dyn - jnp.mean(dyn, axis=-1, keepdims=True) - yn * jnp.mean(dyn * yn, axis=-1, keepdims=True))
            dyb = dy.astype(bf16)
            dyqb = (dy * qd).astype(bf16)
            dstb = dst[hh].astype(bf16)
            sct =(_nt(kb, qb) * mm).astype(bf16)
            ds = (_nt(dyb, vb) * mm).astype(bf16)
            dsT = (_nt(vb, dyb) * mm).astype(bf16)
            dv = _nn(sct, dyb) + _nn(kdb, dstb)
            dqr = _nn(ds, kb) + _nt(dyqb, stb)
            dkr = _nn(dsT, qb) + _nt(vb, dstb) * kd
            dst[hh] = dst[hh] * cd + _tn(qb, dyqb)
            o_ref[rows, col(0)] = _rope_t(dqr, cs, sn).astype(bf16)
            o_ref[rows, col(1)] = (_rope_t(dkr, cs, sn) * scale).astype(bf16)
            o_ref[rows, col(2)] = dv.astype(bf16)

        def chunk(cc, carry):
            c = nck - 1 - cc
            rows = pl.ds(pl.multiple_of(c * CHUNK, CHUNK), CHUNK)
            cs, sn = cos_ref[rows, :], sin_ref[rows, :]
            for hh in range(RET_HP):
                head_chunk(hh, c, rows, cs, sn)
            return carry

        lax.fori_loop(0, nck, chunk, 0, unroll=min(RET_UNROLL_BWD, nck))

    head, tab, mat, vec, one, own, stspec = _ret_specs(tb, rev_nb=nb)
    return pl.pallas_call(
        body, name="ret_bwd", grid=(RET_HEADS // RET_HP, nb),
        in_specs=[head, tab, tab, mat, vec, vec, one, own, own, stspec, ANY],
        out_specs=head, out_shape=jax.ShapeDtypeStruct(dproj.shape, bf16), input_output_aliases={10: 0},
        scratch_shapes=[pltpu.VMEM((RET_HP, dk, dk), f32)], compiler_params=_params(("parallel", "arbitrary")),
    )(proj, cos, sin, intra, qdec, kdec, cdec, y, dyr, states, dproj)


def _conv_fwd(proj, conv_w, conv_b, tb, cw):
    S = proj.shape[0]
    off = 0

    def body(x_ref, halo_ref, w_ref, b_ref, o_ref, xe):
        xe[pl.ds(0, 8), :] = jnp.where(pl.program_id(1) == 0, 0.0, halo_ref[...])
        xe[pl.ds(8, CS), :] = x_ref[pl.ds(0, CS), :]
        ws = [w_ref[pl.ds(j, 1), :] for j in range(SSD_CONV)]
        for s in range(tb // CS):
            tap = (lambda j: xe[pl.ds(5 + j, CS), :]) if s == 0 else (lambda j, s=s: x_ref[pl.ds(s * CS - 3 + j, CS), :])
            acc = b_ref[...] + ws[0] * tap(0)
            for j in range(1, SSD_CONV):
                acc = acc + ws[j] * tap(j)
            o_ref[pl.ds(s * CS, CS), :] = acc

    return pl.pallas_call(
        body, name="conv_fwd", grid=(CONV_DIM // cw, S // tb),
        in_specs=[pl.BlockSpec((tb, cw), lambda j, i: (i, off + j)),
                  pl.BlockSpec((8, cw), lambda j, i: (jnp.maximum(i * (tb // 8) - 1, 0), off + j)),
                  pl.BlockSpec((SSD_CONV, cw), lambda j, i: (0, j)), pl.BlockSpec((1, cw), lambda j, i: (0, j))],
        out_specs=pl.BlockSpec((tb, cw), lambda j, i: (i, j)),
        out_shape=jax.ShapeDtypeStruct((S, CONV_DIM), f32),
        scratch_shapes=[pltpu.VMEM((CS + 8, cw), f32)], compiler_params=_params(("parallel", "arbitrary")),
    )(proj, proj, conv_w, conv_b)


def _conv_bwd(dpre, proj, conv_w, dproj, tb, cw):
    S, n = dpre.shape
    nb = S // tb
    xoff = C_XBC // cw

    def body(d_ref, dh_ref, x_ref, xh_ref, w_ref, _, dx_ref, gw_ref, gb_ref, de, xe, accw, accb):
        i = pl.program_id(1)

        @pl.when(i == 0)
        def _():
            accw[...] = jnp.zeros_like(accw)
            accb[...] = jnp.zeros_like(accb)

        ns = tb // CS
        de[pl.ds(0, CS), :] = d_ref[pl.ds(tb - CS, CS), :]
        de[pl.ds(CS, 8), :] = jnp.where(i == nb - 1, 0.0, dh_ref[...])
        xe[pl.ds(0, 8), :] = jnp.where(i == 0, 0.0, xh_ref[...])
        xe[pl.ds(8, CS), :] = x_ref[pl.ds(0, CS), :]
        ws = [w_ref[pl.ds(j, 1), :] for j in range(SSD_CONV)]
        fold = lambda p: sum(p[8 * q:8 * (q + 1)] for q in range(1, CS // 8)) + p[0:8]
        for s in range(ns):
            dv = d_ref[pl.ds(s * CS, CS), :]
            ahead = (lambda o: de[pl.ds(o, CS), :]) if s == ns - 1 else (lambda o, s=s: d_ref[pl.ds(s * CS + o, CS), :])
            xtap = (lambda j: xe[pl.ds(5 + j, CS), :]) if s == 0 else (lambda j, s=s: x_ref[pl.ds(s * CS - 3 + j, CS), :])
            acc = ws[SSD_CONV - 1] * dv
            for j in range(SSD_CONV - 1):
                acc = acc + ws[j] * ahead(3 - j)
            dx_ref[pl.ds(s * CS, CS), :] = acc.astype(bf16)
            accb[...] += fold(dv)
            for j in range(SSD_CONV):
                accw[j] += fold(dv * xtap(j))

        @pl.when(i == nb - 1)
        def _():
            gb_ref[...] = jnp.sum(accb[...], axis=0, keepdims=True)
            for j in range(SSD_CONV):
                gw_ref[pl.ds(j, 1), :] = jnp.sum(accw[j], axis=0, keepdims=True)

    return pl.pallas_call(
        body, name="conv_bwd", grid=(n // cw, nb),
        in_specs=[pl.BlockSpec((tb, cw), lambda j, i: (i, j)),
                  pl.BlockSpec((8, cw), lambda j, i: (jnp.minimum((i + 1) * (tb // 8), S // 8 - 1), j)),
                  pl.BlockSpec((tb, cw), lambda j, i: (i, j)),
                  pl.BlockSpec((8, cw), lambda j, i: (jnp.maximum(i * (tb // 8) - 1, 0), j)),
                  pl.BlockSpec((SSD_CONV, cw), lambda j, i: (0, j)), ANY],
        out_specs=[pl.BlockSpec((tb, cw), lambda j, i: (i, xoff + j)), pl.BlockSpec((SSD_CONV, cw), lambda j, i: (0, j)),
                   pl.BlockSpec((1, cw), lambda j, i: (0, j))],
        out_shape=[jax.ShapeDtypeStruct(dproj.shape, bf16), jax.ShapeDtypeStruct((SSD_CONV, n), f32), jax.ShapeDtypeStruct((1, n), f32)],
        input_output_aliases={5: 0},
        scratch_shapes=[pltpu.VMEM((CS + 8, cw), f32), pltpu.VMEM((CS + 8, cw), f32), pltpu.VMEM((SSD_CONV, 8, cw), f32),
                        pltpu.VMEM((8, cw), f32)],
        compiler_params=_params(("parallel", "arbitrary")),
    )(dpre, dpre, proj, proj, conv_w, dproj)


def _dt_prep(dt_raw, dt_bias, a_log, tb):
    S = dt_raw.shape[0]

    def body(r_ref, b_ref, al_ref, dt_ref, sg_ref, ac_ref):
        li = lax.broadcasted_iota(jnp.int32, (LS, LS), 0)
        si = lax.broadcasted_iota(jnp.int32, (LS, LS), 1)
        tri = (li >= si).astype(f32)
        neg_a = -jnp.exp(al_ref[...])
        for c in range(tb // LS):
            rows = pl.ds(c * LS, LS)
            xv = r_ref[rows, :] + b_ref[...]
            dtv = jax.nn.softplus(xv)
            dt_ref[rows, :] = dtv
            sg_ref[rows, :] = jax.nn.sigmoid(xv)
            ac_ref[rows, :] = _hi(tri, dtv * neg_a)

    row = pl.BlockSpec((tb, 128), lambda i: (i, 0))
    vec = pl.BlockSpec((1, 128), lambda i: (0, 0))
    o = jax.ShapeDtypeStruct((S, 128), f32)
    return pl.pallas_call(body, name="dt_prep", grid=(S // tb,), in_specs=[row, vec, vec], out_specs=[row, row, row],
                          out_shape=[o, o, o], compiler_params=_params(("parallel",)))(dt_raw, dt_bias, a_log)


def _group_major(t):
    S = t.shape[0]
    return jnp.transpose(t[:, :SSD_HEADS].reshape(S, SSD_GROUPS, SSD_HPG), (1, 0, 2))


def _group_major_t(t):
    S = t.shape[0]
    return jnp.transpose(t[:, :SSD_HEADS].reshape(S // LS, LS, SSD_GROUPS, SSD_HPG), (2, 0, 3, 1))


def _ssd_specs(tb, rev_nb=None):
    def blk(i):
        return i if rev_nb is None else rev_nb - 1 - i
    grp = pl.BlockSpec((tb, SSD_GC), lambda g, i: (blk(i), g))
    xs = pl.BlockSpec((tb, SSD_GW), lambda g, i: (blk(i), g))
    ph = pl.BlockSpec((1, tb, SSD_HPG), lambda g, i: (g, blk(i), 0))
    pht = pl.BlockSpec((1, tb // LS, SSD_HPG, LS), lambda g, i: (g, blk(i), 0, 0))
    gvec = pl.BlockSpec((1, 1, SSD_GW), lambda g, i: (g, 0, 0))
    ex = pl.BlockSpec((SSD_HPG, SSD_GW), lambda g, i: (0, 0))
    st = pl.BlockSpec((1, tb // LS, SSD_STATE, SSD_GW), lambda g, i: (g, blk(i), 0, 0))
    return grp, xs, ph, pht, gvec, ex, st


def _expander():
    return jnp.repeat(jnp.eye(SSD_HPG, dtype=f32), SSD_GW // SSD_HPG, axis=1).astype(bf16)


def _expand3(dt8, ac8, ex):
    stack = jnp.concatenate([dt8, jnp.exp(ac8), jnp.exp(ac8[LS - 1:LS, :] - ac8)], axis=0)
    wide = _sel_r(stack, ex)
    return wide[0:LS], wide[LS:2 * LS], wide[2 * LS:3 * LS]


def _ssd_fwd(pre, dt_g, ac_g, act_g, dskx, tb):
    S = pre.shape[0]
    nc = S // LS
    hd = SSD_GW // SSD_HPG

    def body(p_ref, dt_ref, ac_ref, act_ref, dsk_ref, ex_ref, y_ref, st_ref, st):
        @pl.when(pl.program_id(1) == 0)
        def _():
            st[...] = jnp.zeros_like(st)

        ex = ex_ref[...]
        li = lax.broadcasted_iota(jnp.int32, (LS, LS), 0)
        si = lax.broadcasted_iota(jnp.int32, (LS, LS), 1)
        causal = li >= si

        def chunk(c, carry):
            rows = pl.ds(pl.multiple_of(c * LS, LS), LS)
            xs = _silu(p_ref[rows, pl.ds(0, SSD_GW)])
            bcb = _silu(p_ref[rows, pl.ds(SSD_GW, SSD_STATE)]).astype(bf16)
            ccb = _silu(p_ref[rows, pl.ds(SSD_GW + SSD_STATE, SSD_STATE)]).astype(bf16)
            dt8, ac8, act = dt_ref[0, rows, :], ac_ref[0, rows, :], act_ref[0, c]
            dtx, eax, tailx = _expand3(dt8, ac8, ex)
            xdt = xs * dtx
            cb = _nt(ccb, bcb)
            stb = st[...].astype(bf16)
            st_ref[0, c] = stb
            xdtb = xdt.astype(bf16)
            outs = []
            for h in range(SSD_HPG):
                dec = jnp.exp(jnp.where(causal, ac8[:, h:h + 1] - act[h:h + 1, :], -1e30))
                outs.append(_nn((cb * dec).astype(bf16), xdtb[:, hd * h:hd * (h + 1)]))
            y_ref[rows, :] = (jnp.concatenate(outs, axis=1) + _nn(ccb, stb) * eax + dsk_ref[0] * xs).astype(bf16)
            st[...] = st[...] * eax[LS - 1:LS, :] + _tn(bcb, (xdt * tailx).astype(bf16))
            return carry

        lax.fori_loop(0, tb // LS, chunk, 0, unroll=min(4, tb // LS))

    grp, xs, ph, pht, gvec, ex, stspec = _ssd_specs(tb)
    return pl.pallas_call(
        body, name="ssd_fwd", grid=(SSD_GROUPS, S // tb),
        in_specs=[grp, ph, ph, pht, gvec, ex], out_specs=[xs, stspec],
        out_shape=[jax.ShapeDtypeStruct((S, SSD_WIDTH), bf16), jax.ShapeDtypeStruct((SSD_GROUPS, nc, SSD_STATE, SSD_GW), bf16)],
        scratch_shapes=[pltpu.VMEM((SSD_STATE, SSD_GW), f32)], compiler_params=_params(("parallel", "arbitrary")),
    )(pre, dt_g, ac_g, act_g, dskx, _expander())


def _ssd_bwd(pre, dt_g, ac_g, act_g, sg_g, dskx, nega_g, dy, states, tb):
    S = pre.shape[0]
    nb = S // tb
    nck = tb // LS
    hd = SSD_GW // SSD_HPG

    def body(p_ref, dt_ref, ac_ref, act_ref, sg_ref, dsk_ref, na_ref, ex_ref, ext_ref, dy_ref, st_ref,
             dp_ref, ddt_ref, gsk_ref, gal_ref, gdb_ref, dst, skacc):
        @pl.when(pl.program_id(1) == 0)
        def _():
            dst[...] = jnp.zeros_like(dst)
            skacc[...] = jnp.zeros_like(skacc)
            gal_ref[...] = jnp.zeros_like(gal_ref)
            gdb_ref[...] = jnp.zeros_like(gdb_ref)

        ex, ext = ex_ref[...], ext_ref[...]
        li = lax.broadcasted_iota(jnp.int32, (LS, LS), 0)
        si = lax.broadcasted_iota(jnp.int32, (LS, LS), 1)
        causal = li >= si
        anti = si >= li
        upper = anti.astype(bf16)
        eye = (si == li).astype(bf16)
        last_row = (lax.broadcasted_iota(jnp.int32, (LS, 1), 0) == LS - 1).astype(f32)
        head_id = lax.broadcasted_iota(jnp.int32, (1, SSD_HPG), 1)
        head_col = lax.broadcasted_iota(jnp.int32, (SSD_HPG, 1), 0)
        neg_a = na_ref[0]
        dskv = dsk_ref[0]

        def chunk(cc, carry):
            c = nck - 1 - cc
            rows = pl.ds(pl.multiple_of(c * LS, LS), LS)
            px = p_ref[rows, pl.ds(0, SSD_GW)]
            pb = p_ref[rows, pl.ds(SSD_GW, SSD_STATE)]
            pc = p_ref[rows, pl.ds(SSD_GW + SSD_STATE, SSD_STATE)]
            sgx, sgb, sgc = jax.nn.sigmoid(px), jax.nn.sigmoid(pb), jax.nn.sigmoid(pc)
            xs = px * sgx
            bcb = (pb * sgb).astype(bf16)
            ccb = (pc * sgc).astype(bf16)
            dt8, ac8, act = dt_ref[0, rows, :], ac_ref[0, rows, :], act_ref[0, c]
            dtx, eax, tailx = _expand3(dt8, ac8, ex)
            xdt = xs * dtx
            ex_last = eax[LS - 1:LS, :]
            stb = st_ref[0, c]
            dyv = dy_ref[rows, :]
            dyb = dyv.astype(bf16)
            xdtb = xdt.astype(bf16)
            skacc[...] += jnp.sum(dyv * xs, axis=0, keepdims=True)
            yinter = _nn(ccb, stb) * eax
            dzb = (dyv * eax).astype(bf16)
            dcc = _nt(dzb, stb)
            dstv = dst[...]
            dstb = dstv.astype(bf16)
            xt = xdt * tailx
            dxt = _nn(bcb, dstb)
            dbc = _nt(xt.astype(bf16), dstb)
            dxdt = dxt * tailx
            lastrow = jnp.sum(dxt * xt, axis=0, keepdims=True) + jnp.sum(dstv * stb.astype(f32), axis=0, keepdims=True) * ex_last
            dst[...] = dstv * ex_last + _tn(ccb, dzb)
            cb = _nt(ccb, bcb)
            cbt = _nt(bcb, ccb)
            dcb = jnp.zeros((LS, LS), f32)
            dac8 = jnp.zeros((LS, SSD_HPG), f32)
            dact = jnp.zeros((SSD_HPG, LS), f32)
            dxin = []
            for h in range(SSD_HPG):
                sl = slice(hd * h, hd * (h + 1))
                col, rowv = ac8[:, h:h + 1], act[h:h + 1, :]
                dec = jnp.exp(jnp.where(causal, col - rowv, -1e30))
                dect = jnp.exp(jnp.where(anti, rowv - col, -1e30))
                gm = cb * dec
                dgm = _nt(dyb[:, sl], xdtb[:, sl])
                dxin.append(_nn((cbt * dect).astype(bf16), dyb[:, sl]))
                dcb = dcb + dgm * dec
                w = dgm * gm
                dac8 = dac8 + jnp.sum(w, axis=1, keepdims=True) * (head_id == h).astype(f32)
                dact = dact + (head_col == h).astype(f32) * jnp.sum(w, axis=0, keepdims=True)
            dxintra = jnp.concatenate(dxin, axis=1)
            dcbb = dcb.astype(bf16)
            dcc = dcc + _nn(dcbb, bcb)
            dbc = dbc + _tn(dcbb, ccb)
            dxdt = dxdt + dxintra
            dacx = dyv * yinter - dxt * xt + last_row * lastrow
            red = _sel_r(jnp.concatenate([dacx, dxdt * xs], axis=0), ext)
            dac8 = dac8 - _rows_to_cols(dact, eye) + red[0:LS]
            da8 = _sel_l(upper, dac8)
            ddt8 = red[LS:2 * LS] + da8 * neg_a
            gal_ref[0] += jnp.sum(da8 * dt8 * neg_a, axis=0, keepdims=True)
            ddr = ddt8 * sg_ref[0, rows, :]
            ddt_ref[0, rows, :] = ddr
            gdb_ref[0] += jnp.sum(ddr, axis=0, keepdims=True)
            dsilu = lambda p, s: s * (1.0 + p * (1.0 - s))
            dp_ref[rows, pl.ds(0, SSD_GW)] = (dskv * dyv + dxdt * dtx) * dsilu(px, sgx)
            dp_ref[rows, pl.ds(SSD_GW, SSD_STATE)] = dbc * dsilu(pb, sgb)
            dp_ref[rows, pl.ds(SSD_GW + SSD_STATE, SSD_STATE)] = dcc * dsilu(pc, sgc)
            return carry

        lax.fori_loop(0, nck, chunk, 0, unroll=min(4, nck))

        @pl.when(pl.program_id(1) == nb - 1)
        def _():
            gsk_ref[0] = skacc[...]

    grp, xs, ph, pht, gvec, ex, stspec = _ssd_specs(tb, rev_nb=nb)
    small = pl.BlockSpec((1, 1, SSD_HPG), lambda g, i: (g, 0, 0))
    ext = pl.BlockSpec((SSD_GW, SSD_HPG), lambda g, i: (0, 0))
    sm = jax.ShapeDtypeStruct((SSD_GROUPS, 1, SSD_HPG), f32)
    expander = _expander()
    return pl.pallas_call(
        body, name="ssd_bwd", grid=(SSD_GROUPS, nb),
        in_specs=[grp, ph, ph, pht, ph, gvec, small, ex, ext, xs, stspec],
        out_specs=[grp, ph, gvec, small, small],
        out_shape=[jax.ShapeDtypeStruct((S, CONV_DIM), f32), jax.ShapeDtypeStruct((SSD_GROUPS, S, SSD_HPG), f32),
                   jax.ShapeDtypeStruct((SSD_GROUPS, 1, SSD_GW), f32), sm, sm],
        scratch_shapes=[pltpu.VMEM((SSD_STATE, SSD_GW), f32), pltpu.VMEM((1, SSD_GW), f32)],
        compiler_params=_params(("parallel", "arbitrary")),
    )(pre, dt_g, ac_g, act_g, sg_g, dskx, nega_g, expander, expander.T, dy, states)


def _tiles(S):
    return dict(tb=min(512, S), tr=min(256, S), tm=min(1024, S))


def _local_step(x, positions, target, norm1_w, w_main, w_dt, conv_w, conv_b, dt_bias, a_log, d_skip, ssd_norm_w,
                late_token, late_weights, norm_f_w, reducer):
    S, D = x.shape
    t = _tiles(S)
    tb, tr, tm = t["tb"], t["tr"], t["tm"]

    half = RET_DK // 2
    inv_freq = ROPE_THETA ** (-jnp.arange(half, dtype=f32) / half)
    ang = positions.astype(f32)[:, None] * inv_freq
    cos, sin = jnp.cos(ang), jnp.sin(ang)
    log_gamma = jnp.log1p(-(2.0 ** (-5.0 - jnp.arange(RET_HEADS, dtype=f32))))
    idx = jnp.arange(CHUNK, dtype=f32)
    intra = jnp.exp(jnp.abs(idx[:, None] - idx[None, :]) * log_gamma[:, None, None])
    qdec = jnp.exp((idx + 1.0)[None, :] * log_gamma[:, None])[:, :, None]
    kdec = jnp.exp((CHUNK - 1.0 - idx)[None, :] * log_gamma[:, None])[:, :, None]
    cdec = jnp.exp(CHUNK * log_gamma)[:, None, None]
    conv_wm, conv_bm = _xbc_group_major(conv_w), _xbc_group_major(conv_b)

    h, ht = _norm1_fwd(x, norm1_w + late_token[0, 0], tr)
    proj = _mm([(h, 0, w_main, 0, 0, D, D)], S, C_XBC, tm=tm, tn=1024, out_dtype=bf16, name="proj_main")
    proj_x = _mm([(h, 0, w_main, 0, C_XBC // 1024, D, D)], S, CONV_DIM, tm=tm, tn=1024, out_dtype=f32, name="proj_xbc")
    dt_raw = _mm1(h, w_dt, tm=tm, tn=128, tk=D, out_dtype=f32, name="proj_dt")
    y_ret, yr, yrt, ret_states = _ret_fwd(proj, cos, sin, intra, qdec, kdec, cdec, tb)
    pre = _conv_fwd(proj_x, conv_wm, conv_bm, min(1024, S), 512)
    pad64 = lambda v: jnp.pad(v, ((0, 0), (0, 128 - SSD_HEADS)))
    dt, sg, ac = _dt_prep(dt_raw, pad64(dt_bias), pad64(a_log), tb)
    dt_g, ac_g, sg_g, act_g = _group_major(dt), _group_major(ac), _group_major(sg), _group_major_t(ac)
    dskx = jnp.repeat(d_skip.reshape(SSD_GROUPS, 1, SSD_HPG), SSD_GW // SSD_HPG, axis=2)
    nega_g = (-jnp.exp(a_log)).reshape(SSD_GROUPS, 1, SSD_HPG)
    y_ssd, ssd_states = _ssd_fwd(pre, dt_g, ac_g, act_g, dskx, tb)
    ys, yst = _ssd_norm_fwd(y_ssd, proj, ssd_norm_w, tr // 2)
    w_br, w_bs, w_o = late_weights(ys)
    p_r = _mm1(yr, w_br, tm=tm, tn=1024, tk=2048, out_dtype=bf16, name="branch_ret")
    p_s = _mm1(ys, w_bs, tm=tm, tn=1024, tk=2048, out_dtype=bf16, name="branch_ssd")
    merged, mergedt = _merge_fwd(p_r, p_s, proj, tr)
    mo = _mm1(merged, w_o, tm=tm, tn=1024, tk=2048, out_dtype=bf16, name="out_proj")
    dx2, dx2b, loss, g_norm_f = _final_fwd_bwd(x, mo, target, norm_f_w.reshape(1, D), tr)

    tkt = min(4096, S)
    wg = lambda at, b, name, tn=1024: _mm1(at, b, tm=min(1024, at.shape[0]), tn=tn, tk=tkt, out_dtype=f32, name=name)
    dm = _mm1(dx2b, w_o, tm=tm, tn=1024, tk=2048, out_dtype=bf16, name="d_merged", tb=True)
    g_w_o = wg(mergedt, dx2b, "g_w_out")
    dp_r, dp_s, dproj = _merge_bwd(dm, p_r, p_s, proj, tr)
    dyr = _mm1(dp_r, w_br, tm=tm, tn=1024, tk=2048, out_dtype=bf16, name="d_yr", tb=True)
    dys = _mm1(dp_s, w_bs, tm=tm, tn=1024, tk=2048, out_dtype=bf16, name="d_ys", tb=True)
    g_w_br = wg(yrt, dp_r, "g_w_br_ret")
    g_w_bs = wg(yst, dp_s, "g_w_br_ssd")
    dy_ssd, dproj, g_ssd_norm = _ssd_norm_bwd(y_ssd, proj, ssd_norm_w, dys, dproj, tr // 2)
    dproj = _ret_bwd(proj, cos, sin, intra, qdec, kdec, cdec, y_ret, dyr, ret_states, dproj, tb)
    dpre, ddt_g, gsk, gal, gdb = _ssd_bwd(pre, dt_g, ac_g, act_g, sg_g, dskx, nega_g, dy_ssd, ssd_states, tb)
    dproj, gcw, gcb = _conv_bwd(dpre, proj_x, conv_wm, dproj, min(1024, S), 512)
    ddt = jnp.transpose(ddt_g, (1, 0, 2)).reshape(S, SSD_HEADS)
    ddt_p = jnp.pad(ddt, ((0, 0), (0, 128 - SSD_HEADS))).astype(bf16)

    hr = D // 2
    wg_half = lambda off, b, name, tn=1024: _mm([(ht, 0, b, 0, 0, S, tkt)], hr, b.shape[1], tm=hr, tn=tn, out_dtype=f32,
                                                 name=name, row_off=off)
    off_sib, off_own = reducer.halves()
    gs_main = wg_half(off_sib, dproj, "g_w_in_main_sib")
    gs_dt = wg_half(off_sib, ddt_p, "g_w_in_dt_sib", tn=128)
    swap_state = reducer.first(gs_main, gs_dt, g_w_br, g_w_bs, g_w_o)
    ddt_p = ddt_p + swap_state[-1][0, 0].astype(bf16)
    go_main = wg_half(off_own, dproj, "g_w_in_main_own")
    go_dt = wg_half(off_own, ddt_p, "g_w_in_dt_own", tn=128)
    reduce_state = reducer.second(swap_state, go_main, go_dt)
    ddt_p = ddt_p + reduce_state[-1][0, 0].astype(bf16)
    dh = _mm([(dproj, 0, w_main, 0, 0, N_MAIN, N_MAIN // 8), (ddt_p, 0, w_dt, 0, 0, 128, 128)], S, D, tm=tm, tn=1024,
             out_dtype=bf16, name="d_h", tb=True)
    grad_x, g_norm1 = _norm1_bwd(x, norm1_w, dh, dx2, tr)

    seg = lambda v: jnp.sum(v.reshape(SSD_HEADS, SSD_GW // SSD_HPG), axis=1).reshape(1, SSD_HEADS)
    grads = dict(
        norm1_w=g_norm1, w_in_main=(gs_main, go_main), w_in_dt=(gs_dt, go_dt),
        conv_w=_xbc_original(gcw), conv_b=_xbc_original(gcb),
        dt_bias=gdb.reshape(1, SSD_HEADS), a_log=gal.reshape(1, SSD_HEADS), d_skip=seg(gsk),
        ssd_norm_w=g_ssd_norm, w_br_ret=g_w_br, w_br_ssd=g_w_bs, w_out=g_w_o, norm_f_w=g_norm_f,
    )
    return loss, grad_x, grads, reduce_state


def _me():
    return lax.axis_index("x"), lax.axis_index("y"), lax.axis_index("c")


def _other_chips(x, y):
    return [(1 - x, y), (x, 1 - y), (1 - x, 1 - y)]


def _gather_weights(a, cw):
    R = a.shape[0]
    hr, hq = R // 2, R // 4

    def body(a_ref, cw_ref, ga_ref, gc_ref, send_sems, recv_sems):
        x, y, c = _me()
        me, sibling = (x, y, c), (x, y, 1 - c)
        nx, ny = (1 - x, y, c), (x, 1 - y, c)
        k, kx, ky, kd = 2 * x + y, 2 * (1 - x) + y, 2 * x + (1 - y), 2 * (1 - x) + (1 - y)

        def rows(half, q):
            return pl.ds(pl.multiple_of(half * hr + q * hq, 8), hq)

        def cp(sem, shard, half, q, to, src=None):
            dst = ga_ref.at[shard, rows(half, q), :]
            return pltpu.make_async_remote_copy(src_ref=dst if src is None else src, dst_ref=dst, send_sem=send_sems.at[sem],
                                                recv_sem=recv_sems.at[sem], device_id=to, device_id_type=MESH)

        def small(j, src_shard, to):
            return pltpu.make_async_remote_copy(
                src_ref=cw_ref, dst_ref=gc_ref.at[src_shard], send_sem=send_sems.at[12 + j], recv_sem=recv_sems.at[12 + j],
                device_id=to, device_id_type=MESH)

        own = lambda q: a_ref.at[rows(c, q), :]
        smalls = [small(j, k, (*chip, c)) for j, chip in enumerate(_other_chips(x, y))]
        sends = [cp(0, k, c, 0, nx, own(0)), cp(2, k, c, 1, ny, own(1)), cp(1, k, c, 1, nx, own(1)), cp(3, k, c, 0, ny, own(0))]
        for s in sends + smalls:
            s.start()
        arrivals = [(0, kx, 0, (4, ny)), (2, ky, 1, (5, nx)), (1, kx, 1, None), (3, ky, 0, None), (4, kd, 0, None), (5, kd, 1, None)]
        for sem, shard, q, onward in arrivals:
            cp(sem, shard, c, q, me).wait_recv()
            if onward is not None:
                sends.append(cp(onward[0], shard, c, q, onward[1]))
                sends[-1].start()
            sends.append(cp(6 + sem, shard, c, q, sibling))
            sends[-1].start()
        for sem, shard, q, _ in arrivals:
            cp(6 + sem, shard, 1 - c, q, me).wait_recv()
        for j, chip in enumerate(_other_chips(x, y)):
            small(j, 2 * chip[0] + chip[1], me).wait_recv()
        for s in sends + smalls:
            s.wait_send()

    return pl.pallas_call(
        body, name="gather_weights", in_specs=[ANY, ANY], out_specs=[ANY, ANY],
        out_shape=[jax.ShapeDtypeStruct((N_SHARD,) + a.shape, a.dtype), jax.ShapeDtypeStruct((N_SHARD,) + cw.shape, cw.dtype)],
        scratch_shapes=[pltpu.SemaphoreType.DMA((15,)), pltpu.SemaphoreType.DMA((15,))],
        compiler_params=pltpu.CompilerParams(has_side_effects=True),
    )(a, cw)


def _gather_late_copies(src, land, send_sems, recv_sems):
    x, y, c = _me()
    k = 2 * x + y
    return [pltpu.make_async_remote_copy(src_ref=src, dst_ref=land.at[k], send_sem=send_sems.at[j], recv_sem=recv_sems.at[j],
                                         device_id=(*chip, c), device_id_type=MESH) for j, chip in enumerate(_other_chips(x, y))]


def _gather_late_start(b):
    land = lax.empty((N_SHARD,) + b.shape, b.dtype)

    def body(b_ref, land_ref, send_sems, recv_sems, b_thru, land_thru, token):
        for cp in _gather_late_copies(b_ref, land_ref, send_sems, recv_sems):
            cp.start()
        token[...] = jnp.zeros_like(token)

    return pl.pallas_call(
        body, name="gather_late_start", in_specs=[HBM, HBM],
        out_specs=(SEM, SEM, HBM, HBM, pl.BlockSpec(memory_space=pltpu.VMEM)),
        out_shape=(pltpu.SemaphoreType.DMA((3,)), pltpu.SemaphoreType.DMA((3,)), pltpu.HBM(b.shape, b.dtype),
                   pltpu.HBM(land.shape, land.dtype), jax.ShapeDtypeStruct((8, 128), f32)),
        input_output_aliases={0: 2, 1: 3}, compiler_params=pltpu.CompilerParams(has_side_effects=DATAFLOW),
    )(pltpu.with_memory_space_constraint(b, pltpu.HBM), pltpu.with_memory_space_constraint(land, pltpu.HBM))


def _gather_late_wait(send_sems, recv_sems, src, land, after):
    def body(b_ref, land_ref, send_sems_ref, recv_sems_ref, after_ref, b_dead, land_out):
        x, y, c = _me()
        for j, chip in enumerate(_other_chips(x, y)):
            kk = 2 * chip[0] + chip[1]
            cp = pltpu.make_async_remote_copy(src_ref=b_ref, dst_ref=land_ref.at[kk], send_sem=send_sems_ref.at[j],
                                              recv_sem=recv_sems_ref.at[j], device_id=(x, y, c), device_id_type=MESH)
            cp.wait_send()
            cp.wait_recv()

    return pl.pallas_call(
        body, name="gather_late_wait", in_specs=[HBM, HBM, SEM, SEM, ANY], out_specs=[HBM, HBM],
        out_shape=[pltpu.HBM(src.shape, src.dtype), pltpu.HBM(land.shape, land.dtype)], input_output_aliases={0: 0, 1: 1},
        compiler_params=pltpu.CompilerParams(has_side_effects=DATAFLOW),
    )(src, land, send_sems, recv_sems, after)[1]


HBM = pl.BlockSpec(memory_space=pltpu.HBM)
SEM = pl.BlockSpec(memory_space=pltpu.SEMAPHORE)
DATAFLOW = pltpu.SideEffectType.DATAFLOW_SIDE_EFFECTING


def _swap_copies(srcs, lands, send_sems, recv_sems):
    x, y, c = _me()

    def cp(src, dst, q):
        return pltpu.make_async_remote_copy(src_ref=src, dst_ref=dst, send_sem=send_sems.at[q], recv_sem=recv_sems.at[q],
                                            device_id=(x, y, 1 - c), device_id_type=MESH)

    return [cp(srcs[0], lands[0], 0), cp(srcs[1], lands[1], 1)] + [cp(srcs[2].at[s, 1 - c], lands[2].at[s], 2 + s) for s in range(N_SHARD)]


def _sibling_swap_start(g_main, g_dt, g_b):
    srcs = [g_main, g_dt, g_b]
    lands = [lax.empty(g_main.shape, g_main.dtype), lax.empty(g_dt.shape, g_dt.dtype),
             lax.empty(g_b.shape[:1] + g_b.shape[2:], g_b.dtype)]

    def body(*refs):
        for cp in _swap_copies(refs[0:3], refs[3:6], refs[6], refs[7]):
            cp.start()
        refs[14][...] = jnp.zeros_like(refs[14])

    hbm = lambda a: pltpu.HBM(a.shape, a.dtype)
    out = pl.pallas_call(
        body, name="sibling_swap_start", in_specs=[HBM] * 6,
        out_specs=(SEM, SEM, *[HBM] * 6, pl.BlockSpec(memory_space=pltpu.VMEM)),
        out_shape=(pltpu.SemaphoreType.DMA((2 + N_SHARD,)), pltpu.SemaphoreType.DMA((2 + N_SHARD,)), *[hbm(a) for a in srcs + lands],
                   jax.ShapeDtypeStruct((8, 128), f32)),
        input_output_aliases={t: 2 + t for t in range(6)}, compiler_params=pltpu.CompilerParams(has_side_effects=DATAFLOW),
    )(*[pltpu.with_memory_space_constraint(a, pltpu.HBM) for a in srcs + lands])
    return out[0], out[1], list(out[2:5]), list(out[5:8]), out[8]


def _sibling_swap_wait(send_sems, recv_sems, srcs, lands, after):
    def body(*refs):
        for cp in _swap_copies(refs[0:3], refs[3:6], refs[6], refs[7]):
            cp.wait_send()
            cp.wait_recv()

    hbm = lambda a: pltpu.HBM(a.shape, a.dtype)
    out = pl.pallas_call(
        body, name="sibling_swap_wait", in_specs=[HBM] * 6 + [SEM, SEM, ANY], out_specs=[HBM] * 6,
        out_shape=[hbm(a) for a in list(srcs) + list(lands)], input_output_aliases={t: t for t in range(6)},
        compiler_params=pltpu.CompilerParams(has_side_effects=DATAFLOW),
    )(*srcs, *lands, send_sems, recv_sems, after)
    return list(out[:3]), list(out[3:])


def _exchange_copies(ins, lands, send_sems, recv_sems):
    n = len(ins)
    x, y, c = _me()
    cps = []
    for j, chip in enumerate(_other_chips(x, y)):
        kk = 2 * chip[0] + chip[1]
        for t in range(n):
            cps.append(pltpu.make_async_remote_copy(
                src_ref=ins[t].at[kk], dst_ref=lands[t].at[j], send_sem=send_sems.at[n * j + t],
                recv_sem=recv_sems.at[n * j + t], device_id=(*chip, c), device_id_type=MESH))
    return cps


def _chip_exchange_start(arrs):
    n = len(arrs)
    lands = [lax.empty((3,) + a.shape[1:], a.dtype) for a in arrs]

    def body(*refs):
        ins, lands_in = refs[:n], refs[n:2 * n]
        send_sems, recv_sems = refs[2 * n], refs[2 * n + 1]
        token = refs[4 * n + 2]
        for cp in _exchange_copies(ins, lands_in, send_sems, recv_sems):
            cp.start()
        token[...] = jnp.zeros_like(token)

    hbm = lambda a: pltpu.HBM(a.shape, a.dtype)
    out = pl.pallas_call(
        body, name="chip_exchange_start", in_specs=[HBM] * (2 * n),
        out_specs=(SEM, SEM, *[HBM] * (2 * n), pl.BlockSpec(memory_space=pltpu.VMEM)),
        out_shape=(pltpu.SemaphoreType.DMA((3 * n,)), pltpu.SemaphoreType.DMA((3 * n,)), *[hbm(a) for a in arrs],
                   *[hbm(a) for a in lands], jax.ShapeDtypeStruct((8, 128), f32)),
        input_output_aliases={t: 2 + t for t in range(2 * n)},
        compiler_params=pltpu.CompilerParams(has_side_effects=DATAFLOW),
    )(*[pltpu.with_memory_space_constraint(a, pltpu.HBM) for a in list(arrs) + lands])
    return out[0], out[1], list(out[2:2 + n]), list(out[2 + n:2 + 2 * n]), out[2 + 2 * n]


def _chip_exchange_wait(send_sems, recv_sems, srcs, lands, after):
    n = len(srcs)

    def body(*refs):
        ins, lands_in = refs[:n], refs[n:2 * n]
        send_sems_ref, recv_sems_ref = refs[2 * n], refs[2 * n + 1]
        for cp in _exchange_copies(ins, lands_in, send_sems_ref, recv_sems_ref):
            cp.wait_send()
            cp.wait_recv()

    hbm = lambda a: pltpu.HBM(a.shape, a.dtype)
    out = pl.pallas_call(
        body, name="chip_exchange_wait", in_specs=[HBM] * (2 * n) + [SEM, SEM, ANY],
        out_specs=[HBM] * (2 * n), out_shape=[hbm(a) for a in list(srcs) + list(lands)],
        input_output_aliases={t: t for t in range(2 * n)},
        compiler_params=pltpu.CompilerParams(has_side_effects=DATAFLOW),
    )(*srcs, *lands, send_sems, recv_sems, after)
    return list(out[:n]), list(out[n:])


def _share_halves(bufs, by_cols, name):
    n = len(bufs)

    def body(*refs):
        ins, outs = refs[:n], refs[n:2 * n]
        send_sems, recv_sems = refs[2 * n], refs[2 * n + 1]
        x, y, c = _me()

        def part(ref, t, half):
            if by_cols[t]:
                w = bufs[t].shape[1] // 2
                return ref.at[:, pl.ds(pl.multiple_of(half * w, 128), w)]
            return ref.at[half]

        sends = [pltpu.make_async_remote_copy(src_ref=part(ins[t], t, c), dst_ref=part(outs[t], t, c), send_sem=send_sems.at[t],
                                              recv_sem=recv_sems.at[t], device_id=(x, y, 1 - c), device_id_type=MESH) for t in range(n)]
        for cp in sends:
            cp.start()
        for t in range(n):
            pltpu.make_async_remote_copy(src_ref=part(ins[t], t, c), dst_ref=part(outs[t], t, 1 - c), send_sem=send_sems.at[t],
                                         recv_sem=recv_sems.at[t], device_id=(x, y, c), device_id_type=MESH).wait_recv()
        for cp in sends:
            cp.wait_send()

    return pl.pallas_call(
        body, name=name, in_specs=[ANY] * n, out_specs=[ANY] * n,
        out_shape=[jax.ShapeDtypeStruct(a.shape, a.dtype) for a in bufs], input_output_aliases={t: t for t in range(n)},
        scratch_shapes=[pltpu.SemaphoreType.DMA((n,)), pltpu.SemaphoreType.DMA((n,))],
        compiler_params=pltpu.CompilerParams(has_side_effects=True),
    )(*bufs)


def _gather_vec(v):
    n = v.shape[1]

    def body(v_ref, o_ref, send_sems, recv_sems):
        x, y, c = _me()
        me = 4 * x + 2 * y + c
        cps = []
        for j in range(1, 8):
            fx, fy, fc = (j >> 2) & 1, (j >> 1) & 1, j & 1
            peer = (x ^ fx, y ^ fy, c ^ fc)
            cps.append(pltpu.make_async_remote_copy(
                src_ref=v_ref, dst_ref=o_ref.at[pl.ds(me, 1), :], send_sem=send_sems.at[j - 1], recv_sem=recv_sems.at[j - 1],
                device_id=peer, device_id_type=MESH))
        for cp in cps:
            cp.start()
        for j in range(1, 8):
            fx, fy, fc = (j >> 2) & 1, (j >> 1) & 1, j & 1
            src = 4 * (x ^ fx) + 2 * (y ^ fy) + (c ^ fc)
            pltpu.make_async_remote_copy(
                src_ref=v_ref, dst_ref=o_ref.at[pl.ds(src, 1), :], send_sem=send_sems.at[j - 1], recv_sem=recv_sems.at[j - 1],
                device_id=(x, y, c), device_id_type=MESH).wait_recv()
        for cp in cps:
            cp.wait_send()

    return pl.pallas_call(
        body, name="gather_vec", in_specs=[ANY], out_specs=ANY, out_shape=jax.ShapeDtypeStruct((8, n), v.dtype),
        scratch_shapes=[pltpu.SemaphoreType.DMA((7,)), pltpu.SemaphoreType.DMA((7,))],
        compiler_params=pltpu.CompilerParams(has_side_effects=True),
    )(v)


def _pair_sum(g, r, name, tr):
    L, hr, C = r.shape
    both_halves = g.ndim == 4

    def body(c_ref, g_ref, r_ref, o_ref):
        def strip(rows):
            gv = g_ref[0, 0, rows, :] if both_halves else g_ref[0, rows, :]
            o_ref[0, rows, :] = (gv + r_ref[0, rows, :]).astype(bf16)
        _for_strips(tr, strip)

    g_spec = (pl.BlockSpec((1, 1, tr, C), lambda s, i, c_ref: (s, c_ref[0], i, 0)) if both_halves
              else pl.BlockSpec((1, tr, C), lambda s, i, c_ref: (s, i, 0)))
    grid_spec = pltpu.PrefetchScalarGridSpec(
        num_scalar_prefetch=1, grid=(L, hr // tr),
        in_specs=[g_spec, pl.BlockSpec((1, tr, C), lambda s, i, c_ref: (s, i, 0))],
        out_specs=pl.BlockSpec((1, tr, C), lambda s, i, c_ref: (s, i, 0)))
    c = lax.axis_index("c").reshape(1).astype(jnp.int32)
    return pl.pallas_call(body, name=name, grid_spec=grid_spec, out_shape=jax.ShapeDtypeStruct((L, hr, C), bf16),
                          compiler_params=_params(("parallel", "parallel")))(c, g, r)


def _own_sum(p, got, name, transposed=False):
    _, hr, C = p.shape
    tr = SUM_ROWS
    c_full, c_pad = C // 128 * 128, -(-C // 128) * 128

    def total(p_ref, got_ref, rows):
        return ((p_ref[0, rows, :].astype(f32) + got_ref[0, rows, :].astype(f32)) + got_ref[1, rows, :].astype(f32)) \
            + got_ref[2, rows, :].astype(f32)

    def body(idx_ref, p_ref, got_ref, o_ref):
        def strip(rows):
            o_ref[0, rows, :] = total(p_ref, got_ref, rows)
        _for_strips(tr, strip)

    def body_t(idx_ref, p_ref, got_ref, o_ref, buf):
        if c_pad > c_full:
            buf[:, pl.ds(c_full, c_pad - c_full)] = jnp.zeros((tr, c_pad - c_full), f32)

        def strip(rows):
            buf[rows, pl.ds(0, C)] = total(p_ref, got_ref, rows)
        _for_strips(tr, strip)
        o_ref[...] = buf[...].T[:C]

    in_specs = [pl.BlockSpec((1, tr, C), lambda i, idx: (idx[0], i, 0)), pl.BlockSpec((3, tr, C), lambda i, idx: (0, i, 0))]
    x, y, c = _me()
    idx = jnp.stack([2 * x + y, c]).astype(jnp.int32)
    if transposed:
        grid_spec = pltpu.PrefetchScalarGridSpec(num_scalar_prefetch=1, grid=(hr // tr,), in_specs=in_specs,
                                                 out_specs=pl.BlockSpec((C, tr), lambda i, idx: (0, idx[1] * (hr // tr) + i)),
                                                 scratch_shapes=[pltpu.VMEM((tr, c_pad), f32)])
        return pl.pallas_call(body_t, name=name, grid_spec=grid_spec, out_shape=jax.ShapeDtypeStruct((C, 2 * hr), f32),
                              compiler_params=_params(("parallel",)))(idx, p, got)
    grid_spec = pltpu.PrefetchScalarGridSpec(num_scalar_prefetch=1, grid=(hr // tr,), in_specs=in_specs,
                                             out_specs=pl.BlockSpec((1, tr, C), lambda i, idx: (idx[1], i, 0)))
    return pl.pallas_call(body, name=name, grid_spec=grid_spec, out_shape=jax.ShapeDtypeStruct((2, hr, C), f32),
                          compiler_params=_params(("parallel",)))(idx, p, got)


def _adamw(w, g, m, v, name, tr):
    _, R, C = w.shape
    rs = min(8, tr)

    def body(w_ref, g_ref, m_ref, v_ref, d_ref, nm_ref, nv_ref):
        def strip(s, carry):
            rows = pl.ds(pl.multiple_of(s * rs, rs), rs)
            gv = g_ref[0, rows, :]
            mn = ADAM_B1 * m_ref[0, rows, :] + (1.0 - ADAM_B1) * gv
            vn = ADAM_B2 * v_ref[0, rows, :] + (1.0 - ADAM_B2) * (gv * gv)
            m_hat = mn / (1.0 - ADAM_B1 ** ADAM_STEP)
            v_hat = vn / (1.0 - ADAM_B2 ** ADAM_STEP)
            d_ref[0, rows, :] = -ADAM_LR * (m_hat / (jnp.sqrt(v_hat) + ADAM_EPS) + ADAM_WD * w_ref[0, rows, :])
            nm_ref[0, rows, :] = mn
            nv_ref[0, rows, :] = vn
            return carry

        if R % tr == 0:
            lax.fori_loop(0, tr // rs, strip, 0, unroll=min(2, tr // rs))
        else:
            lax.fori_loop(0, jnp.minimum(tr, R - pl.program_id(0) * tr) // rs, strip, 0)

    blk, grid = pl.BlockSpec((1, tr, C), lambda i: (0, i, 0)), (-(-R // tr),)
    o = jax.ShapeDtypeStruct((1, R, C), f32)
    return pl.pallas_call(body, name=name, grid=grid, in_specs=[blk] * 4, out_specs=[blk] * 3, out_shape=[o, o, o],
                          compiler_params=_params(("parallel",)))(w, g, m, v)


def _sum8(t):
    n = t.shape[1]

    def body(t_ref, o_ref):
        acc = t_ref[pl.ds(0, 1), :]
        for r in range(1, 8):
            acc = acc + t_ref[pl.ds(r, 1), :]
        o_ref[...] = acc

    return pl.pallas_call(body, name="sum_devices", out_shape=jax.ShapeDtypeStruct((1, n), f32))(t)


def _reduce_swap_start(g_main, g_dt, g_b):
    hr = g_main.shape[0]
    return _sibling_swap_start(g_main, g_dt, g_b.reshape(N_SHARD, 2, hr, g_b.shape[-1]))


def _reduce_start(swap_state, g_main, g_dt):
    hr = g_main.shape[0]
    send_sems, recv_sems, srcs, lands, _ = swap_state
    srcs, (r_main, r_dt, r_b) = _sibling_swap_wait(send_sems, recv_sems, srcs, lands, g_dt)
    p_main = _pair_sum(g_main[None], r_main[None], "pair_sum_main", SUM_ROWS // 4)
    p_dt = _pair_sum(g_dt[None], r_dt[None], "pair_sum_dt", SUM_ROWS)
    p_b = _pair_sum(srcs[2], r_b, "pair_sum_b", SUM_ROWS)
    p_in = jnp.transpose(_w_in_grad_full(p_main[0], p_dt[0]).reshape(hr, N_SHARD, W_IN_SHARD), (1, 0, 2))
    return _chip_exchange_start([p_in, p_b])


def _reduce_finish(state, after):
    send_sems, recv_sems, srcs, lands, _ = state
    (p_in, p_b), (got_in, got_b) = _chip_exchange_wait(send_sems, recv_sems, srcs, lands, after)
    mine_in, mine_b = _own_sum(p_in, got_in, "own_sum_in", transposed=True), _own_sum(p_b, got_b, "own_sum_b")
    full_in_t, full_b = _share_halves([mine_in, mine_b], [True, False], "share_halves")
    return full_in_t, full_b.reshape(-1, full_b.shape[-1])


def kernel(x, positions, norm1_w, w_in, conv_w, conv_b, dt_bias, a_log, d_skip, ssd_norm_w, w_br_ret, w_br_ssd, w_out, norm_f_w, loss_target, m_norm1_w, m_w_in, m_conv_w, m_conv_b, m_dt_bias, m_a_log, m_d_skip, m_ssd_norm_w, m_w_br_ret, m_w_br_ssd, m_w_out, m_norm_f_w, v_norm1_w, v_w_in, v_conv_w, v_conv_b, v_dt_bias, v_a_log, v_d_skip, v_ssd_norm_w, v_w_br_ret, v_w_br_ssd, v_w_out, v_norm_f_w):
    D = D_MODEL
    xi, yi, ci = _me()
    k = 2 * xi + yi
    me = 2 * k + ci
    weights = dict(norm1_w=norm1_w, w_in=w_in, conv_w=conv_w, conv_b=conv_b, dt_bias=dt_bias, a_log=a_log, d_skip=d_skip,
                   ssd_norm_w=ssd_norm_w, w_br_ret=w_br_ret, w_br_ssd=w_br_ssd, w_out=w_out, norm_f_w=norm_f_w)
    mom1 = dict(norm1_w=m_norm1_w, w_in=m_w_in, conv_w=m_conv_w, conv_b=m_conv_b, dt_bias=m_dt_bias, a_log=m_a_log, d_skip=m_d_skip,
                ssd_norm_w=m_ssd_norm_w, w_br_ret=m_w_br_ret, w_br_ssd=m_w_br_ssd, w_out=m_w_out, norm_f_w=m_norm_f_w)
    mom2 = dict(norm1_w=v_norm1_w, w_in=v_w_in, conv_w=v_conv_w, conv_b=v_conv_b, dt_bias=v_dt_bias, a_log=v_a_log, d_skip=v_d_skip,
                ssd_norm_w=v_ssd_norm_w, w_br_ret=v_w_br_ret, w_br_ssd=v_w_br_ssd, w_out=v_w_out, norm_f_w=v_norm_f_w)

    a_sh = w_in[0].astype(bf16)
    b_sh = jnp.concatenate([w_br_ret[0], w_br_ssd[0], w_out[0]], axis=0).astype(bf16)
    ga, gc = _gather_weights(a_sh, conv_w[0])
    ga, b_late = lax.optimization_barrier((ga, b_sh))
    late_send, late_recv, late_src, late_land, late_token = _gather_late_start(b_late)
    own = lambda g, s: lax.dynamic_update_slice_in_dim(g, s[None], k, axis=0)
    ga, gc = own(ga, a_sh), own(gc, conv_w[0])
    w_main, w_dt = _w_main_from_shards(ga)
    conv_full = jnp.transpose(gc, (1, 0, 2)).reshape(SSD_CONV, CONV_DIM)

    def late_weights(after):
        gb = own(_gather_late_wait(late_send, late_recv, late_src, late_land, after), b_sh)
        return gb[:, 0:512].reshape(2048, D), gb[:, 512:1536].reshape(4096, D), gb[:, 1536:2048].reshape(2048, D)

    class Reducer:
        @staticmethod
        def halves():
            return (1 - ci).reshape(1).astype(jnp.int32), ci.reshape(1).astype(jnp.int32)

        @staticmethod
        def first(g_main, g_dt, g_w_br, g_w_bs, g_w_o):
            g_b = jnp.concatenate([g_w_br.reshape(N_SHARD, 512, D), g_w_bs.reshape(N_SHARD, 1024, D),
                                   g_w_o.reshape(N_SHARD, 512, D)], axis=1)
            return _reduce_swap_start(g_main, g_dt, g_b)

        second = staticmethod(_reduce_start)

    loss, grad_x, g, reduce_state = _local_step(x[0], positions[0], loss_target[0], norm1_w, w_main, w_dt, conv_full, conv_b, dt_bias,
                                                a_log, d_skip, ssd_norm_w, late_token, late_weights, norm_f_w, Reducer)

    grad_w_in_t, full_b = _reduce_finish(reduce_state, g["norm1_w"])
    grad_mats = dict(w_br_ret=full_b[0:512], w_br_ssd=full_b[512:1536], w_out=full_b[1536:2048])

    small = [(n, weights[n].size) for n in ("norm1_w", "conv_b", "dt_bias", "a_log", "d_skip", "ssd_norm_w", "norm_f_w")]
    parts = [jnp.pad(loss.reshape(1, 1), ((0, 0), (0, 127)))] + [g[n].reshape(1, -1) for n, _ in small] + [g["conv_w"].reshape(1, -1)]
    vec = jnp.concatenate(parts, axis=1)
    nv = vec.shape[1]
    nvp = -(-nv // 128) * 128
    vec = jnp.pad(vec, ((0, 0), (0, nvp - nv)))
    total = _sum8(lax.dynamic_update_slice_in_dim(_gather_vec(vec), vec, me, axis=0))
    loss_out = total[0, 0]
    off = 128
    grad_small = {}
    for n, sz in small:
        grad_small[n] = total[:, off:off + sz]
        off += sz
    g_conv = total[:, off:off + SSD_CONV * CONV_DIM].reshape(SSD_CONV, CONV_DIM)
    g_conv = lax.dynamic_slice_in_dim(g_conv, k * (CONV_DIM // N_SHARD), CONV_DIM // N_SHARD, axis=1)
    grad_small["conv_w"] = g_conv.reshape(1, -1)

    upd = {}
    for n in ("w_br_ret", "w_br_ssd", "w_out"):
        upd[n] = _adamw(weights[n], grad_mats[n][None], mom1[n], mom2[n], "adamw_" + n, tr=SUM_ROWS)
    tp = lambda t: jnp.swapaxes(t, 1, 2)
    upd["w_in"] = tuple(tp(t) for t in _adamw(tp(w_in), grad_w_in_t[None], tp(m_w_in), tp(v_w_in), "adamw_w_in", tr=256))
    grad_mats["w_in"] = tp(grad_w_in_t[None])
    names_small = [n for n, _ in small] + ["conv_w"]
    flat = lambda d: jnp.concatenate([d[n].reshape(1, -1) for n in names_small], axis=1)
    ns = sum(weights[n].size for n in names_small)
    nsp = -(-ns // 128) * 128
    padv = lambda t: jnp.pad(t, ((0, 0), (0, nsp - ns)))
    small_upd = _adamw(padv(flat(weights))[None], padv(flat(grad_small))[None], padv(flat(mom1))[None],
                       jnp.pad(flat(mom2), ((0, 0), (0, nsp - ns)), constant_values=1.0)[None], "adamw_small", 1)
    off = 0
    for n in names_small:
        sz = weights[n].size
        upd[n] = tuple(t[0, :, off:off + sz] for t in small_upd)
        off += sz

    order = ["norm1_w", "w_in", "conv_w", "conv_b", "dt_bias", "a_log", "d_skip", "ssd_norm_w", "w_br_ret", "w_br_ssd", "w_out", "norm_f_w"]
    grads_out = {**grad_mats, **grad_small}
    shp = lambda n, t: t.reshape(weights[n].shape)
    return (loss_out, grad_x[None], *[shp(n, grads_out[n]) for n in order], *[shp(n, upd[n][0]) for n in order],
            *[shp(n, upd[n][1]) for n in order], *[shp(n, upd[n][2]) for n in order])
```

```python
import jax
import jax.numpy as jnp
import numpy as np
from jax import lax
from jax.experimental import pallas as pl
from jax.experimental.pallas import tpu as pltpu

f32 = jnp.float32
bf16 = jnp.bfloat16
HIGHEST = lax.Precision.HIGHEST
MESH = pl.DeviceIdType.MESH

D_MODEL = 2048
EPS = 1e-6
CHUNK = 64
RET_HEADS = 8
RET_DK = 256
RET_HW = 4 * RET_DK
RET_HP = 2
RET_UNROLL_FWD, RET_UNROLL_BWD = 4, 8
ROPE_THETA = 10000.0
SSD_WIDTH = 4096
SSD_GROUPS = 8
SSD_STATE = 128
SSD_GW = 512
SSD_GC = SSD_GW + 2 * SSD_STATE
SSD_HPG = 8
SSD_CONV = 4
CONV_DIM = 6144
SSD_HEADS = 64
LS = 128

C_RET, C_Z, C_GATES, C_XBC = 0, 8192, 12288, 16384
N_MAIN = 22528
DT_OFF = 18432
IN_PROJ = 22592
N_SHARD = 4
W_IN_SHARD = IN_PROJ // N_SHARD

ADAM_LR, ADAM_B1, ADAM_B2, ADAM_EPS, ADAM_WD, ADAM_STEP = 0.001, 0.9, 0.999, 1e-08, 0.01, 10

VMEM_LIMIT = 56 * 1024 * 1024
SUM_ROWS = 128
ANY = pl.BlockSpec(memory_space=pl.ANY)


def _params(dims):
    return pltpu.CompilerParams(dimension_semantics=dims, vmem_limit_bytes=VMEM_LIMIT)


def _silu(x):
    return x * jax.nn.sigmoid(x)


def _nt(a, b):
    return lax.dot_general(a, b, (((1,), (1,)), ((), ())), preferred_element_type=f32)


def _tn(a, b):
    return lax.dot_general(a, b, (((0,), (0,)), ((), ())), preferred_element_type=f32)


def _nn(a, b):
    return jnp.dot(a, b, preferred_element_type=f32)


def _hi(a, b):
    return jnp.dot(a, b, precision=HIGHEST, preferred_element_type=f32)


def _split(a):
    hi = a.astype(bf16)
    return hi, (a - hi.astype(f32)).astype(bf16)


def _sel_r(a, sel):
    hi, lo = _split(a)
    return _nn(hi, sel) + _nn(lo, sel)


def _sel_l(sel, a):
    hi, lo = _split(a)
    return _nn(sel, hi) + _nn(sel, lo)


def _rows_to_cols(t, eye):
    hi = t.astype(bf16)
    r1 = t - hi.astype(f32)
    mid = r1.astype(bf16)
    lo = (r1 - mid.astype(f32)).astype(bf16)
    return _nt(eye, hi) + _nt(eye, mid) + _nt(eye, lo)


def _xbc_group_major(t):
    R = t.shape[0]
    nb = SSD_GROUPS * SSD_STATE
    parts = [t[:, :SSD_WIDTH].reshape(R, SSD_GROUPS, SSD_GW), t[:, SSD_WIDTH:SSD_WIDTH + nb].reshape(R, SSD_GROUPS, SSD_STATE),
             t[:, SSD_WIDTH + nb:].reshape(R, SSD_GROUPS, SSD_STATE)]
    return jnp.concatenate(parts, axis=2).reshape(R, CONV_DIM)


def _xbc_original(t):
    R = t.shape[0]
    g = t.reshape(R, SSD_GROUPS, SSD_GC)
    parts = [g[:, :, :SSD_GW].reshape(R, SSD_WIDTH), g[:, :, SSD_GW:SSD_GW + SSD_STATE].reshape(R, SSD_GROUPS * SSD_STATE),
             g[:, :, SSD_GW + SSD_STATE:].reshape(R, SSD_GROUPS * SSD_STATE)]
    return jnp.concatenate(parts, axis=1)


def _main_segments():
    segs = []
    for h in range(RET_HEADS):
        segs += [(base + RET_DK * h, RET_DK) for base in (0, 2048, 4096, 6144)]
    segs += [(8192, SSD_WIDTH), (DT_OFF + SSD_HEADS, 2 * D_MODEL)]
    nb = SSD_GROUPS * SSD_STATE
    for g in range(SSD_GROUPS):
        segs += [(12288 + SSD_GW * g, SSD_GW), (12288 + SSD_WIDTH + SSD_STATE * g, SSD_STATE),
                 (12288 + SSD_WIDTH + nb + SSD_STATE * g, SSD_STATE)]
    return segs


def _w_main_from_shards(shards):
    def cols(lo, hi):
        out = []
        while lo < hi:
            s = lo // W_IN_SHARD
            top = min(hi, (s + 1) * W_IN_SHARD)
            out.append(shards[s][:, lo - s * W_IN_SHARD:top - s * W_IN_SHARD])
            lo = top
        return out

    main = jnp.concatenate([p for s, n in _main_segments() for p in cols(s, s + n)], axis=1)
    w_dt = jnp.pad(jnp.concatenate(cols(DT_OFF, DT_OFF + SSD_HEADS), axis=1), ((0, 0), (0, 128 - SSD_HEADS)))
    return main, w_dt


def _w_in_grad_full(g_main, g_dt):
    D = g_main.shape[0]
    ret = jnp.transpose(g_main[:, :C_Z].reshape(D, RET_HEADS, 4, RET_DK), (0, 2, 1, 3)).reshape(D, C_Z)
    return jnp.concatenate([ret, g_main[:, C_Z:C_GATES], _xbc_original(g_main[:, C_XBC:]), g_dt[:, :SSD_HEADS],
                            g_main[:, C_GATES:C_XBC]], axis=1)


def _mm(pairs, M, N, *, tm, tn, out_dtype, name, tb=False, row_off=None):
    P = len(pairs)
    nks = [K // tk for (_, _, _, _, _, K, tk) in pairs]
    starts = [int(s) for s in np.cumsum([0] + nks[:-1])]
    KT = int(sum(nks))
    npf = 0 if row_off is None else 1
    in_specs, args = [], []
    for (a, a_cb, b, b_kb, b_nb, K, tk), s, nk in zip(pairs, starts, nks):
        def kk(k, s=s, nk=nk):
            return jnp.clip(k - s, 0, nk - 1)
        in_specs.append(pl.BlockSpec((tm, tk), lambda m, n, k, *pf, kk=kk, a_cb=a_cb: (m + (pf[0][0] if pf else 0), a_cb + kk(k))))
        if tb:
            in_specs.append(pl.BlockSpec((tn, tk), lambda m, n, k, *pf, kk=kk, b_kb=b_kb, b_nb=b_nb: (b_nb + n, b_kb + kk(k))))
        else:
            in_specs.append(pl.BlockSpec((tk, tn), lambda m, n, k, *pf, kk=kk, b_kb=b_kb, b_nb=b_nb: (b_kb + kk(k), b_nb + n)))
        args += [a, b]

    def body(*refs):
        refs = refs[npf:]
        o_ref = refs[2 * P]
        k = pl.program_id(2)

        def prod(i):
            a = refs[2 * i][...].astype(bf16)
            b = refs[2 * i + 1][...].astype(bf16)
            return _nt(a, b) if tb else _nn(a, b)

        if KT == 1:
            o_ref[...] = prod(0).astype(out_dtype)
            return
        acc = refs[2 * P + 1]

        @pl.when(k == 0)
        def _():
            acc[...] = jnp.zeros_like(acc)

        for i in range(P):
            @pl.when((k >= starts[i]) & (k < starts[i] + nks[i]))
            def _(i=i):
                acc[...] += prod(i)

        @pl.when(k == KT - 1)
        def _():
            o_ref[...] = acc[...].astype(out_dtype)

    grid_spec = pltpu.PrefetchScalarGridSpec(
        num_scalar_prefetch=npf, grid=(M // tm, N // tn, KT), in_specs=in_specs,
        out_specs=pl.BlockSpec((tm, tn), lambda m, n, k, *pf: (m, n)),
        scratch_shapes=[] if KT == 1 else [pltpu.VMEM((tm, tn), f32)])
    return pl.pallas_call(
        body, name=name, grid_spec=grid_spec, out_shape=jax.ShapeDtypeStruct((M, N), out_dtype),
        compiler_params=_params(("parallel", "parallel", "arbitrary")),
    )(*([] if row_off is None else [row_off]), *args)


def _mm1(a, b, *, tm, tn, tk, out_dtype, name, tb=False):
    M, K = a.shape
    N = b.shape[0] if tb else b.shape[1]
    return _mm([(a, 0, b, 0, 0, K, tk)], M, N, tm=tm, tn=tn, out_dtype=out_dtype, name=name, tb=tb)


RS = 16
CS = 32


def _for_strips(n_rows, fn, rs=RS, unroll=4):
    def step(s, carry):
        fn(pl.ds(pl.multiple_of(s * rs, rs), rs))
        return carry
    n = n_rows // rs
    lax.fori_loop(0, n, step, 0, unroll=min(unroll, n))


def _norm1_fwd(x, w, tr):
    S, D = x.shape

    def body(x_ref, w_ref, h_ref, ht_ref):
        def strip(rows):
            xv = x_ref[rows, :]
            r = lax.rsqrt(jnp.mean(xv * xv, axis=-1, keepdims=True) + EPS)
            h_ref[rows, :] = (xv * r * w_ref[...]).astype(bf16)
        _for_strips(tr, strip)
        ht_ref[...] = h_ref[...].T

    return pl.pallas_call(
        body, name="norm1_fwd", grid=(S // tr,),
        in_specs=[pl.BlockSpec((tr, D), lambda i: (i, 0)), pl.BlockSpec((1, D), lambda i: (0, 0))],
        out_specs=[pl.BlockSpec((tr, D), lambda i: (i, 0)), pl.BlockSpec((D, tr), lambda i: (0, i))],
        out_shape=[jax.ShapeDtypeStruct((S, D), bf16), jax.ShapeDtypeStruct((D, S), bf16)], compiler_params=_params(("parallel",)),
    )(x, w)


def _norm1_bwd(x, w, dh, dx2, tr):
    S, D = x.shape

    def body(x_ref, w_ref, dh_ref, dx2_ref, gx_ref, gw_ref, acc):
        @pl.when(pl.program_id(0) == 0)
        def _():
            acc[...] = jnp.zeros_like(acc)

        def strip(rows):
            xv = x_ref[rows, :]
            r = lax.rsqrt(jnp.mean(xv * xv, axis=-1, keepdims=True) + EPS)
            xh = xv * r
            dhv = dh_ref[rows, :]
            acc[...] += dhv * xh
            dxh = dhv * w_ref[...]
            gx_ref[rows, :] = dx2_ref[rows, :] + r * (dxh - xh * jnp.mean(dxh * xh, axis=-1, keepdims=True))
        _for_strips(tr, strip)

        @pl.when(pl.program_id(0) == S // tr - 1)
        def _():
            gw_ref[...] = jnp.sum(acc[...], axis=0, keepdims=True)

    row = pl.BlockSpec((tr, D), lambda i: (i, 0))
    vec = pl.BlockSpec((1, D), lambda i: (0, 0))
    return pl.pallas_call(
        body, name="norm1_bwd", grid=(S // tr,), in_specs=[row, vec, row, row], out_specs=[row, vec],
        out_shape=[jax.ShapeDtypeStruct((S, D), f32), jax.ShapeDtypeStruct((1, D), f32)],
        scratch_shapes=[pltpu.VMEM((RS, D), f32)], compiler_params=_params(("arbitrary",)),
    )(x, w, dh, dx2)


def _final_fwd_bwd(x, mo, target, wf, tr):
    S, D = x.shape

    def body(x_ref, mo_ref, t_ref, w_ref, dx2_ref, dx2b_ref, loss_ref, gw_ref, acc, lacc):
        @pl.when(pl.program_id(0) == 0)
        def _():
            acc[...] = jnp.zeros_like(acc)
            lacc[...] = jnp.zeros_like(lacc)

        def strip(rows):
            x2 = x_ref[rows, :] + mo_ref[rows, :]
            r = lax.rsqrt(jnp.mean(x2 * x2, axis=-1, keepdims=True) + EPS)
            xh = x2 * r
            wv = w_ref[...]
            err = xh * wv - t_ref[rows, :]
            lacc[...] += jnp.mean(err * err, axis=-1, keepdims=True)
            dy = err * (1.0 / D)
            acc[...] += dy * xh
            dxh = dy * wv
            dx2 = r * (dxh - xh * jnp.mean(dxh * xh, axis=-1, keepdims=True))
            dx2_ref[rows, :] = dx2
            dx2b_ref[rows, :] = dx2.astype(bf16)
        _for_strips(tr, strip)

        @pl.when(pl.program_id(0) == S // tr - 1)
        def _():
            gw_ref[...] = jnp.sum(acc[...], axis=0, keepdims=True)
            loss_ref[...] = 0.5 * jnp.sum(lacc[...], axis=0, keepdims=True)

    row = pl.BlockSpec((tr, D), lambda i: (i, 0))
    vec = pl.BlockSpec((1, D), lambda i: (0, 0))
    return pl.pallas_call(
        body, name="final_norm_loss", grid=(S // tr,), in_specs=[row, row, row, vec],
        out_specs=[row, row, pl.BlockSpec((1, 1), lambda i: (0, 0)), vec],
        out_shape=[jax.ShapeDtypeStruct((S, D), f32), jax.ShapeDtypeStruct((S, D), bf16), jax.ShapeDtypeStruct((1, 1), f32),
                   jax.ShapeDtypeStruct((1, D), f32)],
        scratch_shapes=[pltpu.VMEM((RS, D), f32), pltpu.VMEM((RS, 1), f32)], compiler_params=_params(("arbitrary",)),
    )(x, mo, target, wf)


def _merge_fwd(p_r, p_s, proj, tr):
    S, D = p_r.shape

    def body(pr_ref, ps_ref, g_ref, o_ref, ot_ref):
        def strip(rows):
            gr, gs = g_ref[rows, pl.ds(0, D)].astype(f32), g_ref[rows, pl.ds(D, D)].astype(f32)
            o_ref[rows, :] = (jax.nn.sigmoid(gr) * pr_ref[rows, :] + jax.nn.sigmoid(gs) * ps_ref[rows, :]).astype(bf16)
        _for_strips(tr, strip)
        ot_ref[...] = o_ref[...].T

    row = pl.BlockSpec((tr, D), lambda i: (i, 0))
    return pl.pallas_call(
        body, name="merge_fwd", grid=(S // tr,),
        in_specs=[row, row, pl.BlockSpec((tr, 2 * D), lambda i: (i, C_GATES // (2 * D)))],
        out_specs=[row, pl.BlockSpec((D, tr), lambda i: (0, i))],
        out_shape=[jax.ShapeDtypeStruct((S, D), bf16), jax.ShapeDtypeStruct((D, S), bf16)], compiler_params=_params(("parallel",)),
    )(p_r, p_s, proj)


def _merge_bwd(dm, p_r, p_s, proj, tr):
    S, D = p_r.shape

    def body(dm_ref, pr_ref, ps_ref, g_ref, dpr_ref, dps_ref, dproj_ref):
        def strip(rows):
            dmv = dm_ref[rows, :]
            sr = jax.nn.sigmoid(g_ref[rows, pl.ds(0, D)].astype(f32))
            ss = jax.nn.sigmoid(g_ref[rows, pl.ds(D, D)].astype(f32))
            dpr_ref[rows, :] = (dmv * sr).astype(bf16)
            dps_ref[rows, :] = (dmv * ss).astype(bf16)
            dproj_ref[rows, pl.ds(0, D)] = (dmv * pr_ref[rows, :] * sr * (1.0 - sr)).astype(bf16)
            dproj_ref[rows, pl.ds(D, D)] = (dmv * ps_ref[rows, :] * ss * (1.0 - ss)).astype(bf16)
        _for_strips(tr, strip)

    row = pl.BlockSpec((tr, D), lambda i: (i, 0))
    gates = pl.BlockSpec((tr, 2 * D), lambda i: (i, C_GATES // (2 * D)))
    o = jax.ShapeDtypeStruct((S, D), bf16)
    return pl.pallas_call(
        body, name="merge_bwd", grid=(S // tr,), in_specs=[row, row, row, gates],
        out_specs=[row, row, gates], out_shape=[o, o, jax.ShapeDtypeStruct((S, N_MAIN), bf16)],
        compiler_params=_params(("parallel",)),
    )(dm, p_r, p_s, proj)


def _ssd_norm_fwd(y, proj, w, tr):
    S, W = y.shape

    def body(y_ref, z_ref, w_ref, o_ref, ot_ref):
        def strip(rows):
            u = y_ref[rows, :] * _silu(z_ref[rows, :].astype(f32))
            r = lax.rsqrt(jnp.mean(u * u, axis=-1, keepdims=True) + EPS)
            o_ref[rows, :] = (u * r * w_ref[...]).astype(bf16)
        _for_strips(tr, strip)
        ot_ref[...] = o_ref[...].T

    row = pl.BlockSpec((tr, W), lambda i: (i, 0))
    return pl.pallas_call(
        body, name="ssd_norm_fwd", grid=(S // tr,),
        in_specs=[row, pl.BlockSpec((tr, W), lambda i: (i, C_Z // W)), pl.BlockSpec((1, W), lambda i: (0, 0))],
        out_specs=[row, pl.BlockSpec((W, tr), lambda i: (0, i))],
        out_shape=[jax.ShapeDtypeStruct((S, W), bf16), jax.ShapeDtypeStruct((W, S), bf16)], compiler_params=_params(("parallel",)),
    )(y, proj, w)


def _ssd_norm_bwd(y, proj, w, dys, dproj, tr):
    S, W = y.shape

    def body(y_ref, z_ref, w_ref, d_ref, _, dy_ref, dz_ref, gw_ref, acc):
        @pl.when(pl.program_id(0) == 0)
        def _():
            acc[...] = jnp.zeros_like(acc)

        def strip(rows):
            yv, zv, dv = y_ref[rows, :], z_ref[rows, :].astype(f32), d_ref[rows, :]
            sg = jax.nn.sigmoid(zv)
            sz = zv * sg
            u = yv * sz
            r = lax.rsqrt(jnp.mean(u * u, axis=-1, keepdims=True) + EPS)
            un = u * r
            acc[...] += dv * un
            dun = dv * w_ref[...]
            du = r * (dun - un * jnp.mean(dun * un, axis=-1, keepdims=True))
            dy_ref[rows, :] = (du * sz).astype(bf16)
            dz_ref[rows, :] = (du * yv * (sg * (1.0 + zv * (1.0 - sg)))).astype(bf16)
        _for_strips(tr, strip)

        @pl.when(pl.program_id(0) == S // tr - 1)
        def _():
            gw_ref[...] = jnp.sum(acc[...], axis=0, keepdims=True)

    row = pl.BlockSpec((tr, W), lambda i: (i, 0))
    zcol = pl.BlockSpec((tr, W), lambda i: (i, C_Z // W))
    vec = pl.BlockSpec((1, W), lambda i: (0, 0))
    return pl.pallas_call(
        body, name="ssd_norm_bwd", grid=(S // tr,),
        in_specs=[row, zcol, vec, row, ANY], out_specs=[row, zcol, vec],
        out_shape=[jax.ShapeDtypeStruct((S, W), bf16), jax.ShapeDtypeStruct(dproj.shape, bf16), jax.ShapeDtypeStruct((1, W), f32)],
        input_output_aliases={4: 1}, scratch_shapes=[pltpu.VMEM((RS, W), f32)], compiler_params=_params(("arbitrary",)),
    )(y, proj, w, dys, dproj)


def _rope(t, cos, sin):
    t1, t2 = t[:, :128], t[:, 128:]
    return jnp.concatenate([t1 * cos - t2 * sin, t2 * cos + t1 * sin], axis=1)


def _rope_t(d, cos, sin):
    d1, d2 = d[:, :128], d[:, 128:]
    return jnp.concatenate([d1 * cos + d2 * sin, d2 * cos - d1 * sin], axis=1)


def _ret_specs(tb, rev_nb=None):
    def blk(i):
        return i if rev_nb is None else rev_nb - 1 - i
    head = pl.BlockSpec((tb, RET_HP * RET_HW), lambda h, i: (blk(i), h))
    tab = pl.BlockSpec((tb, 128), lambda h, i: (blk(i), 0))
    mat = pl.BlockSpec((RET_HP, CHUNK, CHUNK), lambda h, i: (h, 0, 0))
    vec = pl.BlockSpec((RET_HP, CHUNK, 1), lambda h, i: (h, 0, 0))
    one = pl.BlockSpec((RET_HP, 1, 1), lambda h, i: (h, 0, 0))
    own = pl.BlockSpec((tb, RET_HP * RET_DK), lambda h, i: (blk(i), h))
    st = pl.BlockSpec((RET_HP, tb // CHUNK, RET_DK, RET_DK), lambda h, i: (h, blk(i), 0, 0))
    return head, tab, mat, vec, one, own, st


def _ret_fwd(proj, cos, sin, intra, qdec, kdec, cdec, tb):
    S = proj.shape[0]
    nc = S // CHUNK
    scale = RET_DK ** -0.5
    dk = RET_DK

    def body(p_ref, cos_ref, sin_ref, m_ref, qd_ref, kd_ref, cd_ref, y_ref, yr_ref, yrt_ref, st_ref, st):
        @pl.when(pl.program_id(1) == 0)
        def _():
            st[...] = jnp.zeros_like(st)

        def head_chunk(hh, c, rows, cs, sn):
            mm, qd, kd, cd = m_ref[hh], qd_ref[hh], kd_ref[hh], cd_ref[hh]
            col = lambda j: pl.ds(hh * RET_HW + j * dk, dk)
            own = pl.ds(hh * dk, dk)
            qr = _rope(p_ref[rows, col(0)].astype(f32), cs, sn)
            kr = _rope(p_ref[rows, col(1)].astype(f32), cs, sn) * scale
            qb, kb, vb = qr.astype(bf16), kr.astype(bf16), p_ref[rows, col(2)].astype(bf16)
            stb = st[hh].astype(bf16)
            st_ref[hh, c] = stb
            sc = (_nt(qb, kb) * mm).astype(bf16)
            y = _nn(sc, vb) + _nn(qb, stb) * qd
            st[hh] = st[hh] * cd + _tn((kr * kd).astype(bf16), vb)
            y_ref[rows, own] = y
            mu = jnp.mean(y, axis=-1, keepdims=True)
            yc = y - mu
            var = jnp.mean(yc * yc, axis=-1, keepdims=True)
            yr_ref[rows, own] = (yc * lax.rsqrt(var + EPS) * _silu(p_ref[rows, col(3)].astype(f32))).astype(bf16)

        def chunk(c, carry):
            rows = pl.ds(pl.multiple_of(c * CHUNK, CHUNK), CHUNK)
            cs, sn = cos_ref[rows, :], sin_ref[rows, :]
            for hh in range(RET_HP):
                head_chunk(hh, c, rows, cs, sn)
            return carry

        lax.fori_loop(0, tb // CHUNK, chunk, 0, unroll=min(RET_UNROLL_FWD, tb // CHUNK))
        yrt_ref[...] = yr_ref[...].T

    head, tab, mat, vec, one, own, stspec = _ret_specs(tb)
    return pl.pallas_call(
        body, name="ret_fwd", grid=(RET_HEADS // RET_HP, S // tb),
        in_specs=[head, tab, tab, mat, vec, vec, one],
        out_specs=[own, own, pl.BlockSpec((RET_HP * RET_DK, tb), lambda h, i: (h, i)), stspec],
        out_shape=[jax.ShapeDtypeStruct((S, 2048), f32), jax.ShapeDtypeStruct((S, 2048), bf16), jax.ShapeDtypeStruct((2048, S), bf16),
                   jax.ShapeDtypeStruct((RET_HEADS, nc, dk, dk), bf16)],
        scratch_shapes=[pltpu.VMEM((RET_HP, dk, dk), f32)], compiler_params=_params(("parallel", "arbitrary")),
    )(proj, cos, sin, intra, qdec, kdec, cdec)


def _ret_bwd(proj, cos, sin, intra, qdec, kdec, cdec, y, dyr, states, dproj, tb):
    S = proj.shape[0]
    nb = S // tb
    nck = tb // CHUNK
    scale = RET_DK ** -0.5
    dk = RET_DK

    def body(p_ref, cos_ref, sin_ref, m_ref, qd_ref, kd_ref, cd_ref, y_ref, dyr_ref, st_ref, _, o_ref, dst):
        @pl.when(pl.program_id(1) == 0)
        def _():
            dst[...] = jnp.zeros_like(dst)

        def head_chunk(hh, c, rows, cs, sn):
            mm, qd, kd, cd = m_ref[hh], qd_ref[hh], kd_ref[hh], cd_ref[hh]
            col = lambda j: pl.ds(hh * RET_HW + j * dk, dk)
            own = pl.ds(hh * dk, dk)
            qr = _rope(p_ref[rows, col(0)].astype(f32), cs, sn)
            kr = _rope(p_ref[rows, col(1)].astype(f32), cs, sn) * scale
            qb, kb, vb = qr.astype(bf16), kr.astype(bf16), p_ref[rows, col(2)].astype(bf16)
            kdb = (kr * kd).astype(bf16)
            stb = st_ref[hh, c]
            yv, gv, dyrv = y_ref[rows, own], p_ref[rows, col(3)].astype(f32), dyr_ref[rows, own]
            mu = jnp.mean(yv, axis=-1, keepdims=True)
            yc = yv - mu
            rstd = lax.rsqrt(jnp.mean(yc * yc, axis=-1, keepdims=True) + EPS)
            yn = yc * rstd
            sg = jax.nn.sigmoid(gv)
            o_ref[rows, col(3)] = (dyrv * yn * (sg * (1.0 + gv * (1.0 - sg)))).astype(bf16)
            dyn = dyrv * (gv * sg)
            dy = rstd * (dyn - jnp.mean(dyn, axis=-1, keepdims=True) - yn * jnp.mean(dyn * yn, axis=-1, keepdims=True))
            dyb = dy.astype(bf16)
            dyqb = (dy * qd).astype(bf16)
            dstb = dst[hh].astype(bf16)
            sct =(_nt(kb, qb) * mm).astype(bf16)
            ds = (_nt(dyb, vb) * mm).astype(bf16)
            dsT = (_nt(vb, dyb) * mm).astype(bf16)
            dv = _nn(sct, dyb) + _nn(kdb, dstb)
            dqr = _nn(ds, kb) + _nt(dyqb, stb)
            dkr = _nn(dsT, qb) + _nt(vb, dstb) * kd
            dst[hh] = dst[hh] * cd + _tn(qb, dyqb)
            o_ref[rows, col(0)] = _rope_t(dqr, cs, sn).astype(bf16)
            o_ref[rows, col(1)] = (_rope_t(dkr, cs, sn) * scale).astype(bf16)
            o_ref[rows, col(2)] = dv.astype(bf16)

        def chunk(cc, carry):
            c = nck - 1 - cc
            rows = pl.ds(pl.multiple_of(c * CHUNK, CHUNK), CHUNK)
            cs, sn = cos_ref[rows, :], sin_ref[rows, :]
            for hh in range(RET_HP):
                head_chunk(hh, c, rows, cs, sn)
            return carry

        lax.fori_loop(0, nck, chunk, 0, unroll=min(RET_UNROLL_BWD, nck))

    head, tab, mat, vec, one, own, stspec = _ret_specs(tb, rev_nb=nb)
    return pl.pallas_call(
        body, name="ret_bwd", grid=(RET_HEADS // RET_HP, nb),
        in_specs=[head, tab, tab, mat, vec, vec, one, own, own, stspec, ANY],
        out_specs=head, out_shape=jax.ShapeDtypeStruct(dproj.shape, bf16), input_output_aliases={10: 0},
        scratch_shapes=[pltpu.VMEM((RET_HP, dk, dk), f32)], compiler_params=_params(("parallel", "arbitrary")),
    )(proj, cos, sin, intra, qdec, kdec, cdec, y, dyr, states, dproj)


def _conv_fwd(proj, conv_w, conv_b, tb, cw):
    S = proj.shape[0]
    off = 0

    def body(x_ref, halo_ref, w_ref, b_ref, o_ref, xe):
        xe[pl.ds(0, 8), :] = jnp.where(pl.program_id(1) == 0, 0.0, halo_ref[...])
        xe[pl.ds(8, CS), :] = x_ref[pl.ds(0, CS), :]
        ws = [w_ref[pl.ds(j, 1), :] for j in range(SSD_CONV)]
        for s in range(tb // CS):
            tap = (lambda j: xe[pl.ds(5 + j, CS), :]) if s == 0 else (lambda j, s=s: x_ref[pl.ds(s * CS - 3 + j, CS), :])
            acc = b_ref[...] + ws[0] * tap(0)
            for j in range(1, SSD_CONV):
                acc = acc + ws[j] * tap(j)
            o_ref[pl.ds(s * CS, CS), :] = acc

    return pl.pallas_call(
        body, name="conv_fwd", grid=(CONV_DIM // cw, S // tb),
        in_specs=[pl.BlockSpec((tb, cw), lambda j, i: (i, off + j)),
                  pl.BlockSpec((8, cw), lambda j, i: (jnp.maximum(i * (tb // 8) - 1, 0), off + j)),
                  pl.BlockSpec((SSD_CONV, cw), lambda j, i: (0, j)), pl.BlockSpec((1, cw), lambda j, i: (0, j))],
        out_specs=pl.BlockSpec((tb, cw), lambda j, i: (i, j)),
        out_shape=jax.ShapeDtypeStruct((S, CONV_DIM), f32),
        scratch_shapes=[pltpu.VMEM((CS + 8, cw), f32)], compiler_params=_params(("parallel", "arbitrary")),
    )(proj, proj, conv_w, conv_b)


def _conv_bwd(dpre, proj, conv_w, dproj, tb, cw):
    S, n = dpre.shape
    nb = S // tb
    xoff = C_XBC // cw

    def body(d_ref, dh_ref, x_ref, xh_ref, w_ref, _, dx_ref, gw_ref, gb_ref, de, xe, accw, accb):
        i = pl.program_id(1)

        @pl.when(i == 0)
        def _():
            accw[...] = jnp.zeros_like(accw)
            accb[...] = jnp.zeros_like(accb)

        ns = tb // CS
        de[pl.ds(0, CS), :] = d_ref[pl.ds(tb - CS, CS), :]
        de[pl.ds(CS, 8), :] = jnp.where(i == nb - 1, 0.0, dh_ref[...])
        xe[pl.ds(0, 8), :] = jnp.where(i == 0, 0.0, xh_ref[...])
        xe[pl.ds(8, CS), :] = x_ref[pl.ds(0, CS), :]
        ws = [w_ref[pl.ds(j, 1), :] for j in range(SSD_CONV)]
        fold = lambda p: sum(p[8 * q:8 * (q + 1)] for q in range(1, CS // 8)) + p[0:8]
        for s in range(ns):
            dv = d_ref[pl.ds(s * CS, CS), :]
            ahead = (lambda o: de[pl.ds(o, CS), :]) if s == ns - 1 else (lambda o, s=s: d_ref[pl.ds(s * CS + o, CS), :])
            xtap = (lambda j: xe[pl.ds(5 + j, CS), :]) if s == 0 else (lambda j, s=s: x_ref[pl.ds(s * CS - 3 + j, CS), :])
            acc = ws[SSD_CONV - 1] * dv
            for j in range(SSD_CONV - 1):
                acc = acc + ws[j] * ahead(3 - j)
            dx_ref[pl.ds(s * CS, CS), :] = acc.astype(bf16)
            accb[...] += fold(dv)
            for j in range(SSD_CONV):
                accw[j] += fold(dv * xtap(j))

        @pl.when(i == nb - 1)
        def _():
            gb_ref[...] = jnp.sum(accb[...], axis=0, keepdims=True)
            for j in range(SSD_CONV):
                gw_ref[pl.ds(j, 1), :] = jnp.sum(accw[j], axis=0, keepdims=True)

    return pl.pallas_call(
        body, name="conv_bwd", grid=(n // cw, nb),
        in_specs=[pl.BlockSpec((tb, cw), lambda j, i: (i, j)),
                  pl.BlockSpec((8, cw), lambda j, i: (jnp.minimum((i + 1) * (tb // 8), S // 8 - 1), j)),
                  pl.BlockSpec((tb, cw), lambda j, i: (i, j)),
                  pl.BlockSpec((8, cw), lambda j, i: (jnp.maximum(i * (tb // 8) - 1, 0), j)),
                  pl.BlockSpec((SSD_CONV, cw), lambda j, i: (0, j)), ANY],
        out_specs=[pl.BlockSpec((tb, cw), lambda j, i: (i, xoff + j)), pl.BlockSpec((SSD_CONV, cw), lambda j, i: (0, j)),
                   pl.BlockSpec((1, cw), lambda j, i: (0, j))],
        out_shape=[jax.ShapeDtypeStruct(dproj.shape, bf16), jax.ShapeDtypeStruct((SSD_CONV, n), f32), jax.ShapeDtypeStruct((1, n), f32)],
        input_output_aliases={5: 0},
        scratch_shapes=[pltpu.VMEM((CS + 8, cw), f32), pltpu.VMEM((CS + 8, cw), f32), pltpu.VMEM((SSD_CONV, 8, cw), f32),
                        pltpu.VMEM((8, cw), f32)],
        compiler_params=_params(("parallel", "arbitrary")),
    )(dpre, dpre, proj, proj, conv_w, dproj)


def _dt_prep(dt_raw, dt_bias, a_log, tb):
    S = dt_raw.shape[0]

    def body(r_ref, b_ref, al_ref, dt_ref, sg_ref, ac_ref):
        li = lax.broadcasted_iota(jnp.int32, (LS, LS), 0)
        si = lax.broadcasted_iota(jnp.int32, (LS, LS), 1)
        tri = (li >= si).astype(f32)
        neg_a = -jnp.exp(al_ref[...])
        for c in range(tb // LS):
            rows = pl.ds(c * LS, LS)
            xv = r_ref[rows, :] + b_ref[...]
            dtv = jax.nn.softplus(xv)
            dt_ref[rows, :] = dtv
            sg_ref[rows, :] = jax.nn.sigmoid(xv)
            ac_ref[rows, :] = _hi(tri, dtv * neg_a)

    row = pl.BlockSpec((tb, 128), lambda i: (i, 0))
    vec = pl.BlockSpec((1, 128), lambda i: (0, 0))
    o = jax.ShapeDtypeStruct((S, 128), f32)
    return pl.pallas_call(body, name="dt_prep", grid=(S // tb,), in_specs=[row, vec, vec], out_specs=[row, row, row],
                          out_shape=[o, o, o], compiler_params=_params(("parallel",)))(dt_raw, dt_bias, a_log)


def _group_major(t):
    S = t.shape[0]
    return jnp.transpose(t[:, :SSD_HEADS].reshape(S, SSD_GROUPS, SSD_HPG), (1, 0, 2))


def _group_major_t(t):
    S = t.shape[0]
    return jnp.transpose(t[:, :SSD_HEADS].reshape(S // LS, LS, SSD_GROUPS, SSD_HPG), (2, 0, 3, 1))


def _ssd_specs(tb, rev_nb=None):
    def blk(i):
        return i if rev_nb is None else rev_nb - 1 - i
    grp = pl.BlockSpec((tb, SSD_GC), lambda g, i: (blk(i), g))
    xs = pl.BlockSpec((tb, SSD_GW), lambda g, i: (blk(i), g))
    ph = pl.BlockSpec((1, tb, SSD_HPG), lambda g, i: (g, blk(i), 0))
    pht = pl.BlockSpec((1, tb // LS, SSD_HPG, LS), lambda g, i: (g, blk(i), 0, 0))
    gvec = pl.BlockSpec((1, 1, SSD_GW), lambda g, i: (g, 0, 0))
    ex = pl.BlockSpec((SSD_HPG, SSD_GW), lambda g, i: (0, 0))
    st = pl.BlockSpec((1, tb // LS, SSD_STATE, SSD_GW), lambda g, i: (g, blk(i), 0, 0))
    return grp, xs, ph, pht, gvec, ex, st


def _expander():
    return jnp.repeat(jnp.eye(SSD_HPG, dtype=f32), SSD_GW // SSD_HPG, axis=1).astype(bf16)


def _expand3(dt8, ac8, ex):
    stack = jnp.concatenate([dt8, jnp.exp(ac8), jnp.exp(ac8[LS - 1:LS, :] - ac8)], axis=0)
    wide = _sel_r(stack, ex)
    return wide[0:LS], wide[LS:2 * LS], wide[2 * LS:3 * LS]


def _ssd_fwd(pre, dt_g, ac_g, act_g, dskx, tb):
    S = pre.shape[0]
    nc = S // LS
    hd = SSD_GW // SSD_HPG

    def body(p_ref, dt_ref, ac_ref, act_ref, dsk_ref, ex_ref, y_ref, st_ref, st):
        @pl.when(pl.program_id(1) == 0)
        def _():
            st[...] = jnp.zeros_like(st)

        ex = ex_ref[...]
        li = lax.broadcasted_iota(jnp.int32, (LS, LS), 0)
        si = lax.broadcasted_iota(jnp.int32, (LS, LS), 1)
        causal = li >= si

        def chunk(c, carry):
            rows = pl.ds(pl.multiple_of(c * LS, LS), LS)
            xs = _silu(p_ref[rows, pl.ds(0, SSD_GW)])
            bcb = _silu(p_ref[rows, pl.ds(SSD_GW, SSD_STATE)]).astype(bf16)
            ccb = _silu(p_ref[rows, pl.ds(SSD_GW + SSD_STATE, SSD_STATE)]).astype(bf16)
            dt8, ac8, act = dt_ref[0, rows, :], ac_ref[0, rows, :], act_ref[0, c]
            dtx, eax, tailx = _expand3(dt8, ac8, ex)
            xdt = xs * dtx
            cb = _nt(ccb, bcb)
            stb = st[...].astype(bf16)
            st_ref[0, c] = stb
            xdtb = xdt.astype(bf16)
            outs = []
            for h in range(SSD_HPG):
                dec = jnp.exp(jnp.where(causal, ac8[:, h:h + 1] - act[h:h + 1, :], -1e30))
                outs.append(_nn((cb * dec).astype(bf16), xdtb[:, hd * h:hd * (h + 1)]))
            y_ref[rows, :] = (jnp.concatenate(outs, axis=1) + _nn(ccb, stb) * eax + dsk_ref[0] * xs).astype(bf16)
            st[...] = st[...] * eax[LS - 1:LS, :] + _tn(bcb, (xdt * tailx).astype(bf16))
            return carry

        lax.fori_loop(0, tb // LS, chunk, 0, unroll=min(4, tb // LS))

    grp, xs, ph, pht, gvec, ex, stspec = _ssd_specs(tb)
    return pl.pallas_call(
        body, name="ssd_fwd", grid=(SSD_GROUPS, S // tb),
        in_specs=[grp, ph, ph, pht, gvec, ex], out_specs=[xs, stspec],
        out_shape=[jax.ShapeDtypeStruct((S, SSD_WIDTH), bf16), jax.ShapeDtypeStruct((SSD_GROUPS, nc, SSD_STATE, SSD_GW), bf16)],
        scratch_shapes=[pltpu.VMEM((SSD_STATE, SSD_GW), f32)], compiler_params=_params(("parallel", "arbitrary")),
    )(pre, dt_g, ac_g, act_g, dskx, _expander())


def _ssd_bwd(pre, dt_g, ac_g, act_g, sg_g, dskx, nega_g, dy, states, tb):
    S = pre.shape[0]
    nb = S // tb
    nck = tb // LS
    hd = SSD_GW // SSD_HPG

    def body(p_ref, dt_ref, ac_ref, act_ref, sg_ref, dsk_ref, na_ref, ex_ref, ext_ref, dy_ref, st_ref,
             dp_ref, ddt_ref, gsk_ref, gal_ref, gdb_ref, dst, skacc):
        @pl.when(pl.program_id(1) == 0)
        def _():
            dst[...] = jnp.zeros_like(dst)
            skacc[...] = jnp.zeros_like(skacc)
            gal_ref[...] = jnp.zeros_like(gal_ref)
            gdb_ref[...] = jnp.zeros_like(gdb_ref)

        ex, ext = ex_ref[...], ext_ref[...]
        li = lax.broadcasted_iota(jnp.int32, (LS, LS), 0)
        si = lax.broadcasted_iota(jnp.int32, (LS, LS), 1)
        causal = li >= si
        anti = si >= li
        upper = anti.astype(bf16)
        eye = (si == li).astype(bf16)
        last_row = (lax.broadcasted_iota(jnp.int32, (LS, 1), 0) == LS - 1).astype(f32)
        head_id = lax.broadcasted_iota(jnp.int32, (1, SSD_HPG), 1)
        head_col = lax.broadcasted_iota(jnp.int32, (SSD_HPG, 1), 0)
        neg_a = na_ref[0]
        dskv = dsk_ref[0]

        def chunk(cc, carry):
            c = nck - 1 - cc
            rows = pl.ds(pl.multiple_of(c * LS, LS), LS)
            px = p_ref[rows, pl.ds(0, SSD_GW)]
            pb = p_ref[rows, pl.ds(SSD_GW, SSD_STATE)]
            pc = p_ref[rows, pl.ds(SSD_GW + SSD_STATE, SSD_STATE)]
            sgx, sgb, sgc = jax.nn.sigmoid(px), jax.nn.sigmoid(pb), jax.nn.sigmoid(pc)
            xs = px * sgx
            bcb = (pb * sgb).astype(bf16)
            ccb = (pc * sgc).astype(bf16)
            dt8, ac8, act = dt_ref[0, rows, :], ac_ref[0, rows, :], act_ref[0, c]
            dtx, eax, tailx = _expand3(dt8, ac8, ex)
            xdt = xs * dtx
            ex_last = eax[LS - 1:LS, :]
            stb = st_ref[0, c]
            dyv = dy_ref[rows, :]
            dyb = dyv.astype(bf16)
            xdtb = xdt.astype(bf16)
            skacc[...] += jnp.sum(dyv * xs, axis=0, keepdims=True)
            yinter = _nn(ccb, stb) * eax
            dzb = (dyv * eax).astype(bf16)
            dcc = _nt(dzb, stb)
            dstv = dst[...]
            dstb = dstv.astype(bf16)
            xt = xdt * tailx
            dxt = _nn(bcb, dstb)
            dbc = _nt(xt.astype(bf16), dstb)
            dxdt = dxt * tailx
            lastrow = jnp.sum(dxt * xt, axis=0, keepdims=True) + jnp.sum(dstv * stb.astype(f32), axis=0, keepdims=True) * ex_last
            dst[...] = dstv * ex_last + _tn(ccb, dzb)
            cb = _nt(ccb, bcb)
            cbt = _nt(bcb, ccb)
            dcb = jnp.zeros((LS, LS), f32)
            dac8 = jnp.zeros((LS, SSD_HPG), f32)
            dact = jnp.zeros((SSD_HPG, LS), f32)
            dxin = []
            for h in range(SSD_HPG):
                sl = slice(hd * h, hd * (h + 1))
                col, rowv = ac8[:, h:h + 1], act[h:h + 1, :]
                dec = jnp.exp(jnp.where(causal, col - rowv, -1e30))
                dect = jnp.exp(jnp.where(anti, rowv - col, -1e30))
                gm = cb * dec
                dgm = _nt(dyb[:, sl], xdtb[:, sl])
                dxin.append(_nn((cbt * dect).astype(bf16), dyb[:, sl]))
                dcb = dcb + dgm * dec
                w = dgm * gm
                dac8 = dac8 + jnp.sum(w, axis=1, keepdims=True) * (head_id == h).astype(f32)
                dact = dact + (head_col == h).astype(f32) * jnp.sum(w, axis=0, keepdims=True)
            dxintra = jnp.concatenate(dxin, axis=1)
            dcbb = dcb.astype(bf16)
            dcc = dcc + _nn(dcbb, bcb)
            dbc = dbc + _tn(dcbb, ccb)
            dxdt = dxdt + dxintra
            dacx = dyv * yinter - dxt * xt + last_row * lastrow
            red = _sel_r(jnp.concatenate([dacx, dxdt * xs], axis=0), ext)
            dac8 = dac8 - _rows_to_cols(dact, eye) + red[0:LS]
            da8 = _sel_l(upper, dac8)
            ddt8 = red[LS:2 * LS] + da8 * neg_a
            gal_ref[0] += jnp.sum(da8 * dt8 * neg_a, axis=0, keepdims=True)
            ddr = ddt8 * sg_ref[0, rows, :]
            ddt_ref[0, rows, :] = ddr
            gdb_ref[0] += jnp.sum(ddr, axis=0, keepdims=True)
            dsilu = lambda p, s: s * (1.0 + p * (1.0 - s))
            dp_ref[rows, pl.ds(0, SSD_GW)] = (dskv * dyv + dxdt * dtx) * dsilu(px, sgx)
            dp_ref[rows, pl.ds(SSD_GW, SSD_STATE)] = dbc * dsilu(pb, sgb)
            dp_ref[rows, pl.ds(SSD_GW + SSD_STATE, SSD_STATE)] = dcc * dsilu(pc, sgc)
            return carry

        lax.fori_loop(0, nck, chunk, 0, unroll=min(4, nck))

        @pl.when(pl.program_id(1) == nb - 1)
        def _():
            gsk_ref[0] = skacc[...]

    grp, xs, ph, pht, gvec, ex, stspec = _ssd_specs(tb, rev_nb=nb)
    small = pl.BlockSpec((1, 1, SSD_HPG), lambda g, i: (g, 0, 0))
    ext = pl.BlockSpec((SSD_GW, SSD_HPG), lambda g, i: (0, 0))
    sm = jax.ShapeDtypeStruct((SSD_GROUPS, 1, SSD_HPG), f32)
    expander = _expander()
    return pl.pallas_call(
        body, name="ssd_bwd", grid=(SSD_GROUPS, nb),
        in_specs=[grp, ph, ph, pht, ph, gvec, small, ex, ext, xs, stspec],
        out_specs=[grp, ph, gvec, small, small],
        out_shape=[jax.ShapeDtypeStruct((S, CONV_DIM), f32), jax.ShapeDtypeStruct((SSD_GROUPS, S, SSD_HPG), f32),
                   jax.ShapeDtypeStruct((SSD_GROUPS, 1, SSD_GW), f32), sm, sm],
        scratch_shapes=[pltpu.VMEM((SSD_STATE, SSD_GW), f32), pltpu.VMEM((1, SSD_GW), f32)],
        compiler_params=_params(("parallel", "arbitrary")),
    )(pre, dt_g, ac_g, act_g, sg_g, dskx, nega_g, expander, expander.T, dy, states)


def _tiles(S):
    return dict(tb=min(512, S), tr=min(256, S), tm=min(1024, S))


def _local_step(x, positions, target, norm1_w, w_main, w_dt, conv_w, conv_b, dt_bias, a_log, d_skip, ssd_norm_w,
                late_token, late_weights, norm_f_w, reducer):
    S, D = x.shape
    t = _tiles(S)
    tb, tr, tm = t["tb"], t["tr"], t["tm"]

    half = RET_DK // 2
    inv_freq = ROPE_THETA ** (-jnp.arange(half, dtype=f32) / half)
    ang = positions.astype(f32)[:, None] * inv_freq
    cos, sin = jnp.cos(ang), jnp.sin(ang)
    log_gamma = jnp.log1p(-(2.0 ** (-5.0 - jnp.arange(RET_HEADS, dtype=f32))))
    idx = jnp.arange(CHUNK, dtype=f32)
    intra = jnp.exp(jnp.abs(idx[:, None] - idx[None, :]) * log_gamma[:, None, None])
    qdec = jnp.exp((idx + 1.0)[None, :] * log_gamma[:, None])[:, :, None]
    kdec = jnp.exp((CHUNK - 1.0 - idx)[None, :] * log_gamma[:, None])[:, :, None]
    cdec = jnp.exp(CHUNK * log_gamma)[:, None, None]
    conv_wm, conv_bm = _xbc_group_major(conv_w), _xbc_group_major(conv_b)

    h, ht = _norm1_fwd(x, norm1_w + late_token[0, 0], tr)
    proj = _mm([(h, 0, w_main, 0, 0, D, D)], S, C_XBC, tm=tm, tn=1024, out_dtype=bf16, name="proj_main")
    proj_x = _mm([(h, 0, w_main, 0, C_XBC // 1024, D, D)], S, CONV_DIM, tm=tm, tn=1024, out_dtype=f32, name="proj_xbc")
    dt_raw = _mm1(h, w_dt, tm=tm, tn=128, tk=D, out_dtype=f32, name="proj_dt")
    y_ret, yr, yrt, ret_states = _ret_fwd(proj, cos, sin, intra, qdec, kdec, cdec, tb)
    pre = _conv_fwd(proj_x, conv_wm, conv_bm, min(1024, S), 512)
    pad64 = lambda v: jnp.pad(v, ((0, 0), (0, 128 - SSD_HEADS)))
    dt, sg, ac = _dt_prep(dt_raw, pad64(dt_bias), pad64(a_log), tb)
    dt_g, ac_g, sg_g, act_g = _group_major(dt), _group_major(ac), _group_major(sg), _group_major_t(ac)
    dskx = jnp.repeat(d_skip.reshape(SSD_GROUPS, 1, SSD_HPG), SSD_GW // SSD_HPG, axis=2)
    nega_g = (-jnp.exp(a_log)).reshape(SSD_GROUPS, 1, SSD_HPG)
    y_ssd, ssd_states = _ssd_fwd(pre, dt_g, ac_g, act_g, dskx, tb)
    ys, yst = _ssd_norm_fwd(y_ssd, proj, ssd_norm_w, tr // 2)
    w_br, w_bs, w_o = late_weights(ys)
    p_r = _mm1(yr, w_br, tm=tm, tn=1024, tk=2048, out_dtype=bf16, name="branch_ret")
    p_s = _mm1(ys, w_bs, tm=tm, tn=1024, tk=4096, out_dtype=bf16, name="branch_ssd")
    merged, mergedt = _merge_fwd(p_r, p_s, proj, tr)
    mo = _mm1(merged, w_o, tm=tm, tn=1024, tk=2048, out_dtype=bf16, name="out_proj")
    dx2, dx2b, loss, g_norm_f = _final_fwd_bwd(x, mo, target, norm_f_w.reshape(1, D), tr)

    tkt = min(4096, S)
    wg = lambda at, b, name, tn=1024: _mm1(at, b, tm=min(1024, at.shape[0]), tn=tn, tk=tkt, out_dtype=f32, name=name)
    dm = _mm1(dx2b, w_o, tm=tm, tn=1024, tk=2048, out_dtype=bf16, name="d_merged", tb=True)
    g_w_o = wg(mergedt, dx2b, "g_w_out")
    dp_r, dp_s, dproj = _merge_bwd(dm, p_r, p_s, proj, tr)
    dyr = _mm1(dp_r, w_br, tm=tm, tn=1024, tk=2048, out_dtype=bf16, name="d_yr", tb=True)
    dys = _mm1(dp_s, w_bs, tm=tm, tn=1024, tk=2048, out_dtype=bf16, name="d_ys", tb=True)
    g_w_br = wg(yrt, dp_r, "g_w_br_ret")
    g_w_bs = wg(yst, dp_s, "g_w_br_ssd")
    dy_ssd, dproj, g_ssd_norm = _ssd_norm_bwd(y_ssd, proj, ssd_norm_w, dys, dproj, tr // 2)
    dproj = _ret_bwd(proj, cos, sin, intra, qdec, kdec, cdec, y_ret, dyr, ret_states, dproj, tb)
    dpre, ddt_g, gsk, gal, gdb = _ssd_bwd(pre, dt_g, ac_g, act_g, sg_g, dskx, nega_g, dy_ssd, ssd_states, tb)
    dproj, gcw, gcb = _conv_bwd(dpre, proj_x, conv_wm, dproj, min(1024, S), 512)
    ddt = jnp.transpose(ddt_g, (1, 0, 2)).reshape(S, SSD_HEADS)
    ddt_p = jnp.pad(ddt, ((0, 0), (0, 128 - SSD_HEADS))).astype(bf16)

    hr = D // 2
    wg_half = lambda off, b, name, tn=1024: _mm([(ht, 0, b, 0, 0, S, tkt)], hr, b.shape[1], tm=hr, tn=tn, out_dtype=f32,
                                                 name=name, row_off=off)
    off_sib, off_own = reducer.halves()
    gs_main = wg_half(off_sib, dproj, "g_w_in_main_sib")
    gs_dt = wg_half(off_sib, ddt_p, "g_w_in_dt_sib", tn=128)
    swap_state = reducer.first(gs_main, gs_dt, g_w_br, g_w_bs, g_w_o)
    ddt_p = ddt_p + swap_state[-1][0, 0].astype(bf16)
    go_main = wg_half(off_own, dproj, "g_w_in_main_own")
    go_dt = wg_half(off_own, ddt_p, "g_w_in_dt_own", tn=128)
    reduce_state = reducer.second(swap_state, go_main, go_dt)
    ddt_p = ddt_p + reduce_state[-1][0, 0].astype(bf16)
    dh = _mm([(dproj, 0, w_main, 0, 0, N_MAIN, N_MAIN // 8), (ddt_p, 0, w_dt, 0, 0, 128, 128)], S, D, tm=tm, tn=1024,
             out_dtype=bf16, name="d_h", tb=True)
    grad_x, g_norm1 = _norm1_bwd(x, norm1_w, dh, dx2, tr)

    seg = lambda v: jnp.sum(v.reshape(SSD_HEADS, SSD_GW // SSD_HPG), axis=1).reshape(1, SSD_HEADS)
    grads = dict(
        norm1_w=g_norm1, w_in_main=(gs_main, go_main), w_in_dt=(gs_dt, go_dt),
        conv_w=_xbc_original(gcw), conv_b=_xbc_original(gcb),
        dt_bias=gdb.reshape(1, SSD_HEADS), a_log=gal.reshape(1, SSD_HEADS), d_skip=seg(gsk),
        ssd_norm_w=g_ssd_norm, w_br_ret=g_w_br, w_br_ssd=g_w_bs, w_out=g_w_o, norm_f_w=g_norm_f,
    )
    return loss, grad_x, grads, reduce_state


def _me():
    return lax.axis_index("x"), lax.axis_index("y"), lax.axis_index("c")


def _other_chips(x, y):
    return [(1 - x, y), (x, 1 - y), (1 - x, 1 - y)]


def _gather_weights(a, cw):
    R = a.shape[0]
    hr, hq = R // 2, R // 4

    def body(a_ref, cw_ref, ga_ref, gc_ref, send_sems, recv_sems):
        x, y, c = _me()
        me, sibling = (x, y, c), (x, y, 1 - c)
        nx, ny = (1 - x, y, c), (x, 1 - y, c)
        k, kx, ky, kd = 2 * x + y, 2 * (1 - x) + y, 2 * x + (1 - y), 2 * (1 - x) + (1 - y)

        def rows(half, q):
            return pl.ds(pl.multiple_of(half * hr + q * hq, 8), hq)

        def cp(sem, shard, half, q, to, src=None):
            dst = ga_ref.at[shard, rows(half, q), :]
            return pltpu.make_async_remote_copy(src_ref=dst if src is None else src, dst_ref=dst, send_sem=send_sems.at[sem],
                                                recv_sem=recv_sems.at[sem], device_id=to, device_id_type=MESH)

        def small(j, src_shard, to):
            return pltpu.make_async_remote_copy(
                src_ref=cw_ref, dst_ref=gc_ref.at[src_shard], send_sem=send_sems.at[12 + j], recv_sem=recv_sems.at[12 + j],
                device_id=to, device_id_type=MESH)

        own = lambda q: a_ref.at[rows(c, q), :]
        smalls = [small(j, k, (*chip, c)) for j, chip in enumerate(_other_chips(x, y))]
        sends = [cp(0, k, c, 0, nx, own(0)), cp(2, k, c, 1, ny, own(1)), cp(1, k, c, 1, nx, own(1)), cp(3, k, c, 0, ny, own(0))]
        for s in sends + smalls:
            s.start()
        arrivals = [(0, kx, 0, (4, ny)), (2, ky, 1, (5, nx)), (1, kx, 1, None), (3, ky, 0, None), (4, kd, 0, None), (5, kd, 1, None)]
        for sem, shard, q, onward in arrivals:
            cp(sem, shard, c, q, me).wait_recv()
            if onward is not None:
                sends.append(cp(onward[0], shard, c, q, onward[1]))
                sends[-1].start()
            sends.append(cp(6 + sem, shard, c, q, sibling))
            sends[-1].start()
        for sem, shard, q, _ in arrivals:
            cp(6 + sem, shard, 1 - c, q, me).wait_recv()
        for j, chip in enumerate(_other_chips(x, y)):
            small(j, 2 * chip[0] + chip[1], me).wait_recv()
        for s in sends + smalls:
            s.wait_send()

    return pl.pallas_call(
        body, name="gather_weights", in_specs=[ANY, ANY], out_specs=[ANY, ANY],
        out_shape=[jax.ShapeDtypeStruct((N_SHARD,) + a.shape, a.dtype), jax.ShapeDtypeStruct((N_SHARD,) + cw.shape, cw.dtype)],
        scratch_shapes=[pltpu.SemaphoreType.DMA((15,)), pltpu.SemaphoreType.DMA((15,))],
        compiler_params=pltpu.CompilerParams(has_side_effects=True),
    )(a, cw)


def _gather_late_copies(src, land, send_sems, recv_sems):
    x, y, c = _me()
    k = 2 * x + y
    return [pltpu.make_async_remote_copy(src_ref=src, dst_ref=land.at[k], send_sem=send_sems.at[j], recv_sem=recv_sems.at[j],
                                         device_id=(*chip, c), device_id_type=MESH) for j, chip in enumerate(_other_chips(x, y))]


def _gather_late_start(b):
    land = lax.empty((N_SHARD,) + b.shape, b.dtype)

    def body(b_ref, land_ref, send_sems, recv_sems, b_thru, land_thru, token):
        for cp in _gather_late_copies(b_ref, land_ref, send_sems, recv_sems):
            cp.start()
        token[...] = jnp.zeros_like(token)

    return pl.pallas_call(
        body, name="gather_late_start", in_specs=[HBM, HBM],
        out_specs=(SEM, SEM, HBM, HBM, pl.BlockSpec(memory_space=pltpu.VMEM)),
        out_shape=(pltpu.SemaphoreType.DMA((3,)), pltpu.SemaphoreType.DMA((3,)), pltpu.HBM(b.shape, b.dtype),
                   pltpu.HBM(land.shape, land.dtype), jax.ShapeDtypeStruct((8, 128), f32)),
        input_output_aliases={0: 2, 1: 3}, compiler_params=pltpu.CompilerParams(has_side_effects=DATAFLOW),
    )(pltpu.with_memory_space_constraint(b, pltpu.HBM), pltpu.with_memory_space_constraint(land, pltpu.HBM))


def _gather_late_wait(send_sems, recv_sems, src, land, after):
    def body(b_ref, land_ref, send_sems_ref, recv_sems_ref, after_ref, b_dead, land_out):
        x, y, c = _me()
        for j, chip in enumerate(_other_chips(x, y)):
            kk = 2 * chip[0] + chip[1]
            cp = pltpu.make_async_remote_copy(src_ref=b_ref, dst_ref=land_ref.at[kk], send_sem=send_sems_ref.at[j],
                                              recv_sem=recv_sems_ref.at[j], device_id=(x, y, c), device_id_type=MESH)
            cp.wait_send()
            cp.wait_recv()

    return pl.pallas_call(
        body, name="gather_late_wait", in_specs=[HBM, HBM, SEM, SEM, ANY], out_specs=[HBM, HBM],
        out_shape=[pltpu.HBM(src.shape, src.dtype), pltpu.HBM(land.shape, land.dtype)], input_output_aliases={0: 0, 1: 1},
        compiler_params=pltpu.CompilerParams(has_side_effects=DATAFLOW),
    )(src, land, send_sems, recv_sems, after)[1]


HBM = pl.BlockSpec(memory_space=pltpu.HBM)
SEM = pl.BlockSpec(memory_space=pltpu.SEMAPHORE)
DATAFLOW = pltpu.SideEffectType.DATAFLOW_SIDE_EFFECTING


def _swap_copies(srcs, lands, send_sems, recv_sems):
    x, y, c = _me()

    def cp(src, dst, q):
        return pltpu.make_async_remote_copy(src_ref=src, dst_ref=dst, send_sem=send_sems.at[q], recv_sem=recv_sems.at[q],
                                            device_id=(x, y, 1 - c), device_id_type=MESH)

    return [cp(srcs[0], lands[0], 0), cp(srcs[1], lands[1], 1)] + [cp(srcs[2].at[s, 1 - c], lands[2].at[s], 2 + s) for s in range(N_SHARD)]


def _sibling_swap_start(g_main, g_dt, g_b):
    srcs = [g_main, g_dt, g_b]
    lands = [lax.empty(g_main.shape, g_main.dtype), lax.empty(g_dt.shape, g_dt.dtype),
             lax.empty(g_b.shape[:1] + g_b.shape[2:], g_b.dtype)]

    def body(*refs):
        for cp in _swap_copies(refs[0:3], refs[3:6], refs[6], refs[7]):
            cp.start()
        refs[14][...] = jnp.zeros_like(refs[14])

    hbm = lambda a: pltpu.HBM(a.shape, a.dtype)
    out = pl.pallas_call(
        body, name="sibling_swap_start", in_specs=[HBM] * 6,
        out_specs=(SEM, SEM, *[HBM] * 6, pl.BlockSpec(memory_space=pltpu.VMEM)),
        out_shape=(pltpu.SemaphoreType.DMA((2 + N_SHARD,)), pltpu.SemaphoreType.DMA((2 + N_SHARD,)), *[hbm(a) for a in srcs + lands],
                   jax.ShapeDtypeStruct((8, 128), f32)),
        input_output_aliases={t: 2 + t for t in range(6)}, compiler_params=pltpu.CompilerParams(has_side_effects=DATAFLOW),
    )(*[pltpu.with_memory_space_constraint(a, pltpu.HBM) for a in srcs + lands])
    return out[0], out[1], list(out[2:5]), list(out[5:8]), out[8]


def _sibling_swap_wait(send_sems, recv_sems, srcs, lands, after):
    def body(*refs):
        for cp in _swap_copies(refs[0:3], refs[3:6], refs[6], refs[7]):
            cp.wait_send()
            cp.wait_recv()

    hbm = lambda a: pltpu.HBM(a.shape, a.dtype)
    out = pl.pallas_call(
        body, name="sibling_swap_wait", in_specs=[HBM] * 6 + [SEM, SEM, ANY], out_specs=[HBM] * 6,
        out_shape=[hbm(a) for a in list(srcs) + list(lands)], input_output_aliases={t: t for t in range(6)},
        compiler_params=pltpu.CompilerParams(has_side_effects=DATAFLOW),
    )(*srcs, *lands, send_sems, recv_sems, after)
    return list(out[:3]), list(out[3:])


def _exchange_copies(ins, lands, send_sems, recv_sems):
    n = len(ins)
    x, y, c = _me()
    cps = []
    for j, chip in enumerate(_other_chips(x, y)):
        kk = 2 * chip[0] + chip[1]
        for t in range(n):
            cps.append(pltpu.make_async_remote_copy(
                src_ref=ins[t].at[kk], dst_ref=lands[t].at[j], send_sem=send_sems.at[n * j + t],
                recv_sem=recv_sems.at[n * j + t], device_id=(*chip, c), device_id_type=MESH))
    return cps


def _chip_exchange_start(arrs):
    n = len(arrs)
    lands = [lax.empty((3,) + a.shape[1:], a.dtype) for a in arrs]

    def body(*refs):
        ins, lands_in = refs[:n], refs[n:2 * n]
        send_sems, recv_sems = refs[2 * n], refs[2 * n + 1]
        token = refs[4 * n + 2]
        for cp in _exchange_copies(ins, lands_in, send_sems, recv_sems):
            cp.start()
        token[...] = jnp.zeros_like(token)

    hbm = lambda a: pltpu.HBM(a.shape, a.dtype)
    out = pl.pallas_call(
        body, name="chip_exchange_start", in_specs=[HBM] * (2 * n),
        out_specs=(SEM, SEM, *[HBM] * (2 * n), pl.BlockSpec(memory_space=pltpu.VMEM)),
        out_shape=(pltpu.SemaphoreType.DMA((3 * n,)), pltpu.SemaphoreType.DMA((3 * n,)), *[hbm(a) for a in arrs],
                   *[hbm(a) for a in lands], jax.ShapeDtypeStruct((8, 128), f32)),
        input_output_aliases={t: 2 + t for t in range(2 * n)},
        compiler_params=pltpu.CompilerParams(has_side_effects=DATAFLOW),
    )(*[pltpu.with_memory_space_constraint(a, pltpu.HBM) for a in list(arrs) + lands])
    return out[0], out[1], list(out[2:2 + n]), list(out[2 + n:2 + 2 * n]), out[2 + 2 * n]


def _chip_exchange_wait(send_sems, recv_sems, srcs, lands, after):
    n = len(srcs)

    def body(*refs):
        ins, lands_in = refs[:n], refs[n:2 * n]
        send_sems_ref, recv_sems_ref = refs[2 * n], refs[2 * n + 1]
        for cp in _exchange_copies(ins, lands_in, send_sems_ref, recv_sems_ref):
            cp.wait_send()
            cp.wait_recv()

    hbm = lambda a: pltpu.HBM(a.shape, a.dtype)
    out = pl.pallas_call(
        body, name="chip_exchange_wait", in_specs=[HBM] * (2 * n) + [SEM, SEM, ANY],
        out_specs=[HBM] * (2 * n), out_shape=[hbm(a) for a in list(srcs) + list(lands)],
        input_output_aliases={t: t for t in range(2 * n)},
        compiler_params=pltpu.CompilerParams(has_side_effects=DATAFLOW),
    )(*srcs, *lands, send_sems, recv_sems, after)
    return list(out[:n]), list(out[n:])


def _share_halves(bufs, by_cols, name):
    n = len(bufs)

    def body(*refs):
        ins, outs = refs[:n], refs[n:2 * n]
        send_sems, recv_sems = refs[2 * n], refs[2 * n + 1]
        x, y, c = _me()

        def part(ref, t, half):
            if by_cols[t]:
                w = bufs[t].shape[1] // 2
                return ref.at[:, pl.ds(pl.multiple_of(half * w, 128), w)]
            return ref.at[half]

        sends = [pltpu.make_async_remote_copy(src_ref=part(ins[t], t, c), dst_ref=part(outs[t], t, c), send_sem=send_sems.at[t],
                                              recv_sem=recv_sems.at[t], device_id=(x, y, 1 - c), device_id_type=MESH) for t in range(n)]
        for cp in sends:
            cp.start()
        for t in range(n):
            pltpu.make_async_remote_copy(src_ref=part(ins[t], t, c), dst_ref=part(outs[t], t, 1 - c), send_sem=send_sems.at[t],
                                         recv_sem=recv_sems.at[t], device_id=(x, y, c), device_id_type=MESH).wait_recv()
        for cp in sends:
            cp.wait_send()

    return pl.pallas_call(
        body, name=name, in_specs=[ANY] * n, out_specs=[ANY] * n,
        out_shape=[jax.ShapeDtypeStruct(a.shape, a.dtype) for a in bufs], input_output_aliases={t: t for t in range(n)},
        scratch_shapes=[pltpu.SemaphoreType.DMA((n,)), pltpu.SemaphoreType.DMA((n,))],
        compiler_params=pltpu.CompilerParams(has_side_effects=True),
    )(*bufs)


def _gather_vec(v):
    n = v.shape[1]

    def body(v_ref, o_ref, send_sems, recv_sems):
        x, y, c = _me()
        me = 4 * x + 2 * y + c
        cps = []
        for j in range(1, 8):
            fx, fy, fc = (j >> 2) & 1, (j >> 1) & 1, j & 1
            peer = (x ^ fx, y ^ fy, c ^ fc)
            cps.append(pltpu.make_async_remote_copy(
                src_ref=v_ref, dst_ref=o_ref.at[pl.ds(me, 1), :], send_sem=send_sems.at[j - 1], recv_sem=recv_sems.at[j - 1],
                device_id=peer, device_id_type=MESH))
        for cp in cps:
            cp.start()
        for j in range(1, 8):
            fx, fy, fc = (j >> 2) & 1, (j >> 1) & 1, j & 1
            src = 4 * (x ^ fx) + 2 * (y ^ fy) + (c ^ fc)
            pltpu.make_async_remote_copy(
                src_ref=v_ref, dst_ref=o_ref.at[pl.ds(src, 1), :], send_sem=send_sems.at[j - 1], recv_sem=recv_sems.at[j - 1],
                device_id=(x, y, c), device_id_type=MESH).wait_recv()
        for cp in cps:
            cp.wait_send()

    return pl.pallas_call(
        body, name="gather_vec", in_specs=[ANY], out_specs=ANY, out_shape=jax.ShapeDtypeStruct((8, n), v.dtype),
        scratch_shapes=[pltpu.SemaphoreType.DMA((7,)), pltpu.SemaphoreType.DMA((7,))],
        compiler_params=pltpu.CompilerParams(has_side_effects=True),
    )(v)


def _pair_sum(g, r, name, tr):
    L, hr, C = r.shape
    both_halves = g.ndim == 4

    def body(c_ref, g_ref, r_ref, o_ref):
        def strip(rows):
            gv = g_ref[0, 0, rows, :] if both_halves else g_ref[0, rows, :]
            o_ref[0, rows, :] = (gv + r_ref[0, rows, :]).astype(bf16)
        _for_strips(tr, strip)

    g_spec = (pl.BlockSpec((1, 1, tr, C), lambda s, i, c_ref: (s, c_ref[0], i, 0)) if both_halves
              else pl.BlockSpec((1, tr, C), lambda s, i, c_ref: (s, i, 0)))
    grid_spec = pltpu.PrefetchScalarGridSpec(
        num_scalar_prefetch=1, grid=(L, hr // tr),
        in_specs=[g_spec, pl.BlockSpec((1, tr, C), lambda s, i, c_ref: (s, i, 0))],
        out_specs=pl.BlockSpec((1, tr, C), lambda s, i, c_ref: (s, i, 0)))
    c = lax.axis_index("c").reshape(1).astype(jnp.int32)
    return pl.pallas_call(body, name=name, grid_spec=grid_spec, out_shape=jax.ShapeDtypeStruct((L, hr, C), bf16),
                          compiler_params=_params(("parallel", "parallel")))(c, g, r)


def _own_sum(p, got, name, transposed=False):
    _, hr, C = p.shape
    tr = SUM_ROWS
    c_full, c_pad = C // 128 * 128, -(-C // 128) * 128

    def total(p_ref, got_ref, rows):
        return ((p_ref[0, rows, :].astype(f32) + got_ref[0, rows, :].astype(f32)) + got_ref[1, rows, :].astype(f32)) \
            + got_ref[2, rows, :].astype(f32)

    def body(idx_ref, p_ref, got_ref, o_ref):
        def strip(rows):
            o_ref[0, rows, :] = total(p_ref, got_ref, rows)
        _for_strips(tr, strip)

    def body_t(idx_ref, p_ref, got_ref, o_ref, buf):
        if c_pad > c_full:
            buf[:, pl.ds(c_full, c_pad - c_full)] = jnp.zeros((tr, c_pad - c_full), f32)

        def strip(rows):
            buf[rows, pl.ds(0, C)] = total(p_ref, got_ref, rows)
        _for_strips(tr, strip)
        o_ref[...] = buf[...].T[:C]

    in_specs = [pl.BlockSpec((1, tr, C), lambda i, idx: (idx[0], i, 0)), pl.BlockSpec((3, tr, C), lambda i, idx: (0, i, 0))]
    x, y, c = _me()
    idx = jnp.stack([2 * x + y, c]).astype(jnp.int32)
    if transposed:
        grid_spec = pltpu.PrefetchScalarGridSpec(num_scalar_prefetch=1, grid=(hr // tr,), in_specs=in_specs,
                                                 out_specs=pl.BlockSpec((C, tr), lambda i, idx: (0, idx[1] * (hr // tr) + i)),
                                                 scratch_shapes=[pltpu.VMEM((tr, c_pad), f32)])
        return pl.pallas_call(body_t, name=name, grid_spec=grid_spec, out_shape=jax.ShapeDtypeStruct((C, 2 * hr), f32),
                              compiler_params=_params(("parallel",)))(idx, p, got)
    grid_spec = pltpu.PrefetchScalarGridSpec(num_scalar_prefetch=1, grid=(hr // tr,), in_specs=in_specs,
                                             out_specs=pl.BlockSpec((1, tr, C), lambda i, idx: (idx[1], i, 0)))
    return pl.pallas_call(body, name=name, grid_spec=grid_spec, out_shape=jax.ShapeDtypeStruct((2, hr, C), f32),
                          compiler_params=_params(("parallel",)))(idx, p, got)


def _adamw(w, g, m, v, name, tr):
    _, R, C = w.shape
    rs = min(8, tr)

    def body(w_ref, g_ref, m_ref, v_ref, d_ref, nm_ref, nv_ref):
        def strip(s, carry):
            rows = pl.ds(pl.multiple_of(s * rs, rs), rs)
            gv = g_ref[0, rows, :]
            mn = ADAM_B1 * m_ref[0, rows, :] + (1.0 - ADAM_B1) * gv
            vn = ADAM_B2 * v_ref[0, rows, :] + (1.0 - ADAM_B2) * (gv * gv)
            m_hat = mn / (1.0 - ADAM_B1 ** ADAM_STEP)
            v_hat = vn / (1.0 - ADAM_B2 ** ADAM_STEP)
            d_ref[0, rows, :] = -ADAM_LR * (m_hat / (jnp.sqrt(v_hat) + ADAM_EPS) + ADAM_WD * w_ref[0, rows, :])
            nm_ref[0, rows, :] = mn
            nv_ref[0, rows, :] = vn
            return carry

        if R % tr == 0:
            lax.fori_loop(0, tr // rs, strip, 0, unroll=min(2, tr // rs))
        else:
            lax.fori_loop(0, jnp.minimum(tr, R - pl.program_id(0) * tr) // rs, strip, 0)

    blk, grid = pl.BlockSpec((1, tr, C), lambda i: (0, i, 0)), (-(-R // tr),)
    o = jax.ShapeDtypeStruct((1, R, C), f32)
    return pl.pallas_call(body, name=name, grid=grid, in_specs=[blk] * 4, out_specs=[blk] * 3, out_shape=[o, o, o],
                          compiler_params=_params(("parallel",)))(w, g, m, v)


def _sum8(t):
    n = t.shape[1]

    def body(t_ref, o_ref):
        acc = t_ref[pl.ds(0, 1), :]
        for r in range(1, 8):
            acc = acc + t_ref[pl.ds(r, 1), :]
        o_ref[...] = acc

    return pl.pallas_call(body, name="sum_devices", out_shape=jax.ShapeDtypeStruct((1, n), f32))(t)


def _reduce_swap_start(g_main, g_dt, g_b):
    hr = g_main.shape[0]
    return _sibling_swap_start(g_main, g_dt, g_b.reshape(N_SHARD, 2, hr, g_b.shape[-1]))


def _reduce_start(swap_state, g_main, g_dt):
    hr = g_main.shape[0]
    send_sems, recv_sems, srcs, lands, _ = swap_state
    srcs, (r_main, r_dt, r_b) = _sibling_swap_wait(send_sems, recv_sems, srcs, lands, g_dt)
    p_main = _pair_sum(g_main[None], r_main[None], "pair_sum_main", SUM_ROWS // 4)
    p_dt = _pair_sum(g_dt[None], r_dt[None], "pair_sum_dt", SUM_ROWS)
    p_b = _pair_sum(srcs[2], r_b, "pair_sum_b", SUM_ROWS)
    p_in = jnp.transpose(_w_in_grad_full(p_main[0], p_dt[0]).reshape(hr, N_SHARD, W_IN_SHARD), (1, 0, 2))
    return _chip_exchange_start([p_in, p_b])


def _reduce_finish(state, after):
    send_sems, recv_sems, srcs, lands, _ = state
    (p_in, p_b), (got_in, got_b) = _chip_exchange_wait(send_sems, recv_sems, srcs, lands, after)
    mine_in, mine_b = _own_sum(p_in, got_in, "own_sum_in", transposed=True), _own_sum(p_b, got_b, "own_sum_b")
    full_in_t, full_b = _share_halves([mine_in, mine_b], [True, False], "share_halves")
    return full_in_t, full_b.reshape(-1, full_b.shape[-1])


def kernel(x, positions, norm1_w, w_in, conv_w, conv_b, dt_bias, a_log, d_skip, ssd_norm_w, w_br_ret, w_br_ssd, w_out, norm_f_w, loss_target, m_norm1_w, m_w_in, m_conv_w, m_conv_b, m_dt_bias, m_a_log, m_d_skip, m_ssd_norm_w, m_w_br_ret, m_w_br_ssd, m_w_out, m_norm_f_w, v_norm1_w, v_w_in, v_conv_w, v_conv_b, v_dt_bias, v_a_log, v_d_skip, v_ssd_norm_w, v_w_br_ret, v_w_br_ssd, v_w_out, v_norm_f_w):
    D = D_MODEL
    xi, yi, ci = _me()
    k = 2 * xi + yi
    me = 2 * k + ci
    weights = dict(norm1_w=norm1_w, w_in=w_in, conv_w=conv_w, conv_b=conv_b, dt_bias=dt_bias, a_log=a_log, d_skip=d_skip,
                   ssd_norm_w=ssd_norm_w, w_br_ret=w_br_ret, w_br_ssd=w_br_ssd, w_out=w_out, norm_f_w=norm_f_w)
    mom1 = dict(norm1_w=m_norm1_w, w_in=m_w_in, conv_w=m_conv_w, conv_b=m_conv_b, dt_bias=m_dt_bias, a_log=m_a_log, d_skip=m_d_skip,
                ssd_norm_w=m_ssd_norm_w, w_br_ret=m_w_br_ret, w_br_ssd=m_w_br_ssd, w_out=m_w_out, norm_f_w=m_norm_f_w)
    mom2 = dict(norm1_w=v_norm1_w, w_in=v_w_in, conv_w=v_conv_w, conv_b=v_conv_b, dt_bias=v_dt_bias, a_log=v_a_log, d_skip=v_d_skip,
                ssd_norm_w=v_ssd_norm_w, w_br_ret=v_w_br_ret, w_br_ssd=v_w_br_ssd, w_out=v_w_out, norm_f_w=v_norm_f_w)

    a_sh = w_in[0].astype(bf16)
    b_sh = jnp.concatenate([w_br_ret[0], w_br_ssd[0], w_out[0]], axis=0).astype(bf16)
    ga, gc = _gather_weights(a_sh, conv_w[0])
    ga, b_late = lax.optimization_barrier((ga, b_sh))
    late_send, late_recv, late_src, late_land, late_token = _gather_late_start(b_late)
    own = lambda g, s: lax.dynamic_update_slice_in_dim(g, s[None], k, axis=0)
    ga, gc = own(ga, a_sh), own(gc, conv_w[0])
    w_main, w_dt = _w_main_from_shards(ga)
    conv_full = jnp.transpose(gc, (1, 0, 2)).reshape(SSD_CONV, CONV_DIM)

    def late_weights(after):
        gb = own(_gather_late_wait(late_send, late_recv, late_src, late_land, after), b_sh)
        return gb[:, 0:512].reshape(2048, D), gb[:, 512:1536].reshape(4096, D), gb[:, 1536:2048].reshape(2048, D)

    class Reducer:
        @staticmethod
        def halves():
            return (1 - ci).reshape(1).astype(jnp.int32), ci.reshape(1).astype(jnp.int32)

        @staticmethod
        def first(g_main, g_dt, g_w_br, g_w_bs, g_w_o):
            g_b = jnp.concatenate([g_w_br.reshape(N_SHARD, 512, D), g_w_bs.reshape(N_SHARD, 1024, D),
                                   g_w_o.reshape(N_SHARD, 512, D)], axis=1)
            return _reduce_swap_start(g_main, g_dt, g_b)

        second = staticmethod(_reduce_start)

    loss, grad_x, g, reduce_state = _local_step(x[0], positions[0], loss_target[0], norm1_w, w_main, w_dt, conv_full, conv_b, dt_bias,
                                                a_log, d_skip, ssd_norm_w, late_token, late_weights, norm_f_w, Reducer)

    grad_w_in_t, full_b = _reduce_finish(reduce_state, g["norm1_w"])
    grad_mats = dict(w_br_ret=full_b[0:512], w_br_ssd=full_b[512:1536], w_out=full_b[1536:2048])

    small = [(n, weights[n].size) for n in ("norm1_w", "conv_b", "dt_bias", "a_log", "d_skip", "ssd_norm_w", "norm_f_w")]
    parts = [jnp.pad(loss.reshape(1, 1), ((0, 0), (0, 127)))] + [g[n].reshape(1, -1) for n, _ in small] + [g["conv_w"].reshape(1, -1)]
    vec = jnp.concatenate(parts, axis=1)
    nv = vec.shape[1]
    nvp = -(-nv // 128) * 128
    vec = jnp.pad(vec, ((0, 0), (0, nvp - nv)))
    total = _sum8(lax.dynamic_update_slice_in_dim(_gather_vec(vec), vec, me, axis=0))
    loss_out = total[0, 0]
    off = 128
    grad_small = {}
    for n, sz in small:
        grad_small[n] = total[:, off:off + sz]
        off += sz
    g_conv = total[:, off:off + SSD_CONV * CONV_DIM].reshape(SSD_CONV, CONV_DIM)
    g_conv = lax.dynamic_slice_in_dim(g_conv, k * (CONV_DIM // N_SHARD), CONV_DIM // N_SHARD, axis=1)
    grad_small["conv_w"] = g_conv.reshape(1, -1)

    upd = {}
    for n in ("w_br_ret", "w_br_ssd", "w_out"):
        upd[n] = _adamw(weights[n], grad_mats[n][None], mom1[n], mom2[n], "adamw_" + n, tr=SUM_ROWS)
    tp = lambda t: jnp.swapaxes(t, 1, 2)
    upd["w_in"] = tuple(tp(t) for t in _adamw(tp(w_in), grad_w_in_t[None], tp(m_w_in), tp(v_w_in), "adamw_w_in", tr=256))
    grad_mats["w_in"] = tp(grad_w_in_t[None])
    names_small = [n for n, _ in small] + ["conv_w"]
    flat = lambda d: jnp.concatenate([d[n].reshape(1, -1) for n in names_small], axis=1)
    ns = sum(weights[n].size for n in names_small)
    nsp = -(-ns // 128) * 128
    padv = lambda t: jnp.pad(t, ((0, 0), (0, nsp - ns)))
    small_upd = _adamw(padv(flat(weights))[None], padv(flat(grad_small))[None], padv(flat(mom1))[None],
                       jnp.pad(flat(mom2), ((0, 0), (0, nsp - ns)), constant_values=1.0)[None], "adamw_small", 1)
    off = 0
    for n in names_small:
        sz = weights[n].size
        upd[n] = tuple(t[0, :, off:off + sz] for t in small_upd)
        off += sz

    order = ["norm1_w", "w_in", "conv_w", "conv_b", "dt_bias", "a_log", "d_skip", "ssd_norm_w", "w_br_ret", "w_br_ssd", "w_out", "norm_f_w"]
    grads_out = {**grad_mats, **grad_small}
    shp = lambda n, t: t.reshape(weights[n].shape)
    return (loss_out, grad_x[None], *[shp(n, grads_out[n]) for n in order], *[shp(n, upd[n][0]) for n in order],
            *[shp(n, upd[n][1]) for n in order], *[shp(n, upd[n][2]) for n in order])
```

```python
import jax
import jax.numpy as jnp
import numpy as np
from jax import lax
from jax.experimental import pallas as pl
from jax.experimental.pallas import tpu as pltpu

f32 = jnp.float32
bf16 = jnp.bfloat16
HIGHEST = lax.Precision.HIGHEST
MESH = pl.DeviceIdType.MESH

D_MODEL = 2048
EPS = 1e-6
CHUNK = 64
RET_HEADS = 8
RET_DK = 256
RET_HW = 4 * RET_DK
RET_HP = 2
RET_UNROLL_FWD, RET_UNROLL_BWD = 4, 8
ROPE_THETA = 10000.0
SSD_WIDTH = 4096
SSD_GROUPS = 8
SSD_STATE = 128
SSD_GW = 512
SSD_GC = SSD_GW + 2 * SSD_STATE
SSD_HPG = 8
SSD_CONV = 4
CONV_DIM = 6144
SSD_HEADS = 64
LS = 128

C_RET, C_Z, C_GATES, C_XBC = 0, 8192, 12288, 16384
N_MAIN = 22528
DT_OFF = 18432
IN_PROJ = 22592
N_SHARD = 4
W_IN_SHARD = IN_PROJ // N_SHARD

ADAM_LR, ADAM_B1, ADAM_B2, ADAM_EPS, ADAM_WD, ADAM_STEP = 0.001, 0.9, 0.999, 1e-08, 0.01, 10

VMEM_LIMIT = 56 * 1024 * 1024
SUM_ROWS = 128
ANY = pl.BlockSpec(memory_space=pl.ANY)


def _params(dims):
    return pltpu.CompilerParams(dimension_semantics=dims, vmem_limit_bytes=VMEM_LIMIT)


def _silu(x):
    return x * jax.nn.sigmoid(x)


def _nt(a, b):
    return lax.dot_general(a, b, (((1,), (1,)), ((), ())), preferred_element_type=f32)


def _tn(a, b):
    return lax.dot_general(a, b, (((0,), (0,)), ((), ())), preferred_element_type=f32)


def _nn(a, b):
    return jnp.dot(a, b, preferred_element_type=f32)


def _hi(a, b):
    return jnp.dot(a, b, precision=HIGHEST, preferred_element_type=f32)


def _split(a):
    hi = a.astype(bf16)
    return hi, (a - hi.astype(f32)).astype(bf16)


def _sel_r(a, sel):
    hi, lo = _split(a)
    return _nn(hi, sel) + _nn(lo, sel)


def _sel_l(sel, a):
    hi, lo = _split(a)
    return _nn(sel, hi) + _nn(sel, lo)


def _rows_to_cols(t, eye):
    hi = t.astype(bf16)
    r1 = t - hi.astype(f32)
    mid = r1.astype(bf16)
    lo = (r1 - mid.astype(f32)).astype(bf16)
    return _nt(eye, hi) + _nt(eye, mid) + _nt(eye, lo)


def _xbc_group_major(t):
    R = t.shape[0]
    nb = SSD_GROUPS * SSD_STATE
    parts = [t[:, :SSD_WIDTH].reshape(R, SSD_GROUPS, SSD_GW), t[:, SSD_WIDTH:SSD_WIDTH + nb].reshape(R, SSD_GROUPS, SSD_STATE),
             t[:, SSD_WIDTH + nb:].reshape(R, SSD_GROUPS, SSD_STATE)]
    return jnp.concatenate(parts, axis=2).reshape(R, CONV_DIM)


def _xbc_original(t):
    R = t.shape[0]
    g = t.reshape(R, SSD_GROUPS, SSD_GC)
    parts = [g[:, :, :SSD_GW].reshape(R, SSD_WIDTH), g[:, :, SSD_GW:SSD_GW + SSD_STATE].reshape(R, SSD_GROUPS * SSD_STATE),
             g[:, :, SSD_GW + SSD_STATE:].reshape(R, SSD_GROUPS * SSD_STATE)]
    return jnp.concatenate(parts, axis=1)


def _main_segments():
    segs = []
    for h in range(RET_HEADS):
        segs += [(base + RET_DK * h, RET_DK) for base in (0, 2048, 4096, 6144)]
    segs += [(8192, SSD_WIDTH), (DT_OFF + SSD_HEADS, 2 * D_MODEL)]
    nb = SSD_GROUPS * SSD_STATE
    for g in range(SSD_GROUPS):
        segs += [(12288 + SSD_GW * g, SSD_GW), (12288 + SSD_WIDTH + SSD_STATE * g, SSD_STATE),
                 (12288 + SSD_WIDTH + nb + SSD_STATE * g, SSD_STATE)]
    return segs


def _w_main_from_shards(shards):
    def cols(lo, hi):
        out = []
        while lo < hi:
            s = lo // W_IN_SHARD
            top = min(hi, (s + 1) * W_IN_SHARD)
            out.append(shards[s][:, lo - s * W_IN_SHARD:top - s * W_IN_SHARD])
            lo = top
        return out

    main = jnp.concatenate([p for s, n in _main_segments() for p in cols(s, s + n)], axis=1)
    w_dt = jnp.pad(jnp.concatenate(cols(DT_OFF, DT_OFF + SSD_HEADS), axis=1), ((0, 0), (0, 128 - SSD_HEADS)))
    return main, w_dt


def _w_in_grad_full(g_main, g_dt):
    D = g_main.shape[0]
    ret = jnp.transpose(g_main[:, :C_Z].reshape(D, RET_HEADS, 4, RET_DK), (0, 2, 1, 3)).reshape(D, C_Z)
    return jnp.concatenate([ret, g_main[:, C_Z:C_GATES], _xbc_original(g_main[:, C_XBC:]), g_dt[:, :SSD_HEADS],
                            g_main[:, C_GATES:C_XBC]], axis=1)


def _mm(pairs, M, N, *, tm, tn, out_dtype, name, tb=False, row_off=None):
    P = len(pairs)
    nks = [K // tk for (_, _, _, _, _, K, tk) in pairs]
    starts = [int(s) for s in np.cumsum([0] + nks[:-1])]
    KT = int(sum(nks))
    npf = 0 if row_off is None else 1
    in_specs, args = [], []
    for (a, a_cb, b, b_kb, b_nb, K, tk), s, nk in zip(pairs, starts, nks):
        def kk(k, s=s, nk=nk):
            return jnp.clip(k - s, 0, nk - 1)
        in_specs.append(pl.BlockSpec((tm, tk), lambda m, n, k, *pf, kk=kk, a_cb=a_cb: (m + (pf[0][0] if pf else 0), a_cb + kk(k))))
        if tb:
            in_specs.append(pl.BlockSpec((tn, tk), lambda m, n, k, *pf, kk=kk, b_kb=b_kb, b_nb=b_nb: (b_nb + n, b_kb + kk(k))))
        else:
            in_specs.append(pl.BlockSpec((tk, tn), lambda m, n, k, *pf, kk=kk, b_kb=b_kb, b_nb=b_nb: (b_kb + kk(k), b_nb + n)))
        args += [a, b]

    def body(*refs):
        refs = refs[npf:]
        o_ref = refs[2 * P]
        k = pl.program_id(2)

        def prod(i):
            a = refs[2 * i][...].astype(bf16)
            b = refs[2 * i + 1][...].astype(bf16)
            return _nt(a, b) if tb else _nn(a, b)

        if KT == 1:
            o_ref[...] = prod(0).astype(out_dtype)
            return
        acc = refs[2 * P + 1]

        @pl.when(k == 0)
        def _():
            acc[...] = jnp.zeros_like(acc)

        for i in range(P):
            @pl.when((k >= starts[i]) & (k < starts[i] + nks[i]))
            def _(i=i):
                acc[...] += prod(i)

        @pl.when(k == KT - 1)
        def _():
            o_ref[...] = acc[...].astype(out_dtype)

    grid_spec = pltpu.PrefetchScalarGridSpec(
        num_scalar_prefetch=npf, grid=(M // tm, N // tn, KT), in_specs=in_specs,
        out_specs=pl.BlockSpec((tm, tn), lambda m, n, k, *pf: (m, n)),
        scratch_shapes=[] if KT == 1 else [pltpu.VMEM((tm, tn), f32)])
    return pl.pallas_call(
        body, name=name, grid_spec=grid_spec, out_shape=jax.ShapeDtypeStruct((M, N), out_dtype),
        compiler_params=_params(("parallel", "parallel", "arbitrary")),
    )(*([] if row_off is None else [row_off]), *args)


def _mm1(a, b, *, tm, tn, tk, out_dtype, name, tb=False):
    M, K = a.shape
    N = b.shape[0] if tb else b.shape[1]
    return _mm([(a, 0, b, 0, 0, K, tk)], M, N, tm=tm, tn=tn, out_dtype=out_dtype, name=name, tb=tb)


RS = 16
CS = 32


def _for_strips(n_rows, fn, rs=RS, unroll=4):
    def step(s, carry):
        fn(pl.ds(pl.multiple_of(s * rs, rs), rs))
        return carry
    n = n_rows // rs
    lax.fori_loop(0, n, step, 0, unroll=min(unroll, n))


def _norm1_fwd(x, w, tr):
    S, D = x.shape

    def body(x_ref, w_ref, h_ref, ht_ref):
        def strip(rows):
            xv = x_ref[rows, :]
            r = lax.rsqrt(jnp.mean(xv * xv, axis=-1, keepdims=True) + EPS)
            h_ref[rows, :] = (xv * r * w_ref[...]).astype(bf16)
        _for_strips(tr, strip)
        ht_ref[...] = h_ref[...].T

    return pl.pallas_call(
        body, name="norm1_fwd", grid=(S // tr,),
        in_specs=[pl.BlockSpec((tr, D), lambda i: (i, 0)), pl.BlockSpec((1, D), lambda i: (0, 0))],
        out_specs=[pl.BlockSpec((tr, D), lambda i: (i, 0)), pl.BlockSpec((D, tr), lambda i: (0, i))],
        out_shape=[jax.ShapeDtypeStruct((S, D), bf16), jax.ShapeDtypeStruct((D, S), bf16)], compiler_params=_params(("parallel",)),
    )(x, w)


def _norm1_bwd(x, w, dh, dx2, tr):
    S, D = x.shape

    def body(x_ref, w_ref, dh_ref, dx2_ref, gx_ref, gw_ref, acc):
        @pl.when(pl.program_id(0) == 0)
        def _():
            acc[...] = jnp.zeros_like(acc)

        def strip(rows):
            xv = x_ref[rows, :]
            r = lax.rsqrt(jnp.mean(xv * xv, axis=-1, keepdims=True) + EPS)
            xh = xv * r
            dhv = dh_ref[rows, :]
            acc[...] += dhv * xh
            dxh = dhv * w_ref[...]
            gx_ref[rows, :] = dx2_ref[rows, :] + r * (dxh - xh * jnp.mean(dxh * xh, axis=-1, keepdims=True))
        _for_strips(tr, strip)

        @pl.when(pl.program_id(0) == S // tr - 1)
        def _():
            gw_ref[...] = jnp.sum(acc[...], axis=0, keepdims=True)

    row = pl.BlockSpec((tr, D), lambda i: (i, 0))
    vec = pl.BlockSpec((1, D), lambda i: (0, 0))
    return pl.pallas_call(
        body, name="norm1_bwd", grid=(S // tr,), in_specs=[row, vec, row, row], out_specs=[row, vec],
        out_shape=[jax.ShapeDtypeStruct((S, D), f32), jax.ShapeDtypeStruct((1, D), f32)],
        scratch_shapes=[pltpu.VMEM((RS, D), f32)], compiler_params=_params(("arbitrary",)),
    )(x, w, dh, dx2)


def _final_fwd_bwd(x, mo, target, wf, tr):
    S, D = x.shape

    def body(x_ref, mo_ref, t_ref, w_ref, dx2_ref, dx2b_ref, loss_ref, gw_ref, acc, lacc):
        @pl.when(pl.program_id(0) == 0)
        def _():
            acc[...] = jnp.zeros_like(acc)
            lacc[...] = jnp.zeros_like(lacc)

        def strip(rows):
            x2 = x_ref[rows, :] + mo_ref[rows, :]
            r = lax.rsqrt(jnp.mean(x2 * x2, axis=-1, keepdims=True) + EPS)
            xh = x2 * r
            wv = w_ref[...]
            err = xh * wv - t_ref[rows, :]
            lacc[...] += jnp.mean(err * err, axis=-1, keepdims=True)
            dy = err * (1.0 / D)
            acc[...] += dy * xh
            dxh = dy * wv
            dx2 = r * (dxh - xh * jnp.mean(dxh * xh, axis=-1, keepdims=True))
            dx2_ref[rows, :] = dx2
            dx2b_ref[rows, :] = dx2.astype(bf16)
        _for_strips(tr, strip)

        @pl.when(pl.program_id(0) == S // tr - 1)
        def _():
            gw_ref[...] = jnp.sum(acc[...], axis=0, keepdims=True)
            loss_ref[...] = 0.5 * jnp.sum(lacc[...], axis=0, keepdims=True)

    row = pl.BlockSpec((tr, D), lambda i: (i, 0))
    vec = pl.BlockSpec((1, D), lambda i: (0, 0))
    return pl.pallas_call(
        body, name="final_norm_loss", grid=(S // tr,), in_specs=[row, row, row, vec],
        out_specs=[row, row, pl.BlockSpec((1, 1), lambda i: (0, 0)), vec],
        out_shape=[jax.ShapeDtypeStruct((S, D), f32), jax.ShapeDtypeStruct((S, D), bf16), jax.ShapeDtypeStruct((1, 1), f32),
                   jax.ShapeDtypeStruct((1, D), f32)],
        scratch_shapes=[pltpu.VMEM((RS, D), f32), pltpu.VMEM((RS, 1), f32)], compiler_params=_params(("arbitrary",)),
    )(x, mo, target, wf)


def _merge_fwd(p_r, p_s, proj, tr):
    S, D = p_r.shape

    def body(pr_ref, ps_ref, g_ref, o_ref, ot_ref):
        def strip(rows):
            gr, gs = g_ref[rows, pl.ds(0, D)].astype(f32), g_ref[rows, pl.ds(D, D)].astype(f32)
            o_ref[rows, :] = (jax.nn.sigmoid(gr) * pr_ref[rows, :] + jax.nn.sigmoid(gs) * ps_ref[rows, :]).astype(bf16)
        _for_strips(tr, strip)
        ot_ref[...] = o_ref[...].T

    row = pl.BlockSpec((tr, D), lambda i: (i, 0))
    return pl.pallas_call(
        body, name="merge_fwd", grid=(S // tr,),
        in_specs=[row, row, pl.BlockSpec((tr, 2 * D), lambda i: (i, C_GATES // (2 * D)))],
        out_specs=[row, pl.BlockSpec((D, tr), lambda i: (0, i))],
        out_shape=[jax.ShapeDtypeStruct((S, D), bf16), jax.ShapeDtypeStruct((D, S), bf16)], compiler_params=_params(("parallel",)),
    )(p_r, p_s, proj)


def _merge_bwd(dm, p_r, p_s, proj, tr):
    S, D = p_r.shape

    def body(dm_ref, pr_ref, ps_ref, g_ref, dpr_ref, dps_ref, dproj_ref):
        def strip(rows):
            dmv = dm_ref[rows, :]
            sr = jax.nn.sigmoid(g_ref[rows, pl.ds(0, D)].astype(f32))
            ss = jax.nn.sigmoid(g_ref[rows, pl.ds(D, D)].astype(f32))
            dpr_ref[rows, :] = (dmv * sr).astype(bf16)
            dps_ref[rows, :] = (dmv * ss).astype(bf16)
            dproj_ref[rows, pl.ds(0, D)] = (dmv * pr_ref[rows, :] * sr * (1.0 - sr)).astype(bf16)
            dproj_ref[rows, pl.ds(D, D)] = (dmv * ps_ref[rows, :] * ss * (1.0 - ss)).astype(bf16)
        _for_strips(tr, strip)

    row = pl.BlockSpec((tr, D), lambda i: (i, 0))
    gates = pl.BlockSpec((tr, 2 * D), lambda i: (i, C_GATES // (2 * D)))
    o = jax.ShapeDtypeStruct((S, D), bf16)
    return pl.pallas_call(
        body, name="merge_bwd", grid=(S // tr,), in_specs=[row, row, row, gates],
        out_specs=[row, row, gates], out_shape=[o, o, jax.ShapeDtypeStruct((S, N_MAIN), bf16)],
        compiler_params=_params(("parallel",)),
    )(dm, p_r, p_s, proj)


def _ssd_norm_fwd(y, proj, w, tr):
    S, W = y.shape

    def body(y_ref, z_ref, w_ref, o_ref, ot_ref):
        def strip(rows):
            u = y_ref[rows, :] * _silu(z_ref[rows, :].astype(f32))
            r = lax.rsqrt(jnp.mean(u * u, axis=-1, keepdims=True) + EPS)
            o_ref[rows, :] = (u * r * w_ref[...]).astype(bf16)
        _for_strips(tr, strip)
        ot_ref[...] = o_ref[...].T

    row = pl.BlockSpec((tr, W), lambda i: (i, 0))
    return pl.pallas_call(
        body, name="ssd_norm_fwd", grid=(S // tr,),
        in_specs=[row, pl.BlockSpec((tr, W), lambda i: (i, C_Z // W)), pl.BlockSpec((1, W), lambda i: (0, 0))],
        out_specs=[row, pl.BlockSpec((W, tr), lambda i: (0, i))],
        out_shape=[jax.ShapeDtypeStruct((S, W), bf16), jax.ShapeDtypeStruct((W, S), bf16)], compiler_params=_params(("parallel",)),
    )(y, proj, w)


def _ssd_norm_bwd(y, proj, w, dys, dproj, tr):
    S, W = y.shape

    def body(y_ref, z_ref, w_ref, d_ref, _, dy_ref, dz_ref, gw_ref, acc):
        @pl.when(pl.program_id(0) == 0)
        def _():
            acc[...] = jnp.zeros_like(acc)

        def strip(rows):
            yv, zv, dv = y_ref[rows, :], z_ref[rows, :].astype(f32), d_ref[rows, :]
            sg = jax.nn.sigmoid(zv)
            sz = zv * sg
            u = yv * sz
            r = lax.rsqrt(jnp.mean(u * u, axis=-1, keepdims=True) + EPS)
            un = u * r
            acc[...] += dv * un
            dun = dv * w_ref[...]
            du = r * (dun - un * jnp.mean(dun * un, axis=-1, keepdims=True))
            dy_ref[rows, :] = (du * sz).astype(bf16)
            dz_ref[rows, :] = (du * yv * (sg * (1.0 + zv * (1.0 - sg)))).astype(bf16)
        _for_strips(tr, strip)

        @pl.when(pl.program_id(0) == S // tr - 1)
        def _():
            gw_ref[...] = jnp.sum(acc[...], axis=0, keepdims=True)

    row = pl.BlockSpec((tr, W), lambda i: (i, 0))
    zcol = pl.BlockSpec((tr, W), lambda i: (i, C_Z // W))
    vec = pl.BlockSpec((1, W), lambda i: (0, 0))
    return pl.pallas_call(
        body, name="ssd_norm_bwd", grid=(S // tr,),
        in_specs=[row, zcol, vec, row, ANY], out_specs=[row, zcol, vec],
        out_shape=[jax.ShapeDtypeStruct((S, W), bf16), jax.ShapeDtypeStruct(dproj.shape, bf16), jax.ShapeDtypeStruct((1, W), f32)],
        input_output_aliases={4: 1}, scratch_shapes=[pltpu.VMEM((RS, W), f32)], compiler_params=_params(("arbitrary",)),
    )(y, proj, w, dys, dproj)


def _rope(t, cos, sin):
    t1, t2 = t[:, :128], t[:, 128:]
    return jnp.concatenate([t1 * cos - t2 * sin, t2 * cos + t1 * sin], axis=1)


def _rope_t(d, cos, sin):
    d1, d2 = d[:, :128], d[:, 128:]
    return jnp.concatenate([d1 * cos + d2 * sin, d2 * cos - d1 * sin], axis=1)


def _ret_specs(tb, rev_nb=None):
    def blk(i):
        return i if rev_nb is None else rev_nb - 1 - i
    head = pl.BlockSpec((tb, RET_HP * RET_HW), lambda h, i: (blk(i), h))
    tab = pl.BlockSpec((tb, 128), lambda h, i: (blk(i), 0))
    mat = pl.BlockSpec((RET_HP, CHUNK, CHUNK), lambda h, i: (h, 0, 0))
    vec = pl.BlockSpec((RET_HP, CHUNK, 1), lambda h, i: (h, 0, 0))
    one = pl.BlockSpec((RET_HP, 1, 1), lambda h, i: (h, 0, 0))
    own = pl.BlockSpec((tb, RET_HP * RET_DK), lambda h, i: (blk(i), h))
    st = pl.BlockSpec((RET_HP, tb // CHUNK, RET_DK, RET_DK), lambda h, i: (h, blk(i), 0, 0))
    return head, tab, mat, vec, one, own, st


def _ret_fwd(proj, cos, sin, intra, qdec, kdec, cdec, tb):
    S = proj.shape[0]
    nc = S // CHUNK
    scale = RET_DK ** -0.5
    dk = RET_DK

    def body(p_ref, cos_ref, sin_ref, m_ref, qd_ref, kd_ref, cd_ref, y_ref, yr_ref, yrt_ref, st_ref, st):
        @pl.when(pl.program_id(1) == 0)
        def _():
            st[...] = jnp.zeros_like(st)

        def head_chunk(hh, c, rows, cs, sn):
            mm, qd, kd, cd = m_ref[hh], qd_ref[hh], kd_ref[hh], cd_ref[hh]
            col = lambda j: pl.ds(hh * RET_HW + j * dk, dk)
            own = pl.ds(hh * dk, dk)
            qr = _rope(p_ref[rows, col(0)].astype(f32), cs, sn)
            kr = _rope(p_ref[rows, col(1)].astype(f32), cs, sn) * scale
            qb, kb, vb = qr.astype(bf16), kr.astype(bf16), p_ref[rows, col(2)].astype(bf16)
            stb = st[hh].astype(bf16)
            st_ref[hh, c] = stb
            sc = (_nt(qb, kb) * mm).astype(bf16)
            y = _nn(sc, vb) + _nn(qb, stb) * qd
            st[hh] = st[hh] * cd + _tn((kr * kd).astype(bf16), vb)
            y_ref[rows, own] = y
            mu = jnp.mean(y, axis=-1, keepdims=True)
            yc = y - mu
            var = jnp.mean(yc * yc, axis=-1, keepdims=True)
            yr_ref[rows, own] = (yc * lax.rsqrt(var + EPS) * _silu(p_ref[rows, col(3)].astype(f32))).astype(bf16)

        def chunk(c, carry):
            rows = pl.ds(pl.multiple_of(c * CHUNK, CHUNK), CHUNK)
            cs, sn = cos_ref[rows, :], sin_ref[rows, :]
            for hh in range(RET_HP):
                head_chunk(hh, c, rows, cs, sn)
            return carry

        lax.fori_loop(0, tb // CHUNK, chunk, 0, unroll=min(RET_UNROLL_FWD, tb // CHUNK))
        yrt_ref[...] = yr_ref[...].T

    head, tab, mat, vec, one, own, stspec = _ret_specs(tb)
    return pl.pallas_call(
        body, name="ret_fwd", grid=(RET_HEADS // RET_HP, S // tb),
        in_specs=[head, tab, tab, mat, vec, vec, one],
        out_specs=[own, own, pl.BlockSpec((RET_HP * RET_DK, tb), lambda h, i: (h, i)), stspec],
        out_shape=[jax.ShapeDtypeStruct((S, 2048), f32), jax.ShapeDtypeStruct((S, 2048), bf16), jax.ShapeDtypeStruct((2048, S), bf16),
                   jax.ShapeDtypeStruct((RET_HEADS, nc, dk, dk), bf16)],
        scratch_shapes=[pltpu.VMEM((RET_HP, dk, dk), f32)], compiler_params=_params(("parallel", "arbitrary")),
    )(proj, cos, sin, intra, qdec, kdec, cdec)


def _ret_bwd(proj, cos, sin, intra, qdec, kdec, cdec, y, dyr, states, dproj, tb):
    S = proj.shape[0]
    nb = S // tb
    nck = tb // CHUNK
    scale = RET_DK ** -0.5
    dk = RET_DK

    def body(p_ref, cos_ref, sin_ref, m_ref, qd_ref, kd_ref, cd_ref, y_ref, dyr_ref, st_ref, _, o_ref, dst):
        @pl.when(pl.program_id(1) == 0)
        def _():
            dst[...] = jnp.zeros_like(dst)

        def head_chunk(hh, c, rows, cs, sn):
            mm, qd, kd, cd = m_ref[hh], qd_ref[hh], kd_ref[hh], cd_ref[hh]
            col = lambda j: pl.ds(hh * RET_HW + j * dk, dk)
            own = pl.ds(hh * dk, dk)
            qr = _rope(p_ref[rows, col(0)].astype(f32), cs, sn)
            kr = _rope(p_ref[rows, col(1)].astype(f32), cs, sn) * scale
            qb, kb, vb = qr.astype(bf16), kr.astype(bf16), p_ref[rows, col(2)].astype(bf16)
            kdb = (kr * kd).astype(bf16)
            stb = st_ref[hh, c]
            yv, gv, dyrv = y_ref[rows, own], p_ref[rows, col(3)].astype(f32), dyr_ref[rows, own]
            mu = jnp.mean(yv, axis=-1, keepdims=True)
            yc = yv - mu
            rstd = lax.rsqrt(jnp.mean(yc * yc, axis=-1, keepdims=True) + EPS)
            yn = yc * rstd
            sg = jax.nn.sigmoid(gv)
            o_ref[rows, col(3)] = (dyrv * yn * (sg * (1.0 + gv * (1.0 - sg)))).astype(bf16)
            dyn = dyrv * (gv * sg)
            dy = rstd * (dyn - jnp.mean(dyn, axis=-1, keepdims=True) - yn * jnp.mean(dyn * yn, axis=-1, keepdims=True))
            dyb = dy.astype(bf16)
            dyqb = (dy * qd).astype(bf16)
            dstb = dst[hh].astype(bf16)
            sct =(_nt(kb, qb) * mm).astype(bf16)
            ds = (_nt(dyb, vb) * mm).astype(bf16)
            dsT = (_nt(vb, dyb) * mm).astype(bf16)
            dv = _nn(sct, dyb) + _nn(kdb, dstb)
            dqr = _nn(ds, kb) + _nt(dyqb, stb)
            dkr = _nn(dsT, qb) + _nt(vb, dstb) * kd
            dst[hh] = dst[hh] * cd + _tn(qb, dyqb)
            o_ref[rows, col(0)] = _rope_t(dqr, cs, sn).astype(bf16)
            o_ref[rows, col(1)] = (_rope_t(dkr, cs, sn) * scale).astype(bf16)
            o_ref[rows, col(2)] = dv.astype(bf16)

        def chunk(cc, carry):
            c = nck - 1 - cc
            rows = pl.ds(pl.multiple_of(c * CHUNK, CHUNK), CHUNK)
            cs, sn = cos_ref[rows, :], sin_ref[rows, :]
            for hh in range(RET_HP):
                head_chunk(hh, c, rows, cs, sn)
            return carry

        lax.fori_loop(0, nck, chunk, 0, unroll=min(RET_UNROLL_BWD, nck))

    head, tab, mat, vec, one, own, stspec = _ret_specs(tb, rev_nb=nb)
    return pl.pallas_call(
        body, name="ret_bwd", grid=(RET_HEADS // RET_HP, nb),
        in_specs=[head, tab, tab, mat, vec, vec, one, own, own, stspec, ANY],
        out_specs=head, out_shape=jax.ShapeDtypeStruct(dproj.shape, bf16), input_output_aliases={10: 0},
        scratch_shapes=[pltpu.VMEM((RET_HP, dk, dk), f32)], compiler_params=_params(("parallel", "arbitrary")),
    )(proj, cos, sin, intra, qdec, kdec, cdec, y, dyr, states, dproj)


def _conv_fwd(proj, conv_w, conv_b, tb, cw):
    S = proj.shape[0]
    off = 0

    def body(x_ref, halo_ref, w_ref, b_ref, o_ref, xe):
        xe[pl.ds(0, 8), :] = jnp.where(pl.program_id(1) == 0, 0.0, halo_ref[...])
        xe[pl.ds(8, CS), :] = x_ref[pl.ds(0, CS), :]
        ws = [w_ref[pl.ds(j, 1), :] for j in range(SSD_CONV)]
        for s in range(tb // CS):
            tap = (lambda j: xe[pl.ds(5 + j, CS), :]) if s == 0 else (lambda j, s=s: x_ref[pl.ds(s * CS - 3 + j, CS), :])
            acc = b_ref[...] + ws[0] * tap(0)
            for j in range(1, SSD_CONV):
                acc = acc + ws[j] * tap(j)
            o_ref[pl.ds(s * CS, CS), :] = acc.astype(bf16)

    return pl.pallas_call(
        body, name="conv_fwd", grid=(CONV_DIM // cw, S // tb),
        in_specs=[pl.BlockSpec((tb, cw), lambda j, i: (i, off + j)),
                  pl.BlockSpec((8, cw), lambda j, i: (jnp.maximum(i * (tb // 8) - 1, 0), off + j)),
                  pl.BlockSpec((SSD_CONV, cw), lambda j, i: (0, j)), pl.BlockSpec((1, cw), lambda j, i: (0, j))],
        out_specs=pl.BlockSpec((tb, cw), lambda j, i: (i, j)),
        out_shape=jax.ShapeDtypeStruct((S, CONV_DIM), bf16),
        scratch_shapes=[pltpu.VMEM((CS + 8, cw), f32)], compiler_params=_params(("parallel", "arbitrary")),
    )(proj, proj, conv_w, conv_b)


def _conv_bwd(dpre, proj, conv_w, dproj, tb, cw):
    S, n = dpre.shape
    nb = S // tb
    xoff = C_XBC // cw

    def body(d_ref, dh_ref, x_ref, xh_ref, w_ref, _, dx_ref, gw_ref, gb_ref, de, xe, accw, accb):
        i = pl.program_id(1)

        @pl.when(i == 0)
        def _():
            accw[...] = jnp.zeros_like(accw)
            accb[...] = jnp.zeros_like(accb)

        ns = tb // CS
        de[pl.ds(0, CS), :] = d_ref[pl.ds(tb - CS, CS), :]
        de[pl.ds(CS, 8), :] = jnp.where(i == nb - 1, 0.0, dh_ref[...])
        xe[pl.ds(0, 8), :] = jnp.where(i == 0, 0.0, xh_ref[...])
        xe[pl.ds(8, CS), :] = x_ref[pl.ds(0, CS), :]
        ws = [w_ref[pl.ds(j, 1), :] for j in range(SSD_CONV)]
        fold = lambda p: sum(p[8 * q:8 * (q + 1)] for q in range(1, CS // 8)) + p[0:8]
        for s in range(ns):
            dv = d_ref[pl.ds(s * CS, CS), :]
            ahead = (lambda o: de[pl.ds(o, CS), :]) if s == ns - 1 else (lambda o, s=s: d_ref[pl.ds(s * CS + o, CS), :])
            xtap = (lambda j: xe[pl.ds(5 + j, CS), :]) if s == 0 else (lambda j, s=s: x_ref[pl.ds(s * CS - 3 + j, CS), :])
            acc = ws[SSD_CONV - 1] * dv
            for j in range(SSD_CONV - 1):
                acc = acc + ws[j] * ahead(3 - j)
            dx_ref[pl.ds(s * CS, CS), :] = acc.astype(bf16)
            accb[...] += fold(dv)
            for j in range(SSD_CONV):
                accw[j] += fold(dv * xtap(j))

        @pl.when(i == nb - 1)
        def _():
            gb_ref[...] = jnp.sum(accb[...], axis=0, keepdims=True)
            for j in range(SSD_CONV):
                gw_ref[pl.ds(j, 1), :] = jnp.sum(accw[j], axis=0, keepdims=True)

    return pl.pallas_call(
        body, name="conv_bwd", grid=(n // cw, nb),
        in_specs=[pl.BlockSpec((tb, cw), lambda j, i: (i, j)),
                  pl.BlockSpec((8, cw), lambda j, i: (jnp.minimum((i + 1) * (tb // 8), S // 8 - 1), j)),
                  pl.BlockSpec((tb, cw), lambda j, i: (i, j)),
                  pl.BlockSpec((8, cw), lambda j, i: (jnp.maximum(i * (tb // 8) - 1, 0), j)),
                  pl.BlockSpec((SSD_CONV, cw), lambda j, i: (0, j)), ANY],
        out_specs=[pl.BlockSpec((tb, cw), lambda j, i: (i, xoff + j)), pl.BlockSpec((SSD_CONV, cw), lambda j, i: (0, j)),
                   pl.BlockSpec((1, cw), lambda j, i: (0, j))],
        out_shape=[jax.ShapeDtypeStruct(dproj.shape, bf16), jax.ShapeDtypeStruct((SSD_CONV, n), f32), jax.ShapeDtypeStruct((1, n), f32)],
        input_output_aliases={5: 0},
        scratch_shapes=[pltpu.VMEM((CS + 8, cw), f32), pltpu.VMEM((CS + 8, cw), f32), pltpu.VMEM((SSD_CONV, 8, cw), f32),
                        pltpu.VMEM((8, cw), f32)],
        compiler_params=_params(("parallel", "arbitrary")),
    )(dpre, dpre, proj, proj, conv_w, dproj)


def _dt_prep(dt_raw, dt_bias, a_log, tb):
    S = dt_raw.shape[0]

    def body(r_ref, b_ref, al_ref, dt_ref, sg_ref, ac_ref):
        li = lax.broadcasted_iota(jnp.int32, (LS, LS), 0)
        si = lax.broadcasted_iota(jnp.int32, (LS, LS), 1)
        tri = (li >= si).astype(f32)
        neg_a = -jnp.exp(al_ref[...])
        for c in range(tb // LS):
            rows = pl.ds(c * LS, LS)
            xv = r_ref[rows, :] + b_ref[...]
            dtv = jax.nn.softplus(xv)
            dt_ref[rows, :] = dtv
            sg_ref[rows, :] = jax.nn.sigmoid(xv)
            ac_ref[rows, :] = _hi(tri, dtv * neg_a)

    row = pl.BlockSpec((tb, 128), lambda i: (i, 0))
    vec = pl.BlockSpec((1, 128), lambda i: (0, 0))
    o = jax.ShapeDtypeStruct((S, 128), f32)
    return pl.pallas_call(body, name="dt_prep", grid=(S // tb,), in_specs=[row, vec, vec], out_specs=[row, row, row],
                          out_shape=[o, o, o], compiler_params=_params(("parallel",)))(dt_raw, dt_bias, a_log)


def _group_major(t):
    S = t.shape[0]
    return jnp.transpose(t[:, :SSD_HEADS].reshape(S, SSD_GROUPS, SSD_HPG), (1, 0, 2))


def _group_major_t(t):
    S = t.shape[0]
    return jnp.transpose(t[:, :SSD_HEADS].reshape(S // LS, LS, SSD_GROUPS, SSD_HPG), (2, 0, 3, 1))


def _ssd_specs(tb, rev_nb=None):
    def blk(i):
        return i if rev_nb is None else rev_nb - 1 - i
    grp = pl.BlockSpec((tb, SSD_GC), lambda g, i: (blk(i), g))
    xs = pl.BlockSpec((tb, SSD_GW), lambda g, i: (blk(i), g))
    ph = pl.BlockSpec((1, tb, SSD_HPG), lambda g, i: (g, blk(i), 0))
    pht = pl.BlockSpec((1, tb // LS, SSD_HPG, LS), lambda g, i: (g, blk(i), 0, 0))
    gvec = pl.BlockSpec((1, 1, SSD_GW), lambda g, i: (g, 0, 0))
    ex = pl.BlockSpec((SSD_HPG, SSD_GW), lambda g, i: (0, 0))
    st = pl.BlockSpec((1, tb // LS, SSD_STATE, SSD_GW), lambda g, i: (g, blk(i), 0, 0))
    return grp, xs, ph, pht, gvec, ex, st


def _expander():
    return jnp.repeat(jnp.eye(SSD_HPG, dtype=f32), SSD_GW // SSD_HPG, axis=1).astype(bf16)


def _expand3(dt8, ac8, ex):
    stack = jnp.concatenate([dt8, jnp.exp(ac8), jnp.exp(ac8[LS - 1:LS, :] - ac8)], axis=0)
    wide = _sel_r(stack, ex)
    return wide[0:LS], wide[LS:2 * LS], wide[2 * LS:3 * LS]


def _ssd_fwd(pre, dt_g, ac_g, act_g, dskx, tb):
    S = pre.shape[0]
    nc = S // LS
    hd = SSD_GW // SSD_HPG

    def body(p_ref, dt_ref, ac_ref, act_ref, dsk_ref, ex_ref, y_ref, st_ref, st):
        @pl.when(pl.program_id(1) == 0)
        def _():
            st[...] = jnp.zeros_like(st)

        ex = ex_ref[...]
        li = lax.broadcasted_iota(jnp.int32, (LS, LS), 0)
        si = lax.broadcasted_iota(jnp.int32, (LS, LS), 1)
        causal = li >= si

        def chunk(c, carry):
            rows = pl.ds(pl.multiple_of(c * LS, LS), LS)
            xs = _silu(p_ref[rows, pl.ds(0, SSD_GW)].astype(f32))
            bcb = _silu(p_ref[rows, pl.ds(SSD_GW, SSD_STATE)].astype(f32)).astype(bf16)
            ccb = _silu(p_ref[rows, pl.ds(SSD_GW + SSD_STATE, SSD_STATE)].astype(f32)).astype(bf16)
            dt8, ac8, act = dt_ref[0, rows, :], ac_ref[0, rows, :], act_ref[0, c]
            dtx, eax, tailx = _expand3(dt8, ac8, ex)
            xdt = xs * dtx
            cb = _nt(ccb, bcb)
            stb = st[...].astype(bf16)
            st_ref[0, c] = stb
            xdtb = xdt.astype(bf16)
            outs = []
            for h in range(SSD_HPG):
                dec = jnp.exp(jnp.where(causal, ac8[:, h:h + 1] - act[h:h + 1, :], -1e30))
                outs.append(_nn((cb * dec).astype(bf16), xdtb[:, hd * h:hd * (h + 1)]))
            y_ref[rows, :] = (jnp.concatenate(outs, axis=1) + _nn(ccb, stb) * eax + dsk_ref[0] * xs).astype(bf16)
            st[...] = st[...] * eax[LS - 1:LS, :] + _tn(bcb, (xdt * tailx).astype(bf16))
            return carry

        lax.fori_loop(0, tb // LS, chunk, 0, unroll=min(4, tb // LS))

    grp, xs, ph, pht, gvec, ex, stspec = _ssd_specs(tb)
    return pl.pallas_call(
        body, name="ssd_fwd", grid=(SSD_GROUPS, S // tb),
        in_specs=[grp, ph, ph, pht, gvec, ex], out_specs=[xs, stspec],
        out_shape=[jax.ShapeDtypeStruct((S, SSD_WIDTH), bf16), jax.ShapeDtypeStruct((SSD_GROUPS, nc, SSD_STATE, SSD_GW), bf16)],
        scratch_shapes=[pltpu.VMEM((SSD_STATE, SSD_GW), f32)], compiler_params=_params(("parallel", "arbitrary")),
    )(pre, dt_g, ac_g, act_g, dskx, _expander())


def _ssd_bwd(pre, dt_g, ac_g, act_g, sg_g, dskx, nega_g, dy, states, tb):
    S = pre.shape[0]
    nb = S // tb
    nck = tb // LS
    hd = SSD_GW // SSD_HPG

    def body(p_ref, dt_ref, ac_ref, act_ref, sg_ref, dsk_ref, na_ref, ex_ref, ext_ref, dy_ref, st_ref,
             dp_ref, ddt_ref, gsk_ref, gal_ref, gdb_ref, dst, skacc):
        @pl.when(pl.program_id(1) == 0)
        def _():
            dst[...] = jnp.zeros_like(dst)
            skacc[...] = jnp.zeros_like(skacc)
            gal_ref[...] = jnp.zeros_like(gal_ref)
            gdb_ref[...] = jnp.zeros_like(gdb_ref)

        ex, ext = ex_ref[...], ext_ref[...]
        li = lax.broadcasted_iota(jnp.int32, (LS, LS), 0)
        si = lax.broadcasted_iota(jnp.int32, (LS, LS), 1)
        causal = li >= si
        anti = si >= li
        upper = anti.astype(bf16)
        eye = (si == li).astype(bf16)
        last_row = (lax.broadcasted_iota(jnp.int32, (LS, 1), 0) == LS - 1).astype(f32)
        head_id = lax.broadcasted_iota(jnp.int32, (1, SSD_HPG), 1)
        head_col = lax.broadcasted_iota(jnp.int32, (SSD_HPG, 1), 0)
        neg_a = na_ref[0]
        dskv = dsk_ref[0]

        def chunk(cc, carry):
            c = nck - 1 - cc
            rows = pl.ds(pl.multiple_of(c * LS, LS), LS)
            px = p_ref[rows, pl.ds(0, SSD_GW)].astype(f32)
            pb = p_ref[rows, pl.ds(SSD_GW, SSD_STATE)].astype(f32)
            pc = p_ref[rows, pl.ds(SSD_GW + SSD_STATE, SSD_STATE)].astype(f32)
            sgx, sgb, sgc = jax.nn.sigmoid(px), jax.nn.sigmoid(pb), jax.nn.sigmoid(pc)
            xs = px * sgx
            bcb = (pb * sgb).astype(bf16)
            ccb = (pc * sgc).astype(bf16)
            dt8, ac8, act = dt_ref[0, rows, :], ac_ref[0, rows, :], act_ref[0, c]
            dtx, eax, tailx = _expand3(dt8, ac8, ex)
            xdt = xs * dtx
            ex_last = eax[LS - 1:LS, :]
            stb = st_ref[0, c]
            dyv = dy_ref[rows, :]
            dyb = dyv.astype(bf16)
            xdtb = xdt.astype(bf16)
            skacc[...] += jnp.sum(dyv * xs, axis=0, keepdims=True)
            yinter = _nn(ccb, stb) * eax
            dzb = (dyv * eax).astype(bf16)
            dcc = _nt(dzb, stb)
            dstv = dst[...]
            dstb = dstv.astype(bf16)
            xt = xdt * tailx
            dxt = _nn(bcb, dstb)
            dbc = _nt(xt.astype(bf16), dstb)
            dxdt = dxt * tailx
            lastrow = jnp.sum(dxt * xt, axis=0, keepdims=True) + jnp.sum(dstv * stb.astype(f32), axis=0, keepdims=True) * ex_last
            dst[...] = dstv * ex_last + _tn(ccb, dzb)
            cb = _nt(ccb, bcb)
            cbt = _nt(bcb, ccb)
            dcb = jnp.zeros((LS, LS), f32)
            dac8 = jnp.zeros((LS, SSD_HPG), f32)
            dact = jnp.zeros((SSD_HPG, LS), f32)
            dxin = []
            for h in range(SSD_HPG):
                sl = slice(hd * h, hd * (h + 1))
                col, rowv = ac8[:, h:h + 1], act[h:h + 1, :]
                dec = jnp.exp(jnp.where(causal, col - rowv, -1e30))
                dect = jnp.exp(jnp.where(anti, rowv - col, -1e30))
                gm = cb * dec
                dgm = _nt(dyb[:, sl], xdtb[:, sl])
                dxin.append(_nn((cbt * dect).astype(bf16), dyb[:, sl]))
                dcb = dcb + dgm * dec
                w = dgm * gm
                dac8 = dac8 + jnp.sum(w, axis=1, keepdims=True) * (head_id == h).astype(f32)
                dact = dact + (head_col == h).astype(f32) * jnp.sum(w, axis=0, keepdims=True)
            dxintra = jnp.concatenate(dxin, axis=1)
            dcbb = dcb.astype(bf16)
            dcc = dcc + _nn(dcbb, bcb)
            dbc = dbc + _tn(dcbb, ccb)
            dxdt = dxdt + dxintra
            dacx = dyv * yinter - dxt * xt + last_row * lastrow
            red = _sel_r(jnp.concatenate([dacx, dxdt * xs], axis=0), ext)
            dac8 = dac8 - _rows_to_cols(dact, eye) + red[0:LS]
            da8 = _sel_l(upper, dac8)
            ddt8 = red[LS:2 * LS] + da8 * neg_a
            gal_ref[0] += jnp.sum(da8 * dt8 * neg_a, axis=0, keepdims=True)
            ddr = ddt8 * sg_ref[0, rows, :]
            ddt_ref[0, rows, :] = ddr
            gdb_ref[0] += jnp.sum(ddr, axis=0, keepdims=True)
            dsilu = lambda p, s: s * (1.0 + p * (1.0 - s))
            dp_ref[rows, pl.ds(0, SSD_GW)] = (dskv * dyv + dxdt * dtx) * dsilu(px, sgx)
            dp_ref[rows, pl.ds(SSD_GW, SSD_STATE)] = dbc * dsilu(pb, sgb)
            dp_ref[rows, pl.ds(SSD_GW + SSD_STATE, SSD_STATE)] = dcc * dsilu(pc, sgc)
            return carry

        lax.fori_loop(0, nck, chunk, 0, unroll=min(4, nck))

        @pl.when(pl.program_id(1) == nb - 1)
        def _():
            gsk_ref[0] = skacc[...]

    grp, xs, ph, pht, gvec, ex, stspec = _ssd_specs(tb, rev_nb=nb)
    small = pl.BlockSpec((1, 1, SSD_HPG), lambda g, i: (g, 0, 0))
    ext = pl.BlockSpec((SSD_GW, SSD_HPG), lambda g, i: (0, 0))
    sm = jax.ShapeDtypeStruct((SSD_GROUPS, 1, SSD_HPG), f32)
    expander = _expander()
    return pl.pallas_call(
        body, name="ssd_bwd", grid=(SSD_GROUPS, nb),
        in_specs=[grp, ph, ph, pht, ph, gvec, small, ex, ext, xs, stspec],
        out_specs=[grp, ph, gvec, small, small],
        out_shape=[jax.ShapeDtypeStruct((S, CONV_DIM), f32), jax.ShapeDtypeStruct((SSD_GROUPS, S, SSD_HPG), f32),
                   jax.ShapeDtypeStruct((SSD_GROUPS, 1, SSD_GW), f32), sm, sm],
        scratch_shapes=[pltpu.VMEM((SSD_STATE, SSD_GW), f32), pltpu.VMEM((1, SSD_GW), f32)],
        compiler_params=_params(("parallel", "arbitrary")),
    )(pre, dt_g, ac_g, act_g, sg_g, dskx, nega_g, expander, expander.T, dy, states)


def _tiles(S):
    return dict(tb=min(512, S), tr=min(256, S), tm=min(1024, S))


def _local_step(x, positions, target, norm1_w, w_main, w_dt, conv_w, conv_b, dt_bias, a_log, d_skip, ssd_norm_w,
                late_token, late_weights, norm_f_w, reducer):
    S, D = x.shape
    t = _tiles(S)
    tb, tr, tm = t["tb"], t["tr"], t["tm"]

    half = RET_DK // 2
    inv_freq = ROPE_THETA ** (-jnp.arange(half, dtype=f32) / half)
    ang = positions.astype(f32)[:, None] * inv_freq
    cos, sin = jnp.cos(ang), jnp.sin(ang)
    log_gamma = jnp.log1p(-(2.0 ** (-5.0 - jnp.arange(RET_HEADS, dtype=f32))))
    idx = jnp.arange(CHUNK, dtype=f32)
    intra = jnp.exp(jnp.abs(idx[:, None] - idx[None, :]) * log_gamma[:, None, None])
    qdec = jnp.exp((idx + 1.0)[None, :] * log_gamma[:, None])[:, :, None]
    kdec = jnp.exp((CHUNK - 1.0 - idx)[None, :] * log_gamma[:, None])[:, :, None]
    cdec = jnp.exp(CHUNK * log_gamma)[:, None, None]
    conv_wm, conv_bm = _xbc_group_major(conv_w), _xbc_group_major(conv_b)

    h, ht = _norm1_fwd(x, norm1_w + late_token[0, 0], tr)
    proj = _mm([(h, 0, w_main, 0, 0, D, D)], S, C_XBC, tm=tm, tn=1024, out_dtype=bf16, name="proj_main")
    proj_x = _mm([(h, 0, w_main, 0, C_XBC // 1024, D, D)], S, CONV_DIM, tm=tm, tn=1024, out_dtype=f32, name="proj_xbc")
    dt_raw = _mm1(h, w_dt, tm=tm, tn=128, tk=D, out_dtype=f32, name="proj_dt")
    y_ret, yr, yrt, ret_states = _ret_fwd(proj, cos, sin, intra, qdec, kdec, cdec, tb)
    pre = _conv_fwd(proj_x, conv_wm, conv_bm, min(1024, S), 512)
    pad64 = lambda v: jnp.pad(v, ((0, 0), (0, 128 - SSD_HEADS)))
    dt, sg, ac = _dt_prep(dt_raw, pad64(dt_bias), pad64(a_log), tb)
    dt_g, ac_g, sg_g, act_g = _group_major(dt), _group_major(ac), _group_major(sg), _group_major_t(ac)
    dskx = jnp.repeat(d_skip.reshape(SSD_GROUPS, 1, SSD_HPG), SSD_GW // SSD_HPG, axis=2)
    nega_g = (-jnp.exp(a_log)).reshape(SSD_GROUPS, 1, SSD_HPG)
    y_ssd, ssd_states = _ssd_fwd(pre, dt_g, ac_g, act_g, dskx, tb)
    ys, yst = _ssd_norm_fwd(y_ssd, proj, ssd_norm_w, tr // 2)
    w_br, w_bs, w_o = late_weights(ys)
    p_r = _mm1(yr, w_br, tm=tm, tn=1024, tk=2048, out_dtype=bf16, name="branch_ret")
    p_s = _mm1(ys, w_bs, tm=tm, tn=1024, tk=4096, out_dtype=bf16, name="branch_ssd")
    merged, mergedt = _merge_fwd(p_r, p_s, proj, tr)
    mo = _mm1(merged, w_o, tm=tm, tn=1024, tk=2048, out_dtype=bf16, name="out_proj")
    dx2, dx2b, loss, g_norm_f = _final_fwd_bwd(x, mo, target, norm_f_w.reshape(1, D), tr)

    tkt = min(4096, S)
    wg = lambda at, b, name, tn=1024: _mm1(at, b, tm=min(1024, at.shape[0]), tn=tn, tk=tkt, out_dtype=f32, name=name)
    dm = _mm1(dx2b, w_o, tm=tm, tn=1024, tk=2048, out_dtype=bf16, name="d_merged", tb=True)
    g_w_o = wg(mergedt, dx2b, "g_w_out")
    dp_r, dp_s, dproj = _merge_bwd(dm, p_r, p_s, proj, tr)
    dyr = _mm1(dp_r, w_br, tm=tm, tn=1024, tk=2048, out_dtype=bf16, name="d_yr", tb=True)
    dys = _mm1(dp_s, w_bs, tm=tm, tn=1024, tk=2048, out_dtype=bf16, name="d_ys", tb=True)
    g_w_br = wg(yrt, dp_r, "g_w_br_ret")
    g_w_bs = wg(yst, dp_s, "g_w_br_ssd")
    dy_ssd, dproj, g_ssd_norm = _ssd_norm_bwd(y_ssd, proj, ssd_norm_w, dys, dproj, tr // 2)
    dproj = _ret_bwd(proj, cos, sin, intra, qdec, kdec, cdec, y_ret, dyr, ret_states, dproj, tb)
    dpre, ddt_g, gsk, gal, gdb = _ssd_bwd(pre, dt_g, ac_g, act_g, sg_g, dskx, nega_g, dy_ssd, ssd_states, tb)
    dproj, gcw, gcb = _conv_bwd(dpre, proj_x, conv_wm, dproj, min(1024, S), 512)
    ddt = jnp.transpose(ddt_g, (1, 0, 2)).reshape(S, SSD_HEADS)
    ddt_p = jnp.pad(ddt, ((0, 0), (0, 128 - SSD_HEADS))).astype(bf16)

    hr = D // 2
    wg_half = lambda off, b, name, tn=1024: _mm([(ht, 0, b, 0, 0, S, tkt)], hr, b.shape[1], tm=hr, tn=tn, out_dtype=f32,
                                                 name=name, row_off=off)
    off_sib, off_own = reducer.halves()
    gs_main = wg_half(off_sib, dproj, "g_w_in_main_sib")
    gs_dt = wg_half(off_sib, ddt_p, "g_w_in_dt_sib", tn=128)
    swap_state = reducer.first(gs_main, gs_dt, g_w_br, g_w_bs, g_w_o)
    ddt_p = ddt_p + swap_state[-1][0, 0].astype(bf16)
    go_main = wg_half(off_own, dproj, "g_w_in_main_own")
    go_dt = wg_half(off_own, ddt_p, "g_w_in_dt_own", tn=128)
    reduce_state = reducer.second(swap_state, go_main, go_dt)
    ddt_p = ddt_p + reduce_state[-1][0, 0].astype(bf16)
    dh = _mm([(dproj, 0, w_main, 0, 0, N_MAIN, N_MAIN // 8), (ddt_p, 0, w_dt, 0, 0, 128, 128)], S, D, tm=tm, tn=1024,
             out_dtype=bf16, name="d_h", tb=True)
    grad_x, g_norm1 = _norm1_bwd(x, norm1_w, dh, dx2, tr)

    seg = lambda v: jnp.sum(v.reshape(SSD_HEADS, SSD_GW // SSD_HPG), axis=1).reshape(1, SSD_HEADS)
    grads = dict(
        norm1_w=g_norm1, w_in_main=(gs_main, go_main), w_in_dt=(gs_dt, go_dt),
        conv_w=_xbc_original(gcw), conv_b=_xbc_original(gcb),
        dt_bias=gdb.reshape(1, SSD_HEADS), a_log=gal.reshape(1, SSD_HEADS), d_skip=seg(gsk),
        ssd_norm_w=g_ssd_norm, w_br_ret=g_w_br, w_br_ssd=g_w_bs, w_out=g_w_o, norm_f_w=g_norm_f,
    )
    return loss, grad_x, grads, reduce_state


def _me():
    return lax.axis_index("x"), lax.axis_index("y"), lax.axis_index("c")


def _other_chips(x, y):
    return [(1 - x, y), (x, 1 - y), (1 - x, 1 - y)]


def _gather_weights(a, cw):
    R = a.shape[0]
    hr, hq = R // 2, R // 4

    def body(a_ref, cw_ref, ga_ref, gc_ref, send_sems, recv_sems):
        x, y, c = _me()
        me, sibling = (x, y, c), (x, y, 1 - c)
        nx, ny = (1 - x, y, c), (x, 1 - y, c)
        k, kx, ky, kd = 2 * x + y, 2 * (1 - x) + y, 2 * x + (1 - y), 2 * (1 - x) + (1 - y)

        def rows(half, q):
            return pl.ds(pl.multiple_of(half * hr + q * hq, 8), hq)

        def cp(sem, shard, half, q, to, src=None):
            dst = ga_ref.at[shard, rows(half, q), :]
            return pltpu.make_async_remote_copy(src_ref=dst if src is None else src, dst_ref=dst, send_sem=send_sems.at[sem],
                                                recv_sem=recv_sems.at[sem], device_id=to, device_id_type=MESH)

        def small(j, src_shard, to):
            return pltpu.make_async_remote_copy(
                src_ref=cw_ref, dst_ref=gc_ref.at[src_shard], send_sem=send_sems.at[12 + j], recv_sem=recv_sems.at[12 + j],
                device_id=to, device_id_type=MESH)

        own = lambda q: a_ref.at[rows(c, q), :]
        smalls = [small(j, k, (*chip, c)) for j, chip in enumerate(_other_chips(x, y))]
        sends = [cp(0, k, c, 0, nx, own(0)), cp(2, k, c, 1, ny, own(1)), cp(1, k, c, 1, nx, own(1)), cp(3, k, c, 0, ny, own(0))]
        for s in sends + smalls:
            s.start()
        arrivals = [(0, kx, 0, (4, ny)), (2, ky, 1, (5, nx)), (1, kx, 1, None), (3, ky, 0, None), (4, kd, 0, None), (5, kd, 1, None)]
        for sem, shard, q, onward in arrivals:
            cp(sem, shard, c, q, me).wait_recv()
            if onward is not None:
                sends.append(cp(onward[0], shard, c, q, onward[1]))
                sends[-1].start()
            sends.append(cp(6 + sem, shard, c, q, sibling))
            sends[-1].start()
        for sem, shard, q, _ in arrivals:
            cp(6 + sem, shard, 1 - c, q, me).wait_recv()
        for j, chip in enumerate(_other_chips(x, y)):
            small(j, 2 * chip[0] + chip[1], me).wait_recv()
        for s in sends + smalls:
            s.wait_send()

    return pl.pallas_call(
        body, name="gather_weights", in_specs=[ANY, ANY], out_specs=[ANY, ANY],
        out_shape=[jax.ShapeDtypeStruct((N_SHARD,) + a.shape, a.dtype), jax.ShapeDtypeStruct((N_SHARD,) + cw.shape, cw.dtype)],
        scratch_shapes=[pltpu.SemaphoreType.DMA((15,)), pltpu.SemaphoreType.DMA((15,))],
        compiler_params=pltpu.CompilerParams(has_side_effects=True),
    )(a, cw)


def _gather_late_copies(src, land, send_sems, recv_sems):
    x, y, c = _me()
    k = 2 * x + y
    return [pltpu.make_async_remote_copy(src_ref=src, dst_ref=land.at[k], send_sem=send_sems.at[j], recv_sem=recv_sems.at[j],
                                         device_id=(*chip, c), device_id_type=MESH) for j, chip in enumerate(_other_chips(x, y))]


def _gather_late_start(b):
    land = lax.empty((N_SHARD,) + b.shape, b.dtype)

    def body(b_ref, land_ref, send_sems, recv_sems, b_thru, land_thru, token):
        for cp in _gather_late_copies(b_ref, land_ref, send_sems, recv_sems):
            cp.start()
        token[...] = jnp.zeros_like(token)

    return pl.pallas_call(
        body, name="gather_late_start", in_specs=[HBM, HBM],
        out_specs=(SEM, SEM, HBM, HBM, pl.BlockSpec(memory_space=pltpu.VMEM)),
        out_shape=(pltpu.SemaphoreType.DMA((3,)), pltpu.SemaphoreType.DMA((3,)), pltpu.HBM(b.shape, b.dtype),
                   pltpu.HBM(land.shape, land.dtype), jax.ShapeDtypeStruct((8, 128), f32)),
        input_output_aliases={0: 2, 1: 3}, compiler_params=pltpu.CompilerParams(has_side_effects=DATAFLOW),
    )(pltpu.with_memory_space_constraint(b, pltpu.HBM), pltpu.with_memory_space_constraint(land, pltpu.HBM))


def _gather_late_wait(send_sems, recv_sems, src, land, after):
    def body(b_ref, land_ref, send_sems_ref, recv_sems_ref, after_ref, b_dead, land_out):
        x, y, c = _me()
        for j, chip in enumerate(_other_chips(x, y)):
            kk = 2 * chip[0] + chip[1]
            cp = pltpu.make_async_remote_copy(src_ref=b_ref, dst_ref=land_ref.at[kk], send_sem=send_sems_ref.at[j],
                                              recv_sem=recv_sems_ref.at[j], device_id=(x, y, c), device_id_type=MESH)
            cp.wait_send()
            cp.wait_recv()

    return pl.pallas_call(
        body, name="gather_late_wait", in_specs=[HBM, HBM, SEM, SEM, ANY], out_specs=[HBM, HBM],
        out_shape=[pltpu.HBM(src.shape, src.dtype), pltpu.HBM(land.shape, land.dtype)], input_output_aliases={0: 0, 1: 1},
        compiler_params=pltpu.CompilerParams(has_side_effects=DATAFLOW),
    )(src, land, send_sems, recv_sems, after)[1]


HBM = pl.BlockSpec(memory_space=pltpu.HBM)
SEM = pl.BlockSpec(memory_space=pltpu.SEMAPHORE)
DATAFLOW = pltpu.SideEffectType.DATAFLOW_SIDE_EFFECTING


def _swap_copies(srcs, lands, send_sems, recv_sems):
    x, y, c = _me()

    def cp(src, dst, q):
        return pltpu.make_async_remote_copy(src_ref=src, dst_ref=dst, send_sem=send_sems.at[q], recv_sem=recv_sems.at[q],
                                            device_id=(x, y, 1 - c), device_id_type=MESH)

    return [cp(srcs[0], lands[0], 0), cp(srcs[1], lands[1], 1)] + [cp(srcs[2].at[s, 1 - c], lands[2].at[s], 2 + s) for s in range(N_SHARD)]


def _sibling_swap_start(g_main, g_dt, g_b):
    srcs = [g_main, g_dt, g_b]
    lands = [lax.empty(g_main.shape, g_main.dtype), lax.empty(g_dt.shape, g_dt.dtype),
             lax.empty(g_b.shape[:1] + g_b.shape[2:], g_b.dtype)]

    def body(*refs):
        for cp in _swap_copies(refs[0:3], refs[3:6], refs[6], refs[7]):
            cp.start()
        refs[14][...] = jnp.zeros_like(refs[14])

    hbm = lambda a: pltpu.HBM(a.shape, a.dtype)
    out = pl.pallas_call(
        body, name="sibling_swap_start", in_specs=[HBM] * 6,
        out_specs=(SEM, SEM, *[HBM] * 6, pl.BlockSpec(memory_space=pltpu.VMEM)),
        out_shape=(pltpu.SemaphoreType.DMA((2 + N_SHARD,)), pltpu.SemaphoreType.DMA((2 + N_SHARD,)), *[hbm(a) for a in srcs + lands],
                   jax.ShapeDtypeStruct((8, 128), f32)),
        input_output_aliases={t: 2 + t for t in range(6)}, compiler_params=pltpu.CompilerParams(has_side_effects=DATAFLOW),
    )(*[pltpu.with_memory_space_constraint(a, pltpu.HBM) for a in srcs + lands])
    return out[0], out[1], list(out[2:5]), list(out[5:8]), out[8]


def _sibling_swap_wait(send_sems, recv_sems, srcs, lands, after):
    def body(*refs):
        for cp in _swap_copies(refs[0:3], refs[3:6], refs[6], refs[7]):
            cp.wait_send()
            cp.wait_recv()

    hbm = lambda a: pltpu.HBM(a.shape, a.dtype)
    out = pl.pallas_call(
        body, name="sibling_swap_wait", in_specs=[HBM] * 6 + [SEM, SEM, ANY], out_specs=[HBM] * 6,
        out_shape=[hbm(a) for a in list(srcs) + list(lands)], input_output_aliases={t: t for t in range(6)},
        compiler_params=pltpu.CompilerParams(has_side_effects=DATAFLOW),
    )(*srcs, *lands, send_sems, recv_sems, after)
    return list(out[:3]), list(out[3:])


def _exchange_copies(ins, lands, send_sems, recv_sems):
    n = len(ins)
    x, y, c = _me()
    cps = []
    for j, chip in enumerate(_other_chips(x, y)):
        kk = 2 * chip[0] + chip[1]
        for t in range(n):
            cps.append(pltpu.make_async_remote_copy(
                src_ref=ins[t].at[kk], dst_ref=lands[t].at[j], send_sem=send_sems.at[n * j + t],
                recv_sem=recv_sems.at[n * j + t], device_id=(*chip, c), device_id_type=MESH))
    return cps


def _chip_exchange_start(arrs):
    n = len(arrs)
    lands = [lax.empty((3,) + a.shape[1:], a.dtype) for a in arrs]

    def body(*refs):
        ins, lands_in = refs[:n], refs[n:2 * n]
        send_sems, recv_sems = refs[2 * n], refs[2 * n + 1]
        token = refs[4 * n + 2]
        for cp in _exchange_copies(ins, lands_in, send_sems, recv_sems):
            cp.start()
        token[...] = jnp.zeros_like(token)

    hbm = lambda a: pltpu.HBM(a.shape, a.dtype)
    out = pl.pallas_call(
        body, name="chip_exchange_start", in_specs=[HBM] * (2 * n),
        out_specs=(SEM, SEM, *[HBM] * (2 * n), pl.BlockSpec(memory_space=pltpu.VMEM)),
        out_shape=(pltpu.SemaphoreType.DMA((3 * n,)), pltpu.SemaphoreType.DMA((3 * n,)), *[hbm(a) for a in arrs],
                   *[hbm(a) for a in lands], jax.ShapeDtypeStruct((8, 128), f32)),
        input_output_aliases={t: 2 + t for t in range(2 * n)},
        compiler_params=pltpu.CompilerParams(has_side_effects=DATAFLOW),
    )(*[pltpu.with_memory_space_constraint(a, pltpu.HBM) for a in list(arrs) + lands])
    return out[0], out[1], list(out[2:2 + n]), list(out[2 + n:2 + 2 * n]), out[2 + 2 * n]


def _chip_exchange_wait(send_sems, recv_sems, srcs, lands, after):
    n = len(srcs)

    def body(*refs):
        ins, lands_in = refs[:n], refs[n:2 * n]
        send_sems_ref, recv_sems_ref = refs[2 * n], refs[2 * n + 1]
        for cp in _exchange_copies(ins, lands_in, send_sems_ref, recv_sems_ref):
            cp.wait_send()
            cp.wait_recv()

    hbm = lambda a: pltpu.HBM(a.shape, a.dtype)
    out = pl.pallas_call(
        body, name="chip_exchange_wait", in_specs=[HBM] * (2 * n) + [SEM, SEM, ANY],
        out_specs=[HBM] * (2 * n), out_shape=[hbm(a) for a in list(srcs) + list(lands)],
        input_output_aliases={t: t for t in range(2 * n)},
        compiler_params=pltpu.CompilerParams(has_side_effects=DATAFLOW),
    )(*srcs, *lands, send_sems, recv_sems, after)
    return list(out[:n]), list(out[n:])


def _share_halves(bufs, by_cols, name):
    n = len(bufs)

    def body(*refs):
        ins, outs = refs[:n], refs[n:2 * n]
        send_sems, recv_sems = refs[2 * n], refs[2 * n + 1]
        x, y, c = _me()

        def part(ref, t, half):
            if by_cols[t]:
                w = bufs[t].shape[1] // 2
                return ref.at[:, pl.ds(pl.multiple_of(half * w, 128), w)]
            return ref.at[half]

        sends = [pltpu.make_async_remote_copy(src_ref=part(ins[t], t, c), dst_ref=part(outs[t], t, c), send_sem=send_sems.at[t],
                                              recv_sem=recv_sems.at[t], device_id=(x, y, 1 - c), device_id_type=MESH) for t in range(n)]
        for cp in sends:
            cp.start()
        for t in range(n):
            pltpu.make_async_remote_copy(src_ref=part(ins[t], t, c), dst_ref=part(outs[t], t, 1 - c), send_sem=send_sems.at[t],
                                         recv_sem=recv_sems.at[t], device_id=(x, y, c), device_id_type=MESH).wait_recv()
        for cp in sends:
            cp.wait_send()

    return pl.pallas_call(
        body, name=name, in_specs=[ANY] * n, out_specs=[ANY] * n,
        out_shape=[jax.ShapeDtypeStruct(a.shape, a.dtype) for a in bufs], input_output_aliases={t: t for t in range(n)},
        scratch_shapes=[pltpu.SemaphoreType.DMA((n,)), pltpu.SemaphoreType.DMA((n,))],
        compiler_params=pltpu.CompilerParams(has_side_effects=True),
    )(*bufs)


def _gather_vec(v):
    n = v.shape[1]

    def body(v_ref, o_ref, send_sems, recv_sems):
        x, y, c = _me()
        me = 4 * x + 2 * y + c
        cps = []
        for j in range(1, 8):
            fx, fy, fc = (j >> 2) & 1, (j >> 1) & 1, j & 1
            peer = (x ^ fx, y ^ fy, c ^ fc)
            cps.append(pltpu.make_async_remote_copy(
                src_ref=v_ref, dst_ref=o_ref.at[pl.ds(me, 1), :], send_sem=send_sems.at[j - 1], recv_sem=recv_sems.at[j - 1],
                device_id=peer, device_id_type=MESH))
        for cp in cps:
            cp.start()
        for j in range(1, 8):
            fx, fy, fc = (j >> 2) & 1, (j >> 1) & 1, j & 1
            src = 4 * (x ^ fx) + 2 * (y ^ fy) + (c ^ fc)
            pltpu.make_async_remote_copy(
                src_ref=v_ref, dst_ref=o_ref.at[pl.ds(src, 1), :], send_sem=send_sems.at[j - 1], recv_sem=recv_sems.at[j - 1],
                device_id=(x, y, c), device_id_type=MESH).wait_recv()
        for cp in cps:
            cp.wait_send()

    return pl.pallas_call(
        body, name="gather_vec", in_specs=[ANY], out_specs=ANY, out_shape=jax.ShapeDtypeStruct((8, n), v.dtype),
        scratch_shapes=[pltpu.SemaphoreType.DMA((7,)), pltpu.SemaphoreType.DMA((7,))],
        compiler_params=pltpu.CompilerParams(has_side_effects=True),
    )(v)


def _pair_sum(g, r, name, tr):
    L, hr, C = r.shape
    both_halves = g.ndim == 4

    def body(c_ref, g_ref, r_ref, o_ref):
        def strip(rows):
            gv = g_ref[0, 0, rows, :] if both_halves else g_ref[0, rows, :]
            o_ref[0, rows, :] = (gv + r_ref[0, rows, :]).astype(bf16)
        _for_strips(tr, strip)

    g_spec = (pl.BlockSpec((1, 1, tr, C), lambda s, i, c_ref: (s, c_ref[0], i, 0)) if both_halves
              else pl.BlockSpec((1, tr, C), lambda s, i, c_ref: (s, i, 0)))
    grid_spec = pltpu.PrefetchScalarGridSpec(
        num_scalar_prefetch=1, grid=(L, hr // tr),
        in_specs=[g_spec, pl.BlockSpec((1, tr, C), lambda s, i, c_ref: (s, i, 0))],
        out_specs=pl.BlockSpec((1, tr, C), lambda s, i, c_ref: (s, i, 0)))
    c = lax.axis_index("c").reshape(1).astype(jnp.int32)
    return pl.pallas_call(body, name=name, grid_spec=grid_spec, out_shape=jax.ShapeDtypeStruct((L, hr, C), bf16),
                          compiler_params=_params(("parallel", "parallel")))(c, g, r)


def _own_sum(p, got, name, transposed=False):
    _, hr, C = p.shape
    tr = SUM_ROWS
    c_full, c_pad = C // 128 * 128, -(-C // 128) * 128

    def total(p_ref, got_ref, rows):
        return ((p_ref[0, rows, :].astype(f32) + got_ref[0, rows, :].astype(f32)) + got_ref[1, rows, :].astype(f32)) \
            + got_ref[2, rows, :].astype(f32)

    def body(idx_ref, p_ref, got_ref, o_ref):
        def strip(rows):
            o_ref[0, rows, :] = total(p_ref, got_ref, rows)
        _for_strips(tr, strip)

    def body_t(idx_ref, p_ref, got_ref, o_ref, buf):
        if c_pad > c_full:
            buf[:, pl.ds(c_full, c_pad - c_full)] = jnp.zeros((tr, c_pad - c_full), f32)

        def strip(rows):
            buf[rows, pl.ds(0, C)] = total(p_ref, got_ref, rows)
        _for_strips(tr, strip)
        o_ref[...] = buf[...].T[:C]

    in_specs = [pl.BlockSpec((1, tr, C), lambda i, idx: (idx[0], i, 0)), pl.BlockSpec((3, tr, C), lambda i, idx: (0, i, 0))]
    x, y, c = _me()
    idx = jnp.stack([2 * x + y, c]).astype(jnp.int32)
    if transposed:
        grid_spec = pltpu.PrefetchScalarGridSpec(num_scalar_prefetch=1, grid=(hr // tr,), in_specs=in_specs,
                                                 out_specs=pl.BlockSpec((C, tr), lambda i, idx: (0, idx[1] * (hr // tr) + i)),
                                                 scratch_shapes=[pltpu.VMEM((tr, c_pad), f32)])
        return pl.pallas_call(body_t, name=name, grid_spec=grid_spec, out_shape=jax.ShapeDtypeStruct((C, 2 * hr), f32),
                              compiler_params=_params(("parallel",)))(idx, p, got)
    grid_spec = pltpu.PrefetchScalarGridSpec(num_scalar_prefetch=1, grid=(hr // tr,), in_specs=in_specs,
                                             out_specs=pl.BlockSpec((1, tr, C), lambda i, idx: (idx[1], i, 0)))
    return pl.pallas_call(body, name=name, grid_spec=grid_spec, out_shape=jax.ShapeDtypeStruct((2, hr, C), f32),
                          compiler_params=_params(("parallel",)))(idx, p, got)


def _adamw(w, g, m, v, name, tr):
    _, R, C = w.shape
    rs = min(8, tr)

    def body(w_ref, g_ref, m_ref, v_ref, d_ref, nm_ref, nv_ref):
        def strip(s, carry):
            rows = pl.ds(pl.multiple_of(s * rs, rs), rs)
            gv = g_ref[0, rows, :]
            mn = ADAM_B1 * m_ref[0, rows, :] + (1.0 - ADAM_B1) * gv
            vn = ADAM_B2 * v_ref[0, rows, :] + (1.0 - ADAM_B2) * (gv * gv)
            m_hat = mn / (1.0 - ADAM_B1 ** ADAM_STEP)
            v_hat = vn / (1.0 - ADAM_B2 ** ADAM_STEP)
            d_ref[0, rows, :] = -ADAM_LR * (m_hat / (jnp.sqrt(v_hat) + ADAM_EPS) + ADAM_WD * w_ref[0, rows, :])
            nm_ref[0, rows, :] = mn
            nv_ref[0, rows, :] = vn
            return carry

        if R % tr == 0:
            lax.fori_loop(0, tr // rs, strip, 0, unroll=min(2, tr // rs))
        else:
            lax.fori_loop(0, jnp.minimum(tr, R - pl.program_id(0) * tr) // rs, strip, 0)

    blk, grid = pl.BlockSpec((1, tr, C), lambda i: (0, i, 0)), (-(-R // tr),)
    o = jax.ShapeDtypeStruct((1, R, C), f32)
    return pl.pallas_call(body, name=name, grid=grid, in_specs=[blk] * 4, out_specs=[blk] * 3, out_shape=[o, o, o],
                          compiler_params=_params(("parallel",)))(w, g, m, v)


def _sum8(t):
    n = t.shape[1]

    def body(t_ref, o_ref):
        acc = t_ref[pl.ds(0, 1), :]
        for r in range(1, 8):
            acc = acc + t_ref[pl.ds(r, 1), :]
        o_ref[...] = acc

    return pl.pallas_call(body, name="sum_devices", out_shape=jax.ShapeDtypeStruct((1, n), f32))(t)


def _reduce_swap_start(g_main, g_dt, g_b):
    hr = g_main.shape[0]
    return _sibling_swap_start(g_main, g_dt, g_b.reshape(N_SHARD, 2, hr, g_b.shape[-1]))


def _reduce_start(swap_state, g_main, g_dt):
    hr = g_main.shape[0]
    send_sems, recv_sems, srcs, lands, _ = swap_state
    srcs, (r_main, r_dt, r_b) = _sibling_swap_wait(send_sems, recv_sems, srcs, lands, g_dt)
    p_main = _pair_sum(g_main[None], r_main[None], "pair_sum_main", SUM_ROWS // 4)
    p_dt = _pair_sum(g_dt[None], r_dt[None], "pair_sum_dt", SUM_ROWS)
    p_b = _pair_sum(srcs[2], r_b, "pair_sum_b", SUM_ROWS)
    p_in = jnp.transpose(_w_in_grad_full(p_main[0], p_dt[0]).reshape(hr, N_SHARD, W_IN_SHARD), (1, 0, 2))
    return _chip_exchange_start([p_in, p_b])


def _reduce_finish(state, after):
    send_sems, recv_sems, srcs, lands, _ = state
    (p_in, p_b), (got_in, got_b) = _chip_exchange_wait(send_sems, recv_sems, srcs, lands, after)
    mine_in, mine_b = _own_sum(p_in, got_in, "own_sum_in", transposed=True), _own_sum(p_b, got_b, "own_sum_b")
    full_in_t, full_b = _share_halves([mine_in, mine_b], [True, False], "share_halves")
    return full_in_t, full_b.reshape(-1, full_b.shape[-1])


def kernel(x, positions, norm1_w, w_in, conv_w, conv_b, dt_bias, a_log, d_skip, ssd_norm_w, w_br_ret, w_br_ssd, w_out, norm_f_w, loss_target, m_norm1_w, m_w_in, m_conv_w, m_conv_b, m_dt_bias, m_a_log, m_d_skip, m_ssd_norm_w, m_w_br_ret, m_w_br_ssd, m_w_out, m_norm_f_w, v_norm1_w, v_w_in, v_conv_w, v_conv_b, v_dt_bias, v_a_log, v_d_skip, v_ssd_norm_w, v_w_br_ret, v_w_br_ssd, v_w_out, v_norm_f_w):
    D = D_MODEL
    xi, yi, ci = _me()
    k = 2 * xi + yi
    me = 2 * k + ci
    weights = dict(norm1_w=norm1_w, w_in=w_in, conv_w=conv_w, conv_b=conv_b, dt_bias=dt_bias, a_log=a_log, d_skip=d_skip,
                   ssd_norm_w=ssd_norm_w, w_br_ret=w_br_ret, w_br_ssd=w_br_ssd, w_out=w_out, norm_f_w=norm_f_w)
    mom1 = dict(norm1_w=m_norm1_w, w_in=m_w_in, conv_w=m_conv_w, conv_b=m_conv_b, dt_bias=m_dt_bias, a_log=m_a_log, d_skip=m_d_skip,
                ssd_norm_w=m_ssd_norm_w, w_br_ret=m_w_br_ret, w_br_ssd=m_w_br_ssd, w_out=m_w_out, norm_f_w=m_norm_f_w)
    mom2 = dict(norm1_w=v_norm1_w, w_in=v_w_in, conv_w=v_conv_w, conv_b=v_conv_b, dt_bias=v_dt_bias, a_log=v_a_log, d_skip=v_d_skip,
                ssd_norm_w=v_ssd_norm_w, w_br_ret=v_w_br_ret, w_br_ssd=v_w_br_ssd, w_out=v_w_out, norm_f_w=v_norm_f_w)

    a_sh = w_in[0].astype(bf16)
    b_sh = jnp.concatenate([w_br_ret[0], w_br_ssd[0], w_out[0]], axis=0).astype(bf16)
    ga, gc = _gather_weights(a_sh, conv_w[0])
    ga, b_late = lax.optimization_barrier((ga, b_sh))
    late_send, late_recv, late_src, late_land, late_token = _gather_late_start(b_late)
    own = lambda g, s: lax.dynamic_update_slice_in_dim(g, s[None], k, axis=0)
    ga, gc = own(ga, a_sh), own(gc, conv_w[0])
    w_main, w_dt = _w_main_from_shards(ga)
    conv_full = jnp.transpose(gc, (1, 0, 2)).reshape(SSD_CONV, CONV_DIM)

    def late_weights(after):
        gb = own(_gather_late_wait(late_send, late_recv, late_src, late_land, after), b_sh)
        return gb[:, 0:512].reshape(2048, D), gb[:, 512:1536].reshape(4096, D), gb[:, 1536:2048].reshape(2048, D)

    class Reducer:
        @staticmethod
        def halves():
            return (1 - ci).reshape(1).astype(jnp.int32), ci.reshape(1).astype(jnp.int32)

        @staticmethod
        def first(g_main, g_dt, g_w_br, g_w_bs, g_w_o):
            g_b = jnp.concatenate([g_w_br.reshape(N_SHARD, 512, D), g_w_bs.reshape(N_SHARD, 1024, D),
                                   g_w_o.reshape(N_SHARD, 512, D)], axis=1)
            return _reduce_swap_start(g_main, g_dt, g_b)

        second = staticmethod(_reduce_start)

    loss, grad_x, g, reduce_state = _local_step(x[0], positions[0], loss_target[0], norm1_w, w_main, w_dt, conv_full, conv_b, dt_bias,
                                                a_log, d_skip, ssd_norm_w, late_token, late_weights, norm_f_w, Reducer)

    grad_w_in_t, full_b = _reduce_finish(reduce_state, g["norm1_w"])
    grad_mats = dict(w_br_ret=full_b[0:512], w_br_ssd=full_b[512:1536], w_out=full_b[1536:2048])

    small = [(n, weights[n].size) for n in ("norm1_w", "conv_b", "dt_bias", "a_log", "d_skip", "ssd_norm_w", "norm_f_w")]
    parts = [jnp.pad(loss.reshape(1, 1), ((0, 0), (0, 127)))] + [g[n].reshape(1, -1) for n, _ in small] + [g["conv_w"].reshape(1, -1)]
    vec = jnp.concatenate(parts, axis=1)
    nv = vec.shape[1]
    nvp = -(-nv // 128) * 128
    vec = jnp.pad(vec, ((0, 0), (0, nvp - nv)))
    total = _sum8(lax.dynamic_update_slice_in_dim(_gather_vec(vec), vec, me, axis=0))
    loss_out = total[0, 0]
    off = 128
    grad_small = {}
    for n, sz in small:
        grad_small[n] = total[:, off:off + sz]
        off += sz
    g_conv = total[:, off:off + SSD_CONV * CONV_DIM].reshape(SSD_CONV, CONV_DIM)
    g_conv = lax.dynamic_slice_in_dim(g_conv, k * (CONV_DIM // N_SHARD), CONV_DIM // N_SHARD, axis=1)
    grad_small["conv_w"] = g_conv.reshape(1, -1)

    upd = {}
    for n in ("w_br_ret", "w_br_ssd", "w_out"):
        upd[n] = _adamw(weights[n], grad_mats[n][None], mom1[n], mom2[n], "adamw_" + n, tr=SUM_ROWS)
    tp = lambda t: jnp.swapaxes(t, 1, 2)
    upd["w_in"] = tuple(tp(t) for t in _adamw(tp(w_in), grad_w_in_t[None], tp(m_w_in), tp(v_w_in), "adamw_w_in", tr=256))
    grad_mats["w_in"] = tp(grad_w_in_t[None])
    names_small = [n for n, _ in small] + ["conv_w"]
    flat = lambda d: jnp.concatenate([d[n].reshape(1, -1) for n in names_small], axis=1)
    ns = sum(weights[n].size for n in names_small)
    nsp = -(-ns // 128) * 128
    padv = lambda t: jnp.pad(t, ((0, 0), (0, nsp - ns)))
    small_upd = _adamw(padv(flat(weights))[None], padv(flat(grad_small))[None], padv(flat(mom1))[None],
                       jnp.pad(flat(mom2), ((0, 0), (0, nsp - ns)), constant_values=1.0)[None], "adamw_small", 1)
    off = 0
    for n in names_small:
        sz = weights[n].size
        upd[n] = tuple(t[0, :, off:off + sz] for t in small_upd)
        off += sz

    order = ["norm1_w", "w_in", "conv_w", "conv_b", "dt_bias", "a_log", "d_skip", "ssd_norm_w", "w_br_ret", "w_br_ssd", "w_out", "norm_f_w"]
    grads_out = {**grad_mats, **grad_small}
    shp = lambda n, t: t.reshape(weights[n].shape)
    return (loss_out, grad_x[None], *[shp(n, grads_out[n]) for n in order], *[shp(n, upd[n][0]) for n in order],
            *[shp(n, upd[n][1]) for n in order], *[shp(n, upd[n][2]) for n in order])
```

```python
import jax
import jax.numpy as jnp
import numpy as np
from jax import lax
from jax.experimental import pallas as pl
from jax.experimental.pallas import tpu as pltpu

f32 = jnp.float32
bf16 = jnp.bfloat16
HIGHEST = lax.Precision.HIGHEST
MESH = pl.DeviceIdType.MESH

D_MODEL = 2048
EPS = 1e-6
CHUNK = 64
RET_HEADS = 8
RET_DK = 256
RET_HW = 4 * RET_DK
RET_HP = 2
RET_UNROLL_FWD, RET_UNROLL_BWD = 4, 8
ROPE_THETA = 10000.0
SSD_WIDTH = 4096
SSD_GROUPS = 8
SSD_STATE = 128
SSD_GW = 512
SSD_GC = SSD_GW + 2 * SSD_STATE
SSD_HPG = 8
SSD_CONV = 4
CONV_DIM = 6144
SSD_HEADS = 64
LS = 128

C_RET, C_Z, C_GATES, C_XBC = 0, 8192, 12288, 16384
N_MAIN = 22528
DT_OFF = 18432
IN_PROJ = 22592
N_SHARD = 4
W_IN_SHARD = IN_PROJ // N_SHARD

ADAM_LR, ADAM_B1, ADAM_B2, ADAM_EPS, ADAM_WD, ADAM_STEP = 0.001, 0.9, 0.999, 1e-08, 0.01, 10

VMEM_LIMIT = 56 * 1024 * 1024
SUM_ROWS = 128
ANY = pl.BlockSpec(memory_space=pl.ANY)


def _params(dims):
    return pltpu.CompilerParams(dimension_semantics=dims, vmem_limit_bytes=VMEM_LIMIT)


def _silu(x):
    return x * jax.nn.sigmoid(x)


def _nt(a, b):
    return lax.dot_general(a, b, (((1,), (1,)), ((), ())), preferred_element_type=f32)


def _tn(a, b):
    return lax.dot_general(a, b, (((0,), (0,)), ((), ())), preferred_element_type=f32)


def _nn(a, b):
    return jnp.dot(a, b, preferred_element_type=f32)


def _hi(a, b):
    return jnp.dot(a, b, precision=HIGHEST, preferred_element_type=f32)


def _split(a):
    hi = a.astype(bf16)
    return hi, (a - hi.astype(f32)).astype(bf16)


def _sel_r(a, sel):
    hi, lo = _split(a)
    return _nn(hi, sel) + _nn(lo, sel)


def _sel_l(sel, a):
    hi, lo = _split(a)
    return _nn(sel, hi) + _nn(sel, lo)


def _rows_to_cols(t, eye):
    hi = t.astype(bf16)
    r1 = t - hi.astype(f32)
    mid = r1.astype(bf16)
    lo = (r1 - mid.astype(f32)).astype(bf16)
    return _nt(eye, hi) + _nt(eye, mid) + _nt(eye, lo)


def _xbc_group_major(t):
    R = t.shape[0]
    nb = SSD_GROUPS * SSD_STATE
    parts = [t[:, :SSD_WIDTH].reshape(R, SSD_GROUPS, SSD_GW), t[:, SSD_WIDTH:SSD_WIDTH + nb].reshape(R, SSD_GROUPS, SSD_STATE),
             t[:, SSD_WIDTH + nb:].reshape(R, SSD_GROUPS, SSD_STATE)]
    return jnp.concatenate(parts, axis=2).reshape(R, CONV_DIM)


def _xbc_original(t):
    R = t.shape[0]
    g = t.reshape(R, SSD_GROUPS, SSD_GC)
    parts = [g[:, :, :SSD_GW].reshape(R, SSD_WIDTH), g[:, :, SSD_GW:SSD_GW + SSD_STATE].reshape(R, SSD_GROUPS * SSD_STATE),
             g[:, :, SSD_GW + SSD_STATE:].reshape(R, SSD_GROUPS * SSD_STATE)]
    return jnp.concatenate(parts, axis=1)


def _main_segments():
    segs = []
    for h in range(RET_HEADS):
        segs += [(base + RET_DK * h, RET_DK) for base in (0, 2048, 4096, 6144)]
    segs += [(8192, SSD_WIDTH), (DT_OFF + SSD_HEADS, 2 * D_MODEL)]
    nb = SSD_GROUPS * SSD_STATE
    for g in range(SSD_GROUPS):
        segs += [(12288 + SSD_GW * g, SSD_GW), (12288 + SSD_WIDTH + SSD_STATE * g, SSD_STATE),
                 (12288 + SSD_WIDTH + nb + SSD_STATE * g, SSD_STATE)]
    return segs


def _w_main_from_shards(shards):
    def cols(lo, hi):
        out = []
        while lo < hi:
            s = lo // W_IN_SHARD
            top = min(hi, (s + 1) * W_IN_SHARD)
            out.append(shards[s][:, lo - s * W_IN_SHARD:top - s * W_IN_SHARD])
            lo = top
        return out

    main = jnp.concatenate([p for s, n in _main_segments() for p in cols(s, s + n)], axis=1)
    w_dt = jnp.pad(jnp.concatenate(cols(DT_OFF, DT_OFF + SSD_HEADS), axis=1), ((0, 0), (0, 128 - SSD_HEADS)))
    return main, w_dt


def _w_in_grad_full(g_main, g_dt):
    D = g_main.shape[0]
    ret = jnp.transpose(g_main[:, :C_Z].reshape(D, RET_HEADS, 4, RET_DK), (0, 2, 1, 3)).reshape(D, C_Z)
    return jnp.concatenate([ret, g_main[:, C_Z:C_GATES], _xbc_original(g_main[:, C_XBC:]), g_dt[:, :SSD_HEADS],
                            g_main[:, C_GATES:C_XBC]], axis=1)


def _mm(pairs, M, N, *, tm, tn, out_dtype, name, tb=False, row_off=None):
    P = len(pairs)
    nks = [K // tk for (_, _, _, _, _, K, tk) in pairs]
    starts = [int(s) for s in np.cumsum([0] + nks[:-1])]
    KT = int(sum(nks))
    npf = 0 if row_off is None else 1
    in_specs, args = [], []
    for (a, a_cb, b, b_kb, b_nb, K, tk), s, nk in zip(pairs, starts, nks):
        def kk(k, s=s, nk=nk):
            return jnp.clip(k - s, 0, nk - 1)
        in_specs.append(pl.BlockSpec((tm, tk), lambda m, n, k, *pf, kk=kk, a_cb=a_cb: (m + (pf[0][0] if pf else 0), a_cb + kk(k))))
        if tb:
            in_specs.append(pl.BlockSpec((tn, tk), lambda m, n, k, *pf, kk=kk, b_kb=b_kb, b_nb=b_nb: (b_nb + n, b_kb + kk(k))))
        else:
            in_specs.append(pl.BlockSpec((tk, tn), lambda m, n, k, *pf, kk=kk, b_kb=b_kb, b_nb=b_nb: (b_kb + kk(k), b_nb + n)))
        args += [a, b]

    def body(*refs):
        refs = refs[npf:]
        o_ref = refs[2 * P]
        k = pl.program_id(2)

        def prod(i):
            a = refs[2 * i][...].astype(bf16)
            b = refs[2 * i + 1][...].astype(bf16)
            return _nt(a, b) if tb else _nn(a, b)

        if KT == 1:
            o_ref[...] = prod(0).astype(out_dtype)
            return
        acc = refs[2 * P + 1]

        @pl.when(k == 0)
        def _():
            acc[...] = jnp.zeros_like(acc)

        for i in range(P):
            @pl.when((k >= starts[i]) & (k < starts[i] + nks[i]))
            def _(i=i):
                acc[...] += prod(i)

        @pl.when(k == KT - 1)
        def _():
            o_ref[...] = acc[...].astype(out_dtype)

    grid_spec = pltpu.PrefetchScalarGridSpec(
        num_scalar_prefetch=npf, grid=(M // tm, N // tn, KT), in_specs=in_specs,
        out_specs=pl.BlockSpec((tm, tn), lambda m, n, k, *pf: (m, n)),
        scratch_shapes=[] if KT == 1 else [pltpu.VMEM((tm, tn), f32)])
    return pl.pallas_call(
        body, name=name, grid_spec=grid_spec, out_shape=jax.ShapeDtypeStruct((M, N), out_dtype),
        compiler_params=_params(("parallel", "parallel", "arbitrary")),
    )(*([] if row_off is None else [row_off]), *args)


def _mm1(a, b, *, tm, tn, tk, out_dtype, name, tb=False):
    M, K = a.shape
    N = b.shape[0] if tb else b.shape[1]
    return _mm([(a, 0, b, 0, 0, K, tk)], M, N, tm=tm, tn=tn, out_dtype=out_dtype, name=name, tb=tb)


RS = 16
CS = 32


def _for_strips(n_rows, fn, rs=RS, unroll=4):
    def step(s, carry):
        fn(pl.ds(pl.multiple_of(s * rs, rs), rs))
        return carry
    n = n_rows // rs
    lax.fori_loop(0, n, step, 0, unroll=min(unroll, n))


def _norm1_fwd(x, w, tr):
    S, D = x.shape

    def body(x_ref, w_ref, h_ref, ht_ref):
        def strip(rows):
            xv = x_ref[rows, :]
            r = lax.rsqrt(jnp.mean(xv * xv, axis=-1, keepdims=True) + EPS)
            h_ref[rows, :] = (xv * r * w_ref[...]).astype(bf16)
        _for_strips(tr, strip)
        ht_ref[...] = h_ref[...].T

    return pl.pallas_call(
        body, name="norm1_fwd", grid=(S // tr,),
        in_specs=[pl.BlockSpec((tr, D), lambda i: (i, 0)), pl.BlockSpec((1, D), lambda i: (0, 0))],
        out_specs=[pl.BlockSpec((tr, D), lambda i: (i, 0)), pl.BlockSpec((D, tr), lambda i: (0, i))],
        out_shape=[jax.ShapeDtypeStruct((S, D), bf16), jax.ShapeDtypeStruct((D, S), bf16)], compiler_params=_params(("parallel",)),
    )(x, w)


def _norm1_bwd(x, w, dh, dx2, tr):
    S, D = x.shape

    def body(x_ref, w_ref, dh_ref, dx2_ref, gx_ref, gw_ref, acc):
        @pl.when(pl.program_id(0) == 0)
        def _():
            acc[...] = jnp.zeros_like(acc)

        def strip(rows):
            xv = x_ref[rows, :]
            r = lax.rsqrt(jnp.mean(xv * xv, axis=-1, keepdims=True) + EPS)
            xh = xv * r
            dhv = dh_ref[rows, :]
            acc[...] += dhv * xh
            dxh = dhv * w_ref[...]
            gx_ref[rows, :] = dx2_ref[rows, :] + r * (dxh - xh * jnp.mean(dxh * xh, axis=-1, keepdims=True))
        _for_strips(tr, strip)

        @pl.when(pl.program_id(0) == S // tr - 1)
        def _():
            gw_ref[...] = jnp.sum(acc[...], axis=0, keepdims=True)

    row = pl.BlockSpec((tr, D), lambda i: (i, 0))
    vec = pl.BlockSpec((1, D), lambda i: (0, 0))
    return pl.pallas_call(
        body, name="norm1_bwd", grid=(S // tr,), in_specs=[row, vec, row, row], out_specs=[row, vec],
        out_shape=[jax.ShapeDtypeStruct((S, D), f32), jax.ShapeDtypeStruct((1, D), f32)],
        scratch_shapes=[pltpu.VMEM((RS, D), f32)], compiler_params=_params(("arbitrary",)),
    )(x, w, dh, dx2)


def _final_fwd_bwd(x, mo, target, wf, tr):
    S, D = x.shape

    def body(x_ref, mo_ref, t_ref, w_ref, dx2_ref, dx2b_ref, loss_ref, gw_ref, acc, lacc):
        @pl.when(pl.program_id(0) == 0)
        def _():
            acc[...] = jnp.zeros_like(acc)
            lacc[...] = jnp.zeros_like(lacc)

        def strip(rows):
            x2 = x_ref[rows, :] + mo_ref[rows, :]
            r = lax.rsqrt(jnp.mean(x2 * x2, axis=-1, keepdims=True) + EPS)
            xh = x2 * r
            wv = w_ref[...]
            err = xh * wv - t_ref[rows, :]
            lacc[...] += jnp.mean(err * err, axis=-1, keepdims=True)
            dy = err * (1.0 / D)
            acc[...] += dy * xh
            dxh = dy * wv
            dx2 = r * (dxh - xh * jnp.mean(dxh * xh, axis=-1, keepdims=True))
            dx2_ref[rows, :] = dx2
            dx2b_ref[rows, :] = dx2.astype(bf16)
        _for_strips(tr, strip)

        @pl.when(pl.program_id(0) == S // tr - 1)
        def _():
            gw_ref[...] = jnp.sum(acc[...], axis=0, keepdims=True)
            loss_ref[...] = 0.5 * jnp.sum(lacc[...], axis=0, keepdims=True)

    row = pl.BlockSpec((tr, D), lambda i: (i, 0))
    vec = pl.BlockSpec((1, D), lambda i: (0, 0))
    return pl.pallas_call(
        body, name="final_norm_loss", grid=(S // tr,), in_specs=[row, row, row, vec],
        out_specs=[row, row, pl.BlockSpec((1, 1), lambda i: (0, 0)), vec],
        out_shape=[jax.ShapeDtypeStruct((S, D), f32), jax.ShapeDtypeStruct((S, D), bf16), jax.ShapeDtypeStruct((1, 1), f32),
                   jax.ShapeDtypeStruct((1, D), f32)],
        scratch_shapes=[pltpu.VMEM((RS, D), f32), pltpu.VMEM((RS, 1), f32)], compiler_params=_params(("arbitrary",)),
    )(x, mo, target, wf)


def _merge_fwd(p_r, p_s, proj, tr):
    S, D = p_r.shape

    def body(pr_ref, ps_ref, g_ref, o_ref, ot_ref):
        def strip(rows):
            gr, gs = g_ref[rows, pl.ds(0, D)].astype(f32), g_ref[rows, pl.ds(D, D)].astype(f32)
            o_ref[rows, :] = (jax.nn.sigmoid(gr) * pr_ref[rows, :] + jax.nn.sigmoid(gs) * ps_ref[rows, :]).astype(bf16)
        _for_strips(tr, strip)
        ot_ref[...] = o_ref[...].T

    row = pl.BlockSpec((tr, D), lambda i: (i, 0))
    return pl.pallas_call(
        body, name="merge_fwd", grid=(S // tr,),
        in_specs=[row, row, pl.BlockSpec((tr, 2 * D), lambda i: (i, C_GATES // (2 * D)))],
        out_specs=[row, pl.BlockSpec((D, tr), lambda i: (0, i))],
        out_shape=[jax.ShapeDtypeStruct((S, D), bf16), jax.ShapeDtypeStruct((D, S), bf16)], compiler_params=_params(("parallel",)),
    )(p_r, p_s, proj)


def _merge_bwd(dm, p_r, p_s, proj, tr):
    S, D = p_r.shape

    def body(dm_ref, pr_ref, ps_ref, g_ref, dpr_ref, dps_ref, dproj_ref):
        def strip(rows):
            dmv = dm_ref[rows, :]
            sr = jax.nn.sigmoid(g_ref[rows, pl.ds(0, D)].astype(f32))
            ss = jax.nn.sigmoid(g_ref[rows, pl.ds(D, D)].astype(f32))
            dpr_ref[rows, :] = (dmv * sr).astype(bf16)
            dps_ref[rows, :] = (dmv * ss).astype(bf16)
            dproj_ref[rows, pl.ds(0, D)] = (dmv * pr_ref[rows, :] * sr * (1.0 - sr)).astype(bf16)
            dproj_ref[rows, pl.ds(D, D)] = (dmv * ps_ref[rows, :] * ss * (1.0 - ss)).astype(bf16)
        _for_strips(tr, strip)

    row = pl.BlockSpec((tr, D), lambda i: (i, 0))
    gates = pl.BlockSpec((tr, 2 * D), lambda i: (i, C_GATES // (2 * D)))
    o = jax.ShapeDtypeStruct((S, D), bf16)
    return pl.pallas_call(
        body, name="merge_bwd", grid=(S // tr,), in_specs=[row, row, row, gates],
        out_specs=[row, row, gates], out_shape=[o, o, jax.ShapeDtypeStruct((S, N_MAIN), bf16)],
        compiler_params=_params(("parallel",)),
    )(dm, p_r, p_s, proj)


def _ssd_norm_fwd(y, proj, w, tr):
    S, W = y.shape

    def body(y_ref, z_ref, w_ref, o_ref, ot_ref):
        def strip(rows):
            u = y_ref[rows, :] * _silu(z_ref[rows, :].astype(f32))
            r = lax.rsqrt(jnp.mean(u * u, axis=-1, keepdims=True) + EPS)
            o_ref[rows, :] = (u * r * w_ref[...]).astype(bf16)
        _for_strips(tr, strip)
        ot_ref[...] = o_ref[...].T

    row = pl.BlockSpec((tr, W), lambda i: (i, 0))
    return pl.pallas_call(
        body, name="ssd_norm_fwd", grid=(S // tr,),
        in_specs=[row, pl.BlockSpec((tr, W), lambda i: (i, C_Z // W)), pl.BlockSpec((1, W), lambda i: (0, 0))],
        out_specs=[row, pl.BlockSpec((W, tr), lambda i: (0, i))],
        out_shape=[jax.ShapeDtypeStruct((S, W), bf16), jax.ShapeDtypeStruct((W, S), bf16)], compiler_params=_params(("parallel",)),
    )(y, proj, w)


def _ssd_norm_bwd(y, proj, w, dys, dproj, tr):
    S, W = y.shape

    def body(y_ref, z_ref, w_ref, d_ref, _, dy_ref, dz_ref, gw_ref, acc):
        @pl.when(pl.program_id(0) == 0)
        def _():
            acc[...] = jnp.zeros_like(acc)

        def strip(rows):
            yv, zv, dv = y_ref[rows, :], z_ref[rows, :].astype(f32), d_ref[rows, :]
            sg = jax.nn.sigmoid(zv)
            sz = zv * sg
            u = yv * sz
            r = lax.rsqrt(jnp.mean(u * u, axis=-1, keepdims=True) + EPS)
            un = u * r
            acc[...] += dv * un
            dun = dv * w_ref[...]
            du = r * (dun - un * jnp.mean(dun * un, axis=-1, keepdims=True))
            dy_ref[rows, :] = (du * sz).astype(bf16)
            dz_ref[rows, :] = (du * yv * (sg * (1.0 + zv * (1.0 - sg)))).astype(bf16)
        _for_strips(tr, strip)

        @pl.when(pl.program_id(0) == S // tr - 1)
        def _():
            gw_ref[...] = jnp.sum(acc[...], axis=0, keepdims=True)

    row = pl.BlockSpec((tr, W), lambda i: (i, 0))
    zcol = pl.BlockSpec((tr, W), lambda i: (i, C_Z // W))
    vec = pl.BlockSpec((1, W), lambda i: (0, 0))
    return pl.pallas_call(
        body, name="ssd_norm_bwd", grid=(S // tr,),
        in_specs=[row, zcol, vec, row, ANY], out_specs=[row, zcol, vec],
        out_shape=[jax.ShapeDtypeStruct((S, W), bf16), jax.ShapeDtypeStruct(dproj.shape, bf16), jax.ShapeDtypeStruct((1, W), f32)],
        input_output_aliases={4: 1}, scratch_shapes=[pltpu.VMEM((RS, W), f32)], compiler_params=_params(("arbitrary",)),
    )(y, proj, w, dys, dproj)


def _rope(t, cos, sin):
    t1, t2 = t[:, :128], t[:, 128:]
    return jnp.concatenate([t1 * cos - t2 * sin, t2 * cos + t1 * sin], axis=1)


def _rope_t(d, cos, sin):
    d1, d2 = d[:, :128], d[:, 128:]
    return jnp.concatenate([d1 * cos + d2 * sin, d2 * cos - d1 * sin], axis=1)


def _ret_specs(tb, rev_nb=None):
    def blk(i):
        return i if rev_nb is None else rev_nb - 1 - i
    head = pl.BlockSpec((tb, RET_HP * RET_HW), lambda h, i: (blk(i), h))
    tab = pl.BlockSpec((tb, 128), lambda h, i: (blk(i), 0))
    mat = pl.BlockSpec((RET_HP, CHUNK, CHUNK), lambda h, i: (h, 0, 0))
    vec = pl.BlockSpec((RET_HP, CHUNK, 1), lambda h, i: (h, 0, 0))
    one = pl.BlockSpec((RET_HP, 1, 1), lambda h, i: (h, 0, 0))
    own = pl.BlockSpec((tb, RET_HP * RET_DK), lambda h, i: (blk(i), h))
    st = pl.BlockSpec((RET_HP, tb // CHUNK, RET_DK, RET_DK), lambda h, i: (h, blk(i), 0, 0))
    return head, tab, mat, vec, one, own, st


def _ret_fwd(proj, cos, sin, intra, qdec, kdec, cdec, tb):
    S = proj.shape[0]
    nc = S // CHUNK
    scale = RET_DK ** -0.5
    dk = RET_DK

    def body(p_ref, cos_ref, sin_ref, m_ref, qd_ref, kd_ref, cd_ref, y_ref, yr_ref, yrt_ref, st_ref, st):
        @pl.when(pl.program_id(1) == 0)
        def _():
            st[...] = jnp.zeros_like(st)

        def head_chunk(hh, c, rows, cs, sn):
            mm, qd, kd, cd = m_ref[hh], qd_ref[hh], kd_ref[hh], cd_ref[hh]
            col = lambda j: pl.ds(hh * RET_HW + j * dk, dk)
            own = pl.ds(hh * dk, dk)
            qr = _rope(p_ref[rows, col(0)].astype(f32), cs, sn)
            kr = _rope(p_ref[rows, col(1)].astype(f32), cs, sn) * scale
            qb, kb, vb = qr.astype(bf16), kr.astype(bf16), p_ref[rows, col(2)].astype(bf16)
            stb = st[hh].astype(bf16)
            st_ref[hh, c] = stb
            sc = (_nt(qb, kb) * mm).astype(bf16)
            y = _nn(sc, vb) + _nn(qb, stb) * qd
            st[hh] = st[hh] * cd + _tn((kr * kd).astype(bf16), vb)
            y_ref[rows, own] = y
            mu = jnp.mean(y, axis=-1, keepdims=True)
            yc = y - mu
            var = jnp.mean(yc * yc, axis=-1, keepdims=True)
            yr_ref[rows, own] = (yc * lax.rsqrt(var + EPS) * _silu(p_ref[rows, col(3)].astype(f32))).astype(bf16)

        def chunk(c, carry):
            rows = pl.ds(pl.multiple_of(c * CHUNK, CHUNK), CHUNK)
            cs, sn = cos_ref[rows, :], sin_ref[rows, :]
            for hh in range(RET_HP):
                head_chunk(hh, c, rows, cs, sn)
            return carry

        lax.fori_loop(0, tb // CHUNK, chunk, 0, unroll=min(RET_UNROLL_FWD, tb // CHUNK))
        yrt_ref[...] = yr_ref[...].T

    head, tab, mat, vec, one, own, stspec = _ret_specs(tb)
    return pl.pallas_call(
        body, name="ret_fwd", grid=(RET_HEADS // RET_HP, S // tb),
        in_specs=[head, tab, tab, mat, vec, vec, one],
        out_specs=[own, own, pl.BlockSpec((RET_HP * RET_DK, tb), lambda h, i: (h, i)), stspec],
        out_shape=[jax.ShapeDtypeStruct((S, 2048), f32), jax.ShapeDtypeStruct((S, 2048), bf16), jax.ShapeDtypeStruct((2048, S), bf16),
                   jax.ShapeDtypeStruct((RET_HEADS, nc, dk, dk), bf16)],
        scratch_shapes=[pltpu.VMEM((RET_HP, dk, dk), f32)], compiler_params=_params(("parallel", "arbitrary")),
    )(proj, cos, sin, intra, qdec, kdec, cdec)


def _ret_bwd(proj, cos, sin, intra, qdec, kdec, cdec, y, dyr, states, dproj, tb):
    S = proj.shape[0]
    nb = S // tb
    nck = tb // CHUNK
    scale = RET_DK ** -0.5
    dk = RET_DK

    def body(p_ref, cos_ref, sin_ref, m_ref, qd_ref, kd_ref, cd_ref, y_ref, dyr_ref, st_ref, _, o_ref, dst):
        @pl.when(pl.program_id(1) == 0)
        def _():
            dst[...] = jnp.zeros_like(dst)

        def head_chunk(hh, c, rows, cs, sn):
            mm, qd, kd, cd = m_ref[hh], qd_ref[hh], kd_ref[hh], cd_ref[hh]
            col = lambda j: pl.ds(hh * RET_HW + j * dk, dk)
            own = pl.ds(hh * dk, dk)
            qr = _rope(p_ref[rows, col(0)].astype(f32), cs, sn)
            kr = _rope(p_ref[rows, col(1)].astype(f32), cs, sn) * scale
            qb, kb, vb = qr.astype(bf16), kr.astype(bf16), p_ref[rows, col(2)].astype(bf16)
            kdb = (kr * kd).astype(bf16)
            stb = st_ref[hh, c]
            yv, gv, dyrv = y_ref[rows, own], p_ref[rows, col(3)].astype(f32), dyr_ref[rows, own]
            mu = jnp.mean(yv, axis=-1, keepdims=True)
            yc = yv - mu
            rstd = lax.rsqrt(jnp.mean(yc * yc, axis=-1, keepdims=True) + EPS)
            yn = yc * rstd
            sg = jax.nn.sigmoid(gv)
            o_ref[rows, col(3)] = (dyrv * yn * (sg * (1.0 + gv * (1.0 - sg)))).astype(bf16)
            dyn = dyrv * (gv * sg)
            dy = rstd * (dyn - jnp.mean(dyn, axis=-1, keepdims=True) - yn * jnp.mean(dyn * yn, axis=-1, keepdims=True))
            dyb = dy.astype(bf16)
            dyqb = (dy * qd).astype(bf16)
            dstb = dst[hh].astype(bf16)
            sct =(_nt(kb, qb) * mm).astype(bf16)
            ds = (_nt(dyb, vb) * mm).astype(bf16)
            dsT = (_nt(vb, dyb) * mm).astype(bf16)
            dv = _nn(sct, dyb) + _nn(kdb, dstb)
            dqr = _nn(ds, kb) + _nt(dyqb, stb)
            dkr = _nn(dsT, qb) + _nt(vb, dstb) * kd
            dst[hh] = dst[hh] * cd + _tn(qb, dyqb)
            o_ref[rows, col(0)] = _rope_t(dqr, cs, sn).astype(bf16)
            o_ref[rows, col(1)] = (_rope_t(dkr, cs, sn) * scale).astype(bf16)
            o_ref[rows, col(2)] = dv.astype(bf16)

        def chunk(cc, carry):
            c = nck - 1 - cc
            rows = pl.ds(pl.multiple_of(c * CHUNK, CHUNK), CHUNK)
            cs, sn = cos_ref[rows, :], sin_ref[rows, :]
            for hh in range(RET_HP):
                head_chunk(hh, c, rows, cs, sn)
            return carry

        lax.fori_loop(0, nck, chunk, 0, unroll=min(RET_UNROLL_BWD, nck))

    head, tab, mat, vec, one, own, stspec = _ret_specs(tb, rev_nb=nb)
    return pl.pallas_call(
        body, name="ret_bwd", grid=(RET_HEADS // RET_HP, nb),
        in_specs=[head, tab, tab, mat, vec, vec, one, own, own, stspec, ANY],
        out_specs=head, out_shape=jax.ShapeDtypeStruct(dproj.shape, bf16), input_output_aliases={10: 0},
        scratch_shapes=[pltpu.VMEM((RET_HP, dk, dk), f32)], compiler_params=_params(("parallel", "arbitrary")),
    )(proj, cos, sin, intra, qdec, kdec, cdec, y, dyr, states, dproj)


def _conv_fwd(proj, conv_w, conv_b, tb, cw):
    S = proj.shape[0]
    off = 0

    def body(x_ref, halo_ref, w_ref, b_ref, o_ref, xe):
        xe[pl.ds(0, 8), :] = jnp.where(pl.program_id(1) == 0, 0.0, halo_ref[...])
        xe[pl.ds(8, CS), :] = x_ref[pl.ds(0, CS), :]
        ws = [w_ref[pl.ds(j, 1), :] for j in range(SSD_CONV)]
        for s in range(tb // CS):
            tap = (lambda j: xe[pl.ds(5 + j, CS), :]) if s == 0 else (lambda j, s=s: x_ref[pl.ds(s * CS - 3 + j, CS), :])
            acc = b_ref[...] + ws[0] * tap(0)
            for j in range(1, SSD_CONV):
                acc = acc + ws[j] * tap(j)
            o_ref[pl.ds(s * CS, CS), :] = acc

    return pl.pallas_call(
        body, name="conv_fwd", grid=(CONV_DIM // cw, S // tb),
        in_specs=[pl.BlockSpec((tb, cw), lambda j, i: (i, off + j)),
                  pl.BlockSpec((8, cw), lambda j, i: (jnp.maximum(i * (tb // 8) - 1, 0), off + j)),
                  pl.BlockSpec((SSD_CONV, cw), lambda j, i: (0, j)), pl.BlockSpec((1, cw), lambda j, i: (0, j))],
        out_specs=pl.BlockSpec((tb, cw), lambda j, i: (i, j)),
        out_shape=jax.ShapeDtypeStruct((S, CONV_DIM), f32),
        scratch_shapes=[pltpu.VMEM((CS + 8, cw), f32)], compiler_params=_params(("parallel", "arbitrary")),
    )(proj, proj, conv_w, conv_b)


def _conv_bwd(dpre, proj, conv_w, dproj, tb, cw):
    S, n = dpre.shape
    nb = S // tb
    xoff = C_XBC // cw

    def body(d_ref, dh_ref, x_ref, xh_ref, w_ref, _, dx_ref, gw_ref, gb_ref, de, xe, accw, accb):
        i = pl.program_id(1)

        @pl.when(i == 0)
        def _():
            accw[...] = jnp.zeros_like(accw)
            accb[...] = jnp.zeros_like(accb)

        ns = tb // CS
        de[pl.ds(0, CS), :] = d_ref[pl.ds(tb - CS, CS), :]
        de[pl.ds(CS, 8), :] = jnp.where(i == nb - 1, 0.0, dh_ref[...])
        xe[pl.ds(0, 8), :] = jnp.where(i == 0, 0.0, xh_ref[...])
        xe[pl.ds(8, CS), :] = x_ref[pl.ds(0, CS), :]
        ws = [w_ref[pl.ds(j, 1), :] for j in range(SSD_CONV)]
        fold = lambda p: sum(p[8 * q:8 * (q + 1)] for q in range(1, CS // 8)) + p[0:8]
        for s in range(ns):
            dv = d_ref[pl.ds(s * CS, CS), :]
            ahead = (lambda o: de[pl.ds(o, CS), :]) if s == ns - 1 else (lambda o, s=s: d_ref[pl.ds(s * CS + o, CS), :])
            xtap = (lambda j: xe[pl.ds(5 + j, CS), :]) if s == 0 else (lambda j, s=s: x_ref[pl.ds(s * CS - 3 + j, CS), :])
            acc = ws[SSD_CONV - 1] * dv
            for j in range(SSD_CONV - 1):
                acc = acc + ws[j] * ahead(3 - j)
            dx_ref[pl.ds(s * CS, CS), :] = acc.astype(bf16)
            accb[...] += fold(dv)
            for j in range(SSD_CONV):
                accw[j] += fold(dv * xtap(j))

        @pl.when(i == nb - 1)
        def _():
            gb_ref[...] = jnp.sum(accb[...], axis=0, keepdims=True)
            for j in range(SSD_CONV):
                gw_ref[pl.ds(j, 1), :] = jnp.sum(accw[j], axis=0, keepdims=True)

    return pl.pallas_call(
        body, name="conv_bwd", grid=(n // cw, nb),
        in_specs=[pl.BlockSpec((tb, cw), lambda j, i: (i, j)),
                  pl.BlockSpec((8, cw), lambda j, i: (jnp.minimum((i + 1) * (tb // 8), S // 8 - 1), j)),
                  pl.BlockSpec((tb, cw), lambda j, i: (i, j)),
                  pl.BlockSpec((8, cw), lambda j, i: (jnp.maximum(i * (tb // 8) - 1, 0), j)),
                  pl.BlockSpec((SSD_CONV, cw), lambda j, i: (0, j)), ANY],
        out_specs=[pl.BlockSpec((tb, cw), lambda j, i: (i, xoff + j)), pl.BlockSpec((SSD_CONV, cw), lambda j, i: (0, j)),
                   pl.BlockSpec((1, cw), lambda j, i: (0, j))],
        out_shape=[jax.ShapeDtypeStruct(dproj.shape, bf16), jax.ShapeDtypeStruct((SSD_CONV, n), f32), jax.ShapeDtypeStruct((1, n), f32)],
        input_output_aliases={5: 0},
        scratch_shapes=[pltpu.VMEM((CS + 8, cw), f32), pltpu.VMEM((CS + 8, cw), f32), pltpu.VMEM((SSD_CONV, 8, cw), f32),
                        pltpu.VMEM((8, cw), f32)],
        compiler_params=_params(("parallel", "arbitrary")),
    )(dpre, dpre, proj, proj, conv_w, dproj)


def _dt_prep(dt_raw, dt_bias, a_log, tb):
    S = dt_raw.shape[0]

    def body(r_ref, b_ref, al_ref, dt_ref, sg_ref, ac_ref):
        li = lax.broadcasted_iota(jnp.int32, (LS, LS), 0)
        si = lax.broadcasted_iota(jnp.int32, (LS, LS), 1)
        tri = (li >= si).astype(f32)
        neg_a = -jnp.exp(al_ref[...])
        for c in range(tb // LS):
            rows = pl.ds(c * LS, LS)
            xv = r_ref[rows, :] + b_ref[...]
            dtv = jax.nn.softplus(xv)
            dt_ref[rows, :] = dtv
            sg_ref[rows, :] = jax.nn.sigmoid(xv)
            ac_ref[rows, :] = _hi(tri, dtv * neg_a)

    row = pl.BlockSpec((tb, 128), lambda i: (i, 0))
    vec = pl.BlockSpec((1, 128), lambda i: (0, 0))
    o = jax.ShapeDtypeStruct((S, 128), f32)
    return pl.pallas_call(body, name="dt_prep", grid=(S // tb,), in_specs=[row, vec, vec], out_specs=[row, row, row],
                          out_shape=[o, o, o], compiler_params=_params(("parallel",)))(dt_raw, dt_bias, a_log)


def _group_major(t):
    S = t.shape[0]
    return jnp.transpose(t[:, :SSD_HEADS].reshape(S, SSD_GROUPS, SSD_HPG), (1, 0, 2))


def _group_major_t(t):
    S = t.shape[0]
    return jnp.transpose(t[:, :SSD_HEADS].reshape(S // LS, LS, SSD_GROUPS, SSD_HPG), (2, 0, 3, 1))


def _ssd_specs(tb, rev_nb=None):
    def blk(i):
        return i if rev_nb is None else rev_nb - 1 - i
    grp = pl.BlockSpec((tb, SSD_GC), lambda g, i: (blk(i), g))
    xs = pl.BlockSpec((tb, SSD_GW), lambda g, i: (blk(i), g))
    ph = pl.BlockSpec((1, tb, SSD_HPG), lambda g, i: (g, blk(i), 0))
    pht = pl.BlockSpec((1, tb // LS, SSD_HPG, LS), lambda g, i: (g, blk(i), 0, 0))
    gvec = pl.BlockSpec((1, 1, SSD_GW), lambda g, i: (g, 0, 0))
    ex = pl.BlockSpec((SSD_HPG, SSD_GW), lambda g, i: (0, 0))
    st = pl.BlockSpec((1, tb // LS, SSD_STATE, SSD_GW), lambda g, i: (g, blk(i), 0, 0))
    return grp, xs, ph, pht, gvec, ex, st


def _expander():
    return jnp.repeat(jnp.eye(SSD_HPG, dtype=f32), SSD_GW // SSD_HPG, axis=1).astype(bf16)


def _expand3(dt8, ac8, ex):
    stack = jnp.concatenate([dt8, jnp.exp(ac8), jnp.exp(ac8[LS - 1:LS, :] - ac8)], axis=0)
    wide = _sel_r(stack, ex)
    return wide[0:LS], wide[LS:2 * LS], wide[2 * LS:3 * LS]


def _ssd_fwd(pre, dt_g, ac_g, act_g, dskx, tb):
    S = pre.shape[0]
    nc = S // LS
    hd = SSD_GW // SSD_HPG

    def body(p_ref, dt_ref, ac_ref, act_ref, dsk_ref, ex_ref, y_ref, st_ref, st):
        @pl.when(pl.program_id(1) == 0)
        def _():
            st[...] = jnp.zeros_like(st)

        ex = ex_ref[...]
        li = lax.broadcasted_iota(jnp.int32, (LS, LS), 0)
        si = lax.broadcasted_iota(jnp.int32, (LS, LS), 1)
        causal = li >= si

        def chunk(c, carry):
            rows = pl.ds(pl.multiple_of(c * LS, LS), LS)
            xs = _silu(p_ref[rows, pl.ds(0, SSD_GW)])
            bcb = _silu(p_ref[rows, pl.ds(SSD_GW, SSD_STATE)]).astype(bf16)
            ccb = _silu(p_ref[rows, pl.ds(SSD_GW + SSD_STATE, SSD_STATE)]).astype(bf16)
            dt8, ac8, act = dt_ref[0, rows, :], ac_ref[0, rows, :], act_ref[0, c]
            dtx, eax, tailx = _expand3(dt8, ac8, ex)
            xdt = xs * dtx
            cb = _nt(ccb, bcb)
            stb = st[...].astype(bf16)
            st_ref[0, c] = stb
            xdtb = xdt.astype(bf16)
            outs = []
            for h in range(SSD_HPG):
                dec = jnp.exp(jnp.where(causal, ac8[:, h:h + 1] - act[h:h + 1, :], -1e30))
                outs.append(_nn((cb * dec).astype(bf16), xdtb[:, hd * h:hd * (h + 1)]))
            y_ref[rows, :] = (jnp.concatenate(outs, axis=1) + _nn(ccb, stb) * eax + dsk_ref[0] * xs).astype(bf16)
            st[...] = st[...] * eax[LS - 1:LS, :] + _tn(bcb, (xdt * tailx).astype(bf16))
            return carry

        lax.fori_loop(0, tb // LS, chunk, 0, unroll=min(4, tb // LS))

    grp, xs, ph, pht, gvec, ex, stspec = _ssd_specs(tb)
    return pl.pallas_call(
        body, name="ssd_fwd", grid=(SSD_GROUPS, S // tb),
        in_specs=[grp, ph, ph, pht, gvec, ex], out_specs=[xs, stspec],
        out_shape=[jax.ShapeDtypeStruct((S, SSD_WIDTH), bf16), jax.ShapeDtypeStruct((SSD_GROUPS, nc, SSD_STATE, SSD_GW), bf16)],
        scratch_shapes=[pltpu.VMEM((SSD_STATE, SSD_GW), f32)], compiler_params=_params(("parallel", "arbitrary")),
    )(pre, dt_g, ac_g, act_g, dskx, _expander())


def _ssd_bwd(pre, dt_g, ac_g, act_g, sg_g, dskx, nega_g, dy, states, tb):
    S = pre.shape[0]
    nb = S // tb
    nck = tb // LS
    hd = SSD_GW // SSD_HPG

    def body(p_ref, dt_ref, ac_ref, act_ref, sg_ref, dsk_ref, na_ref, ex_ref, ext_ref, dy_ref, st_ref,
             dp_ref, ddt_ref, gsk_ref, gal_ref, gdb_ref, dst, skacc):
        @pl.when(pl.program_id(1) == 0)
        def _():
            dst[...] = jnp.zeros_like(dst)
            skacc[...] = jnp.zeros_like(skacc)
            gal_ref[...] = jnp.zeros_like(gal_ref)
            gdb_ref[...] = jnp.zeros_like(gdb_ref)

        ex, ext = ex_ref[...], ext_ref[...]
        li = lax.broadcasted_iota(jnp.int32, (LS, LS), 0)
        si = lax.broadcasted_iota(jnp.int32, (LS, LS), 1)
        causal = li >= si
        anti = si >= li
        upper = anti.astype(bf16)
        eye = (si == li).astype(bf16)
        last_row = (lax.broadcasted_iota(jnp.int32, (LS, 1), 0) == LS - 1).astype(f32)
        head_id = lax.broadcasted_iota(jnp.int32, (1, SSD_HPG), 1)
        head_col = lax.broadcasted_iota(jnp.int32, (SSD_HPG, 1), 0)
        neg_a = na_ref[0]
        dskv = dsk_ref[0]

        def chunk(cc, carry):
            c = nck - 1 - cc
            rows = pl.ds(pl.multiple_of(c * LS, LS), LS)
            px = p_ref[rows, pl.ds(0, SSD_GW)]
            pb = p_ref[rows, pl.ds(SSD_GW, SSD_STATE)]
            pc = p_ref[rows, pl.ds(SSD_GW + SSD_STATE, SSD_STATE)]
            sgx, sgb, sgc = jax.nn.sigmoid(px), jax.nn.sigmoid(pb), jax.nn.sigmoid(pc)
            xs = px * sgx
            bcb = (pb * sgb).astype(bf16)
            ccb = (pc * sgc).astype(bf16)
            dt8, ac8, act = dt_ref[0, rows, :], ac_ref[0, rows, :], act_ref[0, c]
            dtx, eax, tailx = _expand3(dt8, ac8, ex)
            xdt = xs * dtx
            ex_last = eax[LS - 1:LS, :]
            stb = st_ref[0, c]
            dyv = dy_ref[rows, :]
            dyb = dyv.astype(bf16)
            xdtb = xdt.astype(bf16)
            skacc[...] += jnp.sum(dyv * xs, axis=0, keepdims=True)
            yinter = _nn(ccb, stb) * eax
            dzb = (dyv * eax).astype(bf16)
            dcc = _nt(dzb, stb)
            dstv = dst[...]
            dstb = dstv.astype(bf16)
            xt = xdt * tailx
            dxt = _nn(bcb, dstb)
            dbc = _nt(xt.astype(bf16), dstb)
            dxdt = dxt * tailx
            lastrow = jnp.sum(dxt * xt, axis=0, keepdims=True) + jnp.sum(dstv * stb.astype(f32), axis=0, keepdims=True) * ex_last
            dst[...] = dstv * ex_last + _tn(ccb, dzb)
            cb = _nt(ccb, bcb)
            cbt = _nt(bcb, ccb)
            dcb = jnp.zeros((LS, LS), f32)
            dac8 = jnp.zeros((LS, SSD_HPG), f32)
            dact = jnp.zeros((SSD_HPG, LS), f32)
            dxin = []
            for h in range(SSD_HPG):
                sl = slice(hd * h, hd * (h + 1))
                col, rowv = ac8[:, h:h + 1], act[h:h + 1, :]
                dec = jnp.exp(jnp.where(causal, col - rowv, -1e30))
                dect = jnp.exp(jnp.where(anti, rowv - col, -1e30))
                gm = cb * dec
                dgm = _nt(dyb[:, sl], xdtb[:, sl])
                dxin.append(_nn((cbt * dect).astype(bf16), dyb[:, sl]))
                dcb = dcb + dgm * dec
                w = dgm * gm
                dac8 = dac8 + jnp.sum(w, axis=1, keepdims=True) * (head_id == h).astype(f32)
                dact = dact + (head_col == h).astype(f32) * jnp.sum(w, axis=0, keepdims=True)
            dxintra = jnp.concatenate(dxin, axis=1)
            dcbb = dcb.astype(bf16)
            dcc = dcc + _nn(dcbb, bcb)
            dbc = dbc + _tn(dcbb, ccb)
            dxdt = dxdt + dxintra
            dacx = dyv * yinter - dxt * xt + last_row * lastrow
            red = _sel_r(jnp.concatenate([dacx, dxdt * xs], axis=0), ext)
            dac8 = dac8 - _rows_to_cols(dact, eye) + red[0:LS]
            da8 = _sel_l(upper, dac8)
            ddt8 = red[LS:2 * LS] + da8 * neg_a
            gal_ref[0] += jnp.sum(da8 * dt8 * neg_a, axis=0, keepdims=True)
            ddr = ddt8 * sg_ref[0, rows, :]
            ddt_ref[0, rows, :] = ddr
            gdb_ref[0] += jnp.sum(ddr, axis=0, keepdims=True)
            dsilu = lambda p, s: s * (1.0 + p * (1.0 - s))
            dp_ref[rows, pl.ds(0, SSD_GW)] = (dskv * dyv + dxdt * dtx) * dsilu(px, sgx)
            dp_ref[rows, pl.ds(SSD_GW, SSD_STATE)] = dbc * dsilu(pb, sgb)
            dp_ref[rows, pl.ds(SSD_GW + SSD_STATE, SSD_STATE)] = dcc * dsilu(pc, sgc)
            return carry

        lax.fori_loop(0, nck, chunk, 0, unroll=min(4, nck))

        @pl.when(pl.program_id(1) == nb - 1)
        def _():
            gsk_ref[0] = skacc[...]

    grp, xs, ph, pht, gvec, ex, stspec = _ssd_specs(tb, rev_nb=nb)
    small = pl.BlockSpec((1, 1, SSD_HPG), lambda g, i: (g, 0, 0))
    ext = pl.BlockSpec((SSD_GW, SSD_HPG), lambda g, i: (0, 0))
    sm = jax.ShapeDtypeStruct((SSD_GROUPS, 1, SSD_HPG), f32)
    expander = _expander()
    return pl.pallas_call(
        body, name="ssd_bwd", grid=(SSD_GROUPS, nb),
        in_specs=[grp, ph, ph, pht, ph, gvec, small, ex, ext, xs, stspec],
        out_specs=[grp, ph, gvec, small, small],
        out_shape=[jax.ShapeDtypeStruct((S, CONV_DIM), f32), jax.ShapeDtypeStruct((SSD_GROUPS, S, SSD_HPG), f32),
                   jax.ShapeDtypeStruct((SSD_GROUPS, 1, SSD_GW), f32), sm, sm],
        scratch_shapes=[pltpu.VMEM((SSD_STATE, SSD_GW), f32), pltpu.VMEM((1, SSD_GW), f32)],
        compiler_params=_params(("parallel", "arbitrary")),
    )(pre, dt_g, ac_g, act_g, sg_g, dskx, nega_g, expander, expander.T, dy, states)


def _tiles(S):
    return dict(tb=min(512, S), tr=min(256, S), tm=min(1024, S))


def _local_step(x, positions, target, norm1_w, weights, conv_b, dt_bias, a_log, d_skip, ssd_norm_w, norm_f_w, reducer):
    S, D = x.shape
    t = _tiles(S)
    tb, tr, tm = t["tb"], t["tr"], t["tm"]

    half = RET_DK // 2
    inv_freq = ROPE_THETA ** (-jnp.arange(half, dtype=f32) / half)
    ang = positions.astype(f32)[:, None] * inv_freq
    cos, sin = jnp.cos(ang), jnp.sin(ang)
    log_gamma = jnp.log1p(-(2.0 ** (-5.0 - jnp.arange(RET_HEADS, dtype=f32))))
    idx = jnp.arange(CHUNK, dtype=f32)
    intra = jnp.exp(jnp.abs(idx[:, None] - idx[None, :]) * log_gamma[:, None, None])
    qdec = jnp.exp((idx + 1.0)[None, :] * log_gamma[:, None])[:, :, None]
    kdec = jnp.exp((CHUNK - 1.0 - idx)[None, :] * log_gamma[:, None])[:, :, None]
    cdec = jnp.exp(CHUNK * log_gamma)[:, None, None]

    h, ht = _norm1_fwd(x, norm1_w + weights.token[0, 0], tr)
    w_main, w_dt, conv_w = weights.main(h)
    conv_wm, conv_bm = _xbc_group_major(conv_w), _xbc_group_major(conv_b)
    proj = _mm([(h, 0, w_main, 0, 0, D, D)], S, C_XBC, tm=tm, tn=1024, out_dtype=bf16, name="proj_main")
    proj_x = _mm([(h, 0, w_main, 0, C_XBC // 1024, D, D)], S, CONV_DIM, tm=tm, tn=1024, out_dtype=f32, name="proj_xbc")
    dt_raw = _mm1(h, w_dt, tm=tm, tn=128, tk=D, out_dtype=f32, name="proj_dt")
    y_ret, yr, yrt, ret_states = _ret_fwd(proj, cos, sin, intra, qdec, kdec, cdec, tb)
    pre = _conv_fwd(proj_x, conv_wm, conv_bm, min(1024, S), 512)
    pad64 = lambda v: jnp.pad(v, ((0, 0), (0, 128 - SSD_HEADS)))
    dt, sg, ac = _dt_prep(dt_raw, pad64(dt_bias), pad64(a_log), tb)
    dt_g, ac_g, sg_g, act_g = _group_major(dt), _group_major(ac), _group_major(sg), _group_major_t(ac)
    dskx = jnp.repeat(d_skip.reshape(SSD_GROUPS, 1, SSD_HPG), SSD_GW // SSD_HPG, axis=2)
    nega_g = (-jnp.exp(a_log)).reshape(SSD_GROUPS, 1, SSD_HPG)
    y_ssd, ssd_states = _ssd_fwd(pre, dt_g, ac_g, act_g, dskx, tb)
    ys, yst = _ssd_norm_fwd(y_ssd, proj, ssd_norm_w, tr // 2)
    w_br, w_bs, w_o = weights.late(ys)
    p_r = _mm1(yr, w_br, tm=tm, tn=1024, tk=2048, out_dtype=bf16, name="branch_ret")
    p_s = _mm1(ys, w_bs, tm=tm, tn=1024, tk=4096, out_dtype=bf16, name="branch_ssd")
    merged, mergedt = _merge_fwd(p_r, p_s, proj, tr)
    mo = _mm1(merged, w_o, tm=tm, tn=1024, tk=2048, out_dtype=bf16, name="out_proj")
    dx2, dx2b, loss, g_norm_f = _final_fwd_bwd(x, mo, target, norm_f_w.reshape(1, D), tr)

    tkt = min(4096, S)
    wg = lambda at, b, name, tn=1024: _mm1(at, b, tm=min(1024, at.shape[0]), tn=tn, tk=tkt, out_dtype=f32, name=name)
    dm = _mm1(dx2b, w_o, tm=tm, tn=1024, tk=2048, out_dtype=bf16, name="d_merged", tb=True)
    g_w_o = wg(mergedt, dx2b, "g_w_out")
    dp_r, dp_s, dproj = _merge_bwd(dm, p_r, p_s, proj, tr)
    dyr = _mm1(dp_r, w_br, tm=tm, tn=1024, tk=2048, out_dtype=bf16, name="d_yr", tb=True)
    dys = _mm1(dp_s, w_bs, tm=tm, tn=1024, tk=2048, out_dtype=bf16, name="d_ys", tb=True)
    g_w_br = wg(yrt, dp_r, "g_w_br_ret")
    g_w_bs = wg(yst, dp_s, "g_w_br_ssd")
    dy_ssd, dproj, g_ssd_norm = _ssd_norm_bwd(y_ssd, proj, ssd_norm_w, dys, dproj, tr // 2)
    dproj = _ret_bwd(proj, cos, sin, intra, qdec, kdec, cdec, y_ret, dyr, ret_states, dproj, tb)
    dpre, ddt_g, gsk, gal, gdb = _ssd_bwd(pre, dt_g, ac_g, act_g, sg_g, dskx, nega_g, dy_ssd, ssd_states, tb)
    dproj, gcw, gcb = _conv_bwd(dpre, proj_x, conv_wm, dproj, min(1024, S), 512)
    ddt = jnp.transpose(ddt_g, (1, 0, 2)).reshape(S, SSD_HEADS)
    ddt_p = jnp.pad(ddt, ((0, 0), (0, 128 - SSD_HEADS))).astype(bf16)

    hr = D // 2
    wg_half = lambda off, b, name, tn=1024: _mm([(ht, 0, b, 0, 0, S, tkt)], hr, b.shape[1], tm=hr, tn=tn, out_dtype=f32,
                                                 name=name, row_off=off)
    off_sib, off_own = reducer.halves()
    gs_main = wg_half(off_sib, dproj, "g_w_in_main_sib")
    gs_dt = wg_half(off_sib, ddt_p, "g_w_in_dt_sib", tn=128)
    swap_state = reducer.first(gs_main, gs_dt, g_w_br, g_w_bs, g_w_o)
    ddt_p = ddt_p + swap_state[-1][0, 0].astype(bf16)
    go_main = wg_half(off_own, dproj, "g_w_in_main_own")
    go_dt = wg_half(off_own, ddt_p, "g_w_in_dt_own", tn=128)
    reduce_state = reducer.second(swap_state, go_main, go_dt)
    ddt_p = ddt_p + reduce_state[-1][0, 0].astype(bf16)
    dh = _mm([(dproj, 0, w_main, 0, 0, N_MAIN, N_MAIN // 8), (ddt_p, 0, w_dt, 0, 0, 128, 128)], S, D, tm=tm, tn=1024,
             out_dtype=bf16, name="d_h", tb=True)
    grad_x, g_norm1 = _norm1_bwd(x, norm1_w, dh, dx2, tr)

    seg = lambda v: jnp.sum(v.reshape(SSD_HEADS, SSD_GW // SSD_HPG), axis=1).reshape(1, SSD_HEADS)
    grads = dict(
        norm1_w=g_norm1, w_in_main=(gs_main, go_main), w_in_dt=(gs_dt, go_dt),
        conv_w=_xbc_original(gcw), conv_b=_xbc_original(gcb),
        dt_bias=gdb.reshape(1, SSD_HEADS), a_log=gal.reshape(1, SSD_HEADS), d_skip=seg(gsk),
        ssd_norm_w=g_ssd_norm, w_br_ret=g_w_br, w_br_ssd=g_w_bs, w_out=g_w_o, norm_f_w=g_norm_f,
    )
    return loss, grad_x, grads, reduce_state


def _me():
    return lax.axis_index("x"), lax.axis_index("y"), lax.axis_index("c")


def _other_chips(x, y):
    return [(1 - x, y), (x, 1 - y), (1 - x, 1 - y)]


def _quarter_rows(n_rows, half, q):
    return pl.ds(pl.multiple_of(half * (n_rows // 2) + q * (n_rows // 4), 8), n_rows // 4)


def _gather_own_copies(srcs, lands, send_sems, recv_sems):
    (a_ref, cw_ref), (ga_ref, gc_ref) = srcs, lands
    x, y, c = _me()
    k, nx, ny = 2 * x + y, (1 - x, y, c), (x, 1 - y, c)

    def cp(sem, q, to):
        rows = _quarter_rows(a_ref.shape[0], c, q)
        return pltpu.make_async_remote_copy(src_ref=a_ref.at[rows, :], dst_ref=ga_ref.at[k, rows, :], send_sem=send_sems.at[sem],
                                            recv_sem=recv_sems.at[sem], device_id=to, device_id_type=MESH)

    small = [pltpu.make_async_remote_copy(src_ref=cw_ref, dst_ref=gc_ref.at[k], send_sem=send_sems.at[4 + j],
                                          recv_sem=recv_sems.at[4 + j], device_id=(*chip, c), device_id_type=MESH)
             for j, chip in enumerate(_other_chips(x, y))]
    return [cp(0, 0, nx), cp(2, 1, ny), cp(1, 1, nx), cp(3, 0, ny)] + small


def _gather_own_start(a, cw):
    srcs = [a, cw]
    lands = [lax.empty((N_SHARD,) + a.shape, a.dtype), lax.empty((N_SHARD,) + cw.shape, cw.dtype)]

    def body(*refs):
        for cp in _gather_own_copies(refs[0:2], refs[2:4], refs[4], refs[5]):
            cp.start()
        refs[10][...] = jnp.zeros_like(refs[10])

    hbm = lambda t: pltpu.HBM(t.shape, t.dtype)
    out = pl.pallas_call(
        body, name="gather_own_start", in_specs=[HBM] * 4,
        out_specs=(SEM, SEM, *[HBM] * 4, pl.BlockSpec(memory_space=pltpu.VMEM)),
        out_shape=(pltpu.SemaphoreType.DMA((7,)), pltpu.SemaphoreType.DMA((7,)), *[hbm(t) for t in srcs + lands],
                   jax.ShapeDtypeStruct((8, 128), f32)),
        input_output_aliases={t: 2 + t for t in range(4)}, compiler_params=pltpu.CompilerParams(has_side_effects=DATAFLOW),
    )(*[pltpu.with_memory_space_constraint(t, pltpu.HBM) for t in srcs + lands])
    return out[0], out[1], list(out[2:4]), list(out[4:6]), out[6]


def _gather_own_wait(send_sems, recv_sems, srcs, lands, after):
    def body(*refs):
        for cp in _gather_own_copies(refs[0:2], refs[2:4], refs[4], refs[5]):
            cp.wait_send()
            cp.wait_recv()

    hbm = lambda t: pltpu.HBM(t.shape, t.dtype)
    out = pl.pallas_call(
        body, name="gather_own_wait", in_specs=[HBM] * 4 + [SEM, SEM, ANY], out_specs=[HBM] * 4,
        out_shape=[hbm(t) for t in list(srcs) + list(lands)], input_output_aliases={t: t for t in range(4)},
        compiler_params=pltpu.CompilerParams(has_side_effects=DATAFLOW),
    )(*srcs, *lands, send_sems, recv_sems, after)
    return list(out[2:])


def _gather_pass_on(ga):
    R = ga.shape[1]

    def body(_, ga_ref, send_sems, recv_sems):
        x, y, c = _me()
        me, sibling, nx, ny = (x, y, c), (x, y, 1 - c), (1 - x, y, c), (x, 1 - y, c)
        kx, ky, kd = 2 * (1 - x) + y, 2 * x + (1 - y), 2 * (1 - x) + (1 - y)

        def cp(sem, shard, half, q, to):
            ref = ga_ref.at[shard, _quarter_rows(R, half, q), :]
            return pltpu.make_async_remote_copy(src_ref=ref, dst_ref=ref, send_sem=send_sems.at[sem], recv_sem=recv_sems.at[sem],
                                                device_id=to, device_id_type=MESH)

        landed = [(kx, 0), (kx, 1), (ky, 1), (ky, 0)]
        sends = [cp(0, kx, c, 0, ny), cp(1, ky, c, 1, nx)] + [cp(2 + i, shard, c, q, sibling) for i, (shard, q) in enumerate(landed)]
        for s in sends:
            s.start()
        for sem in (0, 1):
            cp(sem, kd, c, sem, me).wait_recv()
            sends.append(cp(6 + sem, kd, c, sem, sibling))
            sends[-1].start()
        for i, (shard, q) in enumerate(landed + [(kd, 0), (kd, 1)]):
            cp(2 + i, shard, 1 - c, q, me).wait_recv()
        for s in sends:
            s.wait_send()

    return pl.pallas_call(
        body, name="gather_pass_on", in_specs=[ANY], out_specs=ANY, out_shape=jax.ShapeDtypeStruct(ga.shape, ga.dtype),
        input_output_aliases={0: 0}, scratch_shapes=[pltpu.SemaphoreType.DMA((8,)), pltpu.SemaphoreType.DMA((8,))],
        compiler_params=pltpu.CompilerParams(has_side_effects=True),
    )(ga)


def _gather_late_copies(src, land, send_sems, recv_sems):
    x, y, c = _me()
    k = 2 * x + y
    return [pltpu.make_async_remote_copy(src_ref=src, dst_ref=land.at[k], send_sem=send_sems.at[j], recv_sem=recv_sems.at[j],
                                         device_id=(*chip, c), device_id_type=MESH) for j, chip in enumerate(_other_chips(x, y))]


def _gather_late_start(b):
    land = lax.empty((N_SHARD,) + b.shape, b.dtype)

    def body(b_ref, land_ref, send_sems, recv_sems, b_thru, land_thru, token):
        for cp in _gather_late_copies(b_ref, land_ref, send_sems, recv_sems):
            cp.start()
        token[...] = jnp.zeros_like(token)

    return pl.pallas_call(
        body, name="gather_late_start", in_specs=[HBM, HBM],
        out_specs=(SEM, SEM, HBM, HBM, pl.BlockSpec(memory_space=pltpu.VMEM)),
        out_shape=(pltpu.SemaphoreType.DMA((3,)), pltpu.SemaphoreType.DMA((3,)), pltpu.HBM(b.shape, b.dtype),
                   pltpu.HBM(land.shape, land.dtype), jax.ShapeDtypeStruct((8, 128), f32)),
        input_output_aliases={0: 2, 1: 3}, compiler_params=pltpu.CompilerParams(has_side_effects=DATAFLOW),
    )(pltpu.with_memory_space_constraint(b, pltpu.HBM), pltpu.with_memory_space_constraint(land, pltpu.HBM))


def _gather_late_wait(send_sems, recv_sems, src, land, after):
    def body(b_ref, land_ref, send_sems_ref, recv_sems_ref, after_ref, b_dead, land_out):
        x, y, c = _me()
        for j, chip in enumerate(_other_chips(x, y)):
            kk = 2 * chip[0] + chip[1]
            cp = pltpu.make_async_remote_copy(src_ref=b_ref, dst_ref=land_ref.at[kk], send_sem=send_sems_ref.at[j],
                                              recv_sem=recv_sems_ref.at[j], device_id=(x, y, c), device_id_type=MESH)
            cp.wait_send()
            cp.wait_recv()

    return pl.pallas_call(
        body, name="gather_late_wait", in_specs=[HBM, HBM, SEM, SEM, ANY], out_specs=[HBM, HBM],
        out_shape=[pltpu.HBM(src.shape, src.dtype), pltpu.HBM(land.shape, land.dtype)], input_output_aliases={0: 0, 1: 1},
        compiler_params=pltpu.CompilerParams(has_side_effects=DATAFLOW),
    )(src, land, send_sems, recv_sems, after)[1]


HBM = pl.BlockSpec(memory_space=pltpu.HBM)
SEM = pl.BlockSpec(memory_space=pltpu.SEMAPHORE)
DATAFLOW = pltpu.SideEffectType.DATAFLOW_SIDE_EFFECTING


def _swap_copies(srcs, lands, send_sems, recv_sems):
    x, y, c = _me()

    def cp(src, dst, q):
        return pltpu.make_async_remote_copy(src_ref=src, dst_ref=dst, send_sem=send_sems.at[q], recv_sem=recv_sems.at[q],
                                            device_id=(x, y, 1 - c), device_id_type=MESH)

    return [cp(srcs[0], lands[0], 0), cp(srcs[1], lands[1], 1)] + [cp(srcs[2].at[s, 1 - c], lands[2].at[s], 2 + s) for s in range(N_SHARD)]


def _sibling_swap_start(g_main, g_dt, g_b):
    srcs = [g_main, g_dt, g_b]
    lands = [lax.empty(g_main.shape, g_main.dtype), lax.empty(g_dt.shape, g_dt.dtype),
             lax.empty(g_b.shape[:1] + g_b.shape[2:], g_b.dtype)]

    def body(*refs):
        for cp in _swap_copies(refs[0:3], refs[3:6], refs[6], refs[7]):
            cp.start()
        refs[14][...] = jnp.zeros_like(refs[14])

    hbm = lambda a: pltpu.HBM(a.shape, a.dtype)
    out = pl.pallas_call(
        body, name="sibling_swap_start", in_specs=[HBM] * 6,
        out_specs=(SEM, SEM, *[HBM] * 6, pl.BlockSpec(memory_space=pltpu.VMEM)),
        out_shape=(pltpu.SemaphoreType.DMA((2 + N_SHARD,)), pltpu.SemaphoreType.DMA((2 + N_SHARD,)), *[hbm(a) for a in srcs + lands],
                   jax.ShapeDtypeStruct((8, 128), f32)),
        input_output_aliases={t: 2 + t for t in range(6)}, compiler_params=pltpu.CompilerParams(has_side_effects=DATAFLOW),
    )(*[pltpu.with_memory_space_constraint(a, pltpu.HBM) for a in srcs + lands])
    return out[0], out[1], list(out[2:5]), list(out[5:8]), out[8]


def _sibling_swap_wait(send_sems, recv_sems, srcs, lands, after):
    def body(*refs):
        for cp in _swap_copies(refs[0:3], refs[3:6], refs[6], refs[7]):
            cp.wait_send()
            cp.wait_recv()

    hbm = lambda a: pltpu.HBM(a.shape, a.dtype)
    out = pl.pallas_call(
        body, name="sibling_swap_wait", in_specs=[HBM] * 6 + [SEM, SEM, ANY], out_specs=[HBM] * 6,
        out_shape=[hbm(a) for a in list(srcs) + list(lands)], input_output_aliases={t: t for t in range(6)},
        compiler_params=pltpu.CompilerParams(has_side_effects=DATAFLOW),
    )(*srcs, *lands, send_sems, recv_sems, after)
    return list(out[:3]), list(out[3:])


def _exchange_copies(ins, lands, send_sems, recv_sems):
    n = len(ins)
    x, y, c = _me()
    cps = []
    for j, chip in enumerate(_other_chips(x, y)):
        kk = 2 * chip[0] + chip[1]
        for t in range(n):
            cps.append(pltpu.make_async_remote_copy(
                src_ref=ins[t].at[kk], dst_ref=lands[t].at[j], send_sem=send_sems.at[n * j + t],
                recv_sem=recv_sems.at[n * j + t], device_id=(*chip, c), device_id_type=MESH))
    return cps


def _chip_exchange_start(arrs):
    n = len(arrs)
    lands = [lax.empty((3,) + a.shape[1:], a.dtype) for a in arrs]

    def body(*refs):
        ins, lands_in = refs[:n], refs[n:2 * n]
        send_sems, recv_sems = refs[2 * n], refs[2 * n + 1]
        token = refs[4 * n + 2]
        for cp in _exchange_copies(ins, lands_in, send_sems, recv_sems):
            cp.start()
        token[...] = jnp.zeros_like(token)

    hbm = lambda a: pltpu.HBM(a.shape, a.dtype)
    out = pl.pallas_call(
        body, name="chip_exchange_start", in_specs=[HBM] * (2 * n),
        out_specs=(SEM, SEM, *[HBM] * (2 * n), pl.BlockSpec(memory_space=pltpu.VMEM)),
        out_shape=(pltpu.SemaphoreType.DMA((3 * n,)), pltpu.SemaphoreType.DMA((3 * n,)), *[hbm(a) for a in arrs],
                   *[hbm(a) for a in lands], jax.ShapeDtypeStruct((8, 128), f32)),
        input_output_aliases={t: 2 + t for t in range(2 * n)},
        compiler_params=pltpu.CompilerParams(has_side_effects=DATAFLOW),
    )(*[pltpu.with_memory_space_constraint(a, pltpu.HBM) for a in list(arrs) + lands])
    return out[0], out[1], list(out[2:2 + n]), list(out[2 + n:2 + 2 * n]), out[2 + 2 * n]


def _chip_exchange_wait(send_sems, recv_sems, srcs, lands, after):
    n = len(srcs)

    def body(*refs):
        ins, lands_in = refs[:n], refs[n:2 * n]
        send_sems_ref, recv_sems_ref = refs[2 * n], refs[2 * n + 1]
        for cp in _exchange_copies(ins, lands_in, send_sems_ref, recv_sems_ref):
            cp.wait_send()
            cp.wait_recv()

    hbm = lambda a: pltpu.HBM(a.shape, a.dtype)
    out = pl.pallas_call(
        body, name="chip_exchange_wait", in_specs=[HBM] * (2 * n) + [SEM, SEM, ANY],
        out_specs=[HBM] * (2 * n), out_shape=[hbm(a) for a in list(srcs) + list(lands)],
        input_output_aliases={t: t for t in range(2 * n)},
        compiler_params=pltpu.CompilerParams(has_side_effects=DATAFLOW),
    )(*srcs, *lands, send_sems, recv_sems, after)
    return list(out[:n]), list(out[n:])


def _share_halves(bufs, by_cols, name):
    n = len(bufs)

    def body(*refs):
        ins, outs = refs[:n], refs[n:2 * n]
        send_sems, recv_sems = refs[2 * n], refs[2 * n + 1]
        x, y, c = _me()

        def part(ref, t, half):
            if by_cols[t]:
                w = bufs[t].shape[1] // 2
                return ref.at[:, pl.ds(pl.multiple_of(half * w, 128), w)]
            return ref.at[half]

        sends = [pltpu.make_async_remote_copy(src_ref=part(ins[t], t, c), dst_ref=part(outs[t], t, c), send_sem=send_sems.at[t],
                                              recv_sem=recv_sems.at[t], device_id=(x, y, 1 - c), device_id_type=MESH) for t in range(n)]
        for cp in sends:
            cp.start()
        for t in range(n):
            pltpu.make_async_remote_copy(src_ref=part(ins[t], t, c), dst_ref=part(outs[t], t, 1 - c), send_sem=send_sems.at[t],
                                         recv_sem=recv_sems.at[t], device_id=(x, y, c), device_id_type=MESH).wait_recv()
        for cp in sends:
            cp.wait_send()

    return pl.pallas_call(
        body, name=name, in_specs=[ANY] * n, out_specs=[ANY] * n,
        out_shape=[jax.ShapeDtypeStruct(a.shape, a.dtype) for a in bufs], input_output_aliases={t: t for t in range(n)},
        scratch_shapes=[pltpu.SemaphoreType.DMA((n,)), pltpu.SemaphoreType.DMA((n,))],
        compiler_params=pltpu.CompilerParams(has_side_effects=True),
    )(*bufs)


def _gather_vec(v):
    n = v.shape[1]

    def body(v_ref, o_ref, send_sems, recv_sems):
        x, y, c = _me()
        me = 4 * x + 2 * y + c
        cps = []
        for j in range(1, 8):
            fx, fy, fc = (j >> 2) & 1, (j >> 1) & 1, j & 1
            peer = (x ^ fx, y ^ fy, c ^ fc)
            cps.append(pltpu.make_async_remote_copy(
                src_ref=v_ref, dst_ref=o_ref.at[pl.ds(me, 1), :], send_sem=send_sems.at[j - 1], recv_sem=recv_sems.at[j - 1],
                device_id=peer, device_id_type=MESH))
        for cp in cps:
            cp.start()
        for j in range(1, 8):
            fx, fy, fc = (j >> 2) & 1, (j >> 1) & 1, j & 1
            src = 4 * (x ^ fx) + 2 * (y ^ fy) + (c ^ fc)
            pltpu.make_async_remote_copy(
                src_ref=v_ref, dst_ref=o_ref.at[pl.ds(src, 1), :], send_sem=send_sems.at[j - 1], recv_sem=recv_sems.at[j - 1],
                device_id=(x, y, c), device_id_type=MESH).wait_recv()
        for cp in cps:
            cp.wait_send()

    return pl.pallas_call(
        body, name="gather_vec", in_specs=[ANY], out_specs=ANY, out_shape=jax.ShapeDtypeStruct((8, n), v.dtype),
        scratch_shapes=[pltpu.SemaphoreType.DMA((7,)), pltpu.SemaphoreType.DMA((7,))],
        compiler_params=pltpu.CompilerParams(has_side_effects=True),
    )(v)


def _pair_sum(g, r, name, tr):
    L, hr, C = r.shape
    both_halves = g.ndim == 4

    def body(c_ref, g_ref, r_ref, o_ref):
        def strip(rows):
            gv = g_ref[0, 0, rows, :] if both_halves else g_ref[0, rows, :]
            o_ref[0, rows, :] = (gv + r_ref[0, rows, :]).astype(bf16)
        _for_strips(tr, strip)

    g_spec = (pl.BlockSpec((1, 1, tr, C), lambda s, i, c_ref: (s, c_ref[0], i, 0)) if both_halves
              else pl.BlockSpec((1, tr, C), lambda s, i, c_ref: (s, i, 0)))
    grid_spec = pltpu.PrefetchScalarGridSpec(
        num_scalar_prefetch=1, grid=(L, hr // tr),
        in_specs=[g_spec, pl.BlockSpec((1, tr, C), lambda s, i, c_ref: (s, i, 0))],
        out_specs=pl.BlockSpec((1, tr, C), lambda s, i, c_ref: (s, i, 0)))
    c = lax.axis_index("c").reshape(1).astype(jnp.int32)
    return pl.pallas_call(body, name=name, grid_spec=grid_spec, out_shape=jax.ShapeDtypeStruct((L, hr, C), bf16),
                          compiler_params=_params(("parallel", "parallel")))(c, g, r)


def _own_sum(p, got, name, transposed=False):
    _, hr, C = p.shape
    tr = SUM_ROWS
    c_full, c_pad = C // 128 * 128, -(-C // 128) * 128

    def total(p_ref, got_ref, rows):
        return ((p_ref[0, rows, :].astype(f32) + got_ref[0, rows, :].astype(f32)) + got_ref[1, rows, :].astype(f32)) \
            + got_ref[2, rows, :].astype(f32)

    def body(idx_ref, p_ref, got_ref, o_ref):
        def strip(rows):
            o_ref[0, rows, :] = total(p_ref, got_ref, rows)
        _for_strips(tr, strip)

    def body_t(idx_ref, p_ref, got_ref, o_ref, buf):
        if c_pad > c_full:
            buf[:, pl.ds(c_full, c_pad - c_full)] = jnp.zeros((tr, c_pad - c_full), f32)

        def strip(rows):
            buf[rows, pl.ds(0, C)] = total(p_ref, got_ref, rows)
        _for_strips(tr, strip)
        o_ref[...] = buf[...].T[:C]

    in_specs = [pl.BlockSpec((1, tr, C), lambda i, idx: (idx[0], i, 0)), pl.BlockSpec((3, tr, C), lambda i, idx: (0, i, 0))]
    x, y, c = _me()
    idx = jnp.stack([2 * x + y, c]).astype(jnp.int32)
    if transposed:
        grid_spec = pltpu.PrefetchScalarGridSpec(num_scalar_prefetch=1, grid=(hr // tr,), in_specs=in_specs,
                                                 out_specs=pl.BlockSpec((C, tr), lambda i, idx: (0, idx[1] * (hr // tr) + i)),
                                                 scratch_shapes=[pltpu.VMEM((tr, c_pad), f32)])
        return pl.pallas_call(body_t, name=name, grid_spec=grid_spec, out_shape=jax.ShapeDtypeStruct((C, 2 * hr), f32),
                              compiler_params=_params(("parallel",)))(idx, p, got)
    grid_spec = pltpu.PrefetchScalarGridSpec(num_scalar_prefetch=1, grid=(hr // tr,), in_specs=in_specs,
                                             out_specs=pl.BlockSpec((1, tr, C), lambda i, idx: (idx[1], i, 0)))
    return pl.pallas_call(body, name=name, grid_spec=grid_spec, out_shape=jax.ShapeDtypeStruct((2, hr, C), f32),
                          compiler_params=_params(("parallel",)))(idx, p, got)


def _adamw(w, g, m, v, name, tr):
    _, R, C = w.shape
    rs = min(8, tr)

    def body(w_ref, g_ref, m_ref, v_ref, d_ref, nm_ref, nv_ref):
        def strip(s, carry):
            rows = pl.ds(pl.multiple_of(s * rs, rs), rs)
            gv = g_ref[0, rows, :]
            mn = ADAM_B1 * m_ref[0, rows, :] + (1.0 - ADAM_B1) * gv
            vn = ADAM_B2 * v_ref[0, rows, :] + (1.0 - ADAM_B2) * (gv * gv)
            m_hat = mn / (1.0 - ADAM_B1 ** ADAM_STEP)
            v_hat = vn / (1.0 - ADAM_B2 ** ADAM_STEP)
            d_ref[0, rows, :] = -ADAM_LR * (m_hat / (jnp.sqrt(v_hat) + ADAM_EPS) + ADAM_WD * w_ref[0, rows, :])
            nm_ref[0, rows, :] = mn
            nv_ref[0, rows, :] = vn
            return carry

        if R % tr == 0:
            lax.fori_loop(0, tr // rs, strip, 0, unroll=min(2, tr // rs))
        else:
            lax.fori_loop(0, jnp.minimum(tr, R - pl.program_id(0) * tr) // rs, strip, 0)

    blk, grid = pl.BlockSpec((1, tr, C), lambda i: (0, i, 0)), (-(-R // tr),)
    o = jax.ShapeDtypeStruct((1, R, C), f32)
    return pl.pallas_call(body, name=name, grid=grid, in_specs=[blk] * 4, out_specs=[blk] * 3, out_shape=[o, o, o],
                          compiler_params=_params(("parallel",)))(w, g, m, v)


def _sum8(t):
    n = t.shape[1]

    def body(t_ref, o_ref):
        acc = t_ref[pl.ds(0, 1), :]
        for r in range(1, 8):
            acc = acc + t_ref[pl.ds(r, 1), :]
        o_ref[...] = acc

    return pl.pallas_call(body, name="sum_devices", out_shape=jax.ShapeDtypeStruct((1, n), f32))(t)


def _reduce_swap_start(g_main, g_dt, g_b):
    hr = g_main.shape[0]
    return _sibling_swap_start(g_main, g_dt, g_b.reshape(N_SHARD, 2, hr, g_b.shape[-1]))


def _reduce_start(swap_state, g_main, g_dt):
    hr = g_main.shape[0]
    send_sems, recv_sems, srcs, lands, _ = swap_state
    srcs, (r_main, r_dt, r_b) = _sibling_swap_wait(send_sems, recv_sems, srcs, lands, g_dt)
    p_main = _pair_sum(g_main[None], r_main[None], "pair_sum_main", SUM_ROWS // 4)
    p_dt = _pair_sum(g_dt[None], r_dt[None], "pair_sum_dt", SUM_ROWS)
    p_b = _pair_sum(srcs[2], r_b, "pair_sum_b", SUM_ROWS)
    p_in = jnp.transpose(_w_in_grad_full(p_main[0], p_dt[0]).reshape(hr, N_SHARD, W_IN_SHARD), (1, 0, 2))
    return _chip_exchange_start([p_in, p_b])


def _reduce_finish(state, after):
    send_sems, recv_sems, srcs, lands, _ = state
    (p_in, p_b), (got_in, got_b) = _chip_exchange_wait(send_sems, recv_sems, srcs, lands, after)
    mine_in, mine_b = _own_sum(p_in, got_in, "own_sum_in", transposed=True), _own_sum(p_b, got_b, "own_sum_b")
    full_in_t, full_b = _share_halves([mine_in, mine_b], [True, False], "share_halves")
    return full_in_t, full_b.reshape(-1, full_b.shape[-1])


def kernel(x, positions, norm1_w, w_in, conv_w, conv_b, dt_bias, a_log, d_skip, ssd_norm_w, w_br_ret, w_br_ssd, w_out, norm_f_w, loss_target, m_norm1_w, m_w_in, m_conv_w, m_conv_b, m_dt_bias, m_a_log, m_d_skip, m_ssd_norm_w, m_w_br_ret, m_w_br_ssd, m_w_out, m_norm_f_w, v_norm1_w, v_w_in, v_conv_w, v_conv_b, v_dt_bias, v_a_log, v_d_skip, v_ssd_norm_w, v_w_br_ret, v_w_br_ssd, v_w_out, v_norm_f_w):
    D = D_MODEL
    xi, yi, ci = _me()
    k = 2 * xi + yi
    me = 2 * k + ci
    weights = dict(norm1_w=norm1_w, w_in=w_in, conv_w=conv_w, conv_b=conv_b, dt_bias=dt_bias, a_log=a_log, d_skip=d_skip,
                   ssd_norm_w=ssd_norm_w, w_br_ret=w_br_ret, w_br_ssd=w_br_ssd, w_out=w_out, norm_f_w=norm_f_w)
    mom1 = dict(norm1_w=m_norm1_w, w_in=m_w_in, conv_w=m_conv_w, conv_b=m_conv_b, dt_bias=m_dt_bias, a_log=m_a_log, d_skip=m_d_skip,
                ssd_norm_w=m_ssd_norm_w, w_br_ret=m_w_br_ret, w_br_ssd=m_w_br_ssd, w_out=m_w_out, norm_f_w=m_norm_f_w)
    mom2 = dict(norm1_w=v_norm1_w, w_in=v_w_in, conv_w=v_conv_w, conv_b=v_conv_b, dt_bias=v_dt_bias, a_log=v_a_log, d_skip=v_d_skip,
                ssd_norm_w=v_ssd_norm_w, w_br_ret=v_w_br_ret, w_br_ssd=v_w_br_ssd, w_out=v_w_out, norm_f_w=v_norm_f_w)

    a_sh = w_in[0].astype(bf16)
    b_sh = jnp.concatenate([w_br_ret[0], w_br_ssd[0], w_out[0]], axis=0).astype(bf16)
    own_send, own_recv, own_srcs, own_lands, own_token = _gather_own_start(a_sh, conv_w[0])
    own = lambda g, s: lax.dynamic_update_slice_in_dim(g, s[None], k, axis=0)
    late_state = []

    class Weights:
        token = own_token

        @staticmethod
        def main(after):
            ga, gc = _gather_own_wait(own_send, own_recv, own_srcs, own_lands, after)
            ga = _gather_pass_on(ga)
            ga, b_late = lax.optimization_barrier((ga, b_sh))
            late_state.extend(_gather_late_start(b_late))
            w_main, w_dt = _w_main_from_shards(own(ga, a_sh))
            conv_full = jnp.transpose(own(gc, conv_w[0]), (1, 0, 2)).reshape(SSD_CONV, CONV_DIM)
            return w_main, w_dt + late_state[4][0, 0].astype(bf16), conv_full

        @staticmethod
        def late(after):
            gb = own(_gather_late_wait(*late_state[:4], after), b_sh)
            return gb[:, 0:512].reshape(2048, D), gb[:, 512:1536].reshape(4096, D), gb[:, 1536:2048].reshape(2048, D)

    class Reducer:
        @staticmethod
        def halves():
            return (1 - ci).reshape(1).astype(jnp.int32), ci.reshape(1).astype(jnp.int32)

        @staticmethod
        def first(g_main, g_dt, g_w_br, g_w_bs, g_w_o):
            g_b = jnp.concatenate([g_w_br.reshape(N_SHARD, 512, D), g_w_bs.reshape(N_SHARD, 1024, D),
                                   g_w_o.reshape(N_SHARD, 512, D)], axis=1)
            return _reduce_swap_start(g_main, g_dt, g_b)

        second = staticmethod(_reduce_start)

    loss, grad_x, g, reduce_state = _local_step(x[0], positions[0], loss_target[0], norm1_w, Weights, conv_b, dt_bias, a_log, d_skip,
                                                ssd_norm_w, norm_f_w, Reducer)

    grad_w_in_t, full_b = _reduce_finish(reduce_state, g["norm1_w"])
    grad_mats = dict(w_br_ret=full_b[0:512], w_br_ssd=full_b[512:1536], w_out=full_b[1536:2048])

    small = [(n, weights[n].size) for n in ("norm1_w", "conv_b", "dt_bias", "a_log", "d_skip", "ssd_norm_w", "norm_f_w")]
    parts = [jnp.pad(loss.reshape(1, 1), ((0, 0), (0, 127)))] + [g[n].reshape(1, -1) for n, _ in small] + [g["conv_w"].reshape(1, -1)]
    vec = jnp.concatenate(parts, axis=1)
    nv = vec.shape[1]
    nvp = -(-nv // 128) * 128
    vec = jnp.pad(vec, ((0, 0), (0, nvp - nv)))
    total = _sum8(lax.dynamic_update_slice_in_dim(_gather_vec(vec), vec, me, axis=0))
    loss_out = total[0, 0]
    off = 128
    grad_small = {}
    for n, sz in small:
        grad_small[n] = total[:, off:off + sz]
        off += sz
    g_conv = total[:, off:off + SSD_CONV * CONV_DIM].reshape(SSD_CONV, CONV_DIM)
    g_conv = lax.dynamic_slice_in_dim(g_conv, k * (CONV_DIM // N_SHARD), CONV_DIM // N_SHARD, axis=1)
    grad_small["conv_w"] = g_conv.reshape(1, -1)

    upd = {}
    for n in ("w_br_ret", "w_br_ssd", "w_out"):
        upd[n] = _adamw(weights[n], grad_mats[n][None], mom1[n], mom2[n], "adamw_" + n, tr=SUM_ROWS)
    tp = lambda t: jnp.swapaxes(t, 1, 2)
    upd["w_in"] = tuple(tp(t) for t in _adamw(tp(w_in), grad_w_in_t[None], tp(m_w_in), tp(v_w_in), "adamw_w_in", tr=256))
    grad_mats["w_in"] = tp(grad_w_in_t[None])
    names_small = [n for n, _ in small] + ["conv_w"]
    flat = lambda d: jnp.concatenate([d[n].reshape(1, -1) for n in names_small], axis=1)
    ns = sum(weights[n].size for n in names_small)
    nsp = -(-ns // 128) * 128
    padv = lambda t: jnp.pad(t, ((0, 0), (0, nsp - ns)))
    small_upd = _adamw(padv(flat(weights))[None], padv(flat(grad_small))[None], padv(flat(mom1))[None],
                       jnp.pad(flat(mom2), ((0, 0), (0, nsp - ns)), constant_values=1.0)[None], "adamw_small", 1)
    off = 0
    for n in names_small:
        sz = weights[n].size
        upd[n] = tuple(t[0, :, off:off + sz] for t in small_upd)
        off += sz

    order = ["norm1_w", "w_in", "conv_w", "conv_b", "dt_bias", "a_log", "d_skip", "ssd_norm_w", "w_br_ret", "w_br_ssd", "w_out", "norm_f_w"]
    grads_out = {**grad_mats, **grad_small}
    shp = lambda n, t: t.reshape(weights[n].shape)
    return (loss_out, grad_x[None], *[shp(n, grads_out[n]) for n in order], *[shp(n, upd[n][0]) for n in order],
            *[shp(n, upd[n][1]) for n in order], *[shp(n, upd[n][2]) for n in order])
```

```python
import jax
import jax.numpy as jnp
import numpy as np
from jax import lax
from jax.experimental import pallas as pl
from jax.experimental.pallas import tpu as pltpu

f32 = jnp.float32
bf16 = jnp.bfloat16
HIGHEST = lax.Precision.HIGHEST
MESH = pl.DeviceIdType.MESH

D_MODEL = 2048
EPS = 1e-6
CHUNK = 64
RET_HEADS = 8
RET_DK = 256
RET_HW = 4 * RET_DK
RET_HP = 2
RET_UNROLL_FWD, RET_UNROLL_BWD = 4, 8
ROPE_THETA = 10000.0
SSD_WIDTH = 4096
SSD_GROUPS = 8
SSD_STATE = 128
SSD_GW = 512
SSD_GC = SSD_GW + 2 * SSD_STATE
SSD_HPG = 8
SSD_CONV = 4
CONV_DIM = 6144
SSD_HEADS = 64
LS = 128

C_RET, C_Z, C_GATES, C_XBC = 0, 8192, 12288, 16384
N_MAIN = 22528
DT_OFF = 18432
IN_PROJ = 22592
N_SHARD = 4
W_IN_SHARD = IN_PROJ // N_SHARD

ADAM_LR, ADAM_B1, ADAM_B2, ADAM_EPS, ADAM_WD, ADAM_STEP = 0.001, 0.9, 0.999, 1e-08, 0.01, 10

VMEM_LIMIT = 56 * 1024 * 1024
SUM_ROWS = 128
ANY = pl.BlockSpec(memory_space=pl.ANY)


def _params(dims):
    return pltpu.CompilerParams(dimension_semantics=dims, vmem_limit_bytes=VMEM_LIMIT)


def _silu(x):
    return x * jax.nn.sigmoid(x)


def _nt(a, b):
    return lax.dot_general(a, b, (((1,), (1,)), ((), ())), preferred_element_type=f32)


def _tn(a, b):
    return lax.dot_general(a, b, (((0,), (0,)), ((), ())), preferred_element_type=f32)


def _nn(a, b):
    return jnp.dot(a, b, preferred_element_type=f32)


def _hi(a, b):
    return jnp.dot(a, b, precision=HIGHEST, preferred_element_type=f32)


def _split(a):
    hi = a.astype(bf16)
    return hi, (a - hi.astype(f32)).astype(bf16)


def _sel_r(a, sel):
    hi, lo = _split(a)
    return _nn(hi, sel) + _nn(lo, sel)


def _sel_l(sel, a):
    hi, lo = _split(a)
    return _nn(sel, hi) + _nn(sel, lo)


def _rows_to_cols(t, eye):
    hi = t.astype(bf16)
    r1 = t - hi.astype(f32)
    mid = r1.astype(bf16)
    lo = (r1 - mid.astype(f32)).astype(bf16)
    return _nt(eye, hi) + _nt(eye, mid) + _nt(eye, lo)


def _xbc_group_major(t):
    R = t.shape[0]
    nb = SSD_GROUPS * SSD_STATE
    parts = [t[:, :SSD_WIDTH].reshape(R, SSD_GROUPS, SSD_GW), t[:, SSD_WIDTH:SSD_WIDTH + nb].reshape(R, SSD_GROUPS, SSD_STATE),
             t[:, SSD_WIDTH + nb:].reshape(R, SSD_GROUPS, SSD_STATE)]
    return jnp.concatenate(parts, axis=2).reshape(R, CONV_DIM)


def _xbc_original(t):
    R = t.shape[0]
    g = t.reshape(R, SSD_GROUPS, SSD_GC)
    parts = [g[:, :, :SSD_GW].reshape(R, SSD_WIDTH), g[:, :, SSD_GW:SSD_GW + SSD_STATE].reshape(R, SSD_GROUPS * SSD_STATE),
             g[:, :, SSD_GW + SSD_STATE:].reshape(R, SSD_GROUPS * SSD_STATE)]
    return jnp.concatenate(parts, axis=1)


def _main_segments():
    segs = []
    for h in range(RET_HEADS):
        segs += [(base + RET_DK * h, RET_DK) for base in (0, 2048, 4096, 6144)]
    segs += [(8192, SSD_WIDTH), (DT_OFF + SSD_HEADS, 2 * D_MODEL)]
    nb = SSD_GROUPS * SSD_STATE
    for g in range(SSD_GROUPS):
        segs += [(12288 + SSD_GW * g, SSD_GW), (12288 + SSD_WIDTH + SSD_STATE * g, SSD_STATE),
                 (12288 + SSD_WIDTH + nb + SSD_STATE * g, SSD_STATE)]
    return segs


def _w_main_from_shards(shards):
    def cols(lo, hi):
        out = []
        while lo < hi:
            s = lo // W_IN_SHARD
            top = min(hi, (s + 1) * W_IN_SHARD)
            out.append(shards[s][:, lo - s * W_IN_SHARD:top - s * W_IN_SHARD])
            lo = top
        return out

    main = jnp.concatenate([p for s, n in _main_segments() for p in cols(s, s + n)], axis=1)
    w_dt = jnp.pad(jnp.concatenate(cols(DT_OFF, DT_OFF + SSD_HEADS), axis=1), ((0, 0), (0, 128 - SSD_HEADS)))
    return main, w_dt


def _w_in_grad_full(g_main, g_dt):
    D = g_main.shape[0]
    ret = jnp.transpose(g_main[:, :C_Z].reshape(D, RET_HEADS, 4, RET_DK), (0, 2, 1, 3)).reshape(D, C_Z)
    return jnp.concatenate([ret, g_main[:, C_Z:C_GATES], _xbc_original(g_main[:, C_XBC:]), g_dt[:, :SSD_HEADS],
                            g_main[:, C_GATES:C_XBC]], axis=1)


def _mm(pairs, M, N, *, tm, tn, out_dtype, name, tb=False, row_off=None):
    P = len(pairs)
    nks = [K // tk for (_, _, _, _, _, K, tk) in pairs]
    starts = [int(s) for s in np.cumsum([0] + nks[:-1])]
    KT = int(sum(nks))
    npf = 0 if row_off is None else 1
    in_specs, args = [], []
    for (a, a_cb, b, b_kb, b_nb, K, tk), s, nk in zip(pairs, starts, nks):
        def kk(k, s=s, nk=nk):
            return jnp.clip(k - s, 0, nk - 1)
        in_specs.append(pl.BlockSpec((tm, tk), lambda m, n, k, *pf, kk=kk, a_cb=a_cb: (m + (pf[0][0] if pf else 0), a_cb + kk(k))))
        if tb:
            in_specs.append(pl.BlockSpec((tn, tk), lambda m, n, k, *pf, kk=kk, b_kb=b_kb, b_nb=b_nb: (b_nb + n, b_kb + kk(k))))
        else:
            in_specs.append(pl.BlockSpec((tk, tn), lambda m, n, k, *pf, kk=kk, b_kb=b_kb, b_nb=b_nb: (b_kb + kk(k), b_nb + n)))
        args += [a, b]

    def body(*refs):
        refs = refs[npf:]
        o_ref = refs[2 * P]
        k = pl.program_id(2)

        def prod(i):
            a = refs[2 * i][...].astype(bf16)
            b = refs[2 * i + 1][...].astype(bf16)
            return _nt(a, b) if tb else _nn(a, b)

        if KT == 1:
            o_ref[...] = prod(0).astype(out_dtype)
            return
        acc = refs[2 * P + 1]

        @pl.when(k == 0)
        def _():
            acc[...] = jnp.zeros_like(acc)

        for i in range(P):
            @pl.when((k >= starts[i]) & (k < starts[i] + nks[i]))
            def _(i=i):
                acc[...] += prod(i)

        @pl.when(k == KT - 1)
        def _():
            o_ref[...] = acc[...].astype(out_dtype)

    grid_spec = pltpu.PrefetchScalarGridSpec(
        num_scalar_prefetch=npf, grid=(M // tm, N // tn, KT), in_specs=in_specs,
        out_specs=pl.BlockSpec((tm, tn), lambda m, n, k, *pf: (m, n)),
        scratch_shapes=[] if KT == 1 else [pltpu.VMEM((tm, tn), f32)])
    return pl.pallas_call(
        body, name=name, grid_spec=grid_spec, out_shape=jax.ShapeDtypeStruct((M, N), out_dtype),
        compiler_params=_params(("parallel", "parallel", "arbitrary")),
    )(*([] if row_off is None else [row_off]), *args)


def _mm1(a, b, *, tm, tn, tk, out_dtype, name, tb=False):
    M, K = a.shape
    N = b.shape[0] if tb else b.shape[1]
    return _mm([(a, 0, b, 0, 0, K, tk)], M, N, tm=tm, tn=tn, out_dtype=out_dtype, name=name, tb=tb)


RS = 16
CS = 32


def _for_strips(n_rows, fn, rs=RS, unroll=4):
    def step(s, carry):
        fn(pl.ds(pl.multiple_of(s * rs, rs), rs))
        return carry
    n = n_rows // rs
    lax.fori_loop(0, n, step, 0, unroll=min(unroll, n))


def _norm1_fwd(x, w, tr):
    S, D = x.shape

    def body(x_ref, w_ref, h_ref, ht_ref):
        def strip(rows):
            xv = x_ref[rows, :]
            r = lax.rsqrt(jnp.mean(xv * xv, axis=-1, keepdims=True) + EPS)
            h_ref[rows, :] = (xv * r * w_ref[...]).astype(bf16)
        _for_strips(tr, strip)
        ht_ref[...] = h_ref[...].T

    return pl.pallas_call(
        body, name="norm1_fwd", grid=(S // tr,),
        in_specs=[pl.BlockSpec((tr, D), lambda i: (i, 0)), pl.BlockSpec((1, D), lambda i: (0, 0))],
        out_specs=[pl.BlockSpec((tr, D), lambda i: (i, 0)), pl.BlockSpec((D, tr), lambda i: (0, i))],
        out_shape=[jax.ShapeDtypeStruct((S, D), bf16), jax.ShapeDtypeStruct((D, S), bf16)], compiler_params=_params(("parallel",)),
    )(x, w)


def _norm1_bwd(x, w, dh, dx2, tr):
    S, D = x.shape

    def body(x_ref, w_ref, dh_ref, dx2_ref, gx_ref, gw_ref, acc):
        @pl.when(pl.program_id(0) == 0)
        def _():
            acc[...] = jnp.zeros_like(acc)

        def strip(rows):
            xv = x_ref[rows, :]
            r = lax.rsqrt(jnp.mean(xv * xv, axis=-1, keepdims=True) + EPS)
            xh = xv * r
            dhv = dh_ref[rows, :]
            acc[...] += dhv * xh
            dxh = dhv * w_ref[...]
            gx_ref[rows, :] = dx2_ref[rows, :] + r * (dxh - xh * jnp.mean(dxh * xh, axis=-1, keepdims=True))
        _for_strips(tr, strip)

        @pl.when(pl.program_id(0) == S // tr - 1)
        def _():
            gw_ref[...] = jnp.sum(acc[...], axis=0, keepdims=True)

    row = pl.BlockSpec((tr, D), lambda i: (i, 0))
    vec = pl.BlockSpec((1, D), lambda i: (0, 0))
    return pl.pallas_call(
        body, name="norm1_bwd", grid=(S // tr,), in_specs=[row, vec, row, row], out_specs=[row, vec],
        out_shape=[jax.ShapeDtypeStruct((S, D), f32), jax.ShapeDtypeStruct((1, D), f32)],
        scratch_shapes=[pltpu.VMEM((RS, D), f32)], compiler_params=_params(("arbitrary",)),
    )(x, w, dh, dx2)


def _final_fwd_bwd(x, mo, target, wf, tr):
    S, D = x.shape

    def body(x_ref, mo_ref, t_ref, w_ref, dx2_ref, dx2b_ref, loss_ref, gw_ref, acc, lacc):
        @pl.when(pl.program_id(0) == 0)
        def _():
            acc[...] = jnp.zeros_like(acc)
            lacc[...] = jnp.zeros_like(lacc)

        def strip(rows):
            x2 = x_ref[rows, :] + mo_ref[rows, :]
            r = lax.rsqrt(jnp.mean(x2 * x2, axis=-1, keepdims=True) + EPS)
            xh = x2 * r
            wv = w_ref[...]
            err = xh * wv - t_ref[rows, :]
            lacc[...] += jnp.mean(err * err, axis=-1, keepdims=True)
            dy = err * (1.0 / D)
            acc[...] += dy * xh
            dxh = dy * wv
            dx2 = r * (dxh - xh * jnp.mean(dxh * xh, axis=-1, keepdims=True))
            dx2_ref[rows, :] = dx2
            dx2b_ref[rows, :] = dx2.astype(bf16)
        _for_strips(tr, strip)

        @pl.when(pl.program_id(0) == S // tr - 1)
        def _():
            gw_ref[...] = jnp.sum(acc[...], axis=0, keepdims=True)
            loss_ref[...] = 0.5 * jnp.sum(lacc[...], axis=0, keepdims=True)

    row = pl.BlockSpec((tr, D), lambda i: (i, 0))
    vec = pl.BlockSpec((1, D), lambda i: (0, 0))
    return pl.pallas_call(
        body, name="final_norm_loss", grid=(S // tr,), in_specs=[row, row, row, vec],
        out_specs=[row, row, pl.BlockSpec((1, 1), lambda i: (0, 0)), vec],
        out_shape=[jax.ShapeDtypeStruct((S, D), f32), jax.ShapeDtypeStruct((S, D), bf16), jax.ShapeDtypeStruct((1, 1), f32),
                   jax.ShapeDtypeStruct((1, D), f32)],
        scratch_shapes=[pltpu.VMEM((RS, D), f32), pltpu.VMEM((RS, 1), f32)], compiler_params=_params(("arbitrary",)),
    )(x, mo, target, wf)


def _merge_fwd(p_r, p_s, proj, tr):
    S, D = p_r.shape

    def body(pr_ref, ps_ref, g_ref, o_ref, ot_ref):
        def strip(rows):
            gr, gs = g_ref[rows, pl.ds(0, D)].astype(f32), g_ref[rows, pl.ds(D, D)].astype(f32)
            o_ref[rows, :] = (jax.nn.sigmoid(gr) * pr_ref[rows, :] + jax.nn.sigmoid(gs) * ps_ref[rows, :]).astype(bf16)
        _for_strips(tr, strip)
        ot_ref[...] = o_ref[...].T

    row = pl.BlockSpec((tr, D), lambda i: (i, 0))
    return pl.pallas_call(
        body, name="merge_fwd", grid=(S // tr,),
        in_specs=[row, row, pl.BlockSpec((tr, 2 * D), lambda i: (i, C_GATES // (2 * D)))],
        out_specs=[row, pl.BlockSpec((D, tr), lambda i: (0, i))],
        out_shape=[jax.ShapeDtypeStruct((S, D), bf16), jax.ShapeDtypeStruct((D, S), bf16)], compiler_params=_params(("parallel",)),
    )(p_r, p_s, proj)


def _merge_bwd(dm, p_r, p_s, proj, tr):
    S, D = p_r.shape

    def body(dm_ref, pr_ref, ps_ref, g_ref, dpr_ref, dps_ref, dproj_ref):
        def strip(rows):
            dmv = dm_ref[rows, :]
            sr = jax.nn.sigmoid(g_ref[rows, pl.ds(0, D)].astype(f32))
            ss = jax.nn.sigmoid(g_ref[rows, pl.ds(D, D)].astype(f32))
            dpr_ref[rows, :] = (dmv * sr).astype(bf16)
            dps_ref[rows, :] = (dmv * ss).astype(bf16)
            dproj_ref[rows, pl.ds(0, D)] = (dmv * pr_ref[rows, :] * sr * (1.0 - sr)).astype(bf16)
            dproj_ref[rows, pl.ds(D, D)] = (dmv * ps_ref[rows, :] * ss * (1.0 - ss)).astype(bf16)
        _for_strips(tr, strip)

    row = pl.BlockSpec((tr, D), lambda i: (i, 0))
    gates = pl.BlockSpec((tr, 2 * D), lambda i: (i, C_GATES // (2 * D)))
    o = jax.ShapeDtypeStruct((S, D), bf16)
    return pl.pallas_call(
        body, name="merge_bwd", grid=(S // tr,), in_specs=[row, row, row, gates],
        out_specs=[row, row, gates], out_shape=[o, o, jax.ShapeDtypeStruct((S, N_MAIN), bf16)],
        compiler_params=_params(("parallel",)),
    )(dm, p_r, p_s, proj)


def _ssd_norm_fwd(y, proj, w, tr):
    S, W = y.shape

    def body(y_ref, z_ref, w_ref, o_ref, ot_ref):
        def strip(rows):
            u = y_ref[rows, :] * _silu(z_ref[rows, :].astype(f32))
            r = lax.rsqrt(jnp.mean(u * u, axis=-1, keepdims=True) + EPS)
            o_ref[rows, :] = (u * r * w_ref[...]).astype(bf16)
        _for_strips(tr, strip)
        ot_ref[...] = o_ref[...].T

    row = pl.BlockSpec((tr, W), lambda i: (i, 0))
    return pl.pallas_call(
        body, name="ssd_norm_fwd", grid=(S // tr,),
        in_specs=[row, pl.BlockSpec((tr, W), lambda i: (i, C_Z // W)), pl.BlockSpec((1, W), lambda i: (0, 0))],
        out_specs=[row, pl.BlockSpec((W, tr), lambda i: (0, i))],
        out_shape=[jax.ShapeDtypeStruct((S, W), bf16), jax.ShapeDtypeStruct((W, S), bf16)], compiler_params=_params(("parallel",)),
    )(y, proj, w)


def _ssd_norm_bwd(y, proj, w, dys, dproj, tr):
    S, W = y.shape

    def body(y_ref, z_ref, w_ref, d_ref, _, dy_ref, dz_ref, gw_ref, acc):
        @pl.when(pl.program_id(0) == 0)
        def _():
            acc[...] = jnp.zeros_like(acc)

        def strip(rows):
            yv, zv, dv = y_ref[rows, :], z_ref[rows, :].astype(f32), d_ref[rows, :]
            sg = jax.nn.sigmoid(zv)
            sz = zv * sg
            u = yv * sz
            r = lax.rsqrt(jnp.mean(u * u, axis=-1, keepdims=True) + EPS)
            un = u * r
            acc[...] += dv * un
            dun = dv * w_ref[...]
            du = r * (dun - un * jnp.mean(dun * un, axis=-1, keepdims=True))
            dy_ref[rows, :] = (du * sz).astype(bf16)
            dz_ref[rows, :] = (du * yv * (sg * (1.0 + zv * (1.0 - sg)))).astype(bf16)
        _for_strips(tr, strip)

        @pl.when(pl.program_id(0) == S // tr - 1)
        def _():
            gw_ref[...] = jnp.sum(acc[...], axis=0, keepdims=True)

    row = pl.BlockSpec((tr, W), lambda i: (i, 0))
    zcol = pl.BlockSpec((tr, W), lambda i: (i, C_Z // W))
    vec = pl.BlockSpec((1, W), lambda i: (0, 0))
    return pl.pallas_call(
        body, name="ssd_norm_bwd", grid=(S // tr,),
        in_specs=[row, zcol, vec, row, ANY], out_specs=[row, zcol, vec],
        out_shape=[jax.ShapeDtypeStruct((S, W), bf16), jax.ShapeDtypeStruct(dproj.shape, bf16), jax.ShapeDtypeStruct((1, W), f32)],
        input_output_aliases={4: 1}, scratch_shapes=[pltpu.VMEM((RS, W), f32)], compiler_params=_params(("arbitrary",)),
    )(y, proj, w, dys, dproj)


def _rope(t, cos, sin):
    t1, t2 = t[:, :128], t[:, 128:]
    return jnp.concatenate([t1 * cos - t2 * sin, t2 * cos + t1 * sin], axis=1)


def _rope_t(d, cos, sin):
    d1, d2 = d[:, :128], d[:, 128:]
    return jnp.concatenate([d1 * cos + d2 * sin, d2 * cos - d1 * sin], axis=1)


def _ret_specs(tb, rev_nb=None):
    def blk(i):
        return i if rev_nb is None else rev_nb - 1 - i
    head = pl.BlockSpec((tb, RET_HP * RET_HW), lambda h, i: (blk(i), h))
    tab = pl.BlockSpec((tb, 128), lambda h, i: (blk(i), 0))
    mat = pl.BlockSpec((RET_HP, CHUNK, CHUNK), lambda h, i: (h, 0, 0))
    vec = pl.BlockSpec((RET_HP, CHUNK, 1), lambda h, i: (h, 0, 0))
    one = pl.BlockSpec((RET_HP, 1, 1), lambda h, i: (h, 0, 0))
    own = pl.BlockSpec((tb, RET_HP * RET_DK), lambda h, i: (blk(i), h))
    st = pl.BlockSpec((RET_HP, tb // CHUNK, RET_DK, RET_DK), lambda h, i: (h, blk(i), 0, 0))
    return head, tab, mat, vec, one, own, st


def _ret_fwd(proj, cos, sin, intra, qdec, kdec, cdec, tb):
    S = proj.shape[0]
    nc = S // CHUNK
    scale = RET_DK ** -0.5
    dk = RET_DK

    def body(p_ref, cos_ref, sin_ref, m_ref, qd_ref, kd_ref, cd_ref, y_ref, yr_ref, yrt_ref, st_ref, st):
        @pl.when(pl.program_id(1) == 0)
        def _():
            st[...] = jnp.zeros_like(st)

        def head_chunk(hh, c, rows, cs, sn):
            mm, qd, kd, cd = m_ref[hh], qd_ref[hh], kd_ref[hh], cd_ref[hh]
            col = lambda j: pl.ds(hh * RET_HW + j * dk, dk)
            own = pl.ds(hh * dk, dk)
            qr = _rope(p_ref[rows, col(0)].astype(f32), cs, sn)
            kr = _rope(p_ref[rows, col(1)].astype(f32), cs, sn) * scale
            qb, kb, vb = qr.astype(bf16), kr.astype(bf16), p_ref[rows, col(2)].astype(bf16)
            stb = st[hh].astype(bf16)
            st_ref[hh, c] = stb
            sc = (_nt(qb, kb) * mm).astype(bf16)
            y = _nn(sc, vb) + _nn(qb, stb) * qd
            st[hh] = st[hh] * cd + _tn((kr * kd).astype(bf16), vb)
            y_ref[rows, own] = y
            mu = jnp.mean(y, axis=-1, keepdims=True)
            yc = y - mu
            var = jnp.mean(yc * yc, axis=-1, keepdims=True)
            yr_ref[rows, own] = (yc * lax.rsqrt(var + EPS) * _silu(p_ref[rows, col(3)].astype(f32))).astype(bf16)

        def chunk(c, carry):
            rows = pl.ds(pl.multiple_of(c * CHUNK, CHUNK), CHUNK)
            cs, sn = cos_ref[rows, :], sin_ref[rows, :]
            for hh in range(RET_HP):
                head_chunk(hh, c, rows, cs, sn)
            return carry

        lax.fori_loop(0, tb // CHUNK, chunk, 0, unroll=min(RET_UNROLL_FWD, tb // CHUNK))
        yrt_ref[...] = yr_ref[...].T

    head, tab, mat, vec, one, own, stspec = _ret_specs(tb)
    return pl.pallas_call(
        body, name="ret_fwd", grid=(RET_HEADS // RET_HP, S // tb),
        in_specs=[head, tab, tab, mat, vec, vec, one],
        out_specs=[own, own, pl.BlockSpec((RET_HP * RET_DK, tb), lambda h, i: (h, i)), stspec],
        out_shape=[jax.ShapeDtypeStruct((S, 2048), f32), jax.ShapeDtypeStruct((S, 2048), bf16), jax.ShapeDtypeStruct((2048, S), bf16),
                   jax.ShapeDtypeStruct((RET_HEADS, nc, dk, dk), bf16)],
        scratch_shapes=[pltpu.VMEM((RET_HP, dk, dk), f32)], compiler_params=_params(("parallel", "arbitrary")),
    )(proj, cos, sin, intra, qdec, kdec, cdec)


def _ret_bwd(proj, cos, sin, intra, qdec, kdec, cdec, y, dyr, states, dproj, tb):
    S = proj.shape[0]
    nb = S // tb
    nck = tb // CHUNK
    scale = RET_DK ** -0.5
    dk = RET_DK

    def body(p_ref, cos_ref, sin_ref, m_ref, qd_ref, kd_ref, cd_ref, y_ref, dyr_ref, st_ref, _, o_ref, dst):
        @pl.when(pl.program_id(1) == 0)
        def _():
            dst[...] = jnp.zeros_like(dst)

        def head_chunk(hh, c, rows, cs, sn):
            mm, qd, kd, cd = m_ref[hh], qd_ref[hh], kd_ref[hh], cd_ref[hh]
            col = lambda j: pl.ds(hh * RET_HW + j * dk, dk)
            own = pl.ds(hh * dk, dk)
            qr = _rope(p_ref[rows, col(0)].astype(f32), cs, sn)
            kr = _rope(p_ref[rows, col(1)].astype(f32), cs, sn) * scale
            qb, kb, vb = qr.astype(bf16), kr.astype(bf16), p_ref[rows, col(2)].astype(bf16)
            kdb = (kr * kd).astype(bf16)
            stb = st_ref[hh, c]
            yv, gv, dyrv = y_ref[rows, own], p_ref[rows, col(3)].astype(f32), dyr_ref[rows, own]
            mu = jnp.mean(yv, axis=-1, keepdims=True)
            yc = yv - mu
            rstd = lax.rsqrt(jnp.mean(yc * yc, axis=-1, keepdims=True) + EPS)
            yn = yc * rstd
            sg = jax.nn.sigmoid(gv)
            o_ref[rows, col(3)] = (dyrv * yn * (sg * (1.0 + gv * (1.0 - sg)))).astype(bf16)
            dyn = dyrv * (gv * sg)
            dy = rstd * (dyn - jnp.mean(dyn, axis=-1, keepdims=True) - yn * jnp.mean(dyn * yn, axis=-1, keepdims=True))
            dyb = dy.astype(bf16)
            dyqb = (dy * qd).astype(bf16)
            dstb = dst[hh].astype(bf16)
            sct =(_nt(kb, qb) * mm).astype(bf16)
            ds = (_nt(dyb, vb) * mm).astype(bf16)
            dsT = (_nt(vb, dyb) * mm).astype(bf16)
            dv = _nn(sct, dyb) + _nn(kdb, dstb)
            dqr = _nn(ds, kb) + _nt(dyqb, stb)
            dkr = _nn(dsT, qb) + _nt(vb, dstb) * kd
            dst[hh] = dst[hh] * cd + _tn(qb, dyqb)
            o_ref[rows, col(0)] = _rope_t(dqr, cs, sn).astype(bf16)
            o_ref[rows, col(1)] = (_rope_t(dkr, cs, sn) * scale).astype(bf16)
            o_ref[rows, col(2)] = dv.astype(bf16)

        def chunk(cc, carry):
            c = nck - 1 - cc
            rows = pl.ds(pl.multiple_of(c * CHUNK, CHUNK), CHUNK)
            cs, sn = cos_ref[rows, :], sin_ref[rows, :]
            for hh in range(RET_HP):
                head_chunk(hh, c, rows, cs, sn)
            return carry

        lax.fori_loop(0, nck, chunk, 0, unroll=min(RET_UNROLL_BWD, nck))

    head, tab, mat, vec, one, own, stspec = _ret_specs(tb, rev_nb=nb)
    return pl.pallas_call(
        body, name="ret_bwd", grid=(RET_HEADS // RET_HP, nb),
        in_specs=[head, tab, tab, mat, vec, vec, one, own, own, stspec, ANY],
        out_specs=head, out_shape=jax.ShapeDtypeStruct(dproj.shape, bf16), input_output_aliases={10: 0},
        scratch_shapes=[pltpu.VMEM((RET_HP, dk, dk), f32)], compiler_params=_params(("parallel", "arbitrary")),
    )(proj, cos, sin, intra, qdec, kdec, cdec, y, dyr, states, dproj)


def _conv_fwd(proj, conv_w, conv_b, tb, cw):
    S = proj.shape[0]
    off = 0

    def body(x_ref, halo_ref, w_ref, b_ref, o_ref, xe):
        xe[pl.ds(0, 8), :] = jnp.where(pl.program_id(1) == 0, 0.0, halo_ref[...])
        xe[pl.ds(8, CS), :] = x_ref[pl.ds(0, CS), :]
        ws = [w_ref[pl.ds(j, 1), :] for j in range(SSD_CONV)]
        for s in range(tb // CS):
            tap = (lambda j: xe[pl.ds(5 + j, CS), :]) if s == 0 else (lambda j, s=s: x_ref[pl.ds(s * CS - 3 + j, CS), :])
            acc = b_ref[...] + ws[0] * tap(0)
            for j in range(1, SSD_CONV):
                acc = acc + ws[j] * tap(j)
            o_ref[pl.ds(s * CS, CS), :] = acc

    return pl.pallas_call(
        body, name="conv_fwd", grid=(CONV_DIM // cw, S // tb),
        in_specs=[pl.BlockSpec((tb, cw), lambda j, i: (i, off + j)),
                  pl.BlockSpec((8, cw), lambda j, i: (jnp.maximum(i * (tb // 8) - 1, 0), off + j)),
                  pl.BlockSpec((SSD_CONV, cw), lambda j, i: (0, j)), pl.BlockSpec((1, cw), lambda j, i: (0, j))],
        out_specs=pl.BlockSpec((tb, cw), lambda j, i: (i, j)),
        out_shape=jax.ShapeDtypeStruct((S, CONV_DIM), f32),
        scratch_shapes=[pltpu.VMEM((CS + 8, cw), f32)], compiler_params=_params(("parallel", "arbitrary")),
    )(proj, proj, conv_w, conv_b)


def _conv_bwd(dpre, proj, conv_w, dproj, tb, cw):
    S, n = dpre.shape
    nb = S // tb
    xoff = C_XBC // cw

    def body(d_ref, dh_ref, x_ref, xh_ref, w_ref, _, dx_ref, gw_ref, gb_ref, de, xe, accw, accb):
        i = pl.program_id(1)

        @pl.when(i == 0)
        def _():
            accw[...] = jnp.zeros_like(accw)
            accb[...] = jnp.zeros_like(accb)

        ns = tb // CS
        de[pl.ds(0, CS), :] = d_ref[pl.ds(tb - CS, CS), :]
        de[pl.ds(CS, 8), :] = jnp.where(i == nb - 1, 0.0, dh_ref[...])
        xe[pl.ds(0, 8), :] = jnp.where(i == 0, 0.0, xh_ref[...])
        xe[pl.ds(8, CS), :] = x_ref[pl.ds(0, CS), :]
        ws = [w_ref[pl.ds(j, 1), :] for j in range(SSD_CONV)]
        fold = lambda p: sum(p[8 * q:8 * (q + 1)] for q in range(1, CS // 8)) + p[0:8]
        for s in range(ns):
            dv = d_ref[pl.ds(s * CS, CS), :]
            ahead = (lambda o: de[pl.ds(o, CS), :]) if s == ns - 1 else (lambda o, s=s: d_ref[pl.ds(s * CS + o, CS), :])
            xtap = (lambda j: xe[pl.ds(5 + j, CS), :]) if s == 0 else (lambda j, s=s: x_ref[pl.ds(s * CS - 3 + j, CS), :])
            acc = ws[SSD_CONV - 1] * dv
            for j in range(SSD_CONV - 1):
                acc = acc + ws[j] * ahead(3 - j)
            dx_ref[pl.ds(s * CS, CS), :] = acc.astype(bf16)
            accb[...] += fold(dv)
            for j in range(SSD_CONV):
                accw[j] += fold(dv * xtap(j))

        @pl.when(i == nb - 1)
        def _():
            gb_ref[...] = jnp.sum(accb[...], axis=0, keepdims=True)
            for j in range(SSD_CONV):
                gw_ref[pl.ds(j, 1), :] = jnp.sum(accw[j], axis=0, keepdims=True)

    return pl.pallas_call(
        body, name="conv_bwd", grid=(n // cw, nb),
        in_specs=[pl.BlockSpec((tb, cw), lambda j, i: (i, j)),
                  pl.BlockSpec((8, cw), lambda j, i: (jnp.minimum((i + 1) * (tb // 8), S // 8 - 1), j)),
                  pl.BlockSpec((tb, cw), lambda j, i: (i, j)),
                  pl.BlockSpec((8, cw), lambda j, i: (jnp.maximum(i * (tb // 8) - 1, 0), j)),
                  pl.BlockSpec((SSD_CONV, cw), lambda j, i: (0, j)), ANY],
        out_specs=[pl.BlockSpec((tb, cw), lambda j, i: (i, xoff + j)), pl.BlockSpec((SSD_CONV, cw), lambda j, i: (0, j)),
                   pl.BlockSpec((1, cw), lambda j, i: (0, j))],
        out_shape=[jax.ShapeDtypeStruct(dproj.shape, bf16), jax.ShapeDtypeStruct((SSD_CONV, n), f32), jax.ShapeDtypeStruct((1, n), f32)],
        input_output_aliases={5: 0},
        scratch_shapes=[pltpu.VMEM((CS + 8, cw), f32), pltpu.VMEM((CS + 8, cw), f32), pltpu.VMEM((SSD_CONV, 8, cw), f32),
                        pltpu.VMEM((8, cw), f32)],
        compiler_params=_params(("parallel", "arbitrary")),
    )(dpre, dpre, proj, proj, conv_w, dproj)


def _dt_prep(dt_raw, dt_bias, a_log, tb):
    S = dt_raw.shape[0]

    def body(r_ref, b_ref, al_ref, dt_ref, sg_ref, ac_ref):
        li = lax.broadcasted_iota(jnp.int32, (LS, LS), 0)
        si = lax.broadcasted_iota(jnp.int32, (LS, LS), 1)
        tri = (li >= si).astype(f32)
        neg_a = -jnp.exp(al_ref[...])
        for c in range(tb // LS):
            rows = pl.ds(c * LS, LS)
            xv = r_ref[rows, :] + b_ref[...]
            dtv = jax.nn.softplus(xv)
            dt_ref[rows, :] = dtv
            sg_ref[rows, :] = jax.nn.sigmoid(xv)
            ac_ref[rows, :] = _hi(tri, dtv * neg_a)

    row = pl.BlockSpec((tb, 128), lambda i: (i, 0))
    vec = pl.BlockSpec((1, 128), lambda i: (0, 0))
    o = jax.ShapeDtypeStruct((S, 128), f32)
    return pl.pallas_call(body, name="dt_prep", grid=(S // tb,), in_specs=[row, vec, vec], out_specs=[row, row, row],
                          out_shape=[o, o, o], compiler_params=_params(("parallel",)))(dt_raw, dt_bias, a_log)


def _group_major(t):
    S = t.shape[0]
    return jnp.transpose(t[:, :SSD_HEADS].reshape(S, SSD_GROUPS, SSD_HPG), (1, 0, 2))


def _group_major_t(t):
    S = t.shape[0]
    return jnp.transpose(t[:, :SSD_HEADS].reshape(S // LS, LS, SSD_GROUPS, SSD_HPG), (2, 0, 3, 1))


def _ssd_specs(tb, rev_nb=None):
    def blk(i):
        return i if rev_nb is None else rev_nb - 1 - i
    grp = pl.BlockSpec((tb, SSD_GC), lambda g, i: (blk(i), g))
    xs = pl.BlockSpec((tb, SSD_GW), lambda g, i: (blk(i), g))
    ph = pl.BlockSpec((1, tb, SSD_HPG), lambda g, i: (g, blk(i), 0))
    pht = pl.BlockSpec((1, tb // LS, SSD_HPG, LS), lambda g, i: (g, blk(i), 0, 0))
    gvec = pl.BlockSpec((1, 1, SSD_GW), lambda g, i: (g, 0, 0))
    ex = pl.BlockSpec((SSD_HPG, SSD_GW), lambda g, i: (0, 0))
    st = pl.BlockSpec((1, tb // LS, SSD_STATE, SSD_GW), lambda g, i: (g, blk(i), 0, 0))
    return grp, xs, ph, pht, gvec, ex, st


def _expander():
    return jnp.repeat(jnp.eye(SSD_HPG, dtype=f32), SSD_GW // SSD_HPG, axis=1).astype(bf16)


def _expand3(dt8, ac8, ex):
    stack = jnp.concatenate([dt8, jnp.exp(ac8), jnp.exp(ac8[LS - 1:LS, :] - ac8)], axis=0)
    wide = _sel_r(stack, ex)
    return wide[0:LS], wide[LS:2 * LS], wide[2 * LS:3 * LS]


def _ssd_fwd(pre, dt_g, ac_g, act_g, dskx, tb):
    S = pre.shape[0]
    nc = S // LS
    hd = SSD_GW // SSD_HPG

    def body(p_ref, dt_ref, ac_ref, act_ref, dsk_ref, ex_ref, y_ref, st_ref, st):
        @pl.when(pl.program_id(1) == 0)
        def _():
            st[...] = jnp.zeros_like(st)

        ex = ex_ref[...]
        li = lax.broadcasted_iota(jnp.int32, (LS, LS), 0)
        si = lax.broadcasted_iota(jnp.int32, (LS, LS), 1)
        causal = li >= si

        def chunk(c, carry):
            rows = pl.ds(pl.multiple_of(c * LS, LS), LS)
            xs = _silu(p_ref[rows, pl.ds(0, SSD_GW)])
            bcb = _silu(p_ref[rows, pl.ds(SSD_GW, SSD_STATE)]).astype(bf16)
            ccb = _silu(p_ref[rows, pl.ds(SSD_GW + SSD_STATE, SSD_STATE)]).astype(bf16)
            dt8, ac8, act = dt_ref[0, rows, :], ac_ref[0, rows, :], act_ref[0, c]
            dtx, eax, tailx = _expand3(dt8, ac8, ex)
            xdt = xs * dtx
            cb = _nt(ccb, bcb)
            stb = st[...].astype(bf16)
            st_ref[0, c] = stb
            xdtb = xdt.astype(bf16)
            outs = []
            for h in range(SSD_HPG):
                dec = jnp.exp(jnp.where(causal, ac8[:, h:h + 1] - act[h:h + 1, :], -1e30))
                outs.append(_nn((cb * dec).astype(bf16), xdtb[:, hd * h:hd * (h + 1)]))
            y_ref[rows, :] = (jnp.concatenate(outs, axis=1) + _nn(ccb, stb) * eax + dsk_ref[0] * xs).astype(bf16)
            st[...] = st[...] * eax[LS - 1:LS, :] + _tn(bcb, (xdt * tailx).astype(bf16))
            return carry

        lax.fori_loop(0, tb // LS, chunk, 0, unroll=min(4, tb // LS))

    grp, xs, ph, pht, gvec, ex, stspec = _ssd_specs(tb)
    return pl.pallas_call(
        body, name="ssd_fwd", grid=(SSD_GROUPS, S // tb),
        in_specs=[grp, ph, ph, pht, gvec, ex], out_specs=[xs, stspec],
        out_shape=[jax.ShapeDtypeStruct((S, SSD_WIDTH), bf16), jax.ShapeDtypeStruct((SSD_GROUPS, nc, SSD_STATE, SSD_GW), bf16)],
        scratch_shapes=[pltpu.VMEM((SSD_STATE, SSD_GW), f32)], compiler_params=_params(("parallel", "arbitrary")),
    )(pre, dt_g, ac_g, act_g, dskx, _expander())


def _ssd_bwd(pre, dt_g, ac_g, act_g, sg_g, dskx, nega_g, dy, states, tb):
    S = pre.shape[0]
    nb = S // tb
    nck = tb // LS
    hd = SSD_GW // SSD_HPG

    def body(p_ref, dt_ref, ac_ref, act_ref, sg_ref, dsk_ref, na_ref, ex_ref, ext_ref, dy_ref, st_ref,
             dp_ref, ddt_ref, gsk_ref, gal_ref, gdb_ref, dst, skacc):
        @pl.when(pl.program_id(1) == 0)
        def _():
            dst[...] = jnp.zeros_like(dst)
            skacc[...] = jnp.zeros_like(skacc)
            gal_ref[...] = jnp.zeros_like(gal_ref)
            gdb_ref[...] = jnp.zeros_like(gdb_ref)

        ex, ext = ex_ref[...], ext_ref[...]
        li = lax.broadcasted_iota(jnp.int32, (LS, LS), 0)
        si = lax.broadcasted_iota(jnp.int32, (LS, LS), 1)
        causal = li >= si
        anti = si >= li
        upper = anti.astype(bf16)
        eye = (si == li).astype(bf16)
        last_row = (lax.broadcasted_iota(jnp.int32, (LS, 1), 0) == LS - 1).astype(f32)
        head_id = lax.broadcasted_iota(jnp.int32, (1, SSD_HPG), 1)
        head_col = lax.broadcasted_iota(jnp.int32, (SSD_HPG, 1), 0)
        neg_a = na_ref[0]
        dskv = dsk_ref[0]

        def chunk(cc, carry):
            c = nck - 1 - cc
            rows = pl.ds(pl.multiple_of(c * LS, LS), LS)
            px = p_ref[rows, pl.ds(0, SSD_GW)]
            pb = p_ref[rows, pl.ds(SSD_GW, SSD_STATE)]
            pc = p_ref[rows, pl.ds(SSD_GW + SSD_STATE, SSD_STATE)]
            sgx, sgb, sgc = jax.nn.sigmoid(px), jax.nn.sigmoid(pb), jax.nn.sigmoid(pc)
            xs = px * sgx
            bcb = (pb * sgb).astype(bf16)
            ccb = (pc * sgc).astype(bf16)
            dt8, ac8, act = dt_ref[0, rows, :], ac_ref[0, rows, :], act_ref[0, c]
            dtx, eax, tailx = _expand3(dt8, ac8, ex)
            xdt = xs * dtx
            ex_last = eax[LS - 1:LS, :]
            stb = st_ref[0, c]
            dyv = dy_ref[rows, :]
            dyb = dyv.astype(bf16)
            xdtb = xdt.astype(bf16)
            skacc[...] += jnp.sum(dyv * xs, axis=0, keepdims=True)
            yinter = _nn(ccb, stb) * eax
            dzb = (dyv * eax).astype(bf16)
            dcc = _nt(dzb, stb)
            dstv = dst[...]
            dstb = dstv.astype(bf16)
            xt = xdt * tailx
            dxt = _nn(bcb, dstb)
            dbc = _nt(xt.astype(bf16), dstb)
            dxdt = dxt * tailx
            lastrow = jnp.sum(dxt * xt, axis=0, keepdims=True) + jnp.sum(dstv * stb.astype(f32), axis=0, keepdims=True) * ex_last
            dst[...] = dstv * ex_last + _tn(ccb, dzb)
            cb = _nt(ccb, bcb)
            cbt = _nt(bcb, ccb)
            dcb = jnp.zeros((LS, LS), f32)
            dac8 = jnp.zeros((LS, SSD_HPG), f32)
            dact = jnp.zeros((SSD_HPG, LS), f32)
            dxin = []
            for h in range(SSD_HPG):
                sl = slice(hd * h, hd * (h + 1))
                col, rowv = ac8[:, h:h + 1], act[h:h + 1, :]
                dec = jnp.exp(jnp.where(causal, col - rowv, -1e30))
                dect = jnp.exp(jnp.where(anti, rowv - col, -1e30))
                gm = cb * dec
                dgm = _nt(dyb[:, sl], xdtb[:, sl])
                dxin.append(_nn((cbt * dect).astype(bf16), dyb[:, sl]))
                dcb = dcb + dgm * dec
                w = dgm * gm
                dac8 = dac8 + jnp.sum(w, axis=1, keepdims=True) * (head_id == h).astype(f32)
                dact = dact + (head_col == h).astype(f32) * jnp.sum(w, axis=0, keepdims=True)
            dxintra = jnp.concatenate(dxin, axis=1)
            dcbb = dcb.astype(bf16)
            dcc = dcc + _nn(dcbb, bcb)
            dbc = dbc + _tn(dcbb, ccb)
            dxdt = dxdt + dxintra
            dacx = dyv * yinter - dxt * xt + last_row * lastrow
            red = _sel_r(jnp.concatenate([dacx, dxdt * xs], axis=0), ext)
            dac8 = dac8 - _rows_to_cols(dact, eye) + red[0:LS]
            da8 = _sel_l(upper, dac8)
            ddt8 = red[LS:2 * LS] + da8 * neg_a
            gal_ref[0] += jnp.sum(da8 * dt8 * neg_a, axis=0, keepdims=True)
            ddr = ddt8 * sg_ref[0, rows, :]
            ddt_ref[0, rows, :] = ddr
            gdb_ref[0] += jnp.sum(ddr, axis=0, keepdims=True)
            dsilu = lambda p, s: s * (1.0 + p * (1.0 - s))
            dp_ref[rows, pl.ds(0, SSD_GW)] = (dskv * dyv + dxdt * dtx) * dsilu(px, sgx)
            dp_ref[rows, pl.ds(SSD_GW, SSD_STATE)] = dbc * dsilu(pb, sgb)
            dp_ref[rows, pl.ds(SSD_GW + SSD_STATE, SSD_STATE)] = dcc * dsilu(pc, sgc)
            return carry

        lax.fori_loop(0, nck, chunk, 0, unroll=min(4, nck))

        @pl.when(pl.program_id(1) == nb - 1)
        def _():
            gsk_ref[0] = skacc[...]

    grp, xs, ph, pht, gvec, ex, stspec = _ssd_specs(tb, rev_nb=nb)
    small = pl.BlockSpec((1, 1, SSD_HPG), lambda g, i: (g, 0, 0))
    ext = pl.BlockSpec((SSD_GW, SSD_HPG), lambda g, i: (0, 0))
    sm = jax.ShapeDtypeStruct((SSD_GROUPS, 1, SSD_HPG), f32)
    expander = _expander()
    return pl.pallas_call(
        body, name="ssd_bwd", grid=(SSD_GROUPS, nb),
        in_specs=[grp, ph, ph, pht, ph, gvec, small, ex, ext, xs, stspec],
        out_specs=[grp, ph, gvec, small, small],
        out_shape=[jax.ShapeDtypeStruct((S, CONV_DIM), f32), jax.ShapeDtypeStruct((SSD_GROUPS, S, SSD_HPG), f32),
                   jax.ShapeDtypeStruct((SSD_GROUPS, 1, SSD_GW), f32), sm, sm],
        scratch_shapes=[pltpu.VMEM((SSD_STATE, SSD_GW), f32), pltpu.VMEM((1, SSD_GW), f32)],
        compiler_params=_params(("parallel", "arbitrary")),
    )(pre, dt_g, ac_g, act_g, sg_g, dskx, nega_g, expander, expander.T, dy, states)


def _tiles(S):
    return dict(tb=min(512, S), tr=min(256, S), tm=min(1024, S))


def _local_step(x, positions, target, norm1_w, weights, conv_b, dt_bias, a_log, d_skip, ssd_norm_w, norm_f_w, reducer):
    S, D = x.shape
    t = _tiles(S)
    tb, tr, tm = t["tb"], t["tr"], t["tm"]

    half = RET_DK // 2
    inv_freq = ROPE_THETA ** (-jnp.arange(half, dtype=f32) / half)
    ang = positions.astype(f32)[:, None] * inv_freq
    cos, sin = jnp.cos(ang), jnp.sin(ang)
    log_gamma = jnp.log1p(-(2.0 ** (-5.0 - jnp.arange(RET_HEADS, dtype=f32))))
    idx = jnp.arange(CHUNK, dtype=f32)
    intra = jnp.exp(jnp.abs(idx[:, None] - idx[None, :]) * log_gamma[:, None, None])
    qdec = jnp.exp((idx + 1.0)[None, :] * log_gamma[:, None])[:, :, None]
    kdec = jnp.exp((CHUNK - 1.0 - idx)[None, :] * log_gamma[:, None])[:, :, None]
    cdec = jnp.exp(CHUNK * log_gamma)[:, None, None]

    h, ht = _norm1_fwd(x, norm1_w + weights.token[0, 0], tr)
    h_tied, cos, sin, intra = lax.optimization_barrier((h, cos, sin, intra))
    w_main, w_dt, conv_w = weights.main(h_tied)
    conv_wm, conv_bm = _xbc_group_major(conv_w), _xbc_group_major(conv_b)
    proj = _mm([(h, 0, w_main, 0, 0, D, D)], S, C_XBC, tm=tm, tn=1024, out_dtype=bf16, name="proj_main")
    proj_x = _mm([(h, 0, w_main, 0, C_XBC // 1024, D, D)], S, CONV_DIM, tm=tm, tn=1024, out_dtype=f32, name="proj_xbc")
    dt_raw = _mm1(h, w_dt, tm=tm, tn=128, tk=D, out_dtype=f32, name="proj_dt")
    y_ret, yr, yrt, ret_states = _ret_fwd(proj, cos, sin, intra, qdec, kdec, cdec, tb)
    pre = _conv_fwd(proj_x, conv_wm, conv_bm, min(1024, S), 512)
    pad64 = lambda v: jnp.pad(v, ((0, 0), (0, 128 - SSD_HEADS)))
    dt, sg, ac = _dt_prep(dt_raw, pad64(dt_bias), pad64(a_log), tb)
    dt_g, ac_g, sg_g, act_g = _group_major(dt), _group_major(ac), _group_major(sg), _group_major_t(ac)
    dskx = jnp.repeat(d_skip.reshape(SSD_GROUPS, 1, SSD_HPG), SSD_GW // SSD_HPG, axis=2)
    nega_g = (-jnp.exp(a_log)).reshape(SSD_GROUPS, 1, SSD_HPG)
    y_ssd, ssd_states = _ssd_fwd(pre, dt_g, ac_g, act_g, dskx, tb)
    ys, yst = _ssd_norm_fwd(y_ssd, proj, ssd_norm_w, tr // 2)
    w_br, w_bs, w_o = weights.late(ys)
    p_r = _mm1(yr, w_br, tm=tm, tn=1024, tk=2048, out_dtype=bf16, name="branch_ret")
    p_s = _mm1(ys, w_bs, tm=tm, tn=1024, tk=4096, out_dtype=bf16, name="branch_ssd")
    merged, mergedt = _merge_fwd(p_r, p_s, proj, tr)
    mo = _mm1(merged, w_o, tm=tm, tn=1024, tk=2048, out_dtype=bf16, name="out_proj")
    dx2, dx2b, loss, g_norm_f = _final_fwd_bwd(x, mo, target, norm_f_w.reshape(1, D), tr)

    tkt = min(4096, S)
    wg = lambda at, b, name, tn=1024: _mm1(at, b, tm=min(1024, at.shape[0]), tn=tn, tk=tkt, out_dtype=f32, name=name)
    dm = _mm1(dx2b, w_o, tm=tm, tn=1024, tk=2048, out_dtype=bf16, name="d_merged", tb=True)
    g_w_o = wg(mergedt, dx2b, "g_w_out")
    dp_r, dp_s, dproj = _merge_bwd(dm, p_r, p_s, proj, tr)
    dyr = _mm1(dp_r, w_br, tm=tm, tn=1024, tk=2048, out_dtype=bf16, name="d_yr", tb=True)
    dys = _mm1(dp_s, w_bs, tm=tm, tn=1024, tk=2048, out_dtype=bf16, name="d_ys", tb=True)
    g_w_br = wg(yrt, dp_r, "g_w_br_ret")
    g_w_bs = wg(yst, dp_s, "g_w_br_ssd")
    dy_ssd, dproj, g_ssd_norm = _ssd_norm_bwd(y_ssd, proj, ssd_norm_w, dys, dproj, tr // 2)
    dproj = _ret_bwd(proj, cos, sin, intra, qdec, kdec, cdec, y_ret, dyr, ret_states, dproj, tb)
    dpre, ddt_g, gsk, gal, gdb = _ssd_bwd(pre, dt_g, ac_g, act_g, sg_g, dskx, nega_g, dy_ssd, ssd_states, tb)
    dproj, gcw, gcb = _conv_bwd(dpre, proj_x, conv_wm, dproj, min(1024, S), 512)
    ddt = jnp.transpose(ddt_g, (1, 0, 2)).reshape(S, SSD_HEADS)
    ddt_p = jnp.pad(ddt, ((0, 0), (0, 128 - SSD_HEADS))).astype(bf16)

    hr = D // 2
    wg_half = lambda off, b, name, tn=1024: _mm([(ht, 0, b, 0, 0, S, tkt)], hr, b.shape[1], tm=hr, tn=tn, out_dtype=f32,
                                                 name=name, row_off=off)
    off_sib, off_own = reducer.halves()
    gs_main = wg_half(off_sib, dproj, "g_w_in_main_sib")
    gs_dt = wg_half(off_sib, ddt_p, "g_w_in_dt_sib", tn=128)
    swap_state = reducer.first(gs_main, gs_dt, g_w_br, g_w_bs, g_w_o)
    ddt_p = ddt_p + swap_state[-1][0, 0].astype(bf16)
    go_main = wg_half(off_own, dproj, "g_w_in_main_own")
    go_dt = wg_half(off_own, ddt_p, "g_w_in_dt_own", tn=128)
    reduce_state = reducer.second(swap_state, go_main, go_dt)
    ddt_p = ddt_p + reduce_state[-1][0, 0].astype(bf16)
    dh = _mm([(dproj, 0, w_main, 0, 0, N_MAIN, N_MAIN // 8), (ddt_p, 0, w_dt, 0, 0, 128, 128)], S, D, tm=tm, tn=1024,
             out_dtype=bf16, name="d_h", tb=True)
    grad_x, g_norm1 = _norm1_bwd(x, norm1_w, dh, dx2, tr)

    seg = lambda v: jnp.sum(v.reshape(SSD_HEADS, SSD_GW // SSD_HPG), axis=1).reshape(1, SSD_HEADS)
    grads = dict(
        norm1_w=g_norm1, w_in_main=(gs_main, go_main), w_in_dt=(gs_dt, go_dt),
        conv_w=_xbc_original(gcw), conv_b=_xbc_original(gcb),
        dt_bias=gdb.reshape(1, SSD_HEADS), a_log=gal.reshape(1, SSD_HEADS), d_skip=seg(gsk),
        ssd_norm_w=g_ssd_norm, w_br_ret=g_w_br, w_br_ssd=g_w_bs, w_out=g_w_o, norm_f_w=g_norm_f,
    )
    return loss, grad_x, grads, reduce_state


def _me():
    return lax.axis_index("x"), lax.axis_index("y"), lax.axis_index("c")


def _other_chips(x, y):
    return [(1 - x, y), (x, 1 - y), (1 - x, 1 - y)]


def _quarter_rows(n_rows, half, q):
    return pl.ds(pl.multiple_of(half * (n_rows // 2) + q * (n_rows // 4), 8), n_rows // 4)


def _gather_own_copies(srcs, lands, send_sems, recv_sems):
    (a_ref, cw_ref), (ga_ref, gc_ref) = srcs, lands
    x, y, c = _me()
    k, nx, ny = 2 * x + y, (1 - x, y, c), (x, 1 - y, c)

    def cp(sem, q, to):
        rows = _quarter_rows(a_ref.shape[0], c, q)
        return pltpu.make_async_remote_copy(src_ref=a_ref.at[rows, :], dst_ref=ga_ref.at[k, rows, :], send_sem=send_sems.at[sem],
                                            recv_sem=recv_sems.at[sem], device_id=to, device_id_type=MESH)

    small = [pltpu.make_async_remote_copy(src_ref=cw_ref, dst_ref=gc_ref.at[k], send_sem=send_sems.at[4 + j],
                                          recv_sem=recv_sems.at[4 + j], device_id=(*chip, c), device_id_type=MESH)
             for j, chip in enumerate(_other_chips(x, y))]
    return [cp(0, 0, nx), cp(2, 1, ny), cp(1, 1, nx), cp(3, 0, ny)] + small


def _gather_own_start(a, cw):
    srcs = [a, cw]
    lands = [lax.empty((N_SHARD,) + a.shape, a.dtype), lax.empty((N_SHARD,) + cw.shape, cw.dtype)]

    def body(*refs):
        for cp in _gather_own_copies(refs[0:2], refs[2:4], refs[4], refs[5]):
            cp.start()
        refs[10][...] = jnp.zeros_like(refs[10])

    hbm = lambda t: pltpu.HBM(t.shape, t.dtype)
    out = pl.pallas_call(
        body, name="gather_own_start", in_specs=[HBM] * 4,
        out_specs=(SEM, SEM, *[HBM] * 4, pl.BlockSpec(memory_space=pltpu.VMEM)),
        out_shape=(pltpu.SemaphoreType.DMA((7,)), pltpu.SemaphoreType.DMA((7,)), *[hbm(t) for t in srcs + lands],
                   jax.ShapeDtypeStruct((8, 128), f32)),
        input_output_aliases={t: 2 + t for t in range(4)}, compiler_params=pltpu.CompilerParams(has_side_effects=DATAFLOW),
    )(*[pltpu.with_memory_space_constraint(t, pltpu.HBM) for t in srcs + lands])
    return out[0], out[1], list(out[2:4]), list(out[4:6]), out[6]


def _gather_own_wait(send_sems, recv_sems, srcs, lands, after):
    def body(*refs):
        for cp in _gather_own_copies(refs[0:2], refs[2:4], refs[4], refs[5]):
            cp.wait_send()
            cp.wait_recv()

    hbm = lambda t: pltpu.HBM(t.shape, t.dtype)
    out = pl.pallas_call(
        body, name="gather_own_wait", in_specs=[HBM] * 4 + [SEM, SEM, ANY], out_specs=[HBM] * 4,
        out_shape=[hbm(t) for t in list(srcs) + list(lands)], input_output_aliases={t: t for t in range(4)},
        compiler_params=pltpu.CompilerParams(has_side_effects=DATAFLOW),
    )(*srcs, *lands, send_sems, recv_sems, after)
    return list(out[2:])


def _gather_pass_on(ga):
    R = ga.shape[1]

    def body(_, ga_ref, send_sems, recv_sems):
        x, y, c = _me()
        me, sibling, nx, ny = (x, y, c), (x, y, 1 - c), (1 - x, y, c), (x, 1 - y, c)
        kx, ky, kd = 2 * (1 - x) + y, 2 * x + (1 - y), 2 * (1 - x) + (1 - y)

        def cp(sem, shard, half, q, to):
            ref = ga_ref.at[shard, _quarter_rows(R, half, q), :]
            return pltpu.make_async_remote_copy(src_ref=ref, dst_ref=ref, send_sem=send_sems.at[sem], recv_sem=recv_sems.at[sem],
                                                device_id=to, device_id_type=MESH)

        landed = [(kx, 0), (kx, 1), (ky, 1), (ky, 0)]
        sends = [cp(0, kx, c, 0, ny), cp(1, ky, c, 1, nx)] + [cp(2 + i, shard, c, q, sibling) for i, (shard, q) in enumerate(landed)]
        for s in sends:
            s.start()
        for sem in (0, 1):
            cp(sem, kd, c, sem, me).wait_recv()
            sends.append(cp(6 + sem, kd, c, sem, sibling))
            sends[-1].start()
        for i, (shard, q) in enumerate(landed + [(kd, 0), (kd, 1)]):
            cp(2 + i, shard, 1 - c, q, me).wait_recv()
        for s in sends:
            s.wait_send()

    return pl.pallas_call(
        body, name="gather_pass_on", in_specs=[ANY], out_specs=ANY, out_shape=jax.ShapeDtypeStruct(ga.shape, ga.dtype),
        input_output_aliases={0: 0}, scratch_shapes=[pltpu.SemaphoreType.DMA((8,)), pltpu.SemaphoreType.DMA((8,))],
        compiler_params=pltpu.CompilerParams(has_side_effects=True),
    )(ga)


def _gather_late_copies(src, land, send_sems, recv_sems):
    x, y, c = _me()
    k = 2 * x + y
    return [pltpu.make_async_remote_copy(src_ref=src, dst_ref=land.at[k], send_sem=send_sems.at[j], recv_sem=recv_sems.at[j],
                                         device_id=(*chip, c), device_id_type=MESH) for j, chip in enumerate(_other_chips(x, y))]


def _gather_late_start(b):
    land = lax.empty((N_SHARD,) + b.shape, b.dtype)

    def body(b_ref, land_ref, send_sems, recv_sems, b_thru, land_thru, token):
        for cp in _gather_late_copies(b_ref, land_ref, send_sems, recv_sems):
            cp.start()
        token[...] = jnp.zeros_like(token)

    return pl.pallas_call(
        body, name="gather_late_start", in_specs=[HBM, HBM],
        out_specs=(SEM, SEM, HBM, HBM, pl.BlockSpec(memory_space=pltpu.VMEM)),
        out_shape=(pltpu.SemaphoreType.DMA((3,)), pltpu.SemaphoreType.DMA((3,)), pltpu.HBM(b.shape, b.dtype),
                   pltpu.HBM(land.shape, land.dtype), jax.ShapeDtypeStruct((8, 128), f32)),
        input_output_aliases={0: 2, 1: 3}, compiler_params=pltpu.CompilerParams(has_side_effects=DATAFLOW),
    )(pltpu.with_memory_space_constraint(b, pltpu.HBM), pltpu.with_memory_space_constraint(land, pltpu.HBM))


def _gather_late_wait(send_sems, recv_sems, src, land, after):
    def body(b_ref, land_ref, send_sems_ref, recv_sems_ref, after_ref, b_dead, land_out):
        x, y, c = _me()
        for j, chip in enumerate(_other_chips(x, y)):
            kk = 2 * chip[0] + chip[1]
            cp = pltpu.make_async_remote_copy(src_ref=b_ref, dst_ref=land_ref.at[kk], send_sem=send_sems_ref.at[j],
                                              recv_sem=recv_sems_ref.at[j], device_id=(x, y, c), device_id_type=MESH)
            cp.wait_send()
            cp.wait_recv()

    return pl.pallas_call(
        body, name="gather_late_wait", in_specs=[HBM, HBM, SEM, SEM, ANY], out_specs=[HBM, HBM],
        out_shape=[pltpu.HBM(src.shape, src.dtype), pltpu.HBM(land.shape, land.dtype)], input_output_aliases={0: 0, 1: 1},
        compiler_params=pltpu.CompilerParams(has_side_effects=DATAFLOW),
    )(src, land, send_sems, recv_sems, after)[1]


HBM = pl.BlockSpec(memory_space=pltpu.HBM)
SEM = pl.BlockSpec(memory_space=pltpu.SEMAPHORE)
DATAFLOW = pltpu.SideEffectType.DATAFLOW_SIDE_EFFECTING


def _swap_copies(srcs, lands, send_sems, recv_sems):
    x, y, c = _me()

    def cp(src, dst, q):
        return pltpu.make_async_remote_copy(src_ref=src, dst_ref=dst, send_sem=send_sems.at[q], recv_sem=recv_sems.at[q],
                                            device_id=(x, y, 1 - c), device_id_type=MESH)

    return [cp(srcs[0], lands[0], 0), cp(srcs[1], lands[1], 1)] + [cp(srcs[2].at[s, 1 - c], lands[2].at[s], 2 + s) for s in range(N_SHARD)]


def _sibling_swap_start(g_main, g_dt, g_b):
    srcs = [g_main, g_dt, g_b]
    lands = [lax.empty(g_main.shape, g_main.dtype), lax.empty(g_dt.shape, g_dt.dtype),
             lax.empty(g_b.shape[:1] + g_b.shape[2:], g_b.dtype)]

    def body(*refs):
        for cp in _swap_copies(refs[0:3], refs[3:6], refs[6], refs[7]):
            cp.start()
        refs[14][...] = jnp.zeros_like(refs[14])

    hbm = lambda a: pltpu.HBM(a.shape, a.dtype)
    out = pl.pallas_call(
        body, name="sibling_swap_start", in_specs=[HBM] * 6,
        out_specs=(SEM, SEM, *[HBM] * 6, pl.BlockSpec(memory_space=pltpu.VMEM)),
        out_shape=(pltpu.SemaphoreType.DMA((2 + N_SHARD,)), pltpu.SemaphoreType.DMA((2 + N_SHARD,)), *[hbm(a) for a in srcs + lands],
                   jax.ShapeDtypeStruct((8, 128), f32)),
        input_output_aliases={t: 2 + t for t in range(6)}, compiler_params=pltpu.CompilerParams(has_side_effects=DATAFLOW),
    )(*[pltpu.with_memory_space_constraint(a, pltpu.HBM) for a in srcs + lands])
    return out[0], out[1], list(out[2:5]), list(out[5:8]), out[8]


def _sibling_swap_wait(send_sems, recv_sems, srcs, lands, after):
    def body(*refs):
        for cp in _swap_copies(refs[0:3], refs[3:6], refs[6], refs[7]):
            cp.wait_send()
            cp.wait_recv()

    hbm = lambda a: pltpu.HBM(a.shape, a.dtype)
    out = pl.pallas_call(
        body, name="sibling_swap_wait", in_specs=[HBM] * 6 + [SEM, SEM, ANY], out_specs=[HBM] * 6,
        out_shape=[hbm(a) for a in list(srcs) + list(lands)], input_output_aliases={t: t for t in range(6)},
        compiler_params=pltpu.CompilerParams(has_side_effects=DATAFLOW),
    )(*srcs, *lands, send_sems, recv_sems, after)
    return list(out[:3]), list(out[3:])


def _exchange_copies(ins, lands, send_sems, recv_sems):
    n = len(ins)
    x, y, c = _me()
    cps = []
    for j, chip in enumerate(_other_chips(x, y)):
        kk = 2 * chip[0] + chip[1]
        for t in range(n):
            cps.append(pltpu.make_async_remote_copy(
                src_ref=ins[t].at[kk], dst_ref=lands[t].at[j], send_sem=send_sems.at[n * j + t],
                recv_sem=recv_sems.at[n * j + t], device_id=(*chip, c), device_id_type=MESH))
    return cps


def _chip_exchange_start(arrs):
    n = len(arrs)
    lands = [lax.empty((3,) + a.shape[1:], a.dtype) for a in arrs]

    def body(*refs):
        ins, lands_in = refs[:n], refs[n:2 * n]
        send_sems, recv_sems = refs[2 * n], refs[2 * n + 1]
        token = refs[4 * n + 2]
        for cp in _exchange_copies(ins, lands_in, send_sems, recv_sems):
            cp.start()
        token[...] = jnp.zeros_like(token)

    hbm = lambda a: pltpu.HBM(a.shape, a.dtype)
    out = pl.pallas_call(
        body, name="chip_exchange_start", in_specs=[HBM] * (2 * n),
        out_specs=(SEM, SEM, *[HBM] * (2 * n), pl.BlockSpec(memory_space=pltpu.VMEM)),
        out_shape=(pltpu.SemaphoreType.DMA((3 * n,)), pltpu.SemaphoreType.DMA((3 * n,)), *[hbm(a) for a in arrs],
                   *[hbm(a) for a in lands], jax.ShapeDtypeStruct((8, 128), f32)),
        input_output_aliases={t: 2 + t for t in range(2 * n)},
        compiler_params=pltpu.CompilerParams(has_side_effects=DATAFLOW),
    )(*[pltpu.with_memory_space_constraint(a, pltpu.HBM) for a in list(arrs) + lands])
    return out[0], out[1], list(out[2:2 + n]), list(out[2 + n:2 + 2 * n]), out[2 + 2 * n]


def _chip_exchange_wait(send_sems, recv_sems, srcs, lands, after):
    n = len(srcs)

    def body(*refs):
        ins, lands_in = refs[:n], refs[n:2 * n]
        send_sems_ref, recv_sems_ref = refs[2 * n], refs[2 * n + 1]
        for cp in _exchange_copies(ins, lands_in, send_sems_ref, recv_sems_ref):
            cp.wait_send()
            cp.wait_recv()

    hbm = lambda a: pltpu.HBM(a.shape, a.dtype)
    out = pl.pallas_call(
        body, name="chip_exchange_wait", in_specs=[HBM] * (2 * n) + [SEM, SEM, ANY],
        out_specs=[HBM] * (2 * n), out_shape=[hbm(a) for a in list(srcs) + list(lands)],
        input_output_aliases={t: t for t in range(2 * n)},
        compiler_params=pltpu.CompilerParams(has_side_effects=DATAFLOW),
    )(*srcs, *lands, send_sems, recv_sems, after)
    return list(out[:n]), list(out[n:])


def _share_halves(bufs, by_cols, name):
    n = len(bufs)

    def body(*refs):
        ins, outs = refs[:n], refs[n:2 * n]
        send_sems, recv_sems = refs[2 * n], refs[2 * n + 1]
        x, y, c = _me()

        def part(ref, t, half):
            if by_cols[t]:
                w = bufs[t].shape[1] // 2
                return ref.at[:, pl.ds(pl.multiple_of(half * w, 128), w)]
            return ref.at[half]

        sends = [pltpu.make_async_remote_copy(src_ref=part(ins[t], t, c), dst_ref=part(outs[t], t, c), send_sem=send_sems.at[t],
                                              recv_sem=recv_sems.at[t], device_id=(x, y, 1 - c), device_id_type=MESH) for t in range(n)]
        for cp in sends:
            cp.start()
        for t in range(n):
            pltpu.make_async_remote_copy(src_ref=part(ins[t], t, c), dst_ref=part(outs[t], t, 1 - c), send_sem=send_sems.at[t],
                                         recv_sem=recv_sems.at[t], device_id=(x, y, c), device_id_type=MESH).wait_recv()
        for cp in sends:
            cp.wait_send()

    return pl.pallas_call(
        body, name=name, in_specs=[ANY] * n, out_specs=[ANY] * n,
        out_shape=[jax.ShapeDtypeStruct(a.shape, a.dtype) for a in bufs], input_output_aliases={t: t for t in range(n)},
        scratch_shapes=[pltpu.SemaphoreType.DMA((n,)), pltpu.SemaphoreType.DMA((n,))],
        compiler_params=pltpu.CompilerParams(has_side_effects=True),
    )(*bufs)


def _gather_vec(v):
    n = v.shape[1]

    def body(v_ref, o_ref, send_sems, recv_sems):
        x, y, c = _me()
        me = 4 * x + 2 * y + c
        cps = []
        for j in range(1, 8):
            fx, fy, fc = (j >> 2) & 1, (j >> 1) & 1, j & 1
            peer = (x ^ fx, y ^ fy, c ^ fc)
            cps.append(pltpu.make_async_remote_copy(
                src_ref=v_ref, dst_ref=o_ref.at[pl.ds(me, 1), :], send_sem=send_sems.at[j - 1], recv_sem=recv_sems.at[j - 1],
                device_id=peer, device_id_type=MESH))
        for cp in cps:
            cp.start()
        for j in range(1, 8):
            fx, fy, fc = (j >> 2) & 1, (j >> 1) & 1, j & 1
            src = 4 * (x ^ fx) + 2 * (y ^ fy) + (c ^ fc)
            pltpu.make_async_remote_copy(
                src_ref=v_ref, dst_ref=o_ref.at[pl.ds(src, 1), :], send_sem=send_sems.at[j - 1], recv_sem=recv_sems.at[j - 1],
                device_id=(x, y, c), device_id_type=MESH).wait_recv()
        for cp in cps:
            cp.wait_send()

    return pl.pallas_call(
        body, name="gather_vec", in_specs=[ANY], out_specs=ANY, out_shape=jax.ShapeDtypeStruct((8, n), v.dtype),
        scratch_shapes=[pltpu.SemaphoreType.DMA((7,)), pltpu.SemaphoreType.DMA((7,))],
        compiler_params=pltpu.CompilerParams(has_side_effects=True),
    )(v)


def _pair_sum(g, r, name, tr):
    L, hr, C = r.shape
    both_halves = g.ndim == 4

    def body(c_ref, g_ref, r_ref, o_ref):
        def strip(rows):
            gv = g_ref[0, 0, rows, :] if both_halves else g_ref[0, rows, :]
            o_ref[0, rows, :] = (gv + r_ref[0, rows, :]).astype(bf16)
        _for_strips(tr, strip)

    g_spec = (pl.BlockSpec((1, 1, tr, C), lambda s, i, c_ref: (s, c_ref[0], i, 0)) if both_halves
              else pl.BlockSpec((1, tr, C), lambda s, i, c_ref: (s, i, 0)))
    grid_spec = pltpu.PrefetchScalarGridSpec(
        num_scalar_prefetch=1, grid=(L, hr // tr),
        in_specs=[g_spec, pl.BlockSpec((1, tr, C), lambda s, i, c_ref: (s, i, 0))],
        out_specs=pl.BlockSpec((1, tr, C), lambda s, i, c_ref: (s, i, 0)))
    c = lax.axis_index("c").reshape(1).astype(jnp.int32)
    return pl.pallas_call(body, name=name, grid_spec=grid_spec, out_shape=jax.ShapeDtypeStruct((L, hr, C), bf16),
                          compiler_params=_params(("parallel", "parallel")))(c, g, r)


def _own_sum(p, got, name, transposed=False):
    _, hr, C = p.shape
    tr = SUM_ROWS
    c_full, c_pad = C // 128 * 128, -(-C // 128) * 128

    def total(p_ref, got_ref, rows):
        return ((p_ref[0, rows, :].astype(f32) + got_ref[0, rows, :].astype(f32)) + got_ref[1, rows, :].astype(f32)) \
            + got_ref[2, rows, :].astype(f32)

    def body(idx_ref, p_ref, got_ref, o_ref):
        def strip(rows):
            o_ref[0, rows, :] = total(p_ref, got_ref, rows)
        _for_strips(tr, strip)

    def body_t(idx_ref, p_ref, got_ref, o_ref, buf):
        if c_pad > c_full:
            buf[:, pl.ds(c_full, c_pad - c_full)] = jnp.zeros((tr, c_pad - c_full), f32)

        def strip(rows):
            buf[rows, pl.ds(0, C)] = total(p_ref, got_ref, rows)
        _for_strips(tr, strip)
        o_ref[...] = buf[...].T[:C]

    in_specs = [pl.BlockSpec((1, tr, C), lambda i, idx: (idx[0], i, 0)), pl.BlockSpec((3, tr, C), lambda i, idx: (0, i, 0))]
    x, y, c = _me()
    idx = jnp.stack([2 * x + y, c]).astype(jnp.int32)
    if transposed:
        grid_spec = pltpu.PrefetchScalarGridSpec(num_scalar_prefetch=1, grid=(hr // tr,), in_specs=in_specs,
                                                 out_specs=pl.BlockSpec((C, tr), lambda i, idx: (0, idx[1] * (hr // tr) + i)),
                                                 scratch_shapes=[pltpu.VMEM((tr, c_pad), f32)])
        return pl.pallas_call(body_t, name=name, grid_spec=grid_spec, out_shape=jax.ShapeDtypeStruct((C, 2 * hr), f32),
                              compiler_params=_params(("parallel",)))(idx, p, got)
    grid_spec = pltpu.PrefetchScalarGridSpec(num_scalar_prefetch=1, grid=(hr // tr,), in_specs=in_specs,
                                             out_specs=pl.BlockSpec((1, tr, C), lambda i, idx: (idx[1], i, 0)))
    return pl.pallas_call(body, name=name, grid_spec=grid_spec, out_shape=jax.ShapeDtypeStruct((2, hr, C), f32),
                          compiler_params=_params(("parallel",)))(idx, p, got)


def _adamw(w, g, m, v, name, tr):
    _, R, C = w.shape
    rs = min(8, tr)

    def body(w_ref, g_ref, m_ref, v_ref, d_ref, nm_ref, nv_ref):
        def strip(s, carry):
            rows = pl.ds(pl.multiple_of(s * rs, rs), rs)
            gv = g_ref[0, rows, :]
            mn = ADAM_B1 * m_ref[0, rows, :] + (1.0 - ADAM_B1) * gv
            vn = ADAM_B2 * v_ref[0, rows, :] + (1.0 - ADAM_B2) * (gv * gv)
            m_hat = mn / (1.0 - ADAM_B1 ** ADAM_STEP)
            v_hat = vn / (1.0 - ADAM_B2 ** ADAM_STEP)
            d_ref[0, rows, :] = -ADAM_LR * (m_hat / (jnp.sqrt(v_hat) + ADAM_EPS) + ADAM_WD * w_ref[0, rows, :])
            nm_ref[0, rows, :] = mn
            nv_ref[0, rows, :] = vn
            return carry

        if R % tr == 0:
            lax.fori_loop(0, tr // rs, strip, 0, unroll=min(2, tr // rs))
        else:
            lax.fori_loop(0, jnp.minimum(tr, R - pl.program_id(0) * tr) // rs, strip, 0)

    blk, grid = pl.BlockSpec((1, tr, C), lambda i: (0, i, 0)), (-(-R // tr),)
    o = jax.ShapeDtypeStruct((1, R, C), f32)
    return pl.pallas_call(body, name=name, grid=grid, in_specs=[blk] * 4, out_specs=[blk] * 3, out_shape=[o, o, o],
                          compiler_params=_params(("parallel",)))(w, g, m, v)


def _sum8(t):
    n = t.shape[1]

    def body(t_ref, o_ref):
        acc = t_ref[pl.ds(0, 1), :]
        for r in range(1, 8):
            acc = acc + t_ref[pl.ds(r, 1), :]
        o_ref[...] = acc

    return pl.pallas_call(body, name="sum_devices", out_shape=jax.ShapeDtypeStruct((1, n), f32))(t)


def _reduce_swap_start(g_main, g_dt, g_b):
    hr = g_main.shape[0]
    return _sibling_swap_start(g_main, g_dt, g_b.reshape(N_SHARD, 2, hr, g_b.shape[-1]))


def _reduce_start(swap_state, g_main, g_dt):
    hr = g_main.shape[0]
    send_sems, recv_sems, srcs, lands, _ = swap_state
    srcs, (r_main, r_dt, r_b) = _sibling_swap_wait(send_sems, recv_sems, srcs, lands, g_dt)
    p_main = _pair_sum(g_main[None], r_main[None], "pair_sum_main", SUM_ROWS // 4)
    p_dt = _pair_sum(g_dt[None], r_dt[None], "pair_sum_dt", SUM_ROWS)
    p_b = _pair_sum(srcs[2], r_b, "pair_sum_b", SUM_ROWS)
    p_in = jnp.transpose(_w_in_grad_full(p_main[0], p_dt[0]).reshape(hr, N_SHARD, W_IN_SHARD), (1, 0, 2))
    return _chip_exchange_start([p_in, p_b])


def _reduce_finish(state, after):
    send_sems, recv_sems, srcs, lands, _ = state
    (p_in, p_b), (got_in, got_b) = _chip_exchange_wait(send_sems, recv_sems, srcs, lands, after)
    mine_in, mine_b = _own_sum(p_in, got_in, "own_sum_in", transposed=True), _own_sum(p_b, got_b, "own_sum_b")
    full_in_t, full_b = _share_halves([mine_in, mine_b], [True, False], "share_halves")
    return full_in_t, full_b.reshape(-1, full_b.shape[-1])


def kernel(x, positions, norm1_w, w_in, conv_w, conv_b, dt_bias, a_log, d_skip, ssd_norm_w, w_br_ret, w_br_ssd, w_out, norm_f_w, loss_target, m_norm1_w, m_w_in, m_conv_w, m_conv_b, m_dt_bias, m_a_log, m_d_skip, m_ssd_norm_w, m_w_br_ret, m_w_br_ssd, m_w_out, m_norm_f_w, v_norm1_w, v_w_in, v_conv_w, v_conv_b, v_dt_bias, v_a_log, v_d_skip, v_ssd_norm_w, v_w_br_ret, v_w_br_ssd, v_w_out, v_norm_f_w):
    D = D_MODEL
    xi, yi, ci = _me()
    k = 2 * xi + yi
    me = 2 * k + ci
    weights = dict(norm1_w=norm1_w, w_in=w_in, conv_w=conv_w, conv_b=conv_b, dt_bias=dt_bias, a_log=a_log, d_skip=d_skip,
                   ssd_norm_w=ssd_norm_w, w_br_ret=w_br_ret, w_br_ssd=w_br_ssd, w_out=w_out, norm_f_w=norm_f_w)
    mom1 = dict(norm1_w=m_norm1_w, w_in=m_w_in, conv_w=m_conv_w, conv_b=m_conv_b, dt_bias=m_dt_bias, a_log=m_a_log, d_skip=m_d_skip,
                ssd_norm_w=m_ssd_norm_w, w_br_ret=m_w_br_ret, w_br_ssd=m_w_br_ssd, w_out=m_w_out, norm_f_w=m_norm_f_w)
    mom2 = dict(norm1_w=v_norm1_w, w_in=v_w_in, conv_w=v_conv_w, conv_b=v_conv_b, dt_bias=v_dt_bias, a_log=v_a_log, d_skip=v_d_skip,
                ssd_norm_w=v_ssd_norm_w, w_br_ret=v_w_br_ret, w_br_ssd=v_w_br_ssd, w_out=v_w_out, norm_f_w=v_norm_f_w)

    a_sh = w_in[0].astype(bf16)
    b_sh = jnp.concatenate([w_br_ret[0], w_br_ssd[0], w_out[0]], axis=0).astype(bf16)
    own_send, own_recv, own_srcs, own_lands, own_token = _gather_own_start(a_sh, conv_w[0])
    own = lambda g, s: lax.dynamic_update_slice_in_dim(g, s[None], k, axis=0)
    late_state = []

    class Weights:
        token = own_token

        @staticmethod
        def main(after):
            ga, gc = _gather_own_wait(own_send, own_recv, own_srcs, own_lands, after)
            ga = _gather_pass_on(ga)
            ga, b_late = lax.optimization_barrier((ga, b_sh))
            late_state.extend(_gather_late_start(b_late))
            w_main, w_dt = _w_main_from_shards(own(ga, a_sh))
            conv_full = jnp.transpose(own(gc, conv_w[0]), (1, 0, 2)).reshape(SSD_CONV, CONV_DIM)
            return w_main, w_dt + late_state[4][0, 0].astype(bf16), conv_full

        @staticmethod
        def late(after):
            gb = own(_gather_late_wait(*late_state[:4], after), b_sh)
            return gb[:, 0:512].reshape(2048, D), gb[:, 512:1536].reshape(4096, D), gb[:, 1536:2048].reshape(2048, D)

    class Reducer:
        @staticmethod
        def halves():
            return (1 - ci).reshape(1).astype(jnp.int32), ci.reshape(1).astype(jnp.int32)

        @staticmethod
        def first(g_main, g_dt, g_w_br, g_w_bs, g_w_o):
            g_b = jnp.concatenate([g_w_br.reshape(N_SHARD, 512, D), g_w_bs.reshape(N_SHARD, 1024, D),
                                   g_w_o.reshape(N_SHARD, 512, D)], axis=1)
            return _reduce_swap_start(g_main, g_dt, g_b)

        second = staticmethod(_reduce_start)

    loss, grad_x, g, reduce_state = _local_step(x[0], positions[0], loss_target[0], norm1_w, Weights, conv_b, dt_bias, a_log, d_skip,
                                                ssd_norm_w, norm_f_w, Reducer)

    grad_w_in_t, full_b = _reduce_finish(reduce_state, g["norm1_w"])
    grad_mats = dict(w_br_ret=full_b[0:512], w_br_ssd=full_b[512:1536], w_out=full_b[1536:2048])

    small = [(n, weights[n].size) for n in ("norm1_w", "conv_b", "dt_bias", "a_log", "d_skip", "ssd_norm_w", "norm_f_w")]
    parts = [jnp.pad(loss.reshape(1, 1), ((0, 0), (0, 127)))] + [g[n].reshape(1, -1) for n, _ in small] + [g["conv_w"].reshape(1, -1)]
    vec = jnp.concatenate(parts, axis=1)
    nv = vec.shape[1]
    nvp = -(-nv // 128) * 128
    vec = jnp.pad(vec, ((0, 0), (0, nvp - nv)))
    total = _sum8(lax.dynamic_update_slice_in_dim(_gather_vec(vec), vec, me, axis=0))
    loss_out = total[0, 0]
    off = 128
    grad_small = {}
    for n, sz in small:
        grad_small[n] = total[:, off:off + sz]
        off += sz
    g_conv = total[:, off:off + SSD_CONV * CONV_DIM].reshape(SSD_CONV, CONV_DIM)
    g_conv = lax.dynamic_slice_in_dim(g_conv, k * (CONV_DIM // N_SHARD), CONV_DIM // N_SHARD, axis=1)
    grad_small["conv_w"] = g_conv.reshape(1, -1)

    upd = {}
    for n in ("w_br_ret", "w_br_ssd", "w_out"):
        upd[n] = _adamw(weights[n], grad_mats[n][None], mom1[n], mom2[n], "adamw_" + n, tr=SUM_ROWS)
    tp = lambda t: jnp.swapaxes(t, 1, 2)
    upd["w_in"] = tuple(tp(t) for t in _adamw(tp(w_in), grad_w_in_t[None], tp(m_w_in), tp(v_w_in), "adamw_w_in", tr=256))
    grad_mats["w_in"] = tp(grad_w_in_t[None])
    names_small = [n for n, _ in small] + ["conv_w"]
    flat = lambda d: jnp.concatenate([d[n].reshape(1, -1) for n in names_small], axis=1)
    ns = sum(weights[n].size for n in names_small)
    nsp = -(-ns // 128) * 128
    padv = lambda t: jnp.pad(t, ((0, 0), (0, nsp - ns)))
    small_upd = _adamw(padv(flat(weights))[None], padv(flat(grad_small))[None], padv(flat(mom1))[None],
                       jnp.pad(flat(mom2), ((0, 0), (0, nsp - ns)), constant_values=1.0)[None], "adamw_small", 1)
    off = 0
    for n in names_small:
        sz = weights[n].size
        upd[n] = tuple(t[0, :, off:off + sz] for t in small_upd)
        off += sz

    order = ["norm1_w", "w_in", "conv_w", "conv_b", "dt_bias", "a_log", "d_skip", "ssd_norm_w", "w_br_ret", "w_br_ssd", "w_out", "norm_f_w"]
    grads_out = {**grad_mats, **grad_small}
    shp = lambda n, t: t.reshape(weights[n].shape)
    return (loss_out, grad_x[None], *[shp(n, grads_out[n]) for n in order], *[shp(n, upd[n][0]) for n in order],
            *[shp(n, upd[n][1]) for n in order], *[shp(n, upd[n][2]) for n in order])
```

```python
import jax
import jax.numpy as jnp
import numpy as np
from jax import lax
from jax.experimental import pallas as pl
from jax.experimental.pallas import tpu as pltpu

f32 = jnp.float32
bf16 = jnp.bfloat16
HIGHEST = lax.Precision.HIGHEST
MESH = pl.DeviceIdType.MESH

D_MODEL = 2048
EPS = 1e-6
CHUNK = 64
RET_HEADS = 8
RET_DK = 256
RET_HW = 4 * RET_DK
RET_HP = 2
RET_UNROLL_FWD, RET_UNROLL_BWD = 4, 8
ROPE_THETA = 10000.0
SSD_WIDTH = 4096
SSD_GROUPS = 8
SSD_STATE = 128
SSD_GW = 512
SSD_GC = SSD_GW + 2 * SSD_STATE
SSD_HPG = 8
SSD_CONV = 4
CONV_DIM = 6144
SSD_HEADS = 64
LS = 128

C_RET, C_Z, C_GATES, C_XBC = 0, 8192, 12288, 16384
N_MAIN = 22528
DT_OFF = 18432
IN_PROJ = 22592
N_SHARD = 4
W_IN_SHARD = IN_PROJ // N_SHARD

ADAM_LR, ADAM_B1, ADAM_B2, ADAM_EPS, ADAM_WD, ADAM_STEP = 0.001, 0.9, 0.999, 1e-08, 0.01, 10

VMEM_LIMIT = 56 * 1024 * 1024
SUM_ROWS = 128
ANY = pl.BlockSpec(memory_space=pl.ANY)


def _params(dims):
    return pltpu.CompilerParams(dimension_semantics=dims, vmem_limit_bytes=VMEM_LIMIT)


def _silu(x):
    return x * jax.nn.sigmoid(x)


def _nt(a, b):
    return lax.dot_general(a, b, (((1,), (1,)), ((), ())), preferred_element_type=f32)


def _tn(a, b):
    return lax.dot_general(a, b, (((0,), (0,)), ((), ())), preferred_element_type=f32)


def _nn(a, b):
    return jnp.dot(a, b, preferred_element_type=f32)


def _hi(a, b):
    return jnp.dot(a, b, precision=HIGHEST, preferred_element_type=f32)


def _split(a):
    hi = a.astype(bf16)
    return hi, (a - hi.astype(f32)).astype(bf16)


def _sel_r(a, sel):
    hi, lo = _split(a)
    return _nn(hi, sel) + _nn(lo, sel)


def _sel_l(sel, a):
    hi, lo = _split(a)
    return _nn(sel, hi) + _nn(sel, lo)


def _rows_to_cols(t, eye):
    hi = t.astype(bf16)
    r1 = t - hi.astype(f32)
    mid = r1.astype(bf16)
    lo = (r1 - mid.astype(f32)).astype(bf16)
    return _nt(eye, hi) + _nt(eye, mid) + _nt(eye, lo)


def _xbc_group_major(t):
    R = t.shape[0]
    nb = SSD_GROUPS * SSD_STATE
    parts = [t[:, :SSD_WIDTH].reshape(R, SSD_GROUPS, SSD_GW), t[:, SSD_WIDTH:SSD_WIDTH + nb].reshape(R, SSD_GROUPS, SSD_STATE),
             t[:, SSD_WIDTH + nb:].reshape(R, SSD_GROUPS, SSD_STATE)]
    return jnp.concatenate(parts, axis=2).reshape(R, CONV_DIM)


def _xbc_original(t):
    R = t.shape[0]
    g = t.reshape(R, SSD_GROUPS, SSD_GC)
    parts = [g[:, :, :SSD_GW].reshape(R, SSD_WIDTH), g[:, :, SSD_GW:SSD_GW + SSD_STATE].reshape(R, SSD_GROUPS * SSD_STATE),
             g[:, :, SSD_GW + SSD_STATE:].reshape(R, SSD_GROUPS * SSD_STATE)]
    return jnp.concatenate(parts, axis=1)


def _main_segments():
    segs = []
    for h in range(RET_HEADS):
        segs += [(base + RET_DK * h, RET_DK) for base in (0, 2048, 4096, 6144)]
    segs += [(8192, SSD_WIDTH), (DT_OFF + SSD_HEADS, 2 * D_MODEL)]
    nb = SSD_GROUPS * SSD_STATE
    for g in range(SSD_GROUPS):
        segs += [(12288 + SSD_GW * g, SSD_GW), (12288 + SSD_WIDTH + SSD_STATE * g, SSD_STATE),
                 (12288 + SSD_WIDTH + nb + SSD_STATE * g, SSD_STATE)]
    return segs


def _w_main_from_shards(shards):
    def cols(lo, hi):
        out = []
        while lo < hi:
            s = lo // W_IN_SHARD
            top = min(hi, (s + 1) * W_IN_SHARD)
            out.append(shards[s][:, lo - s * W_IN_SHARD:top - s * W_IN_SHARD])
            lo = top
        return out

    main = jnp.concatenate([p for s, n in _main_segments() for p in cols(s, s + n)], axis=1)
    w_dt = jnp.pad(jnp.concatenate(cols(DT_OFF, DT_OFF + SSD_HEADS), axis=1), ((0, 0), (0, 128 - SSD_HEADS)))
    return main, w_dt


def _w_in_grad_full(g_main, g_dt):
    D = g_main.shape[0]
    ret = jnp.transpose(g_main[:, :C_Z].reshape(D, RET_HEADS, 4, RET_DK), (0, 2, 1, 3)).reshape(D, C_Z)
    return jnp.concatenate([ret, g_main[:, C_Z:C_GATES], _xbc_original(g_main[:, C_XBC:]), g_dt[:, :SSD_HEADS],
                            g_main[:, C_GATES:C_XBC]], axis=1)


def _mm(pairs, M, N, *, tm, tn, out_dtype, name, tb=False, row_off=None):
    P = len(pairs)
    nks = [K // tk for (_, _, _, _, _, K, tk) in pairs]
    starts = [int(s) for s in np.cumsum([0] + nks[:-1])]
    KT = int(sum(nks))
    npf = 0 if row_off is None else 1
    in_specs, args = [], []
    for (a, a_cb, b, b_kb, b_nb, K, tk), s, nk in zip(pairs, starts, nks):
        def kk(k, s=s, nk=nk):
            return jnp.clip(k - s, 0, nk - 1)
        in_specs.append(pl.BlockSpec((tm, tk), lambda m, n, k, *pf, kk=kk, a_cb=a_cb: (m + (pf[0][0] if pf else 0), a_cb + kk(k))))
        if tb:
            in_specs.append(pl.BlockSpec((tn, tk), lambda m, n, k, *pf, kk=kk, b_kb=b_kb, b_nb=b_nb: (b_nb + n, b_kb + kk(k))))
        else:
            in_specs.append(pl.BlockSpec((tk, tn), lambda m, n, k, *pf, kk=kk, b_kb=b_kb, b_nb=b_nb: (b_kb + kk(k), b_nb + n)))
        args += [a, b]

    def body(*refs):
        refs = refs[npf:]
        o_ref = refs[2 * P]
        k = pl.program_id(2)

        def prod(i):
            a = refs[2 * i][...].astype(bf16)
            b = refs[2 * i + 1][...].astype(bf16)
            return _nt(a, b) if tb else _nn(a, b)

        if KT == 1:
            o_ref[...] = prod(0).astype(out_dtype)
            return
        acc = refs[2 * P + 1]

        @pl.when(k == 0)
        def _():
            acc[...] = jnp.zeros_like(acc)

        for i in range(P):
            @pl.when((k >= starts[i]) & (k < starts[i] + nks[i]))
            def _(i=i):
                acc[...] += prod(i)

        @pl.when(k == KT - 1)
        def _():
            o_ref[...] = acc[...].astype(out_dtype)

    grid_spec = pltpu.PrefetchScalarGridSpec(
        num_scalar_prefetch=npf, grid=(M // tm, N // tn, KT), in_specs=in_specs,
        out_specs=pl.BlockSpec((tm, tn), lambda m, n, k, *pf: (m, n)),
        scratch_shapes=[] if KT == 1 else [pltpu.VMEM((tm, tn), f32)])
    return pl.pallas_call(
        body, name=name, grid_spec=grid_spec, out_shape=jax.ShapeDtypeStruct((M, N), out_dtype),
        compiler_params=_params(("parallel", "parallel", "arbitrary")),
    )(*([] if row_off is None else [row_off]), *args)


def _mm1(a, b, *, tm, tn, tk, out_dtype, name, tb=False):
    M, K = a.shape
    N = b.shape[0] if tb else b.shape[1]
    return _mm([(a, 0, b, 0, 0, K, tk)], M, N, tm=tm, tn=tn, out_dtype=out_dtype, name=name, tb=tb)


RS = 16
CS = 32


def _for_strips(n_rows, fn, rs=RS, unroll=4):
    def step(s, carry):
        fn(pl.ds(pl.multiple_of(s * rs, rs), rs))
        return carry
    n = n_rows // rs
    lax.fori_loop(0, n, step, 0, unroll=min(unroll, n))


def _norm1_fwd(x, w, tr):
    S, D = x.shape

    def body(x_ref, w_ref, h_ref, ht_ref):
        def strip(rows):
            xv = x_ref[rows, :]
            r = lax.rsqrt(jnp.mean(xv * xv, axis=-1, keepdims=True) + EPS)
            h_ref[rows, :] = (xv * r * w_ref[...]).astype(bf16)
        _for_strips(tr, strip)
        ht_ref[...] = h_ref[...].T

    return pl.pallas_call(
        body, name="norm1_fwd", grid=(S // tr,),
        in_specs=[pl.BlockSpec((tr, D), lambda i: (i, 0)), pl.BlockSpec((1, D), lambda i: (0, 0))],
        out_specs=[pl.BlockSpec((tr, D), lambda i: (i, 0)), pl.BlockSpec((D, tr), lambda i: (0, i))],
        out_shape=[jax.ShapeDtypeStruct((S, D), bf16), jax.ShapeDtypeStruct((D, S), bf16)], compiler_params=_params(("parallel",)),
    )(x, w)


def _norm1_bwd(x, w, dh, dx2, tr):
    S, D = x.shape

    def body(x_ref, w_ref, dh_ref, dx2_ref, gx_ref, gw_ref, acc):
        @pl.when(pl.program_id(0) == 0)
        def _():
            acc[...] = jnp.zeros_like(acc)

        def strip(rows):
            xv = x_ref[rows, :]
            r = lax.rsqrt(jnp.mean(xv * xv, axis=-1, keepdims=True) + EPS)
            xh = xv * r
            dhv = dh_ref[rows, :]
            acc[...] += dhv * xh
            dxh = dhv * w_ref[...]
            gx_ref[rows, :] = dx2_ref[rows, :] + r * (dxh - xh * jnp.mean(dxh * xh, axis=-1, keepdims=True))
        _for_strips(tr, strip)

        @pl.when(pl.program_id(0) == S // tr - 1)
        def _():
            gw_ref[...] = jnp.sum(acc[...], axis=0, keepdims=True)

    row = pl.BlockSpec((tr, D), lambda i: (i, 0))
    vec = pl.BlockSpec((1, D), lambda i: (0, 0))
    return pl.pallas_call(
        body, name="norm1_bwd", grid=(S // tr,), in_specs=[row, vec, row, row], out_specs=[row, vec],
        out_shape=[jax.ShapeDtypeStruct((S, D), f32), jax.ShapeDtypeStruct((1, D), f32)],
        scratch_shapes=[pltpu.VMEM((RS, D), f32)], compiler_params=_params(("arbitrary",)),
    )(x, w, dh, dx2)


def _final_fwd_bwd(x, mo, target, wf, tr):
    S, D = x.shape

    def body(x_ref, mo_ref, t_ref, w_ref, dx2_ref, dx2b_ref, loss_ref, gw_ref, acc, lacc):
        @pl.when(pl.program_id(0) == 0)
        def _():
            acc[...] = jnp.zeros_like(acc)
            lacc[...] = jnp.zeros_like(lacc)

        def strip(rows):
            x2 = x_ref[rows, :] + mo_ref[rows, :]
            r = lax.rsqrt(jnp.mean(x2 * x2, axis=-1, keepdims=True) + EPS)
            xh = x2 * r
            wv = w_ref[...]
            err = xh * wv - t_ref[rows, :]
            lacc[...] += jnp.mean(err * err, axis=-1, keepdims=True)
            dy = err * (1.0 / D)
            acc[...] += dy * xh
            dxh = dy * wv
            dx2 = r * (dxh - xh * jnp.mean(dxh * xh, axis=-1, keepdims=True))
            dx2_ref[rows, :] = dx2
            dx2b_ref[rows, :] = dx2.astype(bf16)
        _for_strips(tr, strip)

        @pl.when(pl.program_id(0) == S // tr - 1)
        def _():
            gw_ref[...] = jnp.sum(acc[...], axis=0, keepdims=True)
            loss_ref[...] = 0.5 * jnp.sum(lacc[...], axis=0, keepdims=True)

    row = pl.BlockSpec((tr, D), lambda i: (i, 0))
    vec = pl.BlockSpec((1, D), lambda i: (0, 0))
    return pl.pallas_call(
        body, name="final_norm_loss", grid=(S // tr,), in_specs=[row, row, row, vec],
        out_specs=[row, row, pl.BlockSpec((1, 1), lambda i: (0, 0)), vec],
        out_shape=[jax.ShapeDtypeStruct((S, D), f32), jax.ShapeDtypeStruct((S, D), bf16), jax.ShapeDtypeStruct((1, 1), f32),
                   jax.ShapeDtypeStruct((1, D), f32)],
        scratch_shapes=[pltpu.VMEM((RS, D), f32), pltpu.VMEM((RS, 1), f32)], compiler_params=_params(("arbitrary",)),
    )(x, mo, target, wf)


def _merge_fwd(p_r, p_s, proj, tr):
    S, D = p_r.shape

    def body(pr_ref, ps_ref, g_ref, o_ref, ot_ref):
        def strip(rows):
            gr, gs = g_ref[rows, pl.ds(0, D)].astype(f32), g_ref[rows, pl.ds(D, D)].astype(f32)
            o_ref[rows, :] = (jax.nn.sigmoid(gr) * pr_ref[rows, :] + jax.nn.sigmoid(gs) * ps_ref[rows, :]).astype(bf16)
        _for_strips(tr, strip)
        ot_ref[...] = o_ref[...].T

    row = pl.BlockSpec((tr, D), lambda i: (i, 0))
    return pl.pallas_call(
        body, name="merge_fwd", grid=(S // tr,),
        in_specs=[row, row, pl.BlockSpec((tr, 2 * D), lambda i: (i, C_GATES // (2 * D)))],
        out_specs=[row, pl.BlockSpec((D, tr), lambda i: (0, i))],
        out_shape=[jax.ShapeDtypeStruct((S, D), bf16), jax.ShapeDtypeStruct((D, S), bf16)], compiler_params=_params(("parallel",)),
    )(p_r, p_s, proj)


def _merge_bwd(dm, p_r, p_s, proj, tr):
    S, D = p_r.shape

    def body(dm_ref, pr_ref, ps_ref, g_ref, dpr_ref, dps_ref, dproj_ref):
        def strip(rows):
            dmv = dm_ref[rows, :]
            sr = jax.nn.sigmoid(g_ref[rows, pl.ds(0, D)].astype(f32))
            ss = jax.nn.sigmoid(g_ref[rows, pl.ds(D, D)].astype(f32))
            dpr_ref[rows, :] = (dmv * sr).astype(bf16)
            dps_ref[rows, :] = (dmv * ss).astype(bf16)
            dproj_ref[rows, pl.ds(0, D)] = (dmv * pr_ref[rows, :] * sr * (1.0 - sr)).astype(bf16)
            dproj_ref[rows, pl.ds(D, D)] = (dmv * ps_ref[rows, :] * ss * (1.0 - ss)).astype(bf16)
        _for_strips(tr, strip)

    row = pl.BlockSpec((tr, D), lambda i: (i, 0))
    gates = pl.BlockSpec((tr, 2 * D), lambda i: (i, C_GATES // (2 * D)))
    o = jax.ShapeDtypeStruct((S, D), bf16)
    return pl.pallas_call(
        body, name="merge_bwd", grid=(S // tr,), in_specs=[row, row, row, gates],
        out_specs=[row, row, gates], out_shape=[o, o, jax.ShapeDtypeStruct((S, N_MAIN), bf16)],
        compiler_params=_params(("parallel",)),
    )(dm, p_r, p_s, proj)


def _ssd_norm_fwd(y, proj, w, tr):
    S, W = y.shape

    def body(y_ref, z_ref, w_ref, o_ref, ot_ref):
        def strip(rows):
            u = y_ref[rows, :] * _silu(z_ref[rows, :].astype(f32))
            r = lax.rsqrt(jnp.mean(u * u, axis=-1, keepdims=True) + EPS)
            o_ref[rows, :] = (u * r * w_ref[...]).astype(bf16)
        _for_strips(tr, strip)
        ot_ref[...] = o_ref[...].T

    row = pl.BlockSpec((tr, W), lambda i: (i, 0))
    return pl.pallas_call(
        body, name="ssd_norm_fwd", grid=(S // tr,),
        in_specs=[row, pl.BlockSpec((tr, W), lambda i: (i, C_Z // W)), pl.BlockSpec((1, W), lambda i: (0, 0))],
        out_specs=[row, pl.BlockSpec((W, tr), lambda i: (0, i))],
        out_shape=[jax.ShapeDtypeStruct((S, W), bf16), jax.ShapeDtypeStruct((W, S), bf16)], compiler_params=_params(("parallel",)),
    )(y, proj, w)


def _ssd_norm_bwd(y, proj, w, dys, dproj, tr):
    S, W = y.shape

    def body(y_ref, z_ref, w_ref, d_ref, _, dy_ref, dz_ref, gw_ref, acc):
        @pl.when(pl.program_id(0) == 0)
        def _():
            acc[...] = jnp.zeros_like(acc)

        def strip(rows):
            yv, zv, dv = y_ref[rows, :], z_ref[rows, :].astype(f32), d_ref[rows, :]
            sg = jax.nn.sigmoid(zv)
            sz = zv * sg
            u = yv * sz
            r = lax.rsqrt(jnp.mean(u * u, axis=-1, keepdims=True) + EPS)
            un = u * r
            acc[...] += dv * un
            dun = dv * w_ref[...]
            du = r * (dun - un * jnp.mean(dun * un, axis=-1, keepdims=True))
            dy_ref[rows, :] = (du * sz).astype(bf16)
            dz_ref[rows, :] = (du * yv * (sg * (1.0 + zv * (1.0 - sg)))).astype(bf16)
        _for_strips(tr, strip)

        @pl.when(pl.program_id(0) == S // tr - 1)
        def _():
            gw_ref[...] = jnp.sum(acc[...], axis=0, keepdims=True)

    row = pl.BlockSpec((tr, W), lambda i: (i, 0))
    zcol = pl.BlockSpec((tr, W), lambda i: (i, C_Z // W))
    vec = pl.BlockSpec((1, W), lambda i: (0, 0))
    return pl.pallas_call(
        body, name="ssd_norm_bwd", grid=(S // tr,),
        in_specs=[row, zcol, vec, row, ANY], out_specs=[row, zcol, vec],
        out_shape=[jax.ShapeDtypeStruct((S, W), bf16), jax.ShapeDtypeStruct(dproj.shape, bf16), jax.ShapeDtypeStruct((1, W), f32)],
        input_output_aliases={4: 1}, scratch_shapes=[pltpu.VMEM((RS, W), f32)], compiler_params=_params(("arbitrary",)),
    )(y, proj, w, dys, dproj)


def _rope(t, cos, sin):
    t1, t2 = t[:, :128], t[:, 128:]
    return jnp.concatenate([t1 * cos - t2 * sin, t2 * cos + t1 * sin], axis=1)


def _rope_t(d, cos, sin):
    d1, d2 = d[:, :128], d[:, 128:]
    return jnp.concatenate([d1 * cos + d2 * sin, d2 * cos - d1 * sin], axis=1)


def _ret_specs(tb, rev_nb=None):
    def blk(i):
        return i if rev_nb is None else rev_nb - 1 - i
    head = pl.BlockSpec((tb, RET_HP * RET_HW), lambda h, i: (blk(i), h))
    tab = pl.BlockSpec((tb, 128), lambda h, i: (blk(i), 0))
    mat = pl.BlockSpec((RET_HP, CHUNK, CHUNK), lambda h, i: (h, 0, 0))
    vec = pl.BlockSpec((RET_HP, CHUNK, 1), lambda h, i: (h, 0, 0))
    one = pl.BlockSpec((RET_HP, 1, 1), lambda h, i: (h, 0, 0))
    own = pl.BlockSpec((tb, RET_HP * RET_DK), lambda h, i: (blk(i), h))
    st = pl.BlockSpec((RET_HP, tb // CHUNK, RET_DK, RET_DK), lambda h, i: (h, blk(i), 0, 0))
    return head, tab, mat, vec, one, own, st


def _ret_fwd(proj, cos, sin, intra, qdec, kdec, cdec, tb):
    S = proj.shape[0]
    nc = S // CHUNK
    scale = RET_DK ** -0.5
    dk = RET_DK

    def body(p_ref, cos_ref, sin_ref, m_ref, qd_ref, kd_ref, cd_ref, y_ref, yr_ref, yrt_ref, st_ref, st):
        @pl.when(pl.program_id(1) == 0)
        def _():
            st[...] = jnp.zeros_like(st)

        def head_chunk(hh, c, rows, cs, sn):
            mm, qd, kd, cd = m_ref[hh], qd_ref[hh], kd_ref[hh], cd_ref[hh]
            col = lambda j: pl.ds(hh * RET_HW + j * dk, dk)
            own = pl.ds(hh * dk, dk)
            qr = _rope(p_ref[rows, col(0)].astype(f32), cs, sn)
            kr = _rope(p_ref[rows, col(1)].astype(f32), cs, sn) * scale
            qb, kb, vb = qr.astype(bf16), kr.astype(bf16), p_ref[rows, col(2)].astype(bf16)
            stb = st[hh].astype(bf16)
            st_ref[hh, c] = stb
            sc = (_nt(qb, kb) * mm).astype(bf16)
            y = _nn(sc, vb) + _nn(qb, stb) * qd
            st[hh] = st[hh] * cd + _tn((kr * kd).astype(bf16), vb)
            y_ref[rows, own] = y
            mu = jnp.mean(y, axis=-1, keepdims=True)
            yc = y - mu
            var = jnp.mean(yc * yc, axis=-1, keepdims=True)
            yr_ref[rows, own] = (yc * lax.rsqrt(var + EPS) * _silu(p_ref[rows, col(3)].astype(f32))).astype(bf16)

        def chunk(c, carry):
            rows = pl.ds(pl.multiple_of(c * CHUNK, CHUNK), CHUNK)
            cs, sn = cos_ref[rows, :], sin_ref[rows, :]
            for hh in range(RET_HP):
                head_chunk(hh, c, rows, cs, sn)
            return carry

        lax.fori_loop(0, tb // CHUNK, chunk, 0, unroll=min(RET_UNROLL_FWD, tb // CHUNK))
        yrt_ref[...] = yr_ref[...].T

    head, tab, mat, vec, one, own, stspec = _ret_specs(tb)
    return pl.pallas_call(
        body, name="ret_fwd", grid=(RET_HEADS // RET_HP, S // tb),
        in_specs=[head, tab, tab, mat, vec, vec, one],
        out_specs=[own, own, pl.BlockSpec((RET_HP * RET_DK, tb), lambda h, i: (h, i)), stspec],
        out_shape=[jax.ShapeDtypeStruct((S, 2048), f32), jax.ShapeDtypeStruct((S, 2048), bf16), jax.ShapeDtypeStruct((2048, S), bf16),
                   jax.ShapeDtypeStruct((RET_HEADS, nc, dk, dk), bf16)],
        scratch_shapes=[pltpu.VMEM((RET_HP, dk, dk), f32)], compiler_params=_params(("parallel", "arbitrary")),
    )(proj, cos, sin, intra, qdec, kdec, cdec)


def _ret_bwd(proj, cos, sin, intra, qdec, kdec, cdec, y, dyr, states, dproj, tb):
    S = proj.shape[0]
    nb = S // tb
    nck = tb // CHUNK
    scale = RET_DK ** -0.5
    dk = RET_DK

    def body(p_ref, cos_ref, sin_ref, m_ref, qd_ref, kd_ref, cd_ref, y_ref, dyr_ref, st_ref, _, o_ref, dst):
        @pl.when(pl.program_id(1) == 0)
        def _():
            dst[...] = jnp.zeros_like(dst)

        def head_chunk(hh, c, rows, cs, sn):
            mm, qd, kd, cd = m_ref[hh], qd_ref[hh], kd_ref[hh], cd_ref[hh]
            col = lambda j: pl.ds(hh * RET_HW + j * dk, dk)
            own = pl.ds(hh * dk, dk)
            qr = _rope(p_ref[rows, col(0)].astype(f32), cs, sn)
            kr = _rope(p_ref[rows, col(1)].astype(f32), cs, sn) * scale
            qb, kb, vb = qr.astype(bf16), kr.astype(bf16), p_ref[rows, col(2)].astype(bf16)
            kdb = (kr * kd).astype(bf16)
            stb = st_ref[hh, c]
            yv, gv, dyrv = y_ref[rows, own], p_ref[rows, col(3)].astype(f32), dyr_ref[rows, own]
            mu = jnp.mean(yv, axis=-1, keepdims=True)
            yc = yv - mu
            rstd = lax.rsqrt(jnp.mean(yc * yc, axis=-1, keepdims=True) + EPS)
            yn = yc * rstd
            sg = jax.nn.sigmoid(gv)
            o_ref[rows, col(3)] = (dyrv * yn * (sg * (1.0 + gv * (1.0 - sg)))).astype(bf16)
            dyn = dyrv * (gv * sg)
            dy = rstd * (dyn - jnp.mean(dyn, axis=-1, keepdims=True) - yn * jnp.mean(dyn * yn, axis=-1, keepdims=True))
            dyb = dy.astype(bf16)
            dyqb = (dy * qd).astype(bf16)
            dstb = dst[hh].astype(bf16)
            sct =(_nt(kb, qb) * mm).astype(bf16)
            ds = (_nt(dyb, vb) * mm).astype(bf16)
            dsT = (_nt(vb, dyb) * mm).astype(bf16)
            dv = _nn(sct, dyb) + _nn(kdb, dstb)
            dqr = _nn(ds, kb) + _nt(dyqb, stb)
            dkr = _nn(dsT, qb) + _nt(vb, dstb) * kd
            dst[hh] = dst[hh] * cd + _tn(qb, dyqb)
            o_ref[rows, col(0)] = _rope_t(dqr, cs, sn).astype(bf16)
            o_ref[rows, col(1)] = (_rope_t(dkr, cs, sn) * scale).astype(bf16)
            o_ref[rows, col(2)] = dv.astype(bf16)

        def chunk(cc, carry):
            c = nck - 1 - cc
            rows = pl.ds(pl.multiple_of(c * CHUNK, CHUNK), CHUNK)
            cs, sn = cos_ref[rows, :], sin_ref[rows, :]
            for hh in range(RET_HP):
                head_chunk(hh, c, rows, cs, sn)
            return carry

        lax.fori_loop(0, nck, chunk, 0, unroll=min(RET_UNROLL_BWD, nck))

    head, tab, mat, vec, one, own, stspec = _ret_specs(tb, rev_nb=nb)
    return pl.pallas_call(
        body, name="ret_bwd", grid=(RET_HEADS // RET_HP, nb),
        in_specs=[head, tab, tab, mat, vec, vec, one, own, own, stspec, ANY],
        out_specs=head, out_shape=jax.ShapeDtypeStruct(dproj.shape, bf16), input_output_aliases={10: 0},
        scratch_shapes=[pltpu.VMEM((RET_HP, dk, dk), f32)], compiler_params=_params(("parallel", "arbitrary")),
    )(proj, cos, sin, intra, qdec, kdec, cdec, y, dyr, states, dproj)


def _conv_fwd(proj, conv_w, conv_b, tb, cw):
    S = proj.shape[0]
    off = 0

    def body(x_ref, halo_ref, w_ref, b_ref, o_ref, xe):
        xe[pl.ds(0, 8), :] = jnp.where(pl.program_id(1) == 0, 0.0, halo_ref[...])
        xe[pl.ds(8, CS), :] = x_ref[pl.ds(0, CS), :]
        ws = [w_ref[pl.ds(j, 1), :] for j in range(SSD_CONV)]
        for s in range(tb // CS):
            tap = (lambda j: xe[pl.ds(5 + j, CS), :]) if s == 0 else (lambda j, s=s: x_ref[pl.ds(s * CS - 3 + j, CS), :])
            acc = b_ref[...] + ws[0] * tap(0)
            for j in range(1, SSD_CONV):
                acc = acc + ws[j] * tap(j)
            o_ref[pl.ds(s * CS, CS), :] = acc

    return pl.pallas_call(
        body, name="conv_fwd", grid=(CONV_DIM // cw, S // tb),
        in_specs=[pl.BlockSpec((tb, cw), lambda j, i: (i, off + j)),
                  pl.BlockSpec((8, cw), lambda j, i: (jnp.maximum(i * (tb // 8) - 1, 0), off + j)),
                  pl.BlockSpec((SSD_CONV, cw), lambda j, i: (0, j)), pl.BlockSpec((1, cw), lambda j, i: (0, j))],
        out_specs=pl.BlockSpec((tb, cw), lambda j, i: (i, j)),
        out_shape=jax.ShapeDtypeStruct((S, CONV_DIM), f32),
        scratch_shapes=[pltpu.VMEM((CS + 8, cw), f32)], compiler_params=_params(("parallel", "arbitrary")),
    )(proj, proj, conv_w, conv_b)


def _conv_bwd(dpre, proj, conv_w, dproj, tb, cw):
    S, n = dpre.shape
    nb = S // tb
    xoff = C_XBC // cw

    def body(d_ref, dh_ref, x_ref, xh_ref, w_ref, _, dx_ref, gw_ref, gb_ref, de, xe, accw, accb):
        i = pl.program_id(1)

        @pl.when(i == 0)
        def _():
            accw[...] = jnp.zeros_like(accw)
            accb[...] = jnp.zeros_like(accb)

        ns = tb // CS
        de[pl.ds(0, CS), :] = d_ref[pl.ds(tb - CS, CS), :]
        de[pl.ds(CS, 8), :] = jnp.where(i == nb - 1, 0.0, dh_ref[...])
        xe[pl.ds(0, 8), :] = jnp.where(i == 0, 0.0, xh_ref[...])
        xe[pl.ds(8, CS), :] = x_ref[pl.ds(0, CS), :]
        ws = [w_ref[pl.ds(j, 1), :] for j in range(SSD_CONV)]
        fold = lambda p: sum(p[8 * q:8 * (q + 1)] for q in range(1, CS // 8)) + p[0:8]
        for s in range(ns):
            dv = d_ref[pl.ds(s * CS, CS), :]
            ahead = (lambda o: de[pl.ds(o, CS), :]) if s == ns - 1 else (lambda o, s=s: d_ref[pl.ds(s * CS + o, CS), :])
            xtap = (lambda j: xe[pl.ds(5 + j, CS), :]) if s == 0 else (lambda j, s=s: x_ref[pl.ds(s * CS - 3 + j, CS), :])
            acc = ws[SSD_CONV - 1] * dv
            for j in range(SSD_CONV - 1):
                acc = acc + ws[j] * ahead(3 - j)
            dx_ref[pl.ds(s * CS, CS), :] = acc.astype(bf16)
            accb[...] += fold(dv)
            for j in range(SSD_CONV):
                accw[j] += fold(dv * xtap(j))

        @pl.when(i == nb - 1)
        def _():
            gb_ref[...] = jnp.sum(accb[...], axis=0, keepdims=True)
            for j in range(SSD_CONV):
                gw_ref[pl.ds(j, 1), :] = jnp.sum(accw[j], axis=0, keepdims=True)

    return pl.pallas_call(
        body, name="conv_bwd", grid=(n // cw, nb),
        in_specs=[pl.BlockSpec((tb, cw), lambda j, i: (i, j)),
                  pl.BlockSpec((8, cw), lambda j, i: (jnp.minimum((i + 1) * (tb // 8), S // 8 - 1), j)),
                  pl.BlockSpec((tb, cw), lambda j, i: (i, j)),
                  pl.BlockSpec((8, cw), lambda j, i: (jnp.maximum(i * (tb // 8) - 1, 0), j)),
                  pl.BlockSpec((SSD_CONV, cw), lambda j, i: (0, j)), ANY],
        out_specs=[pl.BlockSpec((tb, cw), lambda j, i: (i, xoff + j)), pl.BlockSpec((SSD_CONV, cw), lambda j, i: (0, j)),
                   pl.BlockSpec((1, cw), lambda j, i: (0, j))],
        out_shape=[jax.ShapeDtypeStruct(dproj.shape, bf16), jax.ShapeDtypeStruct((SSD_CONV, n), f32), jax.ShapeDtypeStruct((1, n), f32)],
        input_output_aliases={5: 0},
        scratch_shapes=[pltpu.VMEM((CS + 8, cw), f32), pltpu.VMEM((CS + 8, cw), f32), pltpu.VMEM((SSD_CONV, 8, cw), f32),
                        pltpu.VMEM((8, cw), f32)],
        compiler_params=_params(("parallel", "arbitrary")),
    )(dpre, dpre, proj, proj, conv_w, dproj)


def _dt_prep(dt_raw, dt_bias, a_log, tb):
    S = dt_raw.shape[0]

    def body(r_ref, b_ref, al_ref, dt_ref, sg_ref, ac_ref):
        li = lax.broadcasted_iota(jnp.int32, (LS, LS), 0)
        si = lax.broadcasted_iota(jnp.int32, (LS, LS), 1)
        tri = (li >= si).astype(f32)
        neg_a = -jnp.exp(al_ref[...])
        for c in range(tb // LS):
            rows = pl.ds(c * LS, LS)
            xv = r_ref[rows, :] + b_ref[...]
            dtv = jax.nn.softplus(xv)
            dt_ref[rows, :] = dtv
            sg_ref[rows, :] = jax.nn.sigmoid(xv)
            ac_ref[rows, :] = _hi(tri, dtv * neg_a)

    row = pl.BlockSpec((tb, 128), lambda i: (i, 0))
    vec = pl.BlockSpec((1, 128), lambda i: (0, 0))
    o = jax.ShapeDtypeStruct((S, 128), f32)
    return pl.pallas_call(body, name="dt_prep", grid=(S // tb,), in_specs=[row, vec, vec], out_specs=[row, row, row],
                          out_shape=[o, o, o], compiler_params=_params(("parallel",)))(dt_raw, dt_bias, a_log)


def _group_major(t):
    S = t.shape[0]
    return jnp.transpose(t[:, :SSD_HEADS].reshape(S, SSD_GROUPS, SSD_HPG), (1, 0, 2))


def _group_major_t(t):
    S = t.shape[0]
    return jnp.transpose(t[:, :SSD_HEADS].reshape(S // LS, LS, SSD_GROUPS, SSD_HPG), (2, 0, 3, 1))


def _ssd_specs(tb, rev_nb=None):
    def blk(i):
        return i if rev_nb is None else rev_nb - 1 - i
    grp = pl.BlockSpec((tb, SSD_GC), lambda g, i: (blk(i), g))
    xs = pl.BlockSpec((tb, SSD_GW), lambda g, i: (blk(i), g))
    ph = pl.BlockSpec((1, tb, SSD_HPG), lambda g, i: (g, blk(i), 0))
    pht = pl.BlockSpec((1, tb // LS, SSD_HPG, LS), lambda g, i: (g, blk(i), 0, 0))
    gvec = pl.BlockSpec((1, 1, SSD_GW), lambda g, i: (g, 0, 0))
    ex = pl.BlockSpec((SSD_HPG, SSD_GW), lambda g, i: (0, 0))
    st = pl.BlockSpec((1, tb // LS, SSD_STATE, SSD_GW), lambda g, i: (g, blk(i), 0, 0))
    return grp, xs, ph, pht, gvec, ex, st


def _expander():
    return jnp.repeat(jnp.eye(SSD_HPG, dtype=f32), SSD_GW // SSD_HPG, axis=1).astype(bf16)


def _expand3(dt8, ac8, ex):
    stack = jnp.concatenate([dt8, jnp.exp(ac8), jnp.exp(ac8[LS - 1:LS, :] - ac8)], axis=0)
    wide = _sel_r(stack, ex)
    return wide[0:LS], wide[LS:2 * LS], wide[2 * LS:3 * LS]


def _ssd_fwd(pre, dt_g, ac_g, act_g, dskx, tb):
    S = pre.shape[0]
    nc = S // LS
    hd = SSD_GW // SSD_HPG

    def body(p_ref, dt_ref, ac_ref, act_ref, dsk_ref, ex_ref, y_ref, st_ref, st):
        @pl.when(pl.program_id(1) == 0)
        def _():
            st[...] = jnp.zeros_like(st)

        ex = ex_ref[...]
        li = lax.broadcasted_iota(jnp.int32, (LS, LS), 0)
        si = lax.broadcasted_iota(jnp.int32, (LS, LS), 1)
        causal = li >= si

        def chunk(c, carry):
            rows = pl.ds(pl.multiple_of(c * LS, LS), LS)
            xs = _silu(p_ref[rows, pl.ds(0, SSD_GW)])
            bcb = _silu(p_ref[rows, pl.ds(SSD_GW, SSD_STATE)]).astype(bf16)
            ccb = _silu(p_ref[rows, pl.ds(SSD_GW + SSD_STATE, SSD_STATE)]).astype(bf16)
            dt8, ac8, act = dt_ref[0, rows, :], ac_ref[0, rows, :], act_ref[0, c]
            dtx, eax, tailx = _expand3(dt8, ac8, ex)
            xdt = xs * dtx
            cb = _nt(ccb, bcb)
            stb = st[...].astype(bf16)
            st_ref[0, c] = stb
            xdtb = xdt.astype(bf16)
            outs = []
            for h in range(SSD_HPG):
                dec = jnp.exp(jnp.where(causal, ac8[:, h:h + 1] - act[h:h + 1, :], -1e30))
                outs.append(_nn((cb * dec).astype(bf16), xdtb[:, hd * h:hd * (h + 1)]))
            y_ref[rows, :] = (jnp.concatenate(outs, axis=1) + _nn(ccb, stb) * eax + dsk_ref[0] * xs).astype(bf16)
            st[...] = st[...] * eax[LS - 1:LS, :] + _tn(bcb, (xdt * tailx).astype(bf16))
            return carry

        lax.fori_loop(0, tb // LS, chunk, 0, unroll=min(4, tb // LS))

    grp, xs, ph, pht, gvec, ex, stspec = _ssd_specs(tb)
    return pl.pallas_call(
        body, name="ssd_fwd", grid=(SSD_GROUPS, S // tb),
        in_specs=[grp, ph, ph, pht, gvec, ex], out_specs=[xs, stspec],
        out_shape=[jax.ShapeDtypeStruct((S, SSD_WIDTH), bf16), jax.ShapeDtypeStruct((SSD_GROUPS, nc, SSD_STATE, SSD_GW), bf16)],
        scratch_shapes=[pltpu.VMEM((SSD_STATE, SSD_GW), f32)], compiler_params=_params(("parallel", "arbitrary")),
    )(pre, dt_g, ac_g, act_g, dskx, _expander())


def _ssd_bwd(pre, dt_g, ac_g, act_g, sg_g, dskx, nega_g, dy, states, tb):
    S = pre.shape[0]
    nb = S // tb
    nck = tb // LS
    hd = SSD_GW // SSD_HPG

    def body(p_ref, dt_ref, ac_ref, act_ref, sg_ref, dsk_ref, na_ref, ex_ref, ext_ref, dy_ref, st_ref,
             dp_ref, ddt_ref, gsk_ref, gal_ref, gdb_ref, dst, skacc):
        @pl.when(pl.program_id(1) == 0)
        def _():
            dst[...] = jnp.zeros_like(dst)
            skacc[...] = jnp.zeros_like(skacc)
            gal_ref[...] = jnp.zeros_like(gal_ref)
            gdb_ref[...] = jnp.zeros_like(gdb_ref)

        ex, ext = ex_ref[...], ext_ref[...]
        li = lax.broadcasted_iota(jnp.int32, (LS, LS), 0)
        si = lax.broadcasted_iota(jnp.int32, (LS, LS), 1)
        causal = li >= si
        anti = si >= li
        upper = anti.astype(bf16)
        eye = (si == li).astype(bf16)
        last_row = (lax.broadcasted_iota(jnp.int32, (LS, 1), 0) == LS - 1).astype(f32)
        head_id = lax.broadcasted_iota(jnp.int32, (1, SSD_HPG), 1)
        head_col = lax.broadcasted_iota(jnp.int32, (SSD_HPG, 1), 0)
        neg_a = na_ref[0]
        dskv = dsk_ref[0]

        def chunk(cc, carry):
            c = nck - 1 - cc
            rows = pl.ds(pl.multiple_of(c * LS, LS), LS)
            px = p_ref[rows, pl.ds(0, SSD_GW)]
            pb = p_ref[rows, pl.ds(SSD_GW, SSD_STATE)]
            pc = p_ref[rows, pl.ds(SSD_GW + SSD_STATE, SSD_STATE)]
            sgx, sgb, sgc = jax.nn.sigmoid(px), jax.nn.sigmoid(pb), jax.nn.sigmoid(pc)
            xs = px * sgx
            bcb = (pb * sgb).astype(bf16)
            ccb = (pc * sgc).astype(bf16)
            dt8, ac8, act = dt_ref[0, rows, :], ac_ref[0, rows, :], act_ref[0, c]
            dtx, eax, tailx = _expand3(dt8, ac8, ex)
            xdt = xs * dtx
            ex_last = eax[LS - 1:LS, :]
            stb = st_ref[0, c]
            dyv = dy_ref[rows, :]
            dyb = dyv.astype(bf16)
            xdtb = xdt.astype(bf16)
            skacc[...] += jnp.sum(dyv * xs, axis=0, keepdims=True)
            yinter = _nn(ccb, stb) * eax
            dzb = (dyv * eax).astype(bf16)
            dcc = _nt(dzb, stb)
            dstv = dst[...]
            dstb = dstv.astype(bf16)
            xt = xdt * tailx
            dxt = _nn(bcb, dstb)
            dbc = _nt(xt.astype(bf16), dstb)
            dxdt = dxt * tailx
            lastrow = jnp.sum(dxt * xt, axis=0, keepdims=True) + jnp.sum(dstv * stb.astype(f32), axis=0, keepdims=True) * ex_last
            dst[...] = dstv * ex_last + _tn(ccb, dzb)
            cb = _nt(ccb, bcb)
            cbt = _nt(bcb, ccb)
            dcb = jnp.zeros((LS, LS), f32)
            dac8 = jnp.zeros((LS, SSD_HPG), f32)
            dact = jnp.zeros((SSD_HPG, LS), f32)
            dxin = []
            for h in range(SSD_HPG):
                sl = slice(hd * h, hd * (h + 1))
                col, rowv = ac8[:, h:h + 1], act[h:h + 1, :]
                dec = jnp.exp(jnp.where(causal, col - rowv, -1e30))
                dect = jnp.exp(jnp.where(anti, rowv - col, -1e30))
                gm = cb * dec
                dgm = _nt(dyb[:, sl], xdtb[:, sl])
                dxin.append(_nn((cbt * dect).astype(bf16), dyb[:, sl]))
                dcb = dcb + dgm * dec
                w = dgm * gm
                dac8 = dac8 + jnp.sum(w, axis=1, keepdims=True) * (head_id == h).astype(f32)
                dact = dact + (head_col == h).astype(f32) * jnp.sum(w, axis=0, keepdims=True)
            dxintra = jnp.concatenate(dxin, axis=1)
            dcbb = dcb.astype(bf16)
            dcc = dcc + _nn(dcbb, bcb)
            dbc = dbc + _tn(dcbb, ccb)
            dxdt = dxdt + dxintra
            dacx = dyv * yinter - dxt * xt + last_row * lastrow
            red = _sel_r(jnp.concatenate([dacx, dxdt * xs], axis=0), ext)
            dac8 = dac8 - _rows_to_cols(dact, eye) + red[0:LS]
            da8 = _sel_l(upper, dac8)
            ddt8 = red[LS:2 * LS] + da8 * neg_a
            gal_ref[0] += jnp.sum(da8 * dt8 * neg_a, axis=0, keepdims=True)
            ddr = ddt8 * sg_ref[0, rows, :]
            ddt_ref[0, rows, :] = ddr
            gdb_ref[0] += jnp.sum(ddr, axis=0, keepdims=True)
            dsilu = lambda p, s: s * (1.0 + p * (1.0 - s))
            dp_ref[rows, pl.ds(0, SSD_GW)] = (dskv * dyv + dxdt * dtx) * dsilu(px, sgx)
            dp_ref[rows, pl.ds(SSD_GW, SSD_STATE)] = dbc * dsilu(pb, sgb)
            dp_ref[rows, pl.ds(SSD_GW + SSD_STATE, SSD_STATE)] = dcc * dsilu(pc, sgc)
            return carry

        lax.fori_loop(0, nck, chunk, 0, unroll=min(4, nck))

        @pl.when(pl.program_id(1) == nb - 1)
        def _():
            gsk_ref[0] = skacc[...]

    grp, xs, ph, pht, gvec, ex, stspec = _ssd_specs(tb, rev_nb=nb)
    small = pl.BlockSpec((1, 1, SSD_HPG), lambda g, i: (g, 0, 0))
    ext = pl.BlockSpec((SSD_GW, SSD_HPG), lambda g, i: (0, 0))
    sm = jax.ShapeDtypeStruct((SSD_GROUPS, 1, SSD_HPG), f32)
    expander = _expander()
    return pl.pallas_call(
        body, name="ssd_bwd", grid=(SSD_GROUPS, nb),
        in_specs=[grp, ph, ph, pht, ph, gvec, small, ex, ext, xs, stspec],
        out_specs=[grp, ph, gvec, small, small],
        out_shape=[jax.ShapeDtypeStruct((S, CONV_DIM), f32), jax.ShapeDtypeStruct((SSD_GROUPS, S, SSD_HPG), f32),
                   jax.ShapeDtypeStruct((SSD_GROUPS, 1, SSD_GW), f32), sm, sm],
        scratch_shapes=[pltpu.VMEM((SSD_STATE, SSD_GW), f32), pltpu.VMEM((1, SSD_GW), f32)],
        compiler_params=_params(("parallel", "arbitrary")),
    )(pre, dt_g, ac_g, act_g, sg_g, dskx, nega_g, expander, expander.T, dy, states)


def _tiles(S):
    return dict(tb=min(512, S), tr=min(256, S), tm=min(1024, S))


def _local_step(x, positions, target, norm1_w, weights, conv_b, dt_bias, a_log, d_skip, ssd_norm_w, norm_f_w, reducer):
    S, D = x.shape
    t = _tiles(S)
    tb, tr, tm = t["tb"], t["tr"], t["tm"]

    half = RET_DK // 2
    inv_freq = ROPE_THETA ** (-jnp.arange(half, dtype=f32) / half)
    ang = positions.astype(f32)[:, None] * inv_freq
    cos, sin = jnp.cos(ang), jnp.sin(ang)
    log_gamma = jnp.log1p(-(2.0 ** (-5.0 - jnp.arange(RET_HEADS, dtype=f32))))
    idx = jnp.arange(CHUNK, dtype=f32)
    intra = jnp.exp(jnp.abs(idx[:, None] - idx[None, :]) * log_gamma[:, None, None])
    qdec = jnp.exp((idx + 1.0)[None, :] * log_gamma[:, None])[:, :, None]
    kdec = jnp.exp((CHUNK - 1.0 - idx)[None, :] * log_gamma[:, None])[:, :, None]
    cdec = jnp.exp(CHUNK * log_gamma)[:, None, None]

    h, ht = _norm1_fwd(x, norm1_w + weights.token[0, 0], tr)
    w_main, w_dt, conv_w = weights.main(h)
    conv_wm, conv_bm = _xbc_group_major(conv_w), _xbc_group_major(conv_b)
    proj = _mm([(h, 0, w_main, 0, 0, D, D)], S, C_XBC, tm=tm, tn=1024, out_dtype=bf16, name="proj_main")
    proj_x = _mm([(h, 0, w_main, 0, C_XBC // 1024, D, D)], S, CONV_DIM, tm=tm, tn=1024, out_dtype=f32, name="proj_xbc")
    dt_raw = _mm1(h, w_dt, tm=tm, tn=128, tk=D, out_dtype=f32, name="proj_dt")
    y_ret, yr, yrt, ret_states = _ret_fwd(proj, cos, sin, intra, qdec, kdec, cdec, tb)
    pre = _conv_fwd(proj_x, conv_wm, conv_bm, min(1024, S), 512)
    pad64 = lambda v: jnp.pad(v, ((0, 0), (0, 128 - SSD_HEADS)))
    dt, sg, ac = _dt_prep(dt_raw, pad64(dt_bias), pad64(a_log), tb)
    dt_g, ac_g, sg_g, act_g = _group_major(dt), _group_major(ac), _group_major(sg), _group_major_t(ac)
    dskx = jnp.repeat(d_skip.reshape(SSD_GROUPS, 1, SSD_HPG), SSD_GW // SSD_HPG, axis=2)
    nega_g = (-jnp.exp(a_log)).reshape(SSD_GROUPS, 1, SSD_HPG)
    y_ssd, ssd_states = _ssd_fwd(pre, dt_g, ac_g, act_g, dskx, tb)
    ys, yst = _ssd_norm_fwd(y_ssd, proj, ssd_norm_w, tr // 2)
    w_br, w_bs, w_o = weights.late(ys)
    p_r = _mm1(yr, w_br, tm=tm, tn=1024, tk=2048, out_dtype=bf16, name="branch_ret")
    p_s = _mm1(ys, w_bs, tm=tm, tn=1024, tk=4096, out_dtype=bf16, name="branch_ssd")
    merged, mergedt = _merge_fwd(p_r, p_s, proj, tr)
    mo = _mm1(merged, w_o, tm=tm, tn=1024, tk=2048, out_dtype=bf16, name="out_proj")
    dx2, dx2b, loss, g_norm_f = _final_fwd_bwd(x, mo, target, norm_f_w.reshape(1, D), tr)

    tkt = min(4096, S)
    wg = lambda at, b, name, tn=1024: _mm1(at, b, tm=min(1024, at.shape[0]), tn=tn, tk=tkt, out_dtype=f32, name=name)
    dm = _mm1(dx2b, w_o, tm=tm, tn=1024, tk=2048, out_dtype=bf16, name="d_merged", tb=True)
    g_w_o = wg(mergedt, dx2b, "g_w_out")
    dp_r, dp_s, dproj = _merge_bwd(dm, p_r, p_s, proj, tr)
    dyr = _mm1(dp_r, w_br, tm=tm, tn=1024, tk=2048, out_dtype=bf16, name="d_yr", tb=True)
    dys = _mm1(dp_s, w_bs, tm=tm, tn=1024, tk=2048, out_dtype=bf16, name="d_ys", tb=True)
    g_w_br = wg(yrt, dp_r, "g_w_br_ret")
    g_w_bs = wg(yst, dp_s, "g_w_br_ssd")
    dy_ssd, dproj, g_ssd_norm = _ssd_norm_bwd(y_ssd, proj, ssd_norm_w, dys, dproj, tr // 2)
    dproj = _ret_bwd(proj, cos, sin, intra, qdec, kdec, cdec, y_ret, dyr, ret_states, dproj, tb)
    dpre, ddt_g, gsk, gal, gdb = _ssd_bwd(pre, dt_g, ac_g, act_g, sg_g, dskx, nega_g, dy_ssd, ssd_states, tb)
    dproj, gcw, gcb = _conv_bwd(dpre, proj_x, conv_wm, dproj, min(1024, S), 512)
    ddt = jnp.transpose(ddt_g, (1, 0, 2)).reshape(S, SSD_HEADS)
    ddt_p = jnp.pad(ddt, ((0, 0), (0, 128 - SSD_HEADS))).astype(bf16)

    hr = D // 2
    wg_half = lambda off, b, name, tn=1024: _mm([(ht, 0, b, 0, 0, S, tkt)], hr, b.shape[1], tm=hr, tn=tn, out_dtype=f32,
                                                 name=name, row_off=off)
    off_sib, off_own = reducer.halves()
    gs_main = wg_half(off_sib, dproj, "g_w_in_main_sib")
    gs_dt = wg_half(off_sib, ddt_p, "g_w_in_dt_sib", tn=128)
    swap_state = reducer.first(gs_main, gs_dt, g_w_br, g_w_bs, g_w_o)
    ddt_p = ddt_p + swap_state[-1][0, 0].astype(bf16)
    go_main = wg_half(off_own, dproj, "g_w_in_main_own")
    go_dt = wg_half(off_own, ddt_p, "g_w_in_dt_own", tn=128)
    reduce_state = reducer.second(swap_state, go_main, go_dt)
    ddt_p = ddt_p + reduce_state[-1][0, 0].astype(bf16)
    dh = _mm([(dproj, 0, w_main, 0, 0, N_MAIN, N_MAIN // 4), (ddt_p, 0, w_dt, 0, 0, 128, 128)], S, D, tm=tm, tn=1024,
             out_dtype=bf16, name="d_h", tb=True)
    grad_x, g_norm1 = _norm1_bwd(x, norm1_w, dh, dx2, tr)

    seg = lambda v: jnp.sum(v.reshape(SSD_HEADS, SSD_GW // SSD_HPG), axis=1).reshape(1, SSD_HEADS)
    grads = dict(
        norm1_w=g_norm1, w_in_main=(gs_main, go_main), w_in_dt=(gs_dt, go_dt),
        conv_w=_xbc_original(gcw), conv_b=_xbc_original(gcb),
        dt_bias=gdb.reshape(1, SSD_HEADS), a_log=gal.reshape(1, SSD_HEADS), d_skip=seg(gsk),
        ssd_norm_w=g_ssd_norm, w_br_ret=g_w_br, w_br_ssd=g_w_bs, w_out=g_w_o, norm_f_w=g_norm_f,
    )
    return loss, grad_x, grads, reduce_state


def _me():
    return lax.axis_index("x"), lax.axis_index("y"), lax.axis_index("c")


def _other_chips(x, y):
    return [(1 - x, y), (x, 1 - y), (1 - x, 1 - y)]


def _quarter_rows(n_rows, half, q):
    return pl.ds(pl.multiple_of(half * (n_rows // 2) + q * (n_rows // 4), 8), n_rows // 4)


def _gather_own_copies(srcs, lands, send_sems, recv_sems):
    (a_ref, cw_ref), (ga_ref, gc_ref) = srcs, lands
    x, y, c = _me()
    k, nx, ny = 2 * x + y, (1 - x, y, c), (x, 1 - y, c)

    def cp(sem, q, to):
        rows = _quarter_rows(a_ref.shape[0], c, q)
        return pltpu.make_async_remote_copy(src_ref=a_ref.at[rows, :], dst_ref=ga_ref.at[k, rows, :], send_sem=send_sems.at[sem],
                                            recv_sem=recv_sems.at[sem], device_id=to, device_id_type=MESH)

    small = [pltpu.make_async_remote_copy(src_ref=cw_ref, dst_ref=gc_ref.at[k], send_sem=send_sems.at[4 + j],
                                          recv_sem=recv_sems.at[4 + j], device_id=(*chip, c), device_id_type=MESH)
             for j, chip in enumerate(_other_chips(x, y))]
    return [cp(0, 0, nx), cp(2, 1, ny), cp(1, 1, nx), cp(3, 0, ny)] + small


def _gather_own_start(a, cw):
    srcs = [a, cw]
    lands = [lax.empty((N_SHARD,) + a.shape, a.dtype), lax.empty((N_SHARD,) + cw.shape, cw.dtype)]

    def body(*refs):
        for cp in _gather_own_copies(refs[0:2], refs[2:4], refs[4], refs[5]):
            cp.start()
        refs[10][...] = jnp.zeros_like(refs[10])

    hbm = lambda t: pltpu.HBM(t.shape, t.dtype)
    out = pl.pallas_call(
        body, name="gather_own_start", in_specs=[HBM] * 4,
        out_specs=(SEM, SEM, *[HBM] * 4, pl.BlockSpec(memory_space=pltpu.VMEM)),
        out_shape=(pltpu.SemaphoreType.DMA((7,)), pltpu.SemaphoreType.DMA((7,)), *[hbm(t) for t in srcs + lands],
                   jax.ShapeDtypeStruct((8, 128), f32)),
        input_output_aliases={t: 2 + t for t in range(4)}, compiler_params=pltpu.CompilerParams(has_side_effects=DATAFLOW),
    )(*[pltpu.with_memory_space_constraint(t, pltpu.HBM) for t in srcs + lands])
    return out[0], out[1], list(out[2:4]), list(out[4:6]), out[6]


def _gather_own_wait(send_sems, recv_sems, srcs, lands, after):
    def body(*refs):
        for cp in _gather_own_copies(refs[0:2], refs[2:4], refs[4], refs[5]):
            cp.wait_send()
            cp.wait_recv()

    hbm = lambda t: pltpu.HBM(t.shape, t.dtype)
    out = pl.pallas_call(
        body, name="gather_own_wait", in_specs=[HBM] * 4 + [SEM, SEM, ANY], out_specs=[HBM] * 4,
        out_shape=[hbm(t) for t in list(srcs) + list(lands)], input_output_aliases={t: t for t in range(4)},
        compiler_params=pltpu.CompilerParams(has_side_effects=DATAFLOW),
    )(*srcs, *lands, send_sems, recv_sems, after)
    return list(out[2:])


def _gather_pass_on(ga):
    R = ga.shape[1]

    def body(_, ga_ref, send_sems, recv_sems):
        x, y, c = _me()
        me, sibling, nx, ny = (x, y, c), (x, y, 1 - c), (1 - x, y, c), (x, 1 - y, c)
        kx, ky, kd = 2 * (1 - x) + y, 2 * x + (1 - y), 2 * (1 - x) + (1 - y)

        def cp(sem, shard, half, q, to):
            ref = ga_ref.at[shard, _quarter_rows(R, half, q), :]
            return pltpu.make_async_remote_copy(src_ref=ref, dst_ref=ref, send_sem=send_sems.at[sem], recv_sem=recv_sems.at[sem],
                                                device_id=to, device_id_type=MESH)

        landed = [(kx, 0), (kx, 1), (ky, 1), (ky, 0)]
        sends = [cp(0, kx, c, 0, ny), cp(1, ky, c, 1, nx)] + [cp(2 + i, shard, c, q, sibling) for i, (shard, q) in enumerate(landed)]
        for s in sends:
            s.start()
        for sem in (0, 1):
            cp(sem, kd, c, sem, me).wait_recv()
            sends.append(cp(6 + sem, kd, c, sem, sibling))
            sends[-1].start()
        for i, (shard, q) in enumerate(landed + [(kd, 0), (kd, 1)]):
            cp(2 + i, shard, 1 - c, q, me).wait_recv()
        for s in sends:
            s.wait_send()

    return pl.pallas_call(
        body, name="gather_pass_on", in_specs=[ANY], out_specs=ANY, out_shape=jax.ShapeDtypeStruct(ga.shape, ga.dtype),
        input_output_aliases={0: 0}, scratch_shapes=[pltpu.SemaphoreType.DMA((8,)), pltpu.SemaphoreType.DMA((8,))],
        compiler_params=pltpu.CompilerParams(has_side_effects=True),
    )(ga)


def _gather_late_copies(src, land, send_sems, recv_sems):
    x, y, c = _me()
    k = 2 * x + y
    return [pltpu.make_async_remote_copy(src_ref=src, dst_ref=land.at[k], send_sem=send_sems.at[j], recv_sem=recv_sems.at[j],
                                         device_id=(*chip, c), device_id_type=MESH) for j, chip in enumerate(_other_chips(x, y))]


def _gather_late_start(b):
    land = lax.empty((N_SHARD,) + b.shape, b.dtype)

    def body(b_ref, land_ref, send_sems, recv_sems, b_thru, land_thru, token):
        for cp in _gather_late_copies(b_ref, land_ref, send_sems, recv_sems):
            cp.start()
        token[...] = jnp.zeros_like(token)

    return pl.pallas_call(
        body, name="gather_late_start", in_specs=[HBM, HBM],
        out_specs=(SEM, SEM, HBM, HBM, pl.BlockSpec(memory_space=pltpu.VMEM)),
        out_shape=(pltpu.SemaphoreType.DMA((3,)), pltpu.SemaphoreType.DMA((3,)), pltpu.HBM(b.shape, b.dtype),
                   pltpu.HBM(land.shape, land.dtype), jax.ShapeDtypeStruct((8, 128), f32)),
        input_output_aliases={0: 2, 1: 3}, compiler_params=pltpu.CompilerParams(has_side_effects=DATAFLOW),
    )(pltpu.with_memory_space_constraint(b, pltpu.HBM), pltpu.with_memory_space_constraint(land, pltpu.HBM))


def _gather_late_wait(send_sems, recv_sems, src, land, after):
    def body(b_ref, land_ref, send_sems_ref, recv_sems_ref, after_ref, b_dead, land_out):
        x, y, c = _me()
        for j, chip in enumerate(_other_chips(x, y)):
            kk = 2 * chip[0] + chip[1]
            cp = pltpu.make_async_remote_copy(src_ref=b_ref, dst_ref=land_ref.at[kk], send_sem=send_sems_ref.at[j],
                                              recv_sem=recv_sems_ref.at[j], device_id=(x, y, c), device_id_type=MESH)
            cp.wait_send()
            cp.wait_recv()

    return pl.pallas_call(
        body, name="gather_late_wait", in_specs=[HBM, HBM, SEM, SEM, ANY], out_specs=[HBM, HBM],
        out_shape=[pltpu.HBM(src.shape, src.dtype), pltpu.HBM(land.shape, land.dtype)], input_output_aliases={0: 0, 1: 1},
        compiler_params=pltpu.CompilerParams(has_side_effects=DATAFLOW),
    )(src, land, send_sems, recv_sems, after)[1]


HBM = pl.BlockSpec(memory_space=pltpu.HBM)
SEM = pl.BlockSpec(memory_space=pltpu.SEMAPHORE)
DATAFLOW = pltpu.SideEffectType.DATAFLOW_SIDE_EFFECTING


def _swap_copies(srcs, lands, send_sems, recv_sems):
    x, y, c = _me()

    def cp(src, dst, q):
        return pltpu.make_async_remote_copy(src_ref=src, dst_ref=dst, send_sem=send_sems.at[q], recv_sem=recv_sems.at[q],
                                            device_id=(x, y, 1 - c), device_id_type=MESH)

    return [cp(srcs[0], lands[0], 0), cp(srcs[1], lands[1], 1)] + [cp(srcs[2].at[s, 1 - c], lands[2].at[s], 2 + s) for s in range(N_SHARD)]


def _sibling_swap_start(g_main, g_dt, g_b):
    srcs = [g_main, g_dt, g_b]
    lands = [lax.empty(g_main.shape, g_main.dtype), lax.empty(g_dt.shape, g_dt.dtype),
             lax.empty(g_b.shape[:1] + g_b.shape[2:], g_b.dtype)]

    def body(*refs):
        for cp in _swap_copies(refs[0:3], refs[3:6], refs[6], refs[7]):
            cp.start()
        refs[14][...] = jnp.zeros_like(refs[14])

    hbm = lambda a: pltpu.HBM(a.shape, a.dtype)
    out = pl.pallas_call(
        body, name="sibling_swap_start", in_specs=[HBM] * 6,
        out_specs=(SEM, SEM, *[HBM] * 6, pl.BlockSpec(memory_space=pltpu.VMEM)),
        out_shape=(pltpu.SemaphoreType.DMA((2 + N_SHARD,)), pltpu.SemaphoreType.DMA((2 + N_SHARD,)), *[hbm(a) for a in srcs + lands],
                   jax.ShapeDtypeStruct((8, 128), f32)),
        input_output_aliases={t: 2 + t for t in range(6)}, compiler_params=pltpu.CompilerParams(has_side_effects=DATAFLOW),
    )(*[pltpu.with_memory_space_constraint(a, pltpu.HBM) for a in srcs + lands])
    return out[0], out[1], list(out[2:5]), list(out[5:8]), out[8]


def _sibling_swap_wait(send_sems, recv_sems, srcs, lands, after):
    def body(*refs):
        for cp in _swap_copies(refs[0:3], refs[3:6], refs[6], refs[7]):
            cp.wait_send()
            cp.wait_recv()

    hbm = lambda a: pltpu.HBM(a.shape, a.dtype)
    out = pl.pallas_call(
        body, name="sibling_swap_wait", in_specs=[HBM] * 6 + [SEM, SEM, ANY], out_specs=[HBM] * 6,
        out_shape=[hbm(a) for a in list(srcs) + list(lands)], input_output_aliases={t: t for t in range(6)},
        compiler_params=pltpu.CompilerParams(has_side_effects=DATAFLOW),
    )(*srcs, *lands, send_sems, recv_sems, after)
    return list(out[:3]), list(out[3:])


def _exchange_copies(ins, lands, send_sems, recv_sems):
    n = len(ins)
    x, y, c = _me()
    cps = []
    for j, chip in enumerate(_other_chips(x, y)):
        kk = 2 * chip[0] + chip[1]
        for t in range(n):
            cps.append(pltpu.make_async_remote_copy(
                src_ref=ins[t].at[kk], dst_ref=lands[t].at[j], send_sem=send_sems.at[n * j + t],
                recv_sem=recv_sems.at[n * j + t], device_id=(*chip, c), device_id_type=MESH))
    return cps


def _chip_exchange_start(arrs):
    n = len(arrs)
    lands = [lax.empty((3,) + a.shape[1:], a.dtype) for a in arrs]

    def body(*refs):
        ins, lands_in = refs[:n], refs[n:2 * n]
        send_sems, recv_sems = refs[2 * n], refs[2 * n + 1]
        token = refs[4 * n + 2]
        for cp in _exchange_copies(ins, lands_in, send_sems, recv_sems):
            cp.start()
        token[...] = jnp.zeros_like(token)

    hbm = lambda a: pltpu.HBM(a.shape, a.dtype)
    out = pl.pallas_call(
        body, name="chip_exchange_start", in_specs=[HBM] * (2 * n),
        out_specs=(SEM, SEM, *[HBM] * (2 * n), pl.BlockSpec(memory_space=pltpu.VMEM)),
        out_shape=(pltpu.SemaphoreType.DMA((3 * n,)), pltpu.SemaphoreType.DMA((3 * n,)), *[hbm(a) for a in arrs],
                   *[hbm(a) for a in lands], jax.ShapeDtypeStruct((8, 128), f32)),
        input_output_aliases={t: 2 + t for t in range(2 * n)},
        compiler_params=pltpu.CompilerParams(has_side_effects=DATAFLOW),
    )(*[pltpu.with_memory_space_constraint(a, pltpu.HBM) for a in list(arrs) + lands])
    return out[0], out[1], list(out[2:2 + n]), list(out[2 + n:2 + 2 * n]), out[2 + 2 * n]


def _chip_exchange_wait(send_sems, recv_sems, srcs, lands, after):
    n = len(srcs)

    def body(*refs):
        ins, lands_in = refs[:n], refs[n:2 * n]
        send_sems_ref, recv_sems_ref = refs[2 * n], refs[2 * n + 1]
        for cp in _exchange_copies(ins, lands_in, send_sems_ref, recv_sems_ref):
            cp.wait_send()
            cp.wait_recv()

    hbm = lambda a: pltpu.HBM(a.shape, a.dtype)
    out = pl.pallas_call(
        body, name="chip_exchange_wait", in_specs=[HBM] * (2 * n) + [SEM, SEM, ANY],
        out_specs=[HBM] * (2 * n), out_shape=[hbm(a) for a in list(srcs) + list(lands)],
        input_output_aliases={t: t for t in range(2 * n)},
        compiler_params=pltpu.CompilerParams(has_side_effects=DATAFLOW),
    )(*srcs, *lands, send_sems, recv_sems, after)
    return list(out[:n]), list(out[n:])


def _share_halves(bufs, by_cols, name):
    n = len(bufs)

    def body(*refs):
        ins, outs = refs[:n], refs[n:2 * n]
        send_sems, recv_sems = refs[2 * n], refs[2 * n + 1]
        x, y, c = _me()

        def part(ref, t, half):
            if by_cols[t]:
                w = bufs[t].shape[1] // 2
                return ref.at[:, pl.ds(pl.multiple_of(half * w, 128), w)]
            return ref.at[half]

        sends = [pltpu.make_async_remote_copy(src_ref=part(ins[t], t, c), dst_ref=part(outs[t], t, c), send_sem=send_sems.at[t],
                                              recv_sem=recv_sems.at[t], device_id=(x, y, 1 - c), device_id_type=MESH) for t in range(n)]
        for cp in sends:
            cp.start()
        for t in range(n):
            pltpu.make_async_remote_copy(src_ref=part(ins[t], t, c), dst_ref=part(outs[t], t, 1 - c), send_sem=send_sems.at[t],
                                         recv_sem=recv_sems.at[t], device_id=(x, y, c), device_id_type=MESH).wait_recv()
        for cp in sends:
            cp.wait_send()

    return pl.pallas_call(
        body, name=name, in_specs=[ANY] * n, out_specs=[ANY] * n,
        out_shape=[jax.ShapeDtypeStruct(a.shape, a.dtype) for a in bufs], input_output_aliases={t: t for t in range(n)},
        scratch_shapes=[pltpu.SemaphoreType.DMA((n,)), pltpu.SemaphoreType.DMA((n,))],
        compiler_params=pltpu.CompilerParams(has_side_effects=True),
    )(*bufs)


def _gather_vec(v):
    n = v.shape[1]

    def body(v_ref, o_ref, send_sems, recv_sems):
        x, y, c = _me()
        me = 4 * x + 2 * y + c
        cps = []
        for j in range(1, 8):
            fx, fy, fc = (j >> 2) & 1, (j >> 1) & 1, j & 1
            peer = (x ^ fx, y ^ fy, c ^ fc)
            cps.append(pltpu.make_async_remote_copy(
                src_ref=v_ref, dst_ref=o_ref.at[pl.ds(me, 1), :], send_sem=send_sems.at[j - 1], recv_sem=recv_sems.at[j - 1],
                device_id=peer, device_id_type=MESH))
        for cp in cps:
            cp.start()
        for j in range(1, 8):
            fx, fy, fc = (j >> 2) & 1, (j >> 1) & 1, j & 1
            src = 4 * (x ^ fx) + 2 * (y ^ fy) + (c ^ fc)
            pltpu.make_async_remote_copy(
                src_ref=v_ref, dst_ref=o_ref.at[pl.ds(src, 1), :], send_sem=send_sems.at[j - 1], recv_sem=recv_sems.at[j - 1],
                device_id=(x, y, c), device_id_type=MESH).wait_recv()
        for cp in cps:
            cp.wait_send()

    return pl.pallas_call(
        body, name="gather_vec", in_specs=[ANY], out_specs=ANY, out_shape=jax.ShapeDtypeStruct((8, n), v.dtype),
        scratch_shapes=[pltpu.SemaphoreType.DMA((7,)), pltpu.SemaphoreType.DMA((7,))],
        compiler_params=pltpu.CompilerParams(has_side_effects=True),
    )(v)


def _pair_sum(g, r, name, tr):
    L, hr, C = r.shape
    both_halves = g.ndim == 4

    def body(c_ref, g_ref, r_ref, o_ref):
        def strip(rows):
            gv = g_ref[0, 0, rows, :] if both_halves else g_ref[0, rows, :]
            o_ref[0, rows, :] = (gv + r_ref[0, rows, :]).astype(bf16)
        _for_strips(tr, strip)

    g_spec = (pl.BlockSpec((1, 1, tr, C), lambda s, i, c_ref: (s, c_ref[0], i, 0)) if both_halves
              else pl.BlockSpec((1, tr, C), lambda s, i, c_ref: (s, i, 0)))
    grid_spec = pltpu.PrefetchScalarGridSpec(
        num_scalar_prefetch=1, grid=(L, hr // tr),
        in_specs=[g_spec, pl.BlockSpec((1, tr, C), lambda s, i, c_ref: (s, i, 0))],
        out_specs=pl.BlockSpec((1, tr, C), lambda s, i, c_ref: (s, i, 0)))
    c = lax.axis_index("c").reshape(1).astype(jnp.int32)
    return pl.pallas_call(body, name=name, grid_spec=grid_spec, out_shape=jax.ShapeDtypeStruct((L, hr, C), bf16),
                          compiler_params=_params(("parallel", "parallel")))(c, g, r)


def _own_sum(p, got, name, transposed=False):
    _, hr, C = p.shape
    tr = SUM_ROWS
    c_full, c_pad = C // 128 * 128, -(-C // 128) * 128

    def total(p_ref, got_ref, rows):
        return ((p_ref[0, rows, :].astype(f32) + got_ref[0, rows, :].astype(f32)) + got_ref[1, rows, :].astype(f32)) \
            + got_ref[2, rows, :].astype(f32)

    def body(idx_ref, p_ref, got_ref, o_ref):
        def strip(rows):
            o_ref[0, rows, :] = total(p_ref, got_ref, rows)
        _for_strips(tr, strip)

    def body_t(idx_ref, p_ref, got_ref, o_ref, buf):
        if c_pad > c_full:
            buf[:, pl.ds(c_full, c_pad - c_full)] = jnp.zeros((tr, c_pad - c_full), f32)

        def strip(rows):
            buf[rows, pl.ds(0, C)] = total(p_ref, got_ref, rows)
        _for_strips(tr, strip)
        o_ref[...] = buf[...].T[:C]

    in_specs = [pl.BlockSpec((1, tr, C), lambda i, idx: (idx[0], i, 0)), pl.BlockSpec((3, tr, C), lambda i, idx: (0, i, 0))]
    x, y, c = _me()
    idx = jnp.stack([2 * x + y, c]).astype(jnp.int32)
    if transposed:
        grid_spec = pltpu.PrefetchScalarGridSpec(num_scalar_prefetch=1, grid=(hr // tr,), in_specs=in_specs,
                                                 out_specs=pl.BlockSpec((C, tr), lambda i, idx: (0, idx[1] * (hr // tr) + i)),
                                                 scratch_shapes=[pltpu.VMEM((tr, c_pad), f32)])
        return pl.pallas_call(body_t, name=name, grid_spec=grid_spec, out_shape=jax.ShapeDtypeStruct((C, 2 * hr), f32),
                              compiler_params=_params(("parallel",)))(idx, p, got)
    grid_spec = pltpu.PrefetchScalarGridSpec(num_scalar_prefetch=1, grid=(hr // tr,), in_specs=in_specs,
                                             out_specs=pl.BlockSpec((1, tr, C), lambda i, idx: (idx[1], i, 0)))
    return pl.pallas_call(body, name=name, grid_spec=grid_spec, out_shape=jax.ShapeDtypeStruct((2, hr, C), f32),
                          compiler_params=_params(("parallel",)))(idx, p, got)


def _adamw(w, g, m, v, name, tr):
    _, R, C = w.shape
    rs = min(8, tr)

    def body(w_ref, g_ref, m_ref, v_ref, d_ref, nm_ref, nv_ref):
        def strip(s, carry):
            rows = pl.ds(pl.multiple_of(s * rs, rs), rs)
            gv = g_ref[0, rows, :]
            mn = ADAM_B1 * m_ref[0, rows, :] + (1.0 - ADAM_B1) * gv
            vn = ADAM_B2 * v_ref[0, rows, :] + (1.0 - ADAM_B2) * (gv * gv)
            m_hat = mn / (1.0 - ADAM_B1 ** ADAM_STEP)
            v_hat = vn / (1.0 - ADAM_B2 ** ADAM_STEP)
            d_ref[0, rows, :] = -ADAM_LR * (m_hat / (jnp.sqrt(v_hat) + ADAM_EPS) + ADAM_WD * w_ref[0, rows, :])
            nm_ref[0, rows, :] = mn
            nv_ref[0, rows, :] = vn
            return carry

        if R % tr == 0:
            lax.fori_loop(0, tr // rs, strip, 0, unroll=min(2, tr // rs))
        else:
            lax.fori_loop(0, jnp.minimum(tr, R - pl.program_id(0) * tr) // rs, strip, 0)

    blk, grid = pl.BlockSpec((1, tr, C), lambda i: (0, i, 0)), (-(-R // tr),)
    o = jax.ShapeDtypeStruct((1, R, C), f32)
    return pl.pallas_call(body, name=name, grid=grid, in_specs=[blk] * 4, out_specs=[blk] * 3, out_shape=[o, o, o],
                          compiler_params=_params(("parallel",)))(w, g, m, v)


def _sum8(t):
    n = t.shape[1]

    def body(t_ref, o_ref):
        acc = t_ref[pl.ds(0, 1), :]
        for r in range(1, 8):
            acc = acc + t_ref[pl.ds(r, 1), :]
        o_ref[...] = acc

    return pl.pallas_call(body, name="sum_devices", out_shape=jax.ShapeDtypeStruct((1, n), f32))(t)


def _reduce_swap_start(g_main, g_dt, g_b):
    hr = g_main.shape[0]
    return _sibling_swap_start(g_main, g_dt, g_b.reshape(N_SHARD, 2, hr, g_b.shape[-1]))


def _reduce_start(swap_state, g_main, g_dt):
    hr = g_main.shape[0]
    send_sems, recv_sems, srcs, lands, _ = swap_state
    srcs, (r_main, r_dt, r_b) = _sibling_swap_wait(send_sems, recv_sems, srcs, lands, g_dt)
    p_main = _pair_sum(g_main[None], r_main[None], "pair_sum_main", SUM_ROWS // 4)
    p_dt = _pair_sum(g_dt[None], r_dt[None], "pair_sum_dt", SUM_ROWS)
    p_b = _pair_sum(srcs[2], r_b, "pair_sum_b", SUM_ROWS)
    p_in = jnp.transpose(_w_in_grad_full(p_main[0], p_dt[0]).reshape(hr, N_SHARD, W_IN_SHARD), (1, 0, 2))
    return _chip_exchange_start([p_in, p_b])


def _reduce_finish(state, after):
    send_sems, recv_sems, srcs, lands, _ = state
    (p_in, p_b), (got_in, got_b) = _chip_exchange_wait(send_sems, recv_sems, srcs, lands, after)
    mine_in, mine_b = _own_sum(p_in, got_in, "own_sum_in", transposed=True), _own_sum(p_b, got_b, "own_sum_b")
    full_in_t, full_b = _share_halves([mine_in, mine_b], [True, False], "share_halves")
    return full_in_t, full_b.reshape(-1, full_b.shape[-1])


def kernel(x, positions, norm1_w, w_in, conv_w, conv_b, dt_bias, a_log, d_skip, ssd_norm_w, w_br_ret, w_br_ssd, w_out, norm_f_w, loss_target, m_norm1_w, m_w_in, m_conv_w, m_conv_b, m_dt_bias, m_a_log, m_d_skip, m_ssd_norm_w, m_w_br_ret, m_w_br_ssd, m_w_out, m_norm_f_w, v_norm1_w, v_w_in, v_conv_w, v_conv_b, v_dt_bias, v_a_log, v_d_skip, v_ssd_norm_w, v_w_br_ret, v_w_br_ssd, v_w_out, v_norm_f_w):
    D = D_MODEL
    xi, yi, ci = _me()
    k = 2 * xi + yi
    me = 2 * k + ci
    weights = dict(norm1_w=norm1_w, w_in=w_in, conv_w=conv_w, conv_b=conv_b, dt_bias=dt_bias, a_log=a_log, d_skip=d_skip,
                   ssd_norm_w=ssd_norm_w, w_br_ret=w_br_ret, w_br_ssd=w_br_ssd, w_out=w_out, norm_f_w=norm_f_w)
    mom1 = dict(norm1_w=m_norm1_w, w_in=m_w_in, conv_w=m_conv_w, conv_b=m_conv_b, dt_bias=m_dt_bias, a_log=m_a_log, d_skip=m_d_skip,
                ssd_norm_w=m_ssd_norm_w, w_br_ret=m_w_br_ret, w_br_ssd=m_w_br_ssd, w_out=m_w_out, norm_f_w=m_norm_f_w)
    mom2 = dict(norm1_w=v_norm1_w, w_in=v_w_in, conv_w=v_conv_w, conv_b=v_conv_b, dt_bias=v_dt_bias, a_log=v_a_log, d_skip=v_d_skip,
                ssd_norm_w=v_ssd_norm_w, w_br_ret=v_w_br_ret, w_br_ssd=v_w_br_ssd, w_out=v_w_out, norm_f_w=v_norm_f_w)

    a_sh = w_in[0].astype(bf16)
    b_sh = jnp.concatenate([w_br_ret[0], w_br_ssd[0], w_out[0]], axis=0).astype(bf16)
    own_send, own_recv, own_srcs, own_lands, own_token = _gather_own_start(a_sh, conv_w[0])
    own = lambda g, s: lax.dynamic_update_slice_in_dim(g, s[None], k, axis=0)
    late_state = []

    class Weights:
        token = own_token

        @staticmethod
        def main(after):
            ga, gc = _gather_own_wait(own_send, own_recv, own_srcs, own_lands, after)
            ga = _gather_pass_on(ga)
            ga, b_late = lax.optimization_barrier((ga, b_sh))
            late_state.extend(_gather_late_start(b_late))
            w_main, w_dt = _w_main_from_shards(own(ga, a_sh))
            conv_full = jnp.transpose(own(gc, conv_w[0]), (1, 0, 2)).reshape(SSD_CONV, CONV_DIM)
            return w_main, w_dt + late_state[4][0, 0].astype(bf16), conv_full

        @staticmethod
        def late(after):
            gb = own(_gather_late_wait(*late_state[:4], after), b_sh)
            return gb[:, 0:512].reshape(2048, D), gb[:, 512:1536].reshape(4096, D), gb[:, 1536:2048].reshape(2048, D)

    class Reducer:
        @staticmethod
        def halves():
            return (1 - ci).reshape(1).astype(jnp.int32), ci.reshape(1).astype(jnp.int32)

        @staticmethod
        def first(g_main, g_dt, g_w_br, g_w_bs, g_w_o):
            g_b = jnp.concatenate([g_w_br.reshape(N_SHARD, 512, D), g_w_bs.reshape(N_SHARD, 1024, D),
                                   g_w_o.reshape(N_SHARD, 512, D)], axis=1)
            return _reduce_swap_start(g_main, g_dt, g_b)

        second = staticmethod(_reduce_start)

    loss, grad_x, g, reduce_state = _local_step(x[0], positions[0], loss_target[0], norm1_w, Weights, conv_b, dt_bias, a_log, d_skip,
                                                ssd_norm_w, norm_f_w, Reducer)

    grad_w_in_t, full_b = _reduce_finish(reduce_state, g["norm1_w"])
    grad_mats = dict(w_br_ret=full_b[0:512], w_br_ssd=full_b[512:1536], w_out=full_b[1536:2048])

    small = [(n, weights[n].size) for n in ("norm1_w", "conv_b", "dt_bias", "a_log", "d_skip", "ssd_norm_w", "norm_f_w")]
    parts = [jnp.pad(loss.reshape(1, 1), ((0, 0), (0, 127)))] + [g[n].reshape(1, -1) for n, _ in small] + [g["conv_w"].reshape(1, -1)]
    vec = jnp.concatenate(parts, axis=1)
    nv = vec.shape[1]
    nvp = -(-nv // 128) * 128
    vec = jnp.pad(vec, ((0, 0), (0, nvp - nv)))
    total = _sum8(lax.dynamic_update_slice_in_dim(_gather_vec(vec), vec, me, axis=0))
    loss_out = total[0, 0]
    off = 128
    grad_small = {}
    for n, sz in small:
        grad_small[n] = total[:, off:off + sz]
        off += sz
    g_conv = total[:, off:off + SSD_CONV * CONV_DIM].reshape(SSD_CONV, CONV_DIM)
    g_conv = lax.dynamic_slice_in_dim(g_conv, k * (CONV_DIM // N_SHARD), CONV_DIM // N_SHARD, axis=1)
    grad_small["conv_w"] = g_conv.reshape(1, -1)

    upd = {}
    for n in ("w_br_ret", "w_br_ssd", "w_out"):
        upd[n] = _adamw(weights[n], grad_mats[n][None], mom1[n], mom2[n], "adamw_" + n, tr=SUM_ROWS)
    tp = lambda t: jnp.swapaxes(t, 1, 2)
    upd["w_in"] = tuple(tp(t) for t in _adamw(tp(w_in), grad_w_in_t[None], tp(m_w_in), tp(v_w_in), "adamw_w_in", tr=256))
    grad_mats["w_in"] = tp(grad_w_in_t[None])
    names_small = [n for n, _ in small] + ["conv_w"]
    flat = lambda d: jnp.concatenate([d[n].reshape(1, -1) for n in names_small], axis=1)
    ns = sum(weights[n].size for n in names_small)
    nsp = -(-ns // 128) * 128
    padv = lambda t: jnp.pad(t, ((0, 0), (0, nsp - ns)))
    small_upd = _adamw(padv(flat(weights))[None], padv(flat(grad_small))[None], padv(flat(mom1))[None],
                       jnp.pad(flat(mom2), ((0, 0), (0, nsp - ns)), constant_values=1.0)[None], "adamw_small", 1)
    off = 0
    for n in names_small:
        sz = weights[n].size
        upd[n] = tuple(t[0, :, off:off + sz] for t in small_upd)
        off += sz

    order = ["norm1_w", "w_in", "conv_w", "conv_b", "dt_bias", "a_log", "d_skip", "ssd_norm_w", "w_br_ret", "w_br_ssd", "w_out", "norm_f_w"]
    grads_out = {**grad_mats, **grad_small}
    shp = lambda n, t: t.reshape(weights[n].shape)
    return (loss_out, grad_x[None], *[shp(n, grads_out[n]) for n in order], *[shp(n, upd[n][0]) for n in order],
            *[shp(n, upd[n][1]) for n in order], *[shp(n, upd[n][2]) for n in order])
```
